```python
import jax, jax.numpy as jnp
from jax import lax
import numpy as np

D_MODEL = 1024
BATCH = 8
SEQ = 8192
DEPTH = 2

N_MIXERS = 2
EXPAND = 2
D_INNER = EXPAND * D_MODEL
CONV_WIDTH = 3
CHUNK = 128
GMLP_GROUPS = 8
GROUP_WIDTH = D_INNER // GMLP_GROUPS
N_CONV_LAYERS = (DEPTH + 1) // 2
N_GMLP_LAYERS = DEPTH // 2
RMS_EPS = 1e-6
LN_EPS = 1e-5

kernel_name = "hybrid_shortconv_chunked_gmlp_adaln"


def rms_norm(x, g):
    xf = x.astype(jnp.float32)
    y = xf * lax.rsqrt(jnp.mean(xf * xf, axis=-1, keepdims=True) + RMS_EPS)
    return (y * g.astype(jnp.float32)).astype(x.dtype)


def layer_norm(x, g, b):
    xf = x.astype(jnp.float32)
    mu = jnp.mean(xf, axis=-1, keepdims=True)
    var = jnp.mean(jnp.square(xf - mu), axis=-1, keepdims=True)
    y = (xf - mu) * lax.rsqrt(var + LN_EPS)
    return (y * g.astype(jnp.float32) + b.astype(jnp.float32)).astype(x.dtype)


def short_conv_mixer(h, w_in, conv_w, conv_b, w_out):
    seq = h.shape[1]
    proj = h @ w_in
    b_gate, c_gate, xin, z = jnp.split(proj, 4, axis=-1)
    cx = c_gate * xin
    padded = jnp.pad(cx, ((0, 0), (CONV_WIDTH - 1, 0), (0, 0)))
    conv = conv_b + conv_w[CONV_WIDTH - 1] * cx
    for k in range(CONV_WIDTH - 1):
        conv = conv + conv_w[k] * padded[:, k:k + seq]
    y = jax.nn.silu(z) * b_gate * conv
    return y @ w_out


def chunked_gmlp_mixer(h, w_in, ln_g, ln_b, w_s, b_s, w_out):
    bsz, seq, _ = h.shape
    proj = h @ w_in
    uv, z = proj[..., :2 * D_INNER], proj[..., 2 * D_INNER:]
    u, v = jnp.split(jax.nn.gelu(uv, approximate=False), 2, axis=-1)
    v = layer_norm(v, ln_g, ln_b)
    n_chunks = seq // CHUNK
    v = v.reshape(bsz, n_chunks, CHUNK, GMLP_GROUPS, GROUP_WIDTH)
    causal = jnp.tril(jnp.ones((CHUNK, CHUNK), dtype=bool))
    w = jnp.where(causal[None], w_s, jnp.zeros_like(w_s)).astype(v.dtype)
    mixed = jnp.einsum('gts,bnsgc->bntgc', w, v)
    mixed = mixed + jnp.transpose(b_s)[None, None, :, :, None].astype(v.dtype)
    s = u * mixed.reshape(bsz, seq, D_INNER)
    y = jax.nn.silu(z) * s
    return y @ w_out


def _fwd_setup_inputs(seed: int = 0) -> dict:
    key = jax.random.key(seed)
    ks = jax.random.split(key, 20)
    nrm = jax.random.normal
    d, e = D_MODEL, D_INNER
    return {
        "x": nrm(ks[0], (BATCH, SEQ, d), jnp.float32),
        "c": nrm(ks[1], (BATCH, d), jnp.float32),
        "mod_w": nrm(ks[2], (DEPTH, d, 3 * d), jnp.float32) * (0.5 * d ** -0.5),
        "mod_b": nrm(ks[3], (DEPTH, 3 * d), jnp.float32) * 0.02,
        "norm_g": 1.0 + 0.05 * nrm(ks[4], (DEPTH, d), jnp.float32),
        "a_w_in": nrm(ks[5], (N_CONV_LAYERS, d, 4 * e), jnp.float32) * d ** -0.5,
        "a_conv_w": nrm(ks[6], (N_CONV_LAYERS, CONV_WIDTH, e), jnp.float32) * CONV_WIDTH ** -0.5,
        "a_conv_b": nrm(ks[7], (N_CONV_LAYERS, e), jnp.float32) * 0.02,
        "a_w_out": nrm(ks[8], (N_CONV_LAYERS, e, d), jnp.float32) * e ** -0.5,
        "b_w_in": nrm(ks[9], (N_GMLP_LAYERS, d, 3 * e), jnp.float32) * d ** -0.5,
        "b_ln_g": 1.0 + 0.05 * nrm(ks[10], (N_GMLP_LAYERS, e), jnp.float32),
        "b_ln_b": 0.02 * nrm(ks[11], (N_GMLP_LAYERS, e), jnp.float32),
        "b_w_s": nrm(ks[12], (N_GMLP_LAYERS, GMLP_GROUPS, CHUNK, CHUNK), jnp.float32) * CHUNK ** -0.5,
        "b_b_s": 1.0 + 0.1 * nrm(ks[13], (N_GMLP_LAYERS, GMLP_GROUPS, CHUNK), jnp.float32),
        "b_w_out": nrm(ks[14], (N_GMLP_LAYERS, e, d), jnp.float32) * e ** -0.5,
        "final_g": 1.0 + 0.05 * nrm(ks[15], (d,), jnp.float32),
    }


def _fwd_reference(x, c, mod_w, mod_b, norm_g, a_w_in, a_conv_w, a_conv_b, a_w_out,
              b_w_in, b_ln_g, b_ln_b, b_w_s, b_b_s, b_w_out, final_g):
    c_act = jax.nn.silu(c)
    for i in range(DEPTH):
        mod = c_act @ mod_w[i] + mod_b[i]
        shift, scale, gate = jnp.split(mod[:, None, :], 3, axis=-1)
        h = rms_norm(x, norm_g[i]) * (1.0 + scale) + shift
        j = i // N_MIXERS
        if i % N_MIXERS == 0:
            branch = short_conv_mixer(h, a_w_in[j], a_conv_w[j], a_conv_b[j], a_w_out[j])
        else:
            branch = chunked_gmlp_mixer(h, b_w_in[j], b_ln_g[j], b_ln_b[j],
                                        b_w_s[j], b_b_s[j], b_w_out[j])
        x = x + gate * branch
    return rms_norm(x, final_g)


import jax as _jax
import jax.numpy as _jnp

TWIN_FORMAT = 'train_step'
FWD_PARAMS = ['x', 'c', 'mod_w', 'mod_b', 'norm_g', 'a_w_in', 'a_conv_w', 'a_conv_b', 'a_w_out', 'b_w_in', 'b_ln_g', 'b_ln_b', 'b_w_s', 'b_b_s', 'b_w_out', 'final_g']
TWIN_WEIGHTS = ['mod_w', 'mod_b', 'norm_g', 'a_w_in', 'a_conv_w', 'a_conv_b', 'a_w_out', 'b_w_in', 'b_ln_g', 'b_ln_b', 'b_w_s', 'b_b_s', 'b_w_out', 'final_g']
TWIN_DIFF_INPUT = 'x'
TWIN_INPUTS = ['x', 'c', 'mod_w', 'mod_b', 'norm_g', 'a_w_in', 'a_conv_w', 'a_conv_b', 'a_w_out', 'b_w_in', 'b_ln_g', 'b_ln_b', 'b_w_s', 'b_b_s', 'b_w_out', 'final_g', 'loss_target', 'm_mod_w', 'm_mod_b', 'm_norm_g', 'm_a_w_in', 'm_a_conv_w', 'm_a_conv_b', 'm_a_w_out', 'm_b_w_in', 'm_b_ln_g', 'm_b_ln_b', 'm_b_w_s', 'm_b_b_s', 'm_b_w_out', 'm_final_g', 'v_mod_w', 'v_mod_b', 'v_norm_g', 'v_a_w_in', 'v_a_conv_w', 'v_a_conv_b', 'v_a_w_out', 'v_b_w_in', 'v_b_ln_g', 'v_b_ln_b', 'v_b_w_s', 'v_b_b_s', 'v_b_w_out', 'v_final_g']
TWIN_OUTPUTS = ['loss', 'grad_x', 'grad_mod_w', 'grad_mod_b', 'grad_norm_g', 'grad_a_w_in', 'grad_a_conv_w', 'grad_a_conv_b', 'grad_a_w_out', 'grad_b_w_in', 'grad_b_ln_g', 'grad_b_ln_b', 'grad_b_w_s', 'grad_b_b_s', 'grad_b_w_out', 'grad_final_g', 'delta_mod_w', 'delta_mod_b', 'delta_norm_g', 'delta_a_w_in', 'delta_a_conv_w', 'delta_a_conv_b', 'delta_a_w_out', 'delta_b_w_in', 'delta_b_ln_g', 'delta_b_ln_b', 'delta_b_w_s', 'delta_b_b_s', 'delta_b_w_out', 'delta_final_g', 'new_m_mod_w', 'new_m_mod_b', 'new_m_norm_g', 'new_m_a_w_in', 'new_m_a_conv_w', 'new_m_a_conv_b', 'new_m_a_w_out', 'new_m_b_w_in', 'new_m_b_ln_g', 'new_m_b_ln_b', 'new_m_b_w_s', 'new_m_b_b_s', 'new_m_b_w_out', 'new_m_final_g', 'new_v_mod_w', 'new_v_mod_b', 'new_v_norm_g', 'new_v_a_w_in', 'new_v_a_conv_w', 'new_v_a_conv_b', 'new_v_a_w_out', 'new_v_b_w_in', 'new_v_b_ln_g', 'new_v_b_ln_b', 'new_v_b_w_s', 'new_v_b_b_s', 'new_v_b_w_out', 'new_v_final_g']
TWIN_LEAF_KINDS = {'loss': 'loss', 'grad_x': 'grad_x', 'grad_mod_w': 'grad_w', 'grad_mod_b': 'grad_w', 'grad_norm_g': 'grad_w', 'grad_a_w_in': 'grad_w', 'grad_a_conv_w': 'grad_w', 'grad_a_conv_b': 'grad_w', 'grad_a_w_out': 'grad_w', 'grad_b_w_in': 'grad_w', 'grad_b_ln_g': 'grad_w', 'grad_b_ln_b': 'grad_w', 'grad_b_w_s': 'grad_w', 'grad_b_b_s': 'grad_w', 'grad_b_w_out': 'grad_w', 'grad_final_g': 'grad_w', 'delta_mod_w': 'delta_w', 'delta_mod_b': 'delta_w', 'delta_norm_g': 'delta_w', 'delta_a_w_in': 'delta_w', 'delta_a_conv_w': 'delta_w', 'delta_a_conv_b': 'delta_w', 'delta_a_w_out': 'delta_w', 'delta_b_w_in': 'delta_w', 'delta_b_ln_g': 'delta_w', 'delta_b_ln_b': 'delta_w', 'delta_b_w_s': 'delta_w', 'delta_b_b_s': 'delta_w', 'delta_b_w_out': 'delta_w', 'delta_final_g': 'delta_w', 'new_m_mod_w': 'new_m', 'new_m_mod_b': 'new_m', 'new_m_norm_g': 'new_m', 'new_m_a_w_in': 'new_m', 'new_m_a_conv_w': 'new_m', 'new_m_a_conv_b': 'new_m', 'new_m_a_w_out': 'new_m', 'new_m_b_w_in': 'new_m', 'new_m_b_ln_g': 'new_m', 'new_m_b_ln_b': 'new_m', 'new_m_b_w_s': 'new_m', 'new_m_b_b_s': 'new_m', 'new_m_b_w_out': 'new_m', 'new_m_final_g': 'new_m', 'new_v_mod_w': 'new_v', 'new_v_mod_b': 'new_v', 'new_v_norm_g': 'new_v', 'new_v_a_w_in': 'new_v', 'new_v_a_conv_w': 'new_v', 'new_v_a_conv_b': 'new_v', 'new_v_a_w_out': 'new_v', 'new_v_b_w_in': 'new_v', 'new_v_b_ln_g': 'new_v', 'new_v_b_ln_b': 'new_v', 'new_v_b_w_s': 'new_v', 'new_v_b_b_s': 'new_v', 'new_v_b_w_out': 'new_v', 'new_v_final_g': 'new_v'}


def _forward(args):
    return _fwd_reference(*[args[k] for k in FWD_PARAMS])


def _output_shape():
    def fwd():
        inp = _fwd_setup_inputs(0)
        return _fwd_reference(*[inp[k] for k in FWD_PARAMS])
    out = _jax.eval_shape(fwd)
    return out.shape, out.dtype

N_MICROBATCH = 1
ADAM_LR = 0.001
ADAM_B1 = 0.9
ADAM_B2 = 0.999
ADAM_EPS = 1e-08
ADAM_WD = 0.01
ADAM_STEP = 10
PER_EXAMPLE_BATCH_AXIS = {'x': 0, 'c': 0, 'loss_target': 0}
SHARED_INPUTS = []
_WEIGHT_DTYPES = {'mod_w': _jnp.float32, 'mod_b': _jnp.float32, 'norm_g': _jnp.float32, 'a_w_in': _jnp.float32, 'a_conv_w': _jnp.float32, 'a_conv_b': _jnp.float32, 'a_w_out': _jnp.float32, 'b_w_in': _jnp.float32, 'b_ln_g': _jnp.float32, 'b_ln_b': _jnp.float32, 'b_w_s': _jnp.float32, 'b_b_s': _jnp.float32, 'b_w_out': _jnp.float32, 'final_g': _jnp.float32}
MOMENT_SCALE = {'mod_w': 1.074890e-01, 'mod_b': 2.028609e-01, 'norm_g': 1.004076e-01, 'a_w_in': 4.624137e-02, 'a_conv_w': 4.581676e-02, 'a_conv_b': 3.893790e-02, 'a_w_out': 6.540208e-02, 'b_w_in': 2.786065e-02, 'b_ln_g': 1.736596e-02, 'b_ln_b': 1.793193e-02, 'b_w_s': 2.455376e-02, 'b_b_s': 3.473014e-02, 'b_w_out': 4.247517e-02, 'final_g': 6.402670e+01}


def _to_microbatches(a, axis):
    t = _jnp.moveaxis(a, axis, 0)
    t = t.reshape((N_MICROBATCH, t.shape[0] // N_MICROBATCH) + t.shape[1:])
    return _jnp.moveaxis(t, 1, axis + 1)


def setup_inputs(seed: int = 0) -> dict:
    inp = _fwd_setup_inputs(seed)
    key = _jax.random.fold_in(_jax.random.key(seed), 7919)
    shape, _ = _output_shape()
    out = dict(inp)
    out["loss_target"] = _jax.random.normal(_jax.random.fold_in(key, 0), shape, _jnp.float32)
    for i, name in enumerate(TWIN_WEIGHTS):
        w = inp[name].astype(_jnp.float32)
        if MOMENT_SCALE is None:
            s = _jnp.sqrt(_jnp.mean(_jnp.square(w)) + 1e-30)
        else:
            s = MOMENT_SCALE[name]
        km, kv = _jax.random.split(_jax.random.fold_in(key, i + 1))
        out[name] = w
        out["m_" + name] = s * _jax.random.normal(km, w.shape, _jnp.float32)
        out["v_" + name] = (s * s) * _jax.random.uniform(kv, w.shape, _jnp.float32, 0.5, 1.5)
    if N_MICROBATCH > 1:
        for name, axis in PER_EXAMPLE_BATCH_AXIS.items():
            out[name] = _to_microbatches(out[name], axis)
    return {'x': out['x'], 'c': out['c'], 'mod_w': out['mod_w'], 'mod_b': out['mod_b'], 'norm_g': out['norm_g'], 'a_w_in': out['a_w_in'], 'a_conv_w': out['a_conv_w'], 'a_conv_b': out['a_conv_b'], 'a_w_out': out['a_w_out'], 'b_w_in': out['b_w_in'], 'b_ln_g': out['b_ln_g'], 'b_ln_b': out['b_ln_b'], 'b_w_s': out['b_w_s'], 'b_b_s': out['b_b_s'], 'b_w_out': out['b_w_out'], 'final_g': out['final_g'], 'loss_target': out['loss_target'], 'm_mod_w': out['m_mod_w'], 'm_mod_b': out['m_mod_b'], 'm_norm_g': out['m_norm_g'], 'm_a_w_in': out['m_a_w_in'], 'm_a_conv_w': out['m_a_conv_w'], 'm_a_conv_b': out['m_a_conv_b'], 'm_a_w_out': out['m_a_w_out'], 'm_b_w_in': out['m_b_w_in'], 'm_b_ln_g': out['m_b_ln_g'], 'm_b_ln_b': out['m_b_ln_b'], 'm_b_w_s': out['m_b_w_s'], 'm_b_b_s': out['m_b_b_s'], 'm_b_w_out': out['m_b_w_out'], 'm_final_g': out['m_final_g'], 'v_mod_w': out['v_mod_w'], 'v_mod_b': out['v_mod_b'], 'v_norm_g': out['v_norm_g'], 'v_a_w_in': out['v_a_w_in'], 'v_a_conv_w': out['v_a_conv_w'], 'v_a_conv_b': out['v_a_conv_b'], 'v_a_w_out': out['v_a_w_out'], 'v_b_w_in': out['v_b_w_in'], 'v_b_ln_g': out['v_b_ln_g'], 'v_b_ln_b': out['v_b_ln_b'], 'v_b_w_s': out['v_b_w_s'], 'v_b_b_s': out['v_b_b_s'], 'v_b_w_out': out['v_b_w_out'], 'v_final_g': out['v_final_g']}


def _loss(weights, diff, rest, loss_target):
    with _jax.named_scope("forward"):
        args = {**rest, TWIN_DIFF_INPUT: diff, **{k: w.astype(_WEIGHT_DTYPES[k]) for k, w in weights.items()}}
        y = _forward(args)
    with _jax.named_scope("loss_head"):
        err = _jnp.square(y.astype(_jnp.float32) - loss_target)
        return 0.5 * _jnp.sum(_jnp.mean(err, axis=-1)) if err.ndim else 0.5 * err


def _adamw(w, g, m, v):
    m = ADAM_B1 * m + (1.0 - ADAM_B1) * g
    v = ADAM_B2 * v + (1.0 - ADAM_B2) * _jnp.square(g)
    m_hat = m / (1.0 - ADAM_B1 ** ADAM_STEP)
    v_hat = v / (1.0 - ADAM_B2 ** ADAM_STEP)
    delta = -ADAM_LR * (m_hat / (_jnp.sqrt(v_hat) + ADAM_EPS) + ADAM_WD * w)
    return delta, m, v


def reference(x, c, mod_w, mod_b, norm_g, a_w_in, a_conv_w, a_conv_b, a_w_out, b_w_in, b_ln_g, b_ln_b, b_w_s, b_b_s, b_w_out, final_g, loss_target, m_mod_w, m_mod_b, m_norm_g, m_a_w_in, m_a_conv_w, m_a_conv_b, m_a_w_out, m_b_w_in, m_b_ln_g, m_b_ln_b, m_b_w_s, m_b_b_s, m_b_w_out, m_final_g, v_mod_w, v_mod_b, v_norm_g, v_a_w_in, v_a_conv_w, v_a_conv_b, v_a_w_out, v_b_w_in, v_b_ln_g, v_b_ln_b, v_b_w_s, v_b_b_s, v_b_w_out, v_final_g):
    given = dict(x=x, c=c, mod_w=mod_w, mod_b=mod_b, norm_g=norm_g, a_w_in=a_w_in, a_conv_w=a_conv_w, a_conv_b=a_conv_b, a_w_out=a_w_out, b_w_in=b_w_in, b_ln_g=b_ln_g, b_ln_b=b_ln_b, b_w_s=b_w_s, b_b_s=b_b_s, b_w_out=b_w_out, final_g=final_g, loss_target=loss_target, m_mod_w=m_mod_w, m_mod_b=m_mod_b, m_norm_g=m_norm_g, m_a_w_in=m_a_w_in, m_a_conv_w=m_a_conv_w, m_a_conv_b=m_a_conv_b, m_a_w_out=m_a_w_out, m_b_w_in=m_b_w_in, m_b_ln_g=m_b_ln_g, m_b_ln_b=m_b_ln_b, m_b_w_s=m_b_w_s, m_b_b_s=m_b_b_s, m_b_w_out=m_b_w_out, m_final_g=m_final_g, v_mod_w=v_mod_w, v_mod_b=v_mod_b, v_norm_g=v_norm_g, v_a_w_in=v_a_w_in, v_a_conv_w=v_a_conv_w, v_a_conv_b=v_a_conv_b, v_a_w_out=v_a_w_out, v_b_w_in=v_b_w_in, v_b_ln_g=v_b_ln_g, v_b_ln_b=v_b_ln_b, v_b_w_s=v_b_w_s, v_b_b_s=v_b_b_s, v_b_w_out=v_b_w_out, v_final_g=v_final_g)
    weights = {n: given[n] for n in TWIN_WEIGHTS}
    shared = {n: given[n] for n in SHARED_INPUTS}
    per_example = {n: given[n] for n in ['x', 'c']}
    grad_fn = _jax.value_and_grad(_loss, argnums=(0, 1))

    def one_microbatch(ex, loss_target):
        ex = dict(ex)
        diff = ex.pop(TWIN_DIFF_INPUT)
        return grad_fn(weights, diff, {**shared, **ex}, loss_target)

    if N_MICROBATCH == 1:
        loss, (grad_w, grad_x) = one_microbatch(per_example, given["loss_target"])
    else:
        def body(carry, xs):
            loss_sum, grad_sum = carry
            l_k, (gw_k, gx_k) = one_microbatch(xs[0], xs[1])
            with _jax.named_scope("update"):
                return (loss_sum + l_k, _jax.tree.map(_jnp.add, grad_sum, gw_k)), gx_k

        init = (_jnp.zeros((), _jnp.float32), _jax.tree.map(_jnp.zeros_like, weights))
        (loss, grad_w), grad_x = _jax.lax.scan(body, init, (per_example, given["loss_target"]))
    with _jax.named_scope("update"):
        delta_w, new_m, new_v = {}, {}, {}
        for n in TWIN_WEIGHTS:
            delta_w[n], new_m[n], new_v[n] = _adamw(weights[n], grad_w[n], given["m_" + n], given["v_" + n])
    return (loss, grad_x, *[grad_w[n] for n in TWIN_WEIGHTS], *[delta_w[n] for n in TWIN_WEIGHTS],
            *[new_m[n] for n in TWIN_WEIGHTS], *[new_v[n] for n in TWIN_WEIGHTS])
```

```python
import functools

import jax
import jax.numpy as jnp
from jax import lax
from jax.experimental import pallas as pl
from jax.experimental.pallas import tpu as pltpu

NDEV = 8
CHUNK = 128
GROUPS = 8
RMS_EPS = 1e-6
LN_EPS = 1e-5
ADAM_LR, ADAM_B1, ADAM_B2, ADAM_EPS, ADAM_WD, ADAM_STEP = 0.001, 0.9, 0.999, 1e-08, 0.01, 10
V7X_VMEM_BYTES = 64 * 1024 * 1024
VMEM_LIMIT = V7X_VMEM_BYTES - 8 * 1024 * 1024
PACK_W = 1024
F32, BF16 = jnp.float32, jnp.bfloat16
MESH = pl.DeviceIdType.MESH
RSQRT2 = 0.7071067811865476
INV_SQRT_2PI = 0.3989422804014327
NT_DIMS = (((1,), (1,)), ((), ()))
TN_DIMS = (((0,), (0,)), ((), ()))


def _params(sem=None):
    return pltpu.CompilerParams(dimension_semantics=sem, vmem_limit_bytes=VMEM_LIMIT)


def _vmem():
    return pl.BlockSpec(memory_space=pltpu.VMEM)


def _hbm():
    return pl.BlockSpec(memory_space=pltpu.HBM)


def _full(shape):
    return pl.BlockSpec(shape, lambda *_: (0,) * len(shape))


def _pos():
    return lax.axis_index("x"), lax.axis_index("y"), lax.axis_index("c")


def _index(p):
    return 4 * p[0] + 2 * p[1] + p[2]


def _peer(k):
    x, y, c = _pos()
    return ((1 - x) if (k >> 2) & 1 else x, (1 - y) if (k >> 1) & 1 else y, (1 - c) if k & 1 else c)


def _silu(z):
    sg = jax.nn.sigmoid(z)
    return z * sg, sg * (1.0 + z * (1.0 - sg))


def _gelu(v):
    phi = 0.5 * (1.0 + lax.erf(v * RSQRT2))
    return v * phi, phi + v * (jnp.exp(-0.5 * v * v) * INV_SQRT_2PI)


def _colsum(v):
    return jnp.sum(v, axis=0, keepdims=True)


def _rowsum(v):
    return jnp.sum(v, axis=-1, keepdims=True)


def _gather_all_vmem(slab_ref, send_sems, recv_sems, base):
    me = _index(_pos())
    sends = []
    for k in range(1, NDEV):
        cp = pltpu.make_async_remote_copy(
            src_ref=slab_ref.at[me], dst_ref=slab_ref.at[me],
            send_sem=send_sems.at[base + k - 1], recv_sem=recv_sems.at[base + k - 1],
            device_id=_peer(k), device_id_type=MESH)
        cp.start()
        sends.append(cp)
    for k in range(1, NDEV):
        src = _index(_peer(k))
        pltpu.make_async_remote_copy(
            src_ref=slab_ref.at[src], dst_ref=slab_ref.at[src],
            send_sem=send_sems.at[base + k - 1], recv_sem=recv_sems.at[base + k - 1],
            device_id=_peer(k), device_id_type=MESH).wait_recv()
    for cp in sends:
        cp.wait_send()


def _mod_vectors(c, mod_w, mod_b):
    n_layers, d, w3 = mod_w.shape

    def body(c_ref, mw_ref, mb_ref, mod_ref, call_ref, cslab, pslab, send_sems, recv_sems):
        me = _index(_pos())
        cv = c_ref[...]
        cslab[me] = jnp.broadcast_to(cv * jax.nn.sigmoid(cv), (8, d))
        _gather_all_vmem(cslab, send_sems, recv_sems, 0)
        c_all = jnp.concatenate([cslab[k, 0:1, :] for k in range(NDEV)], axis=0)
        call_ref[...] = c_all
        for i in range(n_layers):
            pslab[me, i * NDEV:(i + 1) * NDEV, :] = jnp.dot(
                c_all, mw_ref[i], preferred_element_type=F32, precision=lax.Precision.HIGHEST)
        _gather_all_vmem(pslab, send_sems, recv_sems, NDEV - 1)
        for i in range(n_layers):
            for k in range(NDEV):
                mod_ref[i:i + 1, k * w3:(k + 1) * w3] = (
                    pslab[k, pl.ds(i * NDEV + me, 1), :] + mb_ref[i:i + 1, k * w3:(k + 1) * w3])

    return pl.pallas_call(
        body, name="mod_vectors",
        out_shape=(jax.ShapeDtypeStruct((n_layers, 3 * d), F32), jax.ShapeDtypeStruct((NDEV, d), F32)),
        in_specs=[_vmem(), _vmem(), _vmem()], out_specs=(_vmem(), _vmem()),
        scratch_shapes=[pltpu.VMEM((NDEV, 8, d), F32), pltpu.VMEM((NDEV, n_layers * NDEV, w3), F32),
                        pltpu.SemaphoreType.DMA((2 * (NDEV - 1),)), pltpu.SemaphoreType.DMA((2 * (NDEV - 1),))],
        compiler_params=pltpu.CompilerParams(vmem_limit_bytes=VMEM_LIMIT),
    )(c, mod_w, mod_b)


def _all_gather(shards, name):
    n = len(shards)

    def body(*refs):
        ins, outs = refs[:n], refs[n:2 * n]
        send_sems, recv_sems, local_sems = refs[2 * n:]
        x, y, c = _pos()
        me, sibling = (x, y, c), (x, y, 1 - c)
        chips = [(1 - x, y), (x, 1 - y), (1 - x, 1 - y)]

        def copy(a, k, block, to, src=None):
            dst = outs[a].at[_index(block)]
            return pltpu.make_async_remote_copy(
                src_ref=dst if src is None else src, dst_ref=dst,
                send_sem=send_sems.at[a * 7 + k], recv_sem=recv_sems.at[a * 7 + k],
                device_id=to, device_id_type=MESH)

        mine = [pltpu.make_async_copy(ins[a], outs[a].at[_index(me)], local_sems.at[a]) for a in range(n)]
        for cp in mine:
            cp.start()
        first = []
        for a in range(n):
            first.append(copy(a, 0, me, sibling, src=ins[a]))
            first += [copy(a, 1 + j, me, (*chip, c), src=ins[a]) for j, chip in enumerate(chips)]
        for cp in first:
            cp.start()
        passed = []
        for j, chip in enumerate(chips):
            for a in range(n):
                copy(a, 1 + j, (*chip, c), me).wait_recv()
                cp = copy(a, 4 + j, (*chip, c), sibling)
                cp.start()
                passed.append(cp)
        for a in range(n):
            copy(a, 0, sibling, me).wait_recv()
        for j, chip in enumerate(chips):
            for a in range(n):
                copy(a, 4 + j, (*chip, 1 - c), me).wait_recv()
        for cp in first + passed:
            cp.wait_send()
        for cp in mine:
            cp.wait()

    return pl.pallas_call(
        body, name=name,
        out_shape=tuple(jax.ShapeDtypeStruct((NDEV,) + s.shape, s.dtype) for s in shards),
        in_specs=[_hbm()] * n, out_specs=tuple([_hbm()] * n),
        scratch_shapes=[pltpu.SemaphoreType.DMA((7 * n,)), pltpu.SemaphoreType.DMA((7 * n,)),
                        pltpu.SemaphoreType.DMA((n,))],
    )(*shards)


def _scatter_to_owners(parts, name):
    n = len(parts)

    def body(*refs):
        ins, outs = refs[:n], refs[n:2 * n]
        send_sems, recv_sems, local_sems = refs[2 * n:]
        me = _index(_pos())
        mine = [pltpu.make_async_copy(ins[a].at[me], outs[a].at[me], local_sems.at[a]) for a in range(n)]
        for cp in mine:
            cp.start()
        sends = []
        for k in range(1, NDEV):
            peer = _peer(k)
            for a in range(n):
                cp = pltpu.make_async_remote_copy(
                    src_ref=ins[a].at[_index(peer)], dst_ref=outs[a].at[me],
                    send_sem=send_sems.at[a * 7 + k - 1], recv_sem=recv_sems.at[a * 7 + k - 1],
                    device_id=peer, device_id_type=MESH)
                cp.start()
                sends.append(cp)
        for k in range(1, NDEV):
            peer = _peer(k)
            for a in range(n):
                slot = outs[a].at[_index(peer)]
                pltpu.make_async_remote_copy(
                    src_ref=slot, dst_ref=slot,
                    send_sem=send_sems.at[a * 7 + k - 1], recv_sem=recv_sems.at[a * 7 + k - 1],
                    device_id=peer, device_id_type=MESH).wait_recv()
        for cp in sends:
            cp.wait_send()
        for cp in mine:
            cp.wait()

    return pl.pallas_call(
        body, name=name,
        out_shape=tuple(jax.ShapeDtypeStruct(p.shape, p.dtype) for p in parts),
        in_specs=[_hbm()] * n, out_specs=tuple([_hbm()] * n),
        scratch_shapes=[pltpu.SemaphoreType.DMA((7 * n,)), pltpu.SemaphoreType.DMA((7 * n,)),
                        pltpu.SemaphoreType.DMA((n,))],
    )(*parts)


def _norm_mod_matmul(x, g, scale, shift, wg, name):
    s, d = x.shape
    nb = wg.shape[2]
    tm = min(s, 1024)

    def body(x_ref, g_ref, sc_ref, sh_ref, w_ref, proj_ref, h_ref):
        @pl.when(pl.program_id(1) == 0)
        def _():
            xv = x_ref[...]
            r = lax.rsqrt(jnp.mean(xv * xv, axis=-1, keepdims=True) + RMS_EPS)
            h_ref[...] = ((xv * r) * g_ref[...] * (1.0 + sc_ref[...]) + sh_ref[...]).astype(BF16)
        proj_ref[...] = jnp.dot(h_ref[...], w_ref[...], preferred_element_type=F32).astype(BF16)

    vec = pl.BlockSpec((1, d), lambda i, j: (0, 0))
    return pl.pallas_call(
        body, name=name, grid=(s // tm, NDEV),
        out_shape=(jax.ShapeDtypeStruct((s, NDEV * nb), BF16), jax.ShapeDtypeStruct((s, d), BF16)),
        in_specs=[pl.BlockSpec((tm, d), lambda i, j: (i, 0)), vec, vec, vec,
                  pl.BlockSpec((None, d, nb), lambda i, j: (j, 0, 0))],
        out_specs=(pl.BlockSpec((tm, nb), lambda i, j: (i, j)), pl.BlockSpec((tm, d), lambda i, j: (i, 0))),
        compiler_params=_params(("parallel", "arbitrary")),
    )(x, g, scale, shift, wg)


def _conv_taps(cx, t6, t7, row):
    p1 = jnp.where(row == 0, t7, pltpu.roll(cx, 1, 0))
    p2 = jnp.where(row == 0, t6, jnp.where(row == 1, t7, pltpu.roll(cx, 2, 0)))
    return p1, p2


def _conv_mixer_fwd(proj, x, cw, cb, gate, wo, name):
    s, e4 = proj.shape
    e = e4 // 4
    d = x.shape[1]
    t = min(s, 256)
    cwid = min(e, 512)

    def body(p_ref, x_ref, cw_ref, cb_ref, gate_ref, wo_ref, x1_ref, br_ref, tails_ref, y_scr, tail_scr):
        @pl.when(pl.program_id(0) == 0)
        def _():
            tail_scr[...] = jnp.zeros_like(tail_scr)
        row = lax.broadcasted_iota(jnp.int32, (t, cwid), 0)
        for c0 in range(0, e, cwid):
            sl = slice(c0, c0 + cwid)
            bg = p_ref[:, c0:c0 + cwid].astype(F32)
            cx = p_ref[:, e + c0:e + c0 + cwid].astype(F32) * p_ref[:, 2 * e + c0:2 * e + c0 + cwid].astype(F32)
            z = p_ref[:, 3 * e + c0:3 * e + c0 + cwid].astype(F32)
            p1, p2 = _conv_taps(cx, tail_scr[6:7, sl], tail_scr[7:8, sl], row)
            conv = cb_ref[:, sl] + cw_ref[2:3, sl] * cx + cw_ref[0:1, sl] * p2 + cw_ref[1:2, sl] * p1
            y_scr[:, sl] = (_silu(z)[0] * bg * conv).astype(BF16)
            tail_scr[:, sl] = cx[t - 8:t, :]
        tails_ref[...] = tail_scr[...]
        br = jnp.dot(y_scr[...], wo_ref[...], preferred_element_type=F32)
        x1_ref[...] = x_ref[...] + gate_ref[...] * br
        br_ref[...] = br.astype(BF16)

    return pl.pallas_call(
        body, name=name, grid=(s // t,),
        out_shape=(jax.ShapeDtypeStruct((s, d), F32), jax.ShapeDtypeStruct((s, d), BF16),
                   jax.ShapeDtypeStruct((s // t, 8, e), F32)),
        in_specs=[pl.BlockSpec((t, e4), lambda i: (i, 0)), pl.BlockSpec((t, d), lambda i: (i, 0)),
                  _full((3, e)), _full((1, e)), _full((1, d)), _full((e, d))],
        out_specs=(pl.BlockSpec((t, d), lambda i: (i, 0)), pl.BlockSpec((t, d), lambda i: (i, 0)),
                   pl.BlockSpec((None, 8, e), lambda i: (i, 0, 0))),
        scratch_shapes=[pltpu.VMEM((t, e), BF16), pltpu.VMEM((8, e), F32)],
        compiler_params=_params(("arbitrary",)),
    )(proj, x, cw, cb, gate, wo)


def _ln_stats(p_ref, v_scr, dgv_scr, t, e):
    gw = e // GROUPS
    s1 = jnp.zeros((t, 1), F32)
    for g in range(GROUPS):
        v, dgv = _gelu(p_ref[:, e + g * gw:e + (g + 1) * gw].astype(F32))
        v_scr[:, g * gw:(g + 1) * gw] = v
        if dgv_scr is not None:
            dgv_scr[:, g * gw:(g + 1) * gw] = dgv
        s1 = s1 + _rowsum(v)
    mu = s1 * (1.0 / e)
    s2 = jnp.zeros((t, 1), F32)
    for g in range(GROUPS):
        dv = v_scr[:, g * gw:(g + 1) * gw] - mu
        s2 = s2 + _rowsum(dv * dv)
    return mu, lax.rsqrt(s2 * (1.0 / e) + LN_EPS)


def _gmlp_fwd_loss(proj, x1, tgt, gate, fg, lng, lnb, wt, bsf, wo, name):
    s, e3 = proj.shape
    e = e3 // 3
    d = x1.shape[1]
    gw = e // GROUPS
    t = min(s, 256)

    def body(p_ref, x1_ref, tgt_ref, gate_ref, fg_ref, lng_ref, lnb_ref, wt_ref, bsf_ref, wo_ref,
             dx2_ref, loss_ref, dfg_ref, dgate_ref, v_scr, y_scr):
        @pl.when(pl.program_id(0) == 0)
        def _():
            loss_ref[...] = jnp.zeros_like(loss_ref)
            dfg_ref[...] = jnp.zeros_like(dfg_ref)
            dgate_ref[...] = jnp.zeros_like(dgate_ref)
        mu, rs = _ln_stats(p_ref, v_scr, None, t, e)
        for g in range(GROUPS):
            gs = slice(g * gw, (g + 1) * gw)
            vn = (((v_scr[:, gs] - mu) * rs) * lng_ref[:, gs] + lnb_ref[:, gs]).astype(BF16)
            for ch in range(t // CHUNK):
                rows = slice(ch * CHUNK, (ch + 1) * CHUNK)
                mixed = jnp.dot(wt_ref[g], vn[rows], preferred_element_type=F32) + bsf_ref[:, gs]
                u = _gelu(p_ref[rows, g * gw:(g + 1) * gw].astype(F32))[0]
                sz = _silu(p_ref[rows, 2 * e + g * gw:2 * e + (g + 1) * gw].astype(F32))[0]
                y_scr[rows, gs] = (sz * (u * mixed)).astype(BF16)
        br = jnp.dot(y_scr[...], wo_ref[...], preferred_element_type=F32)
        x2 = x1_ref[...] + gate_ref[...] * br
        r2 = lax.rsqrt(jnp.mean(x2 * x2, axis=-1, keepdims=True) + RMS_EPS)
        xn = x2 * r2
        diff = xn * fg_ref[...] - tgt_ref[...]
        loss_ref[...] += jnp.broadcast_to(0.5 * _colsum(jnp.mean(diff * diff, axis=-1, keepdims=True)), loss_ref.shape)
        dout = diff * (1.0 / d)
        dfg_ref[...] += _colsum(dout * xn)
        dxn = dout * fg_ref[...]
        dx2 = r2 * (dxn - xn * jnp.mean(dxn * xn, axis=-1, keepdims=True))
        dx2_ref[...] = dx2
        dgate_ref[...] += _colsum(dx2 * br)

    tok = pl.BlockSpec((t, d), lambda i: (i, 0))
    return pl.pallas_call(
        body, name=name, grid=(s // t,),
        out_shape=(jax.ShapeDtypeStruct((s, d), F32), jax.ShapeDtypeStruct((8, 128), F32),
                   jax.ShapeDtypeStruct((1, d), F32), jax.ShapeDtypeStruct((1, d), F32)),
        in_specs=[pl.BlockSpec((t, e3), lambda i: (i, 0)), tok, tok, _full((1, d)), _full((1, d)),
                  _full((1, e)), _full((1, e)), _full((GROUPS, CHUNK, CHUNK)), _full((CHUNK, e)), _full((e, d))],
        out_specs=(tok, _full((8, 128)), _full((1, d)), _full((1, d))),
        scratch_shapes=[pltpu.VMEM((t, e), F32), pltpu.VMEM((t, e), BF16)],
        compiler_params=_params(("arbitrary",)),
    )(proj, x1, tgt, gate, fg, lng, lnb, wt, bsf, wo)


def _gmlp_bwd(proj, dx2, gate, lng, lnb, wt, wtt, bsf, wo, name):
    s, e3 = proj.shape
    e = e3 // 3
    d = dx2.shape[1]
    gw = e // GROUPS
    t = min(s, 256)
    n_t = s // t

    def body(p_ref, dx_ref, gate_ref, lng_ref, lnb_ref, wt_ref, wtt_ref, bsf_ref, wo_ref,
             dp_ref, y_ref, dws_ref, dbs_ref, dlg_ref, dlb_ref, v_scr, dgv_scr, dy_scr, dvn_scr, dbs_scr):
        @pl.when(pl.program_id(0) == 0)
        def _():
            dws_ref[...] = jnp.zeros_like(dws_ref)
            dlg_ref[...] = jnp.zeros_like(dlg_ref)
            dlb_ref[...] = jnp.zeros_like(dlb_ref)
            dbs_scr[...] = jnp.zeros_like(dbs_scr)
        dbr = (dx_ref[...] * gate_ref[...]).astype(BF16)
        dy_scr[...] = lax.dot_general(dbr, wo_ref[...], NT_DIMS, preferred_element_type=F32)
        mu, rs = _ln_stats(p_ref, v_scr, dgv_scr, t, e)
        tril = (lax.broadcasted_iota(jnp.int32, (CHUNK, CHUNK), 0) >= lax.broadcasted_iota(jnp.int32, (CHUNK, CHUNK), 1))
        c1 = jnp.zeros((t, 1), F32)
        c2 = jnp.zeros((t, 1), F32)
        for g in range(GROUPS):
            gs = slice(g * gw, (g + 1) * gw)
            vhat = (v_scr[:, gs] - mu) * rs
            lg = lng_ref[:, gs]
            vn = (vhat * lg + lnb_ref[:, gs]).astype(BF16)
            for ch in range(t // CHUNK):
                rows = slice(ch * CHUNK, (ch + 1) * CHUNK)
                mixed = jnp.dot(wt_ref[g], vn[rows], preferred_element_type=F32) + bsf_ref[:, gs]
                u, dgu = _gelu(p_ref[rows, g * gw:(g + 1) * gw].astype(F32))
                sz, dsz = _silu(p_ref[rows, 2 * e + g * gw:2 * e + (g + 1) * gw].astype(F32))
                sgate = u * mixed
                y_ref[rows, gs] = (sz * sgate).astype(BF16)
                dy = dy_scr[rows, gs]
                dp_ref[rows, 2 * e + g * gw:2 * e + (g + 1) * gw] = (dy * sgate * dsz).astype(BF16)
                ds = dy * sz
                dp_ref[rows, gs] = (ds * mixed * dgu).astype(BF16)
                dm = ds * u
                dbs_scr[:, gs] += dm
                dmb = dm.astype(BF16)
                dws_ref[g] += jnp.where(tril, lax.dot_general(dmb, vn[rows], NT_DIMS, preferred_element_type=F32), 0.0)
                dvn_scr[rows, gs] = jnp.dot(wtt_ref[g], dmb, preferred_element_type=F32)
            dvn = dvn_scr[:, gs]
            dlb_ref[:, gs] += _colsum(dvn)
            dlg_ref[:, gs] += _colsum(dvn * vhat)
            dvh = dvn * lg
            c1 = c1 + _rowsum(dvh)
            c2 = c2 + _rowsum(dvh * vhat)
        c1 = c1 * (1.0 / e)
        c2 = c2 * (1.0 / e)
        for g in range(GROUPS):
            gs = slice(g * gw, (g + 1) * gw)
            vhat = (v_scr[:, gs] - mu) * rs
            dv = rs * (dvn_scr[:, gs] * lng_ref[:, gs] - c1 - vhat * c2)
            dp_ref[:, e + g * gw:e + (g + 1) * gw] = (dv * dgv_scr[:, gs]).astype(BF16)

        @pl.when(pl.program_id(0) == n_t - 1)
        def _():
            lane = lax.broadcasted_iota(jnp.int32, (CHUNK, 128), 1)
            acc = jnp.zeros((CHUNK, 128), F32)
            for g in range(GROUPS):
                acc = acc + jnp.where(lane == g, _rowsum(dbs_scr[:, g * gw:(g + 1) * gw]), 0.0)
            dbs_ref[...] = acc

    tok = pl.BlockSpec((t, d), lambda i: (i, 0))
    return pl.pallas_call(
        body, name=name, grid=(n_t,),
        out_shape=(jax.ShapeDtypeStruct((s, e3), BF16), jax.ShapeDtypeStruct((s, e), BF16),
                   jax.ShapeDtypeStruct((GROUPS, CHUNK, CHUNK), F32), jax.ShapeDtypeStruct((CHUNK, 128), F32),
                   jax.ShapeDtypeStruct((1, e), F32), jax.ShapeDtypeStruct((1, e), F32)),
        in_specs=[pl.BlockSpec((t, e3), lambda i: (i, 0)), tok, _full((1, d)), _full((1, e)), _full((1, e)),
                  _full((GROUPS, CHUNK, CHUNK)), _full((GROUPS, CHUNK, CHUNK)), _full((CHUNK, e)), _full((e, d))],
        out_specs=(pl.BlockSpec((t, e3), lambda i: (i, 0)), pl.BlockSpec((t, e), lambda i: (i, 0)),
                   _full((GROUPS, CHUNK, CHUNK)), _full((CHUNK, 128)), _full((1, e)), _full((1, e))),
        scratch_shapes=[pltpu.VMEM((t, e), F32), pltpu.VMEM((t, e), F32), pltpu.VMEM((t, e), F32),
                        pltpu.VMEM((t, e), F32), pltpu.VMEM((CHUNK, e), F32)],
        compiler_params=_params(("arbitrary",)),
    )(proj, dx2, gate, lng, lnb, wt, wtt, bsf, wo)


def _conv_mixer_bwd(proj, dx1, br, tails, cw, cb, gate, wo, name):
    s, e4 = proj.shape
    e = e4 // 4
    d = dx1.shape[1]
    t = min(s, 256)
    n_t = s // t
    cwid = min(e, 512)

    def body(p_ref, dx_ref, br_ref, tails_ref, cw_ref, cb_ref, gate_ref, wo_ref,
             dp_ref, y_ref, dgate_ref, dcb_ref, dcw_ref, dy_scr, head_scr):
        i = pl.program_id(0)

        @pl.when(i == 0)
        def _():
            dgate_ref[...] = jnp.zeros_like(dgate_ref)
            dcb_ref[...] = jnp.zeros_like(dcb_ref)
            dcw_ref[...] = jnp.zeros_like(dcw_ref)
            head_scr[...] = jnp.zeros_like(head_scr)
        dx = dx_ref[...]
        dgate_ref[...] += _colsum(dx * br_ref[...].astype(F32))
        dy_scr[...] = lax.dot_general((dx * gate_ref[...]).astype(BF16), wo_ref[...], NT_DIMS,
                                      preferred_element_type=F32)
        row = lax.broadcasted_iota(jnp.int32, (t, cwid), 0)
        has_prev = (i < n_t - 1).astype(F32)
        for c0 in range(0, e, cwid):
            sl = slice(c0, c0 + cwid)
            bg = p_ref[:, c0:c0 + cwid].astype(F32)
            cg = p_ref[:, e + c0:e + c0 + cwid].astype(F32)
            xin = p_ref[:, 2 * e + c0:2 * e + c0 + cwid].astype(F32)
            z = p_ref[:, 3 * e + c0:3 * e + c0 + cwid].astype(F32)
            cx = cg * xin
            p1, p2 = _conv_taps(cx, tails_ref[6:7, sl] * has_prev, tails_ref[7:8, sl] * has_prev, row)
            w0, w1, w2 = cw_ref[0:1, sl], cw_ref[1:2, sl], cw_ref[2:3, sl]
            conv = cb_ref[:, sl] + w2 * cx + w0 * p2 + w1 * p1
            sz, dsz = _silu(z)
            dy = dy_scr[:, sl]
            y_ref[:, sl] = (sz * bg * conv).astype(BF16)
            dp_ref[:, 3 * e + c0:3 * e + c0 + cwid] = (dy * bg * conv * dsz).astype(BF16)
            dp_ref[:, c0:c0 + cwid] = (dy * sz * conv).astype(BF16)
            dconv = dy * sz * bg
            dcb_ref[:, sl] += _colsum(dconv)
            dcw_ref[2:3, sl] += _colsum(dconv * cx)
            dcw_ref[1:2, sl] += _colsum(dconv * p1)
            dcw_ref[0:1, sl] += _colsum(dconv * p2)
            h0, h1 = head_scr[0:1, sl], head_scr[1:2, sl]
            n1 = jnp.where(row == t - 1, h0, pltpu.roll(dconv, t - 1, 0))
            n2 = jnp.where(row == t - 2, h0, jnp.where(row == t - 1, h1, pltpu.roll(dconv, t - 2, 0)))
            dcx = w2 * dconv + w1 * n1 + w0 * n2
            dp_ref[:, e + c0:e + c0 + cwid] = (dcx * xin).astype(BF16)
            dp_ref[:, 2 * e + c0:2 * e + c0 + cwid] = (dcx * cg).astype(BF16)
            head_scr[:, sl] = dconv[0:8, :]

    rev = lambda i: (n_t - 1 - i, 0)
    return pl.pallas_call(
        body, name=name, grid=(n_t,),
        out_shape=(jax.ShapeDtypeStruct((s, e4), BF16), jax.ShapeDtypeStruct((s, e), BF16),
                   jax.ShapeDtypeStruct((1, d), F32), jax.ShapeDtypeStruct((1, e), F32), jax.ShapeDtypeStruct((8, e), F32)),
        in_specs=[pl.BlockSpec((t, e4), rev), pl.BlockSpec((t, d), rev), pl.BlockSpec((t, d), rev),
                  pl.BlockSpec((None, 8, e), lambda i: (jnp.maximum(n_t - 2 - i, 0), 0, 0)),
                  _full((3, e)), _full((1, e)), _full((1, d)), _full((e, d))],
        out_specs=(pl.BlockSpec((t, e4), rev), pl.BlockSpec((t, e), rev), _full((1, d)), _full((1, e)), _full((8, e))),
        scratch_shapes=[pltpu.VMEM((t, e), F32), pltpu.VMEM((8, e), F32)],
        compiler_params=_params(("arbitrary",)),
    )(proj, dx1, br, tails, cw, cb, gate, wo)


def _matmul_nt_norm_bwd(dproj, wg, xin, dres, g, scale, name):
    s, d = xin.shape
    nb = wg.shape[2]
    tm = min(s, 512)
    n_i = s // tm

    def body(dp_ref, w_ref, x_ref, dres_ref, g_ref, sc_ref, dx_ref, dsh_ref, dsc_ref, dg_ref, acc, p_scr):
        i, k = pl.program_id(0), pl.program_id(1)

        @pl.when((i == 0) & (k == 0))
        def _():
            dsh_ref[...] = jnp.zeros_like(dsh_ref)
            p_scr[...] = jnp.zeros_like(p_scr)
        part = lax.dot_general(dp_ref[...], w_ref[...], NT_DIMS, preferred_element_type=F32)

        @pl.when(k == 0)
        def _():
            acc[...] = part

        @pl.when(k > 0)
        def _():
            acc[...] += part

        @pl.when(k == NDEV - 1)
        def _():
            dh = acc[...]
            xv = x_ref[...]
            r = lax.rsqrt(jnp.mean(xv * xv, axis=-1, keepdims=True) + RMS_EPS)
            xn = xv * r
            dsh_ref[...] += _colsum(dh)
            p_scr[...] += _colsum(dh * xn)
            dxn = dh * (g_ref[...] * (1.0 + sc_ref[...]))
            dx_ref[...] = r * (dxn - xn * jnp.mean(dxn * xn, axis=-1, keepdims=True)) + dres_ref[...]

        @pl.when((i == n_i - 1) & (k == NDEV - 1))
        def _():
            dsc_ref[...] = p_scr[...] * g_ref[...]
            dg_ref[...] = p_scr[...] * (1.0 + sc_ref[...])

    tok = pl.BlockSpec((tm, d), lambda i, k: (i, 0))
    vec = pl.BlockSpec((1, d), lambda i, k: (0, 0))
    vshape = jax.ShapeDtypeStruct((1, d), F32)
    return pl.pallas_call(
        body, name=name, grid=(n_i, NDEV),
        out_shape=(jax.ShapeDtypeStruct((s, d), F32), vshape, vshape, vshape),
        in_specs=[pl.BlockSpec((tm, nb), lambda i, k: (i, k)), pl.BlockSpec((None, d, nb), lambda i, k: (k, 0, 0)),
                  tok, tok, vec, vec],
        out_specs=(tok, vec, vec, vec),
        scratch_shapes=[pltpu.VMEM((tm, d), F32), pltpu.VMEM((1, d), F32)],
        compiler_params=_params(("arbitrary", "arbitrary")),
    )(dproj, wg, xin, dres, g, scale)


def _matmul_tn(a, b, colscale, rows_split, name):
    s, m = a.shape
    n = b.shape[1]
    tm, tn = (m // NDEV, n) if rows_split else (m, n // NDEV)
    tk = min(s, 512)
    n_k = s // tk

    def body(a_ref, b_ref, cs_ref, o_ref, acc):
        k = pl.program_id(1)
        part = lax.dot_general(a_ref[...], b_ref[...].astype(BF16), TN_DIMS, preferred_element_type=F32)

        @pl.when(k == 0)
        def _():
            acc[...] = part

        @pl.when(k > 0)
        def _():
            acc[...] += part

        @pl.when(k == n_k - 1)
        def _():
            o_ref[...] = (acc[...] * cs_ref[...]).astype(BF16)

    if rows_split:
        a_spec = pl.BlockSpec((tk, tm), lambda j, k: (k, j))
        b_spec = pl.BlockSpec((tk, tn), lambda j, k: (k, 0))
        c_spec = pl.BlockSpec((1, tn), lambda j, k: (0, 0))
    else:
        a_spec = pl.BlockSpec((tk, tm), lambda j, k: (k, 0))
        b_spec = pl.BlockSpec((tk, tn), lambda j, k: (k, j))
        c_spec = pl.BlockSpec((1, tn), lambda j, k: (0, j))
    return pl.pallas_call(
        body, name=name, grid=(NDEV, n_k),
        out_shape=jax.ShapeDtypeStruct((NDEV, tm, tn), BF16),
        in_specs=[a_spec, b_spec, c_spec],
        out_specs=pl.BlockSpec((None, tm, tn), lambda j, k: (j, 0, 0)),
        scratch_shapes=[pltpu.VMEM((tm, tn), F32)],
        compiler_params=_params(("parallel", "arbitrary")),
    )(a, b, colscale)


def _adam_update(w, g, m, v):
    m = ADAM_B1 * m + (1.0 - ADAM_B1) * g
    v = ADAM_B2 * v + (1.0 - ADAM_B2) * (g * g)
    m_hat = m / (1.0 - ADAM_B1 ** ADAM_STEP)
    v_hat = v / (1.0 - ADAM_B2 ** ADAM_STEP)
    return -ADAM_LR * (m_hat / (jnp.sqrt(v_hat) + ADAM_EPS) + ADAM_WD * w), m, v


def _adamw_reduce(parts, w, m, v, name):
    n_p, r, c = parts.shape
    tr = min(r, 256)

    def body(p_ref, w_ref, m_ref, v_ref, g_out, d_out, m_out, v_out):
        g = p_ref[0].astype(F32)
        for j in range(1, n_p):
            g = g + p_ref[j].astype(F32)
        g_out[...] = g
        d_out[...], m_out[...], v_out[...] = _adam_update(w_ref[...], g, m_ref[...], v_ref[...])

    blk = pl.BlockSpec((tr, c), lambda i: (i, 0))
    shp = jax.ShapeDtypeStruct((r, c), F32)
    return pl.pallas_call(
        body, name=name, grid=(r // tr,), out_shape=(shp, shp, shp, shp),
        in_specs=[pl.BlockSpec((n_p, tr, c), lambda i: (0, i, 0)), blk, blk, blk],
        out_specs=(blk, blk, blk, blk),
        compiler_params=_params(("parallel",)),
    )(parts, w, m, v)


def _adamw_small(gs, ws, ms, vs, name):
    n = len(gs)

    def body(*refs):
        ins, outs = refs[:4 * n], refs[4 * n:]
        for a in range(n):
            d, m, v = _adam_update(ins[n + a][...], ins[a][...], ins[2 * n + a][...], ins[3 * n + a][...])
            outs[a][...], outs[n + a][...], outs[2 * n + a][...] = d, m, v

    shapes = tuple(jax.ShapeDtypeStruct(w.shape, F32) for w in ws) * 3
    out = pl.pallas_call(
        body, name=name, out_shape=shapes,
        in_specs=[_vmem()] * (4 * n), out_specs=tuple([_vmem()] * (3 * n)),
        compiler_params=pltpu.CompilerParams(vmem_limit_bytes=VMEM_LIMIT),
    )(*gs, *ws, *ms, *vs)
    return out[:n], out[n:2 * n], out[2 * n:]


def _sum_devices(packed, name):
    _, r, wdt = packed.shape

    def body(p_ref, o_ref):
        acc = p_ref[0]
        for j in range(1, NDEV):
            acc = acc + p_ref[j]
        o_ref[...] = acc

    return pl.pallas_call(
        body, name=name, out_shape=jax.ShapeDtypeStruct((r, wdt), F32),
        in_specs=[_vmem()], out_specs=_vmem(),
        compiler_params=pltpu.CompilerParams(vmem_limit_bytes=VMEM_LIMIT),
    )(packed)


def _mod_w_grad(c_t, dmod, name):
    n_layers, _, w3 = dmod.shape
    d = c_t.shape[0]

    def body(c_ref, dm_ref, o_ref):
        for i in range(n_layers):
            acc = c_ref[:, 0:1] * dm_ref[i, 0:1, :]
            for b in range(1, NDEV):
                acc = acc + c_ref[:, b:b + 1] * dm_ref[i, b:b + 1, :]
            o_ref[i] = acc

    return pl.pallas_call(
        body, name=name, out_shape=jax.ShapeDtypeStruct((n_layers, d, w3), F32),
        in_specs=[_vmem(), _vmem()], out_specs=_vmem(),
        compiler_params=pltpu.CompilerParams(vmem_limit_bytes=VMEM_LIMIT),
    )(c_t, dmod)


def _mask_transpose_ws(w_s, name):
    def body(w_ref, wt_ref, wtt_ref):
        tril = (lax.broadcasted_iota(jnp.int32, (CHUNK, CHUNK), 0) >= lax.broadcasted_iota(jnp.int32, (CHUNK, CHUNK), 1))
        for g in range(GROUPS):
            wm = jnp.where(tril, w_ref[g], 0.0)
            wt_ref[g] = wm.astype(BF16)
            wtt_ref[g] = wm.T.astype(BF16)

    shp = jax.ShapeDtypeStruct(w_s.shape, BF16)
    return pl.pallas_call(
        body, name=name, out_shape=(shp, shp), in_specs=[_vmem()], out_specs=(_vmem(), _vmem()),
    )(w_s)


def _pack(pieces):
    flat = jnp.concatenate([p.reshape(-1) for p in pieces])
    rows = -(-flat.shape[0] // (8 * PACK_W)) * 8
    return jnp.pad(flat, (0, rows * PACK_W - flat.shape[0])).reshape(rows, PACK_W)


def _unpack(flat, shapes):
    out, off = [], 0
    for shp in shapes:
        size = 1
        for dim in shp:
            size *= dim
        out.append(flat[off:off + size].reshape(shp))
        off += size
    return out


def kernel(x, c, mod_w, mod_b, norm_g, a_w_in, a_conv_w, a_conv_b, a_w_out, b_w_in, b_ln_g, b_ln_b, b_w_s, b_b_s, b_w_out, final_g, loss_target, m_mod_w, m_mod_b, m_norm_g, m_a_w_in, m_a_conv_w, m_a_conv_b, m_a_w_out, m_b_w_in, m_b_ln_g, m_b_ln_b, m_b_w_s, m_b_b_s, m_b_w_out, m_final_g, v_mod_w, v_mod_b, v_norm_g, v_a_w_in, v_a_conv_w, v_a_conv_b, v_a_w_out, v_b_w_in, v_b_ln_g, v_b_ln_b, v_b_w_s, v_b_b_s, v_b_w_out, v_final_g):
    s, d = x.shape[1], x.shape[2]
    es = a_w_out.shape[1]
    e = NDEV * es
    w3 = mod_w.shape[2]
    me = _index(_pos())
    x0 = x.reshape(s, d)
    tgt = loss_target.reshape(s, d)

    small = jnp.concatenate([a_conv_w[0], b_ln_g, b_ln_b, jnp.zeros((3, es), F32)], axis=0)
    wa, woa, wb, wob, small_all = _all_gather(
        [a_w_in[0].astype(BF16), a_w_out[0].astype(BF16), b_w_in[0].astype(BF16), b_w_out[0].astype(BF16), small],
        "gather_weights")
    woa, wob = woa.reshape(e, d), wob.reshape(e, d)
    small_all = small_all.transpose(1, 0, 2).reshape(8, e)
    conv_w, ln_g, ln_b = small_all[0:3], small_all[3:4], small_all[4:5]
    bsf = jnp.repeat(b_b_s[0].T, e // GROUPS, axis=1)
    wt, wtt = _mask_transpose_ws(b_w_s[0], "mask_w_s")

    mod, c_all = _mod_vectors(c, mod_w, mod_b)
    shift0, scale0, gate0 = mod[0:1, 0:d], mod[0:1, d:2 * d], mod[0:1, 2 * d:]
    shift1, scale1, gate1 = mod[1:2, 0:d], mod[1:2, d:2 * d], mod[1:2, 2 * d:]
    g0, g1, fg = norm_g[0:1], norm_g[1:2], final_g.reshape(1, d)

    proj_a, h0 = _norm_mod_matmul(x0, g0, scale0, shift0, wa, "a_in_proj")
    x1, br_a, tails = _conv_mixer_fwd(proj_a, x0, conv_w, a_conv_b, gate0, woa, "a_mixer_fwd")
    proj_b, h1 = _norm_mod_matmul(x1, g1, scale1, shift1, wb, "b_in_proj")
    dx2, loss_acc, dfg, dgate1 = _gmlp_fwd_loss(proj_b, x1, tgt, gate1, fg, ln_g, ln_b, wt, bsf, wob, "b_mixer_fwd_loss")
    loss = lax.psum(loss_acc[0, 0], ("x", "y", "c"))

    dproj_b, y_b, dws, dbs, dlg, dlb = _gmlp_bwd(proj_b, dx2, gate1, ln_g, ln_b, wt, wtt, bsf, wob, "b_mixer_bwd")
    ones_d = jnp.ones((1, d), F32)
    gs_b_out = _matmul_tn(y_b, dx2, gate1, True, "b_w_out_grad")
    dx1, dshift1, dscale1, dg1 = _matmul_nt_norm_bwd(dproj_b, wb, x1, dx2, g1, scale1, "b_in_bwd")
    gs_b_in = _matmul_tn(h1, dproj_b, jnp.ones((1, dproj_b.shape[1]), F32), False, "b_w_in_grad")
    dproj_a, y_a, dgate0, dcb, dcw = _conv_mixer_bwd(proj_a, dx1, br_a, tails, conv_w, a_conv_b, gate0, woa, "a_mixer_bwd")
    gs_a_out = _matmul_tn(y_a, dx1, gate0, True, "a_w_out_grad")
    dx0, dshift0, dscale0, dg0 = _matmul_nt_norm_bwd(dproj_a, wa, x0, dx1, g0, scale0, "a_in_bwd")
    gs_a_in = _matmul_tn(h0, dproj_a, jnp.ones((1, dproj_a.shape[1]), F32), False, "a_w_in_grad")
    del ones_d

    gr_a_in, gr_a_out, gr_b_in, gr_b_out = _scatter_to_owners([gs_a_in, gs_a_out, gs_b_in, gs_b_out], "scatter_grads")
    pieces = [dshift0, dscale0, dgate0, dshift1, dscale1, dgate1, dg0, dg1, dcb, dcw[0:3], dlg, dlb, dfg,
              dbs[:, 0:GROUPS].T, dws]
    shapes = [p.shape for p in pieces]
    packed = _pack(pieces)
    packed_all, = _all_gather([packed], "gather_small_grads")
    total = _sum_devices(packed_all, "sum_small_grads").reshape(-1)
    (t_sh0, t_sc0, t_ga0, t_sh1, t_sc1, t_ga1, t_g0, t_g1, t_cb, t_cw, t_lg, t_lb, t_fg, t_bs, t_ws) = _unpack(total, shapes)
    grad_mod_b = jnp.concatenate([jnp.concatenate([t_sh0, t_sc0, t_ga0], axis=1),
                                  jnp.concatenate([t_sh1, t_sc1, t_ga1], axis=1)], axis=0)
    grad_norm_g = jnp.concatenate([t_g0, t_g1], axis=0)
    dmod_all = packed_all.reshape(NDEV, -1)[:, 0:6 * d].reshape(NDEV, 2, 3 * d).transpose(1, 0, 2)
    dmod_mine = lax.dynamic_slice_in_dim(dmod_all, me * w3, w3, axis=2)
    grad_mod_w = _mod_w_grad(c_all.T, dmod_mine, "mod_w_grad")
    grad_a_conv_w = lax.dynamic_slice_in_dim(t_cw, me * es, es, axis=1)
    grad_b_ln_g = lax.dynamic_slice_in_dim(t_lg, me * es, es, axis=1)
    grad_b_ln_b = lax.dynamic_slice_in_dim(t_lb, me * es, es, axis=1)

    def big(parts, w, m, v, name):
        shp = w.shape
        r2 = lambda t_: t_.reshape(-1, shp[-1])
        g, dl, nm, nv = _adamw_reduce(parts, r2(w), r2(m), r2(v), name)
        return tuple(t_.reshape(shp) for t_ in (g, dl, nm, nv))

    res = {}
    res["mod_w"] = big(grad_mod_w.reshape(1, -1, w3), mod_w, m_mod_w, v_mod_w, "adamw_mod_w")
    res["a_w_in"] = big(gr_a_in, a_w_in, m_a_w_in, v_a_w_in, "adamw_a_w_in")
    res["a_w_out"] = big(gr_a_out, a_w_out, m_a_w_out, v_a_w_out, "adamw_a_w_out")
    res["b_w_in"] = big(gr_b_in, b_w_in, m_b_w_in, v_b_w_in, "adamw_b_w_in")
    res["b_w_out"] = big(gr_b_out, b_w_out, m_b_w_out, v_b_w_out, "adamw_b_w_out")

    small_names = ["mod_b", "norm_g", "a_conv_w", "a_conv_b", "b_ln_g", "b_ln_b", "b_w_s", "b_b_s", "final_g"]
    small_g = [grad_mod_b, grad_norm_g, grad_a_conv_w, t_cb, grad_b_ln_g, grad_b_ln_b, t_ws, t_bs, t_fg]
    small_w = [mod_b, norm_g, a_conv_w, a_conv_b, b_ln_g, b_ln_b, b_w_s, b_b_s, final_g]
    small_m = [m_mod_b, m_norm_g, m_a_conv_w, m_a_conv_b, m_b_ln_g, m_b_ln_b, m_b_w_s, m_b_b_s, m_final_g]
    small_v = [v_mod_b, v_norm_g, v_a_conv_w, v_a_conv_b, v_b_ln_g, v_b_ln_b, v_b_w_s, v_b_b_s, v_final_g]
    as2d = lambda t_: t_.reshape(-1, t_.shape[-1])
    dls, nms, nvs = _adamw_small([as2d(t_) for t_ in small_g], [as2d(t_) for t_ in small_w],
                                 [as2d(t_) for t_ in small_m], [as2d(t_) for t_ in small_v], "adamw_small")
    for a, nme in enumerate(small_names):
        shp = small_w[a].shape
        res[nme] = (small_g[a].reshape(shp), dls[a].reshape(shp), nms[a].reshape(shp), nvs[a].reshape(shp))

    order = ["mod_w", "mod_b", "norm_g", "a_w_in", "a_conv_w", "a_conv_b", "a_w_out", "b_w_in", "b_ln_g", "b_ln_b",
             "b_w_s", "b_b_s", "b_w_out", "final_g"]
    return (loss, dx0.reshape(x.shape), *[res[k][0] for k in order], *[res[k][1] for k in order],
            *[res[k][2] for k in order], *[res[k][3] for k in order])
```

```python
import functools

import jax
import jax.numpy as jnp
from jax import lax
from jax.experimental import pallas as pl
from jax.experimental.pallas import tpu as pltpu

NDEV = 8
CHUNK = 128
GROUPS = 8
RMS_EPS = 1e-6
LN_EPS = 1e-5
ADAM_LR, ADAM_B1, ADAM_B2, ADAM_EPS, ADAM_WD, ADAM_STEP = 0.001, 0.9, 0.999, 1e-08, 0.01, 10
V7X_VMEM_BYTES = 64 * 1024 * 1024
VMEM_LIMIT = V7X_VMEM_BYTES - 8 * 1024 * 1024
TN_VMEM_BUDGET = 46 * 1024 * 1024
PACK_W = 1024
F32, BF16 = jnp.float32, jnp.bfloat16
MESH = pl.DeviceIdType.MESH
RSQRT2 = 0.7071067811865476
INV_SQRT_2PI = 0.3989422804014327
NT_DIMS = (((1,), (1,)), ((), ()))
TN_DIMS = (((0,), (0,)), ((), ()))


def _params(sem=None):
    return pltpu.CompilerParams(dimension_semantics=sem, vmem_limit_bytes=VMEM_LIMIT)


def _vmem():
    return pl.BlockSpec(memory_space=pltpu.VMEM)


def _hbm():
    return pl.BlockSpec(memory_space=pltpu.HBM)


def _full(shape):
    return pl.BlockSpec(shape, lambda *_: (0,) * len(shape))


def _pos():
    return lax.axis_index("x"), lax.axis_index("y"), lax.axis_index("c")


def _index(p):
    return 4 * p[0] + 2 * p[1] + p[2]


def _peer(k):
    x, y, c = _pos()
    return ((1 - x) if (k >> 2) & 1 else x, (1 - y) if (k >> 1) & 1 else y, (1 - c) if k & 1 else c)


def _silu(z):
    sg = jax.nn.sigmoid(z)
    return z * sg, sg * (1.0 + z * (1.0 - sg))


def _gelu(v):
    phi = 0.5 * (1.0 + lax.erf(v * RSQRT2))
    return v * phi, phi + v * (jnp.exp(-0.5 * v * v) * INV_SQRT_2PI)


def _colsum(v):
    return jnp.sum(v, axis=0, keepdims=True)


def _rowsum(v):
    return jnp.sum(v, axis=-1, keepdims=True)


def _gather_all_vmem(slab_ref, send_sems, recv_sems, base):
    me = _index(_pos())
    sends = []
    for k in range(1, NDEV):
        cp = pltpu.make_async_remote_copy(
            src_ref=slab_ref.at[me], dst_ref=slab_ref.at[me],
            send_sem=send_sems.at[base + k - 1], recv_sem=recv_sems.at[base + k - 1],
            device_id=_peer(k), device_id_type=MESH)
        cp.start()
        sends.append(cp)
    for k in range(1, NDEV):
        src = _index(_peer(k))
        pltpu.make_async_remote_copy(
            src_ref=slab_ref.at[src], dst_ref=slab_ref.at[src],
            send_sem=send_sems.at[base + k - 1], recv_sem=recv_sems.at[base + k - 1],
            device_id=_peer(k), device_id_type=MESH).wait_recv()
    for cp in sends:
        cp.wait_send()


def _mod_vectors(c, mod_w, mod_b):
    n_layers, d, w3 = mod_w.shape

    def body(c_ref, mw_ref, mb_ref, mod_ref, call_ref, cslab, pslab, send_sems, recv_sems):
        me = _index(_pos())
        cv = c_ref[...]
        cslab[me] = jnp.broadcast_to(cv * jax.nn.sigmoid(cv), (8, d))
        _gather_all_vmem(cslab, send_sems, recv_sems, 0)
        c_all = jnp.concatenate([cslab[k, 0:1, :] for k in range(NDEV)], axis=0)
        call_ref[...] = c_all
        for i in range(n_layers):
            pslab[me, i * NDEV:(i + 1) * NDEV, :] = jnp.dot(
                c_all, mw_ref[i], preferred_element_type=F32, precision=lax.Precision.HIGHEST)
        _gather_all_vmem(pslab, send_sems, recv_sems, NDEV - 1)
        for i in range(n_layers):
            for k in range(NDEV):
                mod_ref[i:i + 1, k * w3:(k + 1) * w3] = (
                    pslab[k, pl.ds(i * NDEV + me, 1), :] + mb_ref[i:i + 1, k * w3:(k + 1) * w3])

    return pl.pallas_call(
        body, name="mod_vectors",
        out_shape=(jax.ShapeDtypeStruct((n_layers, 3 * d), F32), jax.ShapeDtypeStruct((NDEV, d), F32)),
        in_specs=[_vmem(), _vmem(), _vmem()], out_specs=(_vmem(), _vmem()),
        scratch_shapes=[pltpu.VMEM((NDEV, 8, d), F32), pltpu.VMEM((NDEV, n_layers * NDEV, w3), F32),
                        pltpu.SemaphoreType.DMA((2 * (NDEV - 1),)), pltpu.SemaphoreType.DMA((2 * (NDEV - 1),))],
        compiler_params=pltpu.CompilerParams(vmem_limit_bytes=VMEM_LIMIT),
    )(c, mod_w, mod_b)


def _all_gather(shards, by_cols, name):
    n = len(shards)

    def body(*refs):
        ins, outs = refs[:n], refs[n:2 * n]
        send_sems, recv_sems, local_sems = refs[2 * n:]
        x, y, c = _pos()
        me, sibling = (x, y, c), (x, y, 1 - c)
        chips = [(1 - x, y), (x, 1 - y), (1 - x, 1 - y)]

        def place(a, block):
            r, cc = shards[a].shape
            if by_cols[a]:
                return outs[a].at[:, pl.ds(_index(block) * cc, cc)]
            return outs[a].at[pl.ds(_index(block) * r, r), :]

        def copy(a, k, block, to, src=None):
            dst = place(a, block)
            return pltpu.make_async_remote_copy(
                src_ref=dst if src is None else src, dst_ref=dst,
                send_sem=send_sems.at[a * 7 + k], recv_sem=recv_sems.at[a * 7 + k],
                device_id=to, device_id_type=MESH)

        mine = [pltpu.make_async_copy(ins[a], place(a, me), local_sems.at[a]) for a in range(n)]
        for cp in mine:
            cp.start()
        first = []
        for a in range(n):
            first.append(copy(a, 0, me, sibling, src=ins[a]))
            first += [copy(a, 1 + j, me, (*chip, c), src=ins[a]) for j, chip in enumerate(chips)]
        for cp in first:
            cp.start()
        passed = []
        for j, chip in enumerate(chips):
            for a in range(n):
                copy(a, 1 + j, (*chip, c), me).wait_recv()
                cp = copy(a, 4 + j, (*chip, c), sibling)
                cp.start()
                passed.append(cp)
        for a in range(n):
            copy(a, 0, sibling, me).wait_recv()
        for j, chip in enumerate(chips):
            for a in range(n):
                copy(a, 4 + j, (*chip, 1 - c), me).wait_recv()
        for cp in first + passed:
            cp.wait_send()
        for cp in mine:
            cp.wait()

    return pl.pallas_call(
        body, name=name,
        out_shape=tuple(
            jax.ShapeDtypeStruct((s.shape[0], NDEV * s.shape[1]) if bc else (NDEV * s.shape[0], s.shape[1]), s.dtype)
            for s, bc in zip(shards, by_cols)),
        in_specs=[_hbm()] * n, out_specs=tuple([_hbm()] * n),
        scratch_shapes=[pltpu.SemaphoreType.DMA((7 * n,)), pltpu.SemaphoreType.DMA((7 * n,)),
                        pltpu.SemaphoreType.DMA((n,))],
    )(*shards)


def _scatter_to_owners(parts, name):
    n = len(parts)

    def body(*refs):
        ins, outs = refs[:n], refs[n:2 * n]
        send_sems, recv_sems, local_sems = refs[2 * n:]
        me = _index(_pos())
        mine = [pltpu.make_async_copy(ins[a].at[me], outs[a].at[me], local_sems.at[a]) for a in range(n)]
        for cp in mine:
            cp.start()
        sends = []
        for k in range(1, NDEV):
            peer = _peer(k)
            for a in range(n):
                cp = pltpu.make_async_remote_copy(
                    src_ref=ins[a].at[_index(peer)], dst_ref=outs[a].at[me],
                    send_sem=send_sems.at[a * 7 + k - 1], recv_sem=recv_sems.at[a * 7 + k - 1],
                    device_id=peer, device_id_type=MESH)
                cp.start()
                sends.append(cp)
        for k in range(1, NDEV):
            peer = _peer(k)
            for a in range(n):
                slot = outs[a].at[_index(peer)]
                pltpu.make_async_remote_copy(
                    src_ref=slot, dst_ref=slot,
                    send_sem=send_sems.at[a * 7 + k - 1], recv_sem=recv_sems.at[a * 7 + k - 1],
                    device_id=peer, device_id_type=MESH).wait_recv()
        for cp in sends:
            cp.wait_send()
        for cp in mine:
            cp.wait()

    return pl.pallas_call(
        body, name=name,
        out_shape=tuple(jax.ShapeDtypeStruct(p.shape, p.dtype) for p in parts),
        in_specs=[_hbm()] * n, out_specs=tuple([_hbm()] * n),
        scratch_shapes=[pltpu.SemaphoreType.DMA((7 * n,)), pltpu.SemaphoreType.DMA((7 * n,)),
                        pltpu.SemaphoreType.DMA((n,))],
    )(*parts)


def _norm_mod_matmul(x, g, scale, shift, w, name):
    s, d = x.shape
    nb = w.shape[1] // NDEV
    tm = min(s, 1024)

    def body(x_ref, g_ref, sc_ref, sh_ref, w_ref, proj_ref, h_ref):
        @pl.when(pl.program_id(1) == 0)
        def _():
            xv = x_ref[...]
            r = lax.rsqrt(jnp.mean(xv * xv, axis=-1, keepdims=True) + RMS_EPS)
            h_ref[...] = ((xv * r) * g_ref[...] * (1.0 + sc_ref[...]) + sh_ref[...]).astype(BF16)
        proj_ref[...] = jnp.dot(h_ref[...], w_ref[...], preferred_element_type=F32).astype(BF16)

    vec = pl.BlockSpec((1, d), lambda i, j: (0, 0))
    return pl.pallas_call(
        body, name=name, grid=(s // tm, NDEV),
        out_shape=(jax.ShapeDtypeStruct((s, NDEV * nb), BF16), jax.ShapeDtypeStruct((s, d), BF16)),
        in_specs=[pl.BlockSpec((tm, d), lambda i, j: (i, 0)), vec, vec, vec,
                  pl.BlockSpec((d, nb), lambda i, j: (0, j))],
        out_specs=(pl.BlockSpec((tm, nb), lambda i, j: (i, j)), pl.BlockSpec((tm, d), lambda i, j: (i, 0))),
        compiler_params=_params(("parallel", "arbitrary")),
    )(x, g, scale, shift, w)


def _conv_taps(cx, t6, t7, row):
    p1 = jnp.where(row == 0, t7, pltpu.roll(cx, 1, 0))
    p2 = jnp.where(row == 0, t6, jnp.where(row == 1, t7, pltpu.roll(cx, 2, 0)))
    return p1, p2


def _conv_mixer_fwd(proj, x, cw, cb, gate, wo, name):
    s, e4 = proj.shape
    e = e4 // 4
    d = x.shape[1]
    t = min(s, 256)
    cwid = min(e, 512)

    def body(p_ref, x_ref, cw_ref, cb_ref, gate_ref, wo_ref, x1_ref, br_ref, tails_ref, y_scr, tail_scr):
        @pl.when(pl.program_id(0) == 0)
        def _():
            tail_scr[...] = jnp.zeros_like(tail_scr)
        row = lax.broadcasted_iota(jnp.int32, (t, cwid), 0)
        for c0 in range(0, e, cwid):
            sl = slice(c0, c0 + cwid)
            bg = p_ref[:, c0:c0 + cwid].astype(F32)
            cx = p_ref[:, e + c0:e + c0 + cwid].astype(F32) * p_ref[:, 2 * e + c0:2 * e + c0 + cwid].astype(F32)
            z = p_ref[:, 3 * e + c0:3 * e + c0 + cwid].astype(F32)
            p1, p2 = _conv_taps(cx, tail_scr[6:7, sl], tail_scr[7:8, sl], row)
            conv = cb_ref[:, sl] + cw_ref[2:3, sl] * cx + cw_ref[0:1, sl] * p2 + cw_ref[1:2, sl] * p1
            y_scr[:, sl] = (_silu(z)[0] * bg * conv).astype(BF16)
            tail_scr[:, sl] = cx[t - 8:t, :]
        tails_ref[...] = tail_scr[...]
        br = jnp.dot(y_scr[...], wo_ref[...], preferred_element_type=F32)
        x1_ref[...] = x_ref[...] + gate_ref[...] * br
        br_ref[...] = br.astype(BF16)

    return pl.pallas_call(
        body, name=name, grid=(s // t,),
        out_shape=(jax.ShapeDtypeStruct((s, d), F32), jax.ShapeDtypeStruct((s, d), BF16),
                   jax.ShapeDtypeStruct((s // t, 8, e), F32)),
        in_specs=[pl.BlockSpec((t, e4), lambda i: (i, 0)), pl.BlockSpec((t, d), lambda i: (i, 0)),
                  _full((3, e)), _full((1, e)), _full((1, d)), _full((e, d))],
        out_specs=(pl.BlockSpec((t, d), lambda i: (i, 0)), pl.BlockSpec((t, d), lambda i: (i, 0)),
                   pl.BlockSpec((None, 8, e), lambda i: (i, 0, 0))),
        scratch_shapes=[pltpu.VMEM((t, e), BF16), pltpu.VMEM((8, e), F32)],
        compiler_params=_params(("arbitrary",)),
    )(proj, x, cw, cb, gate, wo)


def _ln_stats(p_ref, v_scr, dgv_scr, t, e):
    gw = e // GROUPS
    s1 = jnp.zeros((t, 1), F32)
    for g in range(GROUPS):
        v, dgv = _gelu(p_ref[:, e + g * gw:e + (g + 1) * gw].astype(F32))
        v_scr[:, g * gw:(g + 1) * gw] = v
        if dgv_scr is not None:
            dgv_scr[:, g * gw:(g + 1) * gw] = dgv
        s1 = s1 + _rowsum(v)
    mu = s1 * (1.0 / e)
    s2 = jnp.zeros((t, 1), F32)
    for g in range(GROUPS):
        dv = v_scr[:, g * gw:(g + 1) * gw] - mu
        s2 = s2 + _rowsum(dv * dv)
    return mu, lax.rsqrt(s2 * (1.0 / e) + LN_EPS)


def _gmlp_fwd_loss(proj, x1, tgt, gate, fg, lng, lnb, wt, bsf, wo, name):
    s, e3 = proj.shape
    e = e3 // 3
    d = x1.shape[1]
    gw = e // GROUPS
    t = min(s, 256)

    def body(p_ref, x1_ref, tgt_ref, gate_ref, fg_ref, lng_ref, lnb_ref, wt_ref, bsf_ref, wo_ref,
             dx2_ref, loss_ref, dfg_ref, dgate_ref, v_scr, y_scr):
        @pl.when(pl.program_id(0) == 0)
        def _():
            loss_ref[...] = jnp.zeros_like(loss_ref)
            dfg_ref[...] = jnp.zeros_like(dfg_ref)
            dgate_ref[...] = jnp.zeros_like(dgate_ref)
        mu, rs = _ln_stats(p_ref, v_scr, None, t, e)
        for g in range(GROUPS):
            gs = slice(g * gw, (g + 1) * gw)
            vn = (((v_scr[:, gs] - mu) * rs) * lng_ref[:, gs] + lnb_ref[:, gs]).astype(BF16)
            for ch in range(t // CHUNK):
                rows = slice(ch * CHUNK, (ch + 1) * CHUNK)
                mixed = jnp.dot(wt_ref[g], vn[rows], preferred_element_type=F32) + bsf_ref[:, gs]
                u = _gelu(p_ref[rows, g * gw:(g + 1) * gw].astype(F32))[0]
                sz = _silu(p_ref[rows, 2 * e + g * gw:2 * e + (g + 1) * gw].astype(F32))[0]
                y_scr[rows, gs] = (sz * (u * mixed)).astype(BF16)
        br = jnp.dot(y_scr[...], wo_ref[...], preferred_element_type=F32)
        x2 = x1_ref[...] + gate_ref[...] * br
        r2 = lax.rsqrt(jnp.mean(x2 * x2, axis=-1, keepdims=True) + RMS_EPS)
        xn = x2 * r2
        diff = xn * fg_ref[...] - tgt_ref[...]
        loss_ref[...] += jnp.broadcast_to(0.5 * _colsum(jnp.mean(diff * diff, axis=-1, keepdims=True)), loss_ref.shape)
        dout = diff * (1.0 / d)
        dfg_ref[...] += _colsum(dout * xn)
        dxn = dout * fg_ref[...]
        dx2 = r2 * (dxn - xn * jnp.mean(dxn * xn, axis=-1, keepdims=True))
        dx2_ref[...] = dx2
        dgate_ref[...] += _colsum(dx2 * br)

    tok = pl.BlockSpec((t, d), lambda i: (i, 0))
    return pl.pallas_call(
        body, name=name, grid=(s // t,),
        out_shape=(jax.ShapeDtypeStruct((s, d), F32), jax.ShapeDtypeStruct((8, 128), F32),
                   jax.ShapeDtypeStruct((1, d), F32), jax.ShapeDtypeStruct((1, d), F32)),
        in_specs=[pl.BlockSpec((t, e3), lambda i: (i, 0)), tok, tok, _full((1, d)), _full((1, d)),
                  _full((1, e)), _full((1, e)), _full((GROUPS, CHUNK, CHUNK)), _full((CHUNK, e)), _full((e, d))],
        out_specs=(tok, _full((8, 128)), _full((1, d)), _full((1, d))),
        scratch_shapes=[pltpu.VMEM((t, e), F32), pltpu.VMEM((t, e), BF16)],
        compiler_params=_params(("arbitrary",)),
    )(proj, x1, tgt, gate, fg, lng, lnb, wt, bsf, wo)


def _gmlp_bwd(proj, dx2, gate, lng, lnb, wt, wtt, bsf, wo, name):
    s, e3 = proj.shape
    e = e3 // 3
    d = dx2.shape[1]
    gw = e // GROUPS
    t = min(s, 256)
    n_t = s // t

    def body(p_ref, dx_ref, gate_ref, lng_ref, lnb_ref, wt_ref, wtt_ref, bsf_ref, wo_ref,
             dp_ref, y_ref, dws_ref, dbs_ref, dlg_ref, dlb_ref, v_scr, dgv_scr, dy_scr, dvn_scr, dbs_scr):
        @pl.when(pl.program_id(0) == 0)
        def _():
            dws_ref[...] = jnp.zeros_like(dws_ref)
            dlg_ref[...] = jnp.zeros_like(dlg_ref)
            dlb_ref[...] = jnp.zeros_like(dlb_ref)
            dbs_scr[...] = jnp.zeros_like(dbs_scr)
        dbr = (dx_ref[...] * gate_ref[...]).astype(BF16)
        dy_scr[...] = lax.dot_general(dbr, wo_ref[...], NT_DIMS, preferred_element_type=F32)
        mu, rs = _ln_stats(p_ref, v_scr, dgv_scr, t, e)
        tril = (lax.broadcasted_iota(jnp.int32, (CHUNK, CHUNK), 0) >= lax.broadcasted_iota(jnp.int32, (CHUNK, CHUNK), 1))
        c1 = jnp.zeros((t, 1), F32)
        c2 = jnp.zeros((t, 1), F32)
        for g in range(GROUPS):
            gs = slice(g * gw, (g + 1) * gw)
            vhat = (v_scr[:, gs] - mu) * rs
            lg = lng_ref[:, gs]
            vn = (vhat * lg + lnb_ref[:, gs]).astype(BF16)
            for ch in range(t // CHUNK):
                rows = slice(ch * CHUNK, (ch + 1) * CHUNK)
                mixed = jnp.dot(wt_ref[g], vn[rows], preferred_element_type=F32) + bsf_ref[:, gs]
                u, dgu = _gelu(p_ref[rows, g * gw:(g + 1) * gw].astype(F32))
                sz, dsz = _silu(p_ref[rows, 2 * e + g * gw:2 * e + (g + 1) * gw].astype(F32))
                sgate = u * mixed
                y_ref[rows, gs] = (sz * sgate).astype(BF16)
                dy = dy_scr[rows, gs]
                dp_ref[rows, 2 * e + g * gw:2 * e + (g + 1) * gw] = (dy * sgate * dsz).astype(BF16)
                ds = dy * sz
                dp_ref[rows, gs] = (ds * mixed * dgu).astype(BF16)
                dm = ds * u
                dbs_scr[:, gs] += dm
                dmb = dm.astype(BF16)
                dws_ref[g] += jnp.where(tril, lax.dot_general(dmb, vn[rows], NT_DIMS, preferred_element_type=F32), 0.0)
                dvn_scr[rows, gs] = jnp.dot(wtt_ref[g], dmb, preferred_element_type=F32)
            dvn = dvn_scr[:, gs]
            dlb_ref[:, gs] += _colsum(dvn)
            dlg_ref[:, gs] += _colsum(dvn * vhat)
            dvh = dvn * lg
            c1 = c1 + _rowsum(dvh)
            c2 = c2 + _rowsum(dvh * vhat)
        c1 = c1 * (1.0 / e)
        c2 = c2 * (1.0 / e)
        for g in range(GROUPS):
            gs = slice(g * gw, (g + 1) * gw)
            vhat = (v_scr[:, gs] - mu) * rs
            dv = rs * (dvn_scr[:, gs] * lng_ref[:, gs] - c1 - vhat * c2)
            dp_ref[:, e + g * gw:e + (g + 1) * gw] = (dv * dgv_scr[:, gs]).astype(BF16)

        @pl.when(pl.program_id(0) == n_t - 1)
        def _():
            lane = lax.broadcasted_iota(jnp.int32, (CHUNK, 128), 1)
            acc = jnp.zeros((CHUNK, 128), F32)
            for g in range(GROUPS):
                acc = acc + jnp.where(lane == g, _rowsum(dbs_scr[:, g * gw:(g + 1) * gw]), 0.0)
            dbs_ref[...] = acc

    tok = pl.BlockSpec((t, d), lambda i: (i, 0))
    return pl.pallas_call(
        body, name=name, grid=(n_t,),
        out_shape=(jax.ShapeDtypeStruct((s, e3), BF16), jax.ShapeDtypeStruct((s, e), BF16),
                   jax.ShapeDtypeStruct((GROUPS, CHUNK, CHUNK), F32), jax.ShapeDtypeStruct((CHUNK, 128), F32),
                   jax.ShapeDtypeStruct((1, e), F32), jax.ShapeDtypeStruct((1, e), F32)),
        in_specs=[pl.BlockSpec((t, e3), lambda i: (i, 0)), tok, _full((1, d)), _full((1, e)), _full((1, e)),
                  _full((GROUPS, CHUNK, CHUNK)), _full((GROUPS, CHUNK, CHUNK)), _full((CHUNK, e)), _full((e, d))],
        out_specs=(pl.BlockSpec((t, e3), lambda i: (i, 0)), pl.BlockSpec((t, e), lambda i: (i, 0)),
                   _full((GROUPS, CHUNK, CHUNK)), _full((CHUNK, 128)), _full((1, e)), _full((1, e))),
        scratch_shapes=[pltpu.VMEM((t, e), F32), pltpu.VMEM((t, e), F32), pltpu.VMEM((t, e), F32),
                        pltpu.VMEM((t, e), F32), pltpu.VMEM((CHUNK, e), F32)],
        compiler_params=_params(("arbitrary",)),
    )(proj, dx2, gate, lng, lnb, wt, wtt, bsf, wo)


def _conv_mixer_bwd(proj, dx1, br, tails, cw, cb, gate, wo, name):
    s, e4 = proj.shape
    e = e4 // 4
    d = dx1.shape[1]
    t = min(s, 256)
    n_t = s // t
    cwid = min(e, 512)

    def body(p_ref, dx_ref, br_ref, tails_ref, cw_ref, cb_ref, gate_ref, wo_ref,
             dp_ref, y_ref, dgate_ref, dcb_ref, dcw_ref, dy_scr, head_scr):
        i = pl.program_id(0)

        @pl.when(i == 0)
        def _():
            dgate_ref[...] = jnp.zeros_like(dgate_ref)
            dcb_ref[...] = jnp.zeros_like(dcb_ref)
            dcw_ref[...] = jnp.zeros_like(dcw_ref)
            head_scr[...] = jnp.zeros_like(head_scr)
        dx = dx_ref[...]
        dgate_ref[...] += _colsum(dx * br_ref[...].astype(F32))
        dy_scr[...] = lax.dot_general((dx * gate_ref[...]).astype(BF16), wo_ref[...], NT_DIMS,
                                      preferred_element_type=F32)
        row = lax.broadcasted_iota(jnp.int32, (t, cwid), 0)
        has_prev = (i < n_t - 1).astype(F32)
        for c0 in range(0, e, cwid):
            sl = slice(c0, c0 + cwid)
            bg = p_ref[:, c0:c0 + cwid].astype(F32)
            cg = p_ref[:, e + c0:e + c0 + cwid].astype(F32)
            xin = p_ref[:, 2 * e + c0:2 * e + c0 + cwid].astype(F32)
            z = p_ref[:, 3 * e + c0:3 * e + c0 + cwid].astype(F32)
            cx = cg * xin
            p1, p2 = _conv_taps(cx, tails_ref[6:7, sl] * has_prev, tails_ref[7:8, sl] * has_prev, row)
            w0, w1, w2 = cw_ref[0:1, sl], cw_ref[1:2, sl], cw_ref[2:3, sl]
            conv = cb_ref[:, sl] + w2 * cx + w0 * p2 + w1 * p1
            sz, dsz = _silu(z)
            dy = dy_scr[:, sl]
            y_ref[:, sl] = (sz * bg * conv).astype(BF16)
            dp_ref[:, 3 * e + c0:3 * e + c0 + cwid] = (dy * bg * conv * dsz).astype(BF16)
            dp_ref[:, c0:c0 + cwid] = (dy * sz * conv).astype(BF16)
            dconv = dy * sz * bg
            dcb_ref[:, sl] += _colsum(dconv)
            dcw_ref[2:3, sl] += _colsum(dconv * cx)
            dcw_ref[1:2, sl] += _colsum(dconv * p1)
            dcw_ref[0:1, sl] += _colsum(dconv * p2)
            h0, h1 = head_scr[0:1, sl], head_scr[1:2, sl]
            n1 = jnp.where(row == t - 1, h0, pltpu.roll(dconv, t - 1, 0))
            n2 = jnp.where(row == t - 2, h0, jnp.where(row == t - 1, h1, pltpu.roll(dconv, t - 2, 0)))
            dcx = w2 * dconv + w1 * n1 + w0 * n2
            dp_ref[:, e + c0:e + c0 + cwid] = (dcx * xin).astype(BF16)
            dp_ref[:, 2 * e + c0:2 * e + c0 + cwid] = (dcx * cg).astype(BF16)
            head_scr[:, sl] = dconv[0:8, :]

    rev = lambda i: (n_t - 1 - i, 0)
    return pl.pallas_call(
        body, name=name, grid=(n_t,),
        out_shape=(jax.ShapeDtypeStruct((s, e4), BF16), jax.ShapeDtypeStruct((s, e), BF16),
                   jax.ShapeDtypeStruct((1, d), F32), jax.ShapeDtypeStruct((1, e), F32), jax.ShapeDtypeStruct((8, e), F32)),
        in_specs=[pl.BlockSpec((t, e4), rev), pl.BlockSpec((t, d), rev), pl.BlockSpec((t, d), rev),
                  pl.BlockSpec((None, 8, e), lambda i: (jnp.maximum(n_t - 2 - i, 0), 0, 0)),
                  _full((3, e)), _full((1, e)), _full((1, d)), _full((e, d))],
        out_specs=(pl.BlockSpec((t, e4), rev), pl.BlockSpec((t, e), rev), _full((1, d)), _full((1, e)), _full((8, e))),
        scratch_shapes=[pltpu.VMEM((t, e), F32), pltpu.VMEM((8, e), F32)],
        compiler_params=_params(("arbitrary",)),
    )(proj, dx1, br, tails, cw, cb, gate, wo)


def _matmul_nt_norm_bwd(dproj, w, xin, dres, g, scale, name):
    s, d = xin.shape
    n = w.shape[1]
    tm = min(s, 512)
    n_i = s // tm

    def body(dp_ref, w_ref, x_ref, dres_ref, g_ref, sc_ref, dx_ref, dsh_ref, dsc_ref, dg_ref, p_scr):
        i = pl.program_id(0)

        @pl.when(i == 0)
        def _():
            dsh_ref[...] = jnp.zeros_like(dsh_ref)
            p_scr[...] = jnp.zeros_like(p_scr)
        dh = lax.dot_general(dp_ref[...], w_ref[...], NT_DIMS, preferred_element_type=F32)
        xv = x_ref[...]
        r = lax.rsqrt(jnp.mean(xv * xv, axis=-1, keepdims=True) + RMS_EPS)
        xn = xv * r
        dsh_ref[...] += _colsum(dh)
        p_scr[...] += _colsum(dh * xn)
        dxn = dh * (g_ref[...] * (1.0 + sc_ref[...]))
        dx_ref[...] = r * (dxn - xn * jnp.mean(dxn * xn, axis=-1, keepdims=True)) + dres_ref[...]

        @pl.when(i == n_i - 1)
        def _():
            dsc_ref[...] = p_scr[...] * g_ref[...]
            dg_ref[...] = p_scr[...] * (1.0 + sc_ref[...])

    tok = pl.BlockSpec((tm, d), lambda i: (i, 0))
    vec = pl.BlockSpec((1, d), lambda i: (0, 0))
    vshape = jax.ShapeDtypeStruct((1, d), F32)
    return pl.pallas_call(
        body, name=name, grid=(n_i,),
        out_shape=(jax.ShapeDtypeStruct((s, d), F32), vshape, vshape, vshape),
        in_specs=[pl.BlockSpec((tm, n), lambda i: (i, 0)),
                  pl.BlockSpec((d, n), lambda i: (0, 0), pipeline_mode=pl.Buffered(1)), tok, tok, vec, vec],
        out_specs=(tok, vec, vec, vec),
        scratch_shapes=[pltpu.VMEM((1, d), F32)],
        compiler_params=_params(("arbitrary",)),
    )(dproj, w, xin, dres, g, scale)


def _matmul_tn(a, b, colscale, rows_split, name):
    s, m = a.shape
    n = b.shape[1]
    n_j, tn = (1, n) if rows_split else (NDEV, n // NDEV)
    fixed = (4 + 4 + 2 * 2) * m * tn
    tk = s
    while fixed + 2 * tk * (2 * m + b.dtype.itemsize * tn) > TN_VMEM_BUDGET:
        tk //= 2
    n_k = s // tk

    def body(a_ref, b_ref, cs_ref, o_ref, acc):
        k = pl.program_id(1)
        part = lax.dot_general(a_ref[...], b_ref[...].astype(BF16), TN_DIMS, preferred_element_type=F32)
        if n_k == 1:
            o_ref[...] = (part * cs_ref[...]).astype(BF16)
            return

        @pl.when(k == 0)
        def _():
            acc[...] = part

        @pl.when((k > 0) & (k < n_k - 1))
        def _():
            acc[...] += part

        @pl.when(k == n_k - 1)
        def _():
            o_ref[...] = ((acc[...] + part) * cs_ref[...]).astype(BF16)

    out = pl.pallas_call(
        body, name=name, grid=(n_j, n_k),
        out_shape=jax.ShapeDtypeStruct((n_j, m, tn), BF16),
        in_specs=[pl.BlockSpec((tk, m), lambda j, k: (k, 0)), pl.BlockSpec((tk, tn), lambda j, k: (k, j)),
                  pl.BlockSpec((1, tn), lambda j, k: (0, j))],
        out_specs=pl.BlockSpec((None, m, tn), lambda j, k: (j, 0, 0)),
        scratch_shapes=[pltpu.VMEM((m, tn), F32)],
        compiler_params=_params(("parallel", "arbitrary")),
    )(a, b, colscale)
    return out.reshape(NDEV, m // NDEV, n) if rows_split else out


def _adam_update(w, g, m, v):
    m = ADAM_B1 * m + (1.0 - ADAM_B1) * g
    v = ADAM_B2 * v + (1.0 - ADAM_B2) * (g * g)
    m_hat = m / (1.0 - ADAM_B1 ** ADAM_STEP)
    v_hat = v / (1.0 - ADAM_B2 ** ADAM_STEP)
    return -ADAM_LR * (m_hat / (jnp.sqrt(v_hat) + ADAM_EPS) + ADAM_WD * w), m, v


def _adamw_reduce(parts, w, m, v, name):
    n_p, r, c = parts.shape
    tr = min(r, 256)

    def body(p_ref, w_ref, m_ref, v_ref, g_out, d_out, m_out, v_out):
        g = p_ref[0].astype(F32)
        for j in range(1, n_p):
            g = g + p_ref[j].astype(F32)
        g_out[...] = g
        d_out[...], m_out[...], v_out[...] = _adam_update(w_ref[...], g, m_ref[...], v_ref[...])

    blk = pl.BlockSpec((tr, c), lambda i: (i, 0))
    shp = jax.ShapeDtypeStruct((r, c), F32)
    return pl.pallas_call(
        body, name=name, grid=(r // tr,), out_shape=(shp, shp, shp, shp),
        in_specs=[pl.BlockSpec((n_p, tr, c), lambda i: (0, i, 0)), blk, blk, blk],
        out_specs=(blk, blk, blk, blk),
        compiler_params=_params(("parallel",)),
    )(parts, w, m, v)


def _adamw_small(gs, ws, ms, vs, name):
    n = len(gs)

    def body(*refs):
        ins, outs = refs[:4 * n], refs[4 * n:]
        for a in range(n):
            d, m, v = _adam_update(ins[n + a][...], ins[a][...], ins[2 * n + a][...], ins[3 * n + a][...])
            outs[a][...], outs[n + a][...], outs[2 * n + a][...] = d, m, v

    shapes = tuple(jax.ShapeDtypeStruct(w.shape, F32) for w in ws) * 3
    out = pl.pallas_call(
        body, name=name, out_shape=shapes,
        in_specs=[_vmem()] * (4 * n), out_specs=tuple([_vmem()] * (3 * n)),
        compiler_params=pltpu.CompilerParams(vmem_limit_bytes=VMEM_LIMIT),
    )(*gs, *ws, *ms, *vs)
    return out[:n], out[n:2 * n], out[2 * n:]


def _sum_devices(packed, name):
    _, r, wdt = packed.shape

    def body(p_ref, o_ref):
        acc = p_ref[0]
        for j in range(1, NDEV):
            acc = acc + p_ref[j]
        o_ref[...] = acc

    return pl.pallas_call(
        body, name=name, out_shape=jax.ShapeDtypeStruct((r, wdt), F32),
        in_specs=[_vmem()], out_specs=_vmem(),
        compiler_params=pltpu.CompilerParams(vmem_limit_bytes=VMEM_LIMIT),
    )(packed)


def _mod_w_grad(c_t, dmod, name):
    n_layers, _, w3 = dmod.shape
    d = c_t.shape[0]

    def body(c_ref, dm_ref, o_ref):
        for i in range(n_layers):
            acc = c_ref[:, 0:1] * dm_ref[i, 0:1, :]
            for b in range(1, NDEV):
                acc = acc + c_ref[:, b:b + 1] * dm_ref[i, b:b + 1, :]
            o_ref[i] = acc

    return pl.pallas_call(
        body, name=name, out_shape=jax.ShapeDtypeStruct((n_layers, d, w3), F32),
        in_specs=[_vmem(), _vmem()], out_specs=_vmem(),
        compiler_params=pltpu.CompilerParams(vmem_limit_bytes=VMEM_LIMIT),
    )(c_t, dmod)


def _mask_transpose_ws(w_s, name):
    def body(w_ref, wt_ref, wtt_ref):
        tril = (lax.broadcasted_iota(jnp.int32, (CHUNK, CHUNK), 0) >= lax.broadcasted_iota(jnp.int32, (CHUNK, CHUNK), 1))
        for g in range(GROUPS):
            wm = jnp.where(tril, w_ref[g], 0.0)
            wt_ref[g] = wm.astype(BF16)
            wtt_ref[g] = wm.T.astype(BF16)

    shp = jax.ShapeDtypeStruct(w_s.shape, BF16)
    return pl.pallas_call(
        body, name=name, out_shape=(shp, shp), in_specs=[_vmem()], out_specs=(_vmem(), _vmem()),
    )(w_s)


def _pack(pieces):
    flat = jnp.concatenate([p.reshape(-1) for p in pieces])
    rows = -(-flat.shape[0] // (8 * PACK_W)) * 8
    return jnp.pad(flat, (0, rows * PACK_W - flat.shape[0])).reshape(rows, PACK_W)


def _unpack(flat, shapes):
    out, off = [], 0
    for shp in shapes:
        size = 1
        for dim in shp:
            size *= dim
        out.append(flat[off:off + size].reshape(shp))
        off += size
    return out


def kernel(x, c, mod_w, mod_b, norm_g, a_w_in, a_conv_w, a_conv_b, a_w_out, b_w_in, b_ln_g, b_ln_b, b_w_s, b_b_s, b_w_out, final_g, loss_target, m_mod_w, m_mod_b, m_norm_g, m_a_w_in, m_a_conv_w, m_a_conv_b, m_a_w_out, m_b_w_in, m_b_ln_g, m_b_ln_b, m_b_w_s, m_b_b_s, m_b_w_out, m_final_g, v_mod_w, v_mod_b, v_norm_g, v_a_w_in, v_a_conv_w, v_a_conv_b, v_a_w_out, v_b_w_in, v_b_ln_g, v_b_ln_b, v_b_w_s, v_b_b_s, v_b_w_out, v_final_g):
    s, d = x.shape[1], x.shape[2]
    es = a_w_out.shape[1]
    e = NDEV * es
    w3 = mod_w.shape[2]
    me = _index(_pos())
    x0 = x.reshape(s, d)
    tgt = loss_target.reshape(s, d)

    small = jnp.concatenate([a_conv_w[0], b_ln_g, b_ln_b, jnp.zeros((3, es), F32)], axis=0)
    wa, woa, wb, wob, small_all = _all_gather(
        [a_w_in[0].astype(BF16), a_w_out[0].astype(BF16), b_w_in[0].astype(BF16), b_w_out[0].astype(BF16), small],
        [True, False, True, False, True], "gather_weights")
    conv_w, ln_g, ln_b = small_all[0:3], small_all[3:4], small_all[4:5]
    bsf = jnp.repeat(b_b_s[0].T, e // GROUPS, axis=1)
    wt, wtt = _mask_transpose_ws(b_w_s[0], "mask_w_s")

    mod, c_all = _mod_vectors(c, mod_w, mod_b)
    shift0, scale0, gate0 = mod[0:1, 0:d], mod[0:1, d:2 * d], mod[0:1, 2 * d:]
    shift1, scale1, gate1 = mod[1:2, 0:d], mod[1:2, d:2 * d], mod[1:2, 2 * d:]
    g0, g1, fg = norm_g[0:1], norm_g[1:2], final_g.reshape(1, d)

    proj_a, h0 = _norm_mod_matmul(x0, g0, scale0, shift0, wa, "a_in_proj")
    x1, br_a, tails = _conv_mixer_fwd(proj_a, x0, conv_w, a_conv_b, gate0, woa, "a_mixer_fwd")
    proj_b, h1 = _norm_mod_matmul(x1, g1, scale1, shift1, wb, "b_in_proj")
    dx2, loss_acc, dfg, dgate1 = _gmlp_fwd_loss(proj_b, x1, tgt, gate1, fg, ln_g, ln_b, wt, bsf, wob, "b_mixer_fwd_loss")
    loss = lax.psum(loss_acc[0, 0], ("x", "y", "c"))

    dproj_b, y_b, dws, dbs, dlg, dlb = _gmlp_bwd(proj_b, dx2, gate1, ln_g, ln_b, wt, wtt, bsf, wob, "b_mixer_bwd")
    gs_b_out = _matmul_tn(y_b, dx2, gate1, True, "b_w_out_grad")
    dx1, dshift1, dscale1, dg1 = _matmul_nt_norm_bwd(dproj_b, wb, x1, dx2, g1, scale1, "b_in_bwd")
    gs_b_in = _matmul_tn(h1, dproj_b, jnp.ones((1, dproj_b.shape[1]), F32), False, "b_w_in_grad")
    dproj_a, y_a, dgate0, dcb, dcw = _conv_mixer_bwd(proj_a, dx1, br_a, tails, conv_w, a_conv_b, gate0, woa, "a_mixer_bwd")
    gs_a_out = _matmul_tn(y_a, dx1, gate0, True, "a_w_out_grad")
    dx0, dshift0, dscale0, dg0 = _matmul_nt_norm_bwd(dproj_a, wa, x0, dx1, g0, scale0, "a_in_bwd")
    gs_a_in = _matmul_tn(h0, dproj_a, jnp.ones((1, dproj_a.shape[1]), F32), False, "a_w_in_grad")

    gr_a_in, gr_a_out, gr_b_in, gr_b_out = _scatter_to_owners([gs_a_in, gs_a_out, gs_b_in, gs_b_out], "scatter_grads")
    pieces = [dshift0, dscale0, dgate0, dshift1, dscale1, dgate1, dg0, dg1, dcb, dcw[0:3], dlg, dlb, dfg,
              dbs[:, 0:GROUPS].T, dws]
    shapes = [p.shape for p in pieces]
    packed = _pack(pieces)
    packed_all, = _all_gather([packed], [False], "gather_small_grads")
    packed_all = packed_all.reshape(NDEV, -1, PACK_W)
    total =_sum_devices(packed_all, "sum_small_grads").reshape(-1)
    (t_sh0, t_sc0, t_ga0, t_sh1, t_sc1, t_ga1, t_g0, t_g1, t_cb, t_cw, t_lg, t_lb, t_fg, t_bs, t_ws) = _unpack(total, shapes)
    grad_mod_b = jnp.concatenate([jnp.concatenate([t_sh0, t_sc0, t_ga0], axis=1),
                                  jnp.concatenate([t_sh1, t_sc1, t_ga1], axis=1)], axis=0)
    grad_norm_g = jnp.concatenate([t_g0, t_g1], axis=0)
    dmod_all = packed_all.reshape(NDEV, -1)[:, 0:6 * d].reshape(NDEV, 2, 3 * d).transpose(1, 0, 2)
    dmod_mine = lax.dynamic_slice_in_dim(dmod_all, me * w3, w3, axis=2)
    grad_mod_w = _mod_w_grad(c_all.T, dmod_mine, "mod_w_grad")
    grad_a_conv_w = lax.dynamic_slice_in_dim(t_cw, me * es, es, axis=1)
    grad_b_ln_g = lax.dynamic_slice_in_dim(t_lg, me * es, es, axis=1)
    grad_b_ln_b = lax.dynamic_slice_in_dim(t_lb, me * es, es, axis=1)

    def big(parts, w, m, v, name):
        shp = w.shape
        r2 = lambda t_: t_.reshape(-1, shp[-1])
        g, dl, nm, nv = _adamw_reduce(parts, r2(w), r2(m), r2(v), name)
        return tuple(t_.reshape(shp) for t_ in (g, dl, nm, nv))

    res = {}
    res["mod_w"] = big(grad_mod_w.reshape(1, -1, w3), mod_w, m_mod_w, v_mod_w, "adamw_mod_w")
    res["a_w_in"] = big(gr_a_in, a_w_in, m_a_w_in, v_a_w_in, "adamw_a_w_in")
    res["a_w_out"] = big(gr_a_out, a_w_out, m_a_w_out, v_a_w_out, "adamw_a_w_out")
    res["b_w_in"] = big(gr_b_in, b_w_in, m_b_w_in, v_b_w_in, "adamw_b_w_in")
    res["b_w_out"] = big(gr_b_out, b_w_out, m_b_w_out, v_b_w_out, "adamw_b_w_out")

    small_names = ["mod_b", "norm_g", "a_conv_w", "a_conv_b", "b_ln_g", "b_ln_b", "b_w_s", "b_b_s", "final_g"]
    small_g = [grad_mod_b, grad_norm_g, grad_a_conv_w, t_cb, grad_b_ln_g, grad_b_ln_b, t_ws, t_bs, t_fg]
    small_w = [mod_b, norm_g, a_conv_w, a_conv_b, b_ln_g, b_ln_b, b_w_s, b_b_s, final_g]
    small_m = [m_mod_b, m_norm_g, m_a_conv_w, m_a_conv_b, m_b_ln_g, m_b_ln_b, m_b_w_s, m_b_b_s, m_final_g]
    small_v = [v_mod_b, v_norm_g, v_a_conv_w, v_a_conv_b, v_b_ln_g, v_b_ln_b, v_b_w_s, v_b_b_s, v_final_g]
    as2d = lambda t_: t_.reshape(-1, t_.shape[-1])
    dls, nms, nvs = _adamw_small([as2d(t_) for t_ in small_g], [as2d(t_) for t_ in small_w],
                                 [as2d(t_) for t_ in small_m], [as2d(t_) for t_ in small_v], "adamw_small")
    for a, nme in enumerate(small_names):
        shp = small_w[a].shape
        res[nme] = (small_g[a].reshape(shp), dls[a].reshape(shp), nms[a].reshape(shp), nvs[a].reshape(shp))

    order = ["mod_w", "mod_b", "norm_g", "a_w_in", "a_conv_w", "a_conv_b", "a_w_out", "b_w_in", "b_ln_g", "b_ln_b",
             "b_w_s", "b_b_s", "b_w_out", "final_g"]
    return (loss, dx0.reshape(x.shape), *[res[k][0] for k in order], *[res[k][1] for k in order],
            *[res[k][2] for k in order], *[res[k][3] for k in order])
```

```python
import functools

import jax
import jax.numpy as jnp
from jax import lax
from jax.experimental import pallas as pl
from jax.experimental.pallas import tpu as pltpu

NDEV = 8
CHUNK = 128
GROUPS = 8
RMS_EPS = 1e-6
LN_EPS = 1e-5
ADAM_LR, ADAM_B1, ADAM_B2, ADAM_EPS, ADAM_WD, ADAM_STEP = 0.001, 0.9, 0.999, 1e-08, 0.01, 10
V7X_VMEM_BYTES = 64 * 1024 * 1024
VMEM_LIMIT = V7X_VMEM_BYTES - 8 * 1024 * 1024
TN_VMEM_BUDGET = 46 * 1024 * 1024
PACK_W = 1024
F32, BF16 = jnp.float32, jnp.bfloat16
MESH = pl.DeviceIdType.MESH
RSQRT2 = 0.7071067811865476
INV_SQRT_2PI = 0.3989422804014327
NT_DIMS = (((1,), (1,)), ((), ()))
TN_DIMS = (((0,), (0,)), ((), ()))


def _params(sem=None):
    return pltpu.CompilerParams(dimension_semantics=sem, vmem_limit_bytes=VMEM_LIMIT)


def _vmem():
    return pl.BlockSpec(memory_space=pltpu.VMEM)


def _hbm():
    return pl.BlockSpec(memory_space=pltpu.HBM)


def _full(shape):
    return pl.BlockSpec(shape, lambda *_: (0,) * len(shape))


def _pos():
    return lax.axis_index("x"), lax.axis_index("y"), lax.axis_index("c")


def _index(p):
    return 4 * p[0] + 2 * p[1] + p[2]


def _peer(k):
    x, y, c = _pos()
    return ((1 - x) if (k >> 2) & 1 else x, (1 - y) if (k >> 1) & 1 else y, (1 - c) if k & 1 else c)


def _silu(z):
    sg = jax.nn.sigmoid(z)
    return z * sg, sg * (1.0 + z * (1.0 - sg))


def _gelu(v):
    phi = 0.5 * (1.0 + lax.erf(v * RSQRT2))
    return v * phi, phi + v * (jnp.exp(-0.5 * v * v) * INV_SQRT_2PI)


def _colsum(v):
    return jnp.sum(v, axis=0, keepdims=True)


def _rowsum(v):
    return jnp.sum(v, axis=-1, keepdims=True)


def _gather_all_vmem(slab_ref, send_sems, recv_sems, base):
    me = _index(_pos())
    sends = []
    for k in range(1, NDEV):
        cp = pltpu.make_async_remote_copy(
            src_ref=slab_ref.at[me], dst_ref=slab_ref.at[me],
            send_sem=send_sems.at[base + k - 1], recv_sem=recv_sems.at[base + k - 1],
            device_id=_peer(k), device_id_type=MESH)
        cp.start()
        sends.append(cp)
    for k in range(1, NDEV):
        src = _index(_peer(k))
        pltpu.make_async_remote_copy(
            src_ref=slab_ref.at[src], dst_ref=slab_ref.at[src],
            send_sem=send_sems.at[base + k - 1], recv_sem=recv_sems.at[base + k - 1],
            device_id=_peer(k), device_id_type=MESH).wait_recv()
    for cp in sends:
        cp.wait_send()


def _mod_vectors(c, mod_w, mod_b):
    n_layers, d, w3 = mod_w.shape

    def body(c_ref, mw_ref, mb_ref, mod_ref, call_ref, cslab, pslab, send_sems, recv_sems):
        me = _index(_pos())
        cv = c_ref[...]
        cslab[me] = jnp.broadcast_to(cv * jax.nn.sigmoid(cv), (8, d))
        _gather_all_vmem(cslab, send_sems, recv_sems, 0)
        c_all = jnp.concatenate([cslab[k, 0:1, :] for k in range(NDEV)], axis=0)
        call_ref[...] = c_all
        for i in range(n_layers):
            pslab[me, i * NDEV:(i + 1) * NDEV, :] = jnp.dot(
                c_all, mw_ref[i], preferred_element_type=F32, precision=lax.Precision.HIGHEST)
        _gather_all_vmem(pslab, send_sems, recv_sems, NDEV - 1)
        for i in range(n_layers):
            for k in range(NDEV):
                mod_ref[i:i + 1, k * w3:(k + 1) * w3] = (
                    pslab[k, pl.ds(i * NDEV + me, 1), :] + mb_ref[i:i + 1, k * w3:(k + 1) * w3])

    return pl.pallas_call(
        body, name="mod_vectors",
        out_shape=(jax.ShapeDtypeStruct((n_layers, 3 * d), F32), jax.ShapeDtypeStruct((NDEV, d), F32)),
        in_specs=[_vmem(), _vmem(), _vmem()], out_specs=(_vmem(), _vmem()),
        scratch_shapes=[pltpu.VMEM((NDEV, 8, d), F32), pltpu.VMEM((NDEV, n_layers * NDEV, w3), F32),
                        pltpu.SemaphoreType.DMA((2 * (NDEV - 1),)), pltpu.SemaphoreType.DMA((2 * (NDEV - 1),))],
        compiler_params=pltpu.CompilerParams(vmem_limit_bytes=VMEM_LIMIT),
    )(c, mod_w, mod_b)


class _Exchange:
    def __init__(self, arrays, out_shapes, sems, start, middle, finish):
        self.arrays, self.out_shapes, self.sems = list(arrays), list(out_shapes), list(sems)
        self.start, self.middle, self.finish = start, middle, finish


def _gather_exchange(shards, by_cols):
    n = len(shards)
    shapes = [sh.shape for sh in shards]

    def tools(ins, outs, sems):
        send_sems, recv_sems, local_sems = sems
        x, y, c = _pos()
        chips = [(1 - x, y), (x, 1 - y), (1 - x, 1 - y)]

        def place(a, block):
            r, cc = shapes[a]
            if by_cols[a]:
                return outs[a].at[:, pl.ds(_index(block) * cc, cc)]
            return outs[a].at[pl.ds(_index(block) * r, r), :]

        def copy(a, k, block, to, src=None):
            dst = place(a, block)
            return pltpu.make_async_remote_copy(
                src_ref=dst if src is None else src, dst_ref=dst,
                send_sem=send_sems.at[a * 7 + k], recv_sem=recv_sems.at[a * 7 + k],
                device_id=to, device_id_type=MESH)

        mine = [pltpu.make_async_copy(ins[a], place(a, (x, y, c)), local_sems.at[a]) for a in range(n)]
        first = []
        for a in range(n):
            first.append(copy(a, 0, (x, y, c), (x, y, 1 - c), src=ins[a]))
            first += [copy(a, 1 + j, (x, y, c), (*chip, c), src=ins[a]) for j, chip in enumerate(chips)]
        passed = [copy(a, 4 + j, (*chip, c), (x, y, 1 - c)) for j, chip in enumerate(chips) for a in range(n)]
        return (x, y, c), chips, copy, mine, first, passed

    def start(ins, outs, sems):
        _, _, _, mine, first, _ = tools(ins, outs, sems)
        for cp in mine + first:
            cp.start()

    def middle(ins, outs, sems):
        (x, y, c), chips, copy, _, _, passed = tools(ins, outs, sems)
        for j, chip in enumerate(chips):
            for a in range(n):
                copy(a, 1 + j, (*chip, c), (x, y, c)).wait_recv()
                passed[j * n + a].start()

    def finish(ins, outs, sems):
        (x, y, c), chips, copy, mine, first, passed = tools(ins, outs, sems)
        for a in range(n):
            copy(a, 0, (x, y, 1 - c), (x, y, c)).wait_recv()
        for j, chip in enumerate(chips):
            for a in range(n):
                copy(a, 4 + j, (*chip, 1 - c), (x, y, c)).wait_recv()
        for cp in first + passed:
            cp.wait_send()
        for cp in mine:
            cp.wait()

    out_shapes = [jax.ShapeDtypeStruct((r, NDEV * cc) if bc else (NDEV * r, cc), sh.dtype)
                  for (r, cc), bc, sh in zip(shapes, by_cols, shards)]
    sems = [pltpu.SemaphoreType.DMA((7 * n,)), pltpu.SemaphoreType.DMA((7 * n,)), pltpu.SemaphoreType.DMA((n,))]
    return _Exchange(shards, out_shapes, sems, start, middle, finish)


def _scatter_exchange(parts):
    n = len(parts)

    def tools(ins, outs, sems):
        send_sems, recv_sems, local_sems = sems
        me = _index(_pos())
        mine = [pltpu.make_async_copy(ins[a].at[me], outs[a].at[me], local_sems.at[a]) for a in range(n)]
        sends, arrivals = [], []
        for k in range(1, NDEV):
            peer = _peer(k)
            for a in range(n):
                pair = dict(send_sem=send_sems.at[a * 7 + k - 1], recv_sem=recv_sems.at[a * 7 + k - 1],
                            device_id=peer, device_id_type=MESH)
                sends.append(pltpu.make_async_remote_copy(src_ref=ins[a].at[_index(peer)], dst_ref=outs[a].at[me], **pair))
                slot = outs[a].at[_index(peer)]
                arrivals.append(pltpu.make_async_remote_copy(src_ref=slot, dst_ref=slot, **pair))
        return mine, sends, arrivals

    def start(ins, outs, sems):
        mine, sends, _ = tools(ins, outs, sems)
        for cp in mine + sends:
            cp.start()

    def finish(ins, outs, sems):
        mine, sends, arrivals = tools(ins, outs, sems)
        for cp in arrivals:
            cp.wait_recv()
        for cp in sends:
            cp.wait_send()
        for cp in mine:
            cp.wait()

    out_shapes = [jax.ShapeDtypeStruct(p.shape, p.dtype) for p in parts]
    sems = [pltpu.SemaphoreType.DMA((7 * n,)), pltpu.SemaphoreType.DMA((7 * n,)), pltpu.SemaphoreType.DMA((n,))]
    return _Exchange(parts, out_shapes, sems, start, None, finish)


def _exchange_alone(ex, name):
    n_in, n_out = len(ex.arrays), len(ex.out_shapes)

    def body(*refs):
        ins, outs, sems = refs[:n_in], refs[n_in:n_in + n_out], refs[n_in + n_out:]
        ex.start(ins, outs, sems)
        if ex.middle is not None:
            ex.middle(ins, outs, sems)
        ex.finish(ins, outs, sems)

    return pl.pallas_call(
        body, name=name, out_shape=tuple(ex.out_shapes),
        in_specs=[_hbm()] * n_in, out_specs=tuple([_hbm()] * n_out), scratch_shapes=ex.sems,
    )(*ex.arrays)


def _carry(ex, body, n_in, n_out, first, middle, last):
    if ex is None:
        return body
    r_in, r_out = len(ex.arrays), len(ex.out_shapes)

    def wrapped(*refs):
        ins, rins = refs[:n_in], refs[n_in:n_in + r_in]
        outs = refs[n_in + r_in:n_in + r_in + n_out]
        routs = refs[n_in + r_in + n_out:n_in + r_in + n_out + r_out]
        rest = refs[n_in + r_in + n_out + r_out:]
        scratch, sems = rest[:len(rest) - len(ex.sems)], rest[len(rest) - len(ex.sems):]

        @pl.when(first())
        def _():
            ex.start(rins, routs, sems)

        if ex.middle is not None:
            @pl.when(middle())
            def _():
                ex.middle(rins, routs, sems)

        body(*ins, *outs, *scratch)

        @pl.when(last())
        def _():
            ex.finish(rins, routs, sems)

    return wrapped


def _carried(ex):
    if ex is None:
        return [], [], [], [], []
    return ex.arrays, [_hbm()] * len(ex.arrays), ex.out_shapes, [_hbm()] * len(ex.out_shapes), ex.sems


def _norm_mod_matmul(x, g, scale, shift, w, name, ex=None):
    s, d = x.shape
    nb = w.shape[1] // NDEV
    tm = min(s, 1024)
    n_i = s // tm

    def body(x_ref, g_ref, sc_ref, sh_ref, w_ref, proj_ref, h_ref):
        @pl.when(pl.program_id(1) == 0)
        def _():
            xv = x_ref[...]
            r = lax.rsqrt(jnp.mean(xv * xv, axis=-1, keepdims=True) + RMS_EPS)
            h_ref[...] = ((xv * r) * g_ref[...] * (1.0 + sc_ref[...]) + sh_ref[...]).astype(BF16)
        proj_ref[...] = jnp.dot(h_ref[...], w_ref[...], preferred_element_type=F32).astype(BF16)

    at = lambda i, j: (pl.program_id(0) == i) & (pl.program_id(1) == j)
    body = _carry(ex, body, 5, 2, lambda: at(0, 0), lambda: at((5 * n_i) // 8, 0), lambda: at(n_i - 1, NDEV - 1))
    ex_args, ex_in, ex_shapes, ex_out, ex_sems = _carried(ex)
    vec = pl.BlockSpec((1, d), lambda i, j: (0, 0))
    out = pl.pallas_call(
        body, name=name, grid=(n_i, NDEV),
        out_shape=(jax.ShapeDtypeStruct((s, NDEV * nb), BF16), jax.ShapeDtypeStruct((s, d), BF16), *ex_shapes),
        in_specs=[pl.BlockSpec((tm, d), lambda i, j: (i, 0)), vec, vec, vec,
                  pl.BlockSpec((d, nb), lambda i, j: (0, j)), *ex_in],
        out_specs=(pl.BlockSpec((tm, nb), lambda i, j: (i, j)), pl.BlockSpec((tm, d), lambda i, j: (i, 0)), *ex_out),
        scratch_shapes=ex_sems,
        compiler_params=_params(("arbitrary", "arbitrary")),
    )(x, g, scale, shift, w, *ex_args)
    return out[0], out[1], out[2:]


def _conv_taps(cx, t6, t7, row):
    p1 = jnp.where(row == 0, t7, pltpu.roll(cx, 1, 0))
    p2 = jnp.where(row == 0, t6, jnp.where(row == 1, t7, pltpu.roll(cx, 2, 0)))
    return p1, p2


def _conv_mixer_fwd(proj, x, cw, cb, gate, wo, name):
    s, e4 = proj.shape
    e = e4 // 4
    d = x.shape[1]
    t = min(s, 256)
    cwid = min(e, 512)

    def body(p_ref, x_ref, cw_ref, cb_ref, gate_ref, wo_ref, x1_ref, br_ref, tails_ref, y_scr, tail_scr):
        @pl.when(pl.program_id(0) == 0)
        def _():
            tail_scr[...] = jnp.zeros_like(tail_scr)
        row = lax.broadcasted_iota(jnp.int32, (t, cwid), 0)
        for c0 in range(0, e, cwid):
            sl = slice(c0, c0 + cwid)
            bg = p_ref[:, c0:c0 + cwid].astype(F32)
            cx = p_ref[:, e + c0:e + c0 + cwid].astype(F32) * p_ref[:, 2 * e + c0:2 * e + c0 + cwid].astype(F32)
            z = p_ref[:, 3 * e + c0:3 * e + c0 + cwid].astype(F32)
            p1, p2 = _conv_taps(cx, tail_scr[6:7, sl], tail_scr[7:8, sl], row)
            conv = cb_ref[:, sl] + cw_ref[2:3, sl] * cx + cw_ref[0:1, sl] * p2 + cw_ref[1:2, sl] * p1
            y_scr[:, sl] = (_silu(z)[0] * bg * conv).astype(BF16)
            tail_scr[:, sl] = cx[t - 8:t, :]
        tails_ref[...] = tail_scr[...]
        br = jnp.dot(y_scr[...], wo_ref[...], preferred_element_type=F32)
        x1_ref[...] = x_ref[...] + gate_ref[...] * br
        br_ref[...] = br.astype(BF16)

    return pl.pallas_call(
        body, name=name, grid=(s // t,),
        out_shape=(jax.ShapeDtypeStruct((s, d), F32), jax.ShapeDtypeStruct((s, d), BF16),
                   jax.ShapeDtypeStruct((s // t, 8, e), F32)),
        in_specs=[pl.BlockSpec((t, e4), lambda i: (i, 0)), pl.BlockSpec((t, d), lambda i: (i, 0)),
                  _full((3, e)), _full((1, e)), _full((1, d)), _full((e, d))],
        out_specs=(pl.BlockSpec((t, d), lambda i: (i, 0)), pl.BlockSpec((t, d), lambda i: (i, 0)),
                   pl.BlockSpec((None, 8, e), lambda i: (i, 0, 0))),
        scratch_shapes=[pltpu.VMEM((t, e), BF16), pltpu.VMEM((8, e), F32)],
        compiler_params=_params(("arbitrary",)),
    )(proj, x, cw, cb, gate, wo)


def _ln_stats(p_ref, v_scr, dgv_scr, t, e):
    gw = e // GROUPS
    s1 = jnp.zeros((t, 1), F32)
    for g in range(GROUPS):
        v, dgv = _gelu(p_ref[:, e + g * gw:e + (g + 1) * gw].astype(F32))
        v_scr[:, g * gw:(g + 1) * gw] = v
        if dgv_scr is not None:
            dgv_scr[:, g * gw:(g + 1) * gw] = dgv
        s1 = s1 + _rowsum(v)
    mu = s1 * (1.0 / e)
    s2 = jnp.zeros((t, 1), F32)
    for g in range(GROUPS):
        dv = v_scr[:, g * gw:(g + 1) * gw] - mu
        s2 = s2 + _rowsum(dv * dv)
    return mu, lax.rsqrt(s2 * (1.0 / e) + LN_EPS)


def _gmlp_fwd_loss(proj, x1, tgt, gate, fg, lng, lnb, wt, bsf, wo, name):
    s, e3 = proj.shape
    e = e3 // 3
    d = x1.shape[1]
    gw = e // GROUPS
    t = min(s, 256)

    def body(p_ref, x1_ref, tgt_ref, gate_ref, fg_ref, lng_ref, lnb_ref, wt_ref, bsf_ref, wo_ref,
             dx2_ref, loss_ref, dfg_ref, dgate_ref, v_scr, y_scr):
        @pl.when(pl.program_id(0) == 0)
        def _():
            loss_ref[...] = jnp.zeros_like(loss_ref)
            dfg_ref[...] = jnp.zeros_like(dfg_ref)
            dgate_ref[...] = jnp.zeros_like(dgate_ref)
        mu, rs = _ln_stats(p_ref, v_scr, None, t, e)
        for g in range(GROUPS):
            gs = slice(g * gw, (g + 1) * gw)
            vn = (((v_scr[:, gs] - mu) * rs) * lng_ref[:, gs] + lnb_ref[:, gs]).astype(BF16)
            for ch in range(t // CHUNK):
                rows = slice(ch * CHUNK, (ch + 1) * CHUNK)
                mixed = jnp.dot(wt_ref[g], vn[rows], preferred_element_type=F32) + bsf_ref[:, gs]
                u = _gelu(p_ref[rows, g * gw:(g + 1) * gw].astype(F32))[0]
                sz = _silu(p_ref[rows, 2 * e + g * gw:2 * e + (g + 1) * gw].astype(F32))[0]
                y_scr[rows, gs] = (sz * (u * mixed)).astype(BF16)
        br = jnp.dot(y_scr[...], wo_ref[...], preferred_element_type=F32)
        x2 = x1_ref[...] + gate_ref[...] * br
        r2 = lax.rsqrt(jnp.mean(x2 * x2, axis=-1, keepdims=True) + RMS_EPS)
        xn = x2 * r2
        diff = xn * fg_ref[...] - tgt_ref[...]
        loss_ref[...] += jnp.broadcast_to(0.5 * _colsum(jnp.mean(diff * diff, axis=-1, keepdims=True)), loss_ref.shape)
        dout = diff * (1.0 / d)
        dfg_ref[...] += _colsum(dout * xn)
        dxn = dout * fg_ref[...]
        dx2 = r2 * (dxn - xn * jnp.mean(dxn * xn, axis=-1, keepdims=True))
        dx2_ref[...] = dx2
        dgate_ref[...] += _colsum(dx2 * br)

    tok = pl.BlockSpec((t, d), lambda i: (i, 0))
    return pl.pallas_call(
        body, name=name, grid=(s // t,),
        out_shape=(jax.ShapeDtypeStruct((s, d), F32), jax.ShapeDtypeStruct((8, 128), F32),
                   jax.ShapeDtypeStruct((1, d), F32), jax.ShapeDtypeStruct((1, d), F32)),
        in_specs=[pl.BlockSpec((t, e3), lambda i: (i, 0)), tok, tok, _full((1, d)), _full((1, d)),
                  _full((1, e)), _full((1, e)), _full((GROUPS, CHUNK, CHUNK)), _full((CHUNK, e)), _full((e, d))],
        out_specs=(tok, _full((8, 128)), _full((1, d)), _full((1, d))),
        scratch_shapes=[pltpu.VMEM((t, e), F32), pltpu.VMEM((t, e), BF16)],
        compiler_params=_params(("arbitrary",)),
    )(proj, x1, tgt, gate, fg, lng, lnb, wt, bsf, wo)


def _gmlp_bwd(proj, dx2, gate, lng, lnb, wt, wtt, bsf, wo, name):
    s, e3 = proj.shape
    e = e3 // 3
    d = dx2.shape[1]
    gw = e // GROUPS
    t = min(s, 256)
    n_t = s // t

    def body(p_ref, dx_ref, gate_ref, lng_ref, lnb_ref, wt_ref, wtt_ref, bsf_ref, wo_ref,
             dp_ref, y_ref, dws_ref, dbs_ref, dlg_ref, dlb_ref, v_scr, dgv_scr, dy_scr, dvn_scr, dbs_scr):
        @pl.when(pl.program_id(0) == 0)
        def _():
            dws_ref[...] = jnp.zeros_like(dws_ref)
            dlg_ref[...] = jnp.zeros_like(dlg_ref)
            dlb_ref[...] = jnp.zeros_like(dlb_ref)
            dbs_scr[...] = jnp.zeros_like(dbs_scr)
        dbr = (dx_ref[...] * gate_ref[...]).astype(BF16)
        dy_scr[...] = lax.dot_general(dbr, wo_ref[...], NT_DIMS, preferred_element_type=F32)
        mu, rs = _ln_stats(p_ref, v_scr, dgv_scr, t, e)
        tril = (lax.broadcasted_iota(jnp.int32, (CHUNK, CHUNK), 0) >= lax.broadcasted_iota(jnp.int32, (CHUNK, CHUNK), 1))
        c1 = jnp.zeros((t, 1), F32)
        c2 = jnp.zeros((t, 1), F32)
        for g in range(GROUPS):
            gs = slice(g * gw, (g + 1) * gw)
            vhat = (v_scr[:, gs] - mu) * rs
            lg = lng_ref[:, gs]
            vn = (vhat * lg + lnb_ref[:, gs]).astype(BF16)
            for ch in range(t // CHUNK):
                rows = slice(ch * CHUNK, (ch + 1) * CHUNK)
                mixed = jnp.dot(wt_ref[g], vn[rows], preferred_element_type=F32) + bsf_ref[:, gs]
                u, dgu = _gelu(p_ref[rows, g * gw:(g + 1) * gw].astype(F32))
                sz, dsz = _silu(p_ref[rows, 2 * e + g * gw:2 * e + (g + 1) * gw].astype(F32))
                sgate = u * mixed
                y_ref[rows, gs] = (sz * sgate).astype(BF16)
                dy = dy_scr[rows, gs]
                dp_ref[rows, 2 * e + g * gw:2 * e + (g + 1) * gw] = (dy * sgate * dsz).astype(BF16)
                ds = dy * sz
                dp_ref[rows, gs] = (ds * mixed * dgu).astype(BF16)
                dm = ds * u
                dbs_scr[:, gs] += dm
                dmb = dm.astype(BF16)
                dws_ref[g] += jnp.where(tril, lax.dot_general(dmb, vn[rows], NT_DIMS, preferred_element_type=F32), 0.0)
                dvn_scr[rows, gs] = jnp.dot(wtt_ref[g], dmb, preferred_element_type=F32)
            dvn = dvn_scr[:, gs]
            dlb_ref[:, gs] += _colsum(dvn)
            dlg_ref[:, gs] += _colsum(dvn * vhat)
            dvh = dvn * lg
            c1 = c1 + _rowsum(dvh)
            c2 = c2 + _rowsum(dvh * vhat)
        c1 = c1 * (1.0 / e)
        c2 = c2 * (1.0 / e)
        for g in range(GROUPS):
            gs = slice(g * gw, (g + 1) * gw)
            vhat = (v_scr[:, gs] - mu) * rs
            dv = rs * (dvn_scr[:, gs] * lng_ref[:, gs] - c1 - vhat * c2)
            dp_ref[:, e + g * gw:e + (g + 1) * gw] = (dv * dgv_scr[:, gs]).astype(BF16)

        @pl.when(pl.program_id(0) == n_t - 1)
        def _():
            lane = lax.broadcasted_iota(jnp.int32, (CHUNK, 128), 1)
            acc = jnp.zeros((CHUNK, 128), F32)
            for g in range(GROUPS):
                acc = acc + jnp.where(lane == g, _rowsum(dbs_scr[:, g * gw:(g + 1) * gw]), 0.0)
            dbs_ref[...] = acc

    tok = pl.BlockSpec((t, d), lambda i: (i, 0))
    return pl.pallas_call(
        body, name=name, grid=(n_t,),
        out_shape=(jax.ShapeDtypeStruct((s, e3), BF16), jax.ShapeDtypeStruct((s, e), BF16),
                   jax.ShapeDtypeStruct((GROUPS, CHUNK, CHUNK), F32), jax.ShapeDtypeStruct((CHUNK, 128), F32),
                   jax.ShapeDtypeStruct((1, e), F32), jax.ShapeDtypeStruct((1, e), F32)),
        in_specs=[pl.BlockSpec((t, e3), lambda i: (i, 0)), tok, _full((1, d)), _full((1, e)), _full((1, e)),
                  _full((GROUPS, CHUNK, CHUNK)), _full((GROUPS, CHUNK, CHUNK)), _full((CHUNK, e)), _full((e, d))],
        out_specs=(pl.BlockSpec((t, e3), lambda i: (i, 0)), pl.BlockSpec((t, e), lambda i: (i, 0)),
                   _full((GROUPS, CHUNK, CHUNK)), _full((CHUNK, 128)), _full((1, e)), _full((1, e))),
        scratch_shapes=[pltpu.VMEM((t, e), F32), pltpu.VMEM((t, e), F32), pltpu.VMEM((t, e), F32),
                        pltpu.VMEM((t, e), F32), pltpu.VMEM((CHUNK, e), F32)],
        compiler_params=_params(("arbitrary",)),
    )(proj, dx2, gate, lng, lnb, wt, wtt, bsf, wo)


def _conv_mixer_bwd(proj, dx1, br, tails, cw, cb, gate, wo, name, ex=None):
    s, e4 = proj.shape
    e = e4 // 4
    d = dx1.shape[1]
    t = min(s, 256)
    n_t = s // t
    cwid = min(e, 512)

    def body(p_ref, dx_ref, br_ref, tails_ref, cw_ref, cb_ref, gate_ref, wo_ref,
             dp_ref, y_ref, dgate_ref, dcb_ref, dcw_ref, dy_scr, head_scr):
        i = pl.program_id(0)

        @pl.when(i == 0)
        def _():
            dgate_ref[...] = jnp.zeros_like(dgate_ref)
            dcb_ref[...] = jnp.zeros_like(dcb_ref)
            dcw_ref[...] = jnp.zeros_like(dcw_ref)
            head_scr[...] = jnp.zeros_like(head_scr)
        dx = dx_ref[...]
        dgate_ref[...] += _colsum(dx * br_ref[...].astype(F32))
        dy_scr[...] = lax.dot_general((dx * gate_ref[...]).astype(BF16), wo_ref[...], NT_DIMS,
                                      preferred_element_type=F32)
        row = lax.broadcasted_iota(jnp.int32, (t, cwid), 0)
        has_prev = (i < n_t - 1).astype(F32)
        for c0 in range(0, e, cwid):
            sl = slice(c0, c0 + cwid)
            bg = p_ref[:, c0:c0 + cwid].astype(F32)
            cg = p_ref[:, e + c0:e + c0 + cwid].astype(F32)
            xin = p_ref[:, 2 * e + c0:2 * e + c0 + cwid].astype(F32)
            z = p_ref[:, 3 * e + c0:3 * e + c0 + cwid].astype(F32)
            cx = cg * xin
            p1, p2 = _conv_taps(cx, tails_ref[6:7, sl] * has_prev, tails_ref[7:8, sl] * has_prev, row)
            w0, w1, w2 = cw_ref[0:1, sl], cw_ref[1:2, sl], cw_ref[2:3, sl]
            conv = cb_ref[:, sl] + w2 * cx + w0 * p2 + w1 * p1
            sz, dsz = _silu(z)
            dy = dy_scr[:, sl]
            y_ref[:, sl] = (sz * bg * conv).astype(BF16)
            dp_ref[:, 3 * e + c0:3 * e + c0 + cwid] = (dy * bg * conv * dsz).astype(BF16)
            dp_ref[:, c0:c0 + cwid] = (dy * sz * conv).astype(BF16)
            dconv = dy * sz * bg
            dcb_ref[:, sl] += _colsum(dconv)
            dcw_ref[2:3, sl] += _colsum(dconv * cx)
            dcw_ref[1:2, sl] += _colsum(dconv * p1)
            dcw_ref[0:1, sl] += _colsum(dconv * p2)
            h0, h1 = head_scr[0:1, sl], head_scr[1:2, sl]
            n1 = jnp.where(row == t - 1, h0, pltpu.roll(dconv, t - 1, 0))
            n2 = jnp.where(row == t - 2, h0, jnp.where(row == t - 1, h1, pltpu.roll(dconv, t - 2, 0)))
            dcx = w2 * dconv + w1 * n1 + w0 * n2
            dp_ref[:, e + c0:e + c0 + cwid] = (dcx * xin).astype(BF16)
            dp_ref[:, 2 * e + c0:2 * e + c0 + cwid] = (dcx * cg).astype(BF16)
            head_scr[:, sl] = dconv[0:8, :]

    body = _carry(ex, body, 8, 5, lambda: pl.program_id(0) == 0, None, lambda: pl.program_id(0) == n_t - 1)
    ex_args, ex_in, ex_shapes, ex_out, ex_sems = _carried(ex)
    rev = lambda i: (n_t - 1 - i, 0)
    out = pl.pallas_call(
        body, name=name, grid=(n_t,),
        out_shape=(jax.ShapeDtypeStruct((s, e4), BF16), jax.ShapeDtypeStruct((s, e), BF16),
                   jax.ShapeDtypeStruct((1, d), F32), jax.ShapeDtypeStruct((1, e), F32), jax.ShapeDtypeStruct((8, e), F32),
                   *ex_shapes),
        in_specs=[pl.BlockSpec((t, e4), rev), pl.BlockSpec((t, d), rev), pl.BlockSpec((t, d), rev),
                  pl.BlockSpec((None, 8, e), lambda i: (jnp.maximum(n_t - 2 - i, 0), 0, 0)),
                  _full((3, e)), _full((1, e)), _full((1, d)), _full((e, d)), *ex_in],
        out_specs=(pl.BlockSpec((t, e4), rev), pl.BlockSpec((t, e), rev), _full((1, d)), _full((1, e)), _full((8, e)),
                   *ex_out),
        scratch_shapes=[pltpu.VMEM((t, e), F32), pltpu.VMEM((8, e), F32), *ex_sems],
        compiler_params=_params(("arbitrary",)),
    )(proj, dx1, br, tails, cw, cb, gate, wo, *ex_args)
    return (*out[:5], out[5:])


def _matmul_nt_norm_bwd(dproj, w, xin, dres, g, scale, name, ex=None):
    s, d = xin.shape
    n = w.shape[1]
    tm = min(s, 512)
    n_i = s // tm

    def body(dp_ref, w_ref, x_ref, dres_ref, g_ref, sc_ref, dx_ref, dsh_ref, dsc_ref, dg_ref, p_scr):
        i = pl.program_id(0)

        @pl.when(i == 0)
        def _():
            dsh_ref[...] = jnp.zeros_like(dsh_ref)
            p_scr[...] = jnp.zeros_like(p_scr)
        dh = lax.dot_general(dp_ref[...], w_ref[...], NT_DIMS, preferred_element_type=F32)
        xv = x_ref[...]
        r = lax.rsqrt(jnp.mean(xv * xv, axis=-1, keepdims=True) + RMS_EPS)
        xn = xv * r
        dsh_ref[...] += _colsum(dh)
        p_scr[...] += _colsum(dh * xn)
        dxn = dh * (g_ref[...] * (1.0 + sc_ref[...]))
        dx_ref[...] = r * (dxn - xn * jnp.mean(dxn * xn, axis=-1, keepdims=True)) + dres_ref[...]

        @pl.when(i == n_i - 1)
        def _():
            dsc_ref[...] = p_scr[...] * g_ref[...]
            dg_ref[...] = p_scr[...] * (1.0 + sc_ref[...])

    body = _carry(ex, body, 6, 4, lambda: pl.program_id(0) == 0, None, lambda: pl.program_id(0) == n_i - 1)
    ex_args, ex_in, ex_shapes, ex_out, ex_sems = _carried(ex)
    tok = pl.BlockSpec((tm, d), lambda i: (i, 0))
    vec = pl.BlockSpec((1, d), lambda i: (0, 0))
    vshape = jax.ShapeDtypeStruct((1, d), F32)
    out = pl.pallas_call(
        body, name=name, grid=(n_i,),
        out_shape=(jax.ShapeDtypeStruct((s, d), F32), vshape, vshape, vshape, *ex_shapes),
        in_specs=[pl.BlockSpec((tm, n), lambda i: (i, 0)),
                  pl.BlockSpec((d, n), lambda i: (0, 0), pipeline_mode=pl.Buffered(1)), tok, tok, vec, vec, *ex_in],
        out_specs=(tok, vec, vec, vec, *ex_out),
        scratch_shapes=[pltpu.VMEM((1, d), F32), *ex_sems],
        compiler_params=_params(("arbitrary",)),
    )(dproj, w, xin, dres, g, scale, *ex_args)
    return (*out[:4], out[4:])


def _matmul_tn(a, b, colscale, rows_split, name, ex=None):
    s, m = a.shape
    n = b.shape[1]
    n_j, tn = (1, n) if rows_split else (NDEV, n // NDEV)
    fixed = (4 + 4 + 2 * 2) * m * tn
    tk = s
    while fixed + 2 * tk * (2 * m + b.dtype.itemsize * tn) > TN_VMEM_BUDGET:
        tk //= 2
    n_k = s // tk

    def body(a_ref, b_ref, cs_ref, o_ref, acc):
        k = pl.program_id(1)
        part = lax.dot_general(a_ref[...], b_ref[...].astype(BF16), TN_DIMS, preferred_element_type=F32)
        if n_k == 1:
            o_ref[...] = (part * cs_ref[...]).astype(BF16)
            return

        @pl.when(k == 0)
        def _():
            acc[...] = part

        @pl.when((k > 0) & (k < n_k - 1))
        def _():
            acc[...] += part

        @pl.when(k == n_k - 1)
        def _():
            o_ref[...] = ((acc[...] + part) * cs_ref[...]).astype(BF16)

    at = lambda j, k: (pl.program_id(0) == j) & (pl.program_id(1) == k)
    body = _carry(ex, body, 3, 1, lambda: at(0, 0), None, lambda: at(n_j - 1, n_k - 1))
    ex_args, ex_in, ex_shapes, ex_out, ex_sems = _carried(ex)
    out = pl.pallas_call(
        body, name=name, grid=(n_j, n_k),
        out_shape=(jax.ShapeDtypeStruct((n_j, m, tn), BF16), *ex_shapes),
        in_specs=[pl.BlockSpec((tk, m), lambda j, k: (k, 0)), pl.BlockSpec((tk, tn), lambda j, k: (k, j)),
                  pl.BlockSpec((1, tn), lambda j, k: (0, j)), *ex_in],
        out_specs=(pl.BlockSpec((None, m, tn), lambda j, k: (j, 0, 0)), *ex_out),
        scratch_shapes=[pltpu.VMEM((m, tn), F32), *ex_sems],
        compiler_params=_params(("arbitrary", "arbitrary")),
    )(a, b, colscale, *ex_args)
    return (out[0].reshape(NDEV, m // NDEV, n) if rows_split else out[0]), out[1:]


def _adam_update(w, g, m, v):
    m = ADAM_B1 * m + (1.0 - ADAM_B1) * g
    v = ADAM_B2 * v + (1.0 - ADAM_B2) * (g * g)
    m_hat = m / (1.0 - ADAM_B1 ** ADAM_STEP)
    v_hat = v / (1.0 - ADAM_B2 ** ADAM_STEP)
    return -ADAM_LR * (m_hat / (jnp.sqrt(v_hat) + ADAM_EPS) + ADAM_WD * w), m, v


def _adamw_reduce(parts, w, m, v, name):
    n_p, r, c = parts.shape
    tr = min(r, 256)

    def body(p_ref, w_ref, m_ref, v_ref, g_out, d_out, m_out, v_out):
        g = p_ref[0].astype(F32)
        for j in range(1, n_p):
            g = g + p_ref[j].astype(F32)
        g_out[...] = g
        d_out[...], m_out[...], v_out[...] = _adam_update(w_ref[...], g, m_ref[...], v_ref[...])

    blk = pl.BlockSpec((tr, c), lambda i: (i, 0))
    shp = jax.ShapeDtypeStruct((r, c), F32)
    return pl.pallas_call(
        body, name=name, grid=(r // tr,), out_shape=(shp, shp, shp, shp),
        in_specs=[pl.BlockSpec((n_p, tr, c), lambda i: (0, i, 0)), blk, blk, blk],
        out_specs=(blk, blk, blk, blk),
        compiler_params=_params(("parallel",)),
    )(parts, w, m, v)


def _adamw_small(gs, ws, ms, vs, name):
    n = len(gs)

    def body(*refs):
        ins, outs = refs[:4 * n], refs[4 * n:]
        for a in range(n):
            d, m, v = _adam_update(ins[n + a][...], ins[a][...], ins[2 * n + a][...], ins[3 * n + a][...])
            outs[a][...], outs[n + a][...], outs[2 * n + a][...] = d, m, v

    shapes = tuple(jax.ShapeDtypeStruct(w.shape, F32) for w in ws) * 3
    out = pl.pallas_call(
        body, name=name, out_shape=shapes,
        in_specs=[_vmem()] * (4 * n), out_specs=tuple([_vmem()] * (3 * n)),
        compiler_params=pltpu.CompilerParams(vmem_limit_bytes=VMEM_LIMIT),
    )(*gs, *ws, *ms, *vs)
    return out[:n], out[n:2 * n], out[2 * n:]


def _sum_devices(packed, name):
    _, r, wdt = packed.shape

    def body(p_ref, o_ref):
        acc = p_ref[0]
        for j in range(1, NDEV):
            acc = acc + p_ref[j]
        o_ref[...] = acc

    return pl.pallas_call(
        body, name=name, out_shape=jax.ShapeDtypeStruct((r, wdt), F32),
        in_specs=[_vmem()], out_specs=_vmem(),
        compiler_params=pltpu.CompilerParams(vmem_limit_bytes=VMEM_LIMIT),
    )(packed)


def _mod_w_grad(c_t, dmod, name):
    n_layers, _, w3 = dmod.shape
    d = c_t.shape[0]

    def body(c_ref, dm_ref, o_ref):
        for i in range(n_layers):
            acc = c_ref[:, 0:1] * dm_ref[i, 0:1, :]
            for b in range(1, NDEV):
                acc = acc + c_ref[:, b:b + 1] * dm_ref[i, b:b + 1, :]
            o_ref[i] = acc

    return pl.pallas_call(
        body, name=name, out_shape=jax.ShapeDtypeStruct((n_layers, d, w3), F32),
        in_specs=[_vmem(), _vmem()], out_specs=_vmem(),
        compiler_params=pltpu.CompilerParams(vmem_limit_bytes=VMEM_LIMIT),
    )(c_t, dmod)


def _mask_transpose_ws(w_s, name):
    def body(w_ref, wt_ref, wtt_ref):
        tril = (lax.broadcasted_iota(jnp.int32, (CHUNK, CHUNK), 0) >= lax.broadcasted_iota(jnp.int32, (CHUNK, CHUNK), 1))
        for g in range(GROUPS):
            wm = jnp.where(tril, w_ref[g], 0.0)
            wt_ref[g] = wm.astype(BF16)
            wtt_ref[g] = wm.T.astype(BF16)

    shp = jax.ShapeDtypeStruct(w_s.shape, BF16)
    return pl.pallas_call(
        body, name=name, out_shape=(shp, shp), in_specs=[_vmem()], out_specs=(_vmem(), _vmem()),
    )(w_s)


def _pack(pieces):
    flat = jnp.concatenate([p.reshape(-1) for p in pieces])
    rows = -(-flat.shape[0] // (8 * PACK_W)) * 8
    return jnp.pad(flat, (0, rows * PACK_W - flat.shape[0])).reshape(rows, PACK_W)


def _unpack(flat, shapes):
    out, off = [], 0
    for shp in shapes:
        size = 1
        for dim in shp:
            size *= dim
        out.append(flat[off:off + size].reshape(shp))
        off += size
    return out


def kernel(x, c, mod_w, mod_b, norm_g, a_w_in, a_conv_w, a_conv_b, a_w_out, b_w_in, b_ln_g, b_ln_b, b_w_s, b_b_s, b_w_out, final_g, loss_target, m_mod_w, m_mod_b, m_norm_g, m_a_w_in, m_a_conv_w, m_a_conv_b, m_a_w_out, m_b_w_in, m_b_ln_g, m_b_ln_b, m_b_w_s, m_b_b_s, m_b_w_out, m_final_g, v_mod_w, v_mod_b, v_norm_g, v_a_w_in, v_a_conv_w, v_a_conv_b, v_a_w_out, v_b_w_in, v_b_ln_g, v_b_ln_b, v_b_w_s, v_b_b_s, v_b_w_out, v_final_g):
    s, d = x.shape[1], x.shape[2]
    es = a_w_out.shape[1]
    e = NDEV * es
    w3 = mod_w.shape[2]
    me = _index(_pos())
    x0 = x.reshape(s, d)
    tgt = loss_target.reshape(s, d)

    small = jnp.concatenate([a_conv_w[0], b_ln_g, b_ln_b, jnp.zeros((3, es), F32)], axis=0)
    wa, small_all = _exchange_alone(_gather_exchange([a_w_in[0].astype(BF16), small], [True, True]), "gather_a_w_in")
    gather_rest = _gather_exchange([a_w_out[0].astype(BF16), b_w_in[0].astype(BF16), b_w_out[0].astype(BF16)],
                                   [False, True, False])
    conv_w, ln_g, ln_b = small_all[0:3], small_all[3:4], small_all[4:5]
    bsf = jnp.repeat(b_b_s[0].T, e // GROUPS, axis=1)
    wt, wtt = _mask_transpose_ws(b_w_s[0], "mask_w_s")

    mod, c_all = _mod_vectors(c, mod_w, mod_b)
    shift0, scale0, gate0 = mod[0:1, 0:d], mod[0:1, d:2 * d], mod[0:1, 2 * d:]
    shift1, scale1, gate1 = mod[1:2, 0:d], mod[1:2, d:2 * d], mod[1:2, 2 * d:]
    g0, g1, fg = norm_g[0:1], norm_g[1:2], final_g.reshape(1, d)

    proj_a, h0, (woa, wb, wob) = _norm_mod_matmul(x0, g0, scale0, shift0, wa, "a_in_proj", gather_rest)
    x1, br_a, tails = _conv_mixer_fwd(proj_a, x0, conv_w, a_conv_b, gate0, woa, "a_mixer_fwd")
    proj_b, h1, _ = _norm_mod_matmul(x1, g1, scale1, shift1, wb, "b_in_proj")
    dx2, loss_acc, dfg, dgate1 = _gmlp_fwd_loss(proj_b, x1, tgt, gate1, fg, ln_g, ln_b, wt, bsf, wob, "b_mixer_fwd_loss")
    loss = lax.psum(loss_acc[0, 0], ("x", "y", "c"))

    dproj_b, y_b, dws, dbs, dlg, dlb = _gmlp_bwd(proj_b, dx2, gate1, ln_g, ln_b, wt, wtt, bsf, wob, "b_mixer_bwd")
    gs_b_out, _ = _matmul_tn(y_b, dx2, gate1, True, "b_w_out_grad")
    gs_b_in, _ = _matmul_tn(h1, dproj_b, jnp.ones((1, dproj_b.shape[1]), F32), False, "b_w_in_grad")
    dx1, dshift1, dscale1, dg1, _ = _matmul_nt_norm_bwd(dproj_b, wb, x1, dx2, g1, scale1, "b_in_bwd")
    dproj_a, y_a, dgate0, dcb, dcw, (gr_b_in, gr_b_out) = _conv_mixer_bwd(
        proj_a, dx1, br_a, tails, conv_w, a_conv_b, gate0, woa, "a_mixer_bwd", _scatter_exchange([gs_b_in, gs_b_out]))
    gs_a_out, _ = _matmul_tn(y_a, dx1, gate0, True, "a_w_out_grad")
    gs_a_in, (gr_a_out,) = _matmul_tn(h0, dproj_a, jnp.ones((1, dproj_a.shape[1]), F32), False, "a_w_in_grad",
                                      _scatter_exchange([gs_a_out]))
    dx0, dshift0, dscale0, dg0, (gr_a_in,) = _matmul_nt_norm_bwd(dproj_a, wa, x0, dx1, g0, scale0, "a_in_bwd",
                                                                 _scatter_exchange([gs_a_in]))

    pieces = [dshift0, dscale0, dgate0, dshift1, dscale1, dgate1, dg0, dg1, dcb, dcw[0:3], dlg, dlb, dfg,
              dbs[:, 0:GROUPS].T, dws]
    shapes = [p.shape for p in pieces]
    packed = _pack(pieces)
    packed_all, = _exchange_alone(_gather_exchange([packed], [False]), "gather_small_grads")
    packed_all = packed_all.reshape(NDEV, -1, PACK_W)
    total =_sum_devices(packed_all, "sum_small_grads").reshape(-1)
    (t_sh0, t_sc0, t_ga0, t_sh1, t_sc1, t_ga1, t_g0, t_g1, t_cb, t_cw, t_lg, t_lb, t_fg, t_bs, t_ws) = _unpack(total, shapes)
    grad_mod_b = jnp.concatenate([jnp.concatenate([t_sh0, t_sc0, t_ga0], axis=1),
                                  jnp.concatenate([t_sh1, t_sc1, t_ga1], axis=1)], axis=0)
    grad_norm_g = jnp.concatenate([t_g0, t_g1], axis=0)
    dmod_all = packed_all.reshape(NDEV, -1)[:, 0:6 * d].reshape(NDEV, 2, 3 * d).transpose(1, 0, 2)
    dmod_mine = lax.dynamic_slice_in_dim(dmod_all, me * w3, w3, axis=2)
    grad_mod_w = _mod_w_grad(c_all.T, dmod_mine, "mod_w_grad")
    grad_a_conv_w = lax.dynamic_slice_in_dim(t_cw, me * es, es, axis=1)
    grad_b_ln_g = lax.dynamic_slice_in_dim(t_lg, me * es, es, axis=1)
    grad_b_ln_b = lax.dynamic_slice_in_dim(t_lb, me * es, es, axis=1)

    def big(parts, w, m, v, name):
        shp = w.shape
        r2 = lambda t_: t_.reshape(-1, shp[-1])
        g, dl, nm, nv = _adamw_reduce(parts, r2(w), r2(m), r2(v), name)
        return tuple(t_.reshape(shp) for t_ in (g, dl, nm, nv))

    res = {}
    res["mod_w"] = big(grad_mod_w.reshape(1, -1, w3), mod_w, m_mod_w, v_mod_w, "adamw_mod_w")
    res["a_w_in"] = big(gr_a_in, a_w_in, m_a_w_in, v_a_w_in, "adamw_a_w_in")
    res["a_w_out"] = big(gr_a_out, a_w_out, m_a_w_out, v_a_w_out, "adamw_a_w_out")
    res["b_w_in"] = big(gr_b_in, b_w_in, m_b_w_in, v_b_w_in, "adamw_b_w_in")
    res["b_w_out"] = big(gr_b_out, b_w_out, m_b_w_out, v_b_w_out, "adamw_b_w_out")

    small_names = ["mod_b", "norm_g", "a_conv_w", "a_conv_b", "b_ln_g", "b_ln_b", "b_w_s", "b_b_s", "final_g"]
    small_g = [grad_mod_b, grad_norm_g, grad_a_conv_w, t_cb, grad_b_ln_g, grad_b_ln_b, t_ws, t_bs, t_fg]
    small_w = [mod_b, norm_g, a_conv_w, a_conv_b, b_ln_g, b_ln_b, b_w_s, b_b_s, final_g]
    small_m = [m_mod_b, m_norm_g, m_a_conv_w, m_a_conv_b, m_b_ln_g, m_b_ln_b, m_b_w_s, m_b_b_s, m_final_g]
    small_v = [v_mod_b, v_norm_g, v_a_conv_w, v_a_conv_b, v_b_ln_g, v_b_ln_b, v_b_w_s, v_b_b_s, v_final_g]
    as2d = lambda t_: t_.reshape(-1, t_.shape[-1])
    dls, nms, nvs = _adamw_small([as2d(t_) for t_ in small_g], [as2d(t_) for t_ in small_w],
                                 [as2d(t_) for t_ in small_m], [as2d(t_) for t_ in small_v], "adamw_small")
    for a, nme in enumerate(small_names):
        shp = small_w[a].shape
        res[nme] = (small_g[a].reshape(shp), dls[a].reshape(shp), nms[a].reshape(shp), nvs[a].reshape(shp))

    order = ["mod_w", "mod_b", "norm_g", "a_w_in", "a_conv_w", "a_conv_b", "a_w_out", "b_w_in", "b_ln_g", "b_ln_b",
             "b_w_s", "b_b_s", "b_w_out", "final_g"]
    return (loss, dx0.reshape(x.shape), *[res[k][0] for k in order], *[res[k][1] for k in order],
            *[res[k][2] for k in order], *[res[k][3] for k in order])
```

```python
import functools

import jax
import jax.numpy as jnp
from jax import lax
from jax.experimental import pallas as pl
from jax.experimental.pallas import tpu as pltpu

NDEV = 8
CHUNK = 128
GROUPS = 8
RMS_EPS = 1e-6
LN_EPS = 1e-5
ADAM_LR, ADAM_B1, ADAM_B2, ADAM_EPS, ADAM_WD, ADAM_STEP = 0.001, 0.9, 0.999, 1e-08, 0.01, 10
V7X_VMEM_BYTES = 64 * 1024 * 1024
VMEM_LIMIT = V7X_VMEM_BYTES - 8 * 1024 * 1024
TN_VMEM_BUDGET = 46 * 1024 * 1024
PACK_W = 1024
F32, BF16 = jnp.float32, jnp.bfloat16
MESH = pl.DeviceIdType.MESH
RSQRT2 = 0.7071067811865476
INV_SQRT_2PI = 0.3989422804014327
NT_DIMS = (((1,), (1,)), ((), ()))
TN_DIMS = (((0,), (0,)), ((), ()))


def _params(sem=None):
    return pltpu.CompilerParams(dimension_semantics=sem, vmem_limit_bytes=VMEM_LIMIT)


def _vmem():
    return pl.BlockSpec(memory_space=pltpu.VMEM)


def _hbm():
    return pl.BlockSpec(memory_space=pltpu.HBM)


def _full(shape):
    return pl.BlockSpec(shape, lambda *_: (0,) * len(shape))


def _pos():
    return lax.axis_index("x"), lax.axis_index("y"), lax.axis_index("c")


def _index(p):
    return 4 * p[0] + 2 * p[1] + p[2]


def _peer(k):
    x, y, c = _pos()
    return ((1 - x) if (k >> 2) & 1 else x, (1 - y) if (k >> 1) & 1 else y, (1 - c) if k & 1 else c)


def _silu(z):
    sg = jax.nn.sigmoid(z)
    return z * sg, sg * (1.0 + z * (1.0 - sg))


def _gelu(v):
    phi = 0.5 * (1.0 + lax.erf(v * RSQRT2))
    return v * phi, phi + v * (jnp.exp(-0.5 * v * v) * INV_SQRT_2PI)


def _colsum(v):
    return jnp.sum(v, axis=0, keepdims=True)


def _rowsum(v):
    return jnp.sum(v, axis=-1, keepdims=True)


def _gather_all_vmem(slab_ref, send_sems, recv_sems, base):
    me = _index(_pos())
    sends = []
    for k in range(1, NDEV):
        cp = pltpu.make_async_remote_copy(
            src_ref=slab_ref.at[me], dst_ref=slab_ref.at[me],
            send_sem=send_sems.at[base + k - 1], recv_sem=recv_sems.at[base + k - 1],
            device_id=_peer(k), device_id_type=MESH)
        cp.start()
        sends.append(cp)
    for k in range(1, NDEV):
        src = _index(_peer(k))
        pltpu.make_async_remote_copy(
            src_ref=slab_ref.at[src], dst_ref=slab_ref.at[src],
            send_sem=send_sems.at[base + k - 1], recv_sem=recv_sems.at[base + k - 1],
            device_id=_peer(k), device_id_type=MESH).wait_recv()
    for cp in sends:
        cp.wait_send()


def _mod_vectors(c, mod_w, mod_b):
    n_layers, d, w3 = mod_w.shape

    def body(c_ref, mw_ref, mb_ref, mod_ref, call_ref, cslab, pslab, send_sems, recv_sems):
        me = _index(_pos())
        cv = c_ref[...]
        cslab[me] = jnp.broadcast_to(cv * jax.nn.sigmoid(cv), (8, d))
        _gather_all_vmem(cslab, send_sems, recv_sems, 0)
        c_all = jnp.concatenate([cslab[k, 0:1, :] for k in range(NDEV)], axis=0)
        call_ref[...] = c_all
        for i in range(n_layers):
            pslab[me, i * NDEV:(i + 1) * NDEV, :] = jnp.dot(
                c_all, mw_ref[i], preferred_element_type=F32, precision=lax.Precision.HIGHEST)
        _gather_all_vmem(pslab, send_sems, recv_sems, NDEV - 1)
        for i in range(n_layers):
            for k in range(NDEV):
                mod_ref[i:i + 1, k * w3:(k + 1) * w3] = (
                    pslab[k, pl.ds(i * NDEV + me, 1), :] + mb_ref[i:i + 1, k * w3:(k + 1) * w3])

    return pl.pallas_call(
        body, name="mod_vectors",
        out_shape=(jax.ShapeDtypeStruct((n_layers, 3 * d), F32), jax.ShapeDtypeStruct((NDEV, d), F32)),
        in_specs=[_vmem(), _vmem(), _vmem()], out_specs=(_vmem(), _vmem()),
        scratch_shapes=[pltpu.VMEM((NDEV, 8, d), F32), pltpu.VMEM((NDEV, n_layers * NDEV, w3), F32),
                        pltpu.SemaphoreType.DMA((2 * (NDEV - 1),)), pltpu.SemaphoreType.DMA((2 * (NDEV - 1),))],
        compiler_params=pltpu.CompilerParams(vmem_limit_bytes=VMEM_LIMIT),
    )(c, mod_w, mod_b)


class _Exchange:
    def __init__(self, arrays, out_shapes, sems, start, middle, finish):
        self.arrays, self.out_shapes, self.sems = list(arrays), list(out_shapes), list(sems)
        self.start, self.middle, self.finish = start, middle, finish


def _gather_exchange(shards, by_cols):
    n = len(shards)
    shapes = [sh.shape for sh in shards]

    def tools(ins, outs, sems):
        send_sems, recv_sems, local_sems = sems
        x, y, c = _pos()
        chips = [(1 - x, y), (x, 1 - y), (1 - x, 1 - y)]

        def place(a, block):
            r, cc = shapes[a]
            if by_cols[a]:
                return outs[a].at[:, pl.ds(_index(block) * cc, cc)]
            return outs[a].at[pl.ds(_index(block) * r, r), :]

        def copy(a, k, block, to, src=None):
            dst = place(a, block)
            return pltpu.make_async_remote_copy(
                src_ref=dst if src is None else src, dst_ref=dst,
                send_sem=send_sems.at[a * 7 + k], recv_sem=recv_sems.at[a * 7 + k],
                device_id=to, device_id_type=MESH)

        mine = [pltpu.make_async_copy(ins[a], place(a, (x, y, c)), local_sems.at[a]) for a in range(n)]
        first = []
        for a in range(n):
            first.append(copy(a, 0, (x, y, c), (x, y, 1 - c), src=ins[a]))
            first += [copy(a, 1 + j, (x, y, c), (*chip, c), src=ins[a]) for j, chip in enumerate(chips)]
        passed = [copy(a, 4 + j, (*chip, c), (x, y, 1 - c)) for j, chip in enumerate(chips) for a in range(n)]
        return (x, y, c), chips, copy, mine, first, passed

    def start(ins, outs, sems):
        _, _, _, mine, first, _ = tools(ins, outs, sems)
        for cp in mine + first:
            cp.start()

    def middle(ins, outs, sems):
        (x, y, c), chips, copy, _, _, passed = tools(ins, outs, sems)
        for j, chip in enumerate(chips):
            for a in range(n):
                copy(a, 1 + j, (*chip, c), (x, y, c)).wait_recv()
                passed[j * n + a].start()

    def finish(ins, outs, sems):
        (x, y, c), chips, copy, mine, first, passed = tools(ins, outs, sems)
        for a in range(n):
            copy(a, 0, (x, y, 1 - c), (x, y, c)).wait_recv()
        for j, chip in enumerate(chips):
            for a in range(n):
                copy(a, 4 + j, (*chip, 1 - c), (x, y, c)).wait_recv()
        for cp in first + passed:
            cp.wait_send()
        for cp in mine:
            cp.wait()

    out_shapes = [jax.ShapeDtypeStruct((r, NDEV * cc) if bc else (NDEV * r, cc), sh.dtype)
                  for (r, cc), bc, sh in zip(shapes, by_cols, shards)]
    sems = [pltpu.SemaphoreType.DMA((7 * n,)), pltpu.SemaphoreType.DMA((7 * n,)), pltpu.SemaphoreType.DMA((n,))]
    return _Exchange(shards, out_shapes, sems, start, middle, finish)


def _scatter_exchange(parts):
    n = len(parts)

    def tools(ins, outs, sems):
        send_sems, recv_sems, local_sems = sems
        me = _index(_pos())
        mine = [pltpu.make_async_copy(ins[a].at[me], outs[a].at[me], local_sems.at[a]) for a in range(n)]
        sends, arrivals = [], []
        for k in range(1, NDEV):
            peer = _peer(k)
            for a in range(n):
                pair = dict(send_sem=send_sems.at[a * 7 + k - 1], recv_sem=recv_sems.at[a * 7 + k - 1],
                            device_id=peer, device_id_type=MESH)
                sends.append(pltpu.make_async_remote_copy(src_ref=ins[a].at[_index(peer)], dst_ref=outs[a].at[me], **pair))
                slot = outs[a].at[_index(peer)]
                arrivals.append(pltpu.make_async_remote_copy(src_ref=slot, dst_ref=slot, **pair))
        return mine, sends, arrivals

    def start(ins, outs, sems):
        mine, sends, _ = tools(ins, outs, sems)
        for cp in mine + sends:
            cp.start()

    def finish(ins, outs, sems):
        mine, sends, arrivals = tools(ins, outs, sems)
        for cp in arrivals:
            cp.wait_recv()
        for cp in sends:
            cp.wait_send()
        for cp in mine:
            cp.wait()

    out_shapes = [jax.ShapeDtypeStruct(p.shape, p.dtype) for p in parts]
    sems = [pltpu.SemaphoreType.DMA((7 * n,)), pltpu.SemaphoreType.DMA((7 * n,)), pltpu.SemaphoreType.DMA((n,))]
    return _Exchange(parts, out_shapes, sems, start, None, finish)


def _exchange_alone(ex, name):
    n_in, n_out = len(ex.arrays), len(ex.out_shapes)

    def body(*refs):
        ins, outs, sems = refs[:n_in], refs[n_in:n_in + n_out], refs[n_in + n_out:]
        ex.start(ins, outs, sems)
        if ex.middle is not None:
            ex.middle(ins, outs, sems)
        ex.finish(ins, outs, sems)

    return pl.pallas_call(
        body, name=name, out_shape=tuple(ex.out_shapes),
        in_specs=[_hbm()] * n_in, out_specs=tuple([_hbm()] * n_out), scratch_shapes=ex.sems,
    )(*ex.arrays)


def _carry(ex, body, n_in, n_out, first, middle, last):
    if ex is None:
        return body
    r_in, r_out = len(ex.arrays), len(ex.out_shapes)

    def wrapped(*refs):
        ins, rins = refs[:n_in], refs[n_in:n_in + r_in]
        outs = refs[n_in + r_in:n_in + r_in + n_out]
        routs = refs[n_in + r_in + n_out:n_in + r_in + n_out + r_out]
        rest = refs[n_in + r_in + n_out + r_out:]
        scratch, sems = rest[:len(rest) - len(ex.sems)], rest[len(rest) - len(ex.sems):]

        @pl.when(first())
        def _():
            ex.start(rins, routs, sems)

        if ex.middle is not None:
            @pl.when(middle())
            def _():
                ex.middle(rins, routs, sems)

        body(*ins, *outs, *scratch)

        @pl.when(last())
        def _():
            ex.finish(rins, routs, sems)

    return wrapped


def _carried(ex):
    if ex is None:
        return [], [], [], [], []
    return ex.arrays, [_hbm()] * len(ex.arrays), ex.out_shapes, [_hbm()] * len(ex.out_shapes), ex.sems


def _resident(shape):
    return pl.BlockSpec(shape, lambda *_: (0,) * len(shape), pipeline_mode=pl.Buffered(1))


def _norm_modulate(x_ref, g_ref, sc_ref, sh_ref):
    xv = x_ref[...]
    r = lax.rsqrt(jnp.mean(xv * xv, axis=-1, keepdims=True) + RMS_EPS)
    return ((xv * r) * g_ref[...] * (1.0 + sc_ref[...]) + sh_ref[...]).astype(BF16)


def _conv_taps(cx, t6, t7, row):
    p1 = jnp.where(row == 0, t7, pltpu.roll(cx, 1, 0))
    p2 = jnp.where(row == 0, t6, jnp.where(row == 1, t7, pltpu.roll(cx, 2, 0)))
    return p1, p2


def _layer_a_fwd(x, g, scale, shift, gate, wi, cw, cb, wo, name, ex=None):
    s, d = x.shape
    e = wo.shape[0]
    t = min(s, 256)
    n_t = s // t
    cwid = min(e, 512)

    def body(x_ref, g_ref, sc_ref, sh_ref, gate_ref, wi_ref, cw_ref, cb_ref, wo_ref,
             proj_ref, h_ref, x1_ref, br_ref, tails_ref, y_scr, tail_scr):
        @pl.when(pl.program_id(0) == 0)
        def _():
            tail_scr[...] = jnp.zeros_like(tail_scr)
        h_ref[...] = _norm_modulate(x_ref, g_ref, sc_ref, sh_ref)
        row = lax.broadcasted_iota(jnp.int32, (t, cwid), 0)

        def project(c0):
            v = jnp.dot(h_ref[...], wi_ref[:, c0:c0 + cwid], preferred_element_type=F32)
            proj_ref[:, c0:c0 + cwid] = v.astype(BF16)
            return v

        for c0 in range(0, e, cwid):
            sl = slice(c0, c0 + cwid)
            bg, z = project(c0), project(3 * e + c0)
            cx = project(e + c0) * project(2 * e + c0)
            p1, p2 = _conv_taps(cx, tail_scr[6:7, sl], tail_scr[7:8, sl], row)
            conv = cb_ref[:, sl] + cw_ref[2:3, sl] * cx + cw_ref[0:1, sl] * p2 + cw_ref[1:2, sl] * p1
            y_scr[:, sl] = (_silu(z)[0] * bg * conv).astype(BF16)
            tail_scr[:, sl] = cx[t - 8:t, :]
        tails_ref[...] = tail_scr[...]
        br = jnp.dot(y_scr[...], wo_ref[...], preferred_element_type=F32)
        x1_ref[...] = x_ref[...] + gate_ref[...] * br
        br_ref[...] = br.astype(BF16)

    step = lambda k: (lambda: pl.program_id(0) == k)
    body = _carry(ex, body, 9, 5, step(0), step(n_t // 2), step(n_t - 1))
    ex_args, ex_in, ex_shapes, ex_out, ex_sems = _carried(ex)
    tok = pl.BlockSpec((t, d), lambda i: (i, 0))
    out = pl.pallas_call(
        body, name=name, grid=(n_t,),
        out_shape=(jax.ShapeDtypeStruct((s, 4 * e), BF16), jax.ShapeDtypeStruct((s, d), BF16),
                   jax.ShapeDtypeStruct((s, d), F32), jax.ShapeDtypeStruct((s, d), BF16),
                   jax.ShapeDtypeStruct((n_t, 8, e), F32), *ex_shapes),
        in_specs=[tok, _full((1, d)), _full((1, d)), _full((1, d)), _full((1, d)), _resident((d, 4 * e)),
                  _full((3, e)), _full((1, e)), _resident((e, d)), *ex_in],
        out_specs=(pl.BlockSpec((t, 4 * e), lambda i: (i, 0)), tok, tok, tok,
                   pl.BlockSpec((None, 8, e), lambda i: (i, 0, 0)), *ex_out),
        scratch_shapes=[pltpu.VMEM((t, e), BF16), pltpu.VMEM((8, e), F32), *ex_sems],
        compiler_params=_params(("arbitrary",)),
    )(x, g, scale, shift, gate, wi, cw, cb, wo, *ex_args)
    return (*out[:5], out[5:])


def _ln_stats(v_pre, v_scr, dgv_scr, t, e):
    gw = e // GROUPS
    s1 = jnp.zeros((t, 1), F32)
    for g in range(GROUPS):
        v, dgv = _gelu(v_pre(g))
        v_scr[:, g * gw:(g + 1) * gw] = v
        if dgv_scr is not None:
            dgv_scr[:, g * gw:(g + 1) * gw] = dgv.astype(dgv_scr.dtype)
        s1 = s1 + _rowsum(v)
    mu = s1 * (1.0 / e)
    s2 = jnp.zeros((t, 1), F32)
    for g in range(GROUPS):
        dv = v_scr[:, g * gw:(g + 1) * gw] - mu
        s2 = s2 + _rowsum(dv * dv)
    return mu, lax.rsqrt(s2 * (1.0 / e) + LN_EPS)


def _layer_b_fwd_loss(x1, tgt, g1, scale, shift, gate, fg, wi, lng, lnb, wt, bsf, wo, name, ex=None):
    s, d = x1.shape
    e = wo.shape[0]
    gw = e // GROUPS
    t = min(s, 256)
    n_t = s // t

    def body(x1_ref, tgt_ref, g_ref, sc_ref, sh_ref, gate_ref, fg_ref, wi_ref, lng_ref, lnb_ref, wt_ref, bsf_ref, wo_ref,
             proj_ref, h_ref, dx2_ref, loss_ref, dfg_ref, dgate_ref, v_scr, y_scr):
        @pl.when(pl.program_id(0) == 0)
        def _():
            loss_ref[...] = jnp.zeros_like(loss_ref)
            dfg_ref[...] = jnp.zeros_like(dfg_ref)
            dgate_ref[...] = jnp.zeros_like(dgate_ref)
        h_ref[...] = _norm_modulate(x1_ref, g_ref, sc_ref, sh_ref)

        def project(c0):
            v = jnp.dot(h_ref[...], wi_ref[:, c0:c0 + gw], preferred_element_type=F32)
            proj_ref[:, c0:c0 + gw] = v.astype(BF16)
            return v

        mu, rs = _ln_stats(lambda g: project(e + g * gw), v_scr, None, t, e)
        for g in range(GROUPS):
            gs = slice(g * gw, (g + 1) * gw)
            vn = (((v_scr[:, gs] - mu) * rs) * lng_ref[:, gs] + lnb_ref[:, gs]).astype(BF16)
            u = _gelu(project(g * gw))[0]
            sz = _silu(project(2 * e + g * gw))[0]
            for ch in range(t // CHUNK):
                rows = slice(ch * CHUNK, (ch + 1) * CHUNK)
                mixed = jnp.dot(wt_ref[g], vn[rows], preferred_element_type=F32) + bsf_ref[:, gs]
                y_scr[rows, gs] = (sz[rows] * (u[rows] * mixed)).astype(BF16)
        br = jnp.dot(y_scr[...], wo_ref[...], preferred_element_type=F32)
        x2 = x1_ref[...] + gate_ref[...] * br
        r2 = lax.rsqrt(jnp.mean(x2 * x2, axis=-1, keepdims=True) + RMS_EPS)
        xn = x2 * r2
        diff = xn * fg_ref[...] - tgt_ref[...]
        loss_ref[...] += jnp.broadcast_to(0.5 * _colsum(jnp.mean(diff * diff, axis=-1, keepdims=True)), loss_ref.shape)
        dout = diff * (1.0 / d)
        dfg_ref[...] += _colsum(dout * xn)
        dxn = dout * fg_ref[...]
        dx2 = r2 * (dxn - xn * jnp.mean(dxn * xn, axis=-1, keepdims=True))
        dx2_ref[...] = dx2
        dgate_ref[...] += _colsum(dx2 * br)

    step = lambda k: (lambda: pl.program_id(0) == k)
    body = _carry(ex, body, 13, 6, step(0), step(n_t // 2), step(n_t - 1))
    ex_args, ex_in, ex_shapes, ex_out, ex_sems = _carried(ex)
    tok = pl.BlockSpec((t, d), lambda i: (i, 0))
    vec = _full((1, d))
    out = pl.pallas_call(
        body, name=name, grid=(n_t,),
        out_shape=(jax.ShapeDtypeStruct((s, 3 * e), BF16), jax.ShapeDtypeStruct((s, d), BF16),
                   jax.ShapeDtypeStruct((s, d), F32), jax.ShapeDtypeStruct((8, 128), F32),
                   jax.ShapeDtypeStruct((1, d), F32), jax.ShapeDtypeStruct((1, d), F32), *ex_shapes),
        in_specs=[tok, tok, vec, vec, vec, vec, vec, _resident((d, 3 * e)), _full((1, e)), _full((1, e)),
                  _full((GROUPS, CHUNK, CHUNK)), _resident((CHUNK, e)), _resident((e, d)), *ex_in],
        out_specs=(pl.BlockSpec((t, 3 * e), lambda i: (i, 0)), tok, tok, _full((8, 128)), vec, vec, *ex_out),
        scratch_shapes=[pltpu.VMEM((t, e), F32), pltpu.VMEM((t, e), BF16), *ex_sems],
        compiler_params=_params(("arbitrary",)),
    )(x1, tgt, g1, scale, shift, gate, fg, wi, lng, lnb, wt, bsf, wo, *ex_args)
    return (*out[:6], out[6:])


def _norm_modulate_bwd(dh, x_ref, dres_ref, g_ref, sc_ref, dx_ref, dsh_ref, p_scr):
    xv = x_ref[...]
    r = lax.rsqrt(jnp.mean(xv * xv, axis=-1, keepdims=True) + RMS_EPS)
    xn = xv * r
    dsh_ref[...] += _colsum(dh)
    p_scr[...] += _colsum(dh * xn)
    dxn = dh * (g_ref[...] * (1.0 + sc_ref[...]))
    dx_ref[...] = r * (dxn - xn * jnp.mean(dxn * xn, axis=-1, keepdims=True)) + dres_ref[...]


def _layer_b_bwd(proj, dx2, x1, gate, g1, scale, lng, lnb, wt, wtt, bsf, wo, wi, name):
    s, e3 = proj.shape
    e = e3 // 3
    d = dx2.shape[1]
    gw = e // GROUPS
    t = min(s, 256)
    n_t = s // t

    def body(p_ref, dx_ref, x1_ref, gate_ref, g_ref, sc_ref, lng_ref, lnb_ref, wt_ref, wtt_ref, bsf_ref, wo_ref, wi_ref,
             dp_ref, y_ref, dx1_ref, dws_ref, dbs_ref, dlg_ref, dlb_ref, dsh_ref, dsc_ref, dg_ref,
             v_scr, dgv_scr, dbr_scr, dvn_scr, dbs_scr, p_scr):
        @pl.when(pl.program_id(0) == 0)
        def _():
            dws_ref[...] = jnp.zeros_like(dws_ref)
            dlg_ref[...] = jnp.zeros_like(dlg_ref)
            dlb_ref[...] = jnp.zeros_like(dlb_ref)
            dsh_ref[...] = jnp.zeros_like(dsh_ref)
            dbs_scr[...] = jnp.zeros_like(dbs_scr)
            p_scr[...] = jnp.zeros_like(p_scr)
        dbr_scr[...] = (dx_ref[...] * gate_ref[...]).astype(BF16)
        mu, rs = _ln_stats(lambda g: p_ref[:, e + g * gw:e + (g + 1) * gw].astype(F32), v_scr, dgv_scr, t, e)
        tril = (lax.broadcasted_iota(jnp.int32, (CHUNK, CHUNK), 0) >= lax.broadcasted_iota(jnp.int32, (CHUNK, CHUNK), 1))
        c1 = jnp.zeros((t, 1), F32)
        c2 = jnp.zeros((t, 1), F32)
        for g in range(GROUPS):
            gs = slice(g * gw, (g + 1) * gw)
            vhat = (v_scr[:, gs] - mu) * rs
            lg = lng_ref[:, gs]
            vn = (vhat * lg + lnb_ref[:, gs]).astype(BF16)
            dy_g = lax.dot_general(dbr_scr[...], wo_ref[g * gw:(g + 1) * gw, :], NT_DIMS, preferred_element_type=F32)
            for ch in range(t // CHUNK):
                rows = slice(ch * CHUNK, (ch + 1) * CHUNK)
                mixed = jnp.dot(wt_ref[g], vn[rows], preferred_element_type=F32) + bsf_ref[:, gs]
                u, dgu = _gelu(p_ref[rows, g * gw:(g + 1) * gw].astype(F32))
                sz, dsz = _silu(p_ref[rows, 2 * e + g * gw:2 * e + (g + 1) * gw].astype(F32))
                sgate = u * mixed
                y_ref[rows, gs] = (sz * sgate).astype(BF16)
                dy = dy_g[rows]
                dp_ref[rows, 2 * e + g * gw:2 * e + (g + 1) * gw] = (dy * sgate * dsz).astype(BF16)
                ds = dy * sz
                dp_ref[rows, gs] = (ds * mixed * dgu).astype(BF16)
                dm = ds * u
                dbs_scr[:, gs] += dm
                dmb = dm.astype(BF16)
                dws_ref[g] += jnp.where(tril, lax.dot_general(dmb, vn[rows], NT_DIMS, preferred_element_type=F32), 0.0)
                dvn_scr[rows, gs] = jnp.dot(wtt_ref[g], dmb, preferred_element_type=F32)
            dvn = dvn_scr[:, gs]
            dlb_ref[:, gs] += _colsum(dvn)
            dlg_ref[:, gs] += _colsum(dvn * vhat)
            dvh = dvn * lg
            c1 = c1 + _rowsum(dvh)
            c2 = c2 + _rowsum(dvh * vhat)
        c1 = c1 * (1.0 / e)
        c2 = c2 * (1.0 / e)
        for g in range(GROUPS):
            gs = slice(g * gw, (g + 1) * gw)
            vhat = (v_scr[:, gs] - mu) * rs
            dv = rs * (dvn_scr[:, gs] * lng_ref[:, gs] - c1 - vhat * c2)
            dp_ref[:, e + g * gw:e + (g + 1) * gw] = (dv * dgv_scr[:, gs]).astype(BF16)
        dh = lax.dot_general(dp_ref[...], wi_ref[...], NT_DIMS, preferred_element_type=F32)
        _norm_modulate_bwd(dh, x1_ref, dx_ref, g_ref, sc_ref, dx1_ref, dsh_ref, p_scr)

        @pl.when(pl.program_id(0) == n_t - 1)
        def _():
            lane = lax.broadcasted_iota(jnp.int32, (CHUNK, 128), 1)
            acc = jnp.zeros((CHUNK, 128), F32)
            for g in range(GROUPS):
                acc = acc + jnp.where(lane == g, _rowsum(dbs_scr[:, g * gw:(g + 1) * gw]), 0.0)
            dbs_ref[...] = acc
            dsc_ref[...] = p_scr[...] * g_ref[...]
            dg_ref[...] = p_scr[...] * (1.0 + sc_ref[...])

    tok = pl.BlockSpec((t, d), lambda i: (i, 0))
    vec, evec, ws = _full((1, d)), _full((1, e)), _full((GROUPS, CHUNK, CHUNK))
    vshape = jax.ShapeDtypeStruct((1, d), F32)
    return pl.pallas_call(
        body, name=name, grid=(n_t,),
        out_shape=(jax.ShapeDtypeStruct((s, e3), BF16), jax.ShapeDtypeStruct((s, e), BF16), jax.ShapeDtypeStruct((s, d), F32),
                   jax.ShapeDtypeStruct((GROUPS, CHUNK, CHUNK), F32), jax.ShapeDtypeStruct((CHUNK, 128), F32),
                   jax.ShapeDtypeStruct((1, e), F32), jax.ShapeDtypeStruct((1, e), F32), vshape, vshape, vshape),
        in_specs=[pl.BlockSpec((t, e3), lambda i: (i, 0)), tok, tok, vec, vec, vec, evec, evec, ws, ws,
                  _resident((CHUNK, e)), _resident((e, d)), _resident((d, e3))],
        out_specs=(pl.BlockSpec((t, e3), lambda i: (i, 0)), pl.BlockSpec((t, e), lambda i: (i, 0)), tok,
                   ws, _full((CHUNK, 128)), evec, evec, vec, vec, vec),
        scratch_shapes=[pltpu.VMEM((t, e), F32), pltpu.VMEM((t, e), BF16), pltpu.VMEM((t, d), BF16),
                        pltpu.VMEM((t, e), F32), pltpu.VMEM((CHUNK, e), F32), pltpu.VMEM((1, d), F32)],
        compiler_params=_params(("arbitrary",)),
    )(proj, dx2, x1, gate, g1, scale, lng, lnb, wt, wtt, bsf, wo, wi)


def _conv_mixer_bwd(proj, dx1, br, tails, cw, cb, gate, wo, name, ex=None):
    s, e4 = proj.shape
    e = e4 // 4
    d = dx1.shape[1]
    t = min(s, 256)
    n_t = s // t
    cwid = min(e, 512)

    def body(p_ref, dx_ref, br_ref, tails_ref, cw_ref, cb_ref, gate_ref, wo_ref,
             dp_ref, y_ref, dgate_ref, dcb_ref, dcw_ref, dy_scr, head_scr):
        i = pl.program_id(0)

        @pl.when(i == 0)
        def _():
            dgate_ref[...] = jnp.zeros_like(dgate_ref)
            dcb_ref[...] = jnp.zeros_like(dcb_ref)
            dcw_ref[...] = jnp.zeros_like(dcw_ref)
            head_scr[...] = jnp.zeros_like(head_scr)
        dx = dx_ref[...]
        dgate_ref[...] += _colsum(dx * br_ref[...].astype(F32))
        dy_scr[...] = lax.dot_general((dx * gate_ref[...]).astype(BF16), wo_ref[...], NT_DIMS,
                                      preferred_element_type=F32)
        row = lax.broadcasted_iota(jnp.int32, (t, cwid), 0)
        has_prev = (i < n_t - 1).astype(F32)
        for c0 in range(0, e, cwid):
            sl = slice(c0, c0 + cwid)
            bg = p_ref[:, c0:c0 + cwid].astype(F32)
            cg = p_ref[:, e + c0:e + c0 + cwid].astype(F32)
            xin = p_ref[:, 2 * e + c0:2 * e + c0 + cwid].astype(F32)
            z = p_ref[:, 3 * e + c0:3 * e + c0 + cwid].astype(F32)
            cx = cg * xin
            p1, p2 = _conv_taps(cx, tails_ref[6:7, sl] * has_prev, tails_ref[7:8, sl] * has_prev, row)
            w0, w1, w2 = cw_ref[0:1, sl], cw_ref[1:2, sl], cw_ref[2:3, sl]
            conv = cb_ref[:, sl] + w2 * cx + w0 * p2 + w1 * p1
            sz, dsz = _silu(z)
            dy = dy_scr[:, sl]
            y_ref[:, sl] = (sz * bg * conv).astype(BF16)
            dp_ref[:, 3 * e + c0:3 * e + c0 + cwid] = (dy * bg * conv * dsz).astype(BF16)
            dp_ref[:, c0:c0 + cwid] = (dy * sz * conv).astype(BF16)
            dconv = dy * sz * bg
            dcb_ref[:, sl] += _colsum(dconv)
            dcw_ref[2:3, sl] += _colsum(dconv * cx)
            dcw_ref[1:2, sl] += _colsum(dconv * p1)
            dcw_ref[0:1, sl] += _colsum(dconv * p2)
            h0, h1 = head_scr[0:1, sl], head_scr[1:2, sl]
            n1 = jnp.where(row == t - 1, h0, pltpu.roll(dconv, t - 1, 0))
            n2 = jnp.where(row == t - 2, h0, jnp.where(row == t - 1, h1, pltpu.roll(dconv, t - 2, 0)))
            dcx = w2 * dconv + w1 * n1 + w0 * n2
            dp_ref[:, e + c0:e + c0 + cwid] = (dcx * xin).astype(BF16)
            dp_ref[:, 2 * e + c0:2 * e + c0 + cwid] = (dcx * cg).astype(BF16)
            head_scr[:, sl] = dconv[0:8, :]

    body = _carry(ex, body, 8, 5, lambda: pl.program_id(0) == 0, None, lambda: pl.program_id(0) == n_t - 1)
    ex_args, ex_in, ex_shapes, ex_out, ex_sems = _carried(ex)
    rev = lambda i: (n_t - 1 - i, 0)
    out = pl.pallas_call(
        body, name=name, grid=(n_t,),
        out_shape=(jax.ShapeDtypeStruct((s, e4), BF16), jax.ShapeDtypeStruct((s, e), BF16),
                   jax.ShapeDtypeStruct((1, d), F32), jax.ShapeDtypeStruct((1, e), F32), jax.ShapeDtypeStruct((8, e), F32),
                   *ex_shapes),
        in_specs=[pl.BlockSpec((t, e4), rev), pl.BlockSpec((t, d), rev), pl.BlockSpec((t, d), rev),
                  pl.BlockSpec((None, 8, e), lambda i: (jnp.maximum(n_t - 2 - i, 0), 0, 0)),
                  _full((3, e)), _full((1, e)), _full((1, d)), _full((e, d)), *ex_in],
        out_specs=(pl.BlockSpec((t, e4), rev), pl.BlockSpec((t, e), rev), _full((1, d)), _full((1, e)), _full((8, e)),
                   *ex_out),
        scratch_shapes=[pltpu.VMEM((t, e), F32), pltpu.VMEM((8, e), F32), *ex_sems],
        compiler_params=_params(("arbitrary",)),
    )(proj, dx1, br, tails, cw, cb, gate, wo, *ex_args)
    return (*out[:5], out[5:])


def _matmul_nt_norm_bwd(dproj, w, xin, dres, g, scale, name, ex=None):
    s, d = xin.shape
    n = w.shape[1]
    tm = min(s, 512)
    n_i = s // tm

    def body(dp_ref, w_ref, x_ref, dres_ref, g_ref, sc_ref, dx_ref, dsh_ref, dsc_ref, dg_ref, p_scr):
        i = pl.program_id(0)

        @pl.when(i == 0)
        def _():
            dsh_ref[...] = jnp.zeros_like(dsh_ref)
            p_scr[...] = jnp.zeros_like(p_scr)
        dh = lax.dot_general(dp_ref[...], w_ref[...], NT_DIMS, preferred_element_type=F32)
        xv = x_ref[...]
        r = lax.rsqrt(jnp.mean(xv * xv, axis=-1, keepdims=True) + RMS_EPS)
        xn = xv * r
        dsh_ref[...] += _colsum(dh)
        p_scr[...] += _colsum(dh * xn)
        dxn = dh * (g_ref[...] * (1.0 + sc_ref[...]))
        dx_ref[...] = r * (dxn - xn * jnp.mean(dxn * xn, axis=-1, keepdims=True)) + dres_ref[...]

        @pl.when(i == n_i - 1)
        def _():
            dsc_ref[...] = p_scr[...] * g_ref[...]
            dg_ref[...] = p_scr[...] * (1.0 + sc_ref[...])

    body = _carry(ex, body, 6, 4, lambda: pl.program_id(0) == 0, None, lambda: pl.program_id(0) == n_i - 1)
    ex_args, ex_in, ex_shapes, ex_out, ex_sems = _carried(ex)
    tok = pl.BlockSpec((tm, d), lambda i: (i, 0))
    vec = pl.BlockSpec((1, d), lambda i: (0, 0))
    vshape = jax.ShapeDtypeStruct((1, d), F32)
    out = pl.pallas_call(
        body, name=name, grid=(n_i,),
        out_shape=(jax.ShapeDtypeStruct((s, d), F32), vshape, vshape, vshape, *ex_shapes),
        in_specs=[pl.BlockSpec((tm, n), lambda i: (i, 0)),
                  pl.BlockSpec((d, n), lambda i: (0, 0), pipeline_mode=pl.Buffered(1)), tok, tok, vec, vec, *ex_in],
        out_specs=(tok, vec, vec, vec, *ex_out),
        scratch_shapes=[pltpu.VMEM((1, d), F32), *ex_sems],
        compiler_params=_params(("arbitrary",)),
    )(dproj, w, xin, dres, g, scale, *ex_args)
    return (*out[:4], out[4:])


def _matmul_tn(a, b, colscale, rows_split, name, ex=None):
    s, m = a.shape
    n = b.shape[1]
    n_j, tn = (1, n) if rows_split else (NDEV, n // NDEV)
    fixed = (4 + 4 + 2 * 2) * m * tn
    tk = s
    while fixed + 2 * tk * (2 * m + b.dtype.itemsize * tn) > TN_VMEM_BUDGET:
        tk //= 2
    n_k = s // tk

    def body(a_ref, b_ref, cs_ref, o_ref, acc):
        k = pl.program_id(1)
        part = lax.dot_general(a_ref[...], b_ref[...].astype(BF16), TN_DIMS, preferred_element_type=F32)
        if n_k == 1:
            o_ref[...] = (part * cs_ref[...]).astype(BF16)
            return

        @pl.when(k == 0)
        def _():
            acc[...] = part

        @pl.when((k > 0) & (k < n_k - 1))
        def _():
            acc[...] += part

        @pl.when(k == n_k - 1)
        def _():
            o_ref[...] = ((acc[...] + part) * cs_ref[...]).astype(BF16)

    at = lambda j, k: (pl.program_id(0) == j) & (pl.program_id(1) == k)
    body = _carry(ex, body, 3, 1, lambda: at(0, 0), None, lambda: at(n_j - 1, n_k - 1))
    ex_args, ex_in, ex_shapes, ex_out, ex_sems = _carried(ex)
    out = pl.pallas_call(
        body, name=name, grid=(n_j, n_k),
        out_shape=(jax.ShapeDtypeStruct((n_j, m, tn), BF16), *ex_shapes),
        in_specs=[pl.BlockSpec((tk, m), lambda j, k: (k, 0)), pl.BlockSpec((tk, tn), lambda j, k: (k, j)),
                  pl.BlockSpec((1, tn), lambda j, k: (0, j)), *ex_in],
        out_specs=(pl.BlockSpec((None, m, tn), lambda j, k: (j, 0, 0)), *ex_out),
        scratch_shapes=[pltpu.VMEM((m, tn), F32), *ex_sems],
        compiler_params=_params(("arbitrary", "arbitrary")),
    )(a, b, colscale, *ex_args)
    return (out[0].reshape(NDEV, m // NDEV, n) if rows_split else out[0]), out[1:]


def _adam_update(w, g, m, v):
    m = ADAM_B1 * m + (1.0 - ADAM_B1) * g
    v = ADAM_B2 * v + (1.0 - ADAM_B2) * (g * g)
    m_hat = m / (1.0 - ADAM_B1 ** ADAM_STEP)
    v_hat = v / (1.0 - ADAM_B2 ** ADAM_STEP)
    return -ADAM_LR * (m_hat / (jnp.sqrt(v_hat) + ADAM_EPS) + ADAM_WD * w), m, v


def _adamw_reduce(parts, w, m, v, name):
    n_p, r, c = parts.shape
    tr = min(r, 256)

    def body(p_ref, w_ref, m_ref, v_ref, g_out, d_out, m_out, v_out):
        g = p_ref[0].astype(F32)
        for j in range(1, n_p):
            g = g + p_ref[j].astype(F32)
        g_out[...] = g
        d_out[...], m_out[...], v_out[...] = _adam_update(w_ref[...], g, m_ref[...], v_ref[...])

    blk = pl.BlockSpec((tr, c), lambda i: (i, 0))
    shp = jax.ShapeDtypeStruct((r, c), F32)
    return pl.pallas_call(
        body, name=name, grid=(r // tr,), out_shape=(shp, shp, shp, shp),
        in_specs=[pl.BlockSpec((n_p, tr, c), lambda i: (0, i, 0)), blk, blk, blk],
        out_specs=(blk, blk, blk, blk),
        compiler_params=_params(("parallel",)),
    )(parts, w, m, v)


def _adamw_small(gs, ws, ms, vs, name):
    n = len(gs)

    def body(*refs):
        ins, outs = refs[:4 * n], refs[4 * n:]
        for a in range(n):
            d, m, v = _adam_update(ins[n + a][...], ins[a][...], ins[2 * n + a][...], ins[3 * n + a][...])
            outs[a][...], outs[n + a][...], outs[2 * n + a][...] = d, m, v

    shapes = tuple(jax.ShapeDtypeStruct(w.shape, F32) for w in ws) * 3
    out = pl.pallas_call(
        body, name=name, out_shape=shapes,
        in_specs=[_vmem()] * (4 * n), out_specs=tuple([_vmem()] * (3 * n)),
        compiler_params=pltpu.CompilerParams(vmem_limit_bytes=VMEM_LIMIT),
    )(*gs, *ws, *ms, *vs)
    return out[:n], out[n:2 * n], out[2 * n:]


def _sum_devices(packed, name):
    _, r, wdt = packed.shape

    def body(p_ref, o_ref):
        acc = p_ref[0]
        for j in range(1, NDEV):
            acc = acc + p_ref[j]
        o_ref[...] = acc

    return pl.pallas_call(
        body, name=name, out_shape=jax.ShapeDtypeStruct((r, wdt), F32),
        in_specs=[_vmem()], out_specs=_vmem(),
        compiler_params=pltpu.CompilerParams(vmem_limit_bytes=VMEM_LIMIT),
    )(packed)


def _mod_w_grad(c_t, dmod, name):
    n_layers, _, w3 = dmod.shape
    d = c_t.shape[0]

    def body(c_ref, dm_ref, o_ref):
        for i in range(n_layers):
            acc = c_ref[:, 0:1] * dm_ref[i, 0:1, :]
            for b in range(1, NDEV):
                acc = acc + c_ref[:, b:b + 1] * dm_ref[i, b:b + 1, :]
            o_ref[i] = acc

    return pl.pallas_call(
        body, name=name, out_shape=jax.ShapeDtypeStruct((n_layers, d, w3), F32),
        in_specs=[_vmem(), _vmem()], out_specs=_vmem(),
        compiler_params=pltpu.CompilerParams(vmem_limit_bytes=VMEM_LIMIT),
    )(c_t, dmod)


def _mask_transpose_ws(w_s, name):
    def body(w_ref, wt_ref, wtt_ref):
        tril = (lax.broadcasted_iota(jnp.int32, (CHUNK, CHUNK), 0) >= lax.broadcasted_iota(jnp.int32, (CHUNK, CHUNK), 1))
        for g in range(GROUPS):
            wm = jnp.where(tril, w_ref[g], 0.0)
            wt_ref[g] = wm.astype(BF16)
            wtt_ref[g] = wm.T.astype(BF16)

    shp = jax.ShapeDtypeStruct(w_s.shape, BF16)
    return pl.pallas_call(
        body, name=name, out_shape=(shp, shp), in_specs=[_vmem()], out_specs=(_vmem(), _vmem()),
    )(w_s)


def _pack(pieces):
    flat = jnp.concatenate([p.reshape(-1) for p in pieces])
    rows = -(-flat.shape[0] // (8 * PACK_W)) * 8
    return jnp.pad(flat, (0, rows * PACK_W - flat.shape[0])).reshape(rows, PACK_W)


def _unpack(flat, shapes):
    out, off = [], 0
    for shp in shapes:
        size = 1
        for dim in shp:
            size *= dim
        out.append(flat[off:off + size].reshape(shp))
        off += size
    return out


def kernel(x, c, mod_w, mod_b, norm_g, a_w_in, a_conv_w, a_conv_b, a_w_out, b_w_in, b_ln_g, b_ln_b, b_w_s, b_b_s, b_w_out, final_g, loss_target, m_mod_w, m_mod_b, m_norm_g, m_a_w_in, m_a_conv_w, m_a_conv_b, m_a_w_out, m_b_w_in, m_b_ln_g, m_b_ln_b, m_b_w_s, m_b_b_s, m_b_w_out, m_final_g, v_mod_w, v_mod_b, v_norm_g, v_a_w_in, v_a_conv_w, v_a_conv_b, v_a_w_out, v_b_w_in, v_b_ln_g, v_b_ln_b, v_b_w_s, v_b_b_s, v_b_w_out, v_final_g):
    s, d = x.shape[1], x.shape[2]
    es = a_w_out.shape[1]
    e = NDEV * es
    w3 = mod_w.shape[2]
    me = _index(_pos())
    x0 = x.reshape(s, d)
    tgt = loss_target.reshape(s, d)

    small = jnp.concatenate([a_conv_w[0], b_ln_g, b_ln_b, jnp.zeros((3, es), F32)], axis=0)
    wa, woa, small_all = _exchange_alone(
        _gather_exchange([a_w_in[0].astype(BF16), a_w_out[0].astype(BF16), small], [True, False, True]), "gather_a_weights")
    gather_b = _gather_exchange([b_w_in[0].astype(BF16), b_w_out[0].astype(BF16)], [True, False])
    conv_w, ln_g, ln_b = small_all[0:3], small_all[3:4], small_all[4:5]
    bsf = jnp.repeat(b_b_s[0].T, e // GROUPS, axis=1)
    wt, wtt = _mask_transpose_ws(b_w_s[0], "mask_w_s")

    mod, c_all = _mod_vectors(c, mod_w, mod_b)
    shift0, scale0, gate0 = mod[0:1, 0:d], mod[0:1, d:2 * d], mod[0:1, 2 * d:]
    shift1, scale1, gate1 = mod[1:2, 0:d], mod[1:2, d:2 * d], mod[1:2, 2 * d:]
    g0, g1, fg = norm_g[0:1], norm_g[1:2], final_g.reshape(1, d)

    proj_a, h0, x1, br_a, tails, (wb, wob) = _layer_a_fwd(
        x0, g0, scale0, shift0, gate0, wa, conv_w, a_conv_b, woa, "a_fwd", gather_b)
    proj_b, h1, dx2, loss_acc, dfg, dgate1, _ = _layer_b_fwd_loss(
        x1, tgt, g1, scale1, shift1, gate1, fg, wb, ln_g, ln_b, wt, bsf, wob, "b_fwd_loss")
    loss = lax.psum(loss_acc[0, 0], ("x", "y", "c"))

    dproj_b, y_b, dx1, dws, dbs, dlg, dlb, dshift1, dscale1, dg1 = _layer_b_bwd(
        proj_b, dx2, x1, gate1, g1, scale1, ln_g, ln_b, wt, wtt, bsf, wob, wb, "b_bwd")
    gs_b_out, _ = _matmul_tn(y_b, dx2, gate1, True, "b_w_out_grad")
    gs_b_in, _ = _matmul_tn(h1, dproj_b, jnp.ones((1, dproj_b.shape[1]), F32), False, "b_w_in_grad")
    dproj_a, y_a, dgate0, dcb, dcw, (gr_b_in, gr_b_out) = _conv_mixer_bwd(
        proj_a, dx1, br_a, tails, conv_w, a_conv_b, gate0, woa, "a_mixer_bwd", _scatter_exchange([gs_b_in, gs_b_out]))
    gs_a_out, _ = _matmul_tn(y_a, dx1, gate0, True, "a_w_out_grad")
    gs_a_in, (gr_a_out,) = _matmul_tn(h0, dproj_a, jnp.ones((1, dproj_a.shape[1]), F32), False, "a_w_in_grad",
                                      _scatter_exchange([gs_a_out]))
    dx0, dshift0, dscale0, dg0, (gr_a_in,) = _matmul_nt_norm_bwd(dproj_a, wa, x0, dx1, g0, scale0, "a_in_bwd",
                                                                 _scatter_exchange([gs_a_in]))

    pieces = [dshift0, dscale0, dgate0, dshift1, dscale1, dgate1, dg0, dg1, dcb, dcw[0:3], dlg, dlb, dfg,
              dbs[:, 0:GROUPS].T, dws]
    shapes = [p.shape for p in pieces]
    packed = _pack(pieces)
    packed_all, = _exchange_alone(_gather_exchange([packed], [False]), "gather_small_grads")
    packed_all = packed_all.reshape(NDEV, -1, PACK_W)
    total =_sum_devices(packed_all, "sum_small_grads").reshape(-1)
    (t_sh0, t_sc0, t_ga0, t_sh1, t_sc1, t_ga1, t_g0, t_g1, t_cb, t_cw, t_lg, t_lb, t_fg, t_bs, t_ws) = _unpack(total, shapes)
    grad_mod_b = jnp.concatenate([jnp.concatenate([t_sh0, t_sc0, t_ga0], axis=1),
                                  jnp.concatenate([t_sh1, t_sc1, t_ga1], axis=1)], axis=0)
    grad_norm_g = jnp.concatenate([t_g0, t_g1], axis=0)
    dmod_all = packed_all.reshape(NDEV, -1)[:, 0:6 * d].reshape(NDEV, 2, 3 * d).transpose(1, 0, 2)
    dmod_mine = lax.dynamic_slice_in_dim(dmod_all, me * w3, w3, axis=2)
    grad_mod_w = _mod_w_grad(c_all.T, dmod_mine, "mod_w_grad")
    grad_a_conv_w = lax.dynamic_slice_in_dim(t_cw, me * es, es, axis=1)
    grad_b_ln_g = lax.dynamic_slice_in_dim(t_lg, me * es, es, axis=1)
    grad_b_ln_b = lax.dynamic_slice_in_dim(t_lb, me * es, es, axis=1)

    def big(parts, w, m, v, name):
        shp = w.shape
        r2 = lambda t_: t_.reshape(-1, shp[-1])
        g, dl, nm, nv = _adamw_reduce(parts, r2(w), r2(m), r2(v), name)
        return tuple(t_.reshape(shp) for t_ in (g, dl, nm, nv))

    res = {}
    res["mod_w"] = big(grad_mod_w.reshape(1, -1, w3), mod_w, m_mod_w, v_mod_w, "adamw_mod_w")
    res["a_w_in"] = big(gr_a_in, a_w_in, m_a_w_in, v_a_w_in, "adamw_a_w_in")
    res["a_w_out"] = big(gr_a_out, a_w_out, m_a_w_out, v_a_w_out, "adamw_a_w_out")
    res["b_w_in"] = big(gr_b_in, b_w_in, m_b_w_in, v_b_w_in, "adamw_b_w_in")
    res["b_w_out"] = big(gr_b_out, b_w_out, m_b_w_out, v_b_w_out, "adamw_b_w_out")

    small_names = ["mod_b", "norm_g", "a_conv_w", "a_conv_b", "b_ln_g", "b_ln_b", "b_w_s", "b_b_s", "final_g"]
    small_g = [grad_mod_b, grad_norm_g, grad_a_conv_w, t_cb, grad_b_ln_g, grad_b_ln_b, t_ws, t_bs, t_fg]
    small_w = [mod_b, norm_g, a_conv_w, a_conv_b, b_ln_g, b_ln_b, b_w_s, b_b_s, final_g]
    small_m = [m_mod_b, m_norm_g, m_a_conv_w, m_a_conv_b, m_b_ln_g, m_b_ln_b, m_b_w_s, m_b_b_s, m_final_g]
    small_v = [v_mod_b, v_norm_g, v_a_conv_w, v_a_conv_b, v_b_ln_g, v_b_ln_b, v_b_w_s, v_b_b_s, v_final_g]
    as2d = lambda t_: t_.reshape(-1, t_.shape[-1])
    dls, nms, nvs = _adamw_small([as2d(t_) for t_ in small_g], [as2d(t_) for t_ in small_w],
                                 [as2d(t_) for t_ in small_m], [as2d(t_) for t_ in small_v], "adamw_small")
    for a, nme in enumerate(small_names):
        shp = small_w[a].shape
        res[nme] = (small_g[a].reshape(shp), dls[a].reshape(shp), nms[a].reshape(shp), nvs[a].reshape(shp))

    order = ["mod_w", "mod_b", "norm_g", "a_w_in", "a_conv_w", "a_conv_b", "a_w_out", "b_w_in", "b_ln_g", "b_ln_b",
             "b_w_s", "b_b_s", "b_w_out", "final_g"]
    return (loss, dx0.reshape(x.shape), *[res[k][0] for k in order], *[res[k][1] for k in order],
            *[res[k][2] for k in order], *[res[k][3] for k in order])
```

```python
import jax
import jax.numpy as jnp
from jax import lax
from jax.experimental import pallas as pl
from jax.experimental.pallas import tpu as pltpu

NDEV = 8
CHUNK = 128
GROUPS = 8
RMS_EPS = 1e-6
LN_EPS = 1e-5
ADAM_LR, ADAM_B1, ADAM_B2, ADAM_EPS, ADAM_WD, ADAM_STEP = 0.001, 0.9, 0.999, 1e-08, 0.01, 10
V7X_VMEM_BYTES = 64 * 1024 * 1024
VMEM_LIMIT = V7X_VMEM_BYTES - 8 * 1024 * 1024
TN_VMEM_BUDGET = 46 * 1024 * 1024
PACK_W = 1024
F32, BF16 = jnp.float32, jnp.bfloat16
MESH = pl.DeviceIdType.MESH
RSQRT2 = 0.7071067811865476
INV_SQRT_2PI = 0.3989422804014327
NT_DIMS = (((1,), (1,)), ((), ()))
TN_DIMS = (((0,), (0,)), ((), ()))


def _params(sem=None):
    return pltpu.CompilerParams(dimension_semantics=sem, vmem_limit_bytes=VMEM_LIMIT)


def _vmem():
    return pl.BlockSpec(memory_space=pltpu.VMEM)


def _hbm():
    return pl.BlockSpec(memory_space=pltpu.HBM)


def _full(shape):
    return pl.BlockSpec(shape, lambda *_: (0,) * len(shape))


def _pos():
    return lax.axis_index("x"), lax.axis_index("y"), lax.axis_index("c")


def _index(p):
    return 4 * p[0] + 2 * p[1] + p[2]


def _peer(k):
    x, y, c = _pos()
    return ((1 - x) if (k >> 2) & 1 else x, (1 - y) if (k >> 1) & 1 else y, (1 - c) if k & 1 else c)


def _silu(z):
    sg = jax.nn.sigmoid(z)
    return z * sg, sg * (1.0 + z * (1.0 - sg))


def _gelu(v):
    phi = 0.5 * (1.0 + lax.erf(v * RSQRT2))
    return v * phi, phi + v * (jnp.exp(-0.5 * v * v) * INV_SQRT_2PI)


def _colsum(v):
    return jnp.sum(v, axis=0, keepdims=True)


def _rowsum(v):
    return jnp.sum(v, axis=-1, keepdims=True)


def _gather_all_vmem(slab_ref, send_sems, recv_sems, base):
    me = _index(_pos())
    sends = []
    for k in range(1, NDEV):
        cp = pltpu.make_async_remote_copy(
            src_ref=slab_ref.at[me], dst_ref=slab_ref.at[me],
            send_sem=send_sems.at[base + k - 1], recv_sem=recv_sems.at[base + k - 1],
            device_id=_peer(k), device_id_type=MESH)
        cp.start()
        sends.append(cp)
    for k in range(1, NDEV):
        src = _index(_peer(k))
        pltpu.make_async_remote_copy(
            src_ref=slab_ref.at[src], dst_ref=slab_ref.at[src],
            send_sem=send_sems.at[base + k - 1], recv_sem=recv_sems.at[base + k - 1],
            device_id=_peer(k), device_id_type=MESH).wait_recv()
    for cp in sends:
        cp.wait_send()


def _mod_vectors(c, mod_w, mod_b, ex):
    n_layers, d, w3 = mod_w.shape
    r_in, r_out = len(ex.arrays), len(ex.out_shapes)

    def body(*refs):
        c_ref, mw_ref, mb_ref = refs[:3]
        ex_ins = refs[3:3 + r_in]
        mod_ref, call_ref = refs[3 + r_in:5 + r_in]
        ex_outs = refs[5 + r_in:5 + r_in + r_out]
        cslab, pslab, send_sems, recv_sems = refs[5 + r_in + r_out:9 + r_in + r_out]
        ex_sems = refs[9 + r_in + r_out:]
        ex.start(ex_ins, ex_outs, ex_sems)
        me = _index(_pos())
        cv = c_ref[...]
        cslab[me] = jnp.broadcast_to(cv * jax.nn.sigmoid(cv), (8, d))
        _gather_all_vmem(cslab, send_sems, recv_sems, 0)
        c_all = jnp.concatenate([cslab[k, 0:1, :] for k in range(NDEV)], axis=0)
        call_ref[...] = c_all
        for i in range(n_layers):
            pslab[me, i * NDEV:(i + 1) * NDEV, :] = jnp.dot(
                c_all, mw_ref[i], preferred_element_type=F32, precision=lax.Precision.HIGHEST)
        _gather_all_vmem(pslab, send_sems, recv_sems, NDEV - 1)
        for i in range(n_layers):
            for k in range(NDEV):
                mod_ref[i:i + 1, k * w3:(k + 1) * w3] = (
                    pslab[k, pl.ds(i * NDEV + me, 1), :] + mb_ref[i:i + 1, k * w3:(k + 1) * w3])
        if ex.middle is not None:
            ex.middle(ex_ins, ex_outs, ex_sems)
        ex.finish(ex_ins, ex_outs, ex_sems)

    out = pl.pallas_call(
        body, name="mod_vectors",
        out_shape=(jax.ShapeDtypeStruct((n_layers, 3 * d), F32), jax.ShapeDtypeStruct((NDEV, d), F32), *ex.out_shapes),
        in_specs=[_vmem(), _vmem(), _vmem()] + [_hbm()] * r_in, out_specs=(_vmem(), _vmem(), *([_hbm()] * r_out)),
        scratch_shapes=[pltpu.VMEM((NDEV, 8, d), F32), pltpu.VMEM((NDEV, n_layers * NDEV, w3), F32),
                        pltpu.SemaphoreType.DMA((2 * (NDEV - 1),)), pltpu.SemaphoreType.DMA((2 * (NDEV - 1),)), *ex.sems],
        compiler_params=pltpu.CompilerParams(vmem_limit_bytes=VMEM_LIMIT),
    )(c, mod_w, mod_b, *ex.arrays)
    return out[0], out[1], out[2:]


class _Exchange:
    def __init__(self, arrays, out_shapes, sems, start, middle, finish):
        self.arrays, self.out_shapes, self.sems = list(arrays), list(out_shapes), list(sems)
        self.start, self.middle, self.finish = start, middle, finish


def _gather_exchange(shards, by_cols):
    n = len(shards)
    shapes = [sh.shape for sh in shards]

    def tools(ins, outs, sems):
        send_sems, recv_sems, local_sems = sems
        x, y, c = _pos()
        chips = [(1 - x, y), (x, 1 - y), (1 - x, 1 - y)]

        def place(a, block):
            r, cc = shapes[a]
            if by_cols[a]:
                return outs[a].at[:, pl.ds(_index(block) * cc, cc)]
            return outs[a].at[pl.ds(_index(block) * r, r), :]

        def copy(a, k, block, to, src=None):
            dst = place(a, block)
            return pltpu.make_async_remote_copy(
                src_ref=dst if src is None else src, dst_ref=dst,
                send_sem=send_sems.at[a * 7 + k], recv_sem=recv_sems.at[a * 7 + k],
                device_id=to, device_id_type=MESH)

        mine = [pltpu.make_async_copy(ins[a], place(a, (x, y, c)), local_sems.at[a]) for a in range(n)]
        first = []
        for a in range(n):
            first.append(copy(a, 0, (x, y, c), (x, y, 1 - c), src=ins[a]))
            first += [copy(a, 1 + j, (x, y, c), (*chip, c), src=ins[a]) for j, chip in enumerate(chips)]
        passed = [copy(a, 4 + j, (*chip, c), (x, y, 1 - c)) for j, chip in enumerate(chips) for a in range(n)]
        return (x, y, c), chips, copy, mine, first, passed

    def start(ins, outs, sems):
        _, _, _, mine, first, _ = tools(ins, outs, sems)
        for cp in mine + first:
            cp.start()

    def middle(ins, outs, sems):
        (x, y, c), chips, copy, _, _, passed = tools(ins, outs, sems)
        for j, chip in enumerate(chips):
            for a in range(n):
                copy(a, 1 + j, (*chip, c), (x, y, c)).wait_recv()
                passed[j * n + a].start()

    def finish(ins, outs, sems):
        (x, y, c), chips, copy, mine, first, passed = tools(ins, outs, sems)
        for a in range(n):
            copy(a, 0, (x, y, 1 - c), (x, y, c)).wait_recv()
        for j, chip in enumerate(chips):
            for a in range(n):
                copy(a, 4 + j, (*chip, 1 - c), (x, y, c)).wait_recv()
        for cp in first + passed:
            cp.wait_send()
        for cp in mine:
            cp.wait()

    out_shapes = [jax.ShapeDtypeStruct((r, NDEV * cc) if bc else (NDEV * r, cc), sh.dtype)
                  for (r, cc), bc, sh in zip(shapes, by_cols, shards)]
    sems = [pltpu.SemaphoreType.DMA((7 * n,)), pltpu.SemaphoreType.DMA((7 * n,)), pltpu.SemaphoreType.DMA((n,))]
    return _Exchange(shards, out_shapes, sems, start, middle, finish)


def _scatter_exchange(parts):
    n = len(parts)

    def tools(ins, outs, sems):
        send_sems, recv_sems, local_sems = sems
        me = _index(_pos())
        mine = [pltpu.make_async_copy(ins[a].at[me], outs[a].at[me], local_sems.at[a]) for a in range(n)]
        sends, arrivals = [], []
        for k in range(1, NDEV):
            peer = _peer(k)
            for a in range(n):
                pair = dict(send_sem=send_sems.at[a * 7 + k - 1], recv_sem=recv_sems.at[a * 7 + k - 1],
                            device_id=peer, device_id_type=MESH)
                sends.append(pltpu.make_async_remote_copy(src_ref=ins[a].at[_index(peer)], dst_ref=outs[a].at[me], **pair))
                slot = outs[a].at[_index(peer)]
                arrivals.append(pltpu.make_async_remote_copy(src_ref=slot, dst_ref=slot, **pair))
        return mine, sends, arrivals

    def start(ins, outs, sems):
        mine, sends, _ = tools(ins, outs, sems)
        for cp in mine + sends:
            cp.start()

    def finish(ins, outs, sems):
        mine, sends, arrivals = tools(ins, outs, sems)
        for cp in arrivals:
            cp.wait_recv()
        for cp in sends:
            cp.wait_send()
        for cp in mine:
            cp.wait()

    out_shapes = [jax.ShapeDtypeStruct(p.shape, p.dtype) for p in parts]
    sems = [pltpu.SemaphoreType.DMA((7 * n,)), pltpu.SemaphoreType.DMA((7 * n,)), pltpu.SemaphoreType.DMA((n,))]
    return _Exchange(parts, out_shapes, sems, start, None, finish)


def _carry(ex, body, n_in, n_out, first, middle, last):
    if ex is None:
        return body
    r_in, r_out = len(ex.arrays), len(ex.out_shapes)

    def wrapped(*refs):
        ins, rins = refs[:n_in], refs[n_in:n_in + r_in]
        outs = refs[n_in + r_in:n_in + r_in + n_out]
        routs = refs[n_in + r_in + n_out:n_in + r_in + n_out + r_out]
        rest = refs[n_in + r_in + n_out + r_out:]
        scratch, sems = rest[:len(rest) - len(ex.sems)], rest[len(rest) - len(ex.sems):]

        @pl.when(first())
        def _():
            ex.start(rins, routs, sems)

        if ex.middle is not None:
            @pl.when(middle())
            def _():
                ex.middle(rins, routs, sems)

        body(*ins, *outs, *scratch)

        @pl.when(last())
        def _():
            ex.finish(rins, routs, sems)

    return wrapped


def _carried(ex):
    if ex is None:
        return [], [], [], [], []
    return ex.arrays, [_hbm()] * len(ex.arrays), ex.out_shapes, [_hbm()] * len(ex.out_shapes), ex.sems


def _resident(shape):
    return pl.BlockSpec(shape, lambda *_: (0,) * len(shape), pipeline_mode=pl.Buffered(1))


def _norm_modulate(x_ref, g_ref, sc_ref, sh_ref):
    xv = x_ref[...]
    r = lax.rsqrt(jnp.mean(xv * xv, axis=-1, keepdims=True) + RMS_EPS)
    return ((xv * r) * g_ref[...] * (1.0 + sc_ref[...]) + sh_ref[...]).astype(BF16)


def _conv_taps(cx, t6, t7, row):
    p1 = jnp.where(row == 0, t7, pltpu.roll(cx, 1, 0))
    p2 = jnp.where(row == 0, t6, jnp.where(row == 1, t7, pltpu.roll(cx, 2, 0)))
    return p1, p2


def _layer_a_fwd(x, g, scale, shift, gate, wi, cw, cb, wo, name, ex=None):
    s, d = x.shape
    e = wo.shape[0]
    t = min(s, 256)
    n_t = s // t
    cwid = min(e, 512)

    def body(x_ref, g_ref, sc_ref, sh_ref, gate_ref, wi_ref, cw_ref, cb_ref, wo_ref,
             proj_ref, h_ref, x1_ref, br_ref, tails_ref, y_scr, tail_scr):
        @pl.when(pl.program_id(0) == 0)
        def _():
            tail_scr[...] = jnp.zeros_like(tail_scr)
        h_ref[...] = _norm_modulate(x_ref, g_ref, sc_ref, sh_ref)
        row = lax.broadcasted_iota(jnp.int32, (t, cwid), 0)

        def project(c0):
            v = jnp.dot(h_ref[...], wi_ref[:, c0:c0 + cwid], preferred_element_type=F32)
            proj_ref[:, c0:c0 + cwid] = v.astype(BF16)
            return v

        for c0 in range(0, e, cwid):
            sl = slice(c0, c0 + cwid)
            bg, z = project(c0), project(3 * e + c0)
            cx = project(e + c0) * project(2 * e + c0)
            p1, p2 = _conv_taps(cx, tail_scr[6:7, sl], tail_scr[7:8, sl], row)
            conv = cb_ref[:, sl] + cw_ref[2:3, sl] * cx + cw_ref[0:1, sl] * p2 + cw_ref[1:2, sl] * p1
            y_scr[:, sl] = (_silu(z)[0] * bg * conv).astype(BF16)
            tail_scr[:, sl] = cx[t - 8:t, :]
        tails_ref[...] = tail_scr[...]
        br = jnp.dot(y_scr[...], wo_ref[...], preferred_element_type=F32)
        x1_ref[...] = x_ref[...] + gate_ref[...] * br
        br_ref[...] = br.astype(BF16)

    step = lambda k: (lambda: pl.program_id(0) == k)
    body = _carry(ex, body, 9, 5, step(0), step(n_t // 2), step(n_t - 1))
    ex_args, ex_in, ex_shapes, ex_out, ex_sems = _carried(ex)
    tok = pl.BlockSpec((t, d), lambda i: (i, 0))
    out = pl.pallas_call(
        body, name=name, grid=(n_t,),
        out_shape=(jax.ShapeDtypeStruct((s, 4 * e), BF16), jax.ShapeDtypeStruct((s, d), BF16),
                   jax.ShapeDtypeStruct((s, d), F32), jax.ShapeDtypeStruct((s, d), BF16),
                   jax.ShapeDtypeStruct((n_t, 8, e), F32), *ex_shapes),
        in_specs=[tok, _full((1, d)), _full((1, d)), _full((1, d)), _full((1, d)), _resident((d, 4 * e)),
                  _full((3, e)), _full((1, e)), _resident((e, d)), *ex_in],
        out_specs=(pl.BlockSpec((t, 4 * e), lambda i: (i, 0)), tok, tok, tok,
                   pl.BlockSpec((None, 8, e), lambda i: (i, 0, 0)), *ex_out),
        scratch_shapes=[pltpu.VMEM((t, e), BF16), pltpu.VMEM((8, e), F32), *ex_sems],
        compiler_params=_params(("arbitrary",)),
    )(x, g, scale, shift, gate, wi, cw, cb, wo, *ex_args)
    return (*out[:5], out[5:])


def _ln_stats(v_pre, v_scr, dgv_scr, t, e):
    gw = e // GROUPS
    s1 = jnp.zeros((t, 1), F32)
    for g in range(GROUPS):
        v, dgv = _gelu(v_pre(g))
        v_scr[:, g * gw:(g + 1) * gw] = v
        if dgv_scr is not None:
            dgv_scr[:, g * gw:(g + 1) * gw] = dgv.astype(dgv_scr.dtype)
        s1 = s1 + _rowsum(v)
    mu = s1 * (1.0 / e)
    s2 = jnp.zeros((t, 1), F32)
    for g in range(GROUPS):
        dv = v_scr[:, g * gw:(g + 1) * gw] - mu
        s2 = s2 + _rowsum(dv * dv)
    return mu, lax.rsqrt(s2 * (1.0 / e) + LN_EPS)


def _layer_b_fwd_loss(x1, tgt, g1, scale, shift, gate, fg, wi, lng, lnb, wt, bsf, wo, name, ex=None):
    s, d = x1.shape
    e = wo.shape[0]
    gw = e // GROUPS
    t = min(s, 256)
    n_t = s // t

    def body(x1_ref, tgt_ref, g_ref, sc_ref, sh_ref, gate_ref, fg_ref, wi_ref, lng_ref, lnb_ref, wt_ref, bsf_ref, wo_ref,
             proj_ref, h_ref, dx2_ref, loss_ref, dfg_ref, dgate_ref, v_scr, y_scr):
        @pl.when(pl.program_id(0) == 0)
        def _():
            loss_ref[...] = jnp.zeros_like(loss_ref)
            dfg_ref[...] = jnp.zeros_like(dfg_ref)
            dgate_ref[...] = jnp.zeros_like(dgate_ref)
        h_ref[...] = _norm_modulate(x1_ref, g_ref, sc_ref, sh_ref)

        def project(c0):
            v = jnp.dot(h_ref[...], wi_ref[:, c0:c0 + gw], preferred_element_type=F32)
            proj_ref[:, c0:c0 + gw] = v.astype(BF16)
            return v

        mu, rs = _ln_stats(lambda g: project(e + g * gw), v_scr, None, t, e)
        for g in range(GROUPS):
            gs = slice(g * gw, (g + 1) * gw)
            vn = (((v_scr[:, gs] - mu) * rs) * lng_ref[:, gs] + lnb_ref[:, gs]).astype(BF16)
            u = _gelu(project(g * gw))[0]
            sz = _silu(project(2 * e + g * gw))[0]
            for ch in range(t // CHUNK):
                rows = slice(ch * CHUNK, (ch + 1) * CHUNK)
                mixed = jnp.dot(wt_ref[g], vn[rows], preferred_element_type=F32) + bsf_ref[:, gs]
                y_scr[rows, gs] = (sz[rows] * (u[rows] * mixed)).astype(BF16)
        br = jnp.dot(y_scr[...], wo_ref[...], preferred_element_type=F32)
        x2 = x1_ref[...] + gate_ref[...] * br
        r2 = lax.rsqrt(jnp.mean(x2 * x2, axis=-1, keepdims=True) + RMS_EPS)
        xn = x2 * r2
        diff = xn * fg_ref[...] - tgt_ref[...]
        loss_ref[...] += jnp.broadcast_to(0.5 * _colsum(jnp.mean(diff * diff, axis=-1, keepdims=True)), loss_ref.shape)
        dout = diff * (1.0 / d)
        dfg_ref[...] += _colsum(dout * xn)
        dxn = dout * fg_ref[...]
        dx2 = r2 * (dxn - xn * jnp.mean(dxn * xn, axis=-1, keepdims=True))
        dx2_ref[...] = dx2
        dgate_ref[...] += _colsum(dx2 * br)

    step = lambda k: (lambda: pl.program_id(0) == k)
    body = _carry(ex, body, 13, 6, step(0), step(n_t // 2), step(n_t - 1))
    ex_args, ex_in, ex_shapes, ex_out, ex_sems = _carried(ex)
    tok = pl.BlockSpec((t, d), lambda i: (i, 0))
    vec = _full((1, d))
    out = pl.pallas_call(
        body, name=name, grid=(n_t,),
        out_shape=(jax.ShapeDtypeStruct((s, 3 * e), BF16), jax.ShapeDtypeStruct((s, d), BF16),
                   jax.ShapeDtypeStruct((s, d), F32), jax.ShapeDtypeStruct((8, 128), F32),
                   jax.ShapeDtypeStruct((1, d), F32), jax.ShapeDtypeStruct((1, d), F32), *ex_shapes),
        in_specs=[tok, tok, vec, vec, vec, vec, vec, _resident((d, 3 * e)), _full((1, e)), _full((1, e)),
                  _full((GROUPS, CHUNK, CHUNK)), _resident((CHUNK, e)), _resident((e, d)), *ex_in],
        out_specs=(pl.BlockSpec((t, 3 * e), lambda i: (i, 0)), tok, tok, _full((8, 128)), vec, vec, *ex_out),
        scratch_shapes=[pltpu.VMEM((t, e), F32), pltpu.VMEM((t, e), BF16), *ex_sems],
        compiler_params=_params(("arbitrary",)),
    )(x1, tgt, g1, scale, shift, gate, fg, wi, lng, lnb, wt, bsf, wo, *ex_args)
    return (*out[:6], out[6:])


def _norm_modulate_bwd(dh, x_ref, dres_ref, g_ref, sc_ref, dx_ref, dsh_ref, p_scr):
    xv = x_ref[...]
    r = lax.rsqrt(jnp.mean(xv * xv, axis=-1, keepdims=True) + RMS_EPS)
    xn = xv * r
    dsh_ref[...] += _colsum(dh)
    p_scr[...] += _colsum(dh * xn)
    dxn = dh * (g_ref[...] * (1.0 + sc_ref[...]))
    dx_ref[...] = r * (dxn - xn * jnp.mean(dxn * xn, axis=-1, keepdims=True)) + dres_ref[...]


def _layer_b_bwd(proj, dx2, x1, gate, g1, scale, lng, lnb, wt, wtt, bsf, wo, wi, name):
    s, e3 = proj.shape
    e = e3 // 3
    d = dx2.shape[1]
    gw = e // GROUPS
    t = min(s, 256)
    n_t = s // t

    def body(p_ref, dx_ref, x1_ref, gate_ref, g_ref, sc_ref, lng_ref, lnb_ref, wt_ref, wtt_ref, bsf_ref, wo_ref, wi_ref,
             dp_ref, y_ref, dx1_ref, dws_ref, dbs_ref, dlg_ref, dlb_ref, dsh_ref, dsc_ref, dg_ref,
             v_scr, dgv_scr, dbr_scr, dvn_scr, dbs_scr, p_scr):
        @pl.when(pl.program_id(0) == 0)
        def _():
            dws_ref[...] = jnp.zeros_like(dws_ref)
            dlg_ref[...] = jnp.zeros_like(dlg_ref)
            dlb_ref[...] = jnp.zeros_like(dlb_ref)
            dsh_ref[...] = jnp.zeros_like(dsh_ref)
            dbs_scr[...] = jnp.zeros_like(dbs_scr)
            p_scr[...] = jnp.zeros_like(p_scr)
        dbr_scr[...] = (dx_ref[...] * gate_ref[...]).astype(BF16)
        mu, rs = _ln_stats(lambda g: p_ref[:, e + g * gw:e + (g + 1) * gw].astype(F32), v_scr, dgv_scr, t, e)
        tril = (lax.broadcasted_iota(jnp.int32, (CHUNK, CHUNK), 0) >= lax.broadcasted_iota(jnp.int32, (CHUNK, CHUNK), 1))
        c1 = jnp.zeros((t, 1), F32)
        c2 = jnp.zeros((t, 1), F32)
        for g in range(GROUPS):
            gs = slice(g * gw, (g + 1) * gw)
            vhat = (v_scr[:, gs] - mu) * rs
            v_scr[:, gs] = vhat
            lg = lng_ref[:, gs]
            vn = (vhat * lg + lnb_ref[:, gs]).astype(BF16)
            dy_g = lax.dot_general(dbr_scr[...], wo_ref[g * gw:(g + 1) * gw, :], NT_DIMS, preferred_element_type=F32)
            for ch in range(t // CHUNK):
                rows = slice(ch * CHUNK, (ch + 1) * CHUNK)
                mixed = jnp.dot(wt_ref[g], vn[rows], preferred_element_type=F32) + bsf_ref[:, gs]
                u, dgu = _gelu(p_ref[rows, g * gw:(g + 1) * gw].astype(F32))
                sz, dsz = _silu(p_ref[rows, 2 * e + g * gw:2 * e + (g + 1) * gw].astype(F32))
                sgate = u * mixed
                y_ref[rows, gs] = (sz * sgate).astype(BF16)
                dy = dy_g[rows]
                dp_ref[rows, 2 * e + g * gw:2 * e + (g + 1) * gw] = (dy * sgate * dsz).astype(BF16)
                ds = dy * sz
                dp_ref[rows, gs] = (ds * mixed * dgu).astype(BF16)
                dm = ds * u
                dbs_scr[:, gs] += dm
                dmb = dm.astype(BF16)
                dws_ref[g] += jnp.where(tril, lax.dot_general(dmb, vn[rows], NT_DIMS, preferred_element_type=F32), 0.0)
                dvn_scr[rows, gs] = jnp.dot(wtt_ref[g], dmb, preferred_element_type=F32)
            dvn = dvn_scr[:, gs]
            dlb_ref[:, gs] += _colsum(dvn)
            dlg_ref[:, gs] += _colsum(dvn * vhat)
            dvh = dvn * lg
            c1 = c1 + _rowsum(dvh)
            c2 = c2 + _rowsum(dvh * vhat)
        c1 = c1 * (1.0 / e)
        c2 = c2 * (1.0 / e)
        for g in range(GROUPS):
            gs = slice(g * gw, (g + 1) * gw)
            dv = rs * (dvn_scr[:, gs] * lng_ref[:, gs] - c1 - v_scr[:, gs] * c2)
            dp_ref[:, e + g * gw:e + (g + 1) * gw] = (dv * dgv_scr[:, gs]).astype(BF16)
        dh = lax.dot_general(dp_ref[...], wi_ref[...], NT_DIMS, preferred_element_type=F32)
        _norm_modulate_bwd(dh, x1_ref, dx_ref, g_ref, sc_ref, dx1_ref, dsh_ref, p_scr)

        @pl.when(pl.program_id(0) == n_t - 1)
        def _():
            lane = lax.broadcasted_iota(jnp.int32, (CHUNK, 128), 1)
            acc = jnp.zeros((CHUNK, 128), F32)
            for g in range(GROUPS):
                acc = acc + jnp.where(lane == g, _rowsum(dbs_scr[:, g * gw:(g + 1) * gw]), 0.0)
            dbs_ref[...] = acc
            dsc_ref[...] = p_scr[...] * g_ref[...]
            dg_ref[...] = p_scr[...] * (1.0 + sc_ref[...])

    tok = pl.BlockSpec((t, d), lambda i: (i, 0))
    vec, evec, ws = _full((1, d)), _full((1, e)), _full((GROUPS, CHUNK, CHUNK))
    vshape = jax.ShapeDtypeStruct((1, d), F32)
    return pl.pallas_call(
        body, name=name, grid=(n_t,),
        out_shape=(jax.ShapeDtypeStruct((s, e3), BF16), jax.ShapeDtypeStruct((s, e), BF16), jax.ShapeDtypeStruct((s, d), F32),
                   jax.ShapeDtypeStruct((GROUPS, CHUNK, CHUNK), F32), jax.ShapeDtypeStruct((CHUNK, 128), F32),
                   jax.ShapeDtypeStruct((1, e), F32), jax.ShapeDtypeStruct((1, e), F32), vshape, vshape, vshape),
        in_specs=[pl.BlockSpec((t, e3), lambda i: (i, 0)), tok, tok, vec, vec, vec, evec, evec, ws, ws,
                  _resident((CHUNK, e)), _resident((e, d)), _resident((d, e3))],
        out_specs=(pl.BlockSpec((t, e3), lambda i: (i, 0)), pl.BlockSpec((t, e), lambda i: (i, 0)), tok,
                   ws, _full((CHUNK, 128)), evec, evec, vec, vec, vec),
        scratch_shapes=[pltpu.VMEM((t, e), F32), pltpu.VMEM((t, e), BF16), pltpu.VMEM((t, d), BF16),
                        pltpu.VMEM((t, e), F32), pltpu.VMEM((CHUNK, e), F32), pltpu.VMEM((1, d), F32)],
        compiler_params=_params(("arbitrary",)),
    )(proj, dx2, x1, gate, g1, scale, lng, lnb, wt, wtt, bsf, wo, wi)


def _conv_mixer_bwd(proj, dx1, br, tails, cw, cb, gate, wo, name, ex=None):
    s, e4 = proj.shape
    e = e4 // 4
    d = dx1.shape[1]
    t = min(s, 256)
    n_t = s // t
    cwid = min(e, 512)

    def body(p_ref, dx_ref, br_ref, tails_ref, cw_ref, cb_ref, gate_ref, wo_ref,
             dp_ref, y_ref, dgate_ref, dcb_ref, dcw_ref, dy_scr, head_scr):
        i = pl.program_id(0)

        @pl.when(i == 0)
        def _():
            dgate_ref[...] = jnp.zeros_like(dgate_ref)
            dcb_ref[...] = jnp.zeros_like(dcb_ref)
            dcw_ref[...] = jnp.zeros_like(dcw_ref)
            head_scr[...] = jnp.zeros_like(head_scr)
        dx = dx_ref[...]
        dgate_ref[...] += _colsum(dx * br_ref[...].astype(F32))
        dy_scr[...] = lax.dot_general((dx * gate_ref[...]).astype(BF16), wo_ref[...], NT_DIMS,
                                      preferred_element_type=F32)
        row = lax.broadcasted_iota(jnp.int32, (t, cwid), 0)
        has_prev = (i < n_t - 1).astype(F32)
        for c0 in range(0, e, cwid):
            sl = slice(c0, c0 + cwid)
            bg = p_ref[:, c0:c0 + cwid].astype(F32)
            cg = p_ref[:, e + c0:e + c0 + cwid].astype(F32)
            xin = p_ref[:, 2 * e + c0:2 * e + c0 + cwid].astype(F32)
            z = p_ref[:, 3 * e + c0:3 * e + c0 + cwid].astype(F32)
            cx = cg * xin
            p1, p2 = _conv_taps(cx, tails_ref[6:7, sl] * has_prev, tails_ref[7:8, sl] * has_prev, row)
            w0, w1, w2 = cw_ref[0:1, sl], cw_ref[1:2, sl], cw_ref[2:3, sl]
            conv = cb_ref[:, sl] + w2 * cx + w0 * p2 + w1 * p1
            sz, dsz = _silu(z)
            dy = dy_scr[:, sl]
            y_ref[:, sl] = (sz * bg * conv).astype(BF16)
            dp_ref[:, 3 * e + c0:3 * e + c0 + cwid] = (dy * bg * conv * dsz).astype(BF16)
            dp_ref[:, c0:c0 + cwid] = (dy * sz * conv).astype(BF16)
            dconv = dy * sz * bg
            dcb_ref[:, sl] += _colsum(dconv)
            dcw_ref[2:3, sl] += _colsum(dconv * cx)
            dcw_ref[1:2, sl] += _colsum(dconv * p1)
            dcw_ref[0:1, sl] += _colsum(dconv * p2)
            h0, h1 = head_scr[0:1, sl], head_scr[1:2, sl]
            n1 = jnp.where(row == t - 1, h0, pltpu.roll(dconv, t - 1, 0))
            n2 = jnp.where(row == t - 2, h0, jnp.where(row == t - 1, h1, pltpu.roll(dconv, t - 2, 0)))
            dcx = w2 * dconv + w1 * n1 + w0 * n2
            dp_ref[:, e + c0:e + c0 + cwid] = (dcx * xin).astype(BF16)
            dp_ref[:, 2 * e + c0:2 * e + c0 + cwid] = (dcx * cg).astype(BF16)
            head_scr[:, sl] = dconv[0:8, :]

    body = _carry(ex, body, 8, 5, lambda: pl.program_id(0) == 0, None, lambda: pl.program_id(0) == n_t - 1)
    ex_args, ex_in, ex_shapes, ex_out, ex_sems = _carried(ex)
    rev = lambda i: (n_t - 1 - i, 0)
    out = pl.pallas_call(
        body, name=name, grid=(n_t,),
        out_shape=(jax.ShapeDtypeStruct((s, e4), BF16), jax.ShapeDtypeStruct((s, e), BF16),
                   jax.ShapeDtypeStruct((1, d), F32), jax.ShapeDtypeStruct((1, e), F32), jax.ShapeDtypeStruct((8, e), F32),
                   *ex_shapes),
        in_specs=[pl.BlockSpec((t, e4), rev), pl.BlockSpec((t, d), rev), pl.BlockSpec((t, d), rev),
                  pl.BlockSpec((None, 8, e), lambda i: (jnp.maximum(n_t - 2 - i, 0), 0, 0)),
                  _full((3, e)), _full((1, e)), _full((1, d)), _full((e, d)), *ex_in],
        out_specs=(pl.BlockSpec((t, e4), rev), pl.BlockSpec((t, e), rev), _full((1, d)), _full((1, e)), _full((8, e)),
                   *ex_out),
        scratch_shapes=[pltpu.VMEM((t, e), F32), pltpu.VMEM((8, e), F32), *ex_sems],
        compiler_params=_params(("arbitrary",)),
    )(proj, dx1, br, tails, cw, cb, gate, wo, *ex_args)
    return (*out[:5], out[5:])


def _matmul_nt_norm_bwd(dproj, w, xin, dres, g, scale, name, ex=None):
    s, d = xin.shape
    n = w.shape[1]
    tm = min(s, 512)
    n_i = s // tm

    def body(dp_ref, w_ref, x_ref, dres_ref, g_ref, sc_ref, dx_ref, dsh_ref, dsc_ref, dg_ref, p_scr):
        i = pl.program_id(0)

        @pl.when(i == 0)
        def _():
            dsh_ref[...] = jnp.zeros_like(dsh_ref)
            p_scr[...] = jnp.zeros_like(p_scr)
        dh = lax.dot_general(dp_ref[...], w_ref[...], NT_DIMS, preferred_element_type=F32)
        _norm_modulate_bwd(dh, x_ref, dres_ref, g_ref, sc_ref, dx_ref, dsh_ref, p_scr)

        @pl.when(i == n_i - 1)
        def _():
            dsc_ref[...] = p_scr[...] * g_ref[...]
            dg_ref[...] = p_scr[...] * (1.0 + sc_ref[...])

    body = _carry(ex, body, 6, 4, lambda: pl.program_id(0) == 0, None, lambda: pl.program_id(0) == n_i - 1)
    ex_args, ex_in, ex_shapes, ex_out, ex_sems = _carried(ex)
    tok = pl.BlockSpec((tm, d), lambda i: (i, 0))
    vec = pl.BlockSpec((1, d), lambda i: (0, 0))
    vshape = jax.ShapeDtypeStruct((1, d), F32)
    out = pl.pallas_call(
        body, name=name, grid=(n_i,),
        out_shape=(jax.ShapeDtypeStruct((s, d), F32), vshape, vshape, vshape, *ex_shapes),
        in_specs=[pl.BlockSpec((tm, n), lambda i: (i, 0)), _resident((d, n)), tok, tok, vec, vec, *ex_in],
        out_specs=(tok, vec, vec, vec, *ex_out),
        scratch_shapes=[pltpu.VMEM((1, d), F32), *ex_sems],
        compiler_params=_params(("arbitrary",)),
    )(dproj, w, xin, dres, g, scale, *ex_args)
    return (*out[:4], out[4:])


def _matmul_tn(a, b, colscale, rows_split, name, ex=None):
    s, m = a.shape
    n = b.shape[1]
    n_j, tn = (1, n) if rows_split else (NDEV, n // NDEV)
    fixed = (4 + 4 + 2 * 2) * m * tn
    tk = s
    while fixed + 2 * tk * (2 * m + b.dtype.itemsize * tn) > TN_VMEM_BUDGET:
        tk //= 2
    n_k = s // tk

    def body(a_ref, b_ref, cs_ref, o_ref, acc):
        k = pl.program_id(1)
        part = lax.dot_general(a_ref[...], b_ref[...].astype(BF16), TN_DIMS, preferred_element_type=F32)
        if n_k == 1:
            o_ref[...] = (part * cs_ref[...]).astype(BF16)
            return

        @pl.when(k == 0)
        def _():
            acc[...] = part

        @pl.when((k > 0) & (k < n_k - 1))
        def _():
            acc[...] += part

        @pl.when(k == n_k - 1)
        def _():
            o_ref[...] = ((acc[...] + part) * cs_ref[...]).astype(BF16)

    at = lambda j, k: (pl.program_id(0) == j) & (pl.program_id(1) == k)
    body = _carry(ex, body, 3, 1, lambda: at(0, 0), None, lambda: at(n_j - 1, n_k - 1))
    ex_args, ex_in, ex_shapes, ex_out, ex_sems = _carried(ex)
    out = pl.pallas_call(
        body, name=name, grid=(n_j, n_k),
        out_shape=(jax.ShapeDtypeStruct((n_j, m, tn), BF16), *ex_shapes),
        in_specs=[pl.BlockSpec((tk, m), lambda j, k: (k, 0)), pl.BlockSpec((tk, tn), lambda j, k: (k, j)),
                  pl.BlockSpec((1, tn), lambda j, k: (0, j)), *ex_in],
        out_specs=(pl.BlockSpec((None, m, tn), lambda j, k: (j, 0, 0)), *ex_out),
        scratch_shapes=[pltpu.VMEM((m, tn), F32), *ex_sems],
        compiler_params=_params(("arbitrary", "arbitrary")),
    )(a, b, colscale, *ex_args)
    return (out[0].reshape(NDEV, m // NDEV, n) if rows_split else out[0]), out[1:]


def _adam_update(w, g, m, v):
    m = ADAM_B1 * m + (1.0 - ADAM_B1) * g
    v = ADAM_B2 * v + (1.0 - ADAM_B2) * (g * g)
    m_hat = m / (1.0 - ADAM_B1 ** ADAM_STEP)
    v_hat = v / (1.0 - ADAM_B2 ** ADAM_STEP)
    return -ADAM_LR * (m_hat / (jnp.sqrt(v_hat) + ADAM_EPS) + ADAM_WD * w), m, v


def _adamw_reduce(parts, w, m, v, name, ex=None):
    n_p, r, c = parts.shape
    tr = min(r, 128 if ex is not None else 256)
    n_i = r // tr

    def body(p_ref, w_ref, m_ref, v_ref, g_out, d_out, m_out, v_out):
        g = p_ref[0].astype(F32)
        for j in range(1, n_p):
            g = g + p_ref[j].astype(F32)
        g_out[...] = g
        d_out[...], m_out[...], v_out[...] = _adam_update(w_ref[...], g, m_ref[...], v_ref[...])

    step = lambda k: (lambda: pl.program_id(0) == k)
    body = _carry(ex, body, 4, 4, step(0), step(n_i // 2), step(n_i - 1))
    ex_args, ex_in, ex_shapes, ex_out, ex_sems = _carried(ex)
    blk = pl.BlockSpec((tr, c), lambda i: (i, 0))
    shp = jax.ShapeDtypeStruct((r, c), F32)
    out = pl.pallas_call(
        body, name=name, grid=(n_i,), out_shape=(shp, shp, shp, shp, *ex_shapes),
        in_specs=[pl.BlockSpec((n_p, tr, c), lambda i: (0, i, 0)), blk, blk, blk, *ex_in],
        out_specs=(blk, blk, blk, blk, *ex_out), scratch_shapes=ex_sems,
        compiler_params=_params(("arbitrary",)),
    )(parts, w, m, v, *ex_args)
    return (*out[:4], out[4:])


def _adamw_small(gs, ws, ms, vs, name):
    n = len(gs)

    def body(*refs):
        ins, outs = refs[:4 * n], refs[4 * n:]
        for a in range(n):
            d, m, v = _adam_update(ins[n + a][...], ins[a][...], ins[2 * n + a][...], ins[3 * n + a][...])
            outs[a][...], outs[n + a][...], outs[2 * n + a][...] = d, m, v

    shapes = tuple(jax.ShapeDtypeStruct(w.shape, F32) for w in ws) * 3
    out = pl.pallas_call(
        body, name=name, out_shape=shapes,
        in_specs=[_vmem()] * (4 * n), out_specs=tuple([_vmem()] * (3 * n)),
        compiler_params=pltpu.CompilerParams(vmem_limit_bytes=VMEM_LIMIT),
    )(*gs, *ws, *ms, *vs)
    return out[:n], out[n:2 * n], out[2 * n:]


def _sum_devices(packed, name):
    _, r, wdt = packed.shape

    def body(p_ref, o_ref):
        acc = p_ref[0]
        for j in range(1, NDEV):
            acc = acc + p_ref[j]
        o_ref[...] = acc

    return pl.pallas_call(
        body, name=name, out_shape=jax.ShapeDtypeStruct((r, wdt), F32),
        in_specs=[_vmem()], out_specs=_vmem(),
        compiler_params=pltpu.CompilerParams(vmem_limit_bytes=VMEM_LIMIT),
    )(packed)


def _mod_w_grad(c_t, dmod, name):
    n_layers, _, w3 = dmod.shape
    d = c_t.shape[0]

    def body(c_ref, dm_ref, o_ref):
        for i in range(n_layers):
            acc = c_ref[:, 0:1] * dm_ref[i, 0:1, :]
            for b in range(1, NDEV):
                acc = acc + c_ref[:, b:b + 1] * dm_ref[i, b:b + 1, :]
            o_ref[i] = acc

    return pl.pallas_call(
        body, name=name, out_shape=jax.ShapeDtypeStruct((n_layers, d, w3), F32),
        in_specs=[_vmem(), _vmem()], out_specs=_vmem(),
        compiler_params=pltpu.CompilerParams(vmem_limit_bytes=VMEM_LIMIT),
    )(c_t, dmod)


def _mask_transpose_ws(w_s, name):
    def body(w_ref, wt_ref, wtt_ref):
        tril = (lax.broadcasted_iota(jnp.int32, (CHUNK, CHUNK), 0) >= lax.broadcasted_iota(jnp.int32, (CHUNK, CHUNK), 1))
        for g in range(GROUPS):
            wm = jnp.where(tril, w_ref[g], 0.0)
            wt_ref[g] = wm.astype(BF16)
            wtt_ref[g] = wm.T.astype(BF16)

    shp = jax.ShapeDtypeStruct(w_s.shape, BF16)
    return pl.pallas_call(
        body, name=name, out_shape=(shp, shp), in_specs=[_vmem()], out_specs=(_vmem(), _vmem()),
    )(w_s)


def _pack(pieces):
    flat = jnp.concatenate([p.reshape(-1) for p in pieces])
    rows = -(-flat.shape[0] // (8 * PACK_W)) * 8
    return jnp.pad(flat, (0, rows * PACK_W - flat.shape[0])).reshape(rows, PACK_W)


def _unpack(flat, shapes):
    out, off = [], 0
    for shp in shapes:
        size = 1
        for dim in shp:
            size *= dim
        out.append(flat[off:off + size].reshape(shp))
        off += size
    return out


def kernel(x, c, mod_w, mod_b, norm_g, a_w_in, a_conv_w, a_conv_b, a_w_out, b_w_in, b_ln_g, b_ln_b, b_w_s, b_b_s, b_w_out, final_g, loss_target, m_mod_w, m_mod_b, m_norm_g, m_a_w_in, m_a_conv_w, m_a_conv_b, m_a_w_out, m_b_w_in, m_b_ln_g, m_b_ln_b, m_b_w_s, m_b_b_s, m_b_w_out, m_final_g, v_mod_w, v_mod_b, v_norm_g, v_a_w_in, v_a_conv_w, v_a_conv_b, v_a_w_out, v_b_w_in, v_b_ln_g, v_b_ln_b, v_b_w_s, v_b_b_s, v_b_w_out, v_final_g):
    s, d = x.shape[1], x.shape[2]
    es = a_w_out.shape[1]
    e = NDEV * es
    w3 = mod_w.shape[2]
    me = _index(_pos())
    x0 = x.reshape(s, d)
    tgt = loss_target.reshape(s, d)

    small = jnp.concatenate([a_conv_w[0], b_ln_g, b_ln_b, jnp.zeros((3, es), F32)], axis=0)
    gather_a = _gather_exchange([a_w_in[0].astype(BF16), a_w_out[0].astype(BF16), small], [True, False, True])
    gather_b = _gather_exchange([b_w_in[0].astype(BF16), b_w_out[0].astype(BF16)], [True, False])
    mod, c_all, (wa, woa, small_all) = _mod_vectors(c, mod_w, mod_b, gather_a)
    conv_w, ln_g, ln_b = small_all[0:3], small_all[3:4], small_all[4:5]
    bsf = jnp.repeat(b_b_s[0].T, e // GROUPS, axis=1)
    wt, wtt = _mask_transpose_ws(b_w_s[0], "mask_w_s")
    shift0, scale0, gate0 = mod[0:1, 0:d], mod[0:1, d:2 * d], mod[0:1, 2 * d:]
    shift1, scale1, gate1 = mod[1:2, 0:d], mod[1:2, d:2 * d], mod[1:2, 2 * d:]
    g0, g1, fg = norm_g[0:1], norm_g[1:2], final_g.reshape(1, d)

    proj_a, h0, x1, br_a, tails, (wb, wob) = _layer_a_fwd(
        x0, g0, scale0, shift0, gate0, wa, conv_w, a_conv_b, woa, "a_fwd", gather_b)
    proj_b, h1, dx2, loss_acc, dfg, dgate1, _ = _layer_b_fwd_loss(
        x1, tgt, g1, scale1, shift1, gate1, fg, wb, ln_g, ln_b, wt, bsf, wob, "b_fwd_loss")

    dproj_b, y_b, dx1, dws, dbs, dlg, dlb, dshift1, dscale1, dg1 = _layer_b_bwd(
        proj_b, dx2, x1, gate1, g1, scale1, ln_g, ln_b, wt, wtt, bsf, wob, wb, "b_bwd")
    gs_b_out, _ = _matmul_tn(y_b, dx2, gate1, True, "b_w_out_grad")
    gs_b_in, _ = _matmul_tn(h1, dproj_b, jnp.ones((1, dproj_b.shape[1]), F32), False, "b_w_in_grad")
    dproj_a, y_a, dgate0, dcb, dcw, (gr_b_in, gr_b_out) = _conv_mixer_bwd(
        proj_a, dx1, br_a, tails, conv_w, a_conv_b, gate0, woa, "a_mixer_bwd", _scatter_exchange([gs_b_in, gs_b_out]))
    gs_a_out, _ = _matmul_tn(y_a, dx1, gate0, True, "a_w_out_grad")
    gs_a_in, (gr_a_out,) = _matmul_tn(h0, dproj_a, jnp.ones((1, dproj_a.shape[1]), F32), False, "a_w_in_grad",
                                      _scatter_exchange([gs_a_out]))
    dx0, dshift0, dscale0, dg0, (gr_a_in,) = _matmul_nt_norm_bwd(dproj_a, wa, x0, dx1, g0, scale0, "a_in_bwd",
                                                                 _scatter_exchange([gs_a_in]))

    def big(parts, w, m, v, name, ex=None):
        shp = w.shape
        r2 = lambda t_: t_.reshape(-1, shp[-1])
        g, dl, nm, nv, ex_out = _adamw_reduce(parts, r2(w), r2(m), r2(v), name, ex)
        return tuple(t_.reshape(shp) for t_ in (g, dl, nm, nv)), ex_out

    pieces = [dshift0, dscale0, dgate0, dshift1, dscale1, dgate1, dg0, dg1, dcb, dcw[0:3], dlg, dlb, dfg,
              dbs[:, 0:GROUPS].T, dws, loss_acc[0:1, 0:1]]
    shapes = [p.shape for p in pieces]
    res = {}
    res["a_w_in"], (packed_all,) = big(gr_a_in, a_w_in, m_a_w_in, v_a_w_in, "adamw_a_w_in",
                                       _gather_exchange([_pack(pieces)], [False]))
    packed_all = packed_all.reshape(NDEV, -1, PACK_W)
    total = _sum_devices(packed_all, "sum_small_grads").reshape(-1)
    (t_sh0, t_sc0, t_ga0, t_sh1, t_sc1, t_ga1, t_g0, t_g1, t_cb, t_cw, t_lg, t_lb, t_fg, t_bs, t_ws, t_loss) = _unpack(
        total, shapes)
    loss = t_loss.reshape(())
    grad_mod_b = jnp.concatenate([jnp.concatenate([t_sh0, t_sc0, t_ga0], axis=1),
                                  jnp.concatenate([t_sh1, t_sc1, t_ga1], axis=1)], axis=0)
    grad_norm_g = jnp.concatenate([t_g0, t_g1], axis=0)
    dmod_all = packed_all.reshape(NDEV, -1)[:, 0:6 * d].reshape(NDEV, 2, 3 * d).transpose(1, 0, 2)
    dmod_mine = lax.dynamic_slice_in_dim(dmod_all, me * w3, w3, axis=2)
    grad_mod_w = _mod_w_grad(c_all.T, dmod_mine, "mod_w_grad")
    grad_a_conv_w = lax.dynamic_slice_in_dim(t_cw, me * es, es, axis=1)
    grad_b_ln_g = lax.dynamic_slice_in_dim(t_lg, me * es, es, axis=1)
    grad_b_ln_b = lax.dynamic_slice_in_dim(t_lb, me * es, es, axis=1)

    res["mod_w"], _ = big(grad_mod_w.reshape(1, -1, w3), mod_w, m_mod_w, v_mod_w, "adamw_mod_w")
    res["a_w_out"], _ = big(gr_a_out, a_w_out, m_a_w_out, v_a_w_out, "adamw_a_w_out")
    res["b_w_in"], _ = big(gr_b_in, b_w_in, m_b_w_in, v_b_w_in, "adamw_b_w_in")
    res["b_w_out"], _ = big(gr_b_out, b_w_out, m_b_w_out, v_b_w_out, "adamw_b_w_out")

    small_names = ["mod_b", "norm_g", "a_conv_w", "a_conv_b", "b_ln_g", "b_ln_b", "b_w_s", "b_b_s", "final_g"]
    small_g = [grad_mod_b, grad_norm_g, grad_a_conv_w, t_cb, grad_b_ln_g, grad_b_ln_b, t_ws, t_bs, t_fg]
    small_w = [mod_b, norm_g, a_conv_w, a_conv_b, b_ln_g, b_ln_b, b_w_s, b_b_s, final_g]
    small_m = [m_mod_b, m_norm_g, m_a_conv_w, m_a_conv_b, m_b_ln_g, m_b_ln_b, m_b_w_s, m_b_b_s, m_final_g]
    small_v = [v_mod_b, v_norm_g, v_a_conv_w, v_a_conv_b, v_b_ln_g, v_b_ln_b, v_b_w_s, v_b_b_s, v_final_g]
    as2d = lambda t_: t_.reshape(-1, t_.shape[-1])
    dls, nms, nvs = _adamw_small([as2d(t_) for t_ in small_g], [as2d(t_) for t_ in small_w],
                                 [as2d(t_) for t_ in small_m], [as2d(t_) for t_ in small_v], "adamw_small")
    for a, nme in enumerate(small_names):
        shp = small_w[a].shape
        res[nme] = (small_g[a].reshape(shp), dls[a].reshape(shp), nms[a].reshape(shp), nvs[a].reshape(shp))

    order = ["mod_w", "mod_b", "norm_g", "a_w_in", "a_conv_w", "a_conv_b", "a_w_out", "b_w_in", "b_ln_g", "b_ln_b",
             "b_w_s", "b_b_s", "b_w_out", "final_g"]
    return (loss, dx0.reshape(x.shape), *[res[k][0] for k in order], *[res[k][1] for k in order],
            *[res[k][2] for k in order], *[res[k][3] for k in order])
```

```python
import jax
import jax.numpy as jnp
from jax import lax
from jax.experimental import pallas as pl
from jax.experimental.pallas import tpu as pltpu

NDEV = 8
CHUNK = 128
GROUPS = 8
RMS_EPS = 1e-6
LN_EPS = 1e-5
ADAM_LR, ADAM_B1, ADAM_B2, ADAM_EPS, ADAM_WD, ADAM_STEP = 0.001, 0.9, 0.999, 1e-08, 0.01, 10
V7X_VMEM_BYTES = 64 * 1024 * 1024
VMEM_LIMIT = V7X_VMEM_BYTES - 8 * 1024 * 1024
TN_VMEM_BUDGET = 46 * 1024 * 1024
PACK_W = 1024
F32, BF16 = jnp.float32, jnp.bfloat16
MESH = pl.DeviceIdType.MESH
RSQRT2 = 0.7071067811865476
INV_SQRT_2PI = 0.3989422804014327
NT_DIMS = (((1,), (1,)), ((), ()))
TN_DIMS = (((0,), (0,)), ((), ()))


def _params(sem=None):
    return pltpu.CompilerParams(dimension_semantics=sem, vmem_limit_bytes=VMEM_LIMIT)


def _vmem():
    return pl.BlockSpec(memory_space=pltpu.VMEM)


def _hbm():
    return pl.BlockSpec(memory_space=pltpu.HBM)


def _full(shape):
    return pl.BlockSpec(shape, lambda *_: (0,) * len(shape))


def _pos():
    return lax.axis_index("x"), lax.axis_index("y"), lax.axis_index("c")


def _index(p):
    return 4 * p[0] + 2 * p[1] + p[2]


def _peer(k):
    x, y, c = _pos()
    return ((1 - x) if (k >> 2) & 1 else x, (1 - y) if (k >> 1) & 1 else y, (1 - c) if k & 1 else c)


def _silu(z):
    sg = jax.nn.sigmoid(z)
    return z * sg, sg * (1.0 + z * (1.0 - sg))


def _gelu(v):
    phi = 0.5 * (1.0 + lax.erf(v * RSQRT2))
    return v * phi, phi + v * (jnp.exp(-0.5 * v * v) * INV_SQRT_2PI)


def _colsum(v):
    return jnp.sum(v, axis=0, keepdims=True)


def _rowsum(v):
    return jnp.sum(v, axis=-1, keepdims=True)


def _gather_all_vmem(slab_ref, send_sems, recv_sems, base):
    me = _index(_pos())
    sends = []
    for k in range(1, NDEV):
        cp = pltpu.make_async_remote_copy(
            src_ref=slab_ref.at[me], dst_ref=slab_ref.at[me],
            send_sem=send_sems.at[base + k - 1], recv_sem=recv_sems.at[base + k - 1],
            device_id=_peer(k), device_id_type=MESH)
        cp.start()
        sends.append(cp)
    for k in range(1, NDEV):
        src = _index(_peer(k))
        pltpu.make_async_remote_copy(
            src_ref=slab_ref.at[src], dst_ref=slab_ref.at[src],
            send_sem=send_sems.at[base + k - 1], recv_sem=recv_sems.at[base + k - 1],
            device_id=_peer(k), device_id_type=MESH).wait_recv()
    for cp in sends:
        cp.wait_send()


def _mod_vectors(c, mod_w, mod_b, ex):
    n_layers, d, w3 = mod_w.shape
    r_in, r_out = len(ex.arrays), len(ex.out_shapes)

    def body(*refs):
        c_ref, mw_ref, mb_ref = refs[:3]
        ex_ins = refs[3:3 + r_in]
        mod_ref, call_ref = refs[3 + r_in:5 + r_in]
        ex_outs = refs[5 + r_in:5 + r_in + r_out]
        cslab, pslab, send_sems, recv_sems = refs[5 + r_in + r_out:9 + r_in + r_out]
        ex_sems = refs[9 + r_in + r_out:]
        ex.start(ex_ins, ex_outs, ex_sems)
        me = _index(_pos())
        cv = c_ref[...]
        cslab[me] = jnp.broadcast_to(cv * jax.nn.sigmoid(cv), (8, d))
        _gather_all_vmem(cslab, send_sems, recv_sems, 0)
        c_all = jnp.concatenate([cslab[k, 0:1, :] for k in range(NDEV)], axis=0)
        call_ref[...] = c_all
        for i in range(n_layers):
            pslab[me, i * NDEV:(i + 1) * NDEV, :] = jnp.dot(
                c_all, mw_ref[i], preferred_element_type=F32, precision=lax.Precision.HIGHEST)
        _gather_all_vmem(pslab, send_sems, recv_sems, NDEV - 1)
        for i in range(n_layers):
            for k in range(NDEV):
                mod_ref[i:i + 1, k * w3:(k + 1) * w3] = (
                    pslab[k, pl.ds(i * NDEV + me, 1), :] + mb_ref[i:i + 1, k * w3:(k + 1) * w3])
        if ex.middle is not None:
            ex.middle(ex_ins, ex_outs, ex_sems)
        ex.finish(ex_ins, ex_outs, ex_sems)

    out = pl.pallas_call(
        body, name="mod_vectors",
        out_shape=(jax.ShapeDtypeStruct((n_layers, 3 * d), F32), jax.ShapeDtypeStruct((NDEV, d), F32), *ex.out_shapes),
        in_specs=[_vmem(), _vmem(), _vmem()] + [_hbm()] * r_in, out_specs=(_vmem(), _vmem(), *([_hbm()] * r_out)),
        scratch_shapes=[pltpu.VMEM((NDEV, 8, d), F32), pltpu.VMEM((NDEV, n_layers * NDEV, w3), F32),
                        pltpu.SemaphoreType.DMA((2 * (NDEV - 1),)), pltpu.SemaphoreType.DMA((2 * (NDEV - 1),)), *ex.sems],
        compiler_params=pltpu.CompilerParams(vmem_limit_bytes=VMEM_LIMIT),
    )(c, mod_w, mod_b, *ex.arrays)
    return out[0], out[1], out[2:]


class _Exchange:
    def __init__(self, arrays, out_shapes, sems, start, middle, finish):
        self.arrays, self.out_shapes, self.sems = list(arrays), list(out_shapes), list(sems)
        self.start, self.middle, self.finish = start, middle, finish


def _gather_exchange(shards, by_cols):
    n = len(shards)
    shapes = [sh.shape for sh in shards]

    def tools(ins, outs, sems):
        send_sems, recv_sems, local_sems = sems
        x, y, c = _pos()
        chips = [(1 - x, y), (x, 1 - y), (1 - x, 1 - y)]

        def place(a, block):
            r, cc = shapes[a]
            if by_cols[a]:
                return outs[a].at[:, pl.ds(_index(block) * cc, cc)]
            return outs[a].at[pl.ds(_index(block) * r, r), :]

        def copy(a, k, block, to, src=None):
            dst = place(a, block)
            return pltpu.make_async_remote_copy(
                src_ref=dst if src is None else src, dst_ref=dst,
                send_sem=send_sems.at[a * 7 + k], recv_sem=recv_sems.at[a * 7 + k],
                device_id=to, device_id_type=MESH)

        mine = [pltpu.make_async_copy(ins[a], place(a, (x, y, c)), local_sems.at[a]) for a in range(n)]
        first = []
        for a in range(n):
            first.append(copy(a, 0, (x, y, c), (x, y, 1 - c), src=ins[a]))
            first += [copy(a, 1 + j, (x, y, c), (*chip, c), src=ins[a]) for j, chip in enumerate(chips)]
        passed = [copy(a, 4 + j, (*chip, c), (x, y, 1 - c)) for j, chip in enumerate(chips) for a in range(n)]
        return (x, y, c), chips, copy, mine, first, passed

    def start(ins, outs, sems):
        _, _, _, mine, first, _ = tools(ins, outs, sems)
        for cp in mine + first:
            cp.start()

    def middle(ins, outs, sems):
        (x, y, c), chips, copy, _, _, passed = tools(ins, outs, sems)
        for j, chip in enumerate(chips):
            for a in range(n):
                copy(a, 1 + j, (*chip, c), (x, y, c)).wait_recv()
                passed[j * n + a].start()

    def finish(ins, outs, sems):
        (x, y, c), chips, copy, mine, first, passed = tools(ins, outs, sems)
        for a in range(n):
            copy(a, 0, (x, y, 1 - c), (x, y, c)).wait_recv()
        for j, chip in enumerate(chips):
            for a in range(n):
                copy(a, 4 + j, (*chip, 1 - c), (x, y, c)).wait_recv()
        for cp in first + passed:
            cp.wait_send()
        for cp in mine:
            cp.wait()

    out_shapes = [jax.ShapeDtypeStruct((r, NDEV * cc) if bc else (NDEV * r, cc), sh.dtype)
                  for (r, cc), bc, sh in zip(shapes, by_cols, shards)]
    sems = [pltpu.SemaphoreType.DMA((7 * n,)), pltpu.SemaphoreType.DMA((7 * n,)), pltpu.SemaphoreType.DMA((n,))]
    return _Exchange(shards, out_shapes, sems, start, middle, finish)


def _scatter_exchange(parts):
    n = len(parts)

    def tools(ins, outs, sems):
        send_sems, recv_sems, local_sems = sems
        me = _index(_pos())
        mine = [pltpu.make_async_copy(ins[a].at[me], outs[a].at[me], local_sems.at[a]) for a in range(n)]
        sends, arrivals = [], []
        for k in range(1, NDEV):
            peer = _peer(k)
            for a in range(n):
                pair = dict(send_sem=send_sems.at[a * 7 + k - 1], recv_sem=recv_sems.at[a * 7 + k - 1],
                            device_id=peer, device_id_type=MESH)
                sends.append(pltpu.make_async_remote_copy(src_ref=ins[a].at[_index(peer)], dst_ref=outs[a].at[me], **pair))
                slot = outs[a].at[_index(peer)]
                arrivals.append(pltpu.make_async_remote_copy(src_ref=slot, dst_ref=slot, **pair))
        return mine, sends, arrivals

    def start(ins, outs, sems):
        mine, sends, _ = tools(ins, outs, sems)
        for cp in mine + sends:
            cp.start()

    def finish(ins, outs, sems):
        mine, sends, arrivals = tools(ins, outs, sems)
        for cp in arrivals:
            cp.wait_recv()
        for cp in sends:
            cp.wait_send()
        for cp in mine:
            cp.wait()

    out_shapes = [jax.ShapeDtypeStruct(p.shape, p.dtype) for p in parts]
    sems = [pltpu.SemaphoreType.DMA((7 * n,)), pltpu.SemaphoreType.DMA((7 * n,)), pltpu.SemaphoreType.DMA((n,))]
    return _Exchange(parts, out_shapes, sems, start, None, finish)


def _carry(ex, body, n_in, n_out, first, middle, last):
    if ex is None:
        return body
    r_in, r_out = len(ex.arrays), len(ex.out_shapes)

    def wrapped(*refs):
        ins, rins = refs[:n_in], refs[n_in:n_in + r_in]
        outs = refs[n_in + r_in:n_in + r_in + n_out]
        routs = refs[n_in + r_in + n_out:n_in + r_in + n_out + r_out]
        rest = refs[n_in + r_in + n_out + r_out:]
        scratch, sems = rest[:len(rest) - len(ex.sems)], rest[len(rest) - len(ex.sems):]

        @pl.when(first())
        def _():
            ex.start(rins, routs, sems)

        if ex.middle is not None:
            @pl.when(middle())
            def _():
                ex.middle(rins, routs, sems)

        body(*ins, *outs, *scratch)

        @pl.when(last())
        def _():
            ex.finish(rins, routs, sems)

    return wrapped


def _carried(ex):
    if ex is None:
        return [], [], [], [], []
    return ex.arrays, [_hbm()] * len(ex.arrays), ex.out_shapes, [_hbm()] * len(ex.out_shapes), ex.sems


def _resident(shape):
    return pl.BlockSpec(shape, lambda *_: (0,) * len(shape), pipeline_mode=pl.Buffered(1))


def _norm_modulate(x_ref, g_ref, sc_ref, sh_ref):
    xv = x_ref[...]
    r = lax.rsqrt(jnp.mean(xv * xv, axis=-1, keepdims=True) + RMS_EPS)
    return ((xv * r) * g_ref[...] * (1.0 + sc_ref[...]) + sh_ref[...]).astype(BF16)


def _conv_taps(cx, t6, t7, row):
    p1 = jnp.where(row == 0, t7, pltpu.roll(cx, 1, 0))
    p2 = jnp.where(row == 0, t6, jnp.where(row == 1, t7, pltpu.roll(cx, 2, 0)))
    return p1, p2


def _layer_a_fwd(x, g, scale, shift, gate, wi, cw, cb, wo, name, ex=None):
    s, d = x.shape
    e = wo.shape[0]
    t = min(s, 256)
    n_t = s // t
    cwid = min(e, 512)

    def body(x_ref, g_ref, sc_ref, sh_ref, gate_ref, wi_ref, cw_ref, cb_ref, wo_ref,
             proj_ref, h_ref, x1_ref, br_ref, tails_ref, y_scr, tail_scr):
        @pl.when(pl.program_id(0) == 0)
        def _():
            tail_scr[...] = jnp.zeros_like(tail_scr)
        h_ref[...] = _norm_modulate(x_ref, g_ref, sc_ref, sh_ref)
        row = lax.broadcasted_iota(jnp.int32, (t, cwid), 0)

        def project(c0):
            v = jnp.dot(h_ref[...], wi_ref[:, c0:c0 + cwid], preferred_element_type=F32)
            proj_ref[:, c0:c0 + cwid] = v.astype(BF16)
            return v

        for c0 in range(0, e, cwid):
            sl = slice(c0, c0 + cwid)
            bg, z = project(c0), project(3 * e + c0)
            cx = project(e + c0) * project(2 * e + c0)
            p1, p2 = _conv_taps(cx, tail_scr[6:7, sl], tail_scr[7:8, sl], row)
            conv = cb_ref[:, sl] + cw_ref[2:3, sl] * cx + cw_ref[0:1, sl] * p2 + cw_ref[1:2, sl] * p1
            y_scr[:, sl] = (_silu(z)[0] * bg * conv).astype(BF16)
            tail_scr[:, sl] = cx[t - 8:t, :]
        tails_ref[...] = tail_scr[...]
        br = jnp.dot(y_scr[...], wo_ref[...], preferred_element_type=F32)
        x1_ref[...] = x_ref[...] + gate_ref[...] * br
        br_ref[...] = br.astype(BF16)

    step = lambda k: (lambda: pl.program_id(0) == k)
    body = _carry(ex, body, 9, 5, step(0), step(n_t // 2), step(n_t - 1))
    ex_args, ex_in, ex_shapes, ex_out, ex_sems = _carried(ex)
    tok = pl.BlockSpec((t, d), lambda i: (i, 0))
    out = pl.pallas_call(
        body, name=name, grid=(n_t,),
        out_shape=(jax.ShapeDtypeStruct((s, 4 * e), BF16), jax.ShapeDtypeStruct((s, d), BF16),
                   jax.ShapeDtypeStruct((s, d), F32), jax.ShapeDtypeStruct((s, d), BF16),
                   jax.ShapeDtypeStruct((n_t, 8, e), F32), *ex_shapes),
        in_specs=[tok, _full((1, d)), _full((1, d)), _full((1, d)), _full((1, d)), _resident((d, 4 * e)),
                  _full((3, e)), _full((1, e)), _resident((e, d)), *ex_in],
        out_specs=(pl.BlockSpec((t, 4 * e), lambda i: (i, 0)), tok, tok, tok,
                   pl.BlockSpec((None, 8, e), lambda i: (i, 0, 0)), *ex_out),
        scratch_shapes=[pltpu.VMEM((t, e), BF16), pltpu.VMEM((8, e), F32), *ex_sems],
        compiler_params=_params(("arbitrary",)),
    )(x, g, scale, shift, gate, wi, cw, cb, wo, *ex_args)
    return (*out[:5], out[5:])


def _ln_stats(v_pre, v_scr, dgv_scr, t, e):
    gw = e // GROUPS
    s1 = jnp.zeros((t, 1), F32)
    for g in range(GROUPS):
        v, dgv = _gelu(v_pre(g))
        v_scr[:, g * gw:(g + 1) * gw] = v
        if dgv_scr is not None:
            dgv_scr[:, g * gw:(g + 1) * gw] = dgv.astype(dgv_scr.dtype)
        s1 = s1 + _rowsum(v)
    mu = s1 * (1.0 / e)
    s2 = jnp.zeros((t, 1), F32)
    for g in range(GROUPS):
        dv = v_scr[:, g * gw:(g + 1) * gw] - mu
        s2 = s2 + _rowsum(dv * dv)
    return mu, lax.rsqrt(s2 * (1.0 / e) + LN_EPS)


def _layer_b_fwd_loss(x1, tgt, g1, scale, shift, gate, fg, wi, lng, lnb, wt, bsf, wo, name, ex=None):
    s, d = x1.shape
    e = wo.shape[0]
    gw = e // GROUPS
    t = min(s, 256)
    n_t = s // t

    def body(x1_ref, tgt_ref, g_ref, sc_ref, sh_ref, gate_ref, fg_ref, wi_ref, lng_ref, lnb_ref, wt_ref, bsf_ref, wo_ref,
             proj_ref, h_ref, dx2_ref, loss_ref, dfg_ref, dgate_ref, v_scr, y_scr):
        @pl.when(pl.program_id(0) == 0)
        def _():
            loss_ref[...] = jnp.zeros_like(loss_ref)
            dfg_ref[...] = jnp.zeros_like(dfg_ref)
            dgate_ref[...] = jnp.zeros_like(dgate_ref)
        h_ref[...] = _norm_modulate(x1_ref, g_ref, sc_ref, sh_ref)

        def project(c0):
            v = jnp.dot(h_ref[...], wi_ref[:, c0:c0 + gw], preferred_element_type=F32)
            proj_ref[:, c0:c0 + gw] = v.astype(BF16)
            return v

        mu, rs = _ln_stats(lambda g: project(e + g * gw), v_scr, None, t, e)
        for g in range(GROUPS):
            gs = slice(g * gw, (g + 1) * gw)
            vn = (((v_scr[:, gs] - mu) * rs) * lng_ref[:, gs] + lnb_ref[:, gs]).astype(BF16)
            u = _gelu(project(g * gw))[0]
            sz = _silu(project(2 * e + g * gw))[0]
            for ch in range(t // CHUNK):
                rows = slice(ch * CHUNK, (ch + 1) * CHUNK)
                mixed = jnp.dot(wt_ref[g], vn[rows], preferred_element_type=F32) + bsf_ref[:, gs]
                y_scr[rows, gs] = (sz[rows] * (u[rows] * mixed)).astype(BF16)
        br = jnp.dot(y_scr[...], wo_ref[...], preferred_element_type=F32)
        x2 = x1_ref[...] + gate_ref[...] * br
        r2 = lax.rsqrt(jnp.mean(x2 * x2, axis=-1, keepdims=True) + RMS_EPS)
        xn = x2 * r2
        diff = xn * fg_ref[...] - tgt_ref[...]
        loss_ref[...] += jnp.broadcast_to(0.5 * _colsum(jnp.mean(diff * diff, axis=-1, keepdims=True)), loss_ref.shape)
        dout = diff * (1.0 / d)
        dfg_ref[...] += _colsum(dout * xn)
        dxn = dout * fg_ref[...]
        dx2 = r2 * (dxn - xn * jnp.mean(dxn * xn, axis=-1, keepdims=True))
        dx2_ref[...] = dx2
        dgate_ref[...] += _colsum(dx2 * br)

    step = lambda k: (lambda: pl.program_id(0) == k)
    body = _carry(ex, body, 13, 6, step(0), step(n_t // 2), step(n_t - 1))
    ex_args, ex_in, ex_shapes, ex_out, ex_sems = _carried(ex)
    tok = pl.BlockSpec((t, d), lambda i: (i, 0))
    vec = _full((1, d))
    out = pl.pallas_call(
        body, name=name, grid=(n_t,),
        out_shape=(jax.ShapeDtypeStruct((s, 3 * e), BF16), jax.ShapeDtypeStruct((s, d), BF16),
                   jax.ShapeDtypeStruct((s, d), F32), jax.ShapeDtypeStruct((8, 128), F32),
                   jax.ShapeDtypeStruct((1, d), F32), jax.ShapeDtypeStruct((1, d), F32), *ex_shapes),
        in_specs=[tok, tok, vec, vec, vec, vec, vec, _resident((d, 3 * e)), _full((1, e)), _full((1, e)),
                  _full((GROUPS, CHUNK, CHUNK)), _resident((CHUNK, e)), _resident((e, d)), *ex_in],
        out_specs=(pl.BlockSpec((t, 3 * e), lambda i: (i, 0)), tok, tok, _full((8, 128)), vec, vec, *ex_out),
        scratch_shapes=[pltpu.VMEM((t, e), F32), pltpu.VMEM((t, e), BF16), *ex_sems],
        compiler_params=_params(("arbitrary",)),
    )(x1, tgt, g1, scale, shift, gate, fg, wi, lng, lnb, wt, bsf, wo, *ex_args)
    return (*out[:6], out[6:])


def _norm_modulate_bwd(dh, x_ref, dres_ref, g_ref, sc_ref, dx_ref, dsh_ref, p_scr):
    xv = x_ref[...]
    r = lax.rsqrt(jnp.mean(xv * xv, axis=-1, keepdims=True) + RMS_EPS)
    xn = xv * r
    dsh_ref[...] += _colsum(dh)
    p_scr[...] += _colsum(dh * xn)
    dxn = dh * (g_ref[...] * (1.0 + sc_ref[...]))
    dx_ref[...] = r * (dxn - xn * jnp.mean(dxn * xn, axis=-1, keepdims=True)) + dres_ref[...]


def _layer_b_bwd(proj, dx2, x1, gate, g1, scale, lng, lnb, wt, wtt, bsf, wo, wi, name):
    s, e3 = proj.shape
    e = e3 // 3
    d = dx2.shape[1]
    gw = e // GROUPS
    t = min(s, 256)
    n_t = s // t

    def body(p_ref, dx_ref, x1_ref, gate_ref, g_ref, sc_ref, lng_ref, lnb_ref, wt_ref, wtt_ref, bsf_ref, wo_ref, wi_ref,
             dp_ref, y_ref, dx1_ref, dws_ref, dbs_ref, dlg_ref, dlb_ref, dsh_ref, dsc_ref, dg_ref,
             v_scr, dgv_scr, dbr_scr, dvn_scr, dbs_scr, p_scr, dy_scr, vn_scr, mixed_scr):
        @pl.when(pl.program_id(0) == 0)
        def _():
            dws_ref[...] = jnp.zeros_like(dws_ref)
            dlg_ref[...] = jnp.zeros_like(dlg_ref)
            dlb_ref[...] = jnp.zeros_like(dlb_ref)
            dsh_ref[...] = jnp.zeros_like(dsh_ref)
            dbs_scr[...] = jnp.zeros_like(dbs_scr)
            p_scr[...] = jnp.zeros_like(p_scr)
        dbr_scr[...] = (dx_ref[...] * gate_ref[...]).astype(BF16)
        mu, rs = _ln_stats(lambda g: p_ref[:, e + g * gw:e + (g + 1) * gw].astype(F32), v_scr, dgv_scr, t, e)
        tril = (lax.broadcasted_iota(jnp.int32, (CHUNK, CHUNK), 0) >= lax.broadcasted_iota(jnp.int32, (CHUNK, CHUNK), 1))
        c1 = jnp.zeros((t, 1), F32)
        c2 = jnp.zeros((t, 1), F32)
        span = 2
        kw = span * gw
        dh = jnp.zeros((t, d), F32)

        def through_w_in(c0):
            return lax.dot_general(dp_ref[:, c0:c0 + kw], wi_ref[:, c0:c0 + kw], NT_DIMS, preferred_element_type=F32)

        dy_scr[...] = lax.dot_general(dbr_scr[...], wo_ref[...], NT_DIMS, preferred_element_type=F32)
        for g in range(GROUPS):
            gs = slice(g * gw, (g + 1) * gw)
            vhat = (v_scr[:, gs] - mu) * rs
            v_scr[:, gs] = vhat
            vn = (vhat * lng_ref[:, gs] + lnb_ref[:, gs]).astype(BF16)
            vn_scr[:, gs] = vn
            for ch in range(t // CHUNK):
                rows = slice(ch * CHUNK, (ch + 1) * CHUNK)
                mixed_scr[rows, gs] = jnp.dot(wt_ref[g], vn[rows], preferred_element_type=F32) + bsf_ref[:, gs]
        for g in range(GROUPS):
            gs = slice(g * gw, (g + 1) * gw)
            vhat = v_scr[:, gs]
            lg = lng_ref[:, gs]
            for ch in range(t // CHUNK):
                rows = slice(ch * CHUNK, (ch + 1) * CHUNK)
                mixed = mixed_scr[rows, gs]
                u, dgu = _gelu(p_ref[rows, g * gw:(g + 1) * gw].astype(F32))
                sz, dsz = _silu(p_ref[rows, 2 * e + g * gw:2 * e + (g + 1) * gw].astype(F32))
                sgate = u * mixed
                y_ref[rows, gs] = (sz * sgate).astype(BF16)
                dy = dy_scr[rows, gs]
                dp_ref[rows, 2 * e + g * gw:2 * e + (g + 1) * gw] = (dy * sgate * dsz).astype(BF16)
                ds = dy * sz
                dp_ref[rows, gs] = (ds * mixed * dgu).astype(BF16)
                dm = ds * u
                dbs_scr[:, gs] += dm
                dmb = dm.astype(BF16)
                dws_ref[g] += jnp.where(tril, lax.dot_general(dmb, vn_scr[rows, gs], NT_DIMS, preferred_element_type=F32), 0.0)
                dvn_scr[rows, gs] = jnp.dot(wtt_ref[g], dmb, preferred_element_type=F32)
            dvn = dvn_scr[:, gs]
            dlb_ref[:, gs] += _colsum(dvn)
            dlg_ref[:, gs] += _colsum(dvn * vhat)
            dvh = dvn * lg
            c1 = c1 + _rowsum(dvh)
            c2 = c2 + _rowsum(dvh * vhat)
            if g % span == span - 1:
                dh = dh + through_w_in(g * gw + gw - kw) + through_w_in(2 * e + g * gw + gw - kw)
        c1 = c1 * (1.0 / e)
        c2 = c2 * (1.0 / e)
        for g in range(GROUPS):
            gs = slice(g * gw, (g + 1) * gw)
            dv = rs * (dvn_scr[:, gs] * lng_ref[:, gs] - c1 - v_scr[:, gs] * c2)
            dp_ref[:, e + g * gw:e + (g + 1) * gw] = (dv * dgv_scr[:, gs]).astype(BF16)
            if g % span == span - 1:
                dh = dh + through_w_in(e + g * gw + gw - kw)
        _norm_modulate_bwd(dh, x1_ref, dx_ref, g_ref, sc_ref, dx1_ref, dsh_ref, p_scr)

        @pl.when(pl.program_id(0) == n_t - 1)
        def _():
            lane = lax.broadcasted_iota(jnp.int32, (CHUNK, 128), 1)
            acc = jnp.zeros((CHUNK, 128), F32)
            for g in range(GROUPS):
                acc = acc + jnp.where(lane == g, _rowsum(dbs_scr[:, g * gw:(g + 1) * gw]), 0.0)
            dbs_ref[...] = acc
            dsc_ref[...] = p_scr[...] * g_ref[...]
            dg_ref[...] = p_scr[...] * (1.0 + sc_ref[...])

    tok = pl.BlockSpec((t, d), lambda i: (i, 0))
    vec, evec, ws = _full((1, d)), _full((1, e)), _full((GROUPS, CHUNK, CHUNK))
    vshape = jax.ShapeDtypeStruct((1, d), F32)
    return pl.pallas_call(
        body, name=name, grid=(n_t,),
        out_shape=(jax.ShapeDtypeStruct((s, e3), BF16), jax.ShapeDtypeStruct((s, e), BF16), jax.ShapeDtypeStruct((s, d), F32),
                   jax.ShapeDtypeStruct((GROUPS, CHUNK, CHUNK), F32), jax.ShapeDtypeStruct((CHUNK, 128), F32),
                   jax.ShapeDtypeStruct((1, e), F32), jax.ShapeDtypeStruct((1, e), F32), vshape, vshape, vshape),
        in_specs=[pl.BlockSpec((t, e3), lambda i: (i, 0)), tok, tok, vec, vec, vec, evec, evec, ws, ws,
                  _resident((CHUNK, e)), _resident((e, d)), _resident((d, e3))],
        out_specs=(pl.BlockSpec((t, e3), lambda i: (i, 0)), pl.BlockSpec((t, e), lambda i: (i, 0)), tok,
                   ws, _full((CHUNK, 128)), evec, evec, vec, vec, vec),
        scratch_shapes=[pltpu.VMEM((t, e), F32), pltpu.VMEM((t, e), BF16), pltpu.VMEM((t, d), BF16),
                        pltpu.VMEM((t, e), F32), pltpu.VMEM((CHUNK, e), F32), pltpu.VMEM((1, d), F32),
                        pltpu.VMEM((t, e), F32), pltpu.VMEM((t, e), BF16), pltpu.VMEM((t, e), F32)],
        compiler_params=_params(("arbitrary",)),
    )(proj, dx2, x1, gate, g1, scale, lng, lnb, wt, wtt, bsf, wo, wi)


def _conv_mixer_bwd(proj, dx1, br, tails, cw, cb, gate, wo, name, ex=None):
    s, e4 = proj.shape
    e = e4 // 4
    d = dx1.shape[1]
    t = min(s, 256)
    n_t = s // t
    cwid = min(e, 512)

    def body(p_ref, dx_ref, br_ref, tails_ref, cw_ref, cb_ref, gate_ref, wo_ref,
             dp_ref, y_ref, dgate_ref, dcb_ref, dcw_ref, dy_scr, head_scr):
        i = pl.program_id(0)

        @pl.when(i == 0)
        def _():
            dgate_ref[...] = jnp.zeros_like(dgate_ref)
            dcb_ref[...] = jnp.zeros_like(dcb_ref)
            dcw_ref[...] = jnp.zeros_like(dcw_ref)
            head_scr[...] = jnp.zeros_like(head_scr)
        dx = dx_ref[...]
        dgate_ref[...] += _colsum(dx * br_ref[...].astype(F32))
        dy_scr[...] = lax.dot_general((dx * gate_ref[...]).astype(BF16), wo_ref[...], NT_DIMS,
                                      preferred_element_type=F32)
        row = lax.broadcasted_iota(jnp.int32, (t, cwid), 0)
        has_prev = (i < n_t - 1).astype(F32)
        for c0 in range(0, e, cwid):
            sl = slice(c0, c0 + cwid)
            bg = p_ref[:, c0:c0 + cwid].astype(F32)
            cg = p_ref[:, e + c0:e + c0 + cwid].astype(F32)
            xin = p_ref[:, 2 * e + c0:2 * e + c0 + cwid].astype(F32)
            z = p_ref[:, 3 * e + c0:3 * e + c0 + cwid].astype(F32)
            cx = cg * xin
            p1, p2 = _conv_taps(cx, tails_ref[6:7, sl] * has_prev, tails_ref[7:8, sl] * has_prev, row)
            w0, w1, w2 = cw_ref[0:1, sl], cw_ref[1:2, sl], cw_ref[2:3, sl]
            conv = cb_ref[:, sl] + w2 * cx + w0 * p2 + w1 * p1
            sz, dsz = _silu(z)
            dy = dy_scr[:, sl]
            y_ref[:, sl] = (sz * bg * conv).astype(BF16)
            dp_ref[:, 3 * e + c0:3 * e + c0 + cwid] = (dy * bg * conv * dsz).astype(BF16)
            dp_ref[:, c0:c0 + cwid] = (dy * sz * conv).astype(BF16)
            dconv = dy * sz * bg
            dcb_ref[:, sl] += _colsum(dconv)
            dcw_ref[2:3, sl] += _colsum(dconv * cx)
            dcw_ref[1:2, sl] += _colsum(dconv * p1)
            dcw_ref[0:1, sl] += _colsum(dconv * p2)
            h0, h1 = head_scr[0:1, sl], head_scr[1:2, sl]
            n1 = jnp.where(row == t - 1, h0, pltpu.roll(dconv, t - 1, 0))
            n2 = jnp.where(row == t - 2, h0, jnp.where(row == t - 1, h1, pltpu.roll(dconv, t - 2, 0)))
            dcx = w2 * dconv + w1 * n1 + w0 * n2
            dp_ref[:, e + c0:e + c0 + cwid] = (dcx * xin).astype(BF16)
            dp_ref[:, 2 * e + c0:2 * e + c0 + cwid] = (dcx * cg).astype(BF16)
            head_scr[:, sl] = dconv[0:8, :]

    body = _carry(ex, body, 8, 5, lambda: pl.program_id(0) == 0, None, lambda: pl.program_id(0) == n_t - 1)
    ex_args, ex_in, ex_shapes, ex_out, ex_sems = _carried(ex)
    rev = lambda i: (n_t - 1 - i, 0)
    out = pl.pallas_call(
        body, name=name, grid=(n_t,),
        out_shape=(jax.ShapeDtypeStruct((s, e4), BF16), jax.ShapeDtypeStruct((s, e), BF16),
                   jax.ShapeDtypeStruct((1, d), F32), jax.ShapeDtypeStruct((1, e), F32), jax.ShapeDtypeStruct((8, e), F32),
                   *ex_shapes),
        in_specs=[pl.BlockSpec((t, e4), rev), pl.BlockSpec((t, d), rev), pl.BlockSpec((t, d), rev),
                  pl.BlockSpec((None, 8, e), lambda i: (jnp.maximum(n_t - 2 - i, 0), 0, 0)),
                  _full((3, e)), _full((1, e)), _full((1, d)), _full((e, d)), *ex_in],
        out_specs=(pl.BlockSpec((t, e4), rev), pl.BlockSpec((t, e), rev), _full((1, d)), _full((1, e)), _full((8, e)),
                   *ex_out),
        scratch_shapes=[pltpu.VMEM((t, e), F32), pltpu.VMEM((8, e), F32), *ex_sems],
        compiler_params=_params(("arbitrary",)),
    )(proj, dx1, br, tails, cw, cb, gate, wo, *ex_args)
    return (*out[:5], out[5:])


def _matmul_nt_norm_bwd(dproj, w, xin, dres, g, scale, name, ex=None):
    s, d = xin.shape
    n = w.shape[1]
    tm = min(s, 512)
    n_i = s // tm

    def body(dp_ref, w_ref, x_ref, dres_ref, g_ref, sc_ref, dx_ref, dsh_ref, dsc_ref, dg_ref, p_scr):
        i = pl.program_id(0)

        @pl.when(i == 0)
        def _():
            dsh_ref[...] = jnp.zeros_like(dsh_ref)
            p_scr[...] = jnp.zeros_like(p_scr)
        dh = lax.dot_general(dp_ref[...], w_ref[...], NT_DIMS, preferred_element_type=F32)
        _norm_modulate_bwd(dh, x_ref, dres_ref, g_ref, sc_ref, dx_ref, dsh_ref, p_scr)

        @pl.when(i == n_i - 1)
        def _():
            dsc_ref[...] = p_scr[...] * g_ref[...]
            dg_ref[...] = p_scr[...] * (1.0 + sc_ref[...])

    body = _carry(ex, body, 6, 4, lambda: pl.program_id(0) == 0, None, lambda: pl.program_id(0) == n_i - 1)
    ex_args, ex_in, ex_shapes, ex_out, ex_sems = _carried(ex)
    tok = pl.BlockSpec((tm, d), lambda i: (i, 0))
    vec = pl.BlockSpec((1, d), lambda i: (0, 0))
    vshape = jax.ShapeDtypeStruct((1, d), F32)
    out = pl.pallas_call(
        body, name=name, grid=(n_i,),
        out_shape=(jax.ShapeDtypeStruct((s, d), F32), vshape, vshape, vshape, *ex_shapes),
        in_specs=[pl.BlockSpec((tm, n), lambda i: (i, 0)), _resident((d, n)), tok, tok, vec, vec, *ex_in],
        out_specs=(tok, vec, vec, vec, *ex_out),
        scratch_shapes=[pltpu.VMEM((1, d), F32), *ex_sems],
        compiler_params=_params(("arbitrary",)),
    )(dproj, w, xin, dres, g, scale, *ex_args)
    return (*out[:4], out[4:])


def _matmul_tn(a, b, colscale, rows_split, name, ex=None):
    s, m = a.shape
    n = b.shape[1]
    n_j, tn = (1, n) if rows_split else (NDEV, n // NDEV)
    fixed = (4 + 4 + 2 * 2) * m * tn
    tk = s
    while fixed + 2 * tk * (2 * m + b.dtype.itemsize * tn) > TN_VMEM_BUDGET:
        tk //= 2
    n_k = s // tk

    def body(a_ref, b_ref, cs_ref, o_ref, acc):
        k = pl.program_id(1)
        part = lax.dot_general(a_ref[...], b_ref[...].astype(BF16), TN_DIMS, preferred_element_type=F32)
        if n_k == 1:
            o_ref[...] = (part * cs_ref[...]).astype(BF16)
            return

        @pl.when(k == 0)
        def _():
            acc[...] = part

        @pl.when((k > 0) & (k < n_k - 1))
        def _():
            acc[...] += part

        @pl.when(k == n_k - 1)
        def _():
            o_ref[...] = ((acc[...] + part) * cs_ref[...]).astype(BF16)

    at = lambda j, k: (pl.program_id(0) == j) & (pl.program_id(1) == k)
    body = _carry(ex, body, 3, 1, lambda: at(0, 0), None, lambda: at(n_j - 1, n_k - 1))
    ex_args, ex_in, ex_shapes, ex_out, ex_sems = _carried(ex)
    out = pl.pallas_call(
        body, name=name, grid=(n_j, n_k),
        out_shape=(jax.ShapeDtypeStruct((n_j, m, tn), BF16), *ex_shapes),
        in_specs=[pl.BlockSpec((tk, m), lambda j, k: (k, 0)), pl.BlockSpec((tk, tn), lambda j, k: (k, j)),
                  pl.BlockSpec((1, tn), lambda j, k: (0, j)), *ex_in],
        out_specs=(pl.BlockSpec((None, m, tn), lambda j, k: (j, 0, 0)), *ex_out),
        scratch_shapes=[pltpu.VMEM((m, tn), F32), *ex_sems],
        compiler_params=_params(("arbitrary", "arbitrary")),
    )(a, b, colscale, *ex_args)
    return (out[0].reshape(NDEV, m // NDEV, n) if rows_split else out[0]), out[1:]


def _adam_update(w, g, m, v):
    m = ADAM_B1 * m + (1.0 - ADAM_B1) * g
    v = ADAM_B2 * v + (1.0 - ADAM_B2) * (g * g)
    m_hat = m / (1.0 - ADAM_B1 ** ADAM_STEP)
    v_hat = v / (1.0 - ADAM_B2 ** ADAM_STEP)
    return -ADAM_LR * (m_hat / (jnp.sqrt(v_hat) + ADAM_EPS) + ADAM_WD * w), m, v


def _adamw_reduce(parts, w, m, v, name, ex=None):
    n_p, r, c = parts.shape
    tr = min(r, 128 if ex is not None else 256)
    n_i = r // tr

    def body(p_ref, w_ref, m_ref, v_ref, g_out, d_out, m_out, v_out):
        g = p_ref[0].astype(F32)
        for j in range(1, n_p):
            g = g + p_ref[j].astype(F32)
        g_out[...] = g
        d_out[...], m_out[...], v_out[...] = _adam_update(w_ref[...], g, m_ref[...], v_ref[...])

    step = lambda k: (lambda: pl.program_id(0) == k)
    body = _carry(ex, body, 4, 4, step(0), step(n_i // 2), step(n_i - 1))
    ex_args, ex_in, ex_shapes, ex_out, ex_sems = _carried(ex)
    blk = pl.BlockSpec((tr, c), lambda i: (i, 0))
    shp = jax.ShapeDtypeStruct((r, c), F32)
    out = pl.pallas_call(
        body, name=name, grid=(n_i,), out_shape=(shp, shp, shp, shp, *ex_shapes),
        in_specs=[pl.BlockSpec((n_p, tr, c), lambda i: (0, i, 0)), blk, blk, blk, *ex_in],
        out_specs=(blk, blk, blk, blk, *ex_out), scratch_shapes=ex_sems,
        compiler_params=_params(("arbitrary",)),
    )(parts, w, m, v, *ex_args)
    return (*out[:4], out[4:])


def _adamw_small(gs, ws, ms, vs, name):
    n = len(gs)

    def body(*refs):
        ins, outs = refs[:4 * n], refs[4 * n:]
        for a in range(n):
            d, m, v = _adam_update(ins[n + a][...], ins[a][...], ins[2 * n + a][...], ins[3 * n + a][...])
            outs[a][...], outs[n + a][...], outs[2 * n + a][...] = d, m, v

    shapes = tuple(jax.ShapeDtypeStruct(w.shape, F32) for w in ws) * 3
    out = pl.pallas_call(
        body, name=name, out_shape=shapes,
        in_specs=[_vmem()] * (4 * n), out_specs=tuple([_vmem()] * (3 * n)),
        compiler_params=pltpu.CompilerParams(vmem_limit_bytes=VMEM_LIMIT),
    )(*gs, *ws, *ms, *vs)
    return out[:n], out[n:2 * n], out[2 * n:]


def _sum_devices(packed, name):
    _, r, wdt = packed.shape

    def body(p_ref, o_ref):
        acc = p_ref[0]
        for j in range(1, NDEV):
            acc = acc + p_ref[j]
        o_ref[...] = acc

    return pl.pallas_call(
        body, name=name, out_shape=jax.ShapeDtypeStruct((r, wdt), F32),
        in_specs=[_vmem()], out_specs=_vmem(),
        compiler_params=pltpu.CompilerParams(vmem_limit_bytes=VMEM_LIMIT),
    )(packed)


def _mod_w_grad(c_t, dmod, name):
    n_layers, _, w3 = dmod.shape
    d = c_t.shape[0]

    def body(c_ref, dm_ref, o_ref):
        for i in range(n_layers):
            acc = c_ref[:, 0:1] * dm_ref[i, 0:1, :]
            for b in range(1, NDEV):
                acc = acc + c_ref[:, b:b + 1] * dm_ref[i, b:b + 1, :]
            o_ref[i] = acc

    return pl.pallas_call(
        body, name=name, out_shape=jax.ShapeDtypeStruct((n_layers, d, w3), F32),
        in_specs=[_vmem(), _vmem()], out_specs=_vmem(),
        compiler_params=pltpu.CompilerParams(vmem_limit_bytes=VMEM_LIMIT),
    )(c_t, dmod)


def _mask_transpose_ws(w_s, name):
    def body(w_ref, wt_ref, wtt_ref):
        tril = (lax.broadcasted_iota(jnp.int32, (CHUNK, CHUNK), 0) >= lax.broadcasted_iota(jnp.int32, (CHUNK, CHUNK), 1))
        for g in range(GROUPS):
            wm = jnp.where(tril, w_ref[g], 0.0)
            wt_ref[g] = wm.astype(BF16)
            wtt_ref[g] = wm.T.astype(BF16)

    shp = jax.ShapeDtypeStruct(w_s.shape, BF16)
    return pl.pallas_call(
        body, name=name, out_shape=(shp, shp), in_specs=[_vmem()], out_specs=(_vmem(), _vmem()),
    )(w_s)


def _pack(pieces):
    flat = jnp.concatenate([p.reshape(-1) for p in pieces])
    rows = -(-flat.shape[0] // (8 * PACK_W)) * 8
    return jnp.pad(flat, (0, rows * PACK_W - flat.shape[0])).reshape(rows, PACK_W)


def _unpack(flat, shapes):
    out, off = [], 0
    for shp in shapes:
        size = 1
        for dim in shp:
            size *= dim
        out.append(flat[off:off + size].reshape(shp))
        off += size
    return out


def kernel(x, c, mod_w, mod_b, norm_g, a_w_in, a_conv_w, a_conv_b, a_w_out, b_w_in, b_ln_g, b_ln_b, b_w_s, b_b_s, b_w_out, final_g, loss_target, m_mod_w, m_mod_b, m_norm_g, m_a_w_in, m_a_conv_w, m_a_conv_b, m_a_w_out, m_b_w_in, m_b_ln_g, m_b_ln_b, m_b_w_s, m_b_b_s, m_b_w_out, m_final_g, v_mod_w, v_mod_b, v_norm_g, v_a_w_in, v_a_conv_w, v_a_conv_b, v_a_w_out, v_b_w_in, v_b_ln_g, v_b_ln_b, v_b_w_s, v_b_b_s, v_b_w_out, v_final_g):
    s, d = x.shape[1], x.shape[2]
    es = a_w_out.shape[1]
    e = NDEV * es
    w3 = mod_w.shape[2]
    me = _index(_pos())
    x0 = x.reshape(s, d)
    tgt = loss_target.reshape(s, d)

    small = jnp.concatenate([a_conv_w[0], b_ln_g, b_ln_b, jnp.zeros((3, es), F32)], axis=0)
    gather_a = _gather_exchange([a_w_in[0].astype(BF16), a_w_out[0].astype(BF16), small], [True, False, True])
    gather_b = _gather_exchange([b_w_in[0].astype(BF16), b_w_out[0].astype(BF16)], [True, False])
    mod, c_all, (wa, woa, small_all) = _mod_vectors(c, mod_w, mod_b, gather_a)
    conv_w, ln_g, ln_b = small_all[0:3], small_all[3:4], small_all[4:5]
    bsf = jnp.repeat(b_b_s[0].T, e // GROUPS, axis=1)
    wt, wtt = _mask_transpose_ws(b_w_s[0], "mask_w_s")
    shift0, scale0, gate0 = mod[0:1, 0:d], mod[0:1, d:2 * d], mod[0:1, 2 * d:]
    shift1, scale1, gate1 = mod[1:2, 0:d], mod[1:2, d:2 * d], mod[1:2, 2 * d:]
    g0, g1, fg = norm_g[0:1], norm_g[1:2], final_g.reshape(1, d)

    proj_a, h0, x1, br_a, tails, (wb, wob) = _layer_a_fwd(
        x0, g0, scale0, shift0, gate0, wa, conv_w, a_conv_b, woa, "a_fwd", gather_b)
    proj_b, h1, dx2, loss_acc, dfg, dgate1, _ = _layer_b_fwd_loss(
        x1, tgt, g1, scale1, shift1, gate1, fg, wb, ln_g, ln_b, wt, bsf, wob, "b_fwd_loss")

    dproj_b, y_b, dx1, dws, dbs, dlg, dlb, dshift1, dscale1, dg1 = _layer_b_bwd(
        proj_b, dx2, x1, gate1, g1, scale1, ln_g, ln_b, wt, wtt, bsf, wob, wb, "b_bwd")
    gs_b_out, _ = _matmul_tn(y_b, dx2, gate1, True, "b_w_out_grad")
    gs_b_in, _ = _matmul_tn(h1, dproj_b, jnp.ones((1, dproj_b.shape[1]), F32), False, "b_w_in_grad")
    dproj_a, y_a, dgate0, dcb, dcw, (gr_b_in, gr_b_out) = _conv_mixer_bwd(
        proj_a, dx1, br_a, tails, conv_w, a_conv_b, gate0, woa, "a_mixer_bwd", _scatter_exchange([gs_b_in, gs_b_out]))
    gs_a_out, _ = _matmul_tn(y_a, dx1, gate0, True, "a_w_out_grad")
    gs_a_in, (gr_a_out,) = _matmul_tn(h0, dproj_a, jnp.ones((1, dproj_a.shape[1]), F32), False, "a_w_in_grad",
                                      _scatter_exchange([gs_a_out]))
    dx0, dshift0, dscale0, dg0, (gr_a_in,) = _matmul_nt_norm_bwd(dproj_a, wa, x0, dx1, g0, scale0, "a_in_bwd",
                                                                 _scatter_exchange([gs_a_in]))

    def big(parts, w, m, v, name, ex=None):
        shp = w.shape
        r2 = lambda t_: t_.reshape(-1, shp[-1])
        g, dl, nm, nv, ex_out = _adamw_reduce(parts, r2(w), r2(m), r2(v), name, ex)
        return tuple(t_.reshape(shp) for t_ in (g, dl, nm, nv)), ex_out

    pieces = [dshift0, dscale0, dgate0, dshift1, dscale1, dgate1, dg0, dg1, dcb, dcw[0:3], dlg, dlb, dfg,
              dbs[:, 0:GROUPS].T, dws, loss_acc[0:1, 0:1]]
    shapes = [p.shape for p in pieces]
    res = {}
    res["a_w_in"], (packed_all,) = big(gr_a_in, a_w_in, m_a_w_in, v_a_w_in, "adamw_a_w_in",
                                       _gather_exchange([_pack(pieces)], [False]))
    packed_all = packed_all.reshape(NDEV, -1, PACK_W)
    total = _sum_devices(packed_all, "sum_small_grads").reshape(-1)
    (t_sh0, t_sc0, t_ga0, t_sh1, t_sc1, t_ga1, t_g0, t_g1, t_cb, t_cw, t_lg, t_lb, t_fg, t_bs, t_ws, t_loss) = _unpack(
        total, shapes)
    loss = t_loss.reshape(())
    grad_mod_b = jnp.concatenate([jnp.concatenate([t_sh0, t_sc0, t_ga0], axis=1),
                                  jnp.concatenate([t_sh1, t_sc1, t_ga1], axis=1)], axis=0)
    grad_norm_g = jnp.concatenate([t_g0, t_g1], axis=0)
    dmod_all = packed_all.reshape(NDEV, -1)[:, 0:6 * d].reshape(NDEV, 2, 3 * d).transpose(1, 0, 2)
    dmod_mine = lax.dynamic_slice_in_dim(dmod_all, me * w3, w3, axis=2)
    grad_mod_w = _mod_w_grad(c_all.T, dmod_mine, "mod_w_grad")
    grad_a_conv_w = lax.dynamic_slice_in_dim(t_cw, me * es, es, axis=1)
    grad_b_ln_g = lax.dynamic_slice_in_dim(t_lg, me * es, es, axis=1)
    grad_b_ln_b = lax.dynamic_slice_in_dim(t_lb, me * es, es, axis=1)

    res["mod_w"], _ = big(grad_mod_w.reshape(1, -1, w3), mod_w, m_mod_w, v_mod_w, "adamw_mod_w")
    res["a_w_out"], _ = big(gr_a_out, a_w_out, m_a_w_out, v_a_w_out, "adamw_a_w_out")
    res["b_w_in"], _ = big(gr_b_in, b_w_in, m_b_w_in, v_b_w_in, "adamw_b_w_in")
    res["b_w_out"], _ = big(gr_b_out, b_w_out, m_b_w_out, v_b_w_out, "adamw_b_w_out")

    small_names = ["mod_b", "norm_g", "a_conv_w", "a_conv_b", "b_ln_g", "b_ln_b", "b_w_s", "b_b_s", "final_g"]
    small_g = [grad_mod_b, grad_norm_g, grad_a_conv_w, t_cb, grad_b_ln_g, grad_b_ln_b, t_ws, t_bs, t_fg]
    small_w = [mod_b, norm_g, a_conv_w, a_conv_b, b_ln_g, b_ln_b, b_w_s, b_b_s, final_g]
    small_m = [m_mod_b, m_norm_g, m_a_conv_w, m_a_conv_b, m_b_ln_g, m_b_ln_b, m_b_w_s, m_b_b_s, m_final_g]
    small_v = [v_mod_b, v_norm_g, v_a_conv_w, v_a_conv_b, v_b_ln_g, v_b_ln_b, v_b_w_s, v_b_b_s, v_final_g]
    as2d = lambda t_: t_.reshape(-1, t_.shape[-1])
    dls, nms, nvs = _adamw_small([as2d(t_) for t_ in small_g], [as2d(t_) for t_ in small_w],
                                 [as2d(t_) for t_ in small_m], [as2d(t_) for t_ in small_v], "adamw_small")
    for a, nme in enumerate(small_names):
        shp = small_w[a].shape
        res[nme] = (small_g[a].reshape(shp), dls[a].reshape(shp), nms[a].reshape(shp), nvs[a].reshape(shp))

    order = ["mod_w", "mod_b", "norm_g", "a_w_in", "a_conv_w", "a_conv_b", "a_w_out", "b_w_in", "b_ln_g", "b_ln_b",
             "b_w_s", "b_b_s", "b_w_out", "final_g"]
    return (loss, dx0.reshape(x.shape), *[res[k][0] for k in order], *[res[k][1] for k in order],
            *[res[k][2] for k in order], *[res[k][3] for k in order])
```

```python
import jax
import jax.numpy as jnp
from jax import lax
from jax.experimental import pallas as pl
from jax.experimental.pallas import tpu as pltpu

NDEV = 8
CHUNK = 128
GROUPS = 8
RMS_EPS = 1e-6
LN_EPS = 1e-5
ADAM_LR, ADAM_B1, ADAM_B2, ADAM_EPS, ADAM_WD, ADAM_STEP = 0.001, 0.9, 0.999, 1e-08, 0.01, 10
V7X_VMEM_BYTES = 64 * 1024 * 1024
VMEM_LIMIT = V7X_VMEM_BYTES - 8 * 1024 * 1024
TN_VMEM_BUDGET = 46 * 1024 * 1024
PACK_W = 1024
F32, BF16 = jnp.float32, jnp.bfloat16
MESH = pl.DeviceIdType.MESH
RSQRT2 = 0.7071067811865476
INV_SQRT_2PI = 0.3989422804014327
NT_DIMS = (((1,), (1,)), ((), ()))
TN_DIMS = (((0,), (0,)), ((), ()))


def _params(sem=None):
    return pltpu.CompilerParams(dimension_semantics=sem, vmem_limit_bytes=VMEM_LIMIT)


def _vmem():
    return pl.BlockSpec(memory_space=pltpu.VMEM)


def _hbm():
    return pl.BlockSpec(memory_space=pltpu.HBM)


def _full(shape):
    return pl.BlockSpec(shape, lambda *_: (0,) * len(shape))


def _pos():
    return lax.axis_index("x"), lax.axis_index("y"), lax.axis_index("c")


def _index(p):
    return 4 * p[0] + 2 * p[1] + p[2]


def _peer(k):
    x, y, c = _pos()
    return ((1 - x) if (k >> 2) & 1 else x, (1 - y) if (k >> 1) & 1 else y, (1 - c) if k & 1 else c)


def _silu(z):
    sg = jax.nn.sigmoid(z)
    return z * sg, sg * (1.0 + z * (1.0 - sg))


def _gelu(v):
    phi = 0.5 * (1.0 + lax.erf(v * RSQRT2))
    return v * phi, phi + v * (jnp.exp(-0.5 * v * v) * INV_SQRT_2PI)


def _colsum(v):
    return jnp.sum(v, axis=0, keepdims=True)


def _rowsum(v):
    return jnp.sum(v, axis=-1, keepdims=True)


def _gather_all_vmem(slab_ref, send_sems, recv_sems, base):
    me = _index(_pos())
    sends = []
    for k in range(1, NDEV):
        cp = pltpu.make_async_remote_copy(
            src_ref=slab_ref.at[me], dst_ref=slab_ref.at[me],
            send_sem=send_sems.at[base + k - 1], recv_sem=recv_sems.at[base + k - 1],
            device_id=_peer(k), device_id_type=MESH)
        cp.start()
        sends.append(cp)
    for k in range(1, NDEV):
        src = _index(_peer(k))
        pltpu.make_async_remote_copy(
            src_ref=slab_ref.at[src], dst_ref=slab_ref.at[src],
            send_sem=send_sems.at[base + k - 1], recv_sem=recv_sems.at[base + k - 1],
            device_id=_peer(k), device_id_type=MESH).wait_recv()
    for cp in sends:
        cp.wait_send()


def _mod_vectors(c, mod_w, mod_b, ex):
    n_layers, d, w3 = mod_w.shape
    r_in, r_out = len(ex.arrays), len(ex.out_shapes)

    def body(*refs):
        c_ref, mw_ref, mb_ref = refs[:3]
        ex_ins = refs[3:3 + r_in]
        mod_ref, call_ref = refs[3 + r_in:5 + r_in]
        ex_outs = refs[5 + r_in:5 + r_in + r_out]
        cslab, pslab, send_sems, recv_sems = refs[5 + r_in + r_out:9 + r_in + r_out]
        ex_sems = refs[9 + r_in + r_out:]
        ex.start(ex_ins, ex_outs, ex_sems)
        me = _index(_pos())
        cv = c_ref[...]
        cslab[me] = jnp.broadcast_to(cv * jax.nn.sigmoid(cv), (8, d))
        _gather_all_vmem(cslab, send_sems, recv_sems, 0)
        c_all = jnp.concatenate([cslab[k, 0:1, :] for k in range(NDEV)], axis=0)
        call_ref[...] = c_all
        for i in range(n_layers):
            pslab[me, i * NDEV:(i + 1) * NDEV, :] = jnp.dot(
                c_all, mw_ref[i], preferred_element_type=F32, precision=lax.Precision.HIGHEST)
        _gather_all_vmem(pslab, send_sems, recv_sems, NDEV - 1)
        for i in range(n_layers):
            for k in range(NDEV):
                mod_ref[i:i + 1, k * w3:(k + 1) * w3] = (
                    pslab[k, pl.ds(i * NDEV + me, 1), :] + mb_ref[i:i + 1, k * w3:(k + 1) * w3])
        if ex.middle is not None:
            ex.middle(ex_ins, ex_outs, ex_sems)
        ex.finish(ex_ins, ex_outs, ex_sems)

    out = pl.pallas_call(
        body, name="mod_vectors",
        out_shape=(jax.ShapeDtypeStruct((n_layers, 3 * d), F32), jax.ShapeDtypeStruct((NDEV, d), F32), *ex.out_shapes),
        in_specs=[_vmem(), _vmem(), _vmem()] + [_hbm()] * r_in, out_specs=(_vmem(), _vmem(), *([_hbm()] * r_out)),
        scratch_shapes=[pltpu.VMEM((NDEV, 8, d), F32), pltpu.VMEM((NDEV, n_layers * NDEV, w3), F32),
                        pltpu.SemaphoreType.DMA((2 * (NDEV - 1),)), pltpu.SemaphoreType.DMA((2 * (NDEV - 1),)), *ex.sems],
        compiler_params=pltpu.CompilerParams(vmem_limit_bytes=VMEM_LIMIT),
    )(c, mod_w, mod_b, *ex.arrays)
    return out[0], out[1], out[2:]


class _Exchange:
    def __init__(self, arrays, out_shapes, sems, start, middle, finish):
        self.arrays, self.out_shapes, self.sems = list(arrays), list(out_shapes), list(sems)
        self.start, self.middle, self.finish = start, middle, finish


def _gather_exchange(shards, by_cols):
    n = len(shards)
    shapes = [sh.shape for sh in shards]

    def tools(ins, outs, sems):
        send_sems, recv_sems, local_sems = sems
        x, y, c = _pos()
        chips = [(1 - x, y), (x, 1 - y), (1 - x, 1 - y)]

        def place(a, block):
            r, cc = shapes[a]
            if by_cols[a]:
                return outs[a].at[:, pl.ds(_index(block) * cc, cc)]
            return outs[a].at[pl.ds(_index(block) * r, r), :]

        def copy(a, k, block, to, src=None):
            dst = place(a, block)
            return pltpu.make_async_remote_copy(
                src_ref=dst if src is None else src, dst_ref=dst,
                send_sem=send_sems.at[a * 7 + k], recv_sem=recv_sems.at[a * 7 + k],
                device_id=to, device_id_type=MESH)

        mine = [pltpu.make_async_copy(ins[a], place(a, (x, y, c)), local_sems.at[a]) for a in range(n)]
        first = []
        for a in range(n):
            first.append(copy(a, 0, (x, y, c), (x, y, 1 - c), src=ins[a]))
            first += [copy(a, 1 + j, (x, y, c), (*chip, c), src=ins[a]) for j, chip in enumerate(chips)]
        passed = [copy(a, 4 + j, (*chip, c), (x, y, 1 - c)) for j, chip in enumerate(chips) for a in range(n)]
        return (x, y, c), chips, copy, mine, first, passed

    def start(ins, outs, sems):
        _, _, _, mine, first, _ = tools(ins, outs, sems)
        for cp in mine + first:
            cp.start()

    def middle(ins, outs, sems):
        (x, y, c), chips, copy, _, _, passed = tools(ins, outs, sems)
        for j, chip in enumerate(chips):
            for a in range(n):
                copy(a, 1 + j, (*chip, c), (x, y, c)).wait_recv()
                passed[j * n + a].start()

    def finish(ins, outs, sems):
        (x, y, c), chips, copy, mine, first, passed = tools(ins, outs, sems)
        for a in range(n):
            copy(a, 0, (x, y, 1 - c), (x, y, c)).wait_recv()
        for j, chip in enumerate(chips):
            for a in range(n):
                copy(a, 4 + j, (*chip, 1 - c), (x, y, c)).wait_recv()
        for cp in first + passed:
            cp.wait_send()
        for cp in mine:
            cp.wait()

    out_shapes = [jax.ShapeDtypeStruct((r, NDEV * cc) if bc else (NDEV * r, cc), sh.dtype)
                  for (r, cc), bc, sh in zip(shapes, by_cols, shards)]
    sems = [pltpu.SemaphoreType.DMA((7 * n,)), pltpu.SemaphoreType.DMA((7 * n,)), pltpu.SemaphoreType.DMA((n,))]
    return _Exchange(shards, out_shapes, sems, start, middle, finish)


def _scatter_exchange(parts):
    n = len(parts)

    def tools(ins, outs, sems):
        send_sems, recv_sems, local_sems = sems
        me = _index(_pos())
        mine = [pltpu.make_async_copy(ins[a].at[me], outs[a].at[me], local_sems.at[a]) for a in range(n)]
        sends, arrivals = [], []
        for k in range(1, NDEV):
            peer = _peer(k)
            for a in range(n):
                pair = dict(send_sem=send_sems.at[a * 7 + k - 1], recv_sem=recv_sems.at[a * 7 + k - 1],
                            device_id=peer, device_id_type=MESH)
                sends.append(pltpu.make_async_remote_copy(src_ref=ins[a].at[_index(peer)], dst_ref=outs[a].at[me], **pair))
                slot = outs[a].at[_index(peer)]
                arrivals.append(pltpu.make_async_remote_copy(src_ref=slot, dst_ref=slot, **pair))
        return mine, sends, arrivals

    def start(ins, outs, sems):
        mine, sends, _ = tools(ins, outs, sems)
        for cp in mine + sends:
            cp.start()

    def finish(ins, outs, sems):
        mine, sends, arrivals = tools(ins, outs, sems)
        for cp in arrivals:
            cp.wait_recv()
        for cp in sends:
            cp.wait_send()
        for cp in mine:
            cp.wait()

    out_shapes = [jax.ShapeDtypeStruct(p.shape, p.dtype) for p in parts]
    sems = [pltpu.SemaphoreType.DMA((7 * n,)), pltpu.SemaphoreType.DMA((7 * n,)), pltpu.SemaphoreType.DMA((n,))]
    return _Exchange(parts, out_shapes, sems, start, None, finish)


def _carry(ex, body, n_in, n_out, first, middle, last):
    if ex is None:
        return body
    r_in, r_out = len(ex.arrays), len(ex.out_shapes)

    def wrapped(*refs):
        ins, rins = refs[:n_in], refs[n_in:n_in + r_in]
        outs = refs[n_in + r_in:n_in + r_in + n_out]
        routs = refs[n_in + r_in + n_out:n_in + r_in + n_out + r_out]
        rest = refs[n_in + r_in + n_out + r_out:]
        scratch, sems = rest[:len(rest) - len(ex.sems)], rest[len(rest) - len(ex.sems):]

        @pl.when(first())
        def _():
            ex.start(rins, routs, sems)

        if ex.middle is not None:
            @pl.when(middle())
            def _():
                ex.middle(rins, routs, sems)

        body(*ins, *outs, *scratch)

        @pl.when(last())
        def _():
            ex.finish(rins, routs, sems)

    return wrapped


def _carried(ex):
    if ex is None:
        return [], [], [], [], []
    return ex.arrays, [_hbm()] * len(ex.arrays), ex.out_shapes, [_hbm()] * len(ex.out_shapes), ex.sems


def _resident(shape):
    return pl.BlockSpec(shape, lambda *_: (0,) * len(shape), pipeline_mode=pl.Buffered(1))


def _norm_modulate(x_ref, g_ref, sc_ref, sh_ref):
    xv = x_ref[...]
    r = lax.rsqrt(jnp.mean(xv * xv, axis=-1, keepdims=True) + RMS_EPS)
    return ((xv * r) * g_ref[...] * (1.0 + sc_ref[...]) + sh_ref[...]).astype(BF16)


def _conv_taps(cx, t6, t7, row):
    p1 = jnp.where(row == 0, t7, pltpu.roll(cx, 1, 0))
    p2 = jnp.where(row == 0, t6, jnp.where(row == 1, t7, pltpu.roll(cx, 2, 0)))
    return p1, p2


def _layer_a_fwd(x, g, scale, shift, gate, wi, cw, cb, wo, name, ex=None):
    s, d = x.shape
    e = wo.shape[0]
    t = min(s, 256)
    n_t = s // t
    cwid = min(e, 512)

    def body(x_ref, g_ref, sc_ref, sh_ref, gate_ref, wi_ref, cw_ref, cb_ref, wo_ref,
             proj_ref, h_ref, x1_ref, br_ref, tails_ref, y_scr, tail_scr):
        @pl.when(pl.program_id(0) == 0)
        def _():
            tail_scr[...] = jnp.zeros_like(tail_scr)
        h_ref[...] = _norm_modulate(x_ref, g_ref, sc_ref, sh_ref)
        row = lax.broadcasted_iota(jnp.int32, (t, cwid), 0)

        def project(c0):
            v = jnp.dot(h_ref[...], wi_ref[:, c0:c0 + cwid], preferred_element_type=F32)
            proj_ref[:, c0:c0 + cwid] = v.astype(BF16)
            return v

        for c0 in range(0, e, cwid):
            sl = slice(c0, c0 + cwid)
            bg, z = project(c0), project(3 * e + c0)
            cx = project(e + c0) * project(2 * e + c0)
            p1, p2 = _conv_taps(cx, tail_scr[6:7, sl], tail_scr[7:8, sl], row)
            conv = cb_ref[:, sl] + cw_ref[2:3, sl] * cx + cw_ref[0:1, sl] * p2 + cw_ref[1:2, sl] * p1
            y_scr[:, sl] = (_silu(z)[0] * bg * conv).astype(BF16)
            tail_scr[:, sl] = cx[t - 8:t, :]
        tails_ref[...] = tail_scr[...]
        br = jnp.dot(y_scr[...], wo_ref[...], preferred_element_type=F32)
        x1_ref[...] = x_ref[...] + gate_ref[...] * br
        br_ref[...] = br.astype(BF16)

    step = lambda k: (lambda: pl.program_id(0) == k)
    body = _carry(ex, body, 9, 5, step(0), step(n_t // 2), step(n_t - 1))
    ex_args, ex_in, ex_shapes, ex_out, ex_sems = _carried(ex)
    tok = pl.BlockSpec((t, d), lambda i: (i, 0))
    out = pl.pallas_call(
        body, name=name, grid=(n_t,),
        out_shape=(jax.ShapeDtypeStruct((s, 4 * e), BF16), jax.ShapeDtypeStruct((s, d), BF16),
                   jax.ShapeDtypeStruct((s, d), F32), jax.ShapeDtypeStruct((s, d), BF16),
                   jax.ShapeDtypeStruct((n_t, 8, e), F32), *ex_shapes),
        in_specs=[tok, _full((1, d)), _full((1, d)), _full((1, d)), _full((1, d)), _resident((d, 4 * e)),
                  _full((3, e)), _full((1, e)), _resident((e, d)), *ex_in],
        out_specs=(pl.BlockSpec((t, 4 * e), lambda i: (i, 0)), tok, tok, tok,
                   pl.BlockSpec((None, 8, e), lambda i: (i, 0, 0)), *ex_out),
        scratch_shapes=[pltpu.VMEM((t, e), BF16), pltpu.VMEM((8, e), F32), *ex_sems],
        compiler_params=_params(("arbitrary",)),
    )(x, g, scale, shift, gate, wi, cw, cb, wo, *ex_args)
    return (*out[:5], out[5:])


def _ln_stats(v_pre, v_scr, dgv_scr, t, e):
    gw = e // GROUPS
    s1 = jnp.zeros((t, 1), F32)
    for g in range(GROUPS):
        v, dgv = _gelu(v_pre(g))
        v_scr[:, g * gw:(g + 1) * gw] = v
        if dgv_scr is not None:
            dgv_scr[:, g * gw:(g + 1) * gw] = dgv.astype(dgv_scr.dtype)
        s1 = s1 + _rowsum(v)
    mu = s1 * (1.0 / e)
    s2 = jnp.zeros((t, 1), F32)
    for g in range(GROUPS):
        dv = v_scr[:, g * gw:(g + 1) * gw] - mu
        s2 = s2 + _rowsum(dv * dv)
    return mu, lax.rsqrt(s2 * (1.0 / e) + LN_EPS)


def _layer_b_fwd_loss(x1, tgt, g1, scale, shift, gate, fg, wi, lng, lnb, wt, bsf, wo, name, ex=None):
    s, d = x1.shape
    e = wo.shape[0]
    gw = e // GROUPS
    t = min(s, 256)
    n_t = s // t

    def body(x1_ref, tgt_ref, g_ref, sc_ref, sh_ref, gate_ref, fg_ref, wi_ref, lng_ref, lnb_ref, wt_ref, bsf_ref, wo_ref,
             proj_ref, h_ref, dx2_ref, loss_ref, dfg_ref, dgate_ref, v_scr, y_scr):
        @pl.when(pl.program_id(0) == 0)
        def _():
            loss_ref[...] = jnp.zeros_like(loss_ref)
            dfg_ref[...] = jnp.zeros_like(dfg_ref)
            dgate_ref[...] = jnp.zeros_like(dgate_ref)
        h_ref[...] = _norm_modulate(x1_ref, g_ref, sc_ref, sh_ref)

        def project(c0):
            v = jnp.dot(h_ref[...], wi_ref[:, c0:c0 + gw], preferred_element_type=F32)
            proj_ref[:, c0:c0 + gw] = v.astype(BF16)
            return v

        mu, rs = _ln_stats(lambda g: project(e + g * gw), v_scr, None, t, e)
        for g in range(GROUPS):
            gs = slice(g * gw, (g + 1) * gw)
            vn = (((v_scr[:, gs] - mu) * rs) * lng_ref[:, gs] + lnb_ref[:, gs]).astype(BF16)
            u = _gelu(project(g * gw))[0]
            sz = _silu(project(2 * e + g * gw))[0]
            for ch in range(t // CHUNK):
                rows = slice(ch * CHUNK, (ch + 1) * CHUNK)
                mixed = jnp.dot(wt_ref[g], vn[rows], preferred_element_type=F32) + bsf_ref[:, gs]
                y_scr[rows, gs] = (sz[rows] * (u[rows] * mixed)).astype(BF16)
        br = jnp.dot(y_scr[...], wo_ref[...], preferred_element_type=F32)
        x2 = x1_ref[...] + gate_ref[...] * br
        r2 = lax.rsqrt(jnp.mean(x2 * x2, axis=-1, keepdims=True) + RMS_EPS)
        xn = x2 * r2
        diff = xn * fg_ref[...] - tgt_ref[...]
        loss_ref[...] += jnp.broadcast_to(0.5 * _colsum(jnp.mean(diff * diff, axis=-1, keepdims=True)), loss_ref.shape)
        dout = diff * (1.0 / d)
        dfg_ref[...] += _colsum(dout * xn)
        dxn = dout * fg_ref[...]
        dx2 = r2 * (dxn - xn * jnp.mean(dxn * xn, axis=-1, keepdims=True))
        dx2_ref[...] = dx2
        dgate_ref[...] += _colsum(dx2 * br)

    step = lambda k: (lambda: pl.program_id(0) == k)
    body = _carry(ex, body, 13, 6, step(0), step(n_t // 2), step(n_t - 1))
    ex_args, ex_in, ex_shapes, ex_out, ex_sems = _carried(ex)
    tok = pl.BlockSpec((t, d), lambda i: (i, 0))
    vec = _full((1, d))
    out = pl.pallas_call(
        body, name=name, grid=(n_t,),
        out_shape=(jax.ShapeDtypeStruct((s, 3 * e), BF16), jax.ShapeDtypeStruct((s, d), BF16),
                   jax.ShapeDtypeStruct((s, d), F32), jax.ShapeDtypeStruct((8, 128), F32),
                   jax.ShapeDtypeStruct((1, d), F32), jax.ShapeDtypeStruct((1, d), F32), *ex_shapes),
        in_specs=[tok, tok, vec, vec, vec, vec, vec, _resident((d, 3 * e)), _full((1, e)), _full((1, e)),
                  _full((GROUPS, CHUNK, CHUNK)), _resident((CHUNK, e)), _resident((e, d)), *ex_in],
        out_specs=(pl.BlockSpec((t, 3 * e), lambda i: (i, 0)), tok, tok, _full((8, 128)), vec, vec, *ex_out),
        scratch_shapes=[pltpu.VMEM((t, e), F32), pltpu.VMEM((t, e), BF16), *ex_sems],
        compiler_params=_params(("arbitrary",)),
    )(x1, tgt, g1, scale, shift, gate, fg, wi, lng, lnb, wt, bsf, wo, *ex_args)
    return (*out[:6], out[6:])


def _norm_modulate_bwd(dh, x_ref, dres_ref, g_ref, sc_ref, dx_ref, dsh_ref, p_scr):
    xv = x_ref[...]
    r = lax.rsqrt(jnp.mean(xv * xv, axis=-1, keepdims=True) + RMS_EPS)
    xn = xv * r
    dsh_ref[...] += _colsum(dh)
    p_scr[...] += _colsum(dh * xn)
    dxn = dh * (g_ref[...] * (1.0 + sc_ref[...]))
    dx_ref[...] = r * (dxn - xn * jnp.mean(dxn * xn, axis=-1, keepdims=True)) + dres_ref[...]


def _layer_b_bwd(proj, dx2, x1, gate, g1, scale, lng, lnb, wt, wtt, bsf, wo, wi, name):
    s, e3 = proj.shape
    e = e3 // 3
    d = dx2.shape[1]
    gw = e // GROUPS
    t = min(s, 256)
    n_t = s // t

    def body(p_ref, dx_ref, x1_ref, gate_ref, g_ref, sc_ref, lng_ref, lnb_ref, wt_ref, wtt_ref, bsf_ref, wo_ref, wi_ref,
             dp_ref, y_ref, dx1_ref, dws_ref, dbs_ref, dlg_ref, dlb_ref, dsh_ref, dsc_ref, dg_ref,
             v_scr, dgv_scr, dbr_scr, dvn_scr, dbs_scr, p_scr, dy_scr, vn_scr, mixed_scr):
        @pl.when(pl.program_id(0) == 0)
        def _():
            dws_ref[...] = jnp.zeros_like(dws_ref)
            dlg_ref[...] = jnp.zeros_like(dlg_ref)
            dlb_ref[...] = jnp.zeros_like(dlb_ref)
            dsh_ref[...] = jnp.zeros_like(dsh_ref)
            dbs_scr[...] = jnp.zeros_like(dbs_scr)
            p_scr[...] = jnp.zeros_like(p_scr)
        dbr_scr[...] = (dx_ref[...] * gate_ref[...]).astype(BF16)
        mu, rs = _ln_stats(lambda g: p_ref[:, e + g * gw:e + (g + 1) * gw].astype(F32), v_scr, dgv_scr, t, e)
        tril = (lax.broadcasted_iota(jnp.int32, (CHUNK, CHUNK), 0) >= lax.broadcasted_iota(jnp.int32, (CHUNK, CHUNK), 1))
        c1 = jnp.zeros((t, 1), F32)
        c2 = jnp.zeros((t, 1), F32)
        span = 2
        kw = span * gw
        dh = jnp.zeros((t, d), F32)

        def through_w_in(c0):
            return lax.dot_general(dp_ref[:, c0:c0 + kw], wi_ref[:, c0:c0 + kw], NT_DIMS, preferred_element_type=F32)

        dy_scr[...] = lax.dot_general(dbr_scr[...], wo_ref[...], NT_DIMS, preferred_element_type=F32)
        for g in range(GROUPS):
            gs = slice(g * gw, (g + 1) * gw)
            vhat = (v_scr[:, gs] - mu) * rs
            v_scr[:, gs] = vhat
            vn = (vhat * lng_ref[:, gs] + lnb_ref[:, gs]).astype(BF16)
            vn_scr[:, gs] = vn
            for ch in range(t // CHUNK):
                rows = slice(ch * CHUNK, (ch + 1) * CHUNK)
                mixed_scr[rows, gs] = jnp.dot(wt_ref[g], vn[rows], preferred_element_type=F32) + bsf_ref[:, gs]
        for g in range(GROUPS):
            gs = slice(g * gw, (g + 1) * gw)
            vhat = v_scr[:, gs]
            lg = lng_ref[:, gs]
            for ch in range(t // CHUNK):
                rows = slice(ch * CHUNK, (ch + 1) * CHUNK)
                mixed = mixed_scr[rows, gs]
                u, dgu = _gelu(p_ref[rows, g * gw:(g + 1) * gw].astype(F32))
                sz, dsz = _silu(p_ref[rows, 2 * e + g * gw:2 * e + (g + 1) * gw].astype(F32))
                sgate = u * mixed
                y_ref[rows, gs] = (sz * sgate).astype(BF16)
                dy = dy_scr[rows, gs]
                dp_ref[rows, 2 * e + g * gw:2 * e + (g + 1) * gw] = (dy * sgate * dsz).astype(BF16)
                ds = dy * sz
                dp_ref[rows, gs] = (ds * mixed * dgu).astype(BF16)
                dm = ds * u
                dbs_scr[:, gs] += dm
                dmb = dm.astype(BF16)
                dws_ref[g] += jnp.where(tril, lax.dot_general(dmb, vn_scr[rows, gs], NT_DIMS, preferred_element_type=F32), 0.0)
                dvn_scr[rows, gs] = jnp.dot(wtt_ref[g], dmb, preferred_element_type=F32)
            dvn = dvn_scr[:, gs]
            dlb_ref[:, gs] += _colsum(dvn)
            dlg_ref[:, gs] += _colsum(dvn * vhat)
            dvh = dvn * lg
            c1 = c1 + _rowsum(dvh)
            c2 = c2 + _rowsum(dvh * vhat)
            if g % span == span - 1:
                dh = dh + through_w_in(g * gw + gw - kw) + through_w_in(2 * e + g * gw + gw - kw)
        c1 = c1 * (1.0 / e)
        c2 = c2 * (1.0 / e)
        for g in range(GROUPS):
            gs = slice(g * gw, (g + 1) * gw)
            dv = rs * (dvn_scr[:, gs] * lng_ref[:, gs] - c1 - v_scr[:, gs] * c2)
            dp_ref[:, e + g * gw:e + (g + 1) * gw] = (dv * dgv_scr[:, gs]).astype(BF16)
            if g % span == span - 1:
                dh = dh + through_w_in(e + g * gw + gw - kw)
        _norm_modulate_bwd(dh, x1_ref, dx_ref, g_ref, sc_ref, dx1_ref, dsh_ref, p_scr)

        @pl.when(pl.program_id(0) == n_t - 1)
        def _():
            lane = lax.broadcasted_iota(jnp.int32, (CHUNK, 128), 1)
            acc = jnp.zeros((CHUNK, 128), F32)
            for g in range(GROUPS):
                acc = acc + jnp.where(lane == g, _rowsum(dbs_scr[:, g * gw:(g + 1) * gw]), 0.0)
            dbs_ref[...] = acc
            dsc_ref[...] = p_scr[...] * g_ref[...]
            dg_ref[...] = p_scr[...] * (1.0 + sc_ref[...])

    tok = pl.BlockSpec((t, d), lambda i: (i, 0))
    vec, evec, ws = _full((1, d)), _full((1, e)), _full((GROUPS, CHUNK, CHUNK))
    vshape = jax.ShapeDtypeStruct((1, d), F32)
    return pl.pallas_call(
        body, name=name, grid=(n_t,),
        out_shape=(jax.ShapeDtypeStruct((s, e3), BF16), jax.ShapeDtypeStruct((s, e), BF16), jax.ShapeDtypeStruct((s, d), F32),
                   jax.ShapeDtypeStruct((GROUPS, CHUNK, CHUNK), F32), jax.ShapeDtypeStruct((CHUNK, 128), F32),
                   jax.ShapeDtypeStruct((1, e), F32), jax.ShapeDtypeStruct((1, e), F32), vshape, vshape, vshape),
        in_specs=[pl.BlockSpec((t, e3), lambda i: (i, 0)), tok, tok, vec, vec, vec, evec, evec, ws, ws,
                  _resident((CHUNK, e)), _resident((e, d)), _resident((d, e3))],
        out_specs=(pl.BlockSpec((t, e3), lambda i: (i, 0)), pl.BlockSpec((t, e), lambda i: (i, 0)), tok,
                   ws, _full((CHUNK, 128)), evec, evec, vec, vec, vec),
        scratch_shapes=[pltpu.VMEM((t, e), F32), pltpu.VMEM((t, e), BF16), pltpu.VMEM((t, d), BF16),
                        pltpu.VMEM((t, e), F32), pltpu.VMEM((CHUNK, e), F32), pltpu.VMEM((1, d), F32),
                        pltpu.VMEM((t, e), F32), pltpu.VMEM((t, e), BF16), pltpu.VMEM((t, e), F32)],
        compiler_params=_params(("arbitrary",)),
    )(proj, dx2, x1, gate, g1, scale, lng, lnb, wt, wtt, bsf, wo, wi)


def _layer_a_bwd(proj, dx1, br, tails, x, cw, cb, gate, g0, scale, wo, wi, name, ex=None):
    s, e4 = proj.shape
    e = e4 // 4
    d = dx1.shape[1]
    t = min(s, 256)
    n_t = s // t
    cwid = min(e, 256)

    def body(p_ref, dx_ref, br_ref, tails_ref, x_ref, cw_ref, cb_ref, gate_ref, g_ref, sc_ref, wo_ref, wi_ref,
             dp_ref, y_ref, dx0_ref, dgate_ref, dcb_ref, dcw_ref, dsh_ref, dsc_ref, dg_ref, dy_scr, head_scr, p_scr):
        i = pl.program_id(0)

        @pl.when(i == 0)
        def _():
            dgate_ref[...] = jnp.zeros_like(dgate_ref)
            dcb_ref[...] = jnp.zeros_like(dcb_ref)
            dcw_ref[...] = jnp.zeros_like(dcw_ref)
            dsh_ref[...] = jnp.zeros_like(dsh_ref)
            head_scr[...] = jnp.zeros_like(head_scr)
            p_scr[...] = jnp.zeros_like(p_scr)
        dh = jnp.zeros((t, d), F32)
        dx = dx_ref[...]
        dgate_ref[...] += _colsum(dx * br_ref[...].astype(F32))
        dy_scr[...] = lax.dot_general((dx * gate_ref[...]).astype(BF16), wo_ref[...], NT_DIMS,
                                      preferred_element_type=F32)
        row = lax.broadcasted_iota(jnp.int32, (t, cwid), 0)
        has_prev = (i < n_t - 1).astype(F32)
        for c0 in range(0, e, cwid):
            sl = slice(c0, c0 + cwid)
            bg = p_ref[:, c0:c0 + cwid].astype(F32)
            cg = p_ref[:, e + c0:e + c0 + cwid].astype(F32)
            xin = p_ref[:, 2 * e + c0:2 * e + c0 + cwid].astype(F32)
            z = p_ref[:, 3 * e + c0:3 * e + c0 + cwid].astype(F32)
            cx = cg * xin
            p1, p2 = _conv_taps(cx, tails_ref[6:7, sl] * has_prev, tails_ref[7:8, sl] * has_prev, row)
            w0, w1, w2 = cw_ref[0:1, sl], cw_ref[1:2, sl], cw_ref[2:3, sl]
            conv = cb_ref[:, sl] + w2 * cx + w0 * p2 + w1 * p1
            sz, dsz = _silu(z)
            dy = dy_scr[:, sl]
            y_ref[:, sl] = (sz * bg * conv).astype(BF16)
            dp_ref[:, 3 * e + c0:3 * e + c0 + cwid] = (dy * bg * conv * dsz).astype(BF16)
            dp_ref[:, c0:c0 + cwid] = (dy * sz * conv).astype(BF16)
            dconv = dy * sz * bg
            dcb_ref[:, sl] += _colsum(dconv)
            dcw_ref[2:3, sl] += _colsum(dconv * cx)
            dcw_ref[1:2, sl] += _colsum(dconv * p1)
            dcw_ref[0:1, sl] += _colsum(dconv * p2)
            h0, h1 = head_scr[0:1, sl], head_scr[1:2, sl]
            n1 = jnp.where(row == t - 1, h0, pltpu.roll(dconv, t - 1, 0))
            n2 = jnp.where(row == t - 2, h0, jnp.where(row == t - 1, h1, pltpu.roll(dconv, t - 2, 0)))
            dcx = w2 * dconv + w1 * n1 + w0 * n2
            dp_ref[:, e + c0:e + c0 + cwid] = (dcx * xin).astype(BF16)
            dp_ref[:, 2 * e + c0:2 * e + c0 + cwid] = (dcx * cg).astype(BF16)
            head_scr[:, sl] = dconv[0:8, :]
            for seg in range(4):
                cols = slice(seg * e + c0, seg * e + c0 + cwid)
                dh = dh + lax.dot_general(dp_ref[:, cols], wi_ref[:, cols], NT_DIMS, preferred_element_type=F32)
        _norm_modulate_bwd(dh, x_ref, dx_ref, g_ref, sc_ref, dx0_ref, dsh_ref, p_scr)

        @pl.when(i == n_t - 1)
        def _():
            dsc_ref[...] = p_scr[...] * g_ref[...]
            dg_ref[...] = p_scr[...] * (1.0 + sc_ref[...])

    body = _carry(ex, body, 12, 9, lambda: pl.program_id(0) == 0, None, lambda: pl.program_id(0) == n_t - 1)
    ex_args, ex_in, ex_shapes, ex_out, ex_sems = _carried(ex)
    rev = lambda i: (n_t - 1 - i, 0)
    tok, vec = pl.BlockSpec((t, d), rev), _full((1, d))
    vshape = jax.ShapeDtypeStruct((1, d), F32)
    out = pl.pallas_call(
        body, name=name, grid=(n_t,),
        out_shape=(jax.ShapeDtypeStruct((s, e4), BF16), jax.ShapeDtypeStruct((s, e), BF16), jax.ShapeDtypeStruct((s, d), F32),
                   vshape, jax.ShapeDtypeStruct((1, e), F32), jax.ShapeDtypeStruct((8, e), F32), vshape, vshape, vshape,
                   *ex_shapes),
        in_specs=[pl.BlockSpec((t, e4), rev), tok, tok,
                  pl.BlockSpec((None, 8, e), lambda i: (jnp.maximum(n_t - 2 - i, 0), 0, 0)), tok,
                  _full((3, e)), _full((1, e)), vec, vec, vec, _resident((e, d)), _resident((d, e4)), *ex_in],
        out_specs=(pl.BlockSpec((t, e4), rev), pl.BlockSpec((t, e), rev), tok, vec, _full((1, e)), _full((8, e)),
                   vec, vec, vec, *ex_out),
        scratch_shapes=[pltpu.VMEM((t, e), F32), pltpu.VMEM((8, e), F32), pltpu.VMEM((1, d), F32), *ex_sems],
        compiler_params=_params(("arbitrary",)),
    )(proj, dx1, br, tails, x, cw, cb, gate, g0, scale, wo, wi, *ex_args)
    return (*out[:9], out[9:])


def _matmul_tn(a, b, colscale, rows_split, name, ex=None, a_cols=None):
    s, m = a.shape
    a_blk = 0
    if a_cols is not None:
        a_blk, m = a_cols
    n = b.shape[1]
    n_j, tn = (1, n) if rows_split else (NDEV, n // NDEV)
    fixed = (4 + 4 + 2 * 2) * m * tn
    tk = s
    while fixed + 2 * tk * (2 * m + b.dtype.itemsize * tn) > TN_VMEM_BUDGET:
        tk //= 2
    n_k = s // tk

    def body(a_ref, b_ref, cs_ref, o_ref, acc):
        k = pl.program_id(1)
        part = lax.dot_general(a_ref[...], b_ref[...].astype(BF16), TN_DIMS, preferred_element_type=F32)
        if n_k == 1:
            o_ref[...] = (part * cs_ref[...]).astype(BF16)
            return

        @pl.when(k == 0)
        def _():
            acc[...] = part

        @pl.when((k > 0) & (k < n_k - 1))
        def _():
            acc[...] += part

        @pl.when(k == n_k - 1)
        def _():
            o_ref[...] = ((acc[...] + part) * cs_ref[...]).astype(BF16)

    at = lambda j, k: (pl.program_id(0) == j) & (pl.program_id(1) == k)
    body = _carry(ex, body, 3, 1, lambda: at(0, 0), None, lambda: at(n_j - 1, n_k - 1))
    ex_args, ex_in, ex_shapes, ex_out, ex_sems = _carried(ex)
    out = pl.pallas_call(
        body, name=name, grid=(n_j, n_k),
        out_shape=(jax.ShapeDtypeStruct((n_j, m, tn), BF16), *ex_shapes),
        in_specs=[pl.BlockSpec((tk, m), lambda j, k: (k, a_blk)), pl.BlockSpec((tk, tn), lambda j, k: (k, j)),
                  pl.BlockSpec((1, tn), lambda j, k: (0, j)), *ex_in],
        out_specs=(pl.BlockSpec((None, m, tn), lambda j, k: (j, 0, 0)), *ex_out),
        scratch_shapes=[pltpu.VMEM((m, tn), F32), *ex_sems],
        compiler_params=_params(("arbitrary", "arbitrary")),
    )(a, b, colscale, *ex_args)
    return (out[0].reshape(NDEV, m // NDEV, n) if rows_split else out[0]), out[1:]


def _adam_update(w, g, m, v):
    m = ADAM_B1 * m + (1.0 - ADAM_B1) * g
    v = ADAM_B2 * v + (1.0 - ADAM_B2) * (g * g)
    m_hat = m / (1.0 - ADAM_B1 ** ADAM_STEP)
    v_hat = v / (1.0 - ADAM_B2 ** ADAM_STEP)
    return -ADAM_LR * (m_hat / (jnp.sqrt(v_hat) + ADAM_EPS) + ADAM_WD * w), m, v


def _adamw_reduce(parts, w, m, v, name, ex=None):
    n_p, r, c = parts.shape
    tr = min(r, 128 if ex is not None else 256)
    n_i = r // tr

    def body(p_ref, w_ref, m_ref, v_ref, g_out, d_out, m_out, v_out):
        g = p_ref[0].astype(F32)
        for j in range(1, n_p):
            g = g + p_ref[j].astype(F32)
        g_out[...] = g
        d_out[...], m_out[...], v_out[...] = _adam_update(w_ref[...], g, m_ref[...], v_ref[...])

    step = lambda k: (lambda: pl.program_id(0) == k)
    body = _carry(ex, body, 4, 4, step(0), step(n_i // 2), step(n_i - 1))
    ex_args, ex_in, ex_shapes, ex_out, ex_sems = _carried(ex)
    blk = pl.BlockSpec((tr, c), lambda i: (i, 0))
    shp = jax.ShapeDtypeStruct((r, c), F32)
    out = pl.pallas_call(
        body, name=name, grid=(n_i,), out_shape=(shp, shp, shp, shp, *ex_shapes),
        in_specs=[pl.BlockSpec((n_p, tr, c), lambda i: (0, i, 0)), blk, blk, blk, *ex_in],
        out_specs=(blk, blk, blk, blk, *ex_out), scratch_shapes=ex_sems,
        compiler_params=_params(("arbitrary",)),
    )(parts, w, m, v, *ex_args)
    return (*out[:4], out[4:])


def _adamw_small(gs, ws, ms, vs, name):
    n = len(gs)

    def body(*refs):
        ins, outs = refs[:4 * n], refs[4 * n:]
        for a in range(n):
            d, m, v = _adam_update(ins[n + a][...], ins[a][...], ins[2 * n + a][...], ins[3 * n + a][...])
            outs[a][...], outs[n + a][...], outs[2 * n + a][...] = d, m, v

    shapes = tuple(jax.ShapeDtypeStruct(w.shape, F32) for w in ws) * 3
    out = pl.pallas_call(
        body, name=name, out_shape=shapes,
        in_specs=[_vmem()] * (4 * n), out_specs=tuple([_vmem()] * (3 * n)),
        compiler_params=pltpu.CompilerParams(vmem_limit_bytes=VMEM_LIMIT),
    )(*gs, *ws, *ms, *vs)
    return out[:n], out[n:2 * n], out[2 * n:]


def _sum_devices(packed, name):
    _, r, wdt = packed.shape

    def body(p_ref, o_ref):
        acc = p_ref[0]
        for j in range(1, NDEV):
            acc = acc + p_ref[j]
        o_ref[...] = acc

    return pl.pallas_call(
        body, name=name, out_shape=jax.ShapeDtypeStruct((r, wdt), F32),
        in_specs=[_vmem()], out_specs=_vmem(),
        compiler_params=pltpu.CompilerParams(vmem_limit_bytes=VMEM_LIMIT),
    )(packed)


def _mod_w_grad(c_t, dmod, name):
    n_layers, _, w3 = dmod.shape
    d = c_t.shape[0]

    def body(c_ref, dm_ref, o_ref):
        for i in range(n_layers):
            acc = c_ref[:, 0:1] * dm_ref[i, 0:1, :]
            for b in range(1, NDEV):
                acc = acc + c_ref[:, b:b + 1] * dm_ref[i, b:b + 1, :]
            o_ref[i] = acc

    return pl.pallas_call(
        body, name=name, out_shape=jax.ShapeDtypeStruct((n_layers, d, w3), F32),
        in_specs=[_vmem(), _vmem()], out_specs=_vmem(),
        compiler_params=pltpu.CompilerParams(vmem_limit_bytes=VMEM_LIMIT),
    )(c_t, dmod)


def _mask_transpose_ws(w_s, name):
    def body(w_ref, wt_ref, wtt_ref):
        tril = (lax.broadcasted_iota(jnp.int32, (CHUNK, CHUNK), 0) >= lax.broadcasted_iota(jnp.int32, (CHUNK, CHUNK), 1))
        for g in range(GROUPS):
            wm = jnp.where(tril, w_ref[g], 0.0)
            wt_ref[g] = wm.astype(BF16)
            wtt_ref[g] = wm.T.astype(BF16)

    shp = jax.ShapeDtypeStruct(w_s.shape, BF16)
    return pl.pallas_call(
        body, name=name, out_shape=(shp, shp), in_specs=[_vmem()], out_specs=(_vmem(), _vmem()),
    )(w_s)


def _pack(pieces):
    flat = jnp.concatenate([p.reshape(-1) for p in pieces])
    rows = -(-flat.shape[0] // (8 * PACK_W)) * 8
    return jnp.pad(flat, (0, rows * PACK_W - flat.shape[0])).reshape(rows, PACK_W)


def _unpack(flat, shapes):
    out, off = [], 0
    for shp in shapes:
        size = 1
        for dim in shp:
            size *= dim
        out.append(flat[off:off + size].reshape(shp))
        off += size
    return out


def kernel(x, c, mod_w, mod_b, norm_g, a_w_in, a_conv_w, a_conv_b, a_w_out, b_w_in, b_ln_g, b_ln_b, b_w_s, b_b_s, b_w_out, final_g, loss_target, m_mod_w, m_mod_b, m_norm_g, m_a_w_in, m_a_conv_w, m_a_conv_b, m_a_w_out, m_b_w_in, m_b_ln_g, m_b_ln_b, m_b_w_s, m_b_b_s, m_b_w_out, m_final_g, v_mod_w, v_mod_b, v_norm_g, v_a_w_in, v_a_conv_w, v_a_conv_b, v_a_w_out, v_b_w_in, v_b_ln_g, v_b_ln_b, v_b_w_s, v_b_b_s, v_b_w_out, v_final_g):
    s, d = x.shape[1], x.shape[2]
    es = a_w_out.shape[1]
    e = NDEV * es
    w3 = mod_w.shape[2]
    me = _index(_pos())
    x0 = x.reshape(s, d)
    tgt = loss_target.reshape(s, d)

    small = jnp.concatenate([a_conv_w[0], b_ln_g, b_ln_b, jnp.zeros((3, es), F32)], axis=0)
    gather_a = _gather_exchange([a_w_in[0].astype(BF16), a_w_out[0].astype(BF16), small], [True, False, True])
    gather_b = _gather_exchange([b_w_in[0].astype(BF16), b_w_out[0].astype(BF16)], [True, False])
    mod, c_all, (wa, woa, small_all) = _mod_vectors(c, mod_w, mod_b, gather_a)
    conv_w, ln_g, ln_b = small_all[0:3], small_all[3:4], small_all[4:5]
    bsf = jnp.repeat(b_b_s[0].T, e // GROUPS, axis=1)
    wt, wtt = _mask_transpose_ws(b_w_s[0], "mask_w_s")
    shift0, scale0, gate0 = mod[0:1, 0:d], mod[0:1, d:2 * d], mod[0:1, 2 * d:]
    shift1, scale1, gate1 = mod[1:2, 0:d], mod[1:2, d:2 * d], mod[1:2, 2 * d:]
    g0, g1, fg = norm_g[0:1], norm_g[1:2], final_g.reshape(1, d)

    proj_a, h0, x1, br_a, tails, (wb, wob) = _layer_a_fwd(
        x0, g0, scale0, shift0, gate0, wa, conv_w, a_conv_b, woa, "a_fwd", gather_b)
    proj_b, h1, dx2, loss_acc, dfg, dgate1, _ = _layer_b_fwd_loss(
        x1, tgt, g1, scale1, shift1, gate1, fg, wb, ln_g, ln_b, wt, bsf, wob, "b_fwd_loss")

    dproj_b, y_b, dx1, dws, dbs, dlg, dlb, dshift1, dscale1, dg1 = _layer_b_bwd(
        proj_b, dx2, x1, gate1, g1, scale1, ln_g, ln_b, wt, wtt, bsf, wob, wb, "b_bwd")
    gs_b_out, _ = _matmul_tn(y_b, dx2, gate1, True, "b_w_out_grad")
    gs_b_in, _ = _matmul_tn(h1, dproj_b, jnp.ones((1, dproj_b.shape[1]), F32), False, "b_w_in_grad")
    dproj_a, y_a, dx0, dgate0, dcb, dcw, dshift0, dscale0, dg0, (gr_b_in, gr_b_out) = _layer_a_bwd(
        proj_a, dx1, br_a, tails, x0, conv_w, a_conv_b, gate0, g0, scale0, woa, wa, "a_bwd",
        _scatter_exchange([gs_b_in, gs_b_out]))
    gs_a_out, _ = _matmul_tn(y_a, dx1, gate0, True, "a_w_out_grad")
    n_q = 4
    ones_n = jnp.ones((1, dproj_a.shape[1]), F32)
    waiting, gr_a_in_q = _scatter_exchange([gs_a_out]), []
    for q in range(n_q):
        gs_q, arrived = _matmul_tn(h0, dproj_a, ones_n, False, f"a_w_in_grad_{q}", waiting, a_cols=(q, d // n_q))
        if q == 0:
            gr_a_out, = arrived
        else:
            gr_a_in_q.append(arrived[0])
        waiting = _scatter_exchange([gs_q])

    def big(parts, w, m, v, name, ex=None):
        shp = w.shape
        r2 = lambda t_: t_.reshape(-1, shp[-1])
        g, dl, nm, nv, ex_out = _adamw_reduce(parts, r2(w), r2(m), r2(v), name, ex)
        return tuple(t_.reshape(shp) for t_ in (g, dl, nm, nv)), ex_out

    res = {}
    res["b_w_in"], arrived = big(gr_b_in, b_w_in, m_b_w_in, v_b_w_in, "adamw_b_w_in", waiting)
    gr_a_in = jnp.concatenate(gr_a_in_q + [arrived[0]], axis=1)

    pieces = [dshift0, dscale0, dgate0, dshift1, dscale1, dgate1, dg0, dg1, dcb, dcw[0:3], dlg, dlb, dfg,
              dbs[:, 0:GROUPS].T, dws, loss_acc[0:1, 0:1]]
    shapes = [p.shape for p in pieces]
    res["a_w_in"], (packed_all,) = big(gr_a_in, a_w_in, m_a_w_in, v_a_w_in, "adamw_a_w_in",
                                       _gather_exchange([_pack(pieces)], [False]))
    packed_all = packed_all.reshape(NDEV, -1, PACK_W)
    total = _sum_devices(packed_all, "sum_small_grads").reshape(-1)
    (t_sh0, t_sc0, t_ga0, t_sh1, t_sc1, t_ga1, t_g0, t_g1, t_cb, t_cw, t_lg, t_lb, t_fg, t_bs, t_ws, t_loss) = _unpack(
        total, shapes)
    loss = t_loss.reshape(())
    grad_mod_b = jnp.concatenate([jnp.concatenate([t_sh0, t_sc0, t_ga0], axis=1),
                                  jnp.concatenate([t_sh1, t_sc1, t_ga1], axis=1)], axis=0)
    grad_norm_g = jnp.concatenate([t_g0, t_g1], axis=0)
    dmod_all = packed_all.reshape(NDEV, -1)[:, 0:6 * d].reshape(NDEV, 2, 3 * d).transpose(1, 0, 2)
    dmod_mine = lax.dynamic_slice_in_dim(dmod_all, me * w3, w3, axis=2)
    grad_mod_w = _mod_w_grad(c_all.T, dmod_mine, "mod_w_grad")
    grad_a_conv_w = lax.dynamic_slice_in_dim(t_cw, me * es, es, axis=1)
    grad_b_ln_g = lax.dynamic_slice_in_dim(t_lg, me * es, es, axis=1)
    grad_b_ln_b = lax.dynamic_slice_in_dim(t_lb, me * es, es, axis=1)

    res["mod_w"], _ = big(grad_mod_w.reshape(1, -1, w3), mod_w, m_mod_w, v_mod_w, "adamw_mod_w")
    res["a_w_out"], _ = big(gr_a_out, a_w_out, m_a_w_out, v_a_w_out, "adamw_a_w_out")
    res["b_w_out"], _ = big(gr_b_out, b_w_out, m_b_w_out, v_b_w_out, "adamw_b_w_out")

    small_names = ["mod_b", "norm_g", "a_conv_w", "a_conv_b", "b_ln_g", "b_ln_b", "b_w_s", "b_b_s", "final_g"]
    small_g = [grad_mod_b, grad_norm_g, grad_a_conv_w, t_cb, grad_b_ln_g, grad_b_ln_b, t_ws, t_bs, t_fg]
    small_w = [mod_b, norm_g, a_conv_w, a_conv_b, b_ln_g, b_ln_b, b_w_s, b_b_s, final_g]
    small_m = [m_mod_b, m_norm_g, m_a_conv_w, m_a_conv_b, m_b_ln_g, m_b_ln_b, m_b_w_s, m_b_b_s, m_final_g]
    small_v = [v_mod_b, v_norm_g, v_a_conv_w, v_a_conv_b, v_b_ln_g, v_b_ln_b, v_b_w_s, v_b_b_s, v_final_g]
    as2d = lambda t_: t_.reshape(-1, t_.shape[-1])
    dls, nms, nvs = _adamw_small([as2d(t_) for t_ in small_g], [as2d(t_) for t_ in small_w],
                                 [as2d(t_) for t_ in small_m], [as2d(t_) for t_ in small_v], "adamw_small")
    for a, nme in enumerate(small_names):
        shp = small_w[a].shape
        res[nme] = (small_g[a].reshape(shp), dls[a].reshape(shp), nms[a].reshape(shp), nvs[a].reshape(shp))

    order = ["mod_w", "mod_b", "norm_g", "a_w_in", "a_conv_w", "a_conv_b", "a_w_out", "b_w_in", "b_ln_g", "b_ln_b",
             "b_w_s", "b_b_s", "b_w_out", "final_g"]
    return (loss, dx0.reshape(x.shape), *[res[k][0] for k in order], *[res[k][1] for k in order],
            *[res[k][2] for k in order], *[res[k][3] for k in order])
```

```python
import jax
import jax.numpy as jnp
from jax import lax
from jax.experimental import pallas as pl
from jax.experimental.pallas import tpu as pltpu

NDEV = 8
CHUNK = 128
GROUPS = 8
RMS_EPS = 1e-6
LN_EPS = 1e-5
ADAM_LR, ADAM_B1, ADAM_B2, ADAM_EPS, ADAM_WD, ADAM_STEP = 0.001, 0.9, 0.999, 1e-08, 0.01, 10
V7X_VMEM_BYTES = 64 * 1024 * 1024
VMEM_LIMIT = V7X_VMEM_BYTES - 8 * 1024 * 1024
TN_VMEM_BUDGET = 46 * 1024 * 1024
PACK_W = 1024
F32, BF16 = jnp.float32, jnp.bfloat16
MESH = pl.DeviceIdType.MESH
RSQRT2 = 0.7071067811865476
INV_SQRT_2PI = 0.3989422804014327
NT_DIMS = (((1,), (1,)), ((), ()))
TN_DIMS = (((0,), (0,)), ((), ()))


def _params(sem=None):
    return pltpu.CompilerParams(dimension_semantics=sem, vmem_limit_bytes=VMEM_LIMIT)


def _vmem():
    return pl.BlockSpec(memory_space=pltpu.VMEM)


def _hbm():
    return pl.BlockSpec(memory_space=pltpu.HBM)


def _full(shape):
    return pl.BlockSpec(shape, lambda *_: (0,) * len(shape))


def _pos():
    return lax.axis_index("x"), lax.axis_index("y"), lax.axis_index("c")


def _index(p):
    return 4 * p[0] + 2 * p[1] + p[2]


def _peer(k):
    x, y, c = _pos()
    return ((1 - x) if (k >> 2) & 1 else x, (1 - y) if (k >> 1) & 1 else y, (1 - c) if k & 1 else c)


def _silu(z):
    sg = jax.nn.sigmoid(z)
    return z * sg, sg * (1.0 + z * (1.0 - sg))


def _gelu(v):
    phi = 0.5 * (1.0 + lax.erf(v * RSQRT2))
    return v * phi, phi + v * (jnp.exp(-0.5 * v * v) * INV_SQRT_2PI)


def _colsum(v):
    return jnp.sum(v, axis=0, keepdims=True)


def _rowsum(v):
    return jnp.sum(v, axis=-1, keepdims=True)


def _gather_all_vmem(slab_ref, send_sems, recv_sems, base):
    me = _index(_pos())
    sends = []
    for k in range(1, NDEV):
        cp = pltpu.make_async_remote_copy(
            src_ref=slab_ref.at[me], dst_ref=slab_ref.at[me],
            send_sem=send_sems.at[base + k - 1], recv_sem=recv_sems.at[base + k - 1],
            device_id=_peer(k), device_id_type=MESH)
        cp.start()
        sends.append(cp)
    for k in range(1, NDEV):
        src = _index(_peer(k))
        pltpu.make_async_remote_copy(
            src_ref=slab_ref.at[src], dst_ref=slab_ref.at[src],
            send_sem=send_sems.at[base + k - 1], recv_sem=recv_sems.at[base + k - 1],
            device_id=_peer(k), device_id_type=MESH).wait_recv()
    for cp in sends:
        cp.wait_send()


def _mod_vectors(c, mod_w, mod_b, ex):
    n_layers, d, w3 = mod_w.shape
    r_in, r_out = len(ex.arrays), len(ex.out_shapes)

    def body(*refs):
        c_ref, mw_ref, mb_ref = refs[:3]
        ex_ins = refs[3:3 + r_in]
        mod_ref, call_ref = refs[3 + r_in:5 + r_in]
        ex_outs = refs[5 + r_in:5 + r_in + r_out]
        cslab, pslab, send_sems, recv_sems = refs[5 + r_in + r_out:9 + r_in + r_out]
        ex_sems = refs[9 + r_in + r_out:]
        ex.start(ex_ins, ex_outs, ex_sems)
        me = _index(_pos())
        cv = c_ref[...]
        cslab[me] = jnp.broadcast_to(cv * jax.nn.sigmoid(cv), (8, d))
        _gather_all_vmem(cslab, send_sems, recv_sems, 0)
        c_all = jnp.concatenate([cslab[k, 0:1, :] for k in range(NDEV)], axis=0)
        call_ref[...] = c_all
        for i in range(n_layers):
            pslab[me, i * NDEV:(i + 1) * NDEV, :] = jnp.dot(
                c_all, mw_ref[i], preferred_element_type=F32, precision=lax.Precision.HIGHEST)
        _gather_all_vmem(pslab, send_sems, recv_sems, NDEV - 1)
        for i in range(n_layers):
            for k in range(NDEV):
                mod_ref[i:i + 1, k * w3:(k + 1) * w3] = (
                    pslab[k, pl.ds(i * NDEV + me, 1), :] + mb_ref[i:i + 1, k * w3:(k + 1) * w3])
        if ex.middle is not None:
            ex.middle(ex_ins, ex_outs, ex_sems)
        ex.finish(ex_ins, ex_outs, ex_sems)

    out = pl.pallas_call(
        body, name="mod_vectors",
        out_shape=(jax.ShapeDtypeStruct((n_layers, 3 * d), F32), jax.ShapeDtypeStruct((NDEV, d), F32), *ex.out_shapes),
        in_specs=[_vmem(), _vmem(), _vmem()] + [_hbm()] * r_in, out_specs=(_vmem(), _vmem(), *([_hbm()] * r_out)),
        scratch_shapes=[pltpu.VMEM((NDEV, 8, d), F32), pltpu.VMEM((NDEV, n_layers * NDEV, w3), F32),
                        pltpu.SemaphoreType.DMA((2 * (NDEV - 1),)), pltpu.SemaphoreType.DMA((2 * (NDEV - 1),)), *ex.sems],
        compiler_params=pltpu.CompilerParams(vmem_limit_bytes=VMEM_LIMIT),
    )(c, mod_w, mod_b, *ex.arrays)
    return out[0], out[1], out[2:]


class _Exchange:
    def __init__(self, arrays, out_shapes, sems, start, middle, finish):
        self.arrays, self.out_shapes, self.sems = list(arrays), list(out_shapes), list(sems)
        self.start, self.middle, self.finish = start, middle, finish


def _gather_exchange(shards, by_cols):
    n = len(shards)
    shapes = [sh.shape for sh in shards]

    def tools(ins, outs, sems):
        send_sems, recv_sems, local_sems = sems
        x, y, c = _pos()
        chips = [(1 - x, y), (x, 1 - y), (1 - x, 1 - y)]
        south = c == 0
        relayed = (jnp.where(south, 1 - x, x), jnp.where(south, y, 1 - y), c)
        relay_to = (jnp.where(south, x, 1 - x), jnp.where(south, 1 - y, y), c)

        def place(a, block):
            r, cc = shapes[a]
            if by_cols[a]:
                return outs[a].at[:, pl.ds(_index(block) * cc, cc)]
            return outs[a].at[pl.ds(_index(block) * r, r), :]

        def copy(a, k, block, to, src=None):
            dst = place(a, block)
            return pltpu.make_async_remote_copy(
                src_ref=dst if src is None else src, dst_ref=dst,
                send_sem=send_sems.at[a * 7 + k], recv_sem=recv_sems.at[a * 7 + k],
                device_id=to, device_id_type=MESH)

        mine = [pltpu.make_async_copy(ins[a], place(a, (x, y, c)), local_sems.at[a]) for a in range(n)]
        first = []
        for a in range(n):
            first.append(copy(a, 0, (x, y, c), (x, y, 1 - c), src=ins[a]))
            first += [copy(a, 1 + j, (x, y, c), (*chip, c), src=ins[a]) for j, chip in enumerate(chips[:2])]
        relays = [copy(a, 3, relayed, relay_to) for a in range(n)]
        passed = [copy(a, 4 + j, (*chip, c), (x, y, 1 - c)) for j, chip in enumerate(chips) for a in range(n)]
        return (x, y, c), chips, copy, mine, first, relays, passed

    def start(ins, outs, sems):
        _, _, _, mine, first, _, _ = tools(ins, outs, sems)
        for cp in mine + first:
            cp.start()

    def middle(ins, outs, sems):
        (x, y, c), chips, copy, _, _, relays, passed = tools(ins, outs, sems)
        for j, chip in enumerate(chips):
            for a in range(n):
                copy(a, 1 + j, (*chip, c), (x, y, c)).wait_recv()
                passed[j * n + a].start()
            if j == 1:
                for cp in relays:
                    cp.start()

    def finish(ins, outs, sems):
        (x, y, c), chips, copy, mine, first, relays, passed = tools(ins, outs, sems)
        for a in range(n):
            copy(a, 0, (x, y, 1 - c), (x, y, c)).wait_recv()
        for j, chip in enumerate(chips):
            for a in range(n):
                copy(a, 4 + j, (*chip, 1 - c), (x, y, c)).wait_recv()
        for cp in first + relays + passed:
            cp.wait_send()
        for cp in mine:
            cp.wait()

    out_shapes = [jax.ShapeDtypeStruct((r, NDEV * cc) if bc else (NDEV * r, cc), sh.dtype)
                  for (r, cc), bc, sh in zip(shapes, by_cols, shards)]
    sems = [pltpu.SemaphoreType.DMA((7 * n,)), pltpu.SemaphoreType.DMA((7 * n,)), pltpu.SemaphoreType.DMA((n,))]
    return _Exchange(shards, out_shapes, sems, start, middle, finish)


def _scatter_exchange(parts):
    n = len(parts)

    def tools(ins, outs, sems):
        send_sems, recv_sems, local_sems = sems
        me = _index(_pos())
        mine = [pltpu.make_async_copy(ins[a].at[me], outs[a].at[me], local_sems.at[a]) for a in range(n)]
        sends, arrivals = [], []
        for k in range(1, NDEV):
            peer = _peer(k)
            for a in range(n):
                pair = dict(send_sem=send_sems.at[a * 7 + k - 1], recv_sem=recv_sems.at[a * 7 + k - 1],
                            device_id=peer, device_id_type=MESH)
                sends.append(pltpu.make_async_remote_copy(src_ref=ins[a].at[_index(peer)], dst_ref=outs[a].at[me], **pair))
                slot = outs[a].at[_index(peer)]
                arrivals.append(pltpu.make_async_remote_copy(src_ref=slot, dst_ref=slot, **pair))
        return mine, sends, arrivals

    def start(ins, outs, sems):
        mine, sends, _ = tools(ins, outs, sems)
        for cp in mine + sends:
            cp.start()

    def finish(ins, outs, sems):
        mine, sends, arrivals = tools(ins, outs, sems)
        for cp in arrivals:
            cp.wait_recv()
        for cp in sends:
            cp.wait_send()
        for cp in mine:
            cp.wait()

    out_shapes = [jax.ShapeDtypeStruct(p.shape, p.dtype) for p in parts]
    sems = [pltpu.SemaphoreType.DMA((7 * n,)), pltpu.SemaphoreType.DMA((7 * n,)), pltpu.SemaphoreType.DMA((n,))]
    return _Exchange(parts, out_shapes, sems, start, None, finish)


def _carry(ex, body, n_in, n_out, first, middle, last):
    if ex is None:
        return body
    r_in, r_out = len(ex.arrays), len(ex.out_shapes)

    def wrapped(*refs):
        ins, rins = refs[:n_in], refs[n_in:n_in + r_in]
        outs = refs[n_in + r_in:n_in + r_in + n_out]
        routs = refs[n_in + r_in + n_out:n_in + r_in + n_out + r_out]
        rest = refs[n_in + r_in + n_out + r_out:]
        scratch, sems = rest[:len(rest) - len(ex.sems)], rest[len(rest) - len(ex.sems):]

        @pl.when(first())
        def _():
            ex.start(rins, routs, sems)

        if ex.middle is not None:
            @pl.when(middle())
            def _():
                ex.middle(rins, routs, sems)

        body(*ins, *outs, *scratch)

        @pl.when(last())
        def _():
            ex.finish(rins, routs, sems)

    return wrapped


def _carried(ex):
    if ex is None:
        return [], [], [], [], []
    return ex.arrays, [_hbm()] * len(ex.arrays), ex.out_shapes, [_hbm()] * len(ex.out_shapes), ex.sems


def _resident(shape):
    return pl.BlockSpec(shape, lambda *_: (0,) * len(shape), pipeline_mode=pl.Buffered(1))


def _norm_modulate(x_ref, g_ref, sc_ref, sh_ref):
    xv = x_ref[...]
    r = lax.rsqrt(jnp.mean(xv * xv, axis=-1, keepdims=True) + RMS_EPS)
    return ((xv * r) * g_ref[...] * (1.0 + sc_ref[...]) + sh_ref[...]).astype(BF16)


def _conv_taps(cx, t6, t7, row):
    p1 = jnp.where(row == 0, t7, pltpu.roll(cx, 1, 0))
    p2 = jnp.where(row == 0, t6, jnp.where(row == 1, t7, pltpu.roll(cx, 2, 0)))
    return p1, p2


def _layer_a_fwd(x, g, scale, shift, gate, wi, cw, cb, wo, name, ex=None):
    s, d = x.shape
    e = wo.shape[0]
    t = min(s, 256)
    n_t = s // t
    cwid = min(e, 512)

    def body(x_ref, g_ref, sc_ref, sh_ref, gate_ref, wi_ref, cw_ref, cb_ref, wo_ref,
             proj_ref, h_ref, x1_ref, br_ref, tails_ref, y_scr, tail_scr):
        @pl.when(pl.program_id(0) == 0)
        def _():
            tail_scr[...] = jnp.zeros_like(tail_scr)
        h_ref[...] = _norm_modulate(x_ref, g_ref, sc_ref, sh_ref)
        row = lax.broadcasted_iota(jnp.int32, (t, cwid), 0)

        def project(c0):
            v = jnp.dot(h_ref[...], wi_ref[:, c0:c0 + cwid], preferred_element_type=F32)
            proj_ref[:, c0:c0 + cwid] = v.astype(BF16)
            return v

        for c0 in range(0, e, cwid):
            sl = slice(c0, c0 + cwid)
            bg, z = project(c0), project(3 * e + c0)
            cx = project(e + c0) * project(2 * e + c0)
            p1, p2 = _conv_taps(cx, tail_scr[6:7, sl], tail_scr[7:8, sl], row)
            conv = cb_ref[:, sl] + cw_ref[2:3, sl] * cx + cw_ref[0:1, sl] * p2 + cw_ref[1:2, sl] * p1
            y_scr[:, sl] = (_silu(z)[0] * bg * conv).astype(BF16)
            tail_scr[:, sl] = cx[t - 8:t, :]
        tails_ref[...] = tail_scr[...]
        br = jnp.dot(y_scr[...], wo_ref[...], preferred_element_type=F32)
        x1_ref[...] = x_ref[...] + gate_ref[...] * br
        br_ref[...] = br.astype(BF16)

    step = lambda k: (lambda: pl.program_id(0) == k)
    body = _carry(ex, body, 9, 5, step(0), step(n_t // 2), step(n_t - 1))
    ex_args, ex_in, ex_shapes, ex_out, ex_sems = _carried(ex)
    tok = pl.BlockSpec((t, d), lambda i: (i, 0))
    out = pl.pallas_call(
        body, name=name, grid=(n_t,),
        out_shape=(jax.ShapeDtypeStruct((s, 4 * e), BF16), jax.ShapeDtypeStruct((s, d), BF16),
                   jax.ShapeDtypeStruct((s, d), F32), jax.ShapeDtypeStruct((s, d), BF16),
                   jax.ShapeDtypeStruct((n_t, 8, e), F32), *ex_shapes),
        in_specs=[tok, _full((1, d)), _full((1, d)), _full((1, d)), _full((1, d)), _resident((d, 4 * e)),
                  _full((3, e)), _full((1, e)), _resident((e, d)), *ex_in],
        out_specs=(pl.BlockSpec((t, 4 * e), lambda i: (i, 0)), tok, tok, tok,
                   pl.BlockSpec((None, 8, e), lambda i: (i, 0, 0)), *ex_out),
        scratch_shapes=[pltpu.VMEM((t, e), BF16), pltpu.VMEM((8, e), F32), *ex_sems],
        compiler_params=_params(("arbitrary",)),
    )(x, g, scale, shift, gate, wi, cw, cb, wo, *ex_args)
    return (*out[:5], out[5:])


def _ln_stats(v_pre, v_scr, dgv_scr, t, e):
    gw = e // GROUPS
    s1 = jnp.zeros((t, 1), F32)
    for g in range(GROUPS):
        v, dgv = _gelu(v_pre(g))
        v_scr[:, g * gw:(g + 1) * gw] = v
        if dgv_scr is not None:
            dgv_scr[:, g * gw:(g + 1) * gw] = dgv.astype(dgv_scr.dtype)
        s1 = s1 + _rowsum(v)
    mu = s1 * (1.0 / e)
    s2 = jnp.zeros((t, 1), F32)
    for g in range(GROUPS):
        dv = v_scr[:, g * gw:(g + 1) * gw] - mu
        s2 = s2 + _rowsum(dv * dv)
    return mu, lax.rsqrt(s2 * (1.0 / e) + LN_EPS)


def _layer_b_fwd_loss(x1, tgt, g1, scale, shift, gate, fg, wi, lng, lnb, wt, bsf, wo, name, ex=None):
    s, d = x1.shape
    e = wo.shape[0]
    gw = e // GROUPS
    t = min(s, 256)
    n_t = s // t

    def body(x1_ref, tgt_ref, g_ref, sc_ref, sh_ref, gate_ref, fg_ref, wi_ref, lng_ref, lnb_ref, wt_ref, bsf_ref, wo_ref,
             proj_ref, h_ref, dx2_ref, loss_ref, dfg_ref, dgate_ref, v_scr, y_scr):
        @pl.when(pl.program_id(0) == 0)
        def _():
            loss_ref[...] = jnp.zeros_like(loss_ref)
            dfg_ref[...] = jnp.zeros_like(dfg_ref)
            dgate_ref[...] = jnp.zeros_like(dgate_ref)
        h_ref[...] = _norm_modulate(x1_ref, g_ref, sc_ref, sh_ref)

        def project(c0):
            v = jnp.dot(h_ref[...], wi_ref[:, c0:c0 + gw], preferred_element_type=F32)
            proj_ref[:, c0:c0 + gw] = v.astype(BF16)
            return v

        mu, rs = _ln_stats(lambda g: project(e + g * gw), v_scr, None, t, e)
        for g in range(GROUPS):
            gs = slice(g * gw, (g + 1) * gw)
            vn = (((v_scr[:, gs] - mu) * rs) * lng_ref[:, gs] + lnb_ref[:, gs]).astype(BF16)
            u = _gelu(project(g * gw))[0]
            sz = _silu(project(2 * e + g * gw))[0]
            for ch in range(t // CHUNK):
                rows = slice(ch * CHUNK, (ch + 1) * CHUNK)
                mixed = jnp.dot(wt_ref[g], vn[rows], preferred_element_type=F32) + bsf_ref[:, gs]
                y_scr[rows, gs] = (sz[rows] * (u[rows] * mixed)).astype(BF16)
        br = jnp.dot(y_scr[...], wo_ref[...], preferred_element_type=F32)
        x2 = x1_ref[...] + gate_ref[...] * br
        r2 = lax.rsqrt(jnp.mean(x2 * x2, axis=-1, keepdims=True) + RMS_EPS)
        xn = x2 * r2
        diff = xn * fg_ref[...] - tgt_ref[...]
        loss_ref[...] += jnp.broadcast_to(0.5 * _colsum(jnp.mean(diff * diff, axis=-1, keepdims=True)), loss_ref.shape)
        dout = diff * (1.0 / d)
        dfg_ref[...] += _colsum(dout * xn)
        dxn = dout * fg_ref[...]
        dx2 = r2 * (dxn - xn * jnp.mean(dxn * xn, axis=-1, keepdims=True))
        dx2_ref[...] = dx2
        dgate_ref[...] += _colsum(dx2 * br)

    step = lambda k: (lambda: pl.program_id(0) == k)
    body = _carry(ex, body, 13, 6, step(0), step(n_t // 2), step(n_t - 1))
    ex_args, ex_in, ex_shapes, ex_out, ex_sems = _carried(ex)
    tok = pl.BlockSpec((t, d), lambda i: (i, 0))
    vec = _full((1, d))
    out = pl.pallas_call(
        body, name=name, grid=(n_t,),
        out_shape=(jax.ShapeDtypeStruct((s, 3 * e), BF16), jax.ShapeDtypeStruct((s, d), BF16),
                   jax.ShapeDtypeStruct((s, d), F32), jax.ShapeDtypeStruct((8, 128), F32),
                   jax.ShapeDtypeStruct((1, d), F32), jax.ShapeDtypeStruct((1, d), F32), *ex_shapes),
        in_specs=[tok, tok, vec, vec, vec, vec, vec, _resident((d, 3 * e)), _full((1, e)), _full((1, e)),
                  _full((GROUPS, CHUNK, CHUNK)), _resident((CHUNK, e)), _resident((e, d)), *ex_in],
        out_specs=(pl.BlockSpec((t, 3 * e), lambda i: (i, 0)), tok, tok, _full((8, 128)), vec, vec, *ex_out),
        scratch_shapes=[pltpu.VMEM((t, e), F32), pltpu.VMEM((t, e), BF16), *ex_sems],
        compiler_params=_params(("arbitrary",)),
    )(x1, tgt, g1, scale, shift, gate, fg, wi, lng, lnb, wt, bsf, wo, *ex_args)
    return (*out[:6], out[6:])


def _norm_modulate_bwd(dh, x_ref, dres_ref, g_ref, sc_ref, dx_ref, dsh_ref, p_scr):
    xv = x_ref[...]
    r = lax.rsqrt(jnp.mean(xv * xv, axis=-1, keepdims=True) + RMS_EPS)
    xn = xv * r
    dsh_ref[...] += _colsum(dh)
    p_scr[...] += _colsum(dh * xn)
    dxn = dh * (g_ref[...] * (1.0 + sc_ref[...]))
    dx_ref[...] = r * (dxn - xn * jnp.mean(dxn * xn, axis=-1, keepdims=True)) + dres_ref[...]


def _layer_b_bwd(proj, dx2, x1, gate, g1, scale, lng, lnb, wt, wtt, bsf, wo, wi, name):
    s, e3 = proj.shape
    e = e3 // 3
    d = dx2.shape[1]
    gw = e // GROUPS
    t = min(s, 256)
    n_t = s // t

    def body(p_ref, dx_ref, x1_ref, gate_ref, g_ref, sc_ref, lng_ref, lnb_ref, wt_ref, wtt_ref, bsf_ref, wo_ref, wi_ref,
             dp_ref, y_ref, dx1_ref, dws_ref, dbs_ref, dlg_ref, dlb_ref, dsh_ref, dsc_ref, dg_ref,
             v_scr, dgv_scr, dbr_scr, dvn_scr, dbs_scr, p_scr, dy_scr, vn_scr, mixed_scr):
        @pl.when(pl.program_id(0) == 0)
        def _():
            dws_ref[...] = jnp.zeros_like(dws_ref)
            dlg_ref[...] = jnp.zeros_like(dlg_ref)
            dlb_ref[...] = jnp.zeros_like(dlb_ref)
            dsh_ref[...] = jnp.zeros_like(dsh_ref)
            dbs_scr[...] = jnp.zeros_like(dbs_scr)
            p_scr[...] = jnp.zeros_like(p_scr)
        dbr_scr[...] = (dx_ref[...] * gate_ref[...]).astype(BF16)
        mu, rs = _ln_stats(lambda g: p_ref[:, e + g * gw:e + (g + 1) * gw].astype(F32), v_scr, dgv_scr, t, e)
        tril = (lax.broadcasted_iota(jnp.int32, (CHUNK, CHUNK), 0) >= lax.broadcasted_iota(jnp.int32, (CHUNK, CHUNK), 1))
        c1 = jnp.zeros((t, 1), F32)
        c2 = jnp.zeros((t, 1), F32)
        span = 2
        kw = span * gw
        dh = jnp.zeros((t, d), F32)

        def through_w_in(c0):
            return lax.dot_general(dp_ref[:, c0:c0 + kw], wi_ref[:, c0:c0 + kw], NT_DIMS, preferred_element_type=F32)

        dy_scr[...] = lax.dot_general(dbr_scr[...], wo_ref[...], NT_DIMS, preferred_element_type=F32)
        for g in range(GROUPS):
            gs = slice(g * gw, (g + 1) * gw)
            vhat = (v_scr[:, gs] - mu) * rs
            v_scr[:, gs] = vhat
            vn = (vhat * lng_ref[:, gs] + lnb_ref[:, gs]).astype(BF16)
            vn_scr[:, gs] = vn
            for ch in range(t // CHUNK):
                rows = slice(ch * CHUNK, (ch + 1) * CHUNK)
                mixed_scr[rows, gs] = jnp.dot(wt_ref[g], vn[rows], preferred_element_type=F32) + bsf_ref[:, gs]
        for g in range(GROUPS):
            gs = slice(g * gw, (g + 1) * gw)
            vhat = v_scr[:, gs]
            lg = lng_ref[:, gs]
            for ch in range(t // CHUNK):
                rows = slice(ch * CHUNK, (ch + 1) * CHUNK)
                mixed = mixed_scr[rows, gs]
                u, dgu = _gelu(p_ref[rows, g * gw:(g + 1) * gw].astype(F32))
                sz, dsz = _silu(p_ref[rows, 2 * e + g * gw:2 * e + (g + 1) * gw].astype(F32))
                sgate = u * mixed
                y_ref[rows, gs] = (sz * sgate).astype(BF16)
                dy = dy_scr[rows, gs]
                dp_ref[rows, 2 * e + g * gw:2 * e + (g + 1) * gw] = (dy * sgate * dsz).astype(BF16)
                ds = dy * sz
                dp_ref[rows, gs] = (ds * mixed * dgu).astype(BF16)
                dm = ds * u
                dbs_scr[:, gs] += dm
                dmb = dm.astype(BF16)
                dws_ref[g] += jnp.where(tril, lax.dot_general(dmb, vn_scr[rows, gs], NT_DIMS, preferred_element_type=F32), 0.0)
                dvn_scr[rows, gs] = jnp.dot(wtt_ref[g], dmb, preferred_element_type=F32)
            dvn = dvn_scr[:, gs]
            dlb_ref[:, gs] += _colsum(dvn)
            dlg_ref[:, gs] += _colsum(dvn * vhat)
            dvh = dvn * lg
            c1 = c1 + _rowsum(dvh)
            c2 = c2 + _rowsum(dvh * vhat)
            if g % span == span - 1:
                dh = dh + through_w_in(g * gw + gw - kw) + through_w_in(2 * e + g * gw + gw - kw)
        c1 = c1 * (1.0 / e)
        c2 = c2 * (1.0 / e)
        for g in range(GROUPS):
            gs = slice(g * gw, (g + 1) * gw)
            dv = rs * (dvn_scr[:, gs] * lng_ref[:, gs] - c1 - v_scr[:, gs] * c2)
            dp_ref[:, e + g * gw:e + (g + 1) * gw] = (dv * dgv_scr[:, gs]).astype(BF16)
            if g % span == span - 1:
                dh = dh + through_w_in(e + g * gw + gw - kw)
        _norm_modulate_bwd(dh, x1_ref, dx_ref, g_ref, sc_ref, dx1_ref, dsh_ref, p_scr)

        @pl.when(pl.program_id(0) == n_t - 1)
        def _():
            lane = lax.broadcasted_iota(jnp.int32, (CHUNK, 128), 1)
            acc = jnp.zeros((CHUNK, 128), F32)
            for g in range(GROUPS):
                acc = acc + jnp.where(lane == g, _rowsum(dbs_scr[:, g * gw:(g + 1) * gw]), 0.0)
            dbs_ref[...] = acc
            dsc_ref[...] = p_scr[...] * g_ref[...]
            dg_ref[...] = p_scr[...] * (1.0 + sc_ref[...])

    tok = pl.BlockSpec((t, d), lambda i: (i, 0))
    vec, evec, ws = _full((1, d)), _full((1, e)), _full((GROUPS, CHUNK, CHUNK))
    vshape = jax.ShapeDtypeStruct((1, d), F32)
    return pl.pallas_call(
        body, name=name, grid=(n_t,),
        out_shape=(jax.ShapeDtypeStruct((s, e3), BF16), jax.ShapeDtypeStruct((s, e), BF16), jax.ShapeDtypeStruct((s, d), F32),
                   jax.ShapeDtypeStruct((GROUPS, CHUNK, CHUNK), F32), jax.ShapeDtypeStruct((CHUNK, 128), F32),
                   jax.ShapeDtypeStruct((1, e), F32), jax.ShapeDtypeStruct((1, e), F32), vshape, vshape, vshape),
        in_specs=[pl.BlockSpec((t, e3), lambda i: (i, 0)), tok, tok, vec, vec, vec, evec, evec, ws, ws,
                  _resident((CHUNK, e)), _resident((e, d)), _resident((d, e3))],
        out_specs=(pl.BlockSpec((t, e3), lambda i: (i, 0)), pl.BlockSpec((t, e), lambda i: (i, 0)), tok,
                   ws, _full((CHUNK, 128)), evec, evec, vec, vec, vec),
        scratch_shapes=[pltpu.VMEM((t, e), F32), pltpu.VMEM((t, e), BF16), pltpu.VMEM((t, d), BF16),
                        pltpu.VMEM((t, e), F32), pltpu.VMEM((CHUNK, e), F32), pltpu.VMEM((1, d), F32),
                        pltpu.VMEM((t, e), F32), pltpu.VMEM((t, e), BF16), pltpu.VMEM((t, e), F32)],
        compiler_params=_params(("arbitrary",)),
    )(proj, dx2, x1, gate, g1, scale, lng, lnb, wt, wtt, bsf, wo, wi)


def _conv_mixer_bwd(proj, dx1, br, tails, cw, cb, gate, wo, name, ex=None):
    s, e4 = proj.shape
    e = e4 // 4
    d = dx1.shape[1]
    t = min(s, 256)
    n_t = s // t
    cwid = min(e, 512)

    def body(p_ref, dx_ref, br_ref, tails_ref, cw_ref, cb_ref, gate_ref, wo_ref,
             dp_ref, y_ref, dgate_ref, dcb_ref, dcw_ref, dy_scr, head_scr):
        i = pl.program_id(0)

        @pl.when(i == 0)
        def _():
            dgate_ref[...] = jnp.zeros_like(dgate_ref)
            dcb_ref[...] = jnp.zeros_like(dcb_ref)
            dcw_ref[...] = jnp.zeros_like(dcw_ref)
            head_scr[...] = jnp.zeros_like(head_scr)
        dx = dx_ref[...]
        dgate_ref[...] += _colsum(dx * br_ref[...].astype(F32))
        dy_scr[...] = lax.dot_general((dx * gate_ref[...]).astype(BF16), wo_ref[...], NT_DIMS,
                                      preferred_element_type=F32)
        row = lax.broadcasted_iota(jnp.int32, (t, cwid), 0)
        has_prev = (i < n_t - 1).astype(F32)
        for c0 in range(0, e, cwid):
            sl = slice(c0, c0 + cwid)
            bg = p_ref[:, c0:c0 + cwid].astype(F32)
            cg = p_ref[:, e + c0:e + c0 + cwid].astype(F32)
            xin = p_ref[:, 2 * e + c0:2 * e + c0 + cwid].astype(F32)
            z = p_ref[:, 3 * e + c0:3 * e + c0 + cwid].astype(F32)
            cx = cg * xin
            p1, p2 = _conv_taps(cx, tails_ref[6:7, sl] * has_prev, tails_ref[7:8, sl] * has_prev, row)
            w0, w1, w2 = cw_ref[0:1, sl], cw_ref[1:2, sl], cw_ref[2:3, sl]
            conv = cb_ref[:, sl] + w2 * cx + w0 * p2 + w1 * p1
            sz, dsz = _silu(z)
            dy = dy_scr[:, sl]
            y_ref[:, sl] = (sz * bg * conv).astype(BF16)
            dp_ref[:, 3 * e + c0:3 * e + c0 + cwid] = (dy * bg * conv * dsz).astype(BF16)
            dp_ref[:, c0:c0 + cwid] = (dy * sz * conv).astype(BF16)
            dconv = dy * sz * bg
            dcb_ref[:, sl] += _colsum(dconv)
            dcw_ref[2:3, sl] += _colsum(dconv * cx)
            dcw_ref[1:2, sl] += _colsum(dconv * p1)
            dcw_ref[0:1, sl] += _colsum(dconv * p2)
            h0, h1 = head_scr[0:1, sl], head_scr[1:2, sl]
            n1 = jnp.where(row == t - 1, h0, pltpu.roll(dconv, t - 1, 0))
            n2 = jnp.where(row == t - 2, h0, jnp.where(row == t - 1, h1, pltpu.roll(dconv, t - 2, 0)))
            dcx = w2 * dconv + w1 * n1 + w0 * n2
            dp_ref[:, e + c0:e + c0 + cwid] = (dcx * xin).astype(BF16)
            dp_ref[:, 2 * e + c0:2 * e + c0 + cwid] = (dcx * cg).astype(BF16)
            head_scr[:, sl] = dconv[0:8, :]

    body = _carry(ex, body, 8, 5, lambda: pl.program_id(0) == 0, None, lambda: pl.program_id(0) == n_t - 1)
    ex_args, ex_in, ex_shapes, ex_out, ex_sems = _carried(ex)
    rev = lambda i: (n_t - 1 - i, 0)
    out = pl.pallas_call(
        body, name=name, grid=(n_t,),
        out_shape=(jax.ShapeDtypeStruct((s, e4), BF16), jax.ShapeDtypeStruct((s, e), BF16),
                   jax.ShapeDtypeStruct((1, d), F32), jax.ShapeDtypeStruct((1, e), F32), jax.ShapeDtypeStruct((8, e), F32),
                   *ex_shapes),
        in_specs=[pl.BlockSpec((t, e4), rev), pl.BlockSpec((t, d), rev), pl.BlockSpec((t, d), rev),
                  pl.BlockSpec((None, 8, e), lambda i: (jnp.maximum(n_t - 2 - i, 0), 0, 0)),
                  _full((3, e)), _full((1, e)), _full((1, d)), _full((e, d)), *ex_in],
        out_specs=(pl.BlockSpec((t, e4), rev), pl.BlockSpec((t, e), rev), _full((1, d)), _full((1, e)), _full((8, e)),
                   *ex_out),
        scratch_shapes=[pltpu.VMEM((t, e), F32), pltpu.VMEM((8, e), F32), *ex_sems],
        compiler_params=_params(("arbitrary",)),
    )(proj, dx1, br, tails, cw, cb, gate, wo, *ex_args)
    return (*out[:5], out[5:])


def _matmul_nt_norm_bwd(dproj, w, xin, dres, g, scale, name, ex=None):
    s, d = xin.shape
    n = w.shape[1]
    tm = min(s, 512)
    n_i = s // tm

    def body(dp_ref, w_ref, x_ref, dres_ref, g_ref, sc_ref, dx_ref, dsh_ref, dsc_ref, dg_ref, p_scr):
        i = pl.program_id(0)

        @pl.when(i == 0)
        def _():
            dsh_ref[...] = jnp.zeros_like(dsh_ref)
            p_scr[...] = jnp.zeros_like(p_scr)
        dh = lax.dot_general(dp_ref[...], w_ref[...], NT_DIMS, preferred_element_type=F32)
        _norm_modulate_bwd(dh, x_ref, dres_ref, g_ref, sc_ref, dx_ref, dsh_ref, p_scr)

        @pl.when(i == n_i - 1)
        def _():
            dsc_ref[...] = p_scr[...] * g_ref[...]
            dg_ref[...] = p_scr[...] * (1.0 + sc_ref[...])

    body = _carry(ex, body, 6, 4, lambda: pl.program_id(0) == 0, None, lambda: pl.program_id(0) == n_i - 1)
    ex_args, ex_in, ex_shapes, ex_out, ex_sems = _carried(ex)
    tok = pl.BlockSpec((tm, d), lambda i: (i, 0))
    vec = pl.BlockSpec((1, d), lambda i: (0, 0))
    vshape = jax.ShapeDtypeStruct((1, d), F32)
    out = pl.pallas_call(
        body, name=name, grid=(n_i,),
        out_shape=(jax.ShapeDtypeStruct((s, d), F32), vshape, vshape, vshape, *ex_shapes),
        in_specs=[pl.BlockSpec((tm, n), lambda i: (i, 0)), _resident((d, n)), tok, tok, vec, vec, *ex_in],
        out_specs=(tok, vec, vec, vec, *ex_out),
        scratch_shapes=[pltpu.VMEM((1, d), F32), *ex_sems],
        compiler_params=_params(("arbitrary",)),
    )(dproj, w, xin, dres, g, scale, *ex_args)
    return (*out[:4], out[4:])


def _matmul_tn(a, b, colscale, rows_split, name, ex=None, a_cols=None):
    s, m = a.shape
    a_blk = 0
    if a_cols is not None:
        a_blk, m = a_cols
    n = b.shape[1]
    n_j, tn = (1, n) if rows_split else (NDEV, n // NDEV)
    fixed = (4 + 4 + 2 * 2) * m * tn
    tk = s
    while fixed + 2 * tk * (2 * m + b.dtype.itemsize * tn) > TN_VMEM_BUDGET:
        tk //= 2
    n_k = s // tk

    def body(a_ref, b_ref, cs_ref, o_ref, acc):
        k = pl.program_id(1)
        part = lax.dot_general(a_ref[...], b_ref[...].astype(BF16), TN_DIMS, preferred_element_type=F32)
        if n_k == 1:
            o_ref[...] = (part * cs_ref[...]).astype(BF16)
            return

        @pl.when(k == 0)
        def _():
            acc[...] = part

        @pl.when((k > 0) & (k < n_k - 1))
        def _():
            acc[...] += part

        @pl.when(k == n_k - 1)
        def _():
            o_ref[...] = ((acc[...] + part) * cs_ref[...]).astype(BF16)

    at = lambda j, k: (pl.program_id(0) == j) & (pl.program_id(1) == k)
    body = _carry(ex, body, 3, 1, lambda: at(0, 0), None, lambda: at(n_j - 1, n_k - 1))
    ex_args, ex_in, ex_shapes, ex_out, ex_sems = _carried(ex)
    out = pl.pallas_call(
        body, name=name, grid=(n_j, n_k),
        out_shape=(jax.ShapeDtypeStruct((n_j, m, tn), BF16), *ex_shapes),
        in_specs=[pl.BlockSpec((tk, m), lambda j, k: (k, a_blk)), pl.BlockSpec((tk, tn), lambda j, k: (k, j)),
                  pl.BlockSpec((1, tn), lambda j, k: (0, j)), *ex_in],
        out_specs=(pl.BlockSpec((None, m, tn), lambda j, k: (j, 0, 0)), *ex_out),
        scratch_shapes=[pltpu.VMEM((m, tn), F32), *ex_sems],
        compiler_params=_params(("arbitrary", "arbitrary")),
    )(a, b, colscale, *ex_args)
    return (out[0].reshape(NDEV, m // NDEV, n) if rows_split else out[0]), out[1:]


def _adam_update(w, g, m, v):
    m = ADAM_B1 * m + (1.0 - ADAM_B1) * g
    v = ADAM_B2 * v + (1.0 - ADAM_B2) * (g * g)
    m_hat = m / (1.0 - ADAM_B1 ** ADAM_STEP)
    v_hat = v / (1.0 - ADAM_B2 ** ADAM_STEP)
    return -ADAM_LR * (m_hat / (jnp.sqrt(v_hat) + ADAM_EPS) + ADAM_WD * w), m, v


def _adamw_reduce(parts, w, m, v, name, ex=None):
    n_l = len(parts)
    n_p, r_l, c = parts[0].shape
    r = r_l * n_l
    tr = min(r_l, 128 if ex is not None else 256)
    n_i, per = r // tr, r_l // tr

    def body(*refs):
        p_refs, (w_ref, m_ref, v_ref, g_out, d_out, m_out, v_out) = refs[:n_l], refs[n_l:]
        g = None
        for l, p_ref in enumerate(p_refs):
            g_l = p_ref[0].astype(F32)
            for j in range(1, n_p):
                g_l = g_l + p_ref[j].astype(F32)
            g = g_l if g is None else jnp.where(pl.program_id(0) >= l * per, g_l, g)
        g_out[...] = g
        d_out[...], m_out[...], v_out[...] = _adam_update(w_ref[...], g, m_ref[...], v_ref[...])

    step = lambda k: (lambda: pl.program_id(0) == k)
    body = _carry(ex, body, n_l + 3, 4, step(0), step(n_i // 2), step(n_i - 1))
    ex_args, ex_in, ex_shapes, ex_out, ex_sems = _carried(ex)
    blk = pl.BlockSpec((tr, c), lambda i: (i, 0))
    p_specs = [pl.BlockSpec((n_p, tr, c), lambda i, l=l: (0, jnp.clip(i - l * per, 0, per - 1), 0)) for l in range(n_l)]
    shp = jax.ShapeDtypeStruct((r, c), F32)
    out = pl.pallas_call(
        body, name=name, grid=(n_i,), out_shape=(shp, shp, shp, shp, *ex_shapes),
        in_specs=[*p_specs, blk, blk, blk, *ex_in],
        out_specs=(blk, blk, blk, blk, *ex_out), scratch_shapes=ex_sems,
        compiler_params=_params(("arbitrary",)),
    )(*parts, w, m, v, *ex_args)
    return (*out[:4], out[4:])


def _adamw_small(gs, ws, ms, vs, name):
    n = len(gs)

    def body(*refs):
        ins, outs = refs[:4 * n], refs[4 * n:]
        for a in range(n):
            d, m, v = _adam_update(ins[n + a][...], ins[a][...], ins[2 * n + a][...], ins[3 * n + a][...])
            outs[a][...], outs[n + a][...], outs[2 * n + a][...] = d, m, v

    shapes = tuple(jax.ShapeDtypeStruct(w.shape, F32) for w in ws) * 3
    out = pl.pallas_call(
        body, name=name, out_shape=shapes,
        in_specs=[_vmem()] * (4 * n), out_specs=tuple([_vmem()] * (3 * n)),
        compiler_params=pltpu.CompilerParams(vmem_limit_bytes=VMEM_LIMIT),
    )(*gs, *ws, *ms, *vs)
    return out[:n], out[n:2 * n], out[2 * n:]


def _sum_devices(packed, name):
    _, r, wdt = packed.shape

    def body(p_ref, o_ref):
        acc = p_ref[0]
        for j in range(1, NDEV):
            acc = acc + p_ref[j]
        o_ref[...] = acc

    return pl.pallas_call(
        body, name=name, out_shape=jax.ShapeDtypeStruct((r, wdt), F32),
        in_specs=[_vmem()], out_specs=_vmem(),
        compiler_params=pltpu.CompilerParams(vmem_limit_bytes=VMEM_LIMIT),
    )(packed)


def _mod_w_grad(c_t, dmod, name):
    n_layers, _, w3 = dmod.shape
    d = c_t.shape[0]

    def body(c_ref, dm_ref, o_ref):
        for i in range(n_layers):
            acc = c_ref[:, 0:1] * dm_ref[i, 0:1, :]
            for b in range(1, NDEV):
                acc = acc + c_ref[:, b:b + 1] * dm_ref[i, b:b + 1, :]
            o_ref[i] = acc

    return pl.pallas_call(
        body, name=name, out_shape=jax.ShapeDtypeStruct((n_layers, d, w3), F32),
        in_specs=[_vmem(), _vmem()], out_specs=_vmem(),
        compiler_params=pltpu.CompilerParams(vmem_limit_bytes=VMEM_LIMIT),
    )(c_t, dmod)


def _mask_transpose_ws(w_s, name):
    def body(w_ref, wt_ref, wtt_ref):
        tril = (lax.broadcasted_iota(jnp.int32, (CHUNK, CHUNK), 0) >= lax.broadcasted_iota(jnp.int32, (CHUNK, CHUNK), 1))
        for g in range(GROUPS):
            wm = jnp.where(tril, w_ref[g], 0.0)
            wt_ref[g] = wm.astype(BF16)
            wtt_ref[g] = wm.T.astype(BF16)

    shp = jax.ShapeDtypeStruct(w_s.shape, BF16)
    return pl.pallas_call(
        body, name=name, out_shape=(shp, shp), in_specs=[_vmem()], out_specs=(_vmem(), _vmem()),
    )(w_s)


def _pack(pieces):
    flat = jnp.concatenate([p.reshape(-1) for p in pieces])
    rows = -(-flat.shape[0] // (8 * PACK_W)) * 8
    return jnp.pad(flat, (0, rows * PACK_W - flat.shape[0])).reshape(rows, PACK_W)


def _unpack(flat, shapes):
    out, off = [], 0
    for shp in shapes:
        size = 1
        for dim in shp:
            size *= dim
        out.append(flat[off:off + size].reshape(shp))
        off += size
    return out


def kernel(x, c, mod_w, mod_b, norm_g, a_w_in, a_conv_w, a_conv_b, a_w_out, b_w_in, b_ln_g, b_ln_b, b_w_s, b_b_s, b_w_out, final_g, loss_target, m_mod_w, m_mod_b, m_norm_g, m_a_w_in, m_a_conv_w, m_a_conv_b, m_a_w_out, m_b_w_in, m_b_ln_g, m_b_ln_b, m_b_w_s, m_b_b_s, m_b_w_out, m_final_g, v_mod_w, v_mod_b, v_norm_g, v_a_w_in, v_a_conv_w, v_a_conv_b, v_a_w_out, v_b_w_in, v_b_ln_g, v_b_ln_b, v_b_w_s, v_b_b_s, v_b_w_out, v_final_g):
    s, d = x.shape[1], x.shape[2]
    es = a_w_out.shape[1]
    e = NDEV * es
    w3 = mod_w.shape[2]
    me = _index(_pos())
    x0 = x.reshape(s, d)
    tgt = loss_target.reshape(s, d)

    small = jnp.concatenate([a_conv_w[0], b_ln_g, b_ln_b, jnp.zeros((3, es), F32)], axis=0)
    gather_a = _gather_exchange([a_w_in[0].astype(BF16), a_w_out[0].astype(BF16), small], [True, False, True])
    gather_b = _gather_exchange([b_w_in[0].astype(BF16), b_w_out[0].astype(BF16)], [True, False])
    mod, c_all, (wa, woa, small_all) = _mod_vectors(c, mod_w, mod_b, gather_a)
    conv_w, ln_g, ln_b = small_all[0:3], small_all[3:4], small_all[4:5]
    bsf = jnp.repeat(b_b_s[0].T, e // GROUPS, axis=1)
    wt, wtt = _mask_transpose_ws(b_w_s[0], "mask_w_s")
    shift0, scale0, gate0 = mod[0:1, 0:d], mod[0:1, d:2 * d], mod[0:1, 2 * d:]
    shift1, scale1, gate1 = mod[1:2, 0:d], mod[1:2, d:2 * d], mod[1:2, 2 * d:]
    g0, g1, fg = norm_g[0:1], norm_g[1:2], final_g.reshape(1, d)

    proj_a, h0, x1, br_a, tails, (wb, wob) = _layer_a_fwd(
        x0, g0, scale0, shift0, gate0, wa, conv_w, a_conv_b, woa, "a_fwd", gather_b)
    proj_b, h1, dx2, loss_acc, dfg, dgate1, _ = _layer_b_fwd_loss(
        x1, tgt, g1, scale1, shift1, gate1, fg, wb, ln_g, ln_b, wt, bsf, wob, "b_fwd_loss")

    dproj_b, y_b, dx1, dws, dbs, dlg, dlb, dshift1, dscale1, dg1 = _layer_b_bwd(
        proj_b, dx2, x1, gate1, g1, scale1, ln_g, ln_b, wt, wtt, bsf, wob, wb, "b_bwd")
    gs_b_out, _ = _matmul_tn(y_b, dx2, gate1, True, "b_w_out_grad")
    gs_b_in, _ = _matmul_tn(h1, dproj_b, jnp.ones((1, dproj_b.shape[1]), F32), False, "b_w_in_grad")
    dproj_a, y_a, dgate0, dcb, dcw, (gr_b_in, gr_b_out) = _conv_mixer_bwd(
        proj_a, dx1, br_a, tails, conv_w, a_conv_b, gate0, woa, "a_mixer_bwd", _scatter_exchange([gs_b_in, gs_b_out]))
    gs_a_out, _ = _matmul_tn(y_a, dx1, gate0, True, "a_w_out_grad")
    ones_n = jnp.ones((1, dproj_a.shape[1]), F32)
    gs_lo, (gr_a_out,) = _matmul_tn(h0, dproj_a, ones_n, False, "a_w_in_grad_lo", _scatter_exchange([gs_a_out]),
                                    a_cols=(0, d // 2))
    gs_hi, (gr_lo,) = _matmul_tn(h0, dproj_a, ones_n, False, "a_w_in_grad_hi", _scatter_exchange([gs_lo]),
                                 a_cols=(1, d // 2))
    dx0, dshift0, dscale0, dg0, (gr_hi,) = _matmul_nt_norm_bwd(dproj_a, wa, x0, dx1, g0, scale0, "a_in_bwd",
                                                               _scatter_exchange([gs_hi]))

    def big(parts, w, m, v, name, ex=None):
        shp = w.shape
        r2 = lambda t_: t_.reshape(-1, shp[-1])
        g, dl, nm, nv, ex_out = _adamw_reduce(parts, r2(w), r2(m), r2(v), name, ex)
        return tuple(t_.reshape(shp) for t_ in (g, dl, nm, nv)), ex_out

    res = {}
    res["b_w_in"], _ = big([gr_b_in], b_w_in, m_b_w_in, v_b_w_in, "adamw_b_w_in")
    gr_a_in = [gr_lo, gr_hi]

    pieces = [dshift0, dscale0, dgate0, dshift1, dscale1, dgate1, dg0, dg1, dcb, dcw[0:3], dlg, dlb, dfg,
              dbs[:, 0:GROUPS].T, dws, loss_acc[0:1, 0:1]]
    shapes = [p.shape for p in pieces]
    res["a_w_in"], (packed_all,) = big(gr_a_in, a_w_in, m_a_w_in, v_a_w_in, "adamw_a_w_in",
                                       _gather_exchange([_pack(pieces)], [False]))
    packed_all = packed_all.reshape(NDEV, -1, PACK_W)
    total = _sum_devices(packed_all, "sum_small_grads").reshape(-1)
    (t_sh0, t_sc0, t_ga0, t_sh1, t_sc1, t_ga1, t_g0, t_g1, t_cb, t_cw, t_lg, t_lb, t_fg, t_bs, t_ws, t_loss) = _unpack(
        total, shapes)
    loss = t_loss.reshape(())
    grad_mod_b = jnp.concatenate([jnp.concatenate([t_sh0, t_sc0, t_ga0], axis=1),
                                  jnp.concatenate([t_sh1, t_sc1, t_ga1], axis=1)], axis=0)
    grad_norm_g = jnp.concatenate([t_g0, t_g1], axis=0)
    dmod_all = packed_all.reshape(NDEV, -1)[:, 0:6 * d].reshape(NDEV, 2, 3 * d).transpose(1, 0, 2)
    dmod_mine = lax.dynamic_slice_in_dim(dmod_all, me * w3, w3, axis=2)
    grad_mod_w = _mod_w_grad(c_all.T, dmod_mine, "mod_w_grad")
    grad_a_conv_w = lax.dynamic_slice_in_dim(t_cw, me * es, es, axis=1)
    grad_b_ln_g = lax.dynamic_slice_in_dim(t_lg, me * es, es, axis=1)
    grad_b_ln_b = lax.dynamic_slice_in_dim(t_lb, me * es, es, axis=1)

    res["mod_w"], _ = big([grad_mod_w.reshape(1, -1, w3)], mod_w, m_mod_w, v_mod_w, "adamw_mod_w")
    res["a_w_out"], _ = big([gr_a_out], a_w_out, m_a_w_out, v_a_w_out, "adamw_a_w_out")
    res["b_w_out"], _ = big([gr_b_out], b_w_out, m_b_w_out, v_b_w_out, "adamw_b_w_out")

    small_names = ["mod_b", "norm_g", "a_conv_w", "a_conv_b", "b_ln_g", "b_ln_b", "b_w_s", "b_b_s", "final_g"]
    small_g = [grad_mod_b, grad_norm_g, grad_a_conv_w, t_cb, grad_b_ln_g, grad_b_ln_b, t_ws, t_bs, t_fg]
    small_w = [mod_b, norm_g, a_conv_w, a_conv_b, b_ln_g, b_ln_b, b_w_s, b_b_s, final_g]
    small_m = [m_mod_b, m_norm_g, m_a_conv_w, m_a_conv_b, m_b_ln_g, m_b_ln_b, m_b_w_s, m_b_b_s, m_final_g]
    small_v = [v_mod_b, v_norm_g, v_a_conv_w, v_a_conv_b, v_b_ln_g, v_b_ln_b, v_b_w_s, v_b_b_s, v_final_g]
    as2d = lambda t_: t_.reshape(-1, t_.shape[-1])
    dls, nms, nvs = _adamw_small([as2d(t_) for t_ in small_g], [as2d(t_) for t_ in small_w],
                                 [as2d(t_) for t_ in small_m], [as2d(t_) for t_ in small_v], "adamw_small")
    for a, nme in enumerate(small_names):
        shp = small_w[a].shape
        res[nme] = (small_g[a].reshape(shp), dls[a].reshape(shp), nms[a].reshape(shp), nvs[a].reshape(shp))

    order = ["mod_w", "mod_b", "norm_g", "a_w_in", "a_conv_w", "a_conv_b", "a_w_out", "b_w_in", "b_ln_g", "b_ln_b",
             "b_w_s", "b_b_s", "b_w_out", "final_g"]
    return (loss, dx0.reshape(x.shape), *[res[k][0] for k in order], *[res[k][1] for k in order],
            *[res[k][2] for k in order], *[res[k][3] for k in order])
```

```python
import functools

import jax
import jax.numpy as jnp
from jax import lax
from jax.experimental import pallas as pl
from jax.experimental.pallas import tpu as pltpu

NDEV = 8
CHUNK = 128
GROUPS = 8
RMS_EPS = 1e-6
LN_EPS = 1e-5
ADAM_LR, ADAM_B1, ADAM_B2, ADAM_EPS, ADAM_WD, ADAM_STEP = 0.001, 0.9, 0.999, 1e-08, 0.01, 10
V7X_VMEM_BYTES = 64 * 1024 * 1024
VMEM_LIMIT = V7X_VMEM_BYTES - 8 * 1024 * 1024
TN_VMEM_BUDGET = 46 * 1024 * 1024
PACK_W = 1024
F32, BF16 = jnp.float32, jnp.bfloat16
MESH = pl.DeviceIdType.MESH
RSQRT2 = 0.7071067811865476
INV_SQRT_2PI = 0.3989422804014327
NT_DIMS = (((1,), (1,)), ((), ()))
TN_DIMS = (((0,), (0,)), ((), ()))


def _params(sem=None):
    return pltpu.CompilerParams(dimension_semantics=sem, vmem_limit_bytes=VMEM_LIMIT)


def _vmem():
    return pl.BlockSpec(memory_space=pltpu.VMEM)


def _hbm():
    return pl.BlockSpec(memory_space=pltpu.HBM)


def _full(shape):
    return pl.BlockSpec(shape, lambda *_: (0,) * len(shape))


def _pos():
    return lax.axis_index("x"), lax.axis_index("y"), lax.axis_index("c")


def _index(p):
    return 4 * p[0] + 2 * p[1] + p[2]


def _peer(k):
    x, y, c = _pos()
    return ((1 - x) if (k >> 2) & 1 else x, (1 - y) if (k >> 1) & 1 else y, (1 - c) if k & 1 else c)


def _silu(z):
    sg = jax.nn.sigmoid(z)
    return z * sg, sg * (1.0 + z * (1.0 - sg))


def _gelu(v):
    phi = 0.5 * (1.0 + lax.erf(v * RSQRT2))
    return v * phi, phi + v * (jnp.exp(-0.5 * v * v) * INV_SQRT_2PI)


def _colsum(v):
    return jnp.sum(v, axis=0, keepdims=True)


def _rowsum(v):
    return jnp.sum(v, axis=-1, keepdims=True)


def _gather_all_vmem(slab_ref, send_sems, recv_sems, base):
    me = _index(_pos())
    sends = []
    for k in range(1, NDEV):
        cp = pltpu.make_async_remote_copy(
            src_ref=slab_ref.at[me], dst_ref=slab_ref.at[me],
            send_sem=send_sems.at[base + k - 1], recv_sem=recv_sems.at[base + k - 1],
            device_id=_peer(k), device_id_type=MESH)
        cp.start()
        sends.append(cp)
    for k in range(1, NDEV):
        src = _index(_peer(k))
        pltpu.make_async_remote_copy(
            src_ref=slab_ref.at[src], dst_ref=slab_ref.at[src],
            send_sem=send_sems.at[base + k - 1], recv_sem=recv_sems.at[base + k - 1],
            device_id=_peer(k), device_id_type=MESH).wait_recv()
    for cp in sends:
        cp.wait_send()


def _mod_vectors(c, mod_w, mod_b, ex):
    n_layers, d, w3 = mod_w.shape
    r_in, r_out = len(ex.arrays), len(ex.out_shapes)

    def body(*refs):
        c_ref, mw_ref, mb_ref = refs[:3]
        ex_ins = refs[3:3 + r_in]
        mod_ref, call_ref = refs[3 + r_in:5 + r_in]
        ex_outs = refs[5 + r_in:5 + r_in + r_out]
        cslab, pslab, send_sems, recv_sems = refs[5 + r_in + r_out:9 + r_in + r_out]
        ex_sems = refs[9 + r_in + r_out:]
        ex.start(ex_ins, ex_outs, ex_sems)
        me = _index(_pos())
        cv = c_ref[...]
        cslab[me] = jnp.broadcast_to(cv * jax.nn.sigmoid(cv), (8, d))
        _gather_all_vmem(cslab, send_sems, recv_sems, 0)
        c_all = jnp.concatenate([cslab[k, 0:1, :] for k in range(NDEV)], axis=0)
        call_ref[...] = c_all
        for i in range(n_layers):
            pslab[me, i * NDEV:(i + 1) * NDEV, :] = jnp.dot(
                c_all, mw_ref[i], preferred_element_type=F32, precision=lax.Precision.HIGHEST)
        _gather_all_vmem(pslab, send_sems, recv_sems, NDEV - 1)
        for i in range(n_layers):
            for k in range(NDEV):
                mod_ref[i:i + 1, k * w3:(k + 1) * w3] = (
                    pslab[k, pl.ds(i * NDEV + me, 1), :] + mb_ref[i:i + 1, k * w3:(k + 1) * w3])
        for passing_on in ex.middles:
            passing_on(ex_ins, ex_outs, ex_sems)
        ex.finish(ex_ins, ex_outs, ex_sems)

    out = pl.pallas_call(
        body, name="mod_vectors",
        out_shape=(jax.ShapeDtypeStruct((n_layers, 3 * d), F32), jax.ShapeDtypeStruct((NDEV, d), F32), *ex.out_shapes),
        in_specs=[_vmem(), _vmem(), _vmem()] + [_hbm()] * r_in, out_specs=(_vmem(), _vmem(), *([_hbm()] * r_out)),
        scratch_shapes=[pltpu.VMEM((NDEV, 8, d), F32), pltpu.VMEM((NDEV, n_layers * NDEV, w3), F32),
                        pltpu.SemaphoreType.DMA((2 * (NDEV - 1),)), pltpu.SemaphoreType.DMA((2 * (NDEV - 1),)), *ex.sems],
        compiler_params=pltpu.CompilerParams(vmem_limit_bytes=VMEM_LIMIT),
    )(c, mod_w, mod_b, *ex.arrays)
    return out[0], out[1], out[2:]


class _Exchange:
    def __init__(self, arrays, out_shapes, sems, start, middles, finish):
        self.arrays, self.out_shapes, self.sems = list(arrays), list(out_shapes), list(sems)
        self.start, self.middles, self.finish = start, list(middles), finish


def _gather_exchange(shards, by_cols):
    n = len(shards)
    shapes = [sh.shape for sh in shards]

    def tools(ins, outs, sems):
        send_sems, recv_sems, local_sems = sems
        x, y, c = _pos()
        chips = [(1 - x, y), (x, 1 - y), (1 - x, 1 - y)]
        south = c == 0
        relayed = (jnp.where(south, 1 - x, x), jnp.where(south, y, 1 - y), c)
        relay_to = (jnp.where(south, x, 1 - x), jnp.where(south, 1 - y, y), c)

        def place(a, block):
            r, cc = shapes[a]
            if by_cols[a]:
                return outs[a].at[:, pl.ds(_index(block) * cc, cc)]
            return outs[a].at[pl.ds(_index(block) * r, r), :]

        def copy(a, k, block, to, src=None):
            dst = place(a, block)
            return pltpu.make_async_remote_copy(
                src_ref=dst if src is None else src, dst_ref=dst,
                send_sem=send_sems.at[a * 7 + k], recv_sem=recv_sems.at[a * 7 + k],
                device_id=to, device_id_type=MESH)

        mine = [pltpu.make_async_copy(ins[a], place(a, (x, y, c)), local_sems.at[a]) for a in range(n)]
        first = []
        for a in range(n):
            first.append(copy(a, 0, (x, y, c), (x, y, 1 - c), src=ins[a]))
            first += [copy(a, 1 + j, (x, y, c), (*chip, c), src=ins[a]) for j, chip in enumerate(chips[:2])]
        relays = [copy(a, 3, relayed, relay_to) for a in range(n)]
        passed = [copy(a, 4 + j, (*chip, c), (x, y, 1 - c)) for j, chip in enumerate(chips) for a in range(n)]
        return (x, y, c), chips, copy, mine, first, relays, passed

    def start(ins, outs, sems):
        _, _, _, mine, first, _, _ = tools(ins, outs, sems)
        for cp in mine + first:
            cp.start()

    def pass_neighbours(ins, outs, sems):
        (x, y, c), chips, copy, _, _, relays, passed = tools(ins, outs, sems)
        for j, chip in enumerate(chips[:2]):
            for a in range(n):
                copy(a, 1 + j, (*chip, c), (x, y, c)).wait_recv()
                passed[j * n + a].start()
        for cp in relays:
            cp.start()

    def pass_diagonal(ins, outs, sems):
        (x, y, c), chips, copy, _, _, _, passed = tools(ins, outs, sems)
        for a in range(n):
            copy(a, 3, (*chips[2], c), (x, y, c)).wait_recv()
            passed[2 * n + a].start()

    def finish(ins, outs, sems):
        (x, y, c), chips, copy, mine, first, relays, passed = tools(ins, outs, sems)
        for a in range(n):
            copy(a, 0, (x, y, 1 - c), (x, y, c)).wait_recv()
        for j, chip in enumerate(chips):
            for a in range(n):
                copy(a, 4 + j, (*chip, 1 - c), (x, y, c)).wait_recv()
        for cp in first + relays + passed:
            cp.wait_send()
        for cp in mine:
            cp.wait()

    out_shapes = [jax.ShapeDtypeStruct((r, NDEV * cc) if bc else (NDEV * r, cc), sh.dtype)
                  for (r, cc), bc, sh in zip(shapes, by_cols, shards)]
    sems = [pltpu.SemaphoreType.DMA((7 * n,)), pltpu.SemaphoreType.DMA((7 * n,)), pltpu.SemaphoreType.DMA((n,))]
    return _Exchange(shards, out_shapes, sems, start, [pass_neighbours, pass_diagonal], finish)


def _scatter_exchange(parts):
    n = len(parts)

    def tools(ins, outs, sems):
        send_sems, recv_sems, local_sems = sems
        me = _index(_pos())
        mine = [pltpu.make_async_copy(ins[a].at[me], outs[a].at[me], local_sems.at[a]) for a in range(n)]
        sends, arrivals = [], []
        for k in range(1, NDEV):
            peer = _peer(k)
            for a in range(n):
                pair = dict(send_sem=send_sems.at[a * 7 + k - 1], recv_sem=recv_sems.at[a * 7 + k - 1],
                            device_id=peer, device_id_type=MESH)
                sends.append(pltpu.make_async_remote_copy(src_ref=ins[a].at[_index(peer)], dst_ref=outs[a].at[me], **pair))
                slot = outs[a].at[_index(peer)]
                arrivals.append(pltpu.make_async_remote_copy(src_ref=slot, dst_ref=slot, **pair))
        return mine, sends, arrivals

    def start(ins, outs, sems):
        mine, sends, _ = tools(ins, outs, sems)
        for cp in mine + sends:
            cp.start()

    def finish(ins, outs, sems):
        mine, sends, arrivals = tools(ins, outs, sems)
        for cp in arrivals:
            cp.wait_recv()
        for cp in sends:
            cp.wait_send()
        for cp in mine:
            cp.wait()

    out_shapes = [jax.ShapeDtypeStruct(p.shape, p.dtype) for p in parts]
    sems = [pltpu.SemaphoreType.DMA((7 * n,)), pltpu.SemaphoreType.DMA((7 * n,)), pltpu.SemaphoreType.DMA((n,))]
    return _Exchange(parts, out_shapes, sems, start, [], finish)


def _carry(ex, body, n_in, n_out, first, middle, last):
    if ex is None:
        return body
    r_in, r_out = len(ex.arrays), len(ex.out_shapes)

    def wrapped(*refs):
        ins, rins = refs[:n_in], refs[n_in:n_in + r_in]
        outs = refs[n_in + r_in:n_in + r_in + n_out]
        routs = refs[n_in + r_in + n_out:n_in + r_in + n_out + r_out]
        rest = refs[n_in + r_in + n_out + r_out:]
        scratch, sems = rest[:len(rest) - len(ex.sems)], rest[len(rest) - len(ex.sems):]

        @pl.when(first())
        def _():
            ex.start(rins, routs, sems)

        for passing_on, at_step in zip(ex.middles, middle or []):
            pl.when(at_step())(functools.partial(passing_on, rins, routs, sems))

        body(*ins, *outs, *scratch)

        @pl.when(last())
        def _():
            ex.finish(rins, routs, sems)

    return wrapped


def _carried(ex):
    if ex is None:
        return [], [], [], [], []
    return ex.arrays, [_hbm()] * len(ex.arrays), ex.out_shapes, [_hbm()] * len(ex.out_shapes), ex.sems


def _resident(shape):
    return pl.BlockSpec(shape, lambda *_: (0,) * len(shape), pipeline_mode=pl.Buffered(1))


def _norm_modulate(x_ref, g_ref, sc_ref, sh_ref):
    xv = x_ref[...]
    r = lax.rsqrt(jnp.mean(xv * xv, axis=-1, keepdims=True) + RMS_EPS)
    return ((xv * r) * g_ref[...] * (1.0 + sc_ref[...]) + sh_ref[...]).astype(BF16)


def _conv_taps(cx, t6, t7, row):
    p1 = jnp.where(row == 0, t7, pltpu.roll(cx, 1, 0))
    p2 = jnp.where(row == 0, t6, jnp.where(row == 1, t7, pltpu.roll(cx, 2, 0)))
    return p1, p2


def _layer_a_fwd(x, g, scale, shift, gate, wi, cw, cb, wo, name, ex=None):
    s, d = x.shape
    e = wo.shape[0]
    t = min(s, 256)
    n_t = s // t
    cwid = min(e, 512)

    def body(x_ref, g_ref, sc_ref, sh_ref, gate_ref, wi_ref, cw_ref, cb_ref, wo_ref,
             proj_ref, h_ref, x1_ref, br_ref, tails_ref, y_scr, tail_scr):
        @pl.when(pl.program_id(0) == 0)
        def _():
            tail_scr[...] = jnp.zeros_like(tail_scr)
        h_ref[...] = _norm_modulate(x_ref, g_ref, sc_ref, sh_ref)
        row = lax.broadcasted_iota(jnp.int32, (t, cwid), 0)

        def project(c0):
            v = jnp.dot(h_ref[...], wi_ref[:, c0:c0 + cwid], preferred_element_type=F32)
            proj_ref[:, c0:c0 + cwid] = v.astype(BF16)
            return v

        for c0 in range(0, e, cwid):
            sl = slice(c0, c0 + cwid)
            bg, z = project(c0), project(3 * e + c0)
            cx = project(e + c0) * project(2 * e + c0)
            p1, p2 = _conv_taps(cx, tail_scr[6:7, sl], tail_scr[7:8, sl], row)
            conv = cb_ref[:, sl] + cw_ref[2:3, sl] * cx + cw_ref[0:1, sl] * p2 + cw_ref[1:2, sl] * p1
            y_scr[:, sl] = (_silu(z)[0] * bg * conv).astype(BF16)
            tail_scr[:, sl] = cx[t - 8:t, :]
        tails_ref[...] = tail_scr[...]
        br = jnp.dot(y_scr[...], wo_ref[...], preferred_element_type=F32)
        x1_ref[...] = x_ref[...] + gate_ref[...] * br
        br_ref[...] = br.astype(BF16)

    step = lambda k: (lambda: pl.program_id(0) == k)
    body = _carry(ex, body, 9, 5, step(0), [step(n_t // 3), step((2 * n_t) // 3)], step(n_t - 1))
    ex_args, ex_in, ex_shapes, ex_out, ex_sems = _carried(ex)
    tok = pl.BlockSpec((t, d), lambda i: (i, 0))
    out = pl.pallas_call(
        body, name=name, grid=(n_t,),
        out_shape=(jax.ShapeDtypeStruct((s, 4 * e), BF16), jax.ShapeDtypeStruct((s, d), BF16),
                   jax.ShapeDtypeStruct((s, d), F32), jax.ShapeDtypeStruct((s, d), BF16),
                   jax.ShapeDtypeStruct((n_t, 8, e), F32), *ex_shapes),
        in_specs=[tok, _full((1, d)), _full((1, d)), _full((1, d)), _full((1, d)), _resident((d, 4 * e)),
                  _full((3, e)), _full((1, e)), _resident((e, d)), *ex_in],
        out_specs=(pl.BlockSpec((t, 4 * e), lambda i: (i, 0)), tok, tok, tok,
                   pl.BlockSpec((None, 8, e), lambda i: (i, 0, 0)), *ex_out),
        scratch_shapes=[pltpu.VMEM((t, e), BF16), pltpu.VMEM((8, e), F32), *ex_sems],
        compiler_params=_params(("arbitrary",)),
    )(x, g, scale, shift, gate, wi, cw, cb, wo, *ex_args)
    return (*out[:5], out[5:])


def _ln_stats(v_pre, v_scr, dgv_scr, t, e):
    gw = e // GROUPS
    s1 = jnp.zeros((t, 1), F32)
    for g in range(GROUPS):
        v, dgv = _gelu(v_pre(g))
        v_scr[:, g * gw:(g + 1) * gw] = v
        if dgv_scr is not None:
            dgv_scr[:, g * gw:(g + 1) * gw] = dgv.astype(dgv_scr.dtype)
        s1 = s1 + _rowsum(v)
    mu = s1 * (1.0 / e)
    s2 = jnp.zeros((t, 1), F32)
    for g in range(GROUPS):
        dv = v_scr[:, g * gw:(g + 1) * gw] - mu
        s2 = s2 + _rowsum(dv * dv)
    return mu, lax.rsqrt(s2 * (1.0 / e) + LN_EPS)


def _layer_b_fwd_loss(x1, tgt, g1, scale, shift, gate, fg, wi, lng, lnb, wt, bsf, wo, name, ex=None):
    s, d = x1.shape
    e = wo.shape[0]
    gw = e // GROUPS
    t = min(s, 256)
    n_t = s // t

    def body(x1_ref, tgt_ref, g_ref, sc_ref, sh_ref, gate_ref, fg_ref, wi_ref, lng_ref, lnb_ref, wt_ref, bsf_ref, wo_ref,
             proj_ref, h_ref, dx2_ref, loss_ref, dfg_ref, dgate_ref, v_scr, y_scr):
        @pl.when(pl.program_id(0) == 0)
        def _():
            loss_ref[...] = jnp.zeros_like(loss_ref)
            dfg_ref[...] = jnp.zeros_like(dfg_ref)
            dgate_ref[...] = jnp.zeros_like(dgate_ref)
        h_ref[...] = _norm_modulate(x1_ref, g_ref, sc_ref, sh_ref)

        def project(c0):
            v = jnp.dot(h_ref[...], wi_ref[:, c0:c0 + gw], preferred_element_type=F32)
            proj_ref[:, c0:c0 + gw] = v.astype(BF16)
            return v

        mu, rs = _ln_stats(lambda g: project(e + g * gw), v_scr, None, t, e)
        for g in range(GROUPS):
            gs = slice(g * gw, (g + 1) * gw)
            vn = (((v_scr[:, gs] - mu) * rs) * lng_ref[:, gs] + lnb_ref[:, gs]).astype(BF16)
            u = _gelu(project(g * gw))[0]
            sz = _silu(project(2 * e + g * gw))[0]
            for ch in range(t // CHUNK):
                rows = slice(ch * CHUNK, (ch + 1) * CHUNK)
                mixed = jnp.dot(wt_ref[g], vn[rows], preferred_element_type=F32) + bsf_ref[:, gs]
                y_scr[rows, gs] = (sz[rows] * (u[rows] * mixed)).astype(BF16)
        br = jnp.dot(y_scr[...], wo_ref[...], preferred_element_type=F32)
        x2 = x1_ref[...] + gate_ref[...] * br
        r2 = lax.rsqrt(jnp.mean(x2 * x2, axis=-1, keepdims=True) + RMS_EPS)
        xn = x2 * r2
        diff = xn * fg_ref[...] - tgt_ref[...]
        loss_ref[...] += jnp.broadcast_to(0.5 * _colsum(jnp.mean(diff * diff, axis=-1, keepdims=True)), loss_ref.shape)
        dout = diff * (1.0 / d)
        dfg_ref[...] += _colsum(dout * xn)
        dxn = dout * fg_ref[...]
        dx2 = r2 * (dxn - xn * jnp.mean(dxn * xn, axis=-1, keepdims=True))
        dx2_ref[...] = dx2
        dgate_ref[...] += _colsum(dx2 * br)

    step = lambda k: (lambda: pl.program_id(0) == k)
    body = _carry(ex, body, 13, 6, step(0), [step(n_t // 3), step((2 * n_t) // 3)], step(n_t - 1))
    ex_args, ex_in, ex_shapes, ex_out, ex_sems = _carried(ex)
    tok = pl.BlockSpec((t, d), lambda i: (i, 0))
    vec = _full((1, d))
    out = pl.pallas_call(
        body, name=name, grid=(n_t,),
        out_shape=(jax.ShapeDtypeStruct((s, 3 * e), BF16), jax.ShapeDtypeStruct((s, d), BF16),
                   jax.ShapeDtypeStruct((s, d), F32), jax.ShapeDtypeStruct((8, 128), F32),
                   jax.ShapeDtypeStruct((1, d), F32), jax.ShapeDtypeStruct((1, d), F32), *ex_shapes),
        in_specs=[tok, tok, vec, vec, vec, vec, vec, _resident((d, 3 * e)), _full((1, e)), _full((1, e)),
                  _full((GROUPS, CHUNK, CHUNK)), _resident((CHUNK, e)), _resident((e, d)), *ex_in],
        out_specs=(pl.BlockSpec((t, 3 * e), lambda i: (i, 0)), tok, tok, _full((8, 128)), vec, vec, *ex_out),
        scratch_shapes=[pltpu.VMEM((t, e), F32), pltpu.VMEM((t, e), BF16), *ex_sems],
        compiler_params=_params(("arbitrary",)),
    )(x1, tgt, g1, scale, shift, gate, fg, wi, lng, lnb, wt, bsf, wo, *ex_args)
    return (*out[:6], out[6:])


def _norm_modulate_bwd(dh, x_ref, dres_ref, g_ref, sc_ref, dx_ref, dsh_ref, p_scr):
    xv = x_ref[...]
    r = lax.rsqrt(jnp.mean(xv * xv, axis=-1, keepdims=True) + RMS_EPS)
    xn = xv * r
    dsh_ref[...] += _colsum(dh)
    p_scr[...] += _colsum(dh * xn)
    dxn = dh * (g_ref[...] * (1.0 + sc_ref[...]))
    dx_ref[...] = r * (dxn - xn * jnp.mean(dxn * xn, axis=-1, keepdims=True)) + dres_ref[...]


def _layer_b_bwd(proj, dx2, x1, gate, g1, scale, lng, lnb, wt, wtt, bsf, wo, wi, name):
    s, e3 = proj.shape
    e = e3 // 3
    d = dx2.shape[1]
    gw = e // GROUPS
    t = min(s, 256)
    n_t = s // t

    def body(p_ref, dx_ref, x1_ref, gate_ref, g_ref, sc_ref, lng_ref, lnb_ref, wt_ref, wtt_ref, bsf_ref, wo_ref, wi_ref,
             dp_ref, y_ref, dx1_ref, dws_ref, dbs_ref, dlg_ref, dlb_ref, dsh_ref, dsc_ref, dg_ref,
             v_scr, dgv_scr, dbr_scr, dvn_scr, dbs_scr, p_scr, dy_scr, vn_scr, mixed_scr):
        @pl.when(pl.program_id(0) == 0)
        def _():
            dws_ref[...] = jnp.zeros_like(dws_ref)
            dlg_ref[...] = jnp.zeros_like(dlg_ref)
            dlb_ref[...] = jnp.zeros_like(dlb_ref)
            dsh_ref[...] = jnp.zeros_like(dsh_ref)
            dbs_scr[...] = jnp.zeros_like(dbs_scr)
            p_scr[...] = jnp.zeros_like(p_scr)
        dbr_scr[...] = (dx_ref[...] * gate_ref[...]).astype(BF16)
        mu, rs = _ln_stats(lambda g: p_ref[:, e + g * gw:e + (g + 1) * gw].astype(F32), v_scr, dgv_scr, t, e)
        tril = (lax.broadcasted_iota(jnp.int32, (CHUNK, CHUNK), 0) >= lax.broadcasted_iota(jnp.int32, (CHUNK, CHUNK), 1))
        c1 = jnp.zeros((t, 1), F32)
        c2 = jnp.zeros((t, 1), F32)
        span = 2
        kw = span * gw
        dh = jnp.zeros((t, d), F32)

        def through_w_in(c0):
            return lax.dot_general(dp_ref[:, c0:c0 + kw], wi_ref[:, c0:c0 + kw], NT_DIMS, preferred_element_type=F32)

        dy_scr[...] = lax.dot_general(dbr_scr[...], wo_ref[...], NT_DIMS, preferred_element_type=F32)
        for g in range(GROUPS):
            gs = slice(g * gw, (g + 1) * gw)
            vhat = (v_scr[:, gs] - mu) * rs
            v_scr[:, gs] = vhat
            vn = (vhat * lng_ref[:, gs] + lnb_ref[:, gs]).astype(BF16)
            vn_scr[:, gs] = vn
            for ch in range(t // CHUNK):
                rows = slice(ch * CHUNK, (ch + 1) * CHUNK)
                mixed_scr[rows, gs] = jnp.dot(wt_ref[g], vn[rows], preferred_element_type=F32) + bsf_ref[:, gs]
        for g in range(GROUPS):
            gs = slice(g * gw, (g + 1) * gw)
            vhat = v_scr[:, gs]
            lg = lng_ref[:, gs]
            for ch in range(t // CHUNK):
                rows = slice(ch * CHUNK, (ch + 1) * CHUNK)
                mixed = mixed_scr[rows, gs]
                u, dgu = _gelu(p_ref[rows, g * gw:(g + 1) * gw].astype(F32))
                sz, dsz = _silu(p_ref[rows, 2 * e + g * gw:2 * e + (g + 1) * gw].astype(F32))
                sgate = u * mixed
                y_ref[rows, gs] = (sz * sgate).astype(BF16)
                dy = dy_scr[rows, gs]
                dp_ref[rows, 2 * e + g * gw:2 * e + (g + 1) * gw] = (dy * sgate * dsz).astype(BF16)
                ds = dy * sz
                dp_ref[rows, gs] = (ds * mixed * dgu).astype(BF16)
                dm = ds * u
                dbs_scr[:, gs] += dm
                dmb = dm.astype(BF16)
                dws_ref[g] += jnp.where(tril, lax.dot_general(dmb, vn_scr[rows, gs], NT_DIMS, preferred_element_type=F32), 0.0)
                dvn_scr[rows, gs] = jnp.dot(wtt_ref[g], dmb, preferred_element_type=F32)
            dvn = dvn_scr[:, gs]
            dlb_ref[:, gs] += _colsum(dvn)
            dlg_ref[:, gs] += _colsum(dvn * vhat)
            dvh = dvn * lg
            c1 = c1 + _rowsum(dvh)
            c2 = c2 + _rowsum(dvh * vhat)
            if g % span == span - 1:
                dh = dh + through_w_in(g * gw + gw - kw) + through_w_in(2 * e + g * gw + gw - kw)
        c1 = c1 * (1.0 / e)
        c2 = c2 * (1.0 / e)
        for g in range(GROUPS):
            gs = slice(g * gw, (g + 1) * gw)
            dv = rs * (dvn_scr[:, gs] * lng_ref[:, gs] - c1 - v_scr[:, gs] * c2)
            dp_ref[:, e + g * gw:e + (g + 1) * gw] = (dv * dgv_scr[:, gs]).astype(BF16)
            if g % span == span - 1:
                dh = dh + through_w_in(e + g * gw + gw - kw)
        _norm_modulate_bwd(dh, x1_ref, dx_ref, g_ref, sc_ref, dx1_ref, dsh_ref, p_scr)

        @pl.when(pl.program_id(0) == n_t - 1)
        def _():
            lane = lax.broadcasted_iota(jnp.int32, (CHUNK, 128), 1)
            acc = jnp.zeros((CHUNK, 128), F32)
            for g in range(GROUPS):
                acc = acc + jnp.where(lane == g, _rowsum(dbs_scr[:, g * gw:(g + 1) * gw]), 0.0)
            dbs_ref[...] = acc
            dsc_ref[...] = p_scr[...] * g_ref[...]
            dg_ref[...] = p_scr[...] * (1.0 + sc_ref[...])

    tok = pl.BlockSpec((t, d), lambda i: (i, 0))
    vec, evec, ws = _full((1, d)), _full((1, e)), _full((GROUPS, CHUNK, CHUNK))
    vshape = jax.ShapeDtypeStruct((1, d), F32)
    return pl.pallas_call(
        body, name=name, grid=(n_t,),
        out_shape=(jax.ShapeDtypeStruct((s, e3), BF16), jax.ShapeDtypeStruct((s, e), BF16), jax.ShapeDtypeStruct((s, d), F32),
                   jax.ShapeDtypeStruct((GROUPS, CHUNK, CHUNK), F32), jax.ShapeDtypeStruct((CHUNK, 128), F32),
                   jax.ShapeDtypeStruct((1, e), F32), jax.ShapeDtypeStruct((1, e), F32), vshape, vshape, vshape),
        in_specs=[pl.BlockSpec((t, e3), lambda i: (i, 0)), tok, tok, vec, vec, vec, evec, evec, ws, ws,
                  _resident((CHUNK, e)), _resident((e, d)), _resident((d, e3))],
        out_specs=(pl.BlockSpec((t, e3), lambda i: (i, 0)), pl.BlockSpec((t, e), lambda i: (i, 0)), tok,
                   ws, _full((CHUNK, 128)), evec, evec, vec, vec, vec),
        scratch_shapes=[pltpu.VMEM((t, e), F32), pltpu.VMEM((t, e), BF16), pltpu.VMEM((t, d), BF16),
                        pltpu.VMEM((t, e), F32), pltpu.VMEM((CHUNK, e), F32), pltpu.VMEM((1, d), F32),
                        pltpu.VMEM((t, e), F32), pltpu.VMEM((t, e), BF16), pltpu.VMEM((t, e), F32)],
        compiler_params=_params(("arbitrary",)),
    )(proj, dx2, x1, gate, g1, scale, lng, lnb, wt, wtt, bsf, wo, wi)


def _conv_mixer_bwd(proj, dx1, br, tails, cw, cb, gate, wo, name, ex=None):
    s, e4 = proj.shape
    e = e4 // 4
    d = dx1.shape[1]
    t = min(s, 256)
    n_t = s // t
    cwid = min(e, 512)

    def body(p_ref, dx_ref, br_ref, tails_ref, cw_ref, cb_ref, gate_ref, wo_ref,
             dp_ref, y_ref, dgate_ref, dcb_ref, dcw_ref, dy_scr, head_scr):
        i = pl.program_id(0)

        @pl.when(i == 0)
        def _():
            dgate_ref[...] = jnp.zeros_like(dgate_ref)
            dcb_ref[...] = jnp.zeros_like(dcb_ref)
            dcw_ref[...] = jnp.zeros_like(dcw_ref)
            head_scr[...] = jnp.zeros_like(head_scr)
        dx = dx_ref[...]
        dgate_ref[...] += _colsum(dx * br_ref[...].astype(F32))
        dy_scr[...] = lax.dot_general((dx * gate_ref[...]).astype(BF16), wo_ref[...], NT_DIMS,
                                      preferred_element_type=F32)
        row = lax.broadcasted_iota(jnp.int32, (t, cwid), 0)
        has_prev = (i < n_t - 1).astype(F32)
        for c0 in range(0, e, cwid):
            sl = slice(c0, c0 + cwid)
            bg = p_ref[:, c0:c0 + cwid].astype(F32)
            cg = p_ref[:, e + c0:e + c0 + cwid].astype(F32)
            xin = p_ref[:, 2 * e + c0:2 * e + c0 + cwid].astype(F32)
            z = p_ref[:, 3 * e + c0:3 * e + c0 + cwid].astype(F32)
            cx = cg * xin
            p1, p2 = _conv_taps(cx, tails_ref[6:7, sl] * has_prev, tails_ref[7:8, sl] * has_prev, row)
            w0, w1, w2 = cw_ref[0:1, sl], cw_ref[1:2, sl], cw_ref[2:3, sl]
            conv = cb_ref[:, sl] + w2 * cx + w0 * p2 + w1 * p1
            sz, dsz = _silu(z)
            dy = dy_scr[:, sl]
            y_ref[:, sl] = (sz * bg * conv).astype(BF16)
            dp_ref[:, 3 * e + c0:3 * e + c0 + cwid] = (dy * bg * conv * dsz).astype(BF16)
            dp_ref[:, c0:c0 + cwid] = (dy * sz * conv).astype(BF16)
            dconv = dy * sz * bg
            dcb_ref[:, sl] += _colsum(dconv)
            dcw_ref[2:3, sl] += _colsum(dconv * cx)
            dcw_ref[1:2, sl] += _colsum(dconv * p1)
            dcw_ref[0:1, sl] += _colsum(dconv * p2)
            h0, h1 = head_scr[0:1, sl], head_scr[1:2, sl]
            n1 = jnp.where(row == t - 1, h0, pltpu.roll(dconv, t - 1, 0))
            n2 = jnp.where(row == t - 2, h0, jnp.where(row == t - 1, h1, pltpu.roll(dconv, t - 2, 0)))
            dcx = w2 * dconv + w1 * n1 + w0 * n2
            dp_ref[:, e + c0:e + c0 + cwid] = (dcx * xin).astype(BF16)
            dp_ref[:, 2 * e + c0:2 * e + c0 + cwid] = (dcx * cg).astype(BF16)
            head_scr[:, sl] = dconv[0:8, :]

    body = _carry(ex, body, 8, 5, lambda: pl.program_id(0) == 0, None, lambda: pl.program_id(0) == n_t - 1)
    ex_args, ex_in, ex_shapes, ex_out, ex_sems = _carried(ex)
    rev = lambda i: (n_t - 1 - i, 0)
    out = pl.pallas_call(
        body, name=name, grid=(n_t,),
        out_shape=(jax.ShapeDtypeStruct((s, e4), BF16), jax.ShapeDtypeStruct((s, e), BF16),
                   jax.ShapeDtypeStruct((1, d), F32), jax.ShapeDtypeStruct((1, e), F32), jax.ShapeDtypeStruct((8, e), F32),
                   *ex_shapes),
        in_specs=[pl.BlockSpec((t, e4), rev), pl.BlockSpec((t, d), rev), pl.BlockSpec((t, d), rev),
                  pl.BlockSpec((None, 8, e), lambda i: (jnp.maximum(n_t - 2 - i, 0), 0, 0)),
                  _full((3, e)), _full((1, e)), _full((1, d)), _full((e, d)), *ex_in],
        out_specs=(pl.BlockSpec((t, e4), rev), pl.BlockSpec((t, e), rev), _full((1, d)), _full((1, e)), _full((8, e)),
                   *ex_out),
        scratch_shapes=[pltpu.VMEM((t, e), F32), pltpu.VMEM((8, e), F32), *ex_sems],
        compiler_params=_params(("arbitrary",)),
    )(proj, dx1, br, tails, cw, cb, gate, wo, *ex_args)
    return (*out[:5], out[5:])


def _matmul_nt_norm_bwd(dproj, w, xin, dres, g, scale, name, ex=None):
    s, d = xin.shape
    n = w.shape[1]
    tm = min(s, 512)
    n_i = s // tm

    def body(dp_ref, w_ref, x_ref, dres_ref, g_ref, sc_ref, dx_ref, dsh_ref, dsc_ref, dg_ref, p_scr):
        i = pl.program_id(0)

        @pl.when(i == 0)
        def _():
            dsh_ref[...] = jnp.zeros_like(dsh_ref)
            p_scr[...] = jnp.zeros_like(p_scr)
        dh = lax.dot_general(dp_ref[...], w_ref[...], NT_DIMS, preferred_element_type=F32)
        _norm_modulate_bwd(dh, x_ref, dres_ref, g_ref, sc_ref, dx_ref, dsh_ref, p_scr)

        @pl.when(i == n_i - 1)
        def _():
            dsc_ref[...] = p_scr[...] * g_ref[...]
            dg_ref[...] = p_scr[...] * (1.0 + sc_ref[...])

    body = _carry(ex, body, 6, 4, lambda: pl.program_id(0) == 0, None, lambda: pl.program_id(0) == n_i - 1)
    ex_args, ex_in, ex_shapes, ex_out, ex_sems = _carried(ex)
    tok = pl.BlockSpec((tm, d), lambda i: (i, 0))
    vec = pl.BlockSpec((1, d), lambda i: (0, 0))
    vshape = jax.ShapeDtypeStruct((1, d), F32)
    out = pl.pallas_call(
        body, name=name, grid=(n_i,),
        out_shape=(jax.ShapeDtypeStruct((s, d), F32), vshape, vshape, vshape, *ex_shapes),
        in_specs=[pl.BlockSpec((tm, n), lambda i: (i, 0)), _resident((d, n)), tok, tok, vec, vec, *ex_in],
        out_specs=(tok, vec, vec, vec, *ex_out),
        scratch_shapes=[pltpu.VMEM((1, d), F32), *ex_sems],
        compiler_params=_params(("arbitrary",)),
    )(dproj, w, xin, dres, g, scale, *ex_args)
    return (*out[:4], out[4:])


def _matmul_tn(a, b, colscale, rows_split, name, ex=None, a_cols=None):
    s, m = a.shape
    a_blk = 0
    if a_cols is not None:
        a_blk, m = a_cols
    n = b.shape[1]
    n_j, tn = (1, n) if rows_split else (NDEV, n // NDEV)
    fixed = (4 + 4 + 2 * 2) * m * tn
    tk = s
    while fixed + 2 * tk * (2 * m + b.dtype.itemsize * tn) > TN_VMEM_BUDGET:
        tk //= 2
    n_k = s // tk

    def body(a_ref, b_ref, cs_ref, o_ref, acc):
        k = pl.program_id(1)
        part = lax.dot_general(a_ref[...], b_ref[...].astype(BF16), TN_DIMS, preferred_element_type=F32)
        if n_k == 1:
            o_ref[...] = (part * cs_ref[...]).astype(BF16)
            return

        @pl.when(k == 0)
        def _():
            acc[...] = part

        @pl.when((k > 0) & (k < n_k - 1))
        def _():
            acc[...] += part

        @pl.when(k == n_k - 1)
        def _():
            o_ref[...] = ((acc[...] + part) * cs_ref[...]).astype(BF16)

    at = lambda j, k: (pl.program_id(0) == j) & (pl.program_id(1) == k)
    body = _carry(ex, body, 3, 1, lambda: at(0, 0), None, lambda: at(n_j - 1, n_k - 1))
    ex_args, ex_in, ex_shapes, ex_out, ex_sems = _carried(ex)
    out = pl.pallas_call(
        body, name=name, grid=(n_j, n_k),
        out_shape=(jax.ShapeDtypeStruct((n_j, m, tn), BF16), *ex_shapes),
        in_specs=[pl.BlockSpec((tk, m), lambda j, k: (k, a_blk)), pl.BlockSpec((tk, tn), lambda j, k: (k, j)),
                  pl.BlockSpec((1, tn), lambda j, k: (0, j)), *ex_in],
        out_specs=(pl.BlockSpec((None, m, tn), lambda j, k: (j, 0, 0)), *ex_out),
        scratch_shapes=[pltpu.VMEM((m, tn), F32), *ex_sems],
        compiler_params=_params(("arbitrary", "arbitrary")),
    )(a, b, colscale, *ex_args)
    return (out[0].reshape(NDEV, m // NDEV, n) if rows_split else out[0]), out[1:]


def _adam_update(w, g, m, v):
    m = ADAM_B1 * m + (1.0 - ADAM_B1) * g
    v = ADAM_B2 * v + (1.0 - ADAM_B2) * (g * g)
    m_hat = m / (1.0 - ADAM_B1 ** ADAM_STEP)
    v_hat = v / (1.0 - ADAM_B2 ** ADAM_STEP)
    return -ADAM_LR * (m_hat / (jnp.sqrt(v_hat) + ADAM_EPS) + ADAM_WD * w), m, v


def _adamw_reduce(parts, w, m, v, name, ex=None):
    n_l = len(parts)
    n_p, _, c = parts[0].shape
    rows = [p.shape[1] for p in parts]
    r = sum(rows)
    tr = min(min(rows), 128 if ex is not None else 256)
    n_i = r // tr
    tiles = [r_l // tr for r_l in rows]
    first_tile = [sum(tiles[:l]) for l in range(n_l)]

    def body(*refs):
        p_refs, (w_ref, m_ref, v_ref, g_out, d_out, m_out, v_out) = refs[:n_l], refs[n_l:]
        g = None
        for l, p_ref in enumerate(p_refs):
            g_l = p_ref[0].astype(F32)
            for j in range(1, n_p):
                g_l = g_l + p_ref[j].astype(F32)
            g = g_l if g is None else jnp.where(pl.program_id(0) >= first_tile[l], g_l, g)
        g_out[...] = g
        d_out[...], m_out[...], v_out[...] = _adam_update(w_ref[...], g, m_ref[...], v_ref[...])

    step = lambda k: (lambda: pl.program_id(0) == k)
    body = _carry(ex, body, n_l + 3, 4, step(0), [step(n_i // 3), step((2 * n_i) // 3)], step(n_i - 1))
    ex_args, ex_in, ex_shapes, ex_out, ex_sems = _carried(ex)
    blk = pl.BlockSpec((tr, c), lambda i: (i, 0))
    p_specs = [pl.BlockSpec((n_p, tr, c), lambda i, l=l: (0, jnp.clip(i - first_tile[l], 0, tiles[l] - 1), 0))
               for l in range(n_l)]
    shp = jax.ShapeDtypeStruct((r, c), F32)
    out = pl.pallas_call(
        body, name=name, grid=(n_i,), out_shape=(shp, shp, shp, shp, *ex_shapes),
        in_specs=[*p_specs, blk, blk, blk, *ex_in],
        out_specs=(blk, blk, blk, blk, *ex_out), scratch_shapes=ex_sems,
        compiler_params=_params(("arbitrary",)),
    )(*parts, w, m, v, *ex_args)
    return (*out[:4], out[4:])


def _adamw_small(gs, ws, ms, vs, name):
    n = len(gs)

    def body(*refs):
        ins, outs = refs[:4 * n], refs[4 * n:]
        for a in range(n):
            d, m, v = _adam_update(ins[n + a][...], ins[a][...], ins[2 * n + a][...], ins[3 * n + a][...])
            outs[a][...], outs[n + a][...], outs[2 * n + a][...] = d, m, v

    shapes = tuple(jax.ShapeDtypeStruct(w.shape, F32) for w in ws) * 3
    out = pl.pallas_call(
        body, name=name, out_shape=shapes,
        in_specs=[_vmem()] * (4 * n), out_specs=tuple([_vmem()] * (3 * n)),
        compiler_params=pltpu.CompilerParams(vmem_limit_bytes=VMEM_LIMIT),
    )(*gs, *ws, *ms, *vs)
    return out[:n], out[n:2 * n], out[2 * n:]


def _sum_devices(packed, name):
    _, r, wdt = packed.shape

    def body(p_ref, o_ref):
        acc = p_ref[0]
        for j in range(1, NDEV):
            acc = acc + p_ref[j]
        o_ref[...] = acc

    return pl.pallas_call(
        body, name=name, out_shape=jax.ShapeDtypeStruct((r, wdt), F32),
        in_specs=[_vmem()], out_specs=_vmem(),
        compiler_params=pltpu.CompilerParams(vmem_limit_bytes=VMEM_LIMIT),
    )(packed)


def _mod_w_grad(c_t, dmod, name):
    n_layers, _, w3 = dmod.shape
    d = c_t.shape[0]

    def body(c_ref, dm_ref, o_ref):
        for i in range(n_layers):
            acc = c_ref[:, 0:1] * dm_ref[i, 0:1, :]
            for b in range(1, NDEV):
                acc = acc + c_ref[:, b:b + 1] * dm_ref[i, b:b + 1, :]
            o_ref[i] = acc

    return pl.pallas_call(
        body, name=name, out_shape=jax.ShapeDtypeStruct((n_layers, d, w3), F32),
        in_specs=[_vmem(), _vmem()], out_specs=_vmem(),
        compiler_params=pltpu.CompilerParams(vmem_limit_bytes=VMEM_LIMIT),
    )(c_t, dmod)


def _mask_transpose_ws(w_s, name):
    def body(w_ref, wt_ref, wtt_ref):
        tril = (lax.broadcasted_iota(jnp.int32, (CHUNK, CHUNK), 0) >= lax.broadcasted_iota(jnp.int32, (CHUNK, CHUNK), 1))
        for g in range(GROUPS):
            wm = jnp.where(tril, w_ref[g], 0.0)
            wt_ref[g] = wm.astype(BF16)
            wtt_ref[g] = wm.T.astype(BF16)

    shp = jax.ShapeDtypeStruct(w_s.shape, BF16)
    return pl.pallas_call(
        body, name=name, out_shape=(shp, shp), in_specs=[_vmem()], out_specs=(_vmem(), _vmem()),
    )(w_s)


def _pack(pieces):
    flat = jnp.concatenate([p.reshape(-1) for p in pieces])
    rows = -(-flat.shape[0] // (8 * PACK_W)) * 8
    return jnp.pad(flat, (0, rows * PACK_W - flat.shape[0])).reshape(rows, PACK_W)


def _unpack(flat, shapes):
    out, off = [], 0
    for shp in shapes:
        size = 1
        for dim in shp:
            size *= dim
        out.append(flat[off:off + size].reshape(shp))
        off += size
    return out


def kernel(x, c, mod_w, mod_b, norm_g, a_w_in, a_conv_w, a_conv_b, a_w_out, b_w_in, b_ln_g, b_ln_b, b_w_s, b_b_s, b_w_out, final_g, loss_target, m_mod_w, m_mod_b, m_norm_g, m_a_w_in, m_a_conv_w, m_a_conv_b, m_a_w_out, m_b_w_in, m_b_ln_g, m_b_ln_b, m_b_w_s, m_b_b_s, m_b_w_out, m_final_g, v_mod_w, v_mod_b, v_norm_g, v_a_w_in, v_a_conv_w, v_a_conv_b, v_a_w_out, v_b_w_in, v_b_ln_g, v_b_ln_b, v_b_w_s, v_b_b_s, v_b_w_out, v_final_g):
    s, d = x.shape[1], x.shape[2]
    es = a_w_out.shape[1]
    e = NDEV * es
    w3 = mod_w.shape[2]
    me = _index(_pos())
    x0 = x.reshape(s, d)
    tgt = loss_target.reshape(s, d)

    small = jnp.concatenate([a_conv_w[0], b_ln_g, b_ln_b, jnp.zeros((3, es), F32)], axis=0)
    gather_a = _gather_exchange([a_w_in[0].astype(BF16), a_w_out[0].astype(BF16), small], [True, False, True])
    gather_b = _gather_exchange([b_w_in[0].astype(BF16), b_w_out[0].astype(BF16)], [True, False])
    mod, c_all, (wa, woa, small_all) = _mod_vectors(c, mod_w, mod_b, gather_a)
    conv_w, ln_g, ln_b = small_all[0:3], small_all[3:4], small_all[4:5]
    bsf = jnp.repeat(b_b_s[0].T, e // GROUPS, axis=1)
    wt, wtt = _mask_transpose_ws(b_w_s[0], "mask_w_s")
    shift0, scale0, gate0 = mod[0:1, 0:d], mod[0:1, d:2 * d], mod[0:1, 2 * d:]
    shift1, scale1, gate1 = mod[1:2, 0:d], mod[1:2, d:2 * d], mod[1:2, 2 * d:]
    g0, g1, fg = norm_g[0:1], norm_g[1:2], final_g.reshape(1, d)

    proj_a, h0, x1, br_a, tails, (wb, wob) = _layer_a_fwd(
        x0, g0, scale0, shift0, gate0, wa, conv_w, a_conv_b, woa, "a_fwd", gather_b)
    proj_b, h1, dx2, loss_acc, dfg, dgate1, _ = _layer_b_fwd_loss(
        x1, tgt, g1, scale1, shift1, gate1, fg, wb, ln_g, ln_b, wt, bsf, wob, "b_fwd_loss")

    dproj_b, y_b, dx1, dws, dbs, dlg, dlb, dshift1, dscale1, dg1 = _layer_b_bwd(
        proj_b, dx2, x1, gate1, g1, scale1, ln_g, ln_b, wt, wtt, bsf, wob, wb, "b_bwd")
    gs_b_out, _ = _matmul_tn(y_b, dx2, gate1, True, "b_w_out_grad")
    gs_b_in, _ = _matmul_tn(h1, dproj_b, jnp.ones((1, dproj_b.shape[1]), F32), False, "b_w_in_grad")
    dproj_a, y_a, dgate0, dcb, dcw, (gr_b_in, gr_b_out) = _conv_mixer_bwd(
        proj_a, dx1, br_a, tails, conv_w, a_conv_b, gate0, woa, "a_mixer_bwd", _scatter_exchange([gs_b_in, gs_b_out]))
    gs_a_out, _ = _matmul_tn(y_a, dx1, gate0, True, "a_w_out_grad")
    ones_n = jnp.ones((1, dproj_a.shape[1]), F32)
    gs_last, (gr_a_out,) = _matmul_tn(h0, dproj_a, ones_n, False, "a_w_in_grad_last", _scatter_exchange([gs_a_out]),
                                      a_cols=(3, d // 4))
    gs_rest, (gr_last,) = _matmul_tn(h0, dproj_a, ones_n, False, "a_w_in_grad_rest", _scatter_exchange([gs_last]),
                                     a_cols=(0, 3 * d // 4))
    dx0, dshift0, dscale0, dg0, (gr_rest,) = _matmul_nt_norm_bwd(dproj_a, wa, x0, dx1, g0, scale0, "a_in_bwd",
                                                                 _scatter_exchange([gs_rest]))

    def big(parts, w, m, v, name, ex=None):
        shp = w.shape
        r2 = lambda t_: t_.reshape(-1, shp[-1])
        g, dl, nm, nv, ex_out = _adamw_reduce(parts, r2(w), r2(m), r2(v), name, ex)
        return tuple(t_.reshape(shp) for t_ in (g, dl, nm, nv)), ex_out

    res = {}
    res["b_w_in"], _ = big([gr_b_in], b_w_in, m_b_w_in, v_b_w_in, "adamw_b_w_in")
    gr_a_in = [gr_rest, gr_last]

    pieces = [dshift0, dscale0, dgate0, dshift1, dscale1, dgate1, dg0, dg1, dcb, dcw[0:3], dlg, dlb, dfg,
              dbs[:, 0:GROUPS].T, dws, loss_acc[0:1, 0:1]]
    shapes = [p.shape for p in pieces]
    res["a_w_in"], (packed_all,) = big(gr_a_in, a_w_in, m_a_w_in, v_a_w_in, "adamw_a_w_in",
                                       _gather_exchange([_pack(pieces)], [False]))
    packed_all = packed_all.reshape(NDEV, -1, PACK_W)
    total = _sum_devices(packed_all, "sum_small_grads").reshape(-1)
    (t_sh0, t_sc0, t_ga0, t_sh1, t_sc1, t_ga1, t_g0, t_g1, t_cb, t_cw, t_lg, t_lb, t_fg, t_bs, t_ws, t_loss) = _unpack(
        total, shapes)
    loss = t_loss.reshape(())
    grad_mod_b = jnp.concatenate([jnp.concatenate([t_sh0, t_sc0, t_ga0], axis=1),
                                  jnp.concatenate([t_sh1, t_sc1, t_ga1], axis=1)], axis=0)
    grad_norm_g = jnp.concatenate([t_g0, t_g1], axis=0)
    dmod_all = packed_all.reshape(NDEV, -1)[:, 0:6 * d].reshape(NDEV, 2, 3 * d).transpose(1, 0, 2)
    dmod_mine = lax.dynamic_slice_in_dim(dmod_all, me * w3, w3, axis=2)
    grad_mod_w = _mod_w_grad(c_all.T, dmod_mine, "mod_w_grad")
    grad_a_conv_w = lax.dynamic_slice_in_dim(t_cw, me * es, es, axis=1)
    grad_b_ln_g = lax.dynamic_slice_in_dim(t_lg, me * es, es, axis=1)
    grad_b_ln_b = lax.dynamic_slice_in_dim(t_lb, me * es, es, axis=1)

    res["mod_w"], _ = big([grad_mod_w.reshape(1, -1, w3)], mod_w, m_mod_w, v_mod_w, "adamw_mod_w")
    res["a_w_out"], _ = big([gr_a_out], a_w_out, m_a_w_out, v_a_w_out, "adamw_a_w_out")
    res["b_w_out"], _ = big([gr_b_out], b_w_out, m_b_w_out, v_b_w_out, "adamw_b_w_out")

    small_names = ["mod_b", "norm_g", "a_conv_w", "a_conv_b", "b_ln_g", "b_ln_b", "b_w_s", "b_b_s", "final_g"]
    small_g = [grad_mod_b, grad_norm_g, grad_a_conv_w, t_cb, grad_b_ln_g, grad_b_ln_b, t_ws, t_bs, t_fg]
    small_w = [mod_b, norm_g, a_conv_w, a_conv_b, b_ln_g, b_ln_b, b_w_s, b_b_s, final_g]
    small_m = [m_mod_b, m_norm_g, m_a_conv_w, m_a_conv_b, m_b_ln_g, m_b_ln_b, m_b_w_s, m_b_b_s, m_final_g]
    small_v = [v_mod_b, v_norm_g, v_a_conv_w, v_a_conv_b, v_b_ln_g, v_b_ln_b, v_b_w_s, v_b_b_s, v_final_g]
    as2d = lambda t_: t_.reshape(-1, t_.shape[-1])
    dls, nms, nvs = _adamw_small([as2d(t_) for t_ in small_g], [as2d(t_) for t_ in small_w],
                                 [as2d(t_) for t_ in small_m], [as2d(t_) for t_ in small_v], "adamw_small")
    for a, nme in enumerate(small_names):
        shp = small_w[a].shape
        res[nme] = (small_g[a].reshape(shp), dls[a].reshape(shp), nms[a].reshape(shp), nvs[a].reshape(shp))

    order = ["mod_w", "mod_b", "norm_g", "a_w_in", "a_conv_w", "a_conv_b", "a_w_out", "b_w_in", "b_ln_g", "b_ln_b",
             "b_w_s", "b_b_s", "b_w_out", "final_g"]
    return (loss, dx0.reshape(x.shape), *[res[k][0] for k in order], *[res[k][1] for k in order],
            *[res[k][2] for k in order], *[res[k][3] for k in order])
```

```python
import functools

import jax
import jax.numpy as jnp
from jax import lax
from jax.experimental import pallas as pl
from jax.experimental.pallas import tpu as pltpu

NDEV = 8
CHUNK = 128
GROUPS = 8
RMS_EPS = 1e-6
LN_EPS = 1e-5
ADAM_LR, ADAM_B1, ADAM_B2, ADAM_EPS, ADAM_WD, ADAM_STEP = 0.001, 0.9, 0.999, 1e-08, 0.01, 10
V7X_VMEM_BYTES = 64 * 1024 * 1024
VMEM_LIMIT = V7X_VMEM_BYTES - 8 * 1024 * 1024
TN_VMEM_BUDGET = 46 * 1024 * 1024
PACK_W = 1024
F32, BF16 = jnp.float32, jnp.bfloat16
MESH = pl.DeviceIdType.MESH
RSQRT2 = 0.7071067811865476
INV_SQRT_2PI = 0.3989422804014327
NT_DIMS = (((1,), (1,)), ((), ()))
TN_DIMS = (((0,), (0,)), ((), ()))


def _params(sem=None):
    return pltpu.CompilerParams(dimension_semantics=sem, vmem_limit_bytes=VMEM_LIMIT)


def _vmem():
    return pl.BlockSpec(memory_space=pltpu.VMEM)


def _hbm():
    return pl.BlockSpec(memory_space=pltpu.HBM)


def _full(shape):
    return pl.BlockSpec(shape, lambda *_: (0,) * len(shape))


def _pos():
    return lax.axis_index("x"), lax.axis_index("y"), lax.axis_index("c")


def _index(p):
    return 4 * p[0] + 2 * p[1] + p[2]


def _peer(k):
    x, y, c = _pos()
    return ((1 - x) if (k >> 2) & 1 else x, (1 - y) if (k >> 1) & 1 else y, (1 - c) if k & 1 else c)


def _silu(z):
    sg = jax.nn.sigmoid(z)
    return z * sg, sg * (1.0 + z * (1.0 - sg))


def _gelu(v):
    phi = 0.5 * (1.0 + lax.erf(v * RSQRT2))
    return v * phi, phi + v * (jnp.exp(-0.5 * v * v) * INV_SQRT_2PI)


def _colsum(v):
    return jnp.sum(v, axis=0, keepdims=True)


def _rowsum(v):
    return jnp.sum(v, axis=-1, keepdims=True)


def _gather_all_vmem(slab_ref, send_sems, recv_sems, base):
    me = _index(_pos())
    sends = []
    for k in range(1, NDEV):
        cp = pltpu.make_async_remote_copy(
            src_ref=slab_ref.at[me], dst_ref=slab_ref.at[me],
            send_sem=send_sems.at[base + k - 1], recv_sem=recv_sems.at[base + k - 1],
            device_id=_peer(k), device_id_type=MESH)
        cp.start()
        sends.append(cp)
    for k in range(1, NDEV):
        src = _index(_peer(k))
        pltpu.make_async_remote_copy(
            src_ref=slab_ref.at[src], dst_ref=slab_ref.at[src],
            send_sem=send_sems.at[base + k - 1], recv_sem=recv_sems.at[base + k - 1],
            device_id=_peer(k), device_id_type=MESH).wait_recv()
    for cp in sends:
        cp.wait_send()


def _mod_vectors(c, mod_w, mod_b, ex):
    n_layers, d, w3 = mod_w.shape
    r_in, r_out = len(ex.arrays), len(ex.out_shapes)

    def body(*refs):
        c_ref, mw_ref, mb_ref = refs[:3]
        ex_ins = refs[3:3 + r_in]
        mod_ref, call_ref = refs[3 + r_in:5 + r_in]
        ex_outs = refs[5 + r_in:5 + r_in + r_out]
        cslab, pslab, send_sems, recv_sems = refs[5 + r_in + r_out:9 + r_in + r_out]
        ex_sems = refs[9 + r_in + r_out:]
        ex.start(ex_ins, ex_outs, ex_sems)
        me = _index(_pos())
        cv = c_ref[...]
        cslab[me] = jnp.broadcast_to(cv * jax.nn.sigmoid(cv), (8, d))
        _gather_all_vmem(cslab, send_sems, recv_sems, 0)
        c_all = jnp.concatenate([cslab[k, 0:1, :] for k in range(NDEV)], axis=0)
        call_ref[...] = c_all
        for i in range(n_layers):
            pslab[me, i * NDEV:(i + 1) * NDEV, :] = jnp.dot(
                c_all, mw_ref[i], preferred_element_type=F32, precision=lax.Precision.HIGHEST)
        _gather_all_vmem(pslab, send_sems, recv_sems, NDEV - 1)
        for i in range(n_layers):
            for k in range(NDEV):
                mod_ref[i:i + 1, k * w3:(k + 1) * w3] = (
                    pslab[k, pl.ds(i * NDEV + me, 1), :] + mb_ref[i:i + 1, k * w3:(k + 1) * w3])
        for passing_on in ex.middles:
            passing_on(ex_ins, ex_outs, ex_sems)
        ex.finish(ex_ins, ex_outs, ex_sems)

    out = pl.pallas_call(
        body, name="mod_vectors",
        out_shape=(jax.ShapeDtypeStruct((n_layers, 3 * d), F32), jax.ShapeDtypeStruct((NDEV, d), F32), *ex.out_shapes),
        in_specs=[_vmem(), _vmem(), _vmem()] + [_hbm()] * r_in, out_specs=(_vmem(), _vmem(), *([_hbm()] * r_out)),
        scratch_shapes=[pltpu.VMEM((NDEV, 8, d), F32), pltpu.VMEM((NDEV, n_layers * NDEV, w3), F32),
                        pltpu.SemaphoreType.DMA((2 * (NDEV - 1),)), pltpu.SemaphoreType.DMA((2 * (NDEV - 1),)), *ex.sems],
        compiler_params=pltpu.CompilerParams(vmem_limit_bytes=VMEM_LIMIT),
    )(c, mod_w, mod_b, *ex.arrays)
    return out[0], out[1], out[2:]


class _Exchange:
    def __init__(self, arrays, out_shapes, sems, start, middles, finish):
        self.arrays, self.out_shapes, self.sems = list(arrays), list(out_shapes), list(sems)
        self.start, self.middles, self.finish = start, list(middles), finish


def _gather_exchange(shards, by_cols):
    n = len(shards)
    shapes = [sh.shape for sh in shards]

    def tools(ins, outs, sems):
        send_sems, recv_sems, local_sems = sems
        x, y, c = _pos()
        chips = [(1 - x, y), (x, 1 - y), (1 - x, 1 - y)]
        south = c == 0
        relayed = (jnp.where(south, 1 - x, x), jnp.where(south, y, 1 - y), c)
        relay_to = (jnp.where(south, x, 1 - x), jnp.where(south, 1 - y, y), c)

        def place(a, block):
            r, cc = shapes[a]
            if by_cols[a]:
                return outs[a].at[:, pl.ds(_index(block) * cc, cc)]
            return outs[a].at[pl.ds(_index(block) * r, r), :]

        def copy(a, k, block, to, src=None):
            dst = place(a, block)
            return pltpu.make_async_remote_copy(
                src_ref=dst if src is None else src, dst_ref=dst,
                send_sem=send_sems.at[a * 7 + k], recv_sem=recv_sems.at[a * 7 + k],
                device_id=to, device_id_type=MESH)

        mine = [pltpu.make_async_copy(ins[a], place(a, (x, y, c)), local_sems.at[a]) for a in range(n)]
        first = []
        for a in range(n):
            first.append(copy(a, 0, (x, y, c), (x, y, 1 - c), src=ins[a]))
            first += [copy(a, 1 + j, (x, y, c), (*chip, c), src=ins[a]) for j, chip in enumerate(chips[:2])]
        relays = [copy(a, 3, relayed, relay_to) for a in range(n)]
        passed = [copy(a, 4 + j, (*chip, c), (x, y, 1 - c)) for j, chip in enumerate(chips) for a in range(n)]
        return (x, y, c), chips, copy, mine, first, relays, passed

    def start(ins, outs, sems):
        _, _, _, mine, first, _, _ = tools(ins, outs, sems)
        for cp in mine + first:
            cp.start()

    def pass_neighbours(ins, outs, sems):
        (x, y, c), chips, copy, _, _, relays, passed = tools(ins, outs, sems)
        for j, chip in enumerate(chips[:2]):
            for a in range(n):
                copy(a, 1 + j, (*chip, c), (x, y, c)).wait_recv()
                passed[j * n + a].start()
        for cp in relays:
            cp.start()

    def pass_diagonal(ins, outs, sems):
        (x, y, c), chips, copy, _, _, _, passed = tools(ins, outs, sems)
        for a in range(n):
            copy(a, 3, (*chips[2], c), (x, y, c)).wait_recv()
            passed[2 * n + a].start()

    def finish(ins, outs, sems):
        (x, y, c), chips, copy, mine, first, relays, passed = tools(ins, outs, sems)
        for a in range(n):
            copy(a, 0, (x, y, 1 - c), (x, y, c)).wait_recv()
        for j, chip in enumerate(chips):
            for a in range(n):
                copy(a, 4 + j, (*chip, 1 - c), (x, y, c)).wait_recv()
        for cp in first + relays + passed:
            cp.wait_send()
        for cp in mine:
            cp.wait()

    out_shapes = [jax.ShapeDtypeStruct((r, NDEV * cc) if bc else (NDEV * r, cc), sh.dtype)
                  for (r, cc), bc, sh in zip(shapes, by_cols, shards)]
    sems = [pltpu.SemaphoreType.DMA((7 * n,)), pltpu.SemaphoreType.DMA((7 * n,)), pltpu.SemaphoreType.DMA((n,))]
    return _Exchange(shards, out_shapes, sems, start, [pass_neighbours, pass_diagonal], finish)


def _scatter_exchange(parts):
    n = len(parts)

    def tools(ins, outs, sems):
        send_sems, recv_sems, local_sems = sems
        me = _index(_pos())
        mine = [pltpu.make_async_copy(ins[a].at[me], outs[a].at[me], local_sems.at[a]) for a in range(n)]
        sends, arrivals = [], []
        for k in range(1, NDEV):
            peer = _peer(k)
            for a in range(n):
                pair = dict(send_sem=send_sems.at[a * 7 + k - 1], recv_sem=recv_sems.at[a * 7 + k - 1],
                            device_id=peer, device_id_type=MESH)
                sends.append(pltpu.make_async_remote_copy(src_ref=ins[a].at[_index(peer)], dst_ref=outs[a].at[me], **pair))
                slot = outs[a].at[_index(peer)]
                arrivals.append(pltpu.make_async_remote_copy(src_ref=slot, dst_ref=slot, **pair))
        return mine, sends, arrivals

    def start(ins, outs, sems):
        mine, sends, _ = tools(ins, outs, sems)
        for cp in mine + sends:
            cp.start()

    def finish(ins, outs, sems):
        mine, sends, arrivals = tools(ins, outs, sems)
        for cp in arrivals:
            cp.wait_recv()
        for cp in sends:
            cp.wait_send()
        for cp in mine:
            cp.wait()

    out_shapes = [jax.ShapeDtypeStruct(p.shape, p.dtype) for p in parts]
    sems = [pltpu.SemaphoreType.DMA((7 * n,)), pltpu.SemaphoreType.DMA((7 * n,)), pltpu.SemaphoreType.DMA((n,))]
    return _Exchange(parts, out_shapes, sems, start, [], finish)


def _carry(ex, body, n_in, n_out, first, middle, last):
    if ex is None:
        return body
    r_in, r_out = len(ex.arrays), len(ex.out_shapes)

    def wrapped(*refs):
        ins, rins = refs[:n_in], refs[n_in:n_in + r_in]
        outs = refs[n_in + r_in:n_in + r_in + n_out]
        routs = refs[n_in + r_in + n_out:n_in + r_in + n_out + r_out]
        rest = refs[n_in + r_in + n_out + r_out:]
        scratch, sems = rest[:len(rest) - len(ex.sems)], rest[len(rest) - len(ex.sems):]

        @pl.when(first())
        def _():
            ex.start(rins, routs, sems)

        for passing_on, at_step in zip(ex.middles, middle or []):
            pl.when(at_step())(functools.partial(passing_on, rins, routs, sems))

        body(*ins, *outs, *scratch)

        @pl.when(last())
        def _():
            ex.finish(rins, routs, sems)

    return wrapped


def _carried(ex):
    if ex is None:
        return [], [], [], [], []
    return ex.arrays, [_hbm()] * len(ex.arrays), ex.out_shapes, [_hbm()] * len(ex.out_shapes), ex.sems


def _resident(shape):
    return pl.BlockSpec(shape, lambda *_: (0,) * len(shape), pipeline_mode=pl.Buffered(1))


def _norm_modulate(x_ref, g_ref, sc_ref, sh_ref):
    xv = x_ref[...]
    r = lax.rsqrt(jnp.mean(xv * xv, axis=-1, keepdims=True) + RMS_EPS)
    return ((xv * r) * g_ref[...] * (1.0 + sc_ref[...]) + sh_ref[...]).astype(BF16)


def _conv_taps(cx, t6, t7, row):
    p1 = jnp.where(row == 0, t7, pltpu.roll(cx, 1, 0))
    p2 = jnp.where(row == 0, t6, jnp.where(row == 1, t7, pltpu.roll(cx, 2, 0)))
    return p1, p2


def _layer_a_fwd(x, g, scale, shift, gate, wi, cw, cb, wo, name, ex=None):
    s, d = x.shape
    e = wo.shape[0]
    t = min(s, 256)
    n_t = s // t
    cwid = min(e, 512)

    def body(x_ref, g_ref, sc_ref, sh_ref, gate_ref, wi_ref, cw_ref, cb_ref, wo_ref,
             proj_ref, h_ref, x1_ref, br_ref, tails_ref, y_scr, tail_scr):
        @pl.when(pl.program_id(0) == 0)
        def _():
            tail_scr[...] = jnp.zeros_like(tail_scr)
        h_ref[...] = _norm_modulate(x_ref, g_ref, sc_ref, sh_ref)
        row = lax.broadcasted_iota(jnp.int32, (t, cwid), 0)

        def project(c0):
            v = jnp.dot(h_ref[...], wi_ref[:, c0:c0 + cwid], preferred_element_type=F32)
            proj_ref[:, c0:c0 + cwid] = v.astype(BF16)
            return v

        for c0 in range(0, e, cwid):
            sl = slice(c0, c0 + cwid)
            bg, z = project(c0), project(3 * e + c0)
            cx = project(e + c0) * project(2 * e + c0)
            p1, p2 = _conv_taps(cx, tail_scr[6:7, sl], tail_scr[7:8, sl], row)
            conv = cb_ref[:, sl] + cw_ref[2:3, sl] * cx + cw_ref[0:1, sl] * p2 + cw_ref[1:2, sl] * p1
            y_scr[:, sl] = (_silu(z)[0] * bg * conv).astype(BF16)
            tail_scr[:, sl] = cx[t - 8:t, :]
        tails_ref[...] = tail_scr[...]
        br = jnp.dot(y_scr[...], wo_ref[...], preferred_element_type=F32)
        x1_ref[...] = x_ref[...] + gate_ref[...] * br
        br_ref[...] = br.astype(BF16)

    step = lambda k: (lambda: pl.program_id(0) == k)
    body = _carry(ex, body, 9, 5, step(0), [step(n_t // 3), step((2 * n_t) // 3)], step(n_t - 1))
    ex_args, ex_in, ex_shapes, ex_out, ex_sems = _carried(ex)
    tok = pl.BlockSpec((t, d), lambda i: (i, 0))
    out = pl.pallas_call(
        body, name=name, grid=(n_t,),
        out_shape=(jax.ShapeDtypeStruct((s, 4 * e), BF16), jax.ShapeDtypeStruct((s, d), BF16),
                   jax.ShapeDtypeStruct((s, d), F32), jax.ShapeDtypeStruct((s, d), BF16),
                   jax.ShapeDtypeStruct((n_t, 8, e), F32), *ex_shapes),
        in_specs=[tok, _full((1, d)), _full((1, d)), _full((1, d)), _full((1, d)), _resident((d, 4 * e)),
                  _full((3, e)), _full((1, e)), _resident((e, d)), *ex_in],
        out_specs=(pl.BlockSpec((t, 4 * e), lambda i: (i, 0)), tok, tok, tok,
                   pl.BlockSpec((None, 8, e), lambda i: (i, 0, 0)), *ex_out),
        scratch_shapes=[pltpu.VMEM((t, e), BF16), pltpu.VMEM((8, e), F32), *ex_sems],
        compiler_params=_params(("arbitrary",)),
    )(x, g, scale, shift, gate, wi, cw, cb, wo, *ex_args)
    return (*out[:5], out[5:])


def _ln_stats(v_pre, v_scr, dgv_scr, t, e):
    gw = e // GROUPS
    s1 = jnp.zeros((t, 1), F32)
    for g in range(GROUPS):
        v, dgv = _gelu(v_pre(g))
        v_scr[:, g * gw:(g + 1) * gw] = v
        if dgv_scr is not None:
            dgv_scr[:, g * gw:(g + 1) * gw] = dgv.astype(dgv_scr.dtype)
        s1 = s1 + _rowsum(v)
    mu = s1 * (1.0 / e)
    s2 = jnp.zeros((t, 1), F32)
    for g in range(GROUPS):
        dv = v_scr[:, g * gw:(g + 1) * gw] - mu
        s2 = s2 + _rowsum(dv * dv)
    return mu, lax.rsqrt(s2 * (1.0 / e) + LN_EPS)


def _layer_b_fwd_loss(x1, tgt, g1, scale, shift, gate, fg, wi, lng, lnb, wt, bsf, wo, name, ex=None):
    s, d = x1.shape
    e = wo.shape[0]
    gw = e // GROUPS
    t = min(s, 256)
    n_t = s // t

    def body(x1_ref, tgt_ref, g_ref, sc_ref, sh_ref, gate_ref, fg_ref, wi_ref, lng_ref, lnb_ref, wt_ref, bsf_ref, wo_ref,
             proj_ref, h_ref, dx2_ref, loss_ref, dfg_ref, dgate_ref, v_scr, y_scr):
        @pl.when(pl.program_id(0) == 0)
        def _():
            loss_ref[...] = jnp.zeros_like(loss_ref)
            dfg_ref[...] = jnp.zeros_like(dfg_ref)
            dgate_ref[...] = jnp.zeros_like(dgate_ref)
        h_ref[...] = _norm_modulate(x1_ref, g_ref, sc_ref, sh_ref)

        def project(c0):
            v = jnp.dot(h_ref[...], wi_ref[:, c0:c0 + gw], preferred_element_type=F32)
            proj_ref[:, c0:c0 + gw] = v.astype(BF16)
            return v

        mu, rs = _ln_stats(lambda g: project(e + g * gw), v_scr, None, t, e)
        for g in range(GROUPS):
            gs = slice(g * gw, (g + 1) * gw)
            vn = (((v_scr[:, gs] - mu) * rs) * lng_ref[:, gs] + lnb_ref[:, gs]).astype(BF16)
            u = _gelu(project(g * gw))[0]
            sz = _silu(project(2 * e + g * gw))[0]
            for ch in range(t // CHUNK):
                rows = slice(ch * CHUNK, (ch + 1) * CHUNK)
                mixed = jnp.dot(wt_ref[g], vn[rows], preferred_element_type=F32) + bsf_ref[:, gs]
                y_scr[rows, gs] = (sz[rows] * (u[rows] * mixed)).astype(BF16)
        br = jnp.dot(y_scr[...], wo_ref[...], preferred_element_type=F32)
        x2 = x1_ref[...] + gate_ref[...] * br
        r2 = lax.rsqrt(jnp.mean(x2 * x2, axis=-1, keepdims=True) + RMS_EPS)
        xn = x2 * r2
        diff = xn * fg_ref[...] - tgt_ref[...]
        loss_ref[...] += jnp.broadcast_to(0.5 * _colsum(jnp.mean(diff * diff, axis=-1, keepdims=True)), loss_ref.shape)
        dout = diff * (1.0 / d)
        dfg_ref[...] += _colsum(dout * xn)
        dxn = dout * fg_ref[...]
        dx2 = r2 * (dxn - xn * jnp.mean(dxn * xn, axis=-1, keepdims=True))
        dx2_ref[...] = dx2
        dgate_ref[...] += _colsum(dx2 * br)

    step = lambda k: (lambda: pl.program_id(0) == k)
    body = _carry(ex, body, 13, 6, step(0), [step(n_t // 3), step((2 * n_t) // 3)], step(n_t - 1))
    ex_args, ex_in, ex_shapes, ex_out, ex_sems = _carried(ex)
    tok = pl.BlockSpec((t, d), lambda i: (i, 0))
    vec = _full((1, d))
    out = pl.pallas_call(
        body, name=name, grid=(n_t,),
        out_shape=(jax.ShapeDtypeStruct((s, 3 * e), BF16), jax.ShapeDtypeStruct((s, d), BF16),
                   jax.ShapeDtypeStruct((s, d), F32), jax.ShapeDtypeStruct((8, 128), F32),
                   jax.ShapeDtypeStruct((1, d), F32), jax.ShapeDtypeStruct((1, d), F32), *ex_shapes),
        in_specs=[tok, tok, vec, vec, vec, vec, vec, _resident((d, 3 * e)), _full((1, e)), _full((1, e)),
                  _full((GROUPS, CHUNK, CHUNK)), _resident((CHUNK, e)), _resident((e, d)), *ex_in],
        out_specs=(pl.BlockSpec((t, 3 * e), lambda i: (i, 0)), tok, tok, _full((8, 128)), vec, vec, *ex_out),
        scratch_shapes=[pltpu.VMEM((t, e), F32), pltpu.VMEM((t, e), BF16), *ex_sems],
        compiler_params=_params(("arbitrary",)),
    )(x1, tgt, g1, scale, shift, gate, fg, wi, lng, lnb, wt, bsf, wo, *ex_args)
    return (*out[:6], out[6:])


def _norm_modulate_bwd(dh, x_ref, dres_ref, g_ref, sc_ref, dx_ref, dsh_ref, p_scr):
    xv = x_ref[...]
    r = lax.rsqrt(jnp.mean(xv * xv, axis=-1, keepdims=True) + RMS_EPS)
    xn = xv * r
    dsh_ref[...] += _colsum(dh)
    p_scr[...] += _colsum(dh * xn)
    dxn = dh * (g_ref[...] * (1.0 + sc_ref[...]))
    dx_ref[...] = r * (dxn - xn * jnp.mean(dxn * xn, axis=-1, keepdims=True)) + dres_ref[...]


def _layer_b_bwd(proj, dx2, x1, gate, g1, scale, lng, lnb, wt, wtt, bsf, wo, wi, name):
    s, e3 = proj.shape
    e = e3 // 3
    d = dx2.shape[1]
    gw = e // GROUPS
    t = min(s, 256)
    n_t = s // t

    def body(p_ref, dx_ref, x1_ref, gate_ref, g_ref, sc_ref, lng_ref, lnb_ref, wt_ref, wtt_ref, bsf_ref, wo_ref, wi_ref,
             dp_ref, y_ref, dx1_ref, dws_ref, dbs_ref, dlg_ref, dlb_ref, dsh_ref, dsc_ref, dg_ref,
             v_scr, dgv_scr, dbr_scr, dvn_scr, dbs_scr, p_scr, dy_scr, vn_scr, mixed_scr):
        @pl.when(pl.program_id(0) == 0)
        def _():
            dws_ref[...] = jnp.zeros_like(dws_ref)
            dlg_ref[...] = jnp.zeros_like(dlg_ref)
            dlb_ref[...] = jnp.zeros_like(dlb_ref)
            dsh_ref[...] = jnp.zeros_like(dsh_ref)
            dbs_scr[...] = jnp.zeros_like(dbs_scr)
            p_scr[...] = jnp.zeros_like(p_scr)
        dbr_scr[...] = (dx_ref[...] * gate_ref[...]).astype(BF16)
        mu, rs = _ln_stats(lambda g: p_ref[:, e + g * gw:e + (g + 1) * gw].astype(F32), v_scr, dgv_scr, t, e)
        tril = (lax.broadcasted_iota(jnp.int32, (CHUNK, CHUNK), 0) >= lax.broadcasted_iota(jnp.int32, (CHUNK, CHUNK), 1))
        c1 = jnp.zeros((t, 1), F32)
        c2 = jnp.zeros((t, 1), F32)
        span = 2
        kw = span * gw
        dh = jnp.zeros((t, d), F32)

        def through_w_in(c0):
            return lax.dot_general(dp_ref[:, c0:c0 + kw], wi_ref[:, c0:c0 + kw], NT_DIMS, preferred_element_type=F32)

        dy_scr[...] = lax.dot_general(dbr_scr[...], wo_ref[...], NT_DIMS, preferred_element_type=F32)
        for g in range(GROUPS):
            gs = slice(g * gw, (g + 1) * gw)
            vhat = (v_scr[:, gs] - mu) * rs
            v_scr[:, gs] = vhat
            vn = (vhat * lng_ref[:, gs] + lnb_ref[:, gs]).astype(BF16)
            vn_scr[:, gs] = vn
            for ch in range(t // CHUNK):
                rows = slice(ch * CHUNK, (ch + 1) * CHUNK)
                mixed_scr[rows, gs] = jnp.dot(wt_ref[g], vn[rows], preferred_element_type=F32) + bsf_ref[:, gs]
        for g in range(GROUPS):
            gs = slice(g * gw, (g + 1) * gw)
            vhat = v_scr[:, gs]
            lg = lng_ref[:, gs]
            for ch in range(t // CHUNK):
                rows = slice(ch * CHUNK, (ch + 1) * CHUNK)
                mixed = mixed_scr[rows, gs]
                u, dgu = _gelu(p_ref[rows, g * gw:(g + 1) * gw].astype(F32))
                sz, dsz = _silu(p_ref[rows, 2 * e + g * gw:2 * e + (g + 1) * gw].astype(F32))
                sgate = u * mixed
                y_ref[rows, gs] = (sz * sgate).astype(BF16)
                dy = dy_scr[rows, gs]
                dp_ref[rows, 2 * e + g * gw:2 * e + (g + 1) * gw] = (dy * sgate * dsz).astype(BF16)
                ds = dy * sz
                dp_ref[rows, gs] = (ds * mixed * dgu).astype(BF16)
                dm = ds * u
                dbs_scr[:, gs] += dm
                dmb = dm.astype(BF16)
                dws_ref[g] += jnp.where(tril, lax.dot_general(dmb, vn_scr[rows, gs], NT_DIMS, preferred_element_type=F32), 0.0)
                dvn_scr[rows, gs] = jnp.dot(wtt_ref[g], dmb, preferred_element_type=F32)
            dvn = dvn_scr[:, gs]
            dlb_ref[:, gs] += _colsum(dvn)
            dlg_ref[:, gs] += _colsum(dvn * vhat)
            dvh = dvn * lg
            c1 = c1 + _rowsum(dvh)
            c2 = c2 + _rowsum(dvh * vhat)
            if g % span == span - 1:
                dh = dh + through_w_in(g * gw + gw - kw) + through_w_in(2 * e + g * gw + gw - kw)
        c1 = c1 * (1.0 / e)
        c2 = c2 * (1.0 / e)
        for g in range(GROUPS):
            gs = slice(g * gw, (g + 1) * gw)
            dv = rs * (dvn_scr[:, gs] * lng_ref[:, gs] - c1 - v_scr[:, gs] * c2)
            dp_ref[:, e + g * gw:e + (g + 1) * gw] = (dv * dgv_scr[:, gs]).astype(BF16)
            if g % span == span - 1:
                dh = dh + through_w_in(e + g * gw + gw - kw)
        _norm_modulate_bwd(dh, x1_ref, dx_ref, g_ref, sc_ref, dx1_ref, dsh_ref, p_scr)

        @pl.when(pl.program_id(0) == n_t - 1)
        def _():
            lane = lax.broadcasted_iota(jnp.int32, (CHUNK, 128), 1)
            acc = jnp.zeros((CHUNK, 128), F32)
            for g in range(GROUPS):
                acc = acc + jnp.where(lane == g, _rowsum(dbs_scr[:, g * gw:(g + 1) * gw]), 0.0)
            dbs_ref[...] = acc
            dsc_ref[...] = p_scr[...] * g_ref[...]
            dg_ref[...] = p_scr[...] * (1.0 + sc_ref[...])

    tok = pl.BlockSpec((t, d), lambda i: (i, 0))
    vec, evec, ws = _full((1, d)), _full((1, e)), _full((GROUPS, CHUNK, CHUNK))
    vshape = jax.ShapeDtypeStruct((1, d), F32)
    return pl.pallas_call(
        body, name=name, grid=(n_t,),
        out_shape=(jax.ShapeDtypeStruct((s, e3), BF16), jax.ShapeDtypeStruct((s, e), BF16), jax.ShapeDtypeStruct((s, d), F32),
                   jax.ShapeDtypeStruct((GROUPS, CHUNK, CHUNK), F32), jax.ShapeDtypeStruct((CHUNK, 128), F32),
                   jax.ShapeDtypeStruct((1, e), F32), jax.ShapeDtypeStruct((1, e), F32), vshape, vshape, vshape),
        in_specs=[pl.BlockSpec((t, e3), lambda i: (i, 0)), tok, tok, vec, vec, vec, evec, evec, ws, ws,
                  _resident((CHUNK, e)), _resident((e, d)), _resident((d, e3))],
        out_specs=(pl.BlockSpec((t, e3), lambda i: (i, 0)), pl.BlockSpec((t, e), lambda i: (i, 0)), tok,
                   ws, _full((CHUNK, 128)), evec, evec, vec, vec, vec),
        scratch_shapes=[pltpu.VMEM((t, e), F32), pltpu.VMEM((t, e), BF16), pltpu.VMEM((t, d), BF16),
                        pltpu.VMEM((t, e), F32), pltpu.VMEM((CHUNK, e), F32), pltpu.VMEM((1, d), F32),
                        pltpu.VMEM((t, e), F32), pltpu.VMEM((t, e), BF16), pltpu.VMEM((t, e), F32)],
        compiler_params=_params(("arbitrary",)),
    )(proj, dx2, x1, gate, g1, scale, lng, lnb, wt, wtt, bsf, wo, wi)


def _conv_mixer_bwd(proj, dx1, br, tails, cw, cb, gate, wo, name, ex=None):
    s, e4 = proj.shape
    e = e4 // 4
    d = dx1.shape[1]
    t = min(s, 256)
    n_t = s // t
    cwid = min(e, 512)

    def body(p_ref, dx_ref, br_ref, tails_ref, cw_ref, cb_ref, gate_ref, wo_ref,
             dp_ref, y_ref, dgate_ref, dcb_ref, dcw_ref, dy_scr, head_scr):
        i = pl.program_id(0)

        @pl.when(i == 0)
        def _():
            dgate_ref[...] = jnp.zeros_like(dgate_ref)
            dcb_ref[...] = jnp.zeros_like(dcb_ref)
            dcw_ref[...] = jnp.zeros_like(dcw_ref)
            head_scr[...] = jnp.zeros_like(head_scr)
        dx = dx_ref[...]
        dgate_ref[...] += _colsum(dx * br_ref[...].astype(F32))
        dy_scr[...] = lax.dot_general((dx * gate_ref[...]).astype(BF16), wo_ref[...], NT_DIMS,
                                      preferred_element_type=F32)
        row = lax.broadcasted_iota(jnp.int32, (t, cwid), 0)
        has_prev = (i < n_t - 1).astype(F32)
        for c0 in range(0, e, cwid):
            sl = slice(c0, c0 + cwid)
            bg = p_ref[:, c0:c0 + cwid].astype(F32)
            cg = p_ref[:, e + c0:e + c0 + cwid].astype(F32)
            xin = p_ref[:, 2 * e + c0:2 * e + c0 + cwid].astype(F32)
            z = p_ref[:, 3 * e + c0:3 * e + c0 + cwid].astype(F32)
            cx = cg * xin
            p1, p2 = _conv_taps(cx, tails_ref[6:7, sl] * has_prev, tails_ref[7:8, sl] * has_prev, row)
            w0, w1, w2 = cw_ref[0:1, sl], cw_ref[1:2, sl], cw_ref[2:3, sl]
            conv = cb_ref[:, sl] + w2 * cx + w0 * p2 + w1 * p1
            sz, dsz = _silu(z)
            dy = dy_scr[:, sl]
            y_ref[:, sl] = (sz * bg * conv).astype(BF16)
            dp_ref[:, 3 * e + c0:3 * e + c0 + cwid] = (dy * bg * conv * dsz).astype(BF16)
            dp_ref[:, c0:c0 + cwid] = (dy * sz * conv).astype(BF16)
            dconv = dy * sz * bg
            dcb_ref[:, sl] += _colsum(dconv)
            dcw_ref[2:3, sl] += _colsum(dconv * cx)
            dcw_ref[1:2, sl] += _colsum(dconv * p1)
            dcw_ref[0:1, sl] += _colsum(dconv * p2)
            h0, h1 = head_scr[0:1, sl], head_scr[1:2, sl]
            n1 = jnp.where(row == t - 1, h0, pltpu.roll(dconv, t - 1, 0))
            n2 = jnp.where(row == t - 2, h0, jnp.where(row == t - 1, h1, pltpu.roll(dconv, t - 2, 0)))
            dcx = w2 * dconv + w1 * n1 + w0 * n2
            dp_ref[:, e + c0:e + c0 + cwid] = (dcx * xin).astype(BF16)
            dp_ref[:, 2 * e + c0:2 * e + c0 + cwid] = (dcx * cg).astype(BF16)
            head_scr[:, sl] = dconv[0:8, :]

    body = _carry(ex, body, 8, 5, lambda: pl.program_id(0) == 0, None, lambda: pl.program_id(0) == n_t - 1)
    ex_args, ex_in, ex_shapes, ex_out, ex_sems = _carried(ex)
    rev = lambda i: (n_t - 1 - i, 0)
    out = pl.pallas_call(
        body, name=name, grid=(n_t,),
        out_shape=(jax.ShapeDtypeStruct((s, e4), BF16), jax.ShapeDtypeStruct((s, e), BF16),
                   jax.ShapeDtypeStruct((1, d), F32), jax.ShapeDtypeStruct((1, e), F32), jax.ShapeDtypeStruct((8, e), F32),
                   *ex_shapes),
        in_specs=[pl.BlockSpec((t, e4), rev), pl.BlockSpec((t, d), rev), pl.BlockSpec((t, d), rev),
                  pl.BlockSpec((None, 8, e), lambda i: (jnp.maximum(n_t - 2 - i, 0), 0, 0)),
                  _full((3, e)), _full((1, e)), _full((1, d)), _full((e, d)), *ex_in],
        out_specs=(pl.BlockSpec((t, e4), rev), pl.BlockSpec((t, e), rev), _full((1, d)), _full((1, e)), _full((8, e)),
                   *ex_out),
        scratch_shapes=[pltpu.VMEM((t, e), F32), pltpu.VMEM((8, e), F32), *ex_sems],
        compiler_params=_params(("arbitrary",)),
    )(proj, dx1, br, tails, cw, cb, gate, wo, *ex_args)
    return (*out[:5], out[5:])


def _matmul_nt_norm_bwd(dproj, w, xin, dres, g, scale, name, ex=None):
    s, d = xin.shape
    n = w.shape[1]
    tm = min(s, 512)
    n_i = s // tm

    def body(dp_ref, w_ref, x_ref, dres_ref, g_ref, sc_ref, dx_ref, dsh_ref, dsc_ref, dg_ref, p_scr):
        i = pl.program_id(0)

        @pl.when(i == 0)
        def _():
            dsh_ref[...] = jnp.zeros_like(dsh_ref)
            p_scr[...] = jnp.zeros_like(p_scr)
        dh = lax.dot_general(dp_ref[...], w_ref[...], NT_DIMS, preferred_element_type=F32)
        _norm_modulate_bwd(dh, x_ref, dres_ref, g_ref, sc_ref, dx_ref, dsh_ref, p_scr)

        @pl.when(i == n_i - 1)
        def _():
            dsc_ref[...] = p_scr[...] * g_ref[...]
            dg_ref[...] = p_scr[...] * (1.0 + sc_ref[...])

    body = _carry(ex, body, 6, 4, lambda: pl.program_id(0) == 0, None, lambda: pl.program_id(0) == n_i - 1)
    ex_args, ex_in, ex_shapes, ex_out, ex_sems = _carried(ex)
    tok = pl.BlockSpec((tm, d), lambda i: (i, 0))
    vec = pl.BlockSpec((1, d), lambda i: (0, 0))
    vshape = jax.ShapeDtypeStruct((1, d), F32)
    out = pl.pallas_call(
        body, name=name, grid=(n_i,),
        out_shape=(jax.ShapeDtypeStruct((s, d), F32), vshape, vshape, vshape, *ex_shapes),
        in_specs=[pl.BlockSpec((tm, n), lambda i: (i, 0)), _resident((d, n)), tok, tok, vec, vec, *ex_in],
        out_specs=(tok, vec, vec, vec, *ex_out),
        scratch_shapes=[pltpu.VMEM((1, d), F32), *ex_sems],
        compiler_params=_params(("arbitrary",)),
    )(dproj, w, xin, dres, g, scale, *ex_args)
    return (*out[:4], out[4:])


def _matmul_tn(a, b, colscale, rows_split, name, ex=None, a_cols=None):
    s, m = a.shape
    a_blk = 0
    if a_cols is not None:
        a_blk, m = a_cols
    n = b.shape[1]
    n_j, tn = (1, n) if rows_split else (NDEV, n // NDEV)
    fixed = (4 + 4 + 2 * 2) * m * tn
    tk = s
    while fixed + 2 * tk * (2 * m + b.dtype.itemsize * tn) > TN_VMEM_BUDGET:
        tk //= 2
    n_k = s // tk

    def body(a_ref, b_ref, cs_ref, o_ref, acc):
        k = pl.program_id(1)
        part = lax.dot_general(a_ref[...], b_ref[...].astype(BF16), TN_DIMS, preferred_element_type=F32)
        if n_k == 1:
            o_ref[...] = (part * cs_ref[...]).astype(BF16)
            return

        @pl.when(k == 0)
        def _():
            acc[...] = part

        @pl.when((k > 0) & (k < n_k - 1))
        def _():
            acc[...] += part

        @pl.when(k == n_k - 1)
        def _():
            o_ref[...] = ((acc[...] + part) * cs_ref[...]).astype(BF16)

    at = lambda j, k: (pl.program_id(0) == j) & (pl.program_id(1) == k)
    body = _carry(ex, body, 3, 1, lambda: at(0, 0), None, lambda: at(n_j - 1, n_k - 1))
    ex_args, ex_in, ex_shapes, ex_out, ex_sems = _carried(ex)
    out = pl.pallas_call(
        body, name=name, grid=(n_j, n_k),
        out_shape=(jax.ShapeDtypeStruct((n_j, m, tn), BF16), *ex_shapes),
        in_specs=[pl.BlockSpec((tk, m), lambda j, k: (k, a_blk)), pl.BlockSpec((tk, tn), lambda j, k: (k, j)),
                  pl.BlockSpec((1, tn), lambda j, k: (0, j)), *ex_in],
        out_specs=(pl.BlockSpec((None, m, tn), lambda j, k: (j, 0, 0)), *ex_out),
        scratch_shapes=[pltpu.VMEM((m, tn), F32), *ex_sems],
        compiler_params=_params(("arbitrary", "arbitrary")),
    )(a, b, colscale, *ex_args)
    return (out[0].reshape(NDEV, m // NDEV, n) if rows_split else out[0]), out[1:]


def _adam_update(w, g, m, v):
    m = ADAM_B1 * m + (1.0 - ADAM_B1) * g
    v = ADAM_B2 * v + (1.0 - ADAM_B2) * (g * g)
    m_hat = m / (1.0 - ADAM_B1 ** ADAM_STEP)
    v_hat = v / (1.0 - ADAM_B2 ** ADAM_STEP)
    return -ADAM_LR * (m_hat / (jnp.sqrt(v_hat) + ADAM_EPS) + ADAM_WD * w), m, v


def _adamw_reduce(parts, w, m, v, name, ex=None):
    n_l = len(parts)
    n_p, _, c = parts[0].shape
    rows = [p.shape[1] for p in parts]
    r = sum(rows)
    tr = min(min(rows), 128 if ex is not None else 256)
    n_i = r // tr
    tiles = [r_l // tr for r_l in rows]
    first_tile = [sum(tiles[:l]) for l in range(n_l)]

    def body(*refs):
        p_refs, (w_ref, m_ref, v_ref, g_out, d_out, m_out, v_out) = refs[:n_l], refs[n_l:]
        g = None
        for l, p_ref in enumerate(p_refs):
            g_l = p_ref[0].astype(F32)
            for j in range(1, n_p):
                g_l = g_l + p_ref[j].astype(F32)
            g = g_l if g is None else jnp.where(pl.program_id(0) >= first_tile[l], g_l, g)
        g_out[...] = g
        d_out[...], m_out[...], v_out[...] = _adam_update(w_ref[...], g, m_ref[...], v_ref[...])

    step = lambda k: (lambda: pl.program_id(0) == k)
    body = _carry(ex, body, n_l + 3, 4, step(0), [step(n_i // 3), step((2 * n_i) // 3)], step(n_i - 1))
    ex_args, ex_in, ex_shapes, ex_out, ex_sems = _carried(ex)
    blk = pl.BlockSpec((tr, c), lambda i: (i, 0))
    p_specs = [pl.BlockSpec((n_p, tr, c), lambda i, l=l: (0, jnp.clip(i - first_tile[l], 0, tiles[l] - 1), 0))
               for l in range(n_l)]
    shp = jax.ShapeDtypeStruct((r, c), F32)
    out = pl.pallas_call(
        body, name=name, grid=(n_i,), out_shape=(shp, shp, shp, shp, *ex_shapes),
        in_specs=[*p_specs, blk, blk, blk, *ex_in],
        out_specs=(blk, blk, blk, blk, *ex_out), scratch_shapes=ex_sems,
        compiler_params=_params(("arbitrary",)),
    )(*parts, w, m, v, *ex_args)
    return (*out[:4], out[4:])


def _adamw_small(gs, ws, ms, vs, name):
    n = len(gs)

    def body(*refs):
        ins, outs = refs[:4 * n], refs[4 * n:]
        for a in range(n):
            d, m, v = _adam_update(ins[n + a][...], ins[a][...], ins[2 * n + a][...], ins[3 * n + a][...])
            outs[a][...], outs[n + a][...], outs[2 * n + a][...] = d, m, v

    shapes = tuple(jax.ShapeDtypeStruct(w.shape, F32) for w in ws) * 3
    out = pl.pallas_call(
        body, name=name, out_shape=shapes,
        in_specs=[_vmem()] * (4 * n), out_specs=tuple([_vmem()] * (3 * n)),
        compiler_params=pltpu.CompilerParams(vmem_limit_bytes=VMEM_LIMIT),
    )(*gs, *ws, *ms, *vs)
    return out[:n], out[n:2 * n], out[2 * n:]


def _sum_devices(packed, name):
    _, r, wdt = packed.shape

    def body(p_ref, o_ref):
        acc = p_ref[0]
        for j in range(1, NDEV):
            acc = acc + p_ref[j]
        o_ref[...] = acc

    return pl.pallas_call(
        body, name=name, out_shape=jax.ShapeDtypeStruct((r, wdt), F32),
        in_specs=[_vmem()], out_specs=_vmem(),
        compiler_params=pltpu.CompilerParams(vmem_limit_bytes=VMEM_LIMIT),
    )(packed)


def _mod_w_grad(c_t, dmod, name):
    n_layers, _, w3 = dmod.shape
    d = c_t.shape[0]

    def body(c_ref, dm_ref, o_ref):
        for i in range(n_layers):
            acc = c_ref[:, 0:1] * dm_ref[i, 0:1, :]
            for b in range(1, NDEV):
                acc = acc + c_ref[:, b:b + 1] * dm_ref[i, b:b + 1, :]
            o_ref[i] = acc

    return pl.pallas_call(
        body, name=name, out_shape=jax.ShapeDtypeStruct((n_layers, d, w3), F32),
        in_specs=[_vmem(), _vmem()], out_specs=_vmem(),
        compiler_params=pltpu.CompilerParams(vmem_limit_bytes=VMEM_LIMIT),
    )(c_t, dmod)


def _mask_transpose_ws(w_s, name):
    def body(w_ref, wt_ref, wtt_ref):
        tril = (lax.broadcasted_iota(jnp.int32, (CHUNK, CHUNK), 0) >= lax.broadcasted_iota(jnp.int32, (CHUNK, CHUNK), 1))
        for g in range(GROUPS):
            wm = jnp.where(tril, w_ref[g], 0.0)
            wt_ref[g] = wm.astype(BF16)
            wtt_ref[g] = wm.T.astype(BF16)

    shp = jax.ShapeDtypeStruct(w_s.shape, BF16)
    return pl.pallas_call(
        body, name=name, out_shape=(shp, shp), in_specs=[_vmem()], out_specs=(_vmem(), _vmem()),
    )(w_s)


def _pack(pieces):
    flat = jnp.concatenate([p.reshape(-1) for p in pieces])
    rows = -(-flat.shape[0] // (8 * PACK_W)) * 8
    return jnp.pad(flat, (0, rows * PACK_W - flat.shape[0])).reshape(rows, PACK_W)


def _unpack(flat, shapes):
    out, off = [], 0
    for shp in shapes:
        size = 1
        for dim in shp:
            size *= dim
        out.append(flat[off:off + size].reshape(shp))
        off += size
    return out


def kernel(x, c, mod_w, mod_b, norm_g, a_w_in, a_conv_w, a_conv_b, a_w_out, b_w_in, b_ln_g, b_ln_b, b_w_s, b_b_s, b_w_out, final_g, loss_target, m_mod_w, m_mod_b, m_norm_g, m_a_w_in, m_a_conv_w, m_a_conv_b, m_a_w_out, m_b_w_in, m_b_ln_g, m_b_ln_b, m_b_w_s, m_b_b_s, m_b_w_out, m_final_g, v_mod_w, v_mod_b, v_norm_g, v_a_w_in, v_a_conv_w, v_a_conv_b, v_a_w_out, v_b_w_in, v_b_ln_g, v_b_ln_b, v_b_w_s, v_b_b_s, v_b_w_out, v_final_g):
    s, d = x.shape[1], x.shape[2]
    es = a_w_out.shape[1]
    e = NDEV * es
    w3 = mod_w.shape[2]
    me = _index(_pos())
    x0 = x.reshape(s, d)
    tgt = loss_target.reshape(s, d)

    small = jnp.concatenate([a_conv_w[0], b_ln_g, b_ln_b, jnp.zeros((3, es), F32)], axis=0)
    gather_a = _gather_exchange([a_w_in[0].astype(BF16), a_w_out[0].astype(BF16), small], [True, False, True])
    gather_b = _gather_exchange([b_w_in[0].astype(BF16), b_w_out[0].astype(BF16)], [True, False])
    mod, c_all, (wa, woa, small_all) = _mod_vectors(c, mod_w, mod_b, gather_a)
    conv_w, ln_g, ln_b = small_all[0:3], small_all[3:4], small_all[4:5]
    bsf = jnp.repeat(b_b_s[0].T, e // GROUPS, axis=1)
    wt, wtt = _mask_transpose_ws(b_w_s[0], "mask_w_s")
    shift0, scale0, gate0 = mod[0:1, 0:d], mod[0:1, d:2 * d], mod[0:1, 2 * d:]
    shift1, scale1, gate1 = mod[1:2, 0:d], mod[1:2, d:2 * d], mod[1:2, 2 * d:]
    g0, g1, fg = norm_g[0:1], norm_g[1:2], final_g.reshape(1, d)

    proj_a, h0, x1, br_a, tails, (wb, wob) = _layer_a_fwd(
        x0, g0, scale0, shift0, gate0, wa, conv_w, a_conv_b, woa, "a_fwd", gather_b)
    proj_b, h1, dx2, loss_acc, dfg, dgate1, _ = _layer_b_fwd_loss(
        x1, tgt, g1, scale1, shift1, gate1, fg, wb, ln_g, ln_b, wt, bsf, wob, "b_fwd_loss")

    dproj_b, y_b, dx1, dws, dbs, dlg, dlb, dshift1, dscale1, dg1 = _layer_b_bwd(
        proj_b, dx2, x1, gate1, g1, scale1, ln_g, ln_b, wt, wtt, bsf, wob, wb, "b_bwd")
    gs_b_out, _ = _matmul_tn(y_b, dx2, gate1, True, "b_w_out_grad")
    gs_b_in, _ = _matmul_tn(h1, dproj_b, jnp.ones((1, dproj_b.shape[1]), F32), False, "b_w_in_grad")
    dproj_a, y_a, dgate0, dcb, dcw, (gr_b_in, gr_b_out) = _conv_mixer_bwd(
        proj_a, dx1, br_a, tails, conv_w, a_conv_b, gate0, woa, "a_mixer_bwd", _scatter_exchange([gs_b_in, gs_b_out]))
    gs_a_out, _ = _matmul_tn(y_a, dx1, gate0, True, "a_w_out_grad")
    ones_n = jnp.ones((1, dproj_a.shape[1]), F32)
    gs_last, _ = _matmul_tn(h0, dproj_a, ones_n, False, "a_w_in_grad_last", a_cols=(3, d // 4))
    gs_rest, (gr_a_out, gr_last) = _matmul_tn(h0, dproj_a, ones_n, False, "a_w_in_grad_rest",
                                              _scatter_exchange([gs_a_out, gs_last]), a_cols=(0, 3 * d // 4))
    dx0, dshift0, dscale0, dg0, (gr_rest,) = _matmul_nt_norm_bwd(dproj_a, wa, x0, dx1, g0, scale0, "a_in_bwd",
                                                                 _scatter_exchange([gs_rest]))

    def big(parts, w, m, v, name, ex=None):
        shp = w.shape
        r2 = lambda t_: t_.reshape(-1, shp[-1])
        g, dl, nm, nv, ex_out = _adamw_reduce(parts, r2(w), r2(m), r2(v), name, ex)
        return tuple(t_.reshape(shp) for t_ in (g, dl, nm, nv)), ex_out

    res = {}
    res["b_w_in"], _ = big([gr_b_in], b_w_in, m_b_w_in, v_b_w_in, "adamw_b_w_in")
    gr_a_in = [gr_rest, gr_last]

    pieces = [dshift0, dscale0, dgate0, dshift1, dscale1, dgate1, dg0, dg1, dcb, dcw[0:3], dlg, dlb, dfg,
              dbs[:, 0:GROUPS].T, dws, loss_acc[0:1, 0:1]]
    shapes = [p.shape for p in pieces]
    res["a_w_in"], (packed_all,) = big(gr_a_in, a_w_in, m_a_w_in, v_a_w_in, "adamw_a_w_in",
                                       _gather_exchange([_pack(pieces)], [False]))
    packed_all = packed_all.reshape(NDEV, -1, PACK_W)
    total = _sum_devices(packed_all, "sum_small_grads").reshape(-1)
    (t_sh0, t_sc0, t_ga0, t_sh1, t_sc1, t_ga1, t_g0, t_g1, t_cb, t_cw, t_lg, t_lb, t_fg, t_bs, t_ws, t_loss) = _unpack(
        total, shapes)
    loss = t_loss.reshape(())
    grad_mod_b = jnp.concatenate([jnp.concatenate([t_sh0, t_sc0, t_ga0], axis=1),
                                  jnp.concatenate([t_sh1, t_sc1, t_ga1], axis=1)], axis=0)
    grad_norm_g = jnp.concatenate([t_g0, t_g1], axis=0)
    dmod_all = packed_all.reshape(NDEV, -1)[:, 0:6 * d].reshape(NDEV, 2, 3 * d).transpose(1, 0, 2)
    dmod_mine = lax.dynamic_slice_in_dim(dmod_all, me * w3, w3, axis=2)
    grad_mod_w = _mod_w_grad(c_all.T, dmod_mine, "mod_w_grad")
    grad_a_conv_w = lax.dynamic_slice_in_dim(t_cw, me * es, es, axis=1)
    grad_b_ln_g = lax.dynamic_slice_in_dim(t_lg, me * es, es, axis=1)
    grad_b_ln_b = lax.dynamic_slice_in_dim(t_lb, me * es, es, axis=1)

    res["mod_w"], _ = big([grad_mod_w.reshape(1, -1, w3)], mod_w, m_mod_w, v_mod_w, "adamw_mod_w")
    res["a_w_out"], _ = big([gr_a_out], a_w_out, m_a_w_out, v_a_w_out, "adamw_a_w_out")
    res["b_w_out"], _ = big([gr_b_out], b_w_out, m_b_w_out, v_b_w_out, "adamw_b_w_out")

    small_names = ["mod_b", "norm_g", "a_conv_w", "a_conv_b", "b_ln_g", "b_ln_b", "b_w_s", "b_b_s", "final_g"]
    small_g = [grad_mod_b, grad_norm_g, grad_a_conv_w, t_cb, grad_b_ln_g, grad_b_ln_b, t_ws, t_bs, t_fg]
    small_w = [mod_b, norm_g, a_conv_w, a_conv_b, b_ln_g, b_ln_b, b_w_s, b_b_s, final_g]
    small_m = [m_mod_b, m_norm_g, m_a_conv_w, m_a_conv_b, m_b_ln_g, m_b_ln_b, m_b_w_s, m_b_b_s, m_final_g]
    small_v = [v_mod_b, v_norm_g, v_a_conv_w, v_a_conv_b, v_b_ln_g, v_b_ln_b, v_b_w_s, v_b_b_s, v_final_g]
    as2d = lambda t_: t_.reshape(-1, t_.shape[-1])
    dls, nms, nvs = _adamw_small([as2d(t_) for t_ in small_g], [as2d(t_) for t_ in small_w],
                                 [as2d(t_) for t_ in small_m], [as2d(t_) for t_ in small_v], "adamw_small")
    for a, nme in enumerate(small_names):
        shp = small_w[a].shape
        res[nme] = (small_g[a].reshape(shp), dls[a].reshape(shp), nms[a].reshape(shp), nvs[a].reshape(shp))

    order = ["mod_w", "mod_b", "norm_g", "a_w_in", "a_conv_w", "a_conv_b", "a_w_out", "b_w_in", "b_ln_g", "b_ln_b",
             "b_w_s", "b_b_s", "b_w_out", "final_g"]
    return (loss, dx0.reshape(x.shape), *[res[k][0] for k in order], *[res[k][1] for k in order],
            *[res[k][2] for k in order], *[res[k][3] for k in order])
```

```python
import functools

import jax
import jax.numpy as jnp
from jax import lax
from jax.experimental import pallas as pl
from jax.experimental.pallas import tpu as pltpu

NDEV = 8
CHUNK = 128
GROUPS = 8
RMS_EPS = 1e-6
LN_EPS = 1e-5
ADAM_LR, ADAM_B1, ADAM_B2, ADAM_EPS, ADAM_WD, ADAM_STEP = 0.001, 0.9, 0.999, 1e-08, 0.01, 10
V7X_VMEM_BYTES = 64 * 1024 * 1024
VMEM_LIMIT = V7X_VMEM_BYTES - 8 * 1024 * 1024
TN_VMEM_BUDGET = 46 * 1024 * 1024
PACK_W = 1024
F32, BF16 = jnp.float32, jnp.bfloat16
MESH = pl.DeviceIdType.MESH
RSQRT2 = 0.7071067811865476
INV_SQRT_2PI = 0.3989422804014327
NT_DIMS = (((1,), (1,)), ((), ()))
TN_DIMS = (((0,), (0,)), ((), ()))


def _params(sem=None):
    return pltpu.CompilerParams(dimension_semantics=sem, vmem_limit_bytes=VMEM_LIMIT)


def _vmem():
    return pl.BlockSpec(memory_space=pltpu.VMEM)


def _hbm():
    return pl.BlockSpec(memory_space=pltpu.HBM)


def _full(shape):
    return pl.BlockSpec(shape, lambda *_: (0,) * len(shape))


def _pos():
    return lax.axis_index("x"), lax.axis_index("y"), lax.axis_index("c")


def _index(p):
    return 4 * p[0] + 2 * p[1] + p[2]


def _peer(k):
    x, y, c = _pos()
    return ((1 - x) if (k >> 2) & 1 else x, (1 - y) if (k >> 1) & 1 else y, (1 - c) if k & 1 else c)


def _silu(z):
    sg = jax.nn.sigmoid(z)
    return z * sg, sg * (1.0 + z * (1.0 - sg))


def _gelu(v):
    phi = 0.5 * (1.0 + lax.erf(v * RSQRT2))
    return v * phi, phi + v * (jnp.exp(-0.5 * v * v) * INV_SQRT_2PI)


def _colsum(v):
    return jnp.sum(v, axis=0, keepdims=True)


def _rowsum(v):
    return jnp.sum(v, axis=-1, keepdims=True)


def _gather_all_vmem(slab_ref, send_sems, recv_sems, base):
    me = _index(_pos())
    sends = []
    for k in range(1, NDEV):
        cp = pltpu.make_async_remote_copy(
            src_ref=slab_ref.at[me], dst_ref=slab_ref.at[me],
            send_sem=send_sems.at[base + k - 1], recv_sem=recv_sems.at[base + k - 1],
            device_id=_peer(k), device_id_type=MESH)
        cp.start()
        sends.append(cp)
    for k in range(1, NDEV):
        src = _index(_peer(k))
        pltpu.make_async_remote_copy(
            src_ref=slab_ref.at[src], dst_ref=slab_ref.at[src],
            send_sem=send_sems.at[base + k - 1], recv_sem=recv_sems.at[base + k - 1],
            device_id=_peer(k), device_id_type=MESH).wait_recv()
    for cp in sends:
        cp.wait_send()


def _mod_vectors(c, mod_w, mod_b, ex):
    n_layers, d, w3 = mod_w.shape
    r_in, r_out = len(ex.arrays), len(ex.out_shapes)

    def body(*refs):
        c_ref, mw_ref, mb_ref = refs[:3]
        ex_ins = refs[3:3 + r_in]
        mod_ref, call_ref = refs[3 + r_in:5 + r_in]
        ex_outs = refs[5 + r_in:5 + r_in + r_out]
        cslab, pslab, send_sems, recv_sems = refs[5 + r_in + r_out:9 + r_in + r_out]
        ex_sems = refs[9 + r_in + r_out:]
        ex.start(ex_ins, ex_outs, ex_sems)
        me = _index(_pos())
        cv = c_ref[...]
        cslab[me] = jnp.broadcast_to(cv * jax.nn.sigmoid(cv), (8, d))
        _gather_all_vmem(cslab, send_sems, recv_sems, 0)
        c_all = jnp.concatenate([cslab[k, 0:1, :] for k in range(NDEV)], axis=0)
        call_ref[...] = c_all
        for i in range(n_layers):
            pslab[me, i * NDEV:(i + 1) * NDEV, :] = jnp.dot(
                c_all, mw_ref[i], preferred_element_type=F32, precision=lax.Precision.HIGHEST)
        _gather_all_vmem(pslab, send_sems, recv_sems, NDEV - 1)
        for i in range(n_layers):
            for k in range(NDEV):
                mod_ref[i:i + 1, k * w3:(k + 1) * w3] = (
                    pslab[k, pl.ds(i * NDEV + me, 1), :] + mb_ref[i:i + 1, k * w3:(k + 1) * w3])
        for passing_on in ex.middles:
            passing_on(ex_ins, ex_outs, ex_sems)
        ex.finish(ex_ins, ex_outs, ex_sems)

    out = pl.pallas_call(
        body, name="mod_vectors",
        out_shape=(jax.ShapeDtypeStruct((n_layers, 3 * d), F32), jax.ShapeDtypeStruct((NDEV, d), F32), *ex.out_shapes),
        in_specs=[_vmem(), _vmem(), _vmem()] + [_hbm()] * r_in, out_specs=(_vmem(), _vmem(), *([_hbm()] * r_out)),
        scratch_shapes=[pltpu.VMEM((NDEV, 8, d), F32), pltpu.VMEM((NDEV, n_layers * NDEV, w3), F32),
                        pltpu.SemaphoreType.DMA((2 * (NDEV - 1),)), pltpu.SemaphoreType.DMA((2 * (NDEV - 1),)), *ex.sems],
        compiler_params=pltpu.CompilerParams(vmem_limit_bytes=VMEM_LIMIT),
    )(c, mod_w, mod_b, *ex.arrays)
    return out[0], out[1], out[2:]


class _Exchange:
    def __init__(self, arrays, out_shapes, sems, start, middles, finish):
        self.arrays, self.out_shapes, self.sems = list(arrays), list(out_shapes), list(sems)
        self.start, self.middles, self.finish = start, list(middles), finish


def _gather_exchange(shards, by_cols):
    n = len(shards)
    shapes = [sh.shape for sh in shards]

    def tools(ins, outs, sems):
        send_sems, recv_sems, local_sems = sems
        x, y, c = _pos()
        chips = [(1 - x, y), (x, 1 - y), (1 - x, 1 - y)]
        south = c == 0
        relayed = (jnp.where(south, 1 - x, x), jnp.where(south, y, 1 - y), c)
        relay_to = (jnp.where(south, x, 1 - x), jnp.where(south, 1 - y, y), c)

        def place(a, block):
            r, cc = shapes[a]
            if by_cols[a]:
                return outs[a].at[:, pl.ds(_index(block) * cc, cc)]
            return outs[a].at[pl.ds(_index(block) * r, r), :]

        def copy(a, k, block, to, src=None):
            dst = place(a, block)
            return pltpu.make_async_remote_copy(
                src_ref=dst if src is None else src, dst_ref=dst,
                send_sem=send_sems.at[a * 7 + k], recv_sem=recv_sems.at[a * 7 + k],
                device_id=to, device_id_type=MESH)

        mine = [pltpu.make_async_copy(ins[a], place(a, (x, y, c)), local_sems.at[a]) for a in range(n)]
        first = []
        for a in range(n):
            first.append(copy(a, 0, (x, y, c), (x, y, 1 - c), src=ins[a]))
            first += [copy(a, 1 + j, (x, y, c), (*chip, c), src=ins[a]) for j, chip in enumerate(chips[:2])]
        relays = [copy(a, 3, relayed, relay_to) for a in range(n)]
        passed = [copy(a, 4 + j, (*chip, c), (x, y, 1 - c)) for j, chip in enumerate(chips) for a in range(n)]
        return (x, y, c), chips, copy, mine, first, relays, passed

    def start(ins, outs, sems):
        _, _, _, mine, first, _, _ = tools(ins, outs, sems)
        for cp in mine + first:
            cp.start()

    def pass_neighbours(ins, outs, sems):
        (x, y, c), chips, copy, _, _, relays, passed = tools(ins, outs, sems)
        for j, chip in enumerate(chips[:2]):
            for a in range(n):
                copy(a, 1 + j, (*chip, c), (x, y, c)).wait_recv()
                passed[j * n + a].start()
        for cp in relays:
            cp.start()

    def pass_diagonal(ins, outs, sems):
        (x, y, c), chips, copy, _, _, _, passed = tools(ins, outs, sems)
        for a in range(n):
            copy(a, 3, (*chips[2], c), (x, y, c)).wait_recv()
            passed[2 * n + a].start()

    def finish(ins, outs, sems):
        (x, y, c), chips, copy, mine, first, relays, passed = tools(ins, outs, sems)
        for a in range(n):
            copy(a, 0, (x, y, 1 - c), (x, y, c)).wait_recv()
        for j, chip in enumerate(chips):
            for a in range(n):
                copy(a, 4 + j, (*chip, 1 - c), (x, y, c)).wait_recv()
        for cp in first + relays + passed:
            cp.wait_send()
        for cp in mine:
            cp.wait()

    out_shapes = [jax.ShapeDtypeStruct((r, NDEV * cc) if bc else (NDEV * r, cc), sh.dtype)
                  for (r, cc), bc, sh in zip(shapes, by_cols, shards)]
    sems = [pltpu.SemaphoreType.DMA((7 * n,)), pltpu.SemaphoreType.DMA((7 * n,)), pltpu.SemaphoreType.DMA((n,))]
    return _Exchange(shards, out_shapes, sems, start, [pass_neighbours, pass_diagonal], finish)


def _scatter_exchange(parts):
    n = len(parts)

    def tools(ins, outs, sems):
        send_sems, recv_sems, local_sems = sems
        me = _index(_pos())
        mine = [pltpu.make_async_copy(ins[a].at[me], outs[a].at[me], local_sems.at[a]) for a in range(n)]
        sends, arrivals = [], []
        for k in range(1, NDEV):
            peer = _peer(k)
            for a in range(n):
                pair = dict(send_sem=send_sems.at[a * 7 + k - 1], recv_sem=recv_sems.at[a * 7 + k - 1],
                            device_id=peer, device_id_type=MESH)
                sends.append(pltpu.make_async_remote_copy(src_ref=ins[a].at[_index(peer)], dst_ref=outs[a].at[me], **pair))
                slot = outs[a].at[_index(peer)]
                arrivals.append(pltpu.make_async_remote_copy(src_ref=slot, dst_ref=slot, **pair))
        return mine, sends, arrivals

    def start(ins, outs, sems):
        mine, sends, _ = tools(ins, outs, sems)
        for cp in mine + sends:
            cp.start()

    def finish(ins, outs, sems):
        mine, sends, arrivals = tools(ins, outs, sems)
        for cp in arrivals:
            cp.wait_recv()
        for cp in sends:
            cp.wait_send()
        for cp in mine:
            cp.wait()

    out_shapes = [jax.ShapeDtypeStruct(p.shape, p.dtype) for p in parts]
    sems = [pltpu.SemaphoreType.DMA((7 * n,)), pltpu.SemaphoreType.DMA((7 * n,)), pltpu.SemaphoreType.DMA((n,))]
    return _Exchange(parts, out_shapes, sems, start, [], finish)


def _carry(ex, body, n_in, n_out, first, middle, last):
    if ex is None:
        return body
    r_in, r_out = len(ex.arrays), len(ex.out_shapes)

    def wrapped(*refs):
        ins, rins = refs[:n_in], refs[n_in:n_in + r_in]
        outs = refs[n_in + r_in:n_in + r_in + n_out]
        routs = refs[n_in + r_in + n_out:n_in + r_in + n_out + r_out]
        rest = refs[n_in + r_in + n_out + r_out:]
        scratch, sems = rest[:len(rest) - len(ex.sems)], rest[len(rest) - len(ex.sems):]

        @pl.when(first())
        def _():
            ex.start(rins, routs, sems)

        for passing_on, at_step in zip(ex.middles, middle or []):
            pl.when(at_step())(functools.partial(passing_on, rins, routs, sems))

        body(*ins, *outs, *scratch)

        @pl.when(last())
        def _():
            ex.finish(rins, routs, sems)

    return wrapped


def _carried(ex):
    if ex is None:
        return [], [], [], [], []
    return ex.arrays, [_hbm()] * len(ex.arrays), ex.out_shapes, [_hbm()] * len(ex.out_shapes), ex.sems


def _resident(shape):
    return pl.BlockSpec(shape, lambda *_: (0,) * len(shape), pipeline_mode=pl.Buffered(1))


def _norm_modulate(x_ref, g_ref, sc_ref, sh_ref):
    xv = x_ref[...]
    r = lax.rsqrt(jnp.mean(xv * xv, axis=-1, keepdims=True) + RMS_EPS)
    return ((xv * r) * g_ref[...] * (1.0 + sc_ref[...]) + sh_ref[...]).astype(BF16)


def _conv_taps(cx, t6, t7, row):
    p1 = jnp.where(row == 0, t7, pltpu.roll(cx, 1, 0))
    p2 = jnp.where(row == 0, t6, jnp.where(row == 1, t7, pltpu.roll(cx, 2, 0)))
    return p1, p2


def _layer_a_fwd(x, g, scale, shift, gate, wi, cw, cb, wo, name, ex=None):
    s, d = x.shape
    e = wo.shape[0]
    t = min(s, 256)
    n_t = s // t
    cwid = min(e, 512)

    def body(x_ref, g_ref, sc_ref, sh_ref, gate_ref, wi_ref, cw_ref, cb_ref, wo_ref,
             proj_ref, h_ref, x1_ref, br_ref, conv_ref, y_scr, tail_scr):
        @pl.when(pl.program_id(0) == 0)
        def _():
            tail_scr[...] = jnp.zeros_like(tail_scr)
        h_ref[...] = _norm_modulate(x_ref, g_ref, sc_ref, sh_ref)
        row = lax.broadcasted_iota(jnp.int32, (t, cwid), 0)

        def project(c0):
            v = jnp.dot(h_ref[...], wi_ref[:, c0:c0 + cwid], preferred_element_type=F32)
            proj_ref[:, c0:c0 + cwid] = v.astype(BF16)
            return v

        for c0 in range(0, e, cwid):
            sl = slice(c0, c0 + cwid)
            bg, z = project(c0), project(3 * e + c0)
            cx = project(e + c0) * project(2 * e + c0)
            p1, p2 = _conv_taps(cx, tail_scr[6:7, sl], tail_scr[7:8, sl], row)
            conv = cb_ref[:, sl] + cw_ref[2:3, sl] * cx + cw_ref[0:1, sl] * p2 + cw_ref[1:2, sl] * p1
            conv_ref[:, sl] = conv.astype(BF16)
            y_scr[:, sl] = (_silu(z)[0] * bg * conv).astype(BF16)
            tail_scr[:, sl] = cx[t - 8:t, :]
        br = jnp.dot(y_scr[...], wo_ref[...], preferred_element_type=F32)
        x1_ref[...] = x_ref[...] + gate_ref[...] * br
        br_ref[...] = br.astype(BF16)

    step = lambda k: (lambda: pl.program_id(0) == k)
    body = _carry(ex, body, 9, 5, step(0), [step(n_t // 3), step((2 * n_t) // 3)], step(n_t - 1))
    ex_args, ex_in, ex_shapes, ex_out, ex_sems = _carried(ex)
    tok = pl.BlockSpec((t, d), lambda i: (i, 0))
    out = pl.pallas_call(
        body, name=name, grid=(n_t,),
        out_shape=(jax.ShapeDtypeStruct((s, 4 * e), BF16), jax.ShapeDtypeStruct((s, d), BF16),
                   jax.ShapeDtypeStruct((s, d), F32), jax.ShapeDtypeStruct((s, d), BF16),
                   jax.ShapeDtypeStruct((s, e), BF16), *ex_shapes),
        in_specs=[tok, _full((1, d)), _full((1, d)), _full((1, d)), _full((1, d)), _resident((d, 4 * e)),
                  _full((3, e)), _full((1, e)), _resident((e, d)), *ex_in],
        out_specs=(pl.BlockSpec((t, 4 * e), lambda i: (i, 0)), tok, tok, tok,
                   pl.BlockSpec((t, e), lambda i: (i, 0)), *ex_out),
        scratch_shapes=[pltpu.VMEM((t, e), BF16), pltpu.VMEM((8, e), F32), *ex_sems],
        compiler_params=_params(("arbitrary",)),
    )(x, g, scale, shift, gate, wi, cw, cb, wo, *ex_args)
    return (*out[:5], out[5:])


def _ln_stats(v_pre, v_scr, dgv_scr, t, e):
    gw = e // GROUPS
    s1 = jnp.zeros((t, 1), F32)
    for g in range(GROUPS):
        v, dgv = _gelu(v_pre(g))
        v_scr[:, g * gw:(g + 1) * gw] = v
        if dgv_scr is not None:
            dgv_scr[:, g * gw:(g + 1) * gw] = dgv.astype(dgv_scr.dtype)
        s1 = s1 + _rowsum(v)
    mu = s1 * (1.0 / e)
    s2 = jnp.zeros((t, 1), F32)
    for g in range(GROUPS):
        dv = v_scr[:, g * gw:(g + 1) * gw] - mu
        s2 = s2 + _rowsum(dv * dv)
    return mu, lax.rsqrt(s2 * (1.0 / e) + LN_EPS)


def _layer_b_fwd_loss(x1, tgt, g1, scale, shift, gate, fg, wi, lng, lnb, wt, bsf, wo, name, ex=None):
    s, d = x1.shape
    e = wo.shape[0]
    gw = e // GROUPS
    t = min(s, 256)
    n_t = s // t

    def body(x1_ref, tgt_ref, g_ref, sc_ref, sh_ref, gate_ref, fg_ref, wi_ref, lng_ref, lnb_ref, wt_ref, bsf_ref, wo_ref,
             proj_ref, h_ref, dx2_ref, loss_ref, dfg_ref, dgate_ref, v_scr, y_scr):
        @pl.when(pl.program_id(0) == 0)
        def _():
            loss_ref[...] = jnp.zeros_like(loss_ref)
            dfg_ref[...] = jnp.zeros_like(dfg_ref)
            dgate_ref[...] = jnp.zeros_like(dgate_ref)
        h_ref[...] = _norm_modulate(x1_ref, g_ref, sc_ref, sh_ref)

        def project(c0):
            v = jnp.dot(h_ref[...], wi_ref[:, c0:c0 + gw], preferred_element_type=F32)
            proj_ref[:, c0:c0 + gw] = v.astype(BF16)
            return v

        mu, rs = _ln_stats(lambda g: project(e + g * gw), v_scr, None, t, e)
        for g in range(GROUPS):
            gs = slice(g * gw, (g + 1) * gw)
            vn = (((v_scr[:, gs] - mu) * rs) * lng_ref[:, gs] + lnb_ref[:, gs]).astype(BF16)
            u = _gelu(project(g * gw))[0]
            sz = _silu(project(2 * e + g * gw))[0]
            for ch in range(t // CHUNK):
                rows = slice(ch * CHUNK, (ch + 1) * CHUNK)
                mixed = jnp.dot(wt_ref[g], vn[rows], preferred_element_type=F32) + bsf_ref[:, gs]
                y_scr[rows, gs] = (sz[rows] * (u[rows] * mixed)).astype(BF16)
        br = jnp.dot(y_scr[...], wo_ref[...], preferred_element_type=F32)
        x2 = x1_ref[...] + gate_ref[...] * br
        r2 = lax.rsqrt(jnp.mean(x2 * x2, axis=-1, keepdims=True) + RMS_EPS)
        xn = x2 * r2
        diff = xn * fg_ref[...] - tgt_ref[...]
        loss_ref[...] += jnp.broadcast_to(0.5 * _colsum(jnp.mean(diff * diff, axis=-1, keepdims=True)), loss_ref.shape)
        dout = diff * (1.0 / d)
        dfg_ref[...] += _colsum(dout * xn)
        dxn = dout * fg_ref[...]
        dx2 = r2 * (dxn - xn * jnp.mean(dxn * xn, axis=-1, keepdims=True))
        dx2_ref[...] = dx2
        dgate_ref[...] += _colsum(dx2 * br)

    step = lambda k: (lambda: pl.program_id(0) == k)
    body = _carry(ex, body, 13, 6, step(0), [step(n_t // 3), step((2 * n_t) // 3)], step(n_t - 1))
    ex_args, ex_in, ex_shapes, ex_out, ex_sems = _carried(ex)
    tok = pl.BlockSpec((t, d), lambda i: (i, 0))
    vec = _full((1, d))
    out = pl.pallas_call(
        body, name=name, grid=(n_t,),
        out_shape=(jax.ShapeDtypeStruct((s, 3 * e), BF16), jax.ShapeDtypeStruct((s, d), BF16),
                   jax.ShapeDtypeStruct((s, d), F32), jax.ShapeDtypeStruct((8, 128), F32),
                   jax.ShapeDtypeStruct((1, d), F32), jax.ShapeDtypeStruct((1, d), F32), *ex_shapes),
        in_specs=[tok, tok, vec, vec, vec, vec, vec, _resident((d, 3 * e)), _full((1, e)), _full((1, e)),
                  _full((GROUPS, CHUNK, CHUNK)), _resident((CHUNK, e)), _resident((e, d)), *ex_in],
        out_specs=(pl.BlockSpec((t, 3 * e), lambda i: (i, 0)), tok, tok, _full((8, 128)), vec, vec, *ex_out),
        scratch_shapes=[pltpu.VMEM((t, e), F32), pltpu.VMEM((t, e), BF16), *ex_sems],
        compiler_params=_params(("arbitrary",)),
    )(x1, tgt, g1, scale, shift, gate, fg, wi, lng, lnb, wt, bsf, wo, *ex_args)
    return (*out[:6], out[6:])


def _norm_modulate_bwd(dh, x_ref, dres_ref, g_ref, sc_ref, dx_ref, dsh_ref, p_scr):
    xv = x_ref[...]
    r = lax.rsqrt(jnp.mean(xv * xv, axis=-1, keepdims=True) + RMS_EPS)
    xn = xv * r
    dsh_ref[...] += _colsum(dh)
    p_scr[...] += _colsum(dh * xn)
    dxn = dh * (g_ref[...] * (1.0 + sc_ref[...]))
    dx_ref[...] = r * (dxn - xn * jnp.mean(dxn * xn, axis=-1, keepdims=True)) + dres_ref[...]


def _layer_b_bwd(proj, dx2, x1, gate, g1, scale, lng, lnb, wt, wtt, bsf, wo, wi, name):
    s, e3 = proj.shape
    e = e3 // 3
    d = dx2.shape[1]
    gw = e // GROUPS
    t = min(s, 256)
    n_t = s // t

    def body(p_ref, dx_ref, x1_ref, gate_ref, g_ref, sc_ref, lng_ref, lnb_ref, wt_ref, wtt_ref, bsf_ref, wo_ref, wi_ref,
             dp_ref, y_ref, dx1_ref, dws_ref, dbs_ref, dlg_ref, dlb_ref, dsh_ref, dsc_ref, dg_ref,
             v_scr, dgv_scr, dbr_scr, dvn_scr, dbs_scr, p_scr, dy_scr, vn_scr, mixed_scr):
        @pl.when(pl.program_id(0) == 0)
        def _():
            dws_ref[...] = jnp.zeros_like(dws_ref)
            dlg_ref[...] = jnp.zeros_like(dlg_ref)
            dlb_ref[...] = jnp.zeros_like(dlb_ref)
            dsh_ref[...] = jnp.zeros_like(dsh_ref)
            dbs_scr[...] = jnp.zeros_like(dbs_scr)
            p_scr[...] = jnp.zeros_like(p_scr)
        dbr_scr[...] = (dx_ref[...] * gate_ref[...]).astype(BF16)
        mu, rs = _ln_stats(lambda g: p_ref[:, e + g * gw:e + (g + 1) * gw].astype(F32), v_scr, dgv_scr, t, e)
        tril = (lax.broadcasted_iota(jnp.int32, (CHUNK, CHUNK), 0) >= lax.broadcasted_iota(jnp.int32, (CHUNK, CHUNK), 1))
        c1 = jnp.zeros((t, 1), F32)
        c2 = jnp.zeros((t, 1), F32)
        span = 2
        kw = span * gw
        dh = jnp.zeros((t, d), F32)

        def through_w_in(c0):
            return lax.dot_general(dp_ref[:, c0:c0 + kw], wi_ref[:, c0:c0 + kw], NT_DIMS, preferred_element_type=F32)

        dy_scr[...] = lax.dot_general(dbr_scr[...], wo_ref[...], NT_DIMS, preferred_element_type=F32)
        for g in range(GROUPS):
            gs = slice(g * gw, (g + 1) * gw)
            vhat = (v_scr[:, gs] - mu) * rs
            v_scr[:, gs] = vhat
            vn = (vhat * lng_ref[:, gs] + lnb_ref[:, gs]).astype(BF16)
            vn_scr[:, gs] = vn
            for ch in range(t // CHUNK):
                rows = slice(ch * CHUNK, (ch + 1) * CHUNK)
                mixed_scr[rows, gs] = jnp.dot(wt_ref[g], vn[rows], preferred_element_type=F32) + bsf_ref[:, gs]
        for g in range(GROUPS):
            gs = slice(g * gw, (g + 1) * gw)
            vhat = v_scr[:, gs]
            lg = lng_ref[:, gs]
            for ch in range(t // CHUNK):
                rows = slice(ch * CHUNK, (ch + 1) * CHUNK)
                mixed = mixed_scr[rows, gs]
                u, dgu = _gelu(p_ref[rows, g * gw:(g + 1) * gw].astype(F32))
                sz, dsz = _silu(p_ref[rows, 2 * e + g * gw:2 * e + (g + 1) * gw].astype(F32))
                sgate = u * mixed
                y_ref[rows, gs] = (sz * sgate).astype(BF16)
                dy = dy_scr[rows, gs]
                dp_ref[rows, 2 * e + g * gw:2 * e + (g + 1) * gw] = (dy * sgate * dsz).astype(BF16)
                ds = dy * sz
                dp_ref[rows, gs] = (ds * mixed * dgu).astype(BF16)
                dm = ds * u
                dbs_scr[:, gs] += dm
                dmb = dm.astype(BF16)
                dws_ref[g] += jnp.where(tril, lax.dot_general(dmb, vn_scr[rows, gs], NT_DIMS, preferred_element_type=F32), 0.0)
                dvn_scr[rows, gs] = jnp.dot(wtt_ref[g], dmb, preferred_element_type=F32)
            dvn = dvn_scr[:, gs]
            dlb_ref[:, gs] += _colsum(dvn)
            dlg_ref[:, gs] += _colsum(dvn * vhat)
            dvh = dvn * lg
            c1 = c1 + _rowsum(dvh)
            c2 = c2 + _rowsum(dvh * vhat)
            if g % span == span - 1:
                dh = dh + through_w_in(g * gw + gw - kw) + through_w_in(2 * e + g * gw + gw - kw)
        c1 = c1 * (1.0 / e)
        c2 = c2 * (1.0 / e)
        for g in range(GROUPS):
            gs = slice(g * gw, (g + 1) * gw)
            dv = rs * (dvn_scr[:, gs] * lng_ref[:, gs] - c1 - v_scr[:, gs] * c2)
            dp_ref[:, e + g * gw:e + (g + 1) * gw] = (dv * dgv_scr[:, gs]).astype(BF16)
            if g % span == span - 1:
                dh = dh + through_w_in(e + g * gw + gw - kw)
        _norm_modulate_bwd(dh, x1_ref, dx_ref, g_ref, sc_ref, dx1_ref, dsh_ref, p_scr)

        @pl.when(pl.program_id(0) == n_t - 1)
        def _():
            lane = lax.broadcasted_iota(jnp.int32, (CHUNK, 128), 1)
            acc = jnp.zeros((CHUNK, 128), F32)
            for g in range(GROUPS):
                acc = acc + jnp.where(lane == g, _rowsum(dbs_scr[:, g * gw:(g + 1) * gw]), 0.0)
            dbs_ref[...] = acc
            dsc_ref[...] = p_scr[...] * g_ref[...]
            dg_ref[...] = p_scr[...] * (1.0 + sc_ref[...])

    tok = pl.BlockSpec((t, d), lambda i: (i, 0))
    vec, evec, ws = _full((1, d)), _full((1, e)), _full((GROUPS, CHUNK, CHUNK))
    vshape = jax.ShapeDtypeStruct((1, d), F32)
    return pl.pallas_call(
        body, name=name, grid=(n_t,),
        out_shape=(jax.ShapeDtypeStruct((s, e3), BF16), jax.ShapeDtypeStruct((s, e), BF16), jax.ShapeDtypeStruct((s, d), F32),
                   jax.ShapeDtypeStruct((GROUPS, CHUNK, CHUNK), F32), jax.ShapeDtypeStruct((CHUNK, 128), F32),
                   jax.ShapeDtypeStruct((1, e), F32), jax.ShapeDtypeStruct((1, e), F32), vshape, vshape, vshape),
        in_specs=[pl.BlockSpec((t, e3), lambda i: (i, 0)), tok, tok, vec, vec, vec, evec, evec, ws, ws,
                  _resident((CHUNK, e)), _resident((e, d)), _resident((d, e3))],
        out_specs=(pl.BlockSpec((t, e3), lambda i: (i, 0)), pl.BlockSpec((t, e), lambda i: (i, 0)), tok,
                   ws, _full((CHUNK, 128)), evec, evec, vec, vec, vec),
        scratch_shapes=[pltpu.VMEM((t, e), F32), pltpu.VMEM((t, e), BF16), pltpu.VMEM((t, d), BF16),
                        pltpu.VMEM((t, e), F32), pltpu.VMEM((CHUNK, e), F32), pltpu.VMEM((1, d), F32),
                        pltpu.VMEM((t, e), F32), pltpu.VMEM((t, e), BF16), pltpu.VMEM((t, e), F32)],
        compiler_params=_params(("arbitrary",)),
    )(proj, dx2, x1, gate, g1, scale, lng, lnb, wt, wtt, bsf, wo, wi)


def _conv_mixer_bwd(proj, dx1, br, conv_a, cw, gate, wo, name, ex=None):
    s, e4 = proj.shape
    e = e4 // 4
    d = dx1.shape[1]
    t = min(s, 256)
    n_t = s // t
    cwid = min(e, 512)

    def body(p_ref, dx_ref, br_ref, conv_ref, cw_ref, gate_ref, wo_ref,
             dp_ref, y_ref, dgate_ref, dcb_ref, dcw_ref, dy_scr, head_scr):
        i = pl.program_id(0)

        @pl.when(i == 0)
        def _():
            dgate_ref[...] = jnp.zeros_like(dgate_ref)
            dcb_ref[...] = jnp.zeros_like(dcb_ref)
            dcw_ref[...] = jnp.zeros_like(dcw_ref)
            head_scr[...] = jnp.zeros_like(head_scr)
        dx = dx_ref[...]
        dgate_ref[...] += _colsum(dx * br_ref[...].astype(F32))
        dy_scr[...] = lax.dot_general((dx * gate_ref[...]).astype(BF16), wo_ref[...], NT_DIMS,
                                      preferred_element_type=F32)
        row = lax.broadcasted_iota(jnp.int32, (t, cwid), 0)
        for c0 in range(0, e, cwid):
            sl = slice(c0, c0 + cwid)
            bg = p_ref[:, c0:c0 + cwid].astype(F32)
            cg = p_ref[:, e + c0:e + c0 + cwid].astype(F32)
            xin = p_ref[:, 2 * e + c0:2 * e + c0 + cwid].astype(F32)
            z = p_ref[:, 3 * e + c0:3 * e + c0 + cwid].astype(F32)
            cx = cg * xin
            w0, w1, w2 = cw_ref[0:1, sl], cw_ref[1:2, sl], cw_ref[2:3, sl]
            conv = conv_ref[:, sl].astype(F32)
            sz, dsz = _silu(z)
            dy = dy_scr[:, sl]
            y_ref[:, sl] = (sz * bg * conv).astype(BF16)
            dp_ref[:, 3 * e + c0:3 * e + c0 + cwid] = (dy * bg * conv * dsz).astype(BF16)
            dp_ref[:, c0:c0 + cwid] = (dy * sz * conv).astype(BF16)
            dconv = dy * sz * bg
            h0, h1 = head_scr[0:1, sl], head_scr[1:2, sl]
            n1 = jnp.where(row == t - 1, h0, pltpu.roll(dconv, t - 1, 0))
            n2 = jnp.where(row == t - 2, h0, jnp.where(row == t - 1, h1, pltpu.roll(dconv, t - 2, 0)))
            dcb_ref[:, sl] += _colsum(dconv)
            dcw_ref[2:3, sl] += _colsum(dconv * cx)
            dcw_ref[1:2, sl] += _colsum(n1 * cx)
            dcw_ref[0:1, sl] += _colsum(n2 * cx)
            dcx = w2 * dconv + w1 * n1 + w0 * n2
            dp_ref[:, e + c0:e + c0 + cwid] = (dcx * xin).astype(BF16)
            dp_ref[:, 2 * e + c0:2 * e + c0 + cwid] = (dcx * cg).astype(BF16)
            head_scr[:, sl] = dconv[0:8, :]

    body = _carry(ex, body, 7, 5, lambda: pl.program_id(0) == 0, None, lambda: pl.program_id(0) == n_t - 1)
    ex_args, ex_in, ex_shapes, ex_out, ex_sems = _carried(ex)
    rev = lambda i: (n_t - 1 - i, 0)
    out = pl.pallas_call(
        body, name=name, grid=(n_t,),
        out_shape=(jax.ShapeDtypeStruct((s, e4), BF16), jax.ShapeDtypeStruct((s, e), BF16),
                   jax.ShapeDtypeStruct((1, d), F32), jax.ShapeDtypeStruct((1, e), F32), jax.ShapeDtypeStruct((8, e), F32),
                   *ex_shapes),
        in_specs=[pl.BlockSpec((t, e4), rev), pl.BlockSpec((t, d), rev), pl.BlockSpec((t, d), rev),
                  pl.BlockSpec((t, e), rev), _full((3, e)), _full((1, d)), _full((e, d)), *ex_in],
        out_specs=(pl.BlockSpec((t, e4), rev), pl.BlockSpec((t, e), rev), _full((1, d)), _full((1, e)), _full((8, e)),
                   *ex_out),
        scratch_shapes=[pltpu.VMEM((t, e), F32), pltpu.VMEM((8, e), F32), *ex_sems],
        compiler_params=_params(("arbitrary",)),
    )(proj, dx1, br, conv_a, cw, gate, wo, *ex_args)
    return (*out[:5], out[5:])


def _matmul_nt_norm_bwd(dproj, w, xin, dres, g, scale, name, ex=None):
    s, d = xin.shape
    n = w.shape[1]
    tm = min(s, 512)
    n_i = s // tm

    def body(dp_ref, w_ref, x_ref, dres_ref, g_ref, sc_ref, dx_ref, dsh_ref, dsc_ref, dg_ref, p_scr):
        i = pl.program_id(0)

        @pl.when(i == 0)
        def _():
            dsh_ref[...] = jnp.zeros_like(dsh_ref)
            p_scr[...] = jnp.zeros_like(p_scr)
        dh = lax.dot_general(dp_ref[...], w_ref[...], NT_DIMS, preferred_element_type=F32)
        _norm_modulate_bwd(dh, x_ref, dres_ref, g_ref, sc_ref, dx_ref, dsh_ref, p_scr)

        @pl.when(i == n_i - 1)
        def _():
            dsc_ref[...] = p_scr[...] * g_ref[...]
            dg_ref[...] = p_scr[...] * (1.0 + sc_ref[...])

    body = _carry(ex, body, 6, 4, lambda: pl.program_id(0) == 0, None, lambda: pl.program_id(0) == n_i - 1)
    ex_args, ex_in, ex_shapes, ex_out, ex_sems = _carried(ex)
    tok = pl.BlockSpec((tm, d), lambda i: (i, 0))
    vec = pl.BlockSpec((1, d), lambda i: (0, 0))
    vshape = jax.ShapeDtypeStruct((1, d), F32)
    out = pl.pallas_call(
        body, name=name, grid=(n_i,),
        out_shape=(jax.ShapeDtypeStruct((s, d), F32), vshape, vshape, vshape, *ex_shapes),
        in_specs=[pl.BlockSpec((tm, n), lambda i: (i, 0)), _resident((d, n)), tok, tok, vec, vec, *ex_in],
        out_specs=(tok, vec, vec, vec, *ex_out),
        scratch_shapes=[pltpu.VMEM((1, d), F32), *ex_sems],
        compiler_params=_params(("arbitrary",)),
    )(dproj, w, xin, dres, g, scale, *ex_args)
    return (*out[:4], out[4:])


def _matmul_tn(a, b, colscale, rows_split, name, ex=None, a_cols=None):
    s, m = a.shape
    a_blk = 0
    if a_cols is not None:
        a_blk, m = a_cols
    n = b.shape[1]
    n_j, tn = (1, n) if rows_split else (NDEV, n // NDEV)
    fixed = (4 + 4 + 2 * 2) * m * tn
    tk = s
    while fixed + 2 * tk * (2 * m + b.dtype.itemsize * tn) > TN_VMEM_BUDGET:
        tk //= 2
    n_k = s // tk

    def body(a_ref, b_ref, cs_ref, o_ref, acc):
        k = pl.program_id(1)
        part = lax.dot_general(a_ref[...], b_ref[...].astype(BF16), TN_DIMS, preferred_element_type=F32)
        if n_k == 1:
            o_ref[...] = (part * cs_ref[...]).astype(BF16)
            return

        @pl.when(k == 0)
        def _():
            acc[...] = part

        @pl.when((k > 0) & (k < n_k - 1))
        def _():
            acc[...] += part

        @pl.when(k == n_k - 1)
        def _():
            o_ref[...] = ((acc[...] + part) * cs_ref[...]).astype(BF16)

    at = lambda j, k: (pl.program_id(0) == j) & (pl.program_id(1) == k)
    body = _carry(ex, body, 3, 1, lambda: at(0, 0), None, lambda: at(n_j - 1, n_k - 1))
    ex_args, ex_in, ex_shapes, ex_out, ex_sems = _carried(ex)
    out = pl.pallas_call(
        body, name=name, grid=(n_j, n_k),
        out_shape=(jax.ShapeDtypeStruct((n_j, m, tn), BF16), *ex_shapes),
        in_specs=[pl.BlockSpec((tk, m), lambda j, k: (k, a_blk)), pl.BlockSpec((tk, tn), lambda j, k: (k, j)),
                  pl.BlockSpec((1, tn), lambda j, k: (0, j)), *ex_in],
        out_specs=(pl.BlockSpec((None, m, tn), lambda j, k: (j, 0, 0)), *ex_out),
        scratch_shapes=[pltpu.VMEM((m, tn), F32), *ex_sems],
        compiler_params=_params(("arbitrary", "arbitrary")),
    )(a, b, colscale, *ex_args)
    return (out[0].reshape(NDEV, m // NDEV, n) if rows_split else out[0]), out[1:]


def _adam_update(w, g, m, v):
    m = ADAM_B1 * m + (1.0 - ADAM_B1) * g
    v = ADAM_B2 * v + (1.0 - ADAM_B2) * (g * g)
    m_hat = m / (1.0 - ADAM_B1 ** ADAM_STEP)
    v_hat = v / (1.0 - ADAM_B2 ** ADAM_STEP)
    return -ADAM_LR * (m_hat / (jnp.sqrt(v_hat) + ADAM_EPS) + ADAM_WD * w), m, v


def _adamw_reduce(parts, w, m, v, name, ex=None):
    n_l = len(parts)
    n_p, _, c = parts[0].shape
    rows = [p.shape[1] for p in parts]
    r = sum(rows)
    tr = min(min(rows), 128 if ex is not None else 256)
    n_i = r // tr
    tiles = [r_l // tr for r_l in rows]
    first_tile = [sum(tiles[:l]) for l in range(n_l)]

    def body(*refs):
        p_refs, (w_ref, m_ref, v_ref, g_out, d_out, m_out, v_out) = refs[:n_l], refs[n_l:]
        g = None
        for l, p_ref in enumerate(p_refs):
            g_l = p_ref[0].astype(F32)
            for j in range(1, n_p):
                g_l = g_l + p_ref[j].astype(F32)
            g = g_l if g is None else jnp.where(pl.program_id(0) >= first_tile[l], g_l, g)
        g_out[...] = g
        d_out[...], m_out[...], v_out[...] = _adam_update(w_ref[...], g, m_ref[...], v_ref[...])

    step = lambda k: (lambda: pl.program_id(0) == k)
    body = _carry(ex, body, n_l + 3, 4, step(0), [step(n_i // 3), step((2 * n_i) // 3)], step(n_i - 1))
    ex_args, ex_in, ex_shapes, ex_out, ex_sems = _carried(ex)
    blk = pl.BlockSpec((tr, c), lambda i: (i, 0))
    p_specs = [pl.BlockSpec((n_p, tr, c), lambda i, l=l: (0, jnp.clip(i - first_tile[l], 0, tiles[l] - 1), 0))
               for l in range(n_l)]
    shp = jax.ShapeDtypeStruct((r, c), F32)
    out = pl.pallas_call(
        body, name=name, grid=(n_i,), out_shape=(shp, shp, shp, shp, *ex_shapes),
        in_specs=[*p_specs, blk, blk, blk, *ex_in],
        out_specs=(blk, blk, blk, blk, *ex_out), scratch_shapes=ex_sems,
        compiler_params=_params(("arbitrary",)),
    )(*parts, w, m, v, *ex_args)
    return (*out[:4], out[4:])


def _adamw_small(gs, ws, ms, vs, name):
    n = len(gs)

    def body(*refs):
        ins, outs = refs[:4 * n], refs[4 * n:]
        for a in range(n):
            d, m, v = _adam_update(ins[n + a][...], ins[a][...], ins[2 * n + a][...], ins[3 * n + a][...])
            outs[a][...], outs[n + a][...], outs[2 * n + a][...] = d, m, v

    shapes = tuple(jax.ShapeDtypeStruct(w.shape, F32) for w in ws) * 3
    out = pl.pallas_call(
        body, name=name, out_shape=shapes,
        in_specs=[_vmem()] * (4 * n), out_specs=tuple([_vmem()] * (3 * n)),
        compiler_params=pltpu.CompilerParams(vmem_limit_bytes=VMEM_LIMIT),
    )(*gs, *ws, *ms, *vs)
    return out[:n], out[n:2 * n], out[2 * n:]


def _sum_devices(packed, name):
    _, r, wdt = packed.shape

    def body(p_ref, o_ref):
        acc = p_ref[0]
        for j in range(1, NDEV):
            acc = acc + p_ref[j]
        o_ref[...] = acc

    return pl.pallas_call(
        body, name=name, out_shape=jax.ShapeDtypeStruct((r, wdt), F32),
        in_specs=[_vmem()], out_specs=_vmem(),
        compiler_params=pltpu.CompilerParams(vmem_limit_bytes=VMEM_LIMIT),
    )(packed)


def _mod_w_grad(c_t, dmod, name):
    n_layers, _, w3 = dmod.shape
    d = c_t.shape[0]

    def body(c_ref, dm_ref, o_ref):
        for i in range(n_layers):
            acc = c_ref[:, 0:1] * dm_ref[i, 0:1, :]
            for b in range(1, NDEV):
                acc = acc + c_ref[:, b:b + 1] * dm_ref[i, b:b + 1, :]
            o_ref[i] = acc

    return pl.pallas_call(
        body, name=name, out_shape=jax.ShapeDtypeStruct((n_layers, d, w3), F32),
        in_specs=[_vmem(), _vmem()], out_specs=_vmem(),
        compiler_params=pltpu.CompilerParams(vmem_limit_bytes=VMEM_LIMIT),
    )(c_t, dmod)


def _mask_transpose_ws(w_s, name):
    def body(w_ref, wt_ref, wtt_ref):
        tril = (lax.broadcasted_iota(jnp.int32, (CHUNK, CHUNK), 0) >= lax.broadcasted_iota(jnp.int32, (CHUNK, CHUNK), 1))
        for g in range(GROUPS):
            wm = jnp.where(tril, w_ref[g], 0.0)
            wt_ref[g] = wm.astype(BF16)
            wtt_ref[g] = wm.T.astype(BF16)

    shp = jax.ShapeDtypeStruct(w_s.shape, BF16)
    return pl.pallas_call(
        body, name=name, out_shape=(shp, shp), in_specs=[_vmem()], out_specs=(_vmem(), _vmem()),
    )(w_s)


def _pack(pieces):
    flat = jnp.concatenate([p.reshape(-1) for p in pieces])
    rows = -(-flat.shape[0] // (8 * PACK_W)) * 8
    return jnp.pad(flat, (0, rows * PACK_W - flat.shape[0])).reshape(rows, PACK_W)


def _unpack(flat, shapes):
    out, off = [], 0
    for shp in shapes:
        size = 1
        for dim in shp:
            size *= dim
        out.append(flat[off:off + size].reshape(shp))
        off += size
    return out


def kernel(x, c, mod_w, mod_b, norm_g, a_w_in, a_conv_w, a_conv_b, a_w_out, b_w_in, b_ln_g, b_ln_b, b_w_s, b_b_s, b_w_out, final_g, loss_target, m_mod_w, m_mod_b, m_norm_g, m_a_w_in, m_a_conv_w, m_a_conv_b, m_a_w_out, m_b_w_in, m_b_ln_g, m_b_ln_b, m_b_w_s, m_b_b_s, m_b_w_out, m_final_g, v_mod_w, v_mod_b, v_norm_g, v_a_w_in, v_a_conv_w, v_a_conv_b, v_a_w_out, v_b_w_in, v_b_ln_g, v_b_ln_b, v_b_w_s, v_b_b_s, v_b_w_out, v_final_g):
    s, d = x.shape[1], x.shape[2]
    es = a_w_out.shape[1]
    e = NDEV * es
    w3 = mod_w.shape[2]
    me = _index(_pos())
    x0 = x.reshape(s, d)
    tgt = loss_target.reshape(s, d)

    small = jnp.concatenate([a_conv_w[0], b_ln_g, b_ln_b, jnp.zeros((3, es), F32)], axis=0)
    gather_a = _gather_exchange([a_w_in[0].astype(BF16), a_w_out[0].astype(BF16), small], [True, False, True])
    gather_b = _gather_exchange([b_w_in[0].astype(BF16), b_w_out[0].astype(BF16)], [True, False])
    mod, c_all, (wa, woa, small_all) = _mod_vectors(c, mod_w, mod_b, gather_a)
    conv_w, ln_g, ln_b = small_all[0:3], small_all[3:4], small_all[4:5]
    bsf = jnp.repeat(b_b_s[0].T, e // GROUPS, axis=1)
    wt, wtt = _mask_transpose_ws(b_w_s[0], "mask_w_s")
    shift0, scale0, gate0 = mod[0:1, 0:d], mod[0:1, d:2 * d], mod[0:1, 2 * d:]
    shift1, scale1, gate1 = mod[1:2, 0:d], mod[1:2, d:2 * d], mod[1:2, 2 * d:]
    g0, g1, fg = norm_g[0:1], norm_g[1:2], final_g.reshape(1, d)

    proj_a, h0, x1, br_a, conv_a, (wb, wob) = _layer_a_fwd(
        x0, g0, scale0, shift0, gate0, wa, conv_w, a_conv_b, woa, "a_fwd", gather_b)
    proj_b, h1, dx2, loss_acc, dfg, dgate1, _ = _layer_b_fwd_loss(
        x1, tgt, g1, scale1, shift1, gate1, fg, wb, ln_g, ln_b, wt, bsf, wob, "b_fwd_loss")

    dproj_b, y_b, dx1, dws, dbs, dlg, dlb, dshift1, dscale1, dg1 = _layer_b_bwd(
        proj_b, dx2, x1, gate1, g1, scale1, ln_g, ln_b, wt, wtt, bsf, wob, wb, "b_bwd")
    gs_b_out, _ = _matmul_tn(y_b, dx2, gate1, True, "b_w_out_grad")
    gs_b_in, _ = _matmul_tn(h1, dproj_b, jnp.ones((1, dproj_b.shape[1]), F32), False, "b_w_in_grad")
    dproj_a, y_a, dgate0, dcb, dcw, (gr_b_in, gr_b_out) = _conv_mixer_bwd(
        proj_a, dx1, br_a, conv_a, conv_w, gate0, woa, "a_mixer_bwd", _scatter_exchange([gs_b_in, gs_b_out]))
    gs_a_out, _ = _matmul_tn(y_a, dx1, gate0, True, "a_w_out_grad")
    ones_n = jnp.ones((1, dproj_a.shape[1]), F32)
    gs_last, _ = _matmul_tn(h0, dproj_a, ones_n, False, "a_w_in_grad_last", a_cols=(3, d // 4))
    gs_rest, (gr_a_out, gr_last) = _matmul_tn(h0, dproj_a, ones_n, False, "a_w_in_grad_rest",
                                              _scatter_exchange([gs_a_out, gs_last]), a_cols=(0, 3 * d // 4))
    dx0, dshift0, dscale0, dg0, (gr_rest,) = _matmul_nt_norm_bwd(dproj_a, wa, x0, dx1, g0, scale0, "a_in_bwd",
                                                                 _scatter_exchange([gs_rest]))

    def big(parts, w, m, v, name, ex=None):
        shp = w.shape
        r2 = lambda t_: t_.reshape(-1, shp[-1])
        g, dl, nm, nv, ex_out = _adamw_reduce(parts, r2(w), r2(m), r2(v), name, ex)
        return tuple(t_.reshape(shp) for t_ in (g, dl, nm, nv)), ex_out

    res = {}
    res["b_w_in"], _ = big([gr_b_in], b_w_in, m_b_w_in, v_b_w_in, "adamw_b_w_in")
    gr_a_in = [gr_rest, gr_last]

    pieces = [dshift0, dscale0, dgate0, dshift1, dscale1, dgate1, dg0, dg1, dcb, dcw[0:3], dlg, dlb, dfg,
              dbs[:, 0:GROUPS].T, dws, loss_acc[0:1, 0:1]]
    shapes = [p.shape for p in pieces]
    res["a_w_in"], (packed_all,) = big(gr_a_in, a_w_in, m_a_w_in, v_a_w_in, "adamw_a_w_in",
                                       _gather_exchange([_pack(pieces)], [False]))
    packed_all = packed_all.reshape(NDEV, -1, PACK_W)
    total = _sum_devices(packed_all, "sum_small_grads").reshape(-1)
    (t_sh0, t_sc0, t_ga0, t_sh1, t_sc1, t_ga1, t_g0, t_g1, t_cb, t_cw, t_lg, t_lb, t_fg, t_bs, t_ws, t_loss) = _unpack(
        total, shapes)
    loss = t_loss.reshape(())
    grad_mod_b = jnp.concatenate([jnp.concatenate([t_sh0, t_sc0, t_ga0], axis=1),
                                  jnp.concatenate([t_sh1, t_sc1, t_ga1], axis=1)], axis=0)
    grad_norm_g = jnp.concatenate([t_g0, t_g1], axis=0)
    dmod_all = packed_all.reshape(NDEV, -1)[:, 0:6 * d].reshape(NDEV, 2, 3 * d).transpose(1, 0, 2)
    dmod_mine = lax.dynamic_slice_in_dim(dmod_all, me * w3, w3, axis=2)
    grad_mod_w = _mod_w_grad(c_all.T, dmod_mine, "mod_w_grad")
    grad_a_conv_w = lax.dynamic_slice_in_dim(t_cw, me * es, es, axis=1)
    grad_b_ln_g = lax.dynamic_slice_in_dim(t_lg, me * es, es, axis=1)
    grad_b_ln_b = lax.dynamic_slice_in_dim(t_lb, me * es, es, axis=1)

    res["mod_w"], _ = big([grad_mod_w.reshape(1, -1, w3)], mod_w, m_mod_w, v_mod_w, "adamw_mod_w")
    res["a_w_out"], _ = big([gr_a_out], a_w_out, m_a_w_out, v_a_w_out, "adamw_a_w_out")
    res["b_w_out"], _ = big([gr_b_out], b_w_out, m_b_w_out, v_b_w_out, "adamw_b_w_out")

    small_names = ["mod_b", "norm_g", "a_conv_w", "a_conv_b", "b_ln_g", "b_ln_b", "b_w_s", "b_b_s", "final_g"]
    small_g = [grad_mod_b, grad_norm_g, grad_a_conv_w, t_cb, grad_b_ln_g, grad_b_ln_b, t_ws, t_bs, t_fg]
    small_w = [mod_b, norm_g, a_conv_w, a_conv_b, b_ln_g, b_ln_b, b_w_s, b_b_s, final_g]
    small_m = [m_mod_b, m_norm_g, m_a_conv_w, m_a_conv_b, m_b_ln_g, m_b_ln_b, m_b_w_s, m_b_b_s, m_final_g]
    small_v = [v_mod_b, v_norm_g, v_a_conv_w, v_a_conv_b, v_b_ln_g, v_b_ln_b, v_b_w_s, v_b_b_s, v_final_g]
    as2d = lambda t_: t_.reshape(-1, t_.shape[-1])
    dls, nms, nvs = _adamw_small([as2d(t_) for t_ in small_g], [as2d(t_) for t_ in small_w],
                                 [as2d(t_) for t_ in small_m], [as2d(t_) for t_ in small_v], "adamw_small")
    for a, nme in enumerate(small_names):
        shp = small_w[a].shape
        res[nme] = (small_g[a].reshape(shp), dls[a].reshape(shp), nms[a].reshape(shp), nvs[a].reshape(shp))

    order = ["mod_w", "mod_b", "norm_g", "a_w_in", "a_conv_w", "a_conv_b", "a_w_out", "b_w_in", "b_ln_g", "b_ln_b",
             "b_w_s", "b_b_s", "b_w_out", "final_g"]
    return (loss, dx0.reshape(x.shape), *[res[k][0] for k in order], *[res[k][1] for k in order],
            *[res[k][2] for k in order], *[res[k][3] for k in order])
```

```python
import functools

import jax
import jax.numpy as jnp
from jax import lax
from jax.experimental import pallas as pl
from jax.experimental.pallas import tpu as pltpu

NDEV = 8
CHUNK = 128
GROUPS = 8
RMS_EPS = 1e-6
LN_EPS = 1e-5
ADAM_LR, ADAM_B1, ADAM_B2, ADAM_EPS, ADAM_WD, ADAM_STEP = 0.001, 0.9, 0.999, 1e-08, 0.01, 10
V7X_VMEM_BYTES = 64 * 1024 * 1024
VMEM_LIMIT = V7X_VMEM_BYTES - 8 * 1024 * 1024
TN_VMEM_BUDGET = 46 * 1024 * 1024
PACK_W = 1024
F32, BF16 = jnp.float32, jnp.bfloat16
MESH = pl.DeviceIdType.MESH
RSQRT2 = 0.7071067811865476
INV_SQRT_2PI = 0.3989422804014327
NT_DIMS = (((1,), (1,)), ((), ()))
TN_DIMS = (((0,), (0,)), ((), ()))


def _params(sem=None):
    return pltpu.CompilerParams(dimension_semantics=sem, vmem_limit_bytes=VMEM_LIMIT)


def _vmem():
    return pl.BlockSpec(memory_space=pltpu.VMEM)


def _hbm():
    return pl.BlockSpec(memory_space=pltpu.HBM)


def _full(shape):
    return pl.BlockSpec(shape, lambda *_: (0,) * len(shape))


def _pos():
    return lax.axis_index("x"), lax.axis_index("y"), lax.axis_index("c")


def _index(p):
    return 4 * p[0] + 2 * p[1] + p[2]


def _peer(k):
    x, y, c = _pos()
    return ((1 - x) if (k >> 2) & 1 else x, (1 - y) if (k >> 1) & 1 else y, (1 - c) if k & 1 else c)


def _silu(z):
    sg = jax.nn.sigmoid(z)
    return z * sg, sg * (1.0 + z * (1.0 - sg))


def _gelu(v):
    phi = 0.5 * (1.0 + lax.erf(v * RSQRT2))
    return v * phi, phi + v * (jnp.exp(-0.5 * v * v) * INV_SQRT_2PI)


def _colsum(v):
    return jnp.sum(v, axis=0, keepdims=True)


def _rowsum(v):
    return jnp.sum(v, axis=-1, keepdims=True)


def _gather_all_vmem(slab_ref, send_sems, recv_sems, base):
    me = _index(_pos())
    sends = []
    for k in range(1, NDEV):
        cp = pltpu.make_async_remote_copy(
            src_ref=slab_ref.at[me], dst_ref=slab_ref.at[me],
            send_sem=send_sems.at[base + k - 1], recv_sem=recv_sems.at[base + k - 1],
            device_id=_peer(k), device_id_type=MESH)
        cp.start()
        sends.append(cp)
    for k in range(1, NDEV):
        src = _index(_peer(k))
        pltpu.make_async_remote_copy(
            src_ref=slab_ref.at[src], dst_ref=slab_ref.at[src],
            send_sem=send_sems.at[base + k - 1], recv_sem=recv_sems.at[base + k - 1],
            device_id=_peer(k), device_id_type=MESH).wait_recv()
    for cp in sends:
        cp.wait_send()


def _mod_vectors(c, mod_w, mod_b, ex):
    n_layers, d, w3 = mod_w.shape
    r_in, r_out = len(ex.arrays), len(ex.out_shapes)

    def body(*refs):
        c_ref, mw_ref, mb_ref = refs[:3]
        ex_ins = refs[3:3 + r_in]
        mod_ref, call_ref = refs[3 + r_in:5 + r_in]
        ex_outs = refs[5 + r_in:5 + r_in + r_out]
        cslab, pslab, send_sems, recv_sems = refs[5 + r_in + r_out:9 + r_in + r_out]
        ex_sems = refs[9 + r_in + r_out:]
        ex.start(ex_ins, ex_outs, ex_sems)
        me = _index(_pos())
        cv = c_ref[...]
        cslab[me] = jnp.broadcast_to(cv * jax.nn.sigmoid(cv), (8, d))
        _gather_all_vmem(cslab, send_sems, recv_sems, 0)
        c_all = jnp.concatenate([cslab[k, 0:1, :] for k in range(NDEV)], axis=0)
        call_ref[...] = c_all
        for i in range(n_layers):
            pslab[me, i * NDEV:(i + 1) * NDEV, :] = jnp.dot(
                c_all, mw_ref[i], preferred_element_type=F32, precision=lax.Precision.HIGHEST)
        _gather_all_vmem(pslab, send_sems, recv_sems, NDEV - 1)
        for i in range(n_layers):
            for k in range(NDEV):
                mod_ref[i:i + 1, k * w3:(k + 1) * w3] = (
                    pslab[k, pl.ds(i * NDEV + me, 1), :] + mb_ref[i:i + 1, k * w3:(k + 1) * w3])
        for passing_on in ex.middles:
            passing_on(ex_ins, ex_outs, ex_sems)
        ex.finish(ex_ins, ex_outs, ex_sems)

    out = pl.pallas_call(
        body, name="mod_vectors",
        out_shape=(jax.ShapeDtypeStruct((n_layers, 3 * d), F32), jax.ShapeDtypeStruct((NDEV, d), F32), *ex.out_shapes),
        in_specs=[_vmem(), _vmem(), _vmem()] + [_hbm()] * r_in, out_specs=(_vmem(), _vmem(), *([_hbm()] * r_out)),
        scratch_shapes=[pltpu.VMEM((NDEV, 8, d), F32), pltpu.VMEM((NDEV, n_layers * NDEV, w3), F32),
                        pltpu.SemaphoreType.DMA((2 * (NDEV - 1),)), pltpu.SemaphoreType.DMA((2 * (NDEV - 1),)), *ex.sems],
        compiler_params=pltpu.CompilerParams(vmem_limit_bytes=VMEM_LIMIT),
    )(c, mod_w, mod_b, *ex.arrays)
    return out[0], out[1], out[2:]


class _Exchange:
    def __init__(self, arrays, out_shapes, sems, start, middles, finish):
        self.arrays, self.out_shapes, self.sems = list(arrays), list(out_shapes), list(sems)
        self.start, self.middles, self.finish = start, list(middles), finish


def _gather_exchange(shards, by_cols):
    n = len(shards)
    shapes = [sh.shape for sh in shards]

    def tools(ins, outs, sems):
        send_sems, recv_sems, local_sems = sems
        x, y, c = _pos()
        chips = [(1 - x, y), (x, 1 - y), (1 - x, 1 - y)]
        south = c == 0
        relayed = (jnp.where(south, 1 - x, x), jnp.where(south, y, 1 - y), c)
        relay_to = (jnp.where(south, x, 1 - x), jnp.where(south, 1 - y, y), c)

        def place(a, block):
            r, cc = shapes[a]
            if by_cols[a]:
                return outs[a].at[:, pl.ds(_index(block) * cc, cc)]
            return outs[a].at[pl.ds(_index(block) * r, r), :]

        def copy(a, k, block, to, src=None):
            dst = place(a, block)
            return pltpu.make_async_remote_copy(
                src_ref=dst if src is None else src, dst_ref=dst,
                send_sem=send_sems.at[a * 7 + k], recv_sem=recv_sems.at[a * 7 + k],
                device_id=to, device_id_type=MESH)

        mine = [pltpu.make_async_copy(ins[a], place(a, (x, y, c)), local_sems.at[a]) for a in range(n)]
        first = []
        for a in range(n):
            first.append(copy(a, 0, (x, y, c), (x, y, 1 - c), src=ins[a]))
            first += [copy(a, 1 + j, (x, y, c), (*chip, c), src=ins[a]) for j, chip in enumerate(chips[:2])]
        relays = [copy(a, 3, relayed, relay_to) for a in range(n)]
        passed = [copy(a, 4 + j, (*chip, c), (x, y, 1 - c)) for j, chip in enumerate(chips) for a in range(n)]
        return (x, y, c), chips, copy, mine, first, relays, passed

    def start(ins, outs, sems):
        _, _, _, mine, first, _, _ = tools(ins, outs, sems)
        for cp in mine + first:
            cp.start()

    def pass_neighbours(ins, outs, sems):
        (x, y, c), chips, copy, _, _, relays, passed = tools(ins, outs, sems)
        for j, chip in enumerate(chips[:2]):
            for a in range(n):
                copy(a, 1 + j, (*chip, c), (x, y, c)).wait_recv()
                passed[j * n + a].start()
        for cp in relays:
            cp.start()

    def pass_diagonal(ins, outs, sems):
        (x, y, c), chips, copy, _, _, _, passed = tools(ins, outs, sems)
        for a in range(n):
            copy(a, 3, (*chips[2], c), (x, y, c)).wait_recv()
            passed[2 * n + a].start()

    def finish(ins, outs, sems):
        (x, y, c), chips, copy, mine, first, relays, passed = tools(ins, outs, sems)
        for a in range(n):
            copy(a, 0, (x, y, 1 - c), (x, y, c)).wait_recv()
        for j, chip in enumerate(chips):
            for a in range(n):
                copy(a, 4 + j, (*chip, 1 - c), (x, y, c)).wait_recv()
        for cp in first + relays + passed:
            cp.wait_send()
        for cp in mine:
            cp.wait()

    out_shapes = [jax.ShapeDtypeStruct((r, NDEV * cc) if bc else (NDEV * r, cc), sh.dtype)
                  for (r, cc), bc, sh in zip(shapes, by_cols, shards)]
    sems = [pltpu.SemaphoreType.DMA((7 * n,)), pltpu.SemaphoreType.DMA((7 * n,)), pltpu.SemaphoreType.DMA((n,))]
    return _Exchange(shards, out_shapes, sems, start, [pass_neighbours, pass_diagonal], finish)


def _scatter_exchange(parts):
    n = len(parts)

    def tools(ins, outs, sems):
        send_sems, recv_sems, local_sems = sems
        me = _index(_pos())
        mine = [pltpu.make_async_copy(ins[a].at[me], outs[a].at[me], local_sems.at[a]) for a in range(n)]
        sends, arrivals = [], []
        for k in range(1, NDEV):
            peer = _peer(k)
            for a in range(n):
                pair = dict(send_sem=send_sems.at[a * 7 + k - 1], recv_sem=recv_sems.at[a * 7 + k - 1],
                            device_id=peer, device_id_type=MESH)
                sends.append(pltpu.make_async_remote_copy(src_ref=ins[a].at[_index(peer)], dst_ref=outs[a].at[me], **pair))
                slot = outs[a].at[_index(peer)]
                arrivals.append(pltpu.make_async_remote_copy(src_ref=slot, dst_ref=slot, **pair))
        return mine, sends, arrivals

    def start(ins, outs, sems):
        mine, sends, _ = tools(ins, outs, sems)
        for cp in mine + sends:
            cp.start()

    def finish(ins, outs, sems):
        mine, sends, arrivals = tools(ins, outs, sems)
        for cp in arrivals:
            cp.wait_recv()
        for cp in sends:
            cp.wait_send()
        for cp in mine:
            cp.wait()

    out_shapes = [jax.ShapeDtypeStruct(p.shape, p.dtype) for p in parts]
    sems = [pltpu.SemaphoreType.DMA((7 * n,)), pltpu.SemaphoreType.DMA((7 * n,)), pltpu.SemaphoreType.DMA((n,))]
    return _Exchange(parts, out_shapes, sems, start, [], finish)


def _carry(ex, body, n_in, n_out, first, middle, last):
    if ex is None:
        return body
    r_in, r_out = len(ex.arrays), len(ex.out_shapes)

    def wrapped(*refs):
        ins, rins = refs[:n_in], refs[n_in:n_in + r_in]
        outs = refs[n_in + r_in:n_in + r_in + n_out]
        routs = refs[n_in + r_in + n_out:n_in + r_in + n_out + r_out]
        rest = refs[n_in + r_in + n_out + r_out:]
        scratch, sems = rest[:len(rest) - len(ex.sems)], rest[len(rest) - len(ex.sems):]

        @pl.when(first())
        def _():
            ex.start(rins, routs, sems)

        for passing_on, at_step in zip(ex.middles, middle or []):
            pl.when(at_step())(functools.partial(passing_on, rins, routs, sems))

        body(*ins, *outs, *scratch)

        @pl.when(last())
        def _():
            ex.finish(rins, routs, sems)

    return wrapped


def _carried(ex):
    if ex is None:
        return [], [], [], [], []
    return ex.arrays, [_hbm()] * len(ex.arrays), ex.out_shapes, [_hbm()] * len(ex.out_shapes), ex.sems


def _resident(shape):
    return pl.BlockSpec(shape, lambda *_: (0,) * len(shape), pipeline_mode=pl.Buffered(1))


def _norm_modulate(x_ref, g_ref, sc_ref, sh_ref):
    xv = x_ref[...]
    r = lax.rsqrt(jnp.mean(xv * xv, axis=-1, keepdims=True) + RMS_EPS)
    return ((xv * r) * g_ref[...] * (1.0 + sc_ref[...]) + sh_ref[...]).astype(BF16)


def _conv_taps(cx, t6, t7, row):
    p1 = jnp.where(row == 0, t7, pltpu.roll(cx, 1, 0))
    p2 = jnp.where(row == 0, t6, jnp.where(row == 1, t7, pltpu.roll(cx, 2, 0)))
    return p1, p2


def _layer_a_fwd(x, g, scale, shift, gate, wi, cw, cb, wo, name, ex=None):
    s, d = x.shape
    e = wo.shape[0]
    t = min(s, 256)
    n_t = s // t
    cwid = min(e, 512)

    def body(x_ref, g_ref, sc_ref, sh_ref, gate_ref, wi_ref, cw_ref, cb_ref, wo_ref,
             proj_ref, h_ref, x1_ref, br_ref, conv_ref, y_scr, tail_scr):
        @pl.when(pl.program_id(0) == 0)
        def _():
            tail_scr[...] = jnp.zeros_like(tail_scr)
        h_ref[...] = _norm_modulate(x_ref, g_ref, sc_ref, sh_ref)
        row = lax.broadcasted_iota(jnp.int32, (t, cwid), 0)

        def project(c0):
            v = jnp.dot(h_ref[...], wi_ref[:, c0:c0 + cwid], preferred_element_type=F32)
            proj_ref[:, c0:c0 + cwid] = v.astype(BF16)
            return v

        for c0 in range(0, e, cwid):
            sl = slice(c0, c0 + cwid)
            bg, z = project(c0), project(3 * e + c0)
            cx = project(e + c0) * project(2 * e + c0)
            p1, p2 = _conv_taps(cx, tail_scr[6:7, sl], tail_scr[7:8, sl], row)
            conv = cb_ref[:, sl] + cw_ref[2:3, sl] * cx + cw_ref[0:1, sl] * p2 + cw_ref[1:2, sl] * p1
            conv_ref[:, sl] = conv.astype(BF16)
            y_scr[:, sl] = (_silu(z)[0] * bg * conv).astype(BF16)
            tail_scr[:, sl] = cx[t - 8:t, :]
        br = jnp.dot(y_scr[...], wo_ref[...], preferred_element_type=F32)
        x1_ref[...] = x_ref[...] + gate_ref[...] * br
        br_ref[...] = br.astype(BF16)

    step = lambda k: (lambda: pl.program_id(0) == k)
    body = _carry(ex, body, 9, 5, step(0), [step(n_t // 3), step((2 * n_t) // 3)], step(n_t - 1))
    ex_args, ex_in, ex_shapes, ex_out, ex_sems = _carried(ex)
    tok = pl.BlockSpec((t, d), lambda i: (i, 0))
    out = pl.pallas_call(
        body, name=name, grid=(n_t,),
        out_shape=(jax.ShapeDtypeStruct((s, 4 * e), BF16), jax.ShapeDtypeStruct((s, d), BF16),
                   jax.ShapeDtypeStruct((s, d), F32), jax.ShapeDtypeStruct((s, d), BF16),
                   jax.ShapeDtypeStruct((s, e), BF16), *ex_shapes),
        in_specs=[tok, _full((1, d)), _full((1, d)), _full((1, d)), _full((1, d)), _resident((d, 4 * e)),
                  _full((3, e)), _full((1, e)), _resident((e, d)), *ex_in],
        out_specs=(pl.BlockSpec((t, 4 * e), lambda i: (i, 0)), tok, tok, tok,
                   pl.BlockSpec((t, e), lambda i: (i, 0)), *ex_out),
        scratch_shapes=[pltpu.VMEM((t, e), BF16), pltpu.VMEM((8, e), F32), *ex_sems],
        compiler_params=_params(("arbitrary",)),
    )(x, g, scale, shift, gate, wi, cw, cb, wo, *ex_args)
    return (*out[:5], out[5:])


def _ln_stats(v_pre, v_scr, dgv_scr, t, e):
    gw = e // GROUPS
    s1 = jnp.zeros((t, 1), F32)
    for g in range(GROUPS):
        v, dgv = _gelu(v_pre(g))
        v_scr[:, g * gw:(g + 1) * gw] = v
        if dgv_scr is not None:
            dgv_scr[:, g * gw:(g + 1) * gw] = dgv.astype(dgv_scr.dtype)
        s1 = s1 + _rowsum(v)
    mu = s1 * (1.0 / e)
    s2 = jnp.zeros((t, 1), F32)
    for g in range(GROUPS):
        dv = v_scr[:, g * gw:(g + 1) * gw] - mu
        s2 = s2 + _rowsum(dv * dv)
    return mu, lax.rsqrt(s2 * (1.0 / e) + LN_EPS)


def _layer_b_fwd_loss(x1, tgt, g1, scale, shift, gate, fg, wi, lng, lnb, wt, bsf, wo, name, ex=None):
    s, d = x1.shape
    e = wo.shape[0]
    gw = e // GROUPS
    t = min(s, 256)
    n_t = s // t

    def body(x1_ref, tgt_ref, g_ref, sc_ref, sh_ref, gate_ref, fg_ref, wi_ref, lng_ref, lnb_ref, wt_ref, bsf_ref, wo_ref,
             proj_ref, h_ref, dx2_ref, loss_ref, dfg_ref, dgate_ref, v_scr, y_scr):
        @pl.when(pl.program_id(0) == 0)
        def _():
            loss_ref[...] = jnp.zeros_like(loss_ref)
            dfg_ref[...] = jnp.zeros_like(dfg_ref)
            dgate_ref[...] = jnp.zeros_like(dgate_ref)
        h_ref[...] = _norm_modulate(x1_ref, g_ref, sc_ref, sh_ref)

        def project(c0):
            v = jnp.dot(h_ref[...], wi_ref[:, c0:c0 + gw], preferred_element_type=F32)
            proj_ref[:, c0:c0 + gw] = v.astype(BF16)
            return v

        mu, rs = _ln_stats(lambda g: project(e + g * gw), v_scr, None, t, e)
        for g in range(GROUPS):
            gs = slice(g * gw, (g + 1) * gw)
            vn = (((v_scr[:, gs] - mu) * rs) * lng_ref[:, gs] + lnb_ref[:, gs]).astype(BF16)
            u = _gelu(project(g * gw))[0]
            sz = _silu(project(2 * e + g * gw))[0]
            for ch in range(t // CHUNK):
                rows = slice(ch * CHUNK, (ch + 1) * CHUNK)
                mixed = jnp.dot(wt_ref[g], vn[rows], preferred_element_type=F32) + bsf_ref[:, gs]
                y_scr[rows, gs] = (sz[rows] * (u[rows] * mixed)).astype(BF16)
        br = jnp.dot(y_scr[...], wo_ref[...], preferred_element_type=F32)
        x2 = x1_ref[...] + gate_ref[...] * br
        r2 = lax.rsqrt(jnp.mean(x2 * x2, axis=-1, keepdims=True) + RMS_EPS)
        xn = x2 * r2
        diff = xn * fg_ref[...] - tgt_ref[...]
        loss_ref[...] += jnp.broadcast_to(0.5 * _colsum(jnp.mean(diff * diff, axis=-1, keepdims=True)), loss_ref.shape)
        dout = diff * (1.0 / d)
        dfg_ref[...] += _colsum(dout * xn)
        dxn = dout * fg_ref[...]
        dx2 = r2 * (dxn - xn * jnp.mean(dxn * xn, axis=-1, keepdims=True))
        dx2_ref[...] = dx2
        dgate_ref[...] += _colsum(dx2 * br)

    step = lambda k: (lambda: pl.program_id(0) == k)
    body = _carry(ex, body, 13, 6, step(0), [step(n_t // 3), step((2 * n_t) // 3)], step(n_t - 1))
    ex_args, ex_in, ex_shapes, ex_out, ex_sems = _carried(ex)
    tok = pl.BlockSpec((t, d), lambda i: (i, 0))
    vec = _full((1, d))
    out = pl.pallas_call(
        body, name=name, grid=(n_t,),
        out_shape=(jax.ShapeDtypeStruct((s, 3 * e), BF16), jax.ShapeDtypeStruct((s, d), BF16),
                   jax.ShapeDtypeStruct((s, d), F32), jax.ShapeDtypeStruct((8, 128), F32),
                   jax.ShapeDtypeStruct((1, d), F32), jax.ShapeDtypeStruct((1, d), F32), *ex_shapes),
        in_specs=[tok, tok, vec, vec, vec, vec, vec, _resident((d, 3 * e)), _full((1, e)), _full((1, e)),
                  _full((GROUPS, CHUNK, CHUNK)), _resident((CHUNK, e)), _resident((e, d)), *ex_in],
        out_specs=(pl.BlockSpec((t, 3 * e), lambda i: (i, 0)), tok, tok, _full((8, 128)), vec, vec, *ex_out),
        scratch_shapes=[pltpu.VMEM((t, e), F32), pltpu.VMEM((t, e), BF16), *ex_sems],
        compiler_params=_params(("arbitrary",)),
    )(x1, tgt, g1, scale, shift, gate, fg, wi, lng, lnb, wt, bsf, wo, *ex_args)
    return (*out[:6], out[6:])


def _norm_modulate_bwd(dh, x_ref, dres_ref, g_ref, sc_ref, dx_ref, dsh_ref, p_scr):
    xv = x_ref[...]
    r = lax.rsqrt(jnp.mean(xv * xv, axis=-1, keepdims=True) + RMS_EPS)
    xn = xv * r
    dsh_ref[...] += _colsum(dh)
    p_scr[...] += _colsum(dh * xn)
    dxn = dh * (g_ref[...] * (1.0 + sc_ref[...]))
    dx_ref[...] = r * (dxn - xn * jnp.mean(dxn * xn, axis=-1, keepdims=True)) + dres_ref[...]


def _layer_b_bwd(proj, dx2, x1, gate, g1, scale, lng, lnb, wt, wtt, bsf, wo, wi, name):
    s, e3 = proj.shape
    e = e3 // 3
    d = dx2.shape[1]
    gw = e // GROUPS
    t = min(s, 256)
    n_t = s // t

    def body(p_ref, dx_ref, x1_ref, gate_ref, g_ref, sc_ref, lng_ref, lnb_ref, wt_ref, wtt_ref, bsf_ref, wo_ref, wi_ref,
             dp_ref, y_ref, dx1_ref, dws_ref, dbs_ref, dlg_ref, dlb_ref, dsh_ref, dsc_ref, dg_ref,
             v_scr, dgv_scr, dbr_scr, dvn_scr, dbs_scr, p_scr, dy_scr, vn_scr, mixed_scr):
        @pl.when(pl.program_id(0) == 0)
        def _():
            dws_ref[...] = jnp.zeros_like(dws_ref)
            dlg_ref[...] = jnp.zeros_like(dlg_ref)
            dlb_ref[...] = jnp.zeros_like(dlb_ref)
            dsh_ref[...] = jnp.zeros_like(dsh_ref)
            dbs_scr[...] = jnp.zeros_like(dbs_scr)
            p_scr[...] = jnp.zeros_like(p_scr)
        dbr_scr[...] = (dx_ref[...] * gate_ref[...]).astype(BF16)
        mu, rs = _ln_stats(lambda g: p_ref[:, e + g * gw:e + (g + 1) * gw].astype(F32), v_scr, dgv_scr, t, e)
        tril = (lax.broadcasted_iota(jnp.int32, (CHUNK, CHUNK), 0) >= lax.broadcasted_iota(jnp.int32, (CHUNK, CHUNK), 1))
        c1 = jnp.zeros((t, 1), F32)
        c2 = jnp.zeros((t, 1), F32)
        span = 2
        kw = span * gw
        dh = jnp.zeros((t, d), F32)

        def through_w_in(c0):
            return lax.dot_general(dp_ref[:, c0:c0 + kw], wi_ref[:, c0:c0 + kw], NT_DIMS, preferred_element_type=F32)

        dy_scr[...] = lax.dot_general(dbr_scr[...], wo_ref[...], NT_DIMS, preferred_element_type=F32)
        for g in range(GROUPS):
            gs = slice(g * gw, (g + 1) * gw)
            vhat = (v_scr[:, gs] - mu) * rs
            v_scr[:, gs] = vhat
            vn = (vhat * lng_ref[:, gs] + lnb_ref[:, gs]).astype(BF16)
            vn_scr[:, gs] = vn
            for ch in range(t // CHUNK):
                rows = slice(ch * CHUNK, (ch + 1) * CHUNK)
                mixed_scr[rows, gs] = jnp.dot(wt_ref[g], vn[rows], preferred_element_type=F32) + bsf_ref[:, gs]
        for g in range(GROUPS):
            gs = slice(g * gw, (g + 1) * gw)
            vhat = v_scr[:, gs]
            lg = lng_ref[:, gs]
            for ch in range(t // CHUNK):
                rows = slice(ch * CHUNK, (ch + 1) * CHUNK)
                mixed = mixed_scr[rows, gs]
                u, dgu = _gelu(p_ref[rows, g * gw:(g + 1) * gw].astype(F32))
                sz, dsz = _silu(p_ref[rows, 2 * e + g * gw:2 * e + (g + 1) * gw].astype(F32))
                sgate = u * mixed
                y_ref[rows, gs] = (sz * sgate).astype(BF16)
                dy = dy_scr[rows, gs]
                dp_ref[rows, 2 * e + g * gw:2 * e + (g + 1) * gw] = (dy * sgate * dsz).astype(BF16)
                ds = dy * sz
                dp_ref[rows, gs] = (ds * mixed * dgu).astype(BF16)
                dm = ds * u
                dbs_scr[:, gs] += dm
                dmb = dm.astype(BF16)
                dws_ref[g] += jnp.where(tril, lax.dot_general(dmb, vn_scr[rows, gs], NT_DIMS, preferred_element_type=F32), 0.0)
                dvn_scr[rows, gs] = jnp.dot(wtt_ref[g], dmb, preferred_element_type=F32)
            dvn = dvn_scr[:, gs]
            dlb_ref[:, gs] += _colsum(dvn)
            dlg_ref[:, gs] += _colsum(dvn * vhat)
            dvh = dvn * lg
            c1 = c1 + _rowsum(dvh)
            c2 = c2 + _rowsum(dvh * vhat)
            if g % span == span - 1:
                dh = dh + through_w_in(g * gw + gw - kw) + through_w_in(2 * e + g * gw + gw - kw)
        c1 = c1 * (1.0 / e)
        c2 = c2 * (1.0 / e)
        for g in range(GROUPS):
            gs = slice(g * gw, (g + 1) * gw)
            dv = rs * (dvn_scr[:, gs] * lng_ref[:, gs] - c1 - v_scr[:, gs] * c2)
            dp_ref[:, e + g * gw:e + (g + 1) * gw] = (dv * dgv_scr[:, gs]).astype(BF16)
            if g % span == span - 1:
                dh = dh + through_w_in(e + g * gw + gw - kw)
        _norm_modulate_bwd(dh, x1_ref, dx_ref, g_ref, sc_ref, dx1_ref, dsh_ref, p_scr)

        @pl.when(pl.program_id(0) == n_t - 1)
        def _():
            lane = lax.broadcasted_iota(jnp.int32, (CHUNK, 128), 1)
            acc = jnp.zeros((CHUNK, 128), F32)
            for g in range(GROUPS):
                acc = acc + jnp.where(lane == g, _rowsum(dbs_scr[:, g * gw:(g + 1) * gw]), 0.0)
            dbs_ref[...] = acc
            dsc_ref[...] = p_scr[...] * g_ref[...]
            dg_ref[...] = p_scr[...] * (1.0 + sc_ref[...])

    tok = pl.BlockSpec((t, d), lambda i: (i, 0))
    vec, evec, ws = _full((1, d)), _full((1, e)), _full((GROUPS, CHUNK, CHUNK))
    vshape = jax.ShapeDtypeStruct((1, d), F32)
    return pl.pallas_call(
        body, name=name, grid=(n_t,),
        out_shape=(jax.ShapeDtypeStruct((s, e3), BF16), jax.ShapeDtypeStruct((s, e), BF16), jax.ShapeDtypeStruct((s, d), F32),
                   jax.ShapeDtypeStruct((GROUPS, CHUNK, CHUNK), F32), jax.ShapeDtypeStruct((CHUNK, 128), F32),
                   jax.ShapeDtypeStruct((1, e), F32), jax.ShapeDtypeStruct((1, e), F32), vshape, vshape, vshape),
        in_specs=[pl.BlockSpec((t, e3), lambda i: (i, 0)), tok, tok, vec, vec, vec, evec, evec, ws, ws,
                  _resident((CHUNK, e)), _resident((e, d)), _resident((d, e3))],
        out_specs=(pl.BlockSpec((t, e3), lambda i: (i, 0)), pl.BlockSpec((t, e), lambda i: (i, 0)), tok,
                   ws, _full((CHUNK, 128)), evec, evec, vec, vec, vec),
        scratch_shapes=[pltpu.VMEM((t, e), F32), pltpu.VMEM((t, e), BF16), pltpu.VMEM((t, d), BF16),
                        pltpu.VMEM((t, e), F32), pltpu.VMEM((CHUNK, e), F32), pltpu.VMEM((1, d), F32),
                        pltpu.VMEM((t, e), F32), pltpu.VMEM((t, e), BF16), pltpu.VMEM((t, e), F32)],
        compiler_params=_params(("arbitrary",)),
    )(proj, dx2, x1, gate, g1, scale, lng, lnb, wt, wtt, bsf, wo, wi)


def _conv_mixer_bwd(proj, dx1, br, conv_a, cw, gate, wo, name, ex=None):
    s, e4 = proj.shape
    e = e4 // 4
    d = dx1.shape[1]
    t = min(s, 256)
    n_t = s // t
    cwid = min(e, 512)

    def body(p_ref, dx_ref, br_ref, conv_ref, cw_ref, gate_ref, wo_ref,
             dp_ref, y_ref, dgate_ref, dcb_ref, dcw_ref, dy_scr, head_scr):
        i = pl.program_id(0)

        @pl.when(i == 0)
        def _():
            dgate_ref[...] = jnp.zeros_like(dgate_ref)
            dcb_ref[...] = jnp.zeros_like(dcb_ref)
            dcw_ref[...] = jnp.zeros_like(dcw_ref)
            head_scr[...] = jnp.zeros_like(head_scr)
        dx = dx_ref[...]
        dgate_ref[...] += _colsum(dx * br_ref[...].astype(F32))
        dy_scr[...] = lax.dot_general((dx * gate_ref[...]).astype(BF16), wo_ref[...], NT_DIMS,
                                      preferred_element_type=F32)
        row = lax.broadcasted_iota(jnp.int32, (t, cwid), 0)
        for c0 in range(0, e, cwid):
            sl = slice(c0, c0 + cwid)
            bg = p_ref[:, c0:c0 + cwid].astype(F32)
            cg = p_ref[:, e + c0:e + c0 + cwid].astype(F32)
            xin = p_ref[:, 2 * e + c0:2 * e + c0 + cwid].astype(F32)
            z = p_ref[:, 3 * e + c0:3 * e + c0 + cwid].astype(F32)
            cx = cg * xin
            w0, w1, w2 = cw_ref[0:1, sl], cw_ref[1:2, sl], cw_ref[2:3, sl]
            conv = conv_ref[:, sl].astype(F32)
            sz, dsz = _silu(z)
            dy = dy_scr[:, sl]
            y_ref[:, sl] = (sz * bg * conv).astype(BF16)
            dp_ref[:, 3 * e + c0:3 * e + c0 + cwid] = (dy * bg * conv * dsz).astype(BF16)
            dp_ref[:, c0:c0 + cwid] = (dy * sz * conv).astype(BF16)
            dconv = dy * sz * bg
            h0, h1 = head_scr[0:1, sl], head_scr[1:2, sl]
            n1 = jnp.where(row == t - 1, h0, pltpu.roll(dconv, t - 1, 0))
            n2 = jnp.where(row == t - 2, h0, jnp.where(row == t - 1, h1, pltpu.roll(dconv, t - 2, 0)))
            dcb_ref[:, sl] += _colsum(dconv)
            dcw_ref[2:3, sl] += _colsum(dconv * cx)
            dcw_ref[1:2, sl] += _colsum(n1 * cx)
            dcw_ref[0:1, sl] += _colsum(n2 * cx)
            dcx = w2 * dconv + w1 * n1 + w0 * n2
            dp_ref[:, e + c0:e + c0 + cwid] = (dcx * xin).astype(BF16)
            dp_ref[:, 2 * e + c0:2 * e + c0 + cwid] = (dcx * cg).astype(BF16)
            head_scr[:, sl] = dconv[0:8, :]

    body = _carry(ex, body, 7, 5, lambda: pl.program_id(0) == 0, None, lambda: pl.program_id(0) == n_t - 1)
    ex_args, ex_in, ex_shapes, ex_out, ex_sems = _carried(ex)
    rev = lambda i: (n_t - 1 - i, 0)
    out = pl.pallas_call(
        body, name=name, grid=(n_t,),
        out_shape=(jax.ShapeDtypeStruct((s, e4), BF16), jax.ShapeDtypeStruct((s, e), BF16),
                   jax.ShapeDtypeStruct((1, d), F32), jax.ShapeDtypeStruct((1, e), F32), jax.ShapeDtypeStruct((8, e), F32),
                   *ex_shapes),
        in_specs=[pl.BlockSpec((t, e4), rev), pl.BlockSpec((t, d), rev), pl.BlockSpec((t, d), rev),
                  pl.BlockSpec((t, e), rev), _full((3, e)), _full((1, d)), _full((e, d)), *ex_in],
        out_specs=(pl.BlockSpec((t, e4), rev), pl.BlockSpec((t, e), rev), _full((1, d)), _full((1, e)), _full((8, e)),
                   *ex_out),
        scratch_shapes=[pltpu.VMEM((t, e), F32), pltpu.VMEM((8, e), F32), *ex_sems],
        compiler_params=_params(("arbitrary",)),
    )(proj, dx1, br, conv_a, cw, gate, wo, *ex_args)
    return (*out[:5], out[5:])


def _matmul_nt_norm_bwd(dproj, w, xin, dres, g, scale, name, ex=None):
    s, d = xin.shape
    n = w.shape[1]
    tm = min(s, 512)
    n_i = s // tm

    def body(dp_ref, w_ref, x_ref, dres_ref, g_ref, sc_ref, dx_ref, dsh_ref, dsc_ref, dg_ref, p_scr):
        i = pl.program_id(0)

        @pl.when(i == 0)
        def _():
            dsh_ref[...] = jnp.zeros_like(dsh_ref)
            p_scr[...] = jnp.zeros_like(p_scr)
        dh = lax.dot_general(dp_ref[...], w_ref[...], NT_DIMS, preferred_element_type=F32)
        _norm_modulate_bwd(dh, x_ref, dres_ref, g_ref, sc_ref, dx_ref, dsh_ref, p_scr)

        @pl.when(i == n_i - 1)
        def _():
            dsc_ref[...] = p_scr[...] * g_ref[...]
            dg_ref[...] = p_scr[...] * (1.0 + sc_ref[...])

    body = _carry(ex, body, 6, 4, lambda: pl.program_id(0) == 0, None, lambda: pl.program_id(0) == n_i - 1)
    ex_args, ex_in, ex_shapes, ex_out, ex_sems = _carried(ex)
    tok = pl.BlockSpec((tm, d), lambda i: (i, 0))
    vec = pl.BlockSpec((1, d), lambda i: (0, 0))
    vshape = jax.ShapeDtypeStruct((1, d), F32)
    out = pl.pallas_call(
        body, name=name, grid=(n_i,),
        out_shape=(jax.ShapeDtypeStruct((s, d), F32), vshape, vshape, vshape, *ex_shapes),
        in_specs=[pl.BlockSpec((tm, n), lambda i: (i, 0)), _resident((d, n)), tok, tok, vec, vec, *ex_in],
        out_specs=(tok, vec, vec, vec, *ex_out),
        scratch_shapes=[pltpu.VMEM((1, d), F32), *ex_sems],
        compiler_params=_params(("arbitrary",)),
    )(dproj, w, xin, dres, g, scale, *ex_args)
    return (*out[:4], out[4:])


def _matmul_tn(a, b, colscale, rows_split, name, ex=None, a_cols=None):
    s, m = a.shape
    a_blk = 0
    if a_cols is not None:
        a_blk, m = a_cols
    n = b.shape[1]
    n_j, tn = (1, n) if rows_split else (NDEV, n // NDEV)
    fixed = (4 + 4 + 2 * 2) * m * tn
    tk = s
    while fixed + 2 * tk * (2 * m + b.dtype.itemsize * tn) > TN_VMEM_BUDGET:
        tk //= 2
    n_k = s // tk

    def body(a_ref, b_ref, cs_ref, o_ref, acc):
        k = pl.program_id(1)
        part = lax.dot_general(a_ref[...], b_ref[...].astype(BF16), TN_DIMS, preferred_element_type=F32)
        if n_k == 1:
            o_ref[...] = (part * cs_ref[...]).astype(BF16)
            return

        @pl.when(k == 0)
        def _():
            acc[...] = part

        @pl.when((k > 0) & (k < n_k - 1))
        def _():
            acc[...] += part

        @pl.when(k == n_k - 1)
        def _():
            o_ref[...] = ((acc[...] + part) * cs_ref[...]).astype(BF16)

    at = lambda j, k: (pl.program_id(0) == j) & (pl.program_id(1) == k)
    body = _carry(ex, body, 3, 1, lambda: at(0, 0), None, lambda: at(n_j - 1, n_k - 1))
    ex_args, ex_in, ex_shapes, ex_out, ex_sems = _carried(ex)
    out = pl.pallas_call(
        body, name=name, grid=(n_j, n_k),
        out_shape=(jax.ShapeDtypeStruct((n_j, m, tn), BF16), *ex_shapes),
        in_specs=[pl.BlockSpec((tk, m), lambda j, k: (k, a_blk)), pl.BlockSpec((tk, tn), lambda j, k: (k, j)),
                  pl.BlockSpec((1, tn), lambda j, k: (0, j)), *ex_in],
        out_specs=(pl.BlockSpec((None, m, tn), lambda j, k: (j, 0, 0)), *ex_out),
        scratch_shapes=[pltpu.VMEM((m, tn), F32), *ex_sems],
        compiler_params=_params(("arbitrary", "arbitrary")),
    )(a, b, colscale, *ex_args)
    return (out[0].reshape(NDEV, m // NDEV, n) if rows_split else out[0]), out[1:]


def _adam_update(w, g, m, v):
    m = ADAM_B1 * m + (1.0 - ADAM_B1) * g
    v = ADAM_B2 * v + (1.0 - ADAM_B2) * (g * g)
    m_hat = m / (1.0 - ADAM_B1 ** ADAM_STEP)
    v_hat = v / (1.0 - ADAM_B2 ** ADAM_STEP)
    return -ADAM_LR * (m_hat / (jnp.sqrt(v_hat) + ADAM_EPS) + ADAM_WD * w), m, v


def _adamw_reduce(parts, w, m, v, name, ex=None):
    n_l = len(parts)
    n_p, _, c = parts[0].shape
    rows = [p.shape[1] for p in parts]
    r = sum(rows)
    tr = min(min(rows), 128 if ex is not None else 256)
    n_i = r // tr
    tiles = [r_l // tr for r_l in rows]
    first_tile = [sum(tiles[:l]) for l in range(n_l)]

    def body(*refs):
        p_refs, (w_ref, m_ref, v_ref, g_out, d_out, m_out, v_out) = refs[:n_l], refs[n_l:]
        g = None
        for l, p_ref in enumerate(p_refs):
            g_l = p_ref[0].astype(F32)
            for j in range(1, n_p):
                g_l = g_l + p_ref[j].astype(F32)
            g = g_l if g is None else jnp.where(pl.program_id(0) >= first_tile[l], g_l, g)
        g_out[...] = g
        d_out[...], m_out[...], v_out[...] = _adam_update(w_ref[...], g, m_ref[...], v_ref[...])

    step = lambda k: (lambda: pl.program_id(0) == k)
    body = _carry(ex, body, n_l + 3, 4, step(0), [step(n_i // 3), step((2 * n_i) // 3)], step(n_i - 1))
    ex_args, ex_in, ex_shapes, ex_out, ex_sems = _carried(ex)
    blk = pl.BlockSpec((tr, c), lambda i: (i, 0))
    p_specs = [pl.BlockSpec((n_p, tr, c), lambda i, l=l: (0, jnp.clip(i - first_tile[l], 0, tiles[l] - 1), 0))
               for l in range(n_l)]
    shp = jax.ShapeDtypeStruct((r, c), F32)
    out = pl.pallas_call(
        body, name=name, grid=(n_i,), out_shape=(shp, shp, shp, shp, *ex_shapes),
        in_specs=[*p_specs, blk, blk, blk, *ex_in],
        out_specs=(blk, blk, blk, blk, *ex_out), scratch_shapes=ex_sems,
        compiler_params=_params(("arbitrary",)),
    )(*parts, w, m, v, *ex_args)
    return (*out[:4], out[4:])


def _adamw_small(gs, ws, ms, vs, name):
    n = len(gs)

    def body(*refs):
        ins, outs = refs[:4 * n], refs[4 * n:]
        for a in range(n):
            d, m, v = _adam_update(ins[n + a][...], ins[a][...], ins[2 * n + a][...], ins[3 * n + a][...])
            outs[a][...], outs[n + a][...], outs[2 * n + a][...] = d, m, v

    shapes = tuple(jax.ShapeDtypeStruct(w.shape, F32) for w in ws) * 3
    out = pl.pallas_call(
        body, name=name, out_shape=shapes,
        in_specs=[_vmem()] * (4 * n), out_specs=tuple([_vmem()] * (3 * n)),
        compiler_params=pltpu.CompilerParams(vmem_limit_bytes=VMEM_LIMIT),
    )(*gs, *ws, *ms, *vs)
    return out[:n], out[n:2 * n], out[2 * n:]


def _sum_devices(packed, name):
    _, r, wdt = packed.shape

    def body(p_ref, o_ref):
        acc = p_ref[0]
        for j in range(1, NDEV):
            acc = acc + p_ref[j]
        o_ref[...] = acc

    return pl.pallas_call(
        body, name=name, out_shape=jax.ShapeDtypeStruct((r, wdt), F32),
        in_specs=[_vmem()], out_specs=_vmem(),
        compiler_params=pltpu.CompilerParams(vmem_limit_bytes=VMEM_LIMIT),
    )(packed)


def _mod_w_grad(c_t, dmod, name):
    n_layers, _, w3 = dmod.shape
    d = c_t.shape[0]

    def body(c_ref, dm_ref, o_ref):
        for i in range(n_layers):
            acc = c_ref[:, 0:1] * dm_ref[i, 0:1, :]
            for b in range(1, NDEV):
                acc = acc + c_ref[:, b:b + 1] * dm_ref[i, b:b + 1, :]
            o_ref[i] = acc

    return pl.pallas_call(
        body, name=name, out_shape=jax.ShapeDtypeStruct((n_layers, d, w3), F32),
        in_specs=[_vmem(), _vmem()], out_specs=_vmem(),
        compiler_params=pltpu.CompilerParams(vmem_limit_bytes=VMEM_LIMIT),
    )(c_t, dmod)


def _mask_transpose_ws(w_s, name):
    def body(w_ref, wt_ref, wtt_ref):
        tril = (lax.broadcasted_iota(jnp.int32, (CHUNK, CHUNK), 0) >= lax.broadcasted_iota(jnp.int32, (CHUNK, CHUNK), 1))
        for g in range(GROUPS):
            wm = jnp.where(tril, w_ref[g], 0.0)
            wt_ref[g] = wm.astype(BF16)
            wtt_ref[g] = wm.T.astype(BF16)

    shp = jax.ShapeDtypeStruct(w_s.shape, BF16)
    return pl.pallas_call(
        body, name=name, out_shape=(shp, shp), in_specs=[_vmem()], out_specs=(_vmem(), _vmem()),
    )(w_s)


def _pack(pieces):
    flat = jnp.concatenate([p.reshape(-1) for p in pieces])
    rows = -(-flat.shape[0] // (8 * PACK_W)) * 8
    return jnp.pad(flat, (0, rows * PACK_W - flat.shape[0])).reshape(rows, PACK_W)


def _unpack(flat, shapes):
    out, off = [], 0
    for shp in shapes:
        size = 1
        for dim in shp:
            size *= dim
        out.append(flat[off:off + size].reshape(shp))
        off += size
    return out


def kernel(x, c, mod_w, mod_b, norm_g, a_w_in, a_conv_w, a_conv_b, a_w_out, b_w_in, b_ln_g, b_ln_b, b_w_s, b_b_s, b_w_out, final_g, loss_target, m_mod_w, m_mod_b, m_norm_g, m_a_w_in, m_a_conv_w, m_a_conv_b, m_a_w_out, m_b_w_in, m_b_ln_g, m_b_ln_b, m_b_w_s, m_b_b_s, m_b_w_out, m_final_g, v_mod_w, v_mod_b, v_norm_g, v_a_w_in, v_a_conv_w, v_a_conv_b, v_a_w_out, v_b_w_in, v_b_ln_g, v_b_ln_b, v_b_w_s, v_b_b_s, v_b_w_out, v_final_g):
    s, d = x.shape[1], x.shape[2]
    es = a_w_out.shape[1]
    e = NDEV * es
    w3 = mod_w.shape[2]
    me = _index(_pos())
    x0 = x.reshape(s, d)
    tgt = loss_target.reshape(s, d)

    small = jnp.concatenate([a_conv_w[0], b_ln_g, b_ln_b, jnp.zeros((3, es), F32)], axis=0)
    gather_a = _gather_exchange([a_w_in[0].astype(BF16), a_w_out[0].astype(BF16), small], [True, False, True])
    gather_b = _gather_exchange([b_w_in[0].astype(BF16), b_w_out[0].astype(BF16)], [True, False])
    mod, c_all, (wa, woa, small_all) = _mod_vectors(c, mod_w, mod_b, gather_a)
    conv_w, ln_g, ln_b = small_all[0:3], small_all[3:4], small_all[4:5]
    bsf = jnp.repeat(b_b_s[0].T, e // GROUPS, axis=1)
    wt, wtt = _mask_transpose_ws(b_w_s[0], "mask_w_s")
    shift0, scale0, gate0 = mod[0:1, 0:d], mod[0:1, d:2 * d], mod[0:1, 2 * d:]
    shift1, scale1, gate1 = mod[1:2, 0:d], mod[1:2, d:2 * d], mod[1:2, 2 * d:]
    g0, g1, fg = norm_g[0:1], norm_g[1:2], final_g.reshape(1, d)

    proj_a, h0, x1, br_a, conv_a, (wb, wob) = _layer_a_fwd(
        x0, g0, scale0, shift0, gate0, wa, conv_w, a_conv_b, woa, "a_fwd", gather_b)
    proj_b, h1, dx2, loss_acc, dfg, dgate1, _ = _layer_b_fwd_loss(
        x1, tgt, g1, scale1, shift1, gate1, fg, wb, ln_g, ln_b, wt, bsf, wob, "b_fwd_loss")

    dproj_b, y_b, dx1, dws, dbs, dlg, dlb, dshift1, dscale1, dg1 = _layer_b_bwd(
        proj_b, dx2, x1, gate1, g1, scale1, ln_g, ln_b, wt, wtt, bsf, wob, wb, "b_bwd")
    gs_b_out, _ = _matmul_tn(y_b, dx2, gate1, True, "b_w_out_grad")
    gs_b_in, (gr_b_out,) = _matmul_tn(h1, dproj_b, jnp.ones((1, dproj_b.shape[1]), F32), False, "b_w_in_grad",
                                      _scatter_exchange([gs_b_out]))
    dproj_a, y_a, dgate0, dcb, dcw, (gr_b_in,) = _conv_mixer_bwd(
        proj_a, dx1, br_a, conv_a, conv_w, gate0, woa, "a_mixer_bwd", _scatter_exchange([gs_b_in]))
    gs_a_out, _ = _matmul_tn(y_a, dx1, gate0, True, "a_w_out_grad")
    ones_n = jnp.ones((1, dproj_a.shape[1]), F32)
    gs_last, _ = _matmul_tn(h0, dproj_a, ones_n, False, "a_w_in_grad_last", a_cols=(3, d // 4))
    gs_rest, (gr_a_out, gr_last) = _matmul_tn(h0, dproj_a, ones_n, False, "a_w_in_grad_rest",
                                              _scatter_exchange([gs_a_out, gs_last]), a_cols=(0, 3 * d // 4))
    dx0, dshift0, dscale0, dg0, (gr_rest,) = _matmul_nt_norm_bwd(dproj_a, wa, x0, dx1, g0, scale0, "a_in_bwd",
                                                                 _scatter_exchange([gs_rest]))

    def big(parts, w, m, v, name, ex=None):
        shp = w.shape
        r2 = lambda t_: t_.reshape(-1, shp[-1])
        g, dl, nm, nv, ex_out = _adamw_reduce(parts, r2(w), r2(m), r2(v), name, ex)
        return tuple(t_.reshape(shp) for t_ in (g, dl, nm, nv)), ex_out

    res = {}
    res["b_w_in"], _ = big([gr_b_in], b_w_in, m_b_w_in, v_b_w_in, "adamw_b_w_in")
    gr_a_in = [gr_rest, gr_last]

    pieces = [dshift0, dscale0, dgate0, dshift1, dscale1, dgate1, dg0, dg1, dcb, dcw[0:3], dlg, dlb, dfg,
              dbs[:, 0:GROUPS].T, dws, loss_acc[0:1, 0:1]]
    shapes = [p.shape for p in pieces]
    res["a_w_in"], (packed_all,) = big(gr_a_in, a_w_in, m_a_w_in, v_a_w_in, "adamw_a_w_in",
                                       _gather_exchange([_pack(pieces)], [False]))
    packed_all = packed_all.reshape(NDEV, -1, PACK_W)
    total = _sum_devices(packed_all, "sum_small_grads").reshape(-1)
    (t_sh0, t_sc0, t_ga0, t_sh1, t_sc1, t_ga1, t_g0, t_g1, t_cb, t_cw, t_lg, t_lb, t_fg, t_bs, t_ws, t_loss) = _unpack(
        total, shapes)
    loss = t_loss.reshape(())
    grad_mod_b = jnp.concatenate([jnp.concatenate([t_sh0, t_sc0, t_ga0], axis=1),
                                  jnp.concatenate([t_sh1, t_sc1, t_ga1], axis=1)], axis=0)
    grad_norm_g = jnp.concatenate([t_g0, t_g1], axis=0)
    dmod_all = packed_all.reshape(NDEV, -1)[:, 0:6 * d].reshape(NDEV, 2, 3 * d).transpose(1, 0, 2)
    dmod_mine = lax.dynamic_slice_in_dim(dmod_all, me * w3, w3, axis=2)
    grad_mod_w = _mod_w_grad(c_all.T, dmod_mine, "mod_w_grad")
    grad_a_conv_w = lax.dynamic_slice_in_dim(t_cw, me * es, es, axis=1)
    grad_b_ln_g = lax.dynamic_slice_in_dim(t_lg, me * es, es, axis=1)
    grad_b_ln_b = lax.dynamic_slice_in_dim(t_lb, me * es, es, axis=1)

    res["mod_w"], _ = big([grad_mod_w.reshape(1, -1, w3)], mod_w, m_mod_w, v_mod_w, "adamw_mod_w")
    res["a_w_out"], _ = big([gr_a_out], a_w_out, m_a_w_out, v_a_w_out, "adamw_a_w_out")
    res["b_w_out"], _ = big([gr_b_out], b_w_out, m_b_w_out, v_b_w_out, "adamw_b_w_out")

    small_names = ["mod_b", "norm_g", "a_conv_w", "a_conv_b", "b_ln_g", "b_ln_b", "b_w_s", "b_b_s", "final_g"]
    small_g = [grad_mod_b, grad_norm_g, grad_a_conv_w, t_cb, grad_b_ln_g, grad_b_ln_b, t_ws, t_bs, t_fg]
    small_w = [mod_b, norm_g, a_conv_w, a_conv_b, b_ln_g, b_ln_b, b_w_s, b_b_s, final_g]
    small_m = [m_mod_b, m_norm_g, m_a_conv_w, m_a_conv_b, m_b_ln_g, m_b_ln_b, m_b_w_s, m_b_b_s, m_final_g]
    small_v = [v_mod_b, v_norm_g, v_a_conv_w, v_a_conv_b, v_b_ln_g, v_b_ln_b, v_b_w_s, v_b_b_s, v_final_g]
    as2d = lambda t_: t_.reshape(-1, t_.shape[-1])
    dls, nms, nvs = _adamw_small([as2d(t_) for t_ in small_g], [as2d(t_) for t_ in small_w],
                                 [as2d(t_) for t_ in small_m], [as2d(t_) for t_ in small_v], "adamw_small")
    for a, nme in enumerate(small_names):
        shp = small_w[a].shape
        res[nme] = (small_g[a].reshape(shp), dls[a].reshape(shp), nms[a].reshape(shp), nvs[a].reshape(shp))

    order = ["mod_w", "mod_b", "norm_g", "a_w_in", "a_conv_w", "a_conv_b", "a_w_out", "b_w_in", "b_ln_g", "b_ln_b",
             "b_w_s", "b_b_s", "b_w_out", "final_g"]
    return (loss, dx0.reshape(x.shape), *[res[k][0] for k in order], *[res[k][1] for k in order],
            *[res[k][2] for k in order], *[res[k][3] for k in order])
```

```python
import functools

import jax
import jax.numpy as jnp
from jax import lax
from jax.experimental import pallas as pl
from jax.experimental.pallas import tpu as pltpu

NDEV = 8
CHUNK = 128
GROUPS = 8
RMS_EPS = 1e-6
LN_EPS = 1e-5
ADAM_LR, ADAM_B1, ADAM_B2, ADAM_EPS, ADAM_WD, ADAM_STEP = 0.001, 0.9, 0.999, 1e-08, 0.01, 10
V7X_VMEM_BYTES = 64 * 1024 * 1024
VMEM_LIMIT = V7X_VMEM_BYTES - 8 * 1024 * 1024
TN_VMEM_BUDGET = 46 * 1024 * 1024
PACK_W = 1024
F32, BF16 = jnp.float32, jnp.bfloat16
MESH = pl.DeviceIdType.MESH
RSQRT2 = 0.7071067811865476
INV_SQRT_2PI = 0.3989422804014327
NT_DIMS = (((1,), (1,)), ((), ()))
TN_DIMS = (((0,), (0,)), ((), ()))


def _params(sem=None):
    return pltpu.CompilerParams(dimension_semantics=sem, vmem_limit_bytes=VMEM_LIMIT)


def _vmem():
    return pl.BlockSpec(memory_space=pltpu.VMEM)


def _hbm():
    return pl.BlockSpec(memory_space=pltpu.HBM)


def _full(shape):
    return pl.BlockSpec(shape, lambda *_: (0,) * len(shape))


def _pos():
    return lax.axis_index("x"), lax.axis_index("y"), lax.axis_index("c")


def _index(p):
    return 4 * p[0] + 2 * p[1] + p[2]


def _peer(k):
    x, y, c = _pos()
    return ((1 - x) if (k >> 2) & 1 else x, (1 - y) if (k >> 1) & 1 else y, (1 - c) if k & 1 else c)


def _silu(z):
    sg = jax.nn.sigmoid(z)
    return z * sg, sg * (1.0 + z * (1.0 - sg))


def _gelu(v):
    phi = 0.5 * (1.0 + lax.erf(v * RSQRT2))
    return v * phi, phi + v * (jnp.exp(-0.5 * v * v) * INV_SQRT_2PI)


def _colsum(v):
    return jnp.sum(v, axis=0, keepdims=True)


def _rowsum(v):
    return jnp.sum(v, axis=-1, keepdims=True)


def _gather_all_vmem(slab_ref, send_sems, recv_sems, base):
    me = _index(_pos())
    sends = []
    for k in range(1, NDEV):
        cp = pltpu.make_async_remote_copy(
            src_ref=slab_ref.at[me], dst_ref=slab_ref.at[me],
            send_sem=send_sems.at[base + k - 1], recv_sem=recv_sems.at[base + k - 1],
            device_id=_peer(k), device_id_type=MESH)
        cp.start()
        sends.append(cp)
    for k in range(1, NDEV):
        src = _index(_peer(k))
        pltpu.make_async_remote_copy(
            src_ref=slab_ref.at[src], dst_ref=slab_ref.at[src],
            send_sem=send_sems.at[base + k - 1], recv_sem=recv_sems.at[base + k - 1],
            device_id=_peer(k), device_id_type=MESH).wait_recv()
    for cp in sends:
        cp.wait_send()


def _mod_vectors(c, mod_w, mod_b, ex):
    n_layers, d, w3 = mod_w.shape
    r_in, r_out = len(ex.arrays), len(ex.out_shapes)

    def body(*refs):
        c_ref, mw_ref, mb_ref = refs[:3]
        ex_ins = refs[3:3 + r_in]
        mod_ref, call_ref = refs[3 + r_in:5 + r_in]
        ex_outs = refs[5 + r_in:5 + r_in + r_out]
        cslab, pslab, send_sems, recv_sems = refs[5 + r_in + r_out:9 + r_in + r_out]
        ex_sems = refs[9 + r_in + r_out:]
        ex.start(ex_ins, ex_outs, ex_sems)
        me = _index(_pos())
        cv = c_ref[...]
        cslab[me] = jnp.broadcast_to(cv * jax.nn.sigmoid(cv), (8, d))
        _gather_all_vmem(cslab, send_sems, recv_sems, 0)
        c_all = jnp.concatenate([cslab[k, 0:1, :] for k in range(NDEV)], axis=0)
        call_ref[...] = c_all
        for i in range(n_layers):
            pslab[me, i * NDEV:(i + 1) * NDEV, :] = jnp.dot(
                c_all, mw_ref[i], preferred_element_type=F32, precision=lax.Precision.HIGHEST)
        _gather_all_vmem(pslab, send_sems, recv_sems, NDEV - 1)
        for i in range(n_layers):
            for k in range(NDEV):
                mod_ref[i:i + 1, k * w3:(k + 1) * w3] = (
                    pslab[k, pl.ds(i * NDEV + me, 1), :] + mb_ref[i:i + 1, k * w3:(k + 1) * w3])
        for passing_on in ex.middles:
            passing_on(ex_ins, ex_outs, ex_sems)
        ex.finish(ex_ins, ex_outs, ex_sems)

    out = pl.pallas_call(
        body, name="mod_vectors",
        out_shape=(jax.ShapeDtypeStruct((n_layers, 3 * d), F32), jax.ShapeDtypeStruct((NDEV, d), F32), *ex.out_shapes),
        in_specs=[_vmem(), _vmem(), _vmem()] + [_hbm()] * r_in, out_specs=(_vmem(), _vmem(), *([_hbm()] * r_out)),
        scratch_shapes=[pltpu.VMEM((NDEV, 8, d), F32), pltpu.VMEM((NDEV, n_layers * NDEV, w3), F32),
                        pltpu.SemaphoreType.DMA((2 * (NDEV - 1),)), pltpu.SemaphoreType.DMA((2 * (NDEV - 1),)), *ex.sems],
        compiler_params=pltpu.CompilerParams(vmem_limit_bytes=VMEM_LIMIT),
    )(c, mod_w, mod_b, *ex.arrays)
    return out[0], out[1], out[2:]


class _Exchange:
    def __init__(self, arrays, out_shapes, sems, start, middles, finish):
        self.arrays, self.out_shapes, self.sems = list(arrays), list(out_shapes), list(sems)
        self.start, self.middles, self.finish = start, list(middles), finish


def _gather_exchange(shards, by_cols):
    n = len(shards)
    shapes = [sh.shape for sh in shards]

    def tools(ins, outs, sems):
        send_sems, recv_sems, local_sems = sems
        x, y, c = _pos()
        chips = [(1 - x, y), (x, 1 - y), (1 - x, 1 - y)]
        south = c == 0
        relayed = (jnp.where(south, 1 - x, x), jnp.where(south, y, 1 - y), c)
        relay_to = (jnp.where(south, x, 1 - x), jnp.where(south, 1 - y, y), c)

        def place(a, block):
            r, cc = shapes[a]
            if by_cols[a]:
                return outs[a].at[:, pl.ds(_index(block) * cc, cc)]
            return outs[a].at[pl.ds(_index(block) * r, r), :]

        def copy(a, k, block, to, src=None):
            dst = place(a, block)
            return pltpu.make_async_remote_copy(
                src_ref=dst if src is None else src, dst_ref=dst,
                send_sem=send_sems.at[a * 7 + k], recv_sem=recv_sems.at[a * 7 + k],
                device_id=to, device_id_type=MESH)

        mine = [pltpu.make_async_copy(ins[a], place(a, (x, y, c)), local_sems.at[a]) for a in range(n)]
        first = []
        for a in range(n):
            first.append(copy(a, 0, (x, y, c), (x, y, 1 - c), src=ins[a]))
            first += [copy(a, 1 + j, (x, y, c), (*chip, c), src=ins[a]) for j, chip in enumerate(chips[:2])]
        relays = [copy(a, 3, relayed, relay_to) for a in range(n)]
        passed = [copy(a, 4 + j, (*chip, c), (x, y, 1 - c)) for j, chip in enumerate(chips) for a in range(n)]
        return (x, y, c), chips, copy, mine, first, relays, passed

    def start(ins, outs, sems):
        _, _, _, mine, first, _, _ = tools(ins, outs, sems)
        for cp in mine + first:
            cp.start()

    def pass_neighbours(ins, outs, sems):
        (x, y, c), chips, copy, _, _, relays, passed = tools(ins, outs, sems)
        for j, chip in enumerate(chips[:2]):
            for a in range(n):
                copy(a, 1 + j, (*chip, c), (x, y, c)).wait_recv()
                passed[j * n + a].start()
        for cp in relays:
            cp.start()

    def pass_diagonal(ins, outs, sems):
        (x, y, c), chips, copy, _, _, _, passed = tools(ins, outs, sems)
        for a in range(n):
            copy(a, 3, (*chips[2], c), (x, y, c)).wait_recv()
            passed[2 * n + a].start()

    def finish(ins, outs, sems):
        (x, y, c), chips, copy, mine, first, relays, passed = tools(ins, outs, sems)
        for a in range(n):
            copy(a, 0, (x, y, 1 - c), (x, y, c)).wait_recv()
        for j, chip in enumerate(chips):
            for a in range(n):
                copy(a, 4 + j, (*chip, 1 - c), (x, y, c)).wait_recv()
        for cp in first + relays + passed:
            cp.wait_send()
        for cp in mine:
            cp.wait()

    out_shapes = [jax.ShapeDtypeStruct((r, NDEV * cc) if bc else (NDEV * r, cc), sh.dtype)
                  for (r, cc), bc, sh in zip(shapes, by_cols, shards)]
    sems = [pltpu.SemaphoreType.DMA((7 * n,)), pltpu.SemaphoreType.DMA((7 * n,)), pltpu.SemaphoreType.DMA((n,))]
    return _Exchange(shards, out_shapes, sems, start, [pass_neighbours, pass_diagonal], finish)


def _scatter_exchange(parts):
    n = len(parts)

    def tools(ins, outs, sems):
        send_sems, recv_sems, local_sems = sems
        me = _index(_pos())
        mine = [pltpu.make_async_copy(ins[a].at[me], outs[a].at[me], local_sems.at[a]) for a in range(n)]
        sends, arrivals = [], []
        for k in range(1, NDEV):
            peer = _peer(k)
            for a in range(n):
                pair = dict(send_sem=send_sems.at[a * 7 + k - 1], recv_sem=recv_sems.at[a * 7 + k - 1],
                            device_id=peer, device_id_type=MESH)
                sends.append(pltpu.make_async_remote_copy(src_ref=ins[a].at[_index(peer)], dst_ref=outs[a].at[me], **pair))
                slot = outs[a].at[_index(peer)]
                arrivals.append(pltpu.make_async_remote_copy(src_ref=slot, dst_ref=slot, **pair))
        return mine, sends, arrivals

    def start(ins, outs, sems):
        mine, sends, _ = tools(ins, outs, sems)
        for cp in mine + sends:
            cp.start()

    def finish(ins, outs, sems):
        mine, sends, arrivals = tools(ins, outs, sems)
        for cp in arrivals:
            cp.wait_recv()
        for cp in sends:
            cp.wait_send()
        for cp in mine:
            cp.wait()

    out_shapes = [jax.ShapeDtypeStruct(p.shape, p.dtype) for p in parts]
    sems = [pltpu.SemaphoreType.DMA((7 * n,)), pltpu.SemaphoreType.DMA((7 * n,)), pltpu.SemaphoreType.DMA((n,))]
    return _Exchange(parts, out_shapes, sems, start, [], finish)


def _carry(ex, body, n_in, n_out, first, middle, last):
    if ex is None:
        return body
    r_in, r_out = len(ex.arrays), len(ex.out_shapes)

    def wrapped(*refs):
        ins, rins = refs[:n_in], refs[n_in:n_in + r_in]
        outs = refs[n_in + r_in:n_in + r_in + n_out]
        routs = refs[n_in + r_in + n_out:n_in + r_in + n_out + r_out]
        rest = refs[n_in + r_in + n_out + r_out:]
        scratch, sems = rest[:len(rest) - len(ex.sems)], rest[len(rest) - len(ex.sems):]

        @pl.when(first())
        def _():
            ex.start(rins, routs, sems)

        for passing_on, at_step in zip(ex.middles, middle or []):
            pl.when(at_step())(functools.partial(passing_on, rins, routs, sems))

        body(*ins, *outs, *scratch)

        @pl.when(last())
        def _():
            ex.finish(rins, routs, sems)

    return wrapped


def _carried(ex):
    if ex is None:
        return [], [], [], [], []
    return ex.arrays, [_hbm()] * len(ex.arrays), ex.out_shapes, [_hbm()] * len(ex.out_shapes), ex.sems


def _resident(shape):
    return pl.BlockSpec(shape, lambda *_: (0,) * len(shape), pipeline_mode=pl.Buffered(1))


def _norm_modulate(x_ref, g_ref, sc_ref, sh_ref):
    xv = x_ref[...]
    r = lax.rsqrt(jnp.mean(xv * xv, axis=-1, keepdims=True) + RMS_EPS)
    return ((xv * r) * g_ref[...] * (1.0 + sc_ref[...]) + sh_ref[...]).astype(BF16)


def _conv_taps(cx, t6, t7, row):
    p1 = jnp.where(row == 0, t7, pltpu.roll(cx, 1, 0))
    p2 = jnp.where(row == 0, t6, jnp.where(row == 1, t7, pltpu.roll(cx, 2, 0)))
    return p1, p2


def _layer_a_fwd(x, g, scale, shift, gate, wi, cw, cb, wo, name, ex=None):
    s, d = x.shape
    e = wo.shape[0]
    t = min(s, 256)
    n_t = s // t
    cwid = min(e, 512)

    def body(x_ref, g_ref, sc_ref, sh_ref, gate_ref, wi_ref, cw_ref, cb_ref, wo_ref,
             proj_ref, h_ref, x1_ref, br_ref, conv_ref, y_scr, tail_scr):
        @pl.when(pl.program_id(0) == 0)
        def _():
            tail_scr[...] = jnp.zeros_like(tail_scr)
        h_ref[...] = _norm_modulate(x_ref, g_ref, sc_ref, sh_ref)
        row = lax.broadcasted_iota(jnp.int32, (t, cwid), 0)

        def project(c0):
            v = jnp.dot(h_ref[...], wi_ref[:, c0:c0 + cwid], preferred_element_type=F32)
            proj_ref[:, c0:c0 + cwid] = v.astype(BF16)
            return v

        for c0 in range(0, e, cwid):
            sl = slice(c0, c0 + cwid)
            bg, z = project(c0), project(3 * e + c0)
            cx = project(e + c0) * project(2 * e + c0)
            p1, p2 = _conv_taps(cx, tail_scr[6:7, sl], tail_scr[7:8, sl], row)
            conv = cb_ref[:, sl] + cw_ref[2:3, sl] * cx + cw_ref[0:1, sl] * p2 + cw_ref[1:2, sl] * p1
            conv_ref[:, sl] = conv.astype(BF16)
            y_scr[:, sl] = (_silu(z)[0] * bg * conv).astype(BF16)
            tail_scr[:, sl] = cx[t - 8:t, :]
        br = jnp.dot(y_scr[...], wo_ref[...], preferred_element_type=F32)
        x1_ref[...] = x_ref[...] + gate_ref[...] * br
        br_ref[...] = br.astype(BF16)

    step = lambda k: (lambda: pl.program_id(0) == k)
    body = _carry(ex, body, 9, 5, step(0), [step(n_t // 3), step((2 * n_t) // 3)], step(n_t - 1))
    ex_args, ex_in, ex_shapes, ex_out, ex_sems = _carried(ex)
    tok = pl.BlockSpec((t, d), lambda i: (i, 0))
    out = pl.pallas_call(
        body, name=name, grid=(n_t,),
        out_shape=(jax.ShapeDtypeStruct((s, 4 * e), BF16), jax.ShapeDtypeStruct((s, d), BF16),
                   jax.ShapeDtypeStruct((s, d), F32), jax.ShapeDtypeStruct((s, d), BF16),
                   jax.ShapeDtypeStruct((s, e), BF16), *ex_shapes),
        in_specs=[tok, _full((1, d)), _full((1, d)), _full((1, d)), _full((1, d)), _resident((d, 4 * e)),
                  _full((3, e)), _full((1, e)), _resident((e, d)), *ex_in],
        out_specs=(pl.BlockSpec((t, 4 * e), lambda i: (i, 0)), tok, tok, tok,
                   pl.BlockSpec((t, e), lambda i: (i, 0)), *ex_out),
        scratch_shapes=[pltpu.VMEM((t, e), BF16), pltpu.VMEM((8, e), F32), *ex_sems],
        compiler_params=_params(("arbitrary",)),
    )(x, g, scale, shift, gate, wi, cw, cb, wo, *ex_args)
    return (*out[:5], out[5:])


def _ln_stats(v_of, v_scr, t, e):
    gw = e // GROUPS
    s1 = jnp.zeros((t, 1), F32)
    for g in range(GROUPS):
        v = v_of(g)
        v_scr[:, g * gw:(g + 1) * gw] = v
        s1 = s1 + _rowsum(v)
    mu = s1 * (1.0 / e)
    s2 = jnp.zeros((t, 1), F32)
    for g in range(GROUPS):
        dv = v_scr[:, g * gw:(g + 1) * gw] - mu
        s2 = s2 + _rowsum(dv * dv)
    return mu, lax.rsqrt(s2 * (1.0 / e) + LN_EPS)


def _layer_b_fwd_loss(x1, tgt, g1, scale, shift, gate, fg, wi, lng, lnb, wt, bsf, wo, name, ex=None):
    s, d = x1.shape
    e = wo.shape[0]
    gw = e // GROUPS
    t = min(s, 256)
    n_t = s // t

    def body(x1_ref, tgt_ref, g_ref, sc_ref, sh_ref, gate_ref, fg_ref, wi_ref, lng_ref, lnb_ref, wt_ref, bsf_ref, wo_ref,
             proj_ref, h_ref, dx2_ref, loss_ref, dfg_ref, dgate_ref, v_ref, dgv_ref, v_scr, y_scr):
        @pl.when(pl.program_id(0) == 0)
        def _():
            loss_ref[...] = jnp.zeros_like(loss_ref)
            dfg_ref[...] = jnp.zeros_like(dfg_ref)
            dgate_ref[...] = jnp.zeros_like(dgate_ref)
        h_ref[...] = _norm_modulate(x1_ref, g_ref, sc_ref, sh_ref)

        def project(c0):
            v = jnp.dot(h_ref[...], wi_ref[:, c0:c0 + gw], preferred_element_type=F32)
            proj_ref[:, c0:c0 + gw] = v.astype(BF16)
            return v

        def gelu_v(g):
            gs = slice(g * gw, (g + 1) * gw)
            v, dgv = _gelu(project(e + g * gw))
            v_ref[:, gs] = v.astype(BF16)
            dgv_ref[:, gs] = dgv.astype(BF16)
            return v

        mu, rs = _ln_stats(gelu_v, v_scr, t, e)
        for g in range(GROUPS):
            gs = slice(g * gw, (g + 1) * gw)
            vn = (((v_scr[:, gs] - mu) * rs) * lng_ref[:, gs] + lnb_ref[:, gs]).astype(BF16)
            u = _gelu(project(g * gw))[0]
            sz = _silu(project(2 * e + g * gw))[0]
            for ch in range(t // CHUNK):
                rows = slice(ch * CHUNK, (ch + 1) * CHUNK)
                mixed = jnp.dot(wt_ref[g], vn[rows], preferred_element_type=F32) + bsf_ref[:, gs]
                y_scr[rows, gs] = (sz[rows] * (u[rows] * mixed)).astype(BF16)
        br = jnp.dot(y_scr[...], wo_ref[...], preferred_element_type=F32)
        x2 = x1_ref[...] + gate_ref[...] * br
        r2 = lax.rsqrt(jnp.mean(x2 * x2, axis=-1, keepdims=True) + RMS_EPS)
        xn = x2 * r2
        diff = xn * fg_ref[...] - tgt_ref[...]
        loss_ref[...] += jnp.broadcast_to(0.5 * _colsum(jnp.mean(diff * diff, axis=-1, keepdims=True)), loss_ref.shape)
        dout = diff * (1.0 / d)
        dfg_ref[...] += _colsum(dout * xn)
        dxn = dout * fg_ref[...]
        dx2 = r2 * (dxn - xn * jnp.mean(dxn * xn, axis=-1, keepdims=True))
        dx2_ref[...] = dx2
        dgate_ref[...] += _colsum(dx2 * br)

    step = lambda k: (lambda: pl.program_id(0) == k)
    body = _carry(ex, body, 13, 8, step(0), [step(n_t // 3), step((2 * n_t) // 3)], step(n_t - 1))
    ex_args, ex_in, ex_shapes, ex_out, ex_sems = _carried(ex)
    tok = pl.BlockSpec((t, d), lambda i: (i, 0))
    vec = _full((1, d))
    out = pl.pallas_call(
        body, name=name, grid=(n_t,),
        out_shape=(jax.ShapeDtypeStruct((s, 3 * e), BF16), jax.ShapeDtypeStruct((s, d), BF16),
                   jax.ShapeDtypeStruct((s, d), F32), jax.ShapeDtypeStruct((8, 128), F32),
                   jax.ShapeDtypeStruct((1, d), F32), jax.ShapeDtypeStruct((1, d), F32),
                   jax.ShapeDtypeStruct((s, e), BF16), jax.ShapeDtypeStruct((s, e), BF16), *ex_shapes),
        in_specs=[tok, tok, vec, vec, vec, vec, vec, _resident((d, 3 * e)), _full((1, e)), _full((1, e)),
                  _full((GROUPS, CHUNK, CHUNK)), _resident((CHUNK, e)), _resident((e, d)), *ex_in],
        out_specs=(pl.BlockSpec((t, 3 * e), lambda i: (i, 0)), tok, tok, _full((8, 128)), vec, vec,
                   pl.BlockSpec((t, e), lambda i: (i, 0)), pl.BlockSpec((t, e), lambda i: (i, 0)), *ex_out),
        scratch_shapes=[pltpu.VMEM((t, e), F32), pltpu.VMEM((t, e), BF16), *ex_sems],
        compiler_params=_params(("arbitrary",)),
    )(x1, tgt, g1, scale, shift, gate, fg, wi, lng, lnb, wt, bsf, wo, *ex_args)
    return (*out[:8], out[8:])


def _norm_modulate_bwd(dh, x_ref, dres_ref, g_ref, sc_ref, dx_ref, dsh_ref, p_scr):
    xv = x_ref[...]
    r = lax.rsqrt(jnp.mean(xv * xv, axis=-1, keepdims=True) + RMS_EPS)
    xn = xv * r
    dsh_ref[...] += _colsum(dh)
    p_scr[...] += _colsum(dh * xn)
    dxn = dh * (g_ref[...] * (1.0 + sc_ref[...]))
    dx_ref[...] = r * (dxn - xn * jnp.mean(dxn * xn, axis=-1, keepdims=True)) + dres_ref[...]


def _layer_b_bwd(proj, v_act, dgv, dx2, x1, gate, g1, scale, lng, lnb, wt, wtt, bsf, wo, wi, name):
    s, e3 = proj.shape
    e = e3 // 3
    d = dx2.shape[1]
    gw = e // GROUPS
    t = min(s, 256)
    n_t = s // t

    def body(pu_ref, pz_ref, v_ref, dgv_ref, dx_ref, x1_ref, gate_ref, g_ref, sc_ref, lng_ref, lnb_ref, wt_ref, wtt_ref,
             bsf_ref, wo_ref, wi_ref,
             dp_ref, y_ref, dx1_ref, dws_ref, dbs_ref, dlg_ref, dlb_ref, dsh_ref, dsc_ref, dg_ref,
             v_scr, dbr_scr, dvn_scr, dbs_scr, p_scr, dy_scr, vn_scr, mixed_scr):
        @pl.when(pl.program_id(0) == 0)
        def _():
            dws_ref[...] = jnp.zeros_like(dws_ref)
            dlg_ref[...] = jnp.zeros_like(dlg_ref)
            dlb_ref[...] = jnp.zeros_like(dlb_ref)
            dsh_ref[...] = jnp.zeros_like(dsh_ref)
            dbs_scr[...] = jnp.zeros_like(dbs_scr)
            p_scr[...] = jnp.zeros_like(p_scr)
        dbr_scr[...] = (dx_ref[...] * gate_ref[...]).astype(BF16)
        mu, rs = _ln_stats(lambda g: v_ref[:, g * gw:(g + 1) * gw].astype(F32), v_scr, t, e)
        tril = (lax.broadcasted_iota(jnp.int32, (CHUNK, CHUNK), 0) >= lax.broadcasted_iota(jnp.int32, (CHUNK, CHUNK), 1))
        c1 = jnp.zeros((t, 1), F32)
        c2 = jnp.zeros((t, 1), F32)
        span = 2
        kw = span * gw
        dh = jnp.zeros((t, d), F32)

        def through_w_in(c0):
            return lax.dot_general(dp_ref[:, c0:c0 + kw], wi_ref[:, c0:c0 + kw], NT_DIMS, preferred_element_type=F32)

        dy_scr[...] = lax.dot_general(dbr_scr[...], wo_ref[...], NT_DIMS, preferred_element_type=F32)
        for g in range(GROUPS):
            gs = slice(g * gw, (g + 1) * gw)
            vhat = (v_scr[:, gs] - mu) * rs
            v_scr[:, gs] = vhat
            vn = (vhat * lng_ref[:, gs] + lnb_ref[:, gs]).astype(BF16)
            vn_scr[:, gs] = vn
            for ch in range(t // CHUNK):
                rows = slice(ch * CHUNK, (ch + 1) * CHUNK)
                mixed_scr[rows, gs] = jnp.dot(wt_ref[g], vn[rows], preferred_element_type=F32) + bsf_ref[:, gs]
        for g in range(GROUPS):
            gs = slice(g * gw, (g + 1) * gw)
            vhat = v_scr[:, gs]
            lg = lng_ref[:, gs]
            for ch in range(t // CHUNK):
                rows = slice(ch * CHUNK, (ch + 1) * CHUNK)
                mixed = mixed_scr[rows, gs]
                u, dgu = _gelu(pu_ref[rows, gs].astype(F32))
                sz, dsz = _silu(pz_ref[rows, gs].astype(F32))
                sgate = u * mixed
                y_ref[rows, gs] = (sz * sgate).astype(BF16)
                dy = dy_scr[rows, gs]
                dp_ref[rows, 2 * e + g * gw:2 * e + (g + 1) * gw] = (dy * sgate * dsz).astype(BF16)
                ds = dy * sz
                dp_ref[rows, gs] = (ds * mixed * dgu).astype(BF16)
                dm = ds * u
                dbs_scr[:, gs] += dm
                dmb = dm.astype(BF16)
                dws_ref[g] += jnp.where(tril, lax.dot_general(dmb, vn_scr[rows, gs], NT_DIMS, preferred_element_type=F32), 0.0)
                dvn_scr[rows, gs] = jnp.dot(wtt_ref[g], dmb, preferred_element_type=F32)
            dvn = dvn_scr[:, gs]
            dlb_ref[:, gs] += _colsum(dvn)
            dlg_ref[:, gs] += _colsum(dvn * vhat)
            dvh = dvn * lg
            c1 = c1 + _rowsum(dvh)
            c2 = c2 + _rowsum(dvh * vhat)
            if g % span == span - 1:
                dh = dh + through_w_in(g * gw + gw - kw) + through_w_in(2 * e + g * gw + gw - kw)
        c1 = c1 * (1.0 / e)
        c2 = c2 * (1.0 / e)
        for g in range(GROUPS):
            gs = slice(g * gw, (g + 1) * gw)
            dv = rs * (dvn_scr[:, gs] * lng_ref[:, gs] - c1 - v_scr[:, gs] * c2)
            dp_ref[:, e + g * gw:e + (g + 1) * gw] = (dv * dgv_ref[:, gs]).astype(BF16)
            if g % span == span - 1:
                dh = dh + through_w_in(e + g * gw + gw - kw)
        _norm_modulate_bwd(dh, x1_ref, dx_ref, g_ref, sc_ref, dx1_ref, dsh_ref, p_scr)

        @pl.when(pl.program_id(0) == n_t - 1)
        def _():
            lane = lax.broadcasted_iota(jnp.int32, (CHUNK, 128), 1)
            acc = jnp.zeros((CHUNK, 128), F32)
            for g in range(GROUPS):
                acc = acc + jnp.where(lane == g, _rowsum(dbs_scr[:, g * gw:(g + 1) * gw]), 0.0)
            dbs_ref[...] = acc
            dsc_ref[...] = p_scr[...] * g_ref[...]
            dg_ref[...] = p_scr[...] * (1.0 + sc_ref[...])

    tok = pl.BlockSpec((t, d), lambda i: (i, 0))
    vec, evec, ws = _full((1, d)), _full((1, e)), _full((GROUPS, CHUNK, CHUNK))
    vshape = jax.ShapeDtypeStruct((1, d), F32)
    return pl.pallas_call(
        body, name=name, grid=(n_t,),
        out_shape=(jax.ShapeDtypeStruct((s, e3), BF16), jax.ShapeDtypeStruct((s, e), BF16), jax.ShapeDtypeStruct((s, d), F32),
                   jax.ShapeDtypeStruct((GROUPS, CHUNK, CHUNK), F32), jax.ShapeDtypeStruct((CHUNK, 128), F32),
                   jax.ShapeDtypeStruct((1, e), F32), jax.ShapeDtypeStruct((1, e), F32), vshape, vshape, vshape),
        in_specs=[pl.BlockSpec((t, e), lambda i: (i, 0)), pl.BlockSpec((t, e), lambda i: (i, 2)),
                  pl.BlockSpec((t, e), lambda i: (i, 0)), pl.BlockSpec((t, e), lambda i: (i, 0)),
                  tok, tok, vec, vec, vec, evec, evec, ws, ws,
                  _resident((CHUNK, e)), _resident((e, d)), _resident((d, e3))],
        out_specs=(pl.BlockSpec((t, e3), lambda i: (i, 0)), pl.BlockSpec((t, e), lambda i: (i, 0)), tok,
                   ws, _full((CHUNK, 128)), evec, evec, vec, vec, vec),
        scratch_shapes=[pltpu.VMEM((t, e), F32), pltpu.VMEM((t, d), BF16),
                        pltpu.VMEM((t, e), F32), pltpu.VMEM((CHUNK, e), F32), pltpu.VMEM((1, d), F32),
                        pltpu.VMEM((t, e), F32), pltpu.VMEM((t, e), BF16), pltpu.VMEM((t, e), F32)],
        compiler_params=_params(("arbitrary",)),
    )(proj, proj, v_act, dgv, dx2, x1, gate, g1, scale, lng, lnb, wt, wtt, bsf, wo, wi)


def _conv_mixer_bwd(proj, dx1, br, conv_a, cw, gate, wo, name, ex=None):
    s, e4 = proj.shape
    e = e4 // 4
    d = dx1.shape[1]
    t = min(s, 256)
    n_t = s // t
    cwid = min(e, 512)

    def body(p_ref, dx_ref, br_ref, conv_ref, cw_ref, gate_ref, wo_ref,
             dp_ref, y_ref, dgate_ref, dcb_ref, dcw_ref, dy_scr, head_scr):
        i = pl.program_id(0)

        @pl.when(i == 0)
        def _():
            dgate_ref[...] = jnp.zeros_like(dgate_ref)
            dcb_ref[...] = jnp.zeros_like(dcb_ref)
            dcw_ref[...] = jnp.zeros_like(dcw_ref)
            head_scr[...] = jnp.zeros_like(head_scr)
        dx = dx_ref[...]
        dgate_ref[...] += _colsum(dx * br_ref[...].astype(F32))
        dy_scr[...] = lax.dot_general((dx * gate_ref[...]).astype(BF16), wo_ref[...], NT_DIMS,
                                      preferred_element_type=F32)
        row = lax.broadcasted_iota(jnp.int32, (t, cwid), 0)
        for c0 in range(0, e, cwid):
            sl = slice(c0, c0 + cwid)
            bg = p_ref[:, c0:c0 + cwid].astype(F32)
            cg = p_ref[:, e + c0:e + c0 + cwid].astype(F32)
            xin = p_ref[:, 2 * e + c0:2 * e + c0 + cwid].astype(F32)
            z = p_ref[:, 3 * e + c0:3 * e + c0 + cwid].astype(F32)
            cx = cg * xin
            w0, w1, w2 = cw_ref[0:1, sl], cw_ref[1:2, sl], cw_ref[2:3, sl]
            conv = conv_ref[:, sl].astype(F32)
            sz, dsz = _silu(z)
            dy = dy_scr[:, sl]
            y_ref[:, sl] = (sz * bg * conv).astype(BF16)
            dp_ref[:, 3 * e + c0:3 * e + c0 + cwid] = (dy * bg * conv * dsz).astype(BF16)
            dp_ref[:, c0:c0 + cwid] = (dy * sz * conv).astype(BF16)
            dconv = dy * sz * bg
            h0, h1 = head_scr[0:1, sl], head_scr[1:2, sl]
            n1 = jnp.where(row == t - 1, h0, pltpu.roll(dconv, t - 1, 0))
            n2 = jnp.where(row == t - 2, h0, jnp.where(row == t - 1, h1, pltpu.roll(dconv, t - 2, 0)))
            dcb_ref[:, sl] += _colsum(dconv)
            dcw_ref[2:3, sl] += _colsum(dconv * cx)
            dcw_ref[1:2, sl] += _colsum(n1 * cx)
            dcw_ref[0:1, sl] += _colsum(n2 * cx)
            dcx = w2 * dconv + w1 * n1 + w0 * n2
            dp_ref[:, e + c0:e + c0 + cwid] = (dcx * xin).astype(BF16)
            dp_ref[:, 2 * e + c0:2 * e + c0 + cwid] = (dcx * cg).astype(BF16)
            head_scr[:, sl] = dconv[0:8, :]

    body = _carry(ex, body, 7, 5, lambda: pl.program_id(0) == 0, None, lambda: pl.program_id(0) == n_t - 1)
    ex_args, ex_in, ex_shapes, ex_out, ex_sems = _carried(ex)
    rev = lambda i: (n_t - 1 - i, 0)
    out = pl.pallas_call(
        body, name=name, grid=(n_t,),
        out_shape=(jax.ShapeDtypeStruct((s, e4), BF16), jax.ShapeDtypeStruct((s, e), BF16),
                   jax.ShapeDtypeStruct((1, d), F32), jax.ShapeDtypeStruct((1, e), F32), jax.ShapeDtypeStruct((8, e), F32),
                   *ex_shapes),
        in_specs=[pl.BlockSpec((t, e4), rev), pl.BlockSpec((t, d), rev), pl.BlockSpec((t, d), rev),
                  pl.BlockSpec((t, e), rev), _full((3, e)), _full((1, d)), _full((e, d)), *ex_in],
        out_specs=(pl.BlockSpec((t, e4), rev), pl.BlockSpec((t, e), rev), _full((1, d)), _full((1, e)), _full((8, e)),
                   *ex_out),
        scratch_shapes=[pltpu.VMEM((t, e), F32), pltpu.VMEM((8, e), F32), *ex_sems],
        compiler_params=_params(("arbitrary",)),
    )(proj, dx1, br, conv_a, cw, gate, wo, *ex_args)
    return (*out[:5], out[5:])


def _matmul_nt_norm_bwd(dproj, w, xin, dres, g, scale, name, ex=None):
    s, d = xin.shape
    n = w.shape[1]
    tm = min(s, 512)
    n_i = s // tm

    def body(dp_ref, w_ref, x_ref, dres_ref, g_ref, sc_ref, dx_ref, dsh_ref, dsc_ref, dg_ref, p_scr):
        i = pl.program_id(0)

        @pl.when(i == 0)
        def _():
            dsh_ref[...] = jnp.zeros_like(dsh_ref)
            p_scr[...] = jnp.zeros_like(p_scr)
        dh = lax.dot_general(dp_ref[...], w_ref[...], NT_DIMS, preferred_element_type=F32)
        _norm_modulate_bwd(dh, x_ref, dres_ref, g_ref, sc_ref, dx_ref, dsh_ref, p_scr)

        @pl.when(i == n_i - 1)
        def _():
            dsc_ref[...] = p_scr[...] * g_ref[...]
            dg_ref[...] = p_scr[...] * (1.0 + sc_ref[...])

    body = _carry(ex, body, 6, 4, lambda: pl.program_id(0) == 0, None, lambda: pl.program_id(0) == n_i - 1)
    ex_args, ex_in, ex_shapes, ex_out, ex_sems = _carried(ex)
    tok = pl.BlockSpec((tm, d), lambda i: (i, 0))
    vec = pl.BlockSpec((1, d), lambda i: (0, 0))
    vshape = jax.ShapeDtypeStruct((1, d), F32)
    out = pl.pallas_call(
        body, name=name, grid=(n_i,),
        out_shape=(jax.ShapeDtypeStruct((s, d), F32), vshape, vshape, vshape, *ex_shapes),
        in_specs=[pl.BlockSpec((tm, n), lambda i: (i, 0)), _resident((d, n)), tok, tok, vec, vec, *ex_in],
        out_specs=(tok, vec, vec, vec, *ex_out),
        scratch_shapes=[pltpu.VMEM((1, d), F32), *ex_sems],
        compiler_params=_params(("arbitrary",)),
    )(dproj, w, xin, dres, g, scale, *ex_args)
    return (*out[:4], out[4:])


def _matmul_tn(a, b, colscale, rows_split, name, ex=None, a_cols=None):
    s, m = a.shape
    a_blk = 0
    if a_cols is not None:
        a_blk, m = a_cols
    n = b.shape[1]
    n_j, tn = (1, n) if rows_split else (NDEV, n // NDEV)
    fixed = (4 + 4 + 2 * 2) * m * tn
    tk = s
    while fixed + 2 * tk * (2 * m + b.dtype.itemsize * tn) > TN_VMEM_BUDGET:
        tk //= 2
    n_k = s // tk

    def body(a_ref, b_ref, cs_ref, o_ref, acc):
        k = pl.program_id(1)
        part = lax.dot_general(a_ref[...], b_ref[...].astype(BF16), TN_DIMS, preferred_element_type=F32)
        if n_k == 1:
            o_ref[...] = (part * cs_ref[...]).astype(BF16)
            return

        @pl.when(k == 0)
        def _():
            acc[...] = part

        @pl.when((k > 0) & (k < n_k - 1))
        def _():
            acc[...] += part

        @pl.when(k == n_k - 1)
        def _():
            o_ref[...] = ((acc[...] + part) * cs_ref[...]).astype(BF16)

    at = lambda j, k: (pl.program_id(0) == j) & (pl.program_id(1) == k)
    body = _carry(ex, body, 3, 1, lambda: at(0, 0), None, lambda: at(n_j - 1, n_k - 1))
    ex_args, ex_in, ex_shapes, ex_out, ex_sems = _carried(ex)
    out = pl.pallas_call(
        body, name=name, grid=(n_j, n_k),
        out_shape=(jax.ShapeDtypeStruct((n_j, m, tn), BF16), *ex_shapes),
        in_specs=[pl.BlockSpec((tk, m), lambda j, k: (k, a_blk)), pl.BlockSpec((tk, tn), lambda j, k: (k, j)),
                  pl.BlockSpec((1, tn), lambda j, k: (0, j)), *ex_in],
        out_specs=(pl.BlockSpec((None, m, tn), lambda j, k: (j, 0, 0)), *ex_out),
        scratch_shapes=[pltpu.VMEM((m, tn), F32), *ex_sems],
        compiler_params=_params(("arbitrary", "arbitrary")),
    )(a, b, colscale, *ex_args)
    return (out[0].reshape(NDEV, m // NDEV, n) if rows_split else out[0]), out[1:]


def _adam_update(w, g, m, v):
    m = ADAM_B1 * m + (1.0 - ADAM_B1) * g
    v = ADAM_B2 * v + (1.0 - ADAM_B2) * (g * g)
    m_hat = m / (1.0 - ADAM_B1 ** ADAM_STEP)
    v_hat = v / (1.0 - ADAM_B2 ** ADAM_STEP)
    return -ADAM_LR * (m_hat / (jnp.sqrt(v_hat) + ADAM_EPS) + ADAM_WD * w), m, v


def _adamw_reduce(parts, w, m, v, name, ex=None):
    n_l = len(parts)
    n_p, _, c = parts[0].shape
    rows = [p.shape[1] for p in parts]
    r = sum(rows)
    tr = min(min(rows), 128 if ex is not None else 256)
    n_i = r // tr
    tiles = [r_l // tr for r_l in rows]
    first_tile = [sum(tiles[:l]) for l in range(n_l)]

    def body(*refs):
        p_refs, (w_ref, m_ref, v_ref, g_out, d_out, m_out, v_out) = refs[:n_l], refs[n_l:]
        g = None
        for l, p_ref in enumerate(p_refs):
            g_l = p_ref[0].astype(F32)
            for j in range(1, n_p):
                g_l = g_l + p_ref[j].astype(F32)
            g = g_l if g is None else jnp.where(pl.program_id(0) >= first_tile[l], g_l, g)
        g_out[...] = g
        d_out[...], m_out[...], v_out[...] = _adam_update(w_ref[...], g, m_ref[...], v_ref[...])

    step = lambda k: (lambda: pl.program_id(0) == k)
    body = _carry(ex, body, n_l + 3, 4, step(0), [step(n_i // 3), step((2 * n_i) // 3)], step(n_i - 1))
    ex_args, ex_in, ex_shapes, ex_out, ex_sems = _carried(ex)
    blk = pl.BlockSpec((tr, c), lambda i: (i, 0))
    p_specs = [pl.BlockSpec((n_p, tr, c), lambda i, l=l: (0, jnp.clip(i - first_tile[l], 0, tiles[l] - 1), 0))
               for l in range(n_l)]
    shp = jax.ShapeDtypeStruct((r, c), F32)
    out = pl.pallas_call(
        body, name=name, grid=(n_i,), out_shape=(shp, shp, shp, shp, *ex_shapes),
        in_specs=[*p_specs, blk, blk, blk, *ex_in],
        out_specs=(blk, blk, blk, blk, *ex_out), scratch_shapes=ex_sems,
        compiler_params=_params(("arbitrary",)),
    )(*parts, w, m, v, *ex_args)
    return (*out[:4], out[4:])


def _adamw_small(gs, ws, ms, vs, name):
    n = len(gs)

    def body(*refs):
        ins, outs = refs[:4 * n], refs[4 * n:]
        for a in range(n):
            d, m, v = _adam_update(ins[n + a][...], ins[a][...], ins[2 * n + a][...], ins[3 * n + a][...])
            outs[a][...], outs[n + a][...], outs[2 * n + a][...] = d, m, v

    shapes = tuple(jax.ShapeDtypeStruct(w.shape, F32) for w in ws) * 3
    out = pl.pallas_call(
        body, name=name, out_shape=shapes,
        in_specs=[_vmem()] * (4 * n), out_specs=tuple([_vmem()] * (3 * n)),
        compiler_params=pltpu.CompilerParams(vmem_limit_bytes=VMEM_LIMIT),
    )(*gs, *ws, *ms, *vs)
    return out[:n], out[n:2 * n], out[2 * n:]


def _sum_devices(packed, name):
    _, r, wdt = packed.shape

    def body(p_ref, o_ref):
        acc = p_ref[0]
        for j in range(1, NDEV):
            acc = acc + p_ref[j]
        o_ref[...] = acc

    return pl.pallas_call(
        body, name=name, out_shape=jax.ShapeDtypeStruct((r, wdt), F32),
        in_specs=[_vmem()], out_specs=_vmem(),
        compiler_params=pltpu.CompilerParams(vmem_limit_bytes=VMEM_LIMIT),
    )(packed)


def _mod_w_grad(c_t, dmod, name):
    n_layers, _, w3 = dmod.shape
    d = c_t.shape[0]

    def body(c_ref, dm_ref, o_ref):
        for i in range(n_layers):
            acc = c_ref[:, 0:1] * dm_ref[i, 0:1, :]
            for b in range(1, NDEV):
                acc = acc + c_ref[:, b:b + 1] * dm_ref[i, b:b + 1, :]
            o_ref[i] = acc

    return pl.pallas_call(
        body, name=name, out_shape=jax.ShapeDtypeStruct((n_layers, d, w3), F32),
        in_specs=[_vmem(), _vmem()], out_specs=_vmem(),
        compiler_params=pltpu.CompilerParams(vmem_limit_bytes=VMEM_LIMIT),
    )(c_t, dmod)


def _mask_transpose_ws(w_s, name):
    def body(w_ref, wt_ref, wtt_ref):
        tril = (lax.broadcasted_iota(jnp.int32, (CHUNK, CHUNK), 0) >= lax.broadcasted_iota(jnp.int32, (CHUNK, CHUNK), 1))
        for g in range(GROUPS):
            wm = jnp.where(tril, w_ref[g], 0.0)
            wt_ref[g] = wm.astype(BF16)
            wtt_ref[g] = wm.T.astype(BF16)

    shp = jax.ShapeDtypeStruct(w_s.shape, BF16)
    return pl.pallas_call(
        body, name=name, out_shape=(shp, shp), in_specs=[_vmem()], out_specs=(_vmem(), _vmem()),
    )(w_s)


def _pack(pieces):
    flat = jnp.concatenate([p.reshape(-1) for p in pieces])
    rows = -(-flat.shape[0] // (8 * PACK_W)) * 8
    return jnp.pad(flat, (0, rows * PACK_W - flat.shape[0])).reshape(rows, PACK_W)


def _unpack(flat, shapes):
    out, off = [], 0
    for shp in shapes:
        size = 1
        for dim in shp:
            size *= dim
        out.append(flat[off:off + size].reshape(shp))
        off += size
    return out


def kernel(x, c, mod_w, mod_b, norm_g, a_w_in, a_conv_w, a_conv_b, a_w_out, b_w_in, b_ln_g, b_ln_b, b_w_s, b_b_s, b_w_out, final_g, loss_target, m_mod_w, m_mod_b, m_norm_g, m_a_w_in, m_a_conv_w, m_a_conv_b, m_a_w_out, m_b_w_in, m_b_ln_g, m_b_ln_b, m_b_w_s, m_b_b_s, m_b_w_out, m_final_g, v_mod_w, v_mod_b, v_norm_g, v_a_w_in, v_a_conv_w, v_a_conv_b, v_a_w_out, v_b_w_in, v_b_ln_g, v_b_ln_b, v_b_w_s, v_b_b_s, v_b_w_out, v_final_g):
    s, d = x.shape[1], x.shape[2]
    es = a_w_out.shape[1]
    e = NDEV * es
    w3 = mod_w.shape[2]
    me = _index(_pos())
    x0 = x.reshape(s, d)
    tgt = loss_target.reshape(s, d)

    small = jnp.concatenate([a_conv_w[0], b_ln_g, b_ln_b, jnp.zeros((3, es), F32)], axis=0)
    gather_a = _gather_exchange([a_w_in[0].astype(BF16), a_w_out[0].astype(BF16), small], [True, False, True])
    gather_b = _gather_exchange([b_w_in[0].astype(BF16), b_w_out[0].astype(BF16)], [True, False])
    mod, c_all, (wa, woa, small_all) = _mod_vectors(c, mod_w, mod_b, gather_a)
    conv_w, ln_g, ln_b = small_all[0:3], small_all[3:4], small_all[4:5]
    bsf = jnp.repeat(b_b_s[0].T, e // GROUPS, axis=1)
    wt, wtt = _mask_transpose_ws(b_w_s[0], "mask_w_s")
    shift0, scale0, gate0 = mod[0:1, 0:d], mod[0:1, d:2 * d], mod[0:1, 2 * d:]
    shift1, scale1, gate1 = mod[1:2, 0:d], mod[1:2, d:2 * d], mod[1:2, 2 * d:]
    g0, g1, fg = norm_g[0:1], norm_g[1:2], final_g.reshape(1, d)

    proj_a, h0, x1, br_a, conv_a, (wb, wob) = _layer_a_fwd(
        x0, g0, scale0, shift0, gate0, wa, conv_w, a_conv_b, woa, "a_fwd", gather_b)
    proj_b, h1, dx2, loss_acc, dfg, dgate1, v_b, dgv_b, _ = _layer_b_fwd_loss(
        x1, tgt, g1, scale1, shift1, gate1, fg, wb, ln_g, ln_b, wt, bsf, wob, "b_fwd_loss")

    dproj_b, y_b, dx1, dws, dbs, dlg, dlb, dshift1, dscale1, dg1 = _layer_b_bwd(
        proj_b, v_b, dgv_b, dx2, x1, gate1, g1, scale1, ln_g, ln_b, wt, wtt, bsf, wob, wb, "b_bwd")
    gs_b_out, _ = _matmul_tn(y_b, dx2, gate1, True, "b_w_out_grad")
    gs_b_in, (gr_b_out,) = _matmul_tn(h1, dproj_b, jnp.ones((1, dproj_b.shape[1]), F32), False, "b_w_in_grad",
                                      _scatter_exchange([gs_b_out]))
    dproj_a, y_a, dgate0, dcb, dcw, (gr_b_in,) = _conv_mixer_bwd(
        proj_a, dx1, br_a, conv_a, conv_w, gate0, woa, "a_mixer_bwd", _scatter_exchange([gs_b_in]))
    gs_a_out, _ = _matmul_tn(y_a, dx1, gate0, True, "a_w_out_grad")
    ones_n = jnp.ones((1, dproj_a.shape[1]), F32)
    gs_last, _ = _matmul_tn(h0, dproj_a, ones_n, False, "a_w_in_grad_last", a_cols=(3, d // 4))
    gs_rest, (gr_a_out, gr_last) = _matmul_tn(h0, dproj_a, ones_n, False, "a_w_in_grad_rest",
                                              _scatter_exchange([gs_a_out, gs_last]), a_cols=(0, 3 * d // 4))
    dx0, dshift0, dscale0, dg0, (gr_rest,) = _matmul_nt_norm_bwd(dproj_a, wa, x0, dx1, g0, scale0, "a_in_bwd",
                                                                 _scatter_exchange([gs_rest]))

    def big(parts, w, m, v, name, ex=None):
        shp = w.shape
        r2 = lambda t_: t_.reshape(-1, shp[-1])
        g, dl, nm, nv, ex_out = _adamw_reduce(parts, r2(w), r2(m), r2(v), name, ex)
        return tuple(t_.reshape(shp) for t_ in (g, dl, nm, nv)), ex_out

    res = {}
    res["b_w_in"], _ = big([gr_b_in], b_w_in, m_b_w_in, v_b_w_in, "adamw_b_w_in")
    gr_a_in = [gr_rest, gr_last]

    pieces = [dshift0, dscale0, dgate0, dshift1, dscale1, dgate1, dg0, dg1, dcb, dcw[0:3], dlg, dlb, dfg,
              dbs[:, 0:GROUPS].T, dws, loss_acc[0:1, 0:1]]
    shapes = [p.shape for p in pieces]
    res["a_w_in"], (packed_all,) = big(gr_a_in, a_w_in, m_a_w_in, v_a_w_in, "adamw_a_w_in",
                                       _gather_exchange([_pack(pieces)], [False]))
    packed_all = packed_all.reshape(NDEV, -1, PACK_W)
    total = _sum_devices(packed_all, "sum_small_grads").reshape(-1)
    (t_sh0, t_sc0, t_ga0, t_sh1, t_sc1, t_ga1, t_g0, t_g1, t_cb, t_cw, t_lg, t_lb, t_fg, t_bs, t_ws, t_loss) = _unpack(
        total, shapes)
    loss = t_loss.reshape(())
    grad_mod_b = jnp.concatenate([jnp.concatenate([t_sh0, t_sc0, t_ga0], axis=1),
                                  jnp.concatenate([t_sh1, t_sc1, t_ga1], axis=1)], axis=0)
    grad_norm_g = jnp.concatenate([t_g0, t_g1], axis=0)
    dmod_all = packed_all.reshape(NDEV, -1)[:, 0:6 * d].reshape(NDEV, 2, 3 * d).transpose(1, 0, 2)
    dmod_mine = lax.dynamic_slice_in_dim(dmod_all, me * w3, w3, axis=2)
    grad_mod_w = _mod_w_grad(c_all.T, dmod_mine, "mod_w_grad")
    grad_a_conv_w = lax.dynamic_slice_in_dim(t_cw, me * es, es, axis=1)
    grad_b_ln_g = lax.dynamic_slice_in_dim(t_lg, me * es, es, axis=1)
    grad_b_ln_b = lax.dynamic_slice_in_dim(t_lb, me * es, es, axis=1)

    res["mod_w"], _ = big([grad_mod_w.reshape(1, -1, w3)], mod_w, m_mod_w, v_mod_w, "adamw_mod_w")
    res["a_w_out"], _ = big([gr_a_out], a_w_out, m_a_w_out, v_a_w_out, "adamw_a_w_out")
    res["b_w_out"], _ = big([gr_b_out], b_w_out, m_b_w_out, v_b_w_out, "adamw_b_w_out")

    small_names = ["mod_b", "norm_g", "a_conv_w", "a_conv_b", "b_ln_g", "b_ln_b", "b_w_s", "b_b_s", "final_g"]
    small_g = [grad_mod_b, grad_norm_g, grad_a_conv_w, t_cb, grad_b_ln_g, grad_b_ln_b, t_ws, t_bs, t_fg]
    small_w = [mod_b, norm_g, a_conv_w, a_conv_b, b_ln_g, b_ln_b, b_w_s, b_b_s, final_g]
    small_m = [m_mod_b, m_norm_g, m_a_conv_w, m_a_conv_b, m_b_ln_g, m_b_ln_b, m_b_w_s, m_b_b_s, m_final_g]
    small_v = [v_mod_b, v_norm_g, v_a_conv_w, v_a_conv_b, v_b_ln_g, v_b_ln_b, v_b_w_s, v_b_b_s, v_final_g]
    as2d = lambda t_: t_.reshape(-1, t_.shape[-1])
    dls, nms, nvs = _adamw_small([as2d(t_) for t_ in small_g], [as2d(t_) for t_ in small_w],
                                 [as2d(t_) for t_ in small_m], [as2d(t_) for t_ in small_v], "adamw_small")
    for a, nme in enumerate(small_names):
        shp = small_w[a].shape
        res[nme] = (small_g[a].reshape(shp), dls[a].reshape(shp), nms[a].reshape(shp), nvs[a].reshape(shp))

    order = ["mod_w", "mod_b", "norm_g", "a_w_in", "a_conv_w", "a_conv_b", "a_w_out", "b_w_in", "b_ln_g", "b_ln_b",
             "b_w_s", "b_b_s", "b_w_out", "final_g"]
    return (loss, dx0.reshape(x.shape), *[res[k][0] for k in order], *[res[k][1] for k in order],
            *[res[k][2] for k in order], *[res[k][3] for k in order])
```

```python
import functools

import jax
import jax.numpy as jnp
from jax import lax
from jax.experimental import pallas as pl
from jax.experimental.pallas import tpu as pltpu

NDEV = 8
CHUNK = 128
GROUPS = 8
RMS_EPS = 1e-6
LN_EPS = 1e-5
ADAM_LR, ADAM_B1, ADAM_B2, ADAM_EPS, ADAM_WD, ADAM_STEP = 0.001, 0.9, 0.999, 1e-08, 0.01, 10
V7X_VMEM_BYTES = 64 * 1024 * 1024
VMEM_LIMIT = V7X_VMEM_BYTES - 8 * 1024 * 1024
TN_VMEM_BUDGET = 46 * 1024 * 1024
PACK_W = 1024
F32, BF16 = jnp.float32, jnp.bfloat16
MESH = pl.DeviceIdType.MESH
RSQRT2 = 0.7071067811865476
INV_SQRT_2PI = 0.3989422804014327
NT_DIMS = (((1,), (1,)), ((), ()))
TN_DIMS = (((0,), (0,)), ((), ()))


def _params(sem=None):
    return pltpu.CompilerParams(dimension_semantics=sem, vmem_limit_bytes=VMEM_LIMIT)


def _vmem():
    return pl.BlockSpec(memory_space=pltpu.VMEM)


def _hbm():
    return pl.BlockSpec(memory_space=pltpu.HBM)


def _full(shape):
    return pl.BlockSpec(shape, lambda *_: (0,) * len(shape))


def _pos():
    return lax.axis_index("x"), lax.axis_index("y"), lax.axis_index("c")


def _index(p):
    return 4 * p[0] + 2 * p[1] + p[2]


def _peer(k):
    x, y, c = _pos()
    return ((1 - x) if (k >> 2) & 1 else x, (1 - y) if (k >> 1) & 1 else y, (1 - c) if k & 1 else c)


def _silu(z):
    sg = jax.nn.sigmoid(z)
    return z * sg, sg * (1.0 + z * (1.0 - sg))


def _gelu(v):
    phi = 0.5 * (1.0 + lax.erf(v * RSQRT2))
    return v * phi, phi + v * (jnp.exp(-0.5 * v * v) * INV_SQRT_2PI)


def _colsum(v):
    return jnp.sum(v, axis=0, keepdims=True)


def _rowsum(v):
    return jnp.sum(v, axis=-1, keepdims=True)


def _gather_all_vmem(slab_ref, send_sems, recv_sems, base):
    me = _index(_pos())
    sends = []
    for k in range(1, NDEV):
        cp = pltpu.make_async_remote_copy(
            src_ref=slab_ref.at[me], dst_ref=slab_ref.at[me],
            send_sem=send_sems.at[base + k - 1], recv_sem=recv_sems.at[base + k - 1],
            device_id=_peer(k), device_id_type=MESH)
        cp.start()
        sends.append(cp)
    for k in range(1, NDEV):
        src = _index(_peer(k))
        pltpu.make_async_remote_copy(
            src_ref=slab_ref.at[src], dst_ref=slab_ref.at[src],
            send_sem=send_sems.at[base + k - 1], recv_sem=recv_sems.at[base + k - 1],
            device_id=_peer(k), device_id_type=MESH).wait_recv()
    for cp in sends:
        cp.wait_send()


def _mod_vectors(c, mod_w, mod_b, ex):
    n_layers, d, w3 = mod_w.shape
    r_in, r_out = len(ex.arrays), len(ex.out_shapes)

    def body(*refs):
        c_ref, mw_ref, mb_ref = refs[:3]
        ex_ins = refs[3:3 + r_in]
        mod_ref, call_ref = refs[3 + r_in:5 + r_in]
        ex_outs = refs[5 + r_in:5 + r_in + r_out]
        cslab, pslab, send_sems, recv_sems = refs[5 + r_in + r_out:9 + r_in + r_out]
        ex_sems = refs[9 + r_in + r_out:]
        ex.start(ex_ins, ex_outs, ex_sems)
        me = _index(_pos())
        cv = c_ref[...]
        cslab[me] = jnp.broadcast_to(cv * jax.nn.sigmoid(cv), (8, d))
        _gather_all_vmem(cslab, send_sems, recv_sems, 0)
        c_all = jnp.concatenate([cslab[k, 0:1, :] for k in range(NDEV)], axis=0)
        call_ref[...] = c_all
        for i in range(n_layers):
            pslab[me, i * NDEV:(i + 1) * NDEV, :] = jnp.dot(
                c_all, mw_ref[i], preferred_element_type=F32, precision=lax.Precision.HIGHEST)
        _gather_all_vmem(pslab, send_sems, recv_sems, NDEV - 1)
        for i in range(n_layers):
            for k in range(NDEV):
                mod_ref[i:i + 1, k * w3:(k + 1) * w3] = (
                    pslab[k, pl.ds(i * NDEV + me, 1), :] + mb_ref[i:i + 1, k * w3:(k + 1) * w3])
        for passing_on in ex.middles:
            passing_on(ex_ins, ex_outs, ex_sems)
        ex.finish(ex_ins, ex_outs, ex_sems)

    out = pl.pallas_call(
        body, name="mod_vectors",
        out_shape=(jax.ShapeDtypeStruct((n_layers, 3 * d), F32), jax.ShapeDtypeStruct((NDEV, d), F32), *ex.out_shapes),
        in_specs=[_vmem(), _vmem(), _vmem()] + [_hbm()] * r_in, out_specs=(_vmem(), _vmem(), *([_hbm()] * r_out)),
        scratch_shapes=[pltpu.VMEM((NDEV, 8, d), F32), pltpu.VMEM((NDEV, n_layers * NDEV, w3), F32),
                        pltpu.SemaphoreType.DMA((2 * (NDEV - 1),)), pltpu.SemaphoreType.DMA((2 * (NDEV - 1),)), *ex.sems],
        compiler_params=pltpu.CompilerParams(vmem_limit_bytes=VMEM_LIMIT),
    )(c, mod_w, mod_b, *ex.arrays)
    return out[0], out[1], out[2:]


class _Exchange:
    def __init__(self, arrays, out_shapes, sems, start, middles, finish):
        self.arrays, self.out_shapes, self.sems = list(arrays), list(out_shapes), list(sems)
        self.start, self.middles, self.finish = start, list(middles), finish


def _gather_exchange(shards, by_cols):
    n = len(shards)
    shapes = [sh.shape for sh in shards]

    def tools(ins, outs, sems):
        send_sems, recv_sems, local_sems = sems
        x, y, c = _pos()
        chips = [(1 - x, y), (x, 1 - y), (1 - x, 1 - y)]
        south = c == 0
        relayed = (jnp.where(south, 1 - x, x), jnp.where(south, y, 1 - y), c)
        relay_to = (jnp.where(south, x, 1 - x), jnp.where(south, 1 - y, y), c)

        def place(a, block):
            r, cc = shapes[a]
            if by_cols[a]:
                return outs[a].at[:, pl.ds(_index(block) * cc, cc)]
            return outs[a].at[pl.ds(_index(block) * r, r), :]

        def copy(a, k, block, to, src=None):
            dst = place(a, block)
            return pltpu.make_async_remote_copy(
                src_ref=dst if src is None else src, dst_ref=dst,
                send_sem=send_sems.at[a * 7 + k], recv_sem=recv_sems.at[a * 7 + k],
                device_id=to, device_id_type=MESH)

        mine = [pltpu.make_async_copy(ins[a], place(a, (x, y, c)), local_sems.at[a]) for a in range(n)]
        first = []
        for a in range(n):
            first.append(copy(a, 0, (x, y, c), (x, y, 1 - c), src=ins[a]))
            first += [copy(a, 1 + j, (x, y, c), (*chip, c), src=ins[a]) for j, chip in enumerate(chips[:2])]
        relays = [copy(a, 3, relayed, relay_to) for a in range(n)]
        passed = [copy(a, 4 + j, (*chip, c), (x, y, 1 - c)) for j, chip in enumerate(chips) for a in range(n)]
        return (x, y, c), chips, copy, mine, first, relays, passed

    def start(ins, outs, sems):
        _, _, _, mine, first, _, _ = tools(ins, outs, sems)
        for cp in mine + first:
            cp.start()

    def pass_neighbours(ins, outs, sems):
        (x, y, c), chips, copy, _, _, relays, passed = tools(ins, outs, sems)
        for j, chip in enumerate(chips[:2]):
            for a in range(n):
                copy(a, 1 + j, (*chip, c), (x, y, c)).wait_recv()
                passed[j * n + a].start()
        for cp in relays:
            cp.start()

    def pass_diagonal(ins, outs, sems):
        (x, y, c), chips, copy, _, _, _, passed = tools(ins, outs, sems)
        for a in range(n):
            copy(a, 3, (*chips[2], c), (x, y, c)).wait_recv()
            passed[2 * n + a].start()

    def finish(ins, outs, sems):
        (x, y, c), chips, copy, mine, first, relays, passed = tools(ins, outs, sems)
        for a in range(n):
            copy(a, 0, (x, y, 1 - c), (x, y, c)).wait_recv()
        for j, chip in enumerate(chips):
            for a in range(n):
                copy(a, 4 + j, (*chip, 1 - c), (x, y, c)).wait_recv()
        for cp in first + relays + passed:
            cp.wait_send()
        for cp in mine:
            cp.wait()

    out_shapes = [jax.ShapeDtypeStruct((r, NDEV * cc) if bc else (NDEV * r, cc), sh.dtype)
                  for (r, cc), bc, sh in zip(shapes, by_cols, shards)]
    sems = [pltpu.SemaphoreType.DMA((7 * n,)), pltpu.SemaphoreType.DMA((7 * n,)), pltpu.SemaphoreType.DMA((n,))]
    return _Exchange(shards, out_shapes, sems, start, [pass_neighbours, pass_diagonal], finish)


def _scatter_exchange(parts):
    n = len(parts)

    def tools(ins, outs, sems):
        send_sems, recv_sems, local_sems = sems
        me = _index(_pos())
        mine = [pltpu.make_async_copy(ins[a].at[me], outs[a].at[me], local_sems.at[a]) for a in range(n)]
        sends, arrivals = [], []
        for k in range(1, NDEV):
            peer = _peer(k)
            for a in range(n):
                pair = dict(send_sem=send_sems.at[a * 7 + k - 1], recv_sem=recv_sems.at[a * 7 + k - 1],
                            device_id=peer, device_id_type=MESH)
                sends.append(pltpu.make_async_remote_copy(src_ref=ins[a].at[_index(peer)], dst_ref=outs[a].at[me], **pair))
                slot = outs[a].at[_index(peer)]
                arrivals.append(pltpu.make_async_remote_copy(src_ref=slot, dst_ref=slot, **pair))
        return mine, sends, arrivals

    def start(ins, outs, sems):
        mine, sends, _ = tools(ins, outs, sems)
        for cp in mine + sends:
            cp.start()

    def finish(ins, outs, sems):
        mine, sends, arrivals = tools(ins, outs, sems)
        for cp in arrivals:
            cp.wait_recv()
        for cp in sends:
            cp.wait_send()
        for cp in mine:
            cp.wait()

    out_shapes = [jax.ShapeDtypeStruct(p.shape, p.dtype) for p in parts]
    sems = [pltpu.SemaphoreType.DMA((7 * n,)), pltpu.SemaphoreType.DMA((7 * n,)), pltpu.SemaphoreType.DMA((n,))]
    return _Exchange(parts, out_shapes, sems, start, [], finish)


def _carry(ex, body, n_in, n_out, first, middle, last):
    if ex is None:
        return body
    r_in, r_out = len(ex.arrays), len(ex.out_shapes)

    def wrapped(*refs):
        ins, rins = refs[:n_in], refs[n_in:n_in + r_in]
        outs = refs[n_in + r_in:n_in + r_in + n_out]
        routs = refs[n_in + r_in + n_out:n_in + r_in + n_out + r_out]
        rest = refs[n_in + r_in + n_out + r_out:]
        scratch, sems = rest[:len(rest) - len(ex.sems)], rest[len(rest) - len(ex.sems):]

        @pl.when(first())
        def _():
            ex.start(rins, routs, sems)

        for passing_on, at_step in zip(ex.middles, middle or []):
            pl.when(at_step())(functools.partial(passing_on, rins, routs, sems))

        body(*ins, *outs, *scratch)

        @pl.when(last())
        def _():
            ex.finish(rins, routs, sems)

    return wrapped


def _carried(ex):
    if ex is None:
        return [], [], [], [], []
    return ex.arrays, [_hbm()] * len(ex.arrays), ex.out_shapes, [_hbm()] * len(ex.out_shapes), ex.sems


def _resident(shape):
    return pl.BlockSpec(shape, lambda *_: (0,) * len(shape), pipeline_mode=pl.Buffered(1))


def _norm_modulate(x_ref, g_ref, sc_ref, sh_ref):
    xv = x_ref[...]
    r = lax.rsqrt(jnp.mean(xv * xv, axis=-1, keepdims=True) + RMS_EPS)
    return ((xv * r) * g_ref[...] * (1.0 + sc_ref[...]) + sh_ref[...]).astype(BF16)


def _conv_taps(cx, t6, t7, row):
    p1 = jnp.where(row == 0, t7, pltpu.roll(cx, 1, 0))
    p2 = jnp.where(row == 0, t6, jnp.where(row == 1, t7, pltpu.roll(cx, 2, 0)))
    return p1, p2


def _layer_a_fwd(x, g, scale, shift, gate, wi, cw, cb, wo, name, ex=None):
    s, d = x.shape
    e = wo.shape[0]
    t = min(s, 256)
    n_t = s // t
    cwid = min(e, 512)

    def body(x_ref, g_ref, sc_ref, sh_ref, gate_ref, wi_ref, cw_ref, cb_ref, wo_ref,
             proj_ref, h_ref, x1_ref, br_ref, conv_ref, y_scr, tail_scr):
        @pl.when(pl.program_id(0) == 0)
        def _():
            tail_scr[...] = jnp.zeros_like(tail_scr)
        h_ref[...] = _norm_modulate(x_ref, g_ref, sc_ref, sh_ref)
        row = lax.broadcasted_iota(jnp.int32, (t, cwid), 0)

        def project(c0):
            v = jnp.dot(h_ref[...], wi_ref[:, c0:c0 + cwid], preferred_element_type=F32)
            proj_ref[:, c0:c0 + cwid] = v.astype(BF16)
            return v

        for c0 in range(0, e, cwid):
            sl = slice(c0, c0 + cwid)
            bg, z = project(c0), project(3 * e + c0)
            cx = project(e + c0) * project(2 * e + c0)
            p1, p2 = _conv_taps(cx, tail_scr[6:7, sl], tail_scr[7:8, sl], row)
            conv = cb_ref[:, sl] + cw_ref[2:3, sl] * cx + cw_ref[0:1, sl] * p2 + cw_ref[1:2, sl] * p1
            conv_ref[:, sl] = conv.astype(BF16)
            y_scr[:, sl] = (_silu(z)[0] * bg * conv).astype(BF16)
            tail_scr[:, sl] = cx[t - 8:t, :]
        br = jnp.dot(y_scr[...], wo_ref[...], preferred_element_type=F32)
        x1_ref[...] = x_ref[...] + gate_ref[...] * br
        br_ref[...] = br.astype(BF16)

    step = lambda k: (lambda: pl.program_id(0) == k)
    body = _carry(ex, body, 9, 5, step(0), [step(n_t // 3), step((2 * n_t) // 3)], step(n_t - 1))
    ex_args, ex_in, ex_shapes, ex_out, ex_sems = _carried(ex)
    tok = pl.BlockSpec((t, d), lambda i: (i, 0))
    out = pl.pallas_call(
        body, name=name, grid=(n_t,),
        out_shape=(jax.ShapeDtypeStruct((s, 4 * e), BF16), jax.ShapeDtypeStruct((s, d), BF16),
                   jax.ShapeDtypeStruct((s, d), F32), jax.ShapeDtypeStruct((s, d), BF16),
                   jax.ShapeDtypeStruct((s, e), BF16), *ex_shapes),
        in_specs=[tok, _full((1, d)), _full((1, d)), _full((1, d)), _full((1, d)), _resident((d, 4 * e)),
                  _full((3, e)), _full((1, e)), _resident((e, d)), *ex_in],
        out_specs=(pl.BlockSpec((t, 4 * e), lambda i: (i, 0)), tok, tok, tok,
                   pl.BlockSpec((t, e), lambda i: (i, 0)), *ex_out),
        scratch_shapes=[pltpu.VMEM((t, e), BF16), pltpu.VMEM((8, e), F32), *ex_sems],
        compiler_params=_params(("arbitrary",)),
    )(x, g, scale, shift, gate, wi, cw, cb, wo, *ex_args)
    return (*out[:5], out[5:])


def _ln_stats(v_of, v_scr, t, e):
    gw = e // GROUPS
    s1 = jnp.zeros((t, 1), F32)
    for g in range(GROUPS):
        v = v_of(g)
        v_scr[:, g * gw:(g + 1) * gw] = v
        s1 = s1 + _rowsum(v)
    mu = s1 * (1.0 / e)
    s2 = jnp.zeros((t, 1), F32)
    for g in range(GROUPS):
        dv = v_scr[:, g * gw:(g + 1) * gw] - mu
        s2 = s2 + _rowsum(dv * dv)
    return mu, lax.rsqrt(s2 * (1.0 / e) + LN_EPS)


def _layer_b_fwd_loss(x1, tgt, g1, scale, shift, gate, fg, wi, lng, lnb, wt, bsf, wo, name, ex=None):
    s, d = x1.shape
    e = wo.shape[0]
    gw = e // GROUPS
    t = min(s, 256)
    n_t = s // t

    def body(x1_ref, tgt_ref, g_ref, sc_ref, sh_ref, gate_ref, fg_ref, wi_ref, lng_ref, lnb_ref, wt_ref, bsf_ref, wo_ref,
             proj_ref, h_ref, dx2_ref, loss_ref, dfg_ref, dgate_ref, v_ref, dgv_ref, v_scr, y_scr):
        @pl.when(pl.program_id(0) == 0)
        def _():
            loss_ref[...] = jnp.zeros_like(loss_ref)
            dfg_ref[...] = jnp.zeros_like(dfg_ref)
            dgate_ref[...] = jnp.zeros_like(dgate_ref)
        h_ref[...] = _norm_modulate(x1_ref, g_ref, sc_ref, sh_ref)

        def project(c0):
            v = jnp.dot(h_ref[...], wi_ref[:, c0:c0 + gw], preferred_element_type=F32)
            proj_ref[:, c0:c0 + gw] = v.astype(BF16)
            return v

        def gelu_v(g):
            gs = slice(g * gw, (g + 1) * gw)
            v, dgv = _gelu(project(e + g * gw))
            v_ref[:, gs] = v.astype(BF16)
            dgv_ref[:, gs] = dgv.astype(BF16)
            return v

        mu, rs = _ln_stats(gelu_v, v_scr, t, e)
        for g in range(GROUPS):
            gs = slice(g * gw, (g + 1) * gw)
            vn = (((v_scr[:, gs] - mu) * rs) * lng_ref[:, gs] + lnb_ref[:, gs]).astype(BF16)
            u = _gelu(project(g * gw))[0]
            sz = _silu(project(2 * e + g * gw))[0]
            for ch in range(t // CHUNK):
                rows = slice(ch * CHUNK, (ch + 1) * CHUNK)
                mixed = jnp.dot(wt_ref[g], vn[rows], preferred_element_type=F32) + bsf_ref[:, gs]
                y_scr[rows, gs] = (sz[rows] * (u[rows] * mixed)).astype(BF16)
        br = jnp.dot(y_scr[...], wo_ref[...], preferred_element_type=F32)
        x2 = x1_ref[...] + gate_ref[...] * br
        r2 = lax.rsqrt(jnp.mean(x2 * x2, axis=-1, keepdims=True) + RMS_EPS)
        xn = x2 * r2
        diff = xn * fg_ref[...] - tgt_ref[...]
        loss_ref[...] += jnp.broadcast_to(0.5 * _colsum(jnp.mean(diff * diff, axis=-1, keepdims=True)), loss_ref.shape)
        dout = diff * (1.0 / d)
        dfg_ref[...] += _colsum(dout * xn)
        dxn = dout * fg_ref[...]
        dx2 = r2 * (dxn - xn * jnp.mean(dxn * xn, axis=-1, keepdims=True))
        dx2_ref[...] = dx2
        dgate_ref[...] += _colsum(dx2 * br)

    step = lambda k: (lambda: pl.program_id(0) == k)
    body = _carry(ex, body, 13, 8, step(0), [step(n_t // 3), step((2 * n_t) // 3)], step(n_t - 1))
    ex_args, ex_in, ex_shapes, ex_out, ex_sems = _carried(ex)
    tok = pl.BlockSpec((t, d), lambda i: (i, 0))
    vec = _full((1, d))
    out = pl.pallas_call(
        body, name=name, grid=(n_t,),
        out_shape=(jax.ShapeDtypeStruct((s, 3 * e), BF16), jax.ShapeDtypeStruct((s, d), BF16),
                   jax.ShapeDtypeStruct((s, d), F32), jax.ShapeDtypeStruct((8, 128), F32),
                   jax.ShapeDtypeStruct((1, d), F32), jax.ShapeDtypeStruct((1, d), F32),
                   jax.ShapeDtypeStruct((s, e), BF16), jax.ShapeDtypeStruct((s, e), BF16), *ex_shapes),
        in_specs=[tok, tok, vec, vec, vec, vec, vec, _resident((d, 3 * e)), _full((1, e)), _full((1, e)),
                  _full((GROUPS, CHUNK, CHUNK)), _resident((CHUNK, e)), _resident((e, d)), *ex_in],
        out_specs=(pl.BlockSpec((t, 3 * e), lambda i: (i, 0)), tok, tok, _full((8, 128)), vec, vec,
                   pl.BlockSpec((t, e), lambda i: (i, 0)), pl.BlockSpec((t, e), lambda i: (i, 0)), *ex_out),
        scratch_shapes=[pltpu.VMEM((t, e), F32), pltpu.VMEM((t, e), BF16), *ex_sems],
        compiler_params=_params(("arbitrary",)),
    )(x1, tgt, g1, scale, shift, gate, fg, wi, lng, lnb, wt, bsf, wo, *ex_args)
    return (*out[:8], out[8:])


def _norm_modulate_bwd(dh, x_ref, dres_ref, g_ref, sc_ref, dx_ref, dsh_ref, p_scr):
    xv = x_ref[...]
    r = lax.rsqrt(jnp.mean(xv * xv, axis=-1, keepdims=True) + RMS_EPS)
    xn = xv * r
    dsh_ref[...] += _colsum(dh)
    p_scr[...] += _colsum(dh * xn)
    dxn = dh * (g_ref[...] * (1.0 + sc_ref[...]))
    dx_ref[...] = r * (dxn - xn * jnp.mean(dxn * xn, axis=-1, keepdims=True)) + dres_ref[...]


def _layer_b_bwd(proj, v_act, dgv, dx2, x1, gate, g1, scale, lng, lnb, wt, wtt, bsf, wo, wi, name):
    s, e3 = proj.shape
    e = e3 // 3
    d = dx2.shape[1]
    gw = e // GROUPS
    t = min(s, 256)
    n_t = s // t

    def body(pu_ref, pz_ref, v_ref, dgv_ref, dx_ref, x1_ref, gate_ref, g_ref, sc_ref, lng_ref, lnb_ref, wt_ref, wtt_ref,
             bsf_ref, wo_ref, wi_ref,
             dp_ref, y_ref, dx1_ref, dws_ref, dbs_ref, dlg_ref, dlb_ref, dsh_ref, dsc_ref, dg_ref,
             v_scr, dbr_scr, dvn_scr, dbs_scr, p_scr, dy_scr, vn_scr, mixed_scr):
        @pl.when(pl.program_id(0) == 0)
        def _():
            dws_ref[...] = jnp.zeros_like(dws_ref)
            dlg_ref[...] = jnp.zeros_like(dlg_ref)
            dlb_ref[...] = jnp.zeros_like(dlb_ref)
            dsh_ref[...] = jnp.zeros_like(dsh_ref)
            dbs_scr[...] = jnp.zeros_like(dbs_scr)
            p_scr[...] = jnp.zeros_like(p_scr)
        dbr_scr[...] = (dx_ref[...] * gate_ref[...]).astype(BF16)
        mu, rs = _ln_stats(lambda g: v_ref[:, g * gw:(g + 1) * gw].astype(F32), v_scr, t, e)
        tril = (lax.broadcasted_iota(jnp.int32, (CHUNK, CHUNK), 0) >= lax.broadcasted_iota(jnp.int32, (CHUNK, CHUNK), 1))
        c1 = jnp.zeros((t, 1), F32)
        c2 = jnp.zeros((t, 1), F32)
        span = 2
        kw = span * gw
        dh = jnp.zeros((t, d), F32)

        def through_w_in(c0):
            return lax.dot_general(dp_ref[:, c0:c0 + kw], wi_ref[:, c0:c0 + kw], NT_DIMS, preferred_element_type=F32)

        dy_scr[...] = lax.dot_general(dbr_scr[...], wo_ref[...], NT_DIMS, preferred_element_type=F32)
        for g in range(GROUPS):
            gs = slice(g * gw, (g + 1) * gw)
            vhat = (v_scr[:, gs] - mu) * rs
            v_scr[:, gs] = vhat
            vn = (vhat * lng_ref[:, gs] + lnb_ref[:, gs]).astype(BF16)
            vn_scr[:, gs] = vn
            for ch in range(t // CHUNK):
                rows = slice(ch * CHUNK, (ch + 1) * CHUNK)
                mixed_scr[rows, gs] = jnp.dot(wt_ref[g], vn[rows], preferred_element_type=F32) + bsf_ref[:, gs]
        for g in range(GROUPS):
            gs = slice(g * gw, (g + 1) * gw)
            vhat = v_scr[:, gs]
            lg = lng_ref[:, gs]
            for ch in range(t // CHUNK):
                rows = slice(ch * CHUNK, (ch + 1) * CHUNK)
                mixed = mixed_scr[rows, gs]
                u, dgu = _gelu(pu_ref[rows, gs].astype(F32))
                sz, dsz = _silu(pz_ref[rows, gs].astype(F32))
                sgate = u * mixed
                y_ref[rows, gs] = (sz * sgate).astype(BF16)
                dy = dy_scr[rows, gs]
                dp_ref[rows, 2 * e + g * gw:2 * e + (g + 1) * gw] = (dy * sgate * dsz).astype(BF16)
                ds = dy * sz
                dp_ref[rows, gs] = (ds * mixed * dgu).astype(BF16)
                dm = ds * u
                dbs_scr[:, gs] += dm
                dmb = dm.astype(BF16)
                dws_ref[g] += jnp.where(tril, lax.dot_general(dmb, vn_scr[rows, gs], NT_DIMS, preferred_element_type=F32), 0.0)
                dvn_scr[rows, gs] = jnp.dot(wtt_ref[g], dmb, preferred_element_type=F32)
            dvn = dvn_scr[:, gs]
            dlb_ref[:, gs] += _colsum(dvn)
            dlg_ref[:, gs] += _colsum(dvn * vhat)
            dvh = dvn * lg
            c1 = c1 + _rowsum(dvh)
            c2 = c2 + _rowsum(dvh * vhat)
            if g % span == span - 1:
                dh = dh + through_w_in(g * gw + gw - kw) + through_w_in(2 * e + g * gw + gw - kw)
        c1 = c1 * (1.0 / e)
        c2 = c2 * (1.0 / e)
        for g in range(GROUPS):
            gs = slice(g * gw, (g + 1) * gw)
            dv = rs * (dvn_scr[:, gs] * lng_ref[:, gs] - c1 - v_scr[:, gs] * c2)
            dp_ref[:, e + g * gw:e + (g + 1) * gw] = (dv * dgv_ref[:, gs]).astype(BF16)
            if g % span == span - 1:
                dh = dh + through_w_in(e + g * gw + gw - kw)
        _norm_modulate_bwd(dh, x1_ref, dx_ref, g_ref, sc_ref, dx1_ref, dsh_ref, p_scr)

        @pl.when(pl.program_id(0) == n_t - 1)
        def _():
            lane = lax.broadcasted_iota(jnp.int32, (CHUNK, 128), 1)
            acc = jnp.zeros((CHUNK, 128), F32)
            for g in range(GROUPS):
                acc = acc + jnp.where(lane == g, _rowsum(dbs_scr[:, g * gw:(g + 1) * gw]), 0.0)
            dbs_ref[...] = acc
            dsc_ref[...] = p_scr[...] * g_ref[...]
            dg_ref[...] = p_scr[...] * (1.0 + sc_ref[...])

    tok = pl.BlockSpec((t, d), lambda i: (i, 0))
    vec, evec, ws = _full((1, d)), _full((1, e)), _full((GROUPS, CHUNK, CHUNK))
    vshape = jax.ShapeDtypeStruct((1, d), F32)
    return pl.pallas_call(
        body, name=name, grid=(n_t,),
        out_shape=(jax.ShapeDtypeStruct((s, e3), BF16), jax.ShapeDtypeStruct((s, e), BF16), jax.ShapeDtypeStruct((s, d), F32),
                   jax.ShapeDtypeStruct((GROUPS, CHUNK, CHUNK), F32), jax.ShapeDtypeStruct((CHUNK, 128), F32),
                   jax.ShapeDtypeStruct((1, e), F32), jax.ShapeDtypeStruct((1, e), F32), vshape, vshape, vshape),
        in_specs=[pl.BlockSpec((t, e), lambda i: (i, 0)), pl.BlockSpec((t, e), lambda i: (i, 2)),
                  pl.BlockSpec((t, e), lambda i: (i, 0)), pl.BlockSpec((t, e), lambda i: (i, 0)),
                  tok, tok, vec, vec, vec, evec, evec, ws, ws,
                  _resident((CHUNK, e)), _resident((e, d)), _resident((d, e3))],
        out_specs=(pl.BlockSpec((t, e3), lambda i: (i, 0)), pl.BlockSpec((t, e), lambda i: (i, 0)), tok,
                   ws, _full((CHUNK, 128)), evec, evec, vec, vec, vec),
        scratch_shapes=[pltpu.VMEM((t, e), F32), pltpu.VMEM((t, d), BF16),
                        pltpu.VMEM((t, e), F32), pltpu.VMEM((CHUNK, e), F32), pltpu.VMEM((1, d), F32),
                        pltpu.VMEM((t, e), F32), pltpu.VMEM((t, e), BF16), pltpu.VMEM((t, e), F32)],
        compiler_params=_params(("arbitrary",)),
    )(proj, proj, v_act, dgv, dx2, x1, gate, g1, scale, lng, lnb, wt, wtt, bsf, wo, wi)


def _conv_mixer_bwd(proj, dx1, br, conv_a, cw, gate, wo, name, ex=None):
    s, e4 = proj.shape
    e = e4 // 4
    d = dx1.shape[1]
    t = min(s, 256)
    n_t = s // t
    cwid = min(e, 512)

    def body(p_ref, dx_ref, br_ref, conv_ref, cw_ref, gate_ref, wo_ref,
             dp_ref, y_ref, dgate_ref, dcb_ref, dcw_ref, dy_scr, head_scr):
        i = pl.program_id(0)

        @pl.when(i == 0)
        def _():
            dgate_ref[...] = jnp.zeros_like(dgate_ref)
            dcb_ref[...] = jnp.zeros_like(dcb_ref)
            dcw_ref[...] = jnp.zeros_like(dcw_ref)
            head_scr[...] = jnp.zeros_like(head_scr)
        dx = dx_ref[...]
        dgate_ref[...] += _colsum(dx * br_ref[...].astype(F32))
        dy_scr[...] = lax.dot_general((dx * gate_ref[...]).astype(BF16), wo_ref[...], NT_DIMS,
                                      preferred_element_type=F32)
        row = lax.broadcasted_iota(jnp.int32, (t, cwid), 0)
        for c0 in range(0, e, cwid):
            sl = slice(c0, c0 + cwid)
            bg = p_ref[:, c0:c0 + cwid].astype(F32)
            cg = p_ref[:, e + c0:e + c0 + cwid].astype(F32)
            xin = p_ref[:, 2 * e + c0:2 * e + c0 + cwid].astype(F32)
            z = p_ref[:, 3 * e + c0:3 * e + c0 + cwid].astype(F32)
            cx = cg * xin
            w0, w1, w2 = cw_ref[0:1, sl], cw_ref[1:2, sl], cw_ref[2:3, sl]
            conv = conv_ref[:, sl].astype(F32)
            sz, dsz = _silu(z)
            dy = dy_scr[:, sl]
            y_ref[:, sl] = (sz * bg * conv).astype(BF16)
            dp_ref[:, 3 * e + c0:3 * e + c0 + cwid] = (dy * bg * conv * dsz).astype(BF16)
            dp_ref[:, c0:c0 + cwid] = (dy * sz * conv).astype(BF16)
            dconv = dy * sz * bg
            h0, h1 = head_scr[0:1, sl], head_scr[1:2, sl]
            n1 = jnp.where(row == t - 1, h0, pltpu.roll(dconv, t - 1, 0))
            n2 = jnp.where(row == t - 2, h0, jnp.where(row == t - 1, h1, pltpu.roll(dconv, t - 2, 0)))
            dcb_ref[:, sl] += _colsum(dconv)
            dcw_ref[2:3, sl] += _colsum(dconv * cx)
            dcw_ref[1:2, sl] += _colsum(n1 * cx)
            dcw_ref[0:1, sl] += _colsum(n2 * cx)
            dcx = w2 * dconv + w1 * n1 + w0 * n2
            dp_ref[:, e + c0:e + c0 + cwid] = (dcx * xin).astype(BF16)
            dp_ref[:, 2 * e + c0:2 * e + c0 + cwid] = (dcx * cg).astype(BF16)
            head_scr[:, sl] = dconv[0:8, :]

    body = _carry(ex, body, 7, 5, lambda: pl.program_id(0) == 0, None, lambda: pl.program_id(0) == n_t - 1)
    ex_args, ex_in, ex_shapes, ex_out, ex_sems = _carried(ex)
    rev = lambda i: (n_t - 1 - i, 0)
    out = pl.pallas_call(
        body, name=name, grid=(n_t,),
        out_shape=(jax.ShapeDtypeStruct((s, e4), BF16), jax.ShapeDtypeStruct((s, e), BF16),
                   jax.ShapeDtypeStruct((1, d), F32), jax.ShapeDtypeStruct((1, e), F32), jax.ShapeDtypeStruct((8, e), F32),
                   *ex_shapes),
        in_specs=[pl.BlockSpec((t, e4), rev), pl.BlockSpec((t, d), rev), pl.BlockSpec((t, d), rev),
                  pl.BlockSpec((t, e), rev), _full((3, e)), _full((1, d)), _full((e, d)), *ex_in],
        out_specs=(pl.BlockSpec((t, e4), rev), pl.BlockSpec((t, e), rev), _full((1, d)), _full((1, e)), _full((8, e)),
                   *ex_out),
        scratch_shapes=[pltpu.VMEM((t, e), F32), pltpu.VMEM((8, e), F32), *ex_sems],
        compiler_params=_params(("arbitrary",)),
    )(proj, dx1, br, conv_a, cw, gate, wo, *ex_args)
    return (*out[:5], out[5:])


def _matmul_nt_norm_bwd(dproj, w, xin, dres, g, scale, name, ex=None):
    s, d = xin.shape
    n = w.shape[1]
    tm = min(s, 512)
    n_i = s // tm
    cb = min(d, 512)

    def body(dp_ref, w_ref, x_ref, dres_ref, g_ref, sc_ref, dx_ref, dsh_ref, dsc_ref, dg_ref, p_scr):
        i = pl.program_id(0)

        @pl.when(i == 0)
        def _():
            dsh_ref[...] = jnp.zeros_like(dsh_ref)
            p_scr[...] = jnp.zeros_like(p_scr)
        xv = x_ref[...]
        r = lax.rsqrt(jnp.mean(xv * xv, axis=-1, keepdims=True) + RMS_EPS)
        rowdot = jnp.zeros((tm, 1), F32)
        for c0 in range(0, d, cb):
            cols = slice(c0, c0 + cb)
            dh = lax.dot_general(dp_ref[...], w_ref[cols, :], NT_DIMS, preferred_element_type=F32)
            xn = x_ref[:, cols] * r
            dsh_ref[:, cols] += _colsum(dh)
            p_scr[:, cols] += _colsum(dh * xn)
            dxn = dh * (g_ref[:, cols] * (1.0 + sc_ref[:, cols]))
            dx_ref[:, cols] = dxn
            rowdot = rowdot + _rowsum(dxn * xn)
        rowdot = rowdot * (1.0 / d)
        dx_ref[...] = r * (dx_ref[...] - (x_ref[...] * r) * rowdot) + dres_ref[...]

        @pl.when(i == n_i - 1)
        def _():
            dsc_ref[...] = p_scr[...] * g_ref[...]
            dg_ref[...] = p_scr[...] * (1.0 + sc_ref[...])

    body = _carry(ex, body, 6, 4, lambda: pl.program_id(0) == 0, None, lambda: pl.program_id(0) == n_i - 1)
    ex_args, ex_in, ex_shapes, ex_out, ex_sems = _carried(ex)
    tok = pl.BlockSpec((tm, d), lambda i: (i, 0))
    vec = pl.BlockSpec((1, d), lambda i: (0, 0))
    vshape = jax.ShapeDtypeStruct((1, d), F32)
    out = pl.pallas_call(
        body, name=name, grid=(n_i,),
        out_shape=(jax.ShapeDtypeStruct((s, d), F32), vshape, vshape, vshape, *ex_shapes),
        in_specs=[pl.BlockSpec((tm, n), lambda i: (i, 0)), _resident((d, n)), tok, tok, vec, vec, *ex_in],
        out_specs=(tok, vec, vec, vec, *ex_out),
        scratch_shapes=[pltpu.VMEM((1, d), F32), *ex_sems],
        compiler_params=_params(("arbitrary",)),
    )(dproj, w, xin, dres, g, scale, *ex_args)
    return (*out[:4], out[4:])


def _matmul_tn(a, b, colscale, rows_split, name, ex=None, a_cols=None):
    s, m = a.shape
    a_blk = 0
    if a_cols is not None:
        a_blk, m = a_cols
    n = b.shape[1]
    n_j, tn = (1, n) if rows_split else (NDEV, n // NDEV)
    fixed = (4 + 4 + 2 * 2) * m * tn
    tk = s
    while fixed + 2 * tk * (2 * m + b.dtype.itemsize * tn) > TN_VMEM_BUDGET:
        tk //= 2
    n_k = s // tk

    def body(a_ref, b_ref, cs_ref, o_ref, acc):
        k = pl.program_id(1)
        part = lax.dot_general(a_ref[...], b_ref[...].astype(BF16), TN_DIMS, preferred_element_type=F32)
        if n_k == 1:
            o_ref[...] = (part * cs_ref[...]).astype(BF16)
            return

        @pl.when(k == 0)
        def _():
            acc[...] = part

        @pl.when((k > 0) & (k < n_k - 1))
        def _():
            acc[...] += part

        @pl.when(k == n_k - 1)
        def _():
            o_ref[...] = ((acc[...] + part) * cs_ref[...]).astype(BF16)

    at = lambda j, k: (pl.program_id(0) == j) & (pl.program_id(1) == k)
    body = _carry(ex, body, 3, 1, lambda: at(0, 0), None, lambda: at(n_j - 1, n_k - 1))
    ex_args, ex_in, ex_shapes, ex_out, ex_sems = _carried(ex)
    out = pl.pallas_call(
        body, name=name, grid=(n_j, n_k),
        out_shape=(jax.ShapeDtypeStruct((n_j, m, tn), BF16), *ex_shapes),
        in_specs=[pl.BlockSpec((tk, m), lambda j, k: (k, a_blk)), pl.BlockSpec((tk, tn), lambda j, k: (k, j)),
                  pl.BlockSpec((1, tn), lambda j, k: (0, j)), *ex_in],
        out_specs=(pl.BlockSpec((None, m, tn), lambda j, k: (j, 0, 0)), *ex_out),
        scratch_shapes=[pltpu.VMEM((m, tn), F32), *ex_sems],
        compiler_params=_params(("arbitrary", "arbitrary")),
    )(a, b, colscale, *ex_args)
    return (out[0].reshape(NDEV, m // NDEV, n) if rows_split else out[0]), out[1:]


def _adam_update(w, g, m, v):
    m = ADAM_B1 * m + (1.0 - ADAM_B1) * g
    v = ADAM_B2 * v + (1.0 - ADAM_B2) * (g * g)
    m_hat = m / (1.0 - ADAM_B1 ** ADAM_STEP)
    v_hat = v / (1.0 - ADAM_B2 ** ADAM_STEP)
    return -ADAM_LR * (m_hat / (jnp.sqrt(v_hat) + ADAM_EPS) + ADAM_WD * w), m, v


def _adamw_reduce(parts, w, m, v, name, ex=None):
    n_l = len(parts)
    n_p, _, c = parts[0].shape
    rows = [p.shape[1] for p in parts]
    r = sum(rows)
    tr = min(min(rows), 128 if ex is not None else 256)
    n_i = r // tr
    tiles = [r_l // tr for r_l in rows]
    first_tile = [sum(tiles[:l]) for l in range(n_l)]

    def body(*refs):
        p_refs, (w_ref, m_ref, v_ref, g_out, d_out, m_out, v_out) = refs[:n_l], refs[n_l:]
        g = None
        for l, p_ref in enumerate(p_refs):
            g_l = p_ref[0].astype(F32)
            for j in range(1, n_p):
                g_l = g_l + p_ref[j].astype(F32)
            g = g_l if g is None else jnp.where(pl.program_id(0) >= first_tile[l], g_l, g)
        g_out[...] = g
        d_out[...], m_out[...], v_out[...] = _adam_update(w_ref[...], g, m_ref[...], v_ref[...])

    step = lambda k: (lambda: pl.program_id(0) == k)
    body = _carry(ex, body, n_l + 3, 4, step(0), [step(n_i // 3), step((2 * n_i) // 3)], step(n_i - 1))
    ex_args, ex_in, ex_shapes, ex_out, ex_sems = _carried(ex)
    blk = pl.BlockSpec((tr, c), lambda i: (i, 0))
    p_specs = [pl.BlockSpec((n_p, tr, c), lambda i, l=l: (0, jnp.clip(i - first_tile[l], 0, tiles[l] - 1), 0))
               for l in range(n_l)]
    shp = jax.ShapeDtypeStruct((r, c), F32)
    out = pl.pallas_call(
        body, name=name, grid=(n_i,), out_shape=(shp, shp, shp, shp, *ex_shapes),
        in_specs=[*p_specs, blk, blk, blk, *ex_in],
        out_specs=(blk, blk, blk, blk, *ex_out), scratch_shapes=ex_sems,
        compiler_params=_params(("arbitrary",)),
    )(*parts, w, m, v, *ex_args)
    return (*out[:4], out[4:])


def _adamw_small(gs, ws, ms, vs, name):
    n = len(gs)

    def body(*refs):
        ins, outs = refs[:4 * n], refs[4 * n:]
        for a in range(n):
            d, m, v = _adam_update(ins[n + a][...], ins[a][...], ins[2 * n + a][...], ins[3 * n + a][...])
            outs[a][...], outs[n + a][...], outs[2 * n + a][...] = d, m, v

    shapes = tuple(jax.ShapeDtypeStruct(w.shape, F32) for w in ws) * 3
    out = pl.pallas_call(
        body, name=name, out_shape=shapes,
        in_specs=[_vmem()] * (4 * n), out_specs=tuple([_vmem()] * (3 * n)),
        compiler_params=pltpu.CompilerParams(vmem_limit_bytes=VMEM_LIMIT),
    )(*gs, *ws, *ms, *vs)
    return out[:n], out[n:2 * n], out[2 * n:]


def _sum_devices(packed, name):
    _, r, wdt = packed.shape

    def body(p_ref, o_ref):
        acc = p_ref[0]
        for j in range(1, NDEV):
            acc = acc + p_ref[j]
        o_ref[...] = acc

    return pl.pallas_call(
        body, name=name, out_shape=jax.ShapeDtypeStruct((r, wdt), F32),
        in_specs=[_vmem()], out_specs=_vmem(),
        compiler_params=pltpu.CompilerParams(vmem_limit_bytes=VMEM_LIMIT),
    )(packed)


def _mod_w_grad(c_t, dmod, name):
    n_layers, _, w3 = dmod.shape
    d = c_t.shape[0]

    def body(c_ref, dm_ref, o_ref):
        for i in range(n_layers):
            acc = c_ref[:, 0:1] * dm_ref[i, 0:1, :]
            for b in range(1, NDEV):
                acc = acc + c_ref[:, b:b + 1] * dm_ref[i, b:b + 1, :]
            o_ref[i] = acc

    return pl.pallas_call(
        body, name=name, out_shape=jax.ShapeDtypeStruct((n_layers, d, w3), F32),
        in_specs=[_vmem(), _vmem()], out_specs=_vmem(),
        compiler_params=pltpu.CompilerParams(vmem_limit_bytes=VMEM_LIMIT),
    )(c_t, dmod)


def _mask_transpose_ws(w_s, name):
    def body(w_ref, wt_ref, wtt_ref):
        tril = (lax.broadcasted_iota(jnp.int32, (CHUNK, CHUNK), 0) >= lax.broadcasted_iota(jnp.int32, (CHUNK, CHUNK), 1))
        for g in range(GROUPS):
            wm = jnp.where(tril, w_ref[g], 0.0)
            wt_ref[g] = wm.astype(BF16)
            wtt_ref[g] = wm.T.astype(BF16)

    shp = jax.ShapeDtypeStruct(w_s.shape, BF16)
    return pl.pallas_call(
        body, name=name, out_shape=(shp, shp), in_specs=[_vmem()], out_specs=(_vmem(), _vmem()),
    )(w_s)


def _pack(pieces):
    flat = jnp.concatenate([p.reshape(-1) for p in pieces])
    rows = -(-flat.shape[0] // (8 * PACK_W)) * 8
    return jnp.pad(flat, (0, rows * PACK_W - flat.shape[0])).reshape(rows, PACK_W)


def _unpack(flat, shapes):
    out, off = [], 0
    for shp in shapes:
        size = 1
        for dim in shp:
            size *= dim
        out.append(flat[off:off + size].reshape(shp))
        off += size
    return out


def kernel(x, c, mod_w, mod_b, norm_g, a_w_in, a_conv_w, a_conv_b, a_w_out, b_w_in, b_ln_g, b_ln_b, b_w_s, b_b_s, b_w_out, final_g, loss_target, m_mod_w, m_mod_b, m_norm_g, m_a_w_in, m_a_conv_w, m_a_conv_b, m_a_w_out, m_b_w_in, m_b_ln_g, m_b_ln_b, m_b_w_s, m_b_b_s, m_b_w_out, m_final_g, v_mod_w, v_mod_b, v_norm_g, v_a_w_in, v_a_conv_w, v_a_conv_b, v_a_w_out, v_b_w_in, v_b_ln_g, v_b_ln_b, v_b_w_s, v_b_b_s, v_b_w_out, v_final_g):
    s, d = x.shape[1], x.shape[2]
    es = a_w_out.shape[1]
    e = NDEV * es
    w3 = mod_w.shape[2]
    me = _index(_pos())
    x0 = x.reshape(s, d)
    tgt = loss_target.reshape(s, d)

    small = jnp.concatenate([a_conv_w[0], b_ln_g, b_ln_b, jnp.zeros((3, es), F32)], axis=0)
    gather_a = _gather_exchange([a_w_in[0].astype(BF16), a_w_out[0].astype(BF16), small], [True, False, True])
    gather_b = _gather_exchange([b_w_in[0].astype(BF16), b_w_out[0].astype(BF16)], [True, False])
    mod, c_all, (wa, woa, small_all) = _mod_vectors(c, mod_w, mod_b, gather_a)
    conv_w, ln_g, ln_b = small_all[0:3], small_all[3:4], small_all[4:5]
    bsf = jnp.repeat(b_b_s[0].T, e // GROUPS, axis=1)
    wt, wtt = _mask_transpose_ws(b_w_s[0], "mask_w_s")
    shift0, scale0, gate0 = mod[0:1, 0:d], mod[0:1, d:2 * d], mod[0:1, 2 * d:]
    shift1, scale1, gate1 = mod[1:2, 0:d], mod[1:2, d:2 * d], mod[1:2, 2 * d:]
    g0, g1, fg = norm_g[0:1], norm_g[1:2], final_g.reshape(1, d)

    proj_a, h0, x1, br_a, conv_a, (wb, wob) = _layer_a_fwd(
        x0, g0, scale0, shift0, gate0, wa, conv_w, a_conv_b, woa, "a_fwd", gather_b)
    proj_b, h1, dx2, loss_acc, dfg, dgate1, v_b, dgv_b, _ = _layer_b_fwd_loss(
        x1, tgt, g1, scale1, shift1, gate1, fg, wb, ln_g, ln_b, wt, bsf, wob, "b_fwd_loss")

    dproj_b, y_b, dx1, dws, dbs, dlg, dlb, dshift1, dscale1, dg1 = _layer_b_bwd(
        proj_b, v_b, dgv_b, dx2, x1, gate1, g1, scale1, ln_g, ln_b, wt, wtt, bsf, wob, wb, "b_bwd")
    gs_b_out, _ = _matmul_tn(y_b, dx2, gate1, True, "b_w_out_grad")
    gs_b_in, (gr_b_out,) = _matmul_tn(h1, dproj_b, jnp.ones((1, dproj_b.shape[1]), F32), False, "b_w_in_grad",
                                      _scatter_exchange([gs_b_out]))
    dproj_a, y_a, dgate0, dcb, dcw, (gr_b_in,) = _conv_mixer_bwd(
        proj_a, dx1, br_a, conv_a, conv_w, gate0, woa, "a_mixer_bwd", _scatter_exchange([gs_b_in]))
    gs_a_out, _ = _matmul_tn(y_a, dx1, gate0, True, "a_w_out_grad")
    ones_n = jnp.ones((1, dproj_a.shape[1]), F32)
    gs_last, _ = _matmul_tn(h0, dproj_a, ones_n, False, "a_w_in_grad_last", a_cols=(3, d // 4))
    gs_rest, (gr_a_out, gr_last) = _matmul_tn(h0, dproj_a, ones_n, False, "a_w_in_grad_rest",
                                              _scatter_exchange([gs_a_out, gs_last]), a_cols=(0, 3 * d // 4))
    dx0, dshift0, dscale0, dg0, (gr_rest,) = _matmul_nt_norm_bwd(dproj_a, wa, x0, dx1, g0, scale0, "a_in_bwd",
                                                                 _scatter_exchange([gs_rest]))

    def big(parts, w, m, v, name, ex=None):
        shp = w.shape
        r2 = lambda t_: t_.reshape(-1, shp[-1])
        g, dl, nm, nv, ex_out = _adamw_reduce(parts, r2(w), r2(m), r2(v), name, ex)
        return tuple(t_.reshape(shp) for t_ in (g, dl, nm, nv)), ex_out

    res = {}
    res["b_w_in"], _ = big([gr_b_in], b_w_in, m_b_w_in, v_b_w_in, "adamw_b_w_in")
    gr_a_in = [gr_rest, gr_last]

    pieces = [dshift0, dscale0, dgate0, dshift1, dscale1, dgate1, dg0, dg1, dcb, dcw[0:3], dlg, dlb, dfg,
              dbs[:, 0:GROUPS].T, dws, loss_acc[0:1, 0:1]]
    shapes = [p.shape for p in pieces]
    res["a_w_in"], (packed_all,) = big(gr_a_in, a_w_in, m_a_w_in, v_a_w_in, "adamw_a_w_in",
                                       _gather_exchange([_pack(pieces)], [False]))
    packed_all = packed_all.reshape(NDEV, -1, PACK_W)
    total = _sum_devices(packed_all, "sum_small_grads").reshape(-1)
    (t_sh0, t_sc0, t_ga0, t_sh1, t_sc1, t_ga1, t_g0, t_g1, t_cb, t_cw, t_lg, t_lb, t_fg, t_bs, t_ws, t_loss) = _unpack(
        total, shapes)
    loss = t_loss.reshape(())
    grad_mod_b = jnp.concatenate([jnp.concatenate([t_sh0, t_sc0, t_ga0], axis=1),
                                  jnp.concatenate([t_sh1, t_sc1, t_ga1], axis=1)], axis=0)
    grad_norm_g = jnp.concatenate([t_g0, t_g1], axis=0)
    dmod_all = packed_all.reshape(NDEV, -1)[:, 0:6 * d].reshape(NDEV, 2, 3 * d).transpose(1, 0, 2)
    dmod_mine = lax.dynamic_slice_in_dim(dmod_all, me * w3, w3, axis=2)
    grad_mod_w = _mod_w_grad(c_all.T, dmod_mine, "mod_w_grad")
    grad_a_conv_w = lax.dynamic_slice_in_dim(t_cw, me * es, es, axis=1)
    grad_b_ln_g = lax.dynamic_slice_in_dim(t_lg, me * es, es, axis=1)
    grad_b_ln_b = lax.dynamic_slice_in_dim(t_lb, me * es, es, axis=1)

    res["mod_w"], _ = big([grad_mod_w.reshape(1, -1, w3)], mod_w, m_mod_w, v_mod_w, "adamw_mod_w")
    res["a_w_out"], _ = big([gr_a_out], a_w_out, m_a_w_out, v_a_w_out, "adamw_a_w_out")
    res["b_w_out"], _ = big([gr_b_out], b_w_out, m_b_w_out, v_b_w_out, "adamw_b_w_out")

    small_names = ["mod_b", "norm_g", "a_conv_w", "a_conv_b", "b_ln_g", "b_ln_b", "b_w_s", "b_b_s", "final_g"]
    small_g = [grad_mod_b, grad_norm_g, grad_a_conv_w, t_cb, grad_b_ln_g, grad_b_ln_b, t_ws, t_bs, t_fg]
    small_w = [mod_b, norm_g, a_conv_w, a_conv_b, b_ln_g, b_ln_b, b_w_s, b_b_s, final_g]
    small_m = [m_mod_b, m_norm_g, m_a_conv_w, m_a_conv_b, m_b_ln_g, m_b_ln_b, m_b_w_s, m_b_b_s, m_final_g]
    small_v = [v_mod_b, v_norm_g, v_a_conv_w, v_a_conv_b, v_b_ln_g, v_b_ln_b, v_b_w_s, v_b_b_s, v_final_g]
    as2d = lambda t_: t_.reshape(-1, t_.shape[-1])
    dls, nms, nvs = _adamw_small([as2d(t_) for t_ in small_g], [as2d(t_) for t_ in small_w],
                                 [as2d(t_) for t_ in small_m], [as2d(t_) for t_ in small_v], "adamw_small")
    for a, nme in enumerate(small_names):
        shp = small_w[a].shape
        res[nme] = (small_g[a].reshape(shp), dls[a].reshape(shp), nms[a].reshape(shp), nvs[a].reshape(shp))

    order = ["mod_w", "mod_b", "norm_g", "a_w_in", "a_conv_w", "a_conv_b", "a_w_out", "b_w_in", "b_ln_g", "b_ln_b",
             "b_w_s", "b_b_s", "b_w_out", "final_g"]
    return (loss, dx0.reshape(x.shape), *[res[k][0] for k in order], *[res[k][1] for k in order],
            *[res[k][2] for k in order], *[res[k][3] for k in order])
```

```python
import functools

import jax
import jax.numpy as jnp
from jax import lax
from jax.experimental import pallas as pl
from jax.experimental.pallas import tpu as pltpu

NDEV = 8
CHUNK = 128
GROUPS = 8
RMS_EPS = 1e-6
LN_EPS = 1e-5
ADAM_LR, ADAM_B1, ADAM_B2, ADAM_EPS, ADAM_WD, ADAM_STEP = 0.001, 0.9, 0.999, 1e-08, 0.01, 10
V7X_VMEM_BYTES = 64 * 1024 * 1024
VMEM_LIMIT = V7X_VMEM_BYTES - 8 * 1024 * 1024
TN_VMEM_BUDGET = 46 * 1024 * 1024
PACK_W = 1024
F32, BF16 = jnp.float32, jnp.bfloat16
MESH = pl.DeviceIdType.MESH
RSQRT2 = 0.7071067811865476
INV_SQRT_2PI = 0.3989422804014327
NT_DIMS = (((1,), (1,)), ((), ()))
TN_DIMS = (((0,), (0,)), ((), ()))


def _params(sem=None):
    return pltpu.CompilerParams(dimension_semantics=sem, vmem_limit_bytes=VMEM_LIMIT)


def _vmem():
    return pl.BlockSpec(memory_space=pltpu.VMEM)


def _hbm():
    return pl.BlockSpec(memory_space=pltpu.HBM)


def _full(shape):
    return pl.BlockSpec(shape, lambda *_: (0,) * len(shape))


def _pos():
    return lax.axis_index("x"), lax.axis_index("y"), lax.axis_index("c")


def _index(p):
    return 4 * p[0] + 2 * p[1] + p[2]


def _peer(k):
    x, y, c = _pos()
    return ((1 - x) if (k >> 2) & 1 else x, (1 - y) if (k >> 1) & 1 else y, (1 - c) if k & 1 else c)


def _silu(z):
    sg = jax.nn.sigmoid(z)
    return z * sg, sg * (1.0 + z * (1.0 - sg))


def _gelu(v):
    phi = 0.5 * (1.0 + lax.erf(v * RSQRT2))
    return v * phi, phi + v * (jnp.exp(-0.5 * v * v) * INV_SQRT_2PI)


def _colsum(v):
    return jnp.sum(v, axis=0, keepdims=True)


def _rowsum(v):
    return jnp.sum(v, axis=-1, keepdims=True)


def _gather_all_vmem(slab_ref, send_sems, recv_sems, base):
    me = _index(_pos())
    sends = []
    for k in range(1, NDEV):
        cp = pltpu.make_async_remote_copy(
            src_ref=slab_ref.at[me], dst_ref=slab_ref.at[me],
            send_sem=send_sems.at[base + k - 1], recv_sem=recv_sems.at[base + k - 1],
            device_id=_peer(k), device_id_type=MESH)
        cp.start()
        sends.append(cp)
    for k in range(1, NDEV):
        src = _index(_peer(k))
        pltpu.make_async_remote_copy(
            src_ref=slab_ref.at[src], dst_ref=slab_ref.at[src],
            send_sem=send_sems.at[base + k - 1], recv_sem=recv_sems.at[base + k - 1],
            device_id=_peer(k), device_id_type=MESH).wait_recv()
    for cp in sends:
        cp.wait_send()


def _mod_vectors(c, mod_w, mod_b, ex):
    n_layers, d, w3 = mod_w.shape
    r_in, r_out = len(ex.arrays), len(ex.out_shapes)

    def body(*refs):
        c_ref, mw_ref, mb_ref = refs[:3]
        ex_ins = refs[3:3 + r_in]
        mod_ref, call_ref = refs[3 + r_in:5 + r_in]
        ex_outs = refs[5 + r_in:5 + r_in + r_out]
        cslab, pslab, send_sems, recv_sems = refs[5 + r_in + r_out:9 + r_in + r_out]
        ex_sems = refs[9 + r_in + r_out:]
        ex.start(ex_ins, ex_outs, ex_sems)
        me = _index(_pos())
        cv = c_ref[...]
        cslab[me] = jnp.broadcast_to(cv * jax.nn.sigmoid(cv), (8, d))
        _gather_all_vmem(cslab, send_sems, recv_sems, 0)
        c_all = jnp.concatenate([cslab[k, 0:1, :] for k in range(NDEV)], axis=0)
        call_ref[...] = c_all
        for i in range(n_layers):
            pslab[me, i * NDEV:(i + 1) * NDEV, :] = jnp.dot(
                c_all, mw_ref[i], preferred_element_type=F32, precision=lax.Precision.HIGHEST)
        _gather_all_vmem(pslab, send_sems, recv_sems, NDEV - 1)
        for i in range(n_layers):
            for k in range(NDEV):
                mod_ref[i:i + 1, k * w3:(k + 1) * w3] = (
                    pslab[k, pl.ds(i * NDEV + me, 1), :] + mb_ref[i:i + 1, k * w3:(k + 1) * w3])
        for passing_on in ex.middles:
            passing_on(ex_ins, ex_outs, ex_sems)
        ex.finish(ex_ins, ex_outs, ex_sems)

    out = pl.pallas_call(
        body, name="mod_vectors",
        out_shape=(jax.ShapeDtypeStruct((n_layers, 3 * d), F32), jax.ShapeDtypeStruct((NDEV, d), F32), *ex.out_shapes),
        in_specs=[_vmem(), _vmem(), _vmem()] + [_hbm()] * r_in, out_specs=(_vmem(), _vmem(), *([_hbm()] * r_out)),
        scratch_shapes=[pltpu.VMEM((NDEV, 8, d), F32), pltpu.VMEM((NDEV, n_layers * NDEV, w3), F32),
                        pltpu.SemaphoreType.DMA((2 * (NDEV - 1),)), pltpu.SemaphoreType.DMA((2 * (NDEV - 1),)), *ex.sems],
        compiler_params=pltpu.CompilerParams(vmem_limit_bytes=VMEM_LIMIT),
    )(c, mod_w, mod_b, *ex.arrays)
    return out[0], out[1], out[2:]


class _Exchange:
    def __init__(self, arrays, out_shapes, sems, start, middles, finish):
        self.arrays, self.out_shapes, self.sems = list(arrays), list(out_shapes), list(sems)
        self.start, self.middles, self.finish = start, list(middles), finish


def _gather_exchange(shards, by_cols):
    n = len(shards)
    shapes = [sh.shape for sh in shards]

    def tools(ins, outs, sems):
        send_sems, recv_sems, local_sems = sems
        x, y, c = _pos()
        chips = [(1 - x, y), (x, 1 - y), (1 - x, 1 - y)]
        south = c == 0
        relayed = (jnp.where(south, 1 - x, x), jnp.where(south, y, 1 - y), c)
        relay_to = (jnp.where(south, x, 1 - x), jnp.where(south, 1 - y, y), c)

        def place(a, block):
            r, cc = shapes[a]
            if by_cols[a]:
                return outs[a].at[:, pl.ds(_index(block) * cc, cc)]
            return outs[a].at[pl.ds(_index(block) * r, r), :]

        def copy(a, k, block, to, src=None):
            dst = place(a, block)
            return pltpu.make_async_remote_copy(
                src_ref=dst if src is None else src, dst_ref=dst,
                send_sem=send_sems.at[a * 7 + k], recv_sem=recv_sems.at[a * 7 + k],
                device_id=to, device_id_type=MESH)

        mine = [pltpu.make_async_copy(ins[a], place(a, (x, y, c)), local_sems.at[a]) for a in range(n)]
        first = []
        for a in range(n):
            first.append(copy(a, 0, (x, y, c), (x, y, 1 - c), src=ins[a]))
            first += [copy(a, 1 + j, (x, y, c), (*chip, c), src=ins[a]) for j, chip in enumerate(chips[:2])]
        relays = [copy(a, 3, relayed, relay_to) for a in range(n)]
        passed = [copy(a, 4 + j, (*chip, c), (x, y, 1 - c)) for j, chip in enumerate(chips) for a in range(n)]
        return (x, y, c), chips, copy, mine, first, relays, passed

    def start(ins, outs, sems):
        _, _, _, mine, first, _, _ = tools(ins, outs, sems)
        for cp in mine + first:
            cp.start()

    def pass_neighbours(ins, outs, sems):
        (x, y, c), chips, copy, _, _, relays, passed = tools(ins, outs, sems)
        for j, chip in enumerate(chips[:2]):
            for a in range(n):
                copy(a, 1 + j, (*chip, c), (x, y, c)).wait_recv()
                passed[j * n + a].start()
        for cp in relays:
            cp.start()

    def pass_diagonal(ins, outs, sems):
        (x, y, c), chips, copy, _, _, _, passed = tools(ins, outs, sems)
        for a in range(n):
            copy(a, 3, (*chips[2], c), (x, y, c)).wait_recv()
            passed[2 * n + a].start()

    def finish(ins, outs, sems):
        (x, y, c), chips, copy, mine, first, relays, passed = tools(ins, outs, sems)
        for a in range(n):
            copy(a, 0, (x, y, 1 - c), (x, y, c)).wait_recv()
        for j, chip in enumerate(chips):
            for a in range(n):
                copy(a, 4 + j, (*chip, 1 - c), (x, y, c)).wait_recv()
        for cp in first + relays + passed:
            cp.wait_send()
        for cp in mine:
            cp.wait()

    out_shapes = [jax.ShapeDtypeStruct((r, NDEV * cc) if bc else (NDEV * r, cc), sh.dtype)
                  for (r, cc), bc, sh in zip(shapes, by_cols, shards)]
    sems = [pltpu.SemaphoreType.DMA((7 * n,)), pltpu.SemaphoreType.DMA((7 * n,)), pltpu.SemaphoreType.DMA((n,))]
    return _Exchange(shards, out_shapes, sems, start, [pass_neighbours, pass_diagonal], finish)


def _scatter_exchange(parts):
    n = len(parts)

    def tools(ins, outs, sems):
        send_sems, recv_sems, local_sems = sems
        me = _index(_pos())
        mine = [pltpu.make_async_copy(ins[a].at[me], outs[a].at[me], local_sems.at[a]) for a in range(n)]
        sends, arrivals = [], []
        for k in range(1, NDEV):
            peer = _peer(k)
            for a in range(n):
                pair = dict(send_sem=send_sems.at[a * 7 + k - 1], recv_sem=recv_sems.at[a * 7 + k - 1],
                            device_id=peer, device_id_type=MESH)
                sends.append(pltpu.make_async_remote_copy(src_ref=ins[a].at[_index(peer)], dst_ref=outs[a].at[me], **pair))
                slot = outs[a].at[_index(peer)]
                arrivals.append(pltpu.make_async_remote_copy(src_ref=slot, dst_ref=slot, **pair))
        return mine, sends, arrivals

    def start(ins, outs, sems):
        mine, sends, _ = tools(ins, outs, sems)
        for cp in mine + sends:
            cp.start()

    def finish(ins, outs, sems):
        mine, sends, arrivals = tools(ins, outs, sems)
        for cp in arrivals:
            cp.wait_recv()
        for cp in sends:
            cp.wait_send()
        for cp in mine:
            cp.wait()

    out_shapes = [jax.ShapeDtypeStruct(p.shape, p.dtype) for p in parts]
    sems = [pltpu.SemaphoreType.DMA((7 * n,)), pltpu.SemaphoreType.DMA((7 * n,)), pltpu.SemaphoreType.DMA((n,))]
    return _Exchange(parts, out_shapes, sems, start, [], finish)


def _scatter_start(part, name):
    def body(part_ref, land_ref, send_sems, recv_sems, part_thru, land_thru, token):
        me = _index(_pos())
        for k in range(1, NDEV):
            peer = _peer(k)
            pltpu.make_async_remote_copy(
                src_ref=part_ref.at[_index(peer)], dst_ref=land_ref.at[me],
                send_sem=send_sems.at[k - 1], recv_sem=recv_sems.at[k - 1],
                device_id=peer, device_id_type=MESH).start()
        token[...] = jnp.zeros_like(token)

    sem = pl.BlockSpec(memory_space=pltpu.SEMAPHORE)
    return pl.pallas_call(
        body, name=name,
        out_shape=(pltpu.SemaphoreType.DMA((NDEV - 1,)), pltpu.SemaphoreType.DMA((NDEV - 1,)),
                   pltpu.HBM(part.shape, part.dtype), pltpu.HBM(part.shape, part.dtype),
                   jax.ShapeDtypeStruct((8, 128), F32)),
        in_specs=(_hbm(), _hbm()), out_specs=(sem, sem, _hbm(), _hbm(), _vmem()),
        input_output_aliases={0: 2, 1: 3},
        compiler_params=pltpu.CompilerParams(has_side_effects=pltpu.SideEffectType.DATAFLOW_SIDE_EFFECTING),
    )(pltpu.with_memory_space_constraint(part, pltpu.HBM),
      pltpu.with_memory_space_constraint(lax.empty(part.shape, part.dtype), pltpu.HBM))


def _scatter_wait(send_sems, recv_sems, part_thru, land_thru, after, name):
    def body(part_ref, land_ref, send_sems, recv_sems, after_ref, part_dead, got_ref):
        for k in range(1, NDEV):
            peer = _peer(k)
            slot = land_ref.at[_index(peer)]
            copy = pltpu.make_async_remote_copy(
                src_ref=part_ref.at[_index(peer)], dst_ref=slot,
                send_sem=send_sems.at[k - 1], recv_sem=recv_sems.at[k - 1],
                device_id=peer, device_id_type=MESH)
            copy.wait_send()
            copy.wait_recv()

    sem = pl.BlockSpec(memory_space=pltpu.SEMAPHORE)
    return pl.pallas_call(
        body, name=name,
        out_shape=(pltpu.HBM(part_thru.shape, part_thru.dtype), pltpu.HBM(land_thru.shape, land_thru.dtype)),
        in_specs=(_hbm(), _hbm(), sem, sem, pl.BlockSpec(memory_space=pl.ANY)), out_specs=(_hbm(), _hbm()),
        input_output_aliases={0: 0, 1: 1},
        compiler_params=pltpu.CompilerParams(has_side_effects=pltpu.SideEffectType.DATAFLOW_SIDE_EFFECTING),
    )(part_thru, land_thru, send_sems, recv_sems, after)


def _carry(ex, body, n_in, n_out, first, middle, last):
    if ex is None:
        return body
    r_in, r_out = len(ex.arrays), len(ex.out_shapes)

    def wrapped(*refs):
        ins, rins = refs[:n_in], refs[n_in:n_in + r_in]
        outs = refs[n_in + r_in:n_in + r_in + n_out]
        routs = refs[n_in + r_in + n_out:n_in + r_in + n_out + r_out]
        rest = refs[n_in + r_in + n_out + r_out:]
        scratch, sems = rest[:len(rest) - len(ex.sems)], rest[len(rest) - len(ex.sems):]

        @pl.when(first())
        def _():
            ex.start(rins, routs, sems)

        for passing_on, at_step in zip(ex.middles, middle or []):
            pl.when(at_step())(functools.partial(passing_on, rins, routs, sems))

        body(*ins, *outs, *scratch)

        @pl.when(last())
        def _():
            ex.finish(rins, routs, sems)

    return wrapped


def _carried(ex):
    if ex is None:
        return [], [], [], [], []
    return ex.arrays, [_hbm()] * len(ex.arrays), ex.out_shapes, [_hbm()] * len(ex.out_shapes), ex.sems


def _resident(shape):
    return pl.BlockSpec(shape, lambda *_: (0,) * len(shape), pipeline_mode=pl.Buffered(1))


def _norm_modulate(x_ref, g_ref, sc_ref, sh_ref):
    xv = x_ref[...]
    r = lax.rsqrt(jnp.mean(xv * xv, axis=-1, keepdims=True) + RMS_EPS)
    return ((xv * r) * g_ref[...] * (1.0 + sc_ref[...]) + sh_ref[...]).astype(BF16)


def _conv_taps(cx, t6, t7, row):
    p1 = jnp.where(row == 0, t7, pltpu.roll(cx, 1, 0))
    p2 = jnp.where(row == 0, t6, jnp.where(row == 1, t7, pltpu.roll(cx, 2, 0)))
    return p1, p2


def _layer_a_fwd(x, g, scale, shift, gate, wi, cw, cb, wo, name, ex=None):
    s, d = x.shape
    e = wo.shape[0]
    t = min(s, 256)
    n_t = s // t
    cwid = min(e, 512)

    def body(x_ref, g_ref, sc_ref, sh_ref, gate_ref, wi_ref, cw_ref, cb_ref, wo_ref,
             proj_ref, h_ref, x1_ref, br_ref, conv_ref, y_scr, tail_scr):
        @pl.when(pl.program_id(0) == 0)
        def _():
            tail_scr[...] = jnp.zeros_like(tail_scr)
        h_ref[...] = _norm_modulate(x_ref, g_ref, sc_ref, sh_ref)
        row = lax.broadcasted_iota(jnp.int32, (t, cwid), 0)

        def project(c0):
            v = jnp.dot(h_ref[...], wi_ref[:, c0:c0 + cwid], preferred_element_type=F32)
            proj_ref[:, c0:c0 + cwid] = v.astype(BF16)
            return v

        for c0 in range(0, e, cwid):
            sl = slice(c0, c0 + cwid)
            bg, z = project(c0), project(3 * e + c0)
            cx = project(e + c0) * project(2 * e + c0)
            p1, p2 = _conv_taps(cx, tail_scr[6:7, sl], tail_scr[7:8, sl], row)
            conv = cb_ref[:, sl] + cw_ref[2:3, sl] * cx + cw_ref[0:1, sl] * p2 + cw_ref[1:2, sl] * p1
            conv_ref[:, sl] = conv.astype(BF16)
            y_scr[:, sl] = (_silu(z)[0] * bg * conv).astype(BF16)
            tail_scr[:, sl] = cx[t - 8:t, :]
        br = jnp.dot(y_scr[...], wo_ref[...], preferred_element_type=F32)
        x1_ref[...] = x_ref[...] + gate_ref[...] * br
        br_ref[...] = br.astype(BF16)

    step = lambda k: (lambda: pl.program_id(0) == k)
    body = _carry(ex, body, 9, 5, step(0), [step(n_t // 3), step((2 * n_t) // 3)], step(n_t - 1))
    ex_args, ex_in, ex_shapes, ex_out, ex_sems = _carried(ex)
    tok = pl.BlockSpec((t, d), lambda i: (i, 0))
    out = pl.pallas_call(
        body, name=name, grid=(n_t,),
        out_shape=(jax.ShapeDtypeStruct((s, 4 * e), BF16), jax.ShapeDtypeStruct((s, d), BF16),
                   jax.ShapeDtypeStruct((s, d), F32), jax.ShapeDtypeStruct((s, d), BF16),
                   jax.ShapeDtypeStruct((s, e), BF16), *ex_shapes),
        in_specs=[tok, _full((1, d)), _full((1, d)), _full((1, d)), _full((1, d)), _resident((d, 4 * e)),
                  _full((3, e)), _full((1, e)), _resident((e, d)), *ex_in],
        out_specs=(pl.BlockSpec((t, 4 * e), lambda i: (i, 0)), tok, tok, tok,
                   pl.BlockSpec((t, e), lambda i: (i, 0)), *ex_out),
        scratch_shapes=[pltpu.VMEM((t, e), BF16), pltpu.VMEM((8, e), F32), *ex_sems],
        compiler_params=_params(("arbitrary",)),
    )(x, g, scale, shift, gate, wi, cw, cb, wo, *ex_args)
    return (*out[:5], out[5:])


def _ln_stats(v_of, v_scr, t, e):
    gw = e // GROUPS
    s1 = jnp.zeros((t, 1), F32)
    for g in range(GROUPS):
        v = v_of(g)
        v_scr[:, g * gw:(g + 1) * gw] = v
        s1 = s1 + _rowsum(v)
    mu = s1 * (1.0 / e)
    s2 = jnp.zeros((t, 1), F32)
    for g in range(GROUPS):
        dv = v_scr[:, g * gw:(g + 1) * gw] - mu
        s2 = s2 + _rowsum(dv * dv)
    return mu, lax.rsqrt(s2 * (1.0 / e) + LN_EPS)


def _layer_b_fwd_loss(x1, tgt, g1, scale, shift, gate, fg, wi, lng, lnb, wt, bsf, wo, name, ex=None):
    s, d = x1.shape
    e = wo.shape[0]
    gw = e // GROUPS
    t = min(s, 256)
    n_t = s // t

    def body(x1_ref, tgt_ref, g_ref, sc_ref, sh_ref, gate_ref, fg_ref, wi_ref, lng_ref, lnb_ref, wt_ref, bsf_ref, wo_ref,
             proj_ref, h_ref, dx2_ref, loss_ref, dfg_ref, dgate_ref, v_ref, dgv_ref, v_scr, y_scr):
        @pl.when(pl.program_id(0) == 0)
        def _():
            loss_ref[...] = jnp.zeros_like(loss_ref)
            dfg_ref[...] = jnp.zeros_like(dfg_ref)
            dgate_ref[...] = jnp.zeros_like(dgate_ref)
        h_ref[...] = _norm_modulate(x1_ref, g_ref, sc_ref, sh_ref)

        def project(c0):
            v = jnp.dot(h_ref[...], wi_ref[:, c0:c0 + gw], preferred_element_type=F32)
            proj_ref[:, c0:c0 + gw] = v.astype(BF16)
            return v

        def gelu_v(g):
            gs = slice(g * gw, (g + 1) * gw)
            v, dgv = _gelu(project(e + g * gw))
            v_ref[:, gs] = v.astype(BF16)
            dgv_ref[:, gs] = dgv.astype(BF16)
            return v

        mu, rs = _ln_stats(gelu_v, v_scr, t, e)
        for g in range(GROUPS):
            gs = slice(g * gw, (g + 1) * gw)
            vn = (((v_scr[:, gs] - mu) * rs) * lng_ref[:, gs] + lnb_ref[:, gs]).astype(BF16)
            u = _gelu(project(g * gw))[0]
            sz = _silu(project(2 * e + g * gw))[0]
            for ch in range(t // CHUNK):
                rows = slice(ch * CHUNK, (ch + 1) * CHUNK)
                mixed = jnp.dot(wt_ref[g], vn[rows], preferred_element_type=F32) + bsf_ref[:, gs]
                y_scr[rows, gs] = (sz[rows] * (u[rows] * mixed)).astype(BF16)
        br = jnp.dot(y_scr[...], wo_ref[...], preferred_element_type=F32)
        x2 = x1_ref[...] + gate_ref[...] * br
        r2 = lax.rsqrt(jnp.mean(x2 * x2, axis=-1, keepdims=True) + RMS_EPS)
        xn = x2 * r2
        diff = xn * fg_ref[...] - tgt_ref[...]
        loss_ref[...] += jnp.broadcast_to(0.5 * _colsum(jnp.mean(diff * diff, axis=-1, keepdims=True)), loss_ref.shape)
        dout = diff * (1.0 / d)
        dfg_ref[...] += _colsum(dout * xn)
        dxn = dout * fg_ref[...]
        dx2 = r2 * (dxn - xn * jnp.mean(dxn * xn, axis=-1, keepdims=True))
        dx2_ref[...] = dx2
        dgate_ref[...] += _colsum(dx2 * br)

    step = lambda k: (lambda: pl.program_id(0) == k)
    body = _carry(ex, body, 13, 8, step(0), [step(n_t // 3), step((2 * n_t) // 3)], step(n_t - 1))
    ex_args, ex_in, ex_shapes, ex_out, ex_sems = _carried(ex)
    tok = pl.BlockSpec((t, d), lambda i: (i, 0))
    vec = _full((1, d))
    out = pl.pallas_call(
        body, name=name, grid=(n_t,),
        out_shape=(jax.ShapeDtypeStruct((s, 3 * e), BF16), jax.ShapeDtypeStruct((s, d), BF16),
                   jax.ShapeDtypeStruct((s, d), F32), jax.ShapeDtypeStruct((8, 128), F32),
                   jax.ShapeDtypeStruct((1, d), F32), jax.ShapeDtypeStruct((1, d), F32),
                   jax.ShapeDtypeStruct((s, e), BF16), jax.ShapeDtypeStruct((s, e), BF16), *ex_shapes),
        in_specs=[tok, tok, vec, vec, vec, vec, vec, _resident((d, 3 * e)), _full((1, e)), _full((1, e)),
                  _full((GROUPS, CHUNK, CHUNK)), _resident((CHUNK, e)), _resident((e, d)), *ex_in],
        out_specs=(pl.BlockSpec((t, 3 * e), lambda i: (i, 0)), tok, tok, _full((8, 128)), vec, vec,
                   pl.BlockSpec((t, e), lambda i: (i, 0)), pl.BlockSpec((t, e), lambda i: (i, 0)), *ex_out),
        scratch_shapes=[pltpu.VMEM((t, e), F32), pltpu.VMEM((t, e), BF16), *ex_sems],
        compiler_params=_params(("arbitrary",)),
    )(x1, tgt, g1, scale, shift, gate, fg, wi, lng, lnb, wt, bsf, wo, *ex_args)
    return (*out[:8], out[8:])


def _norm_modulate_bwd(dh, x_ref, dres_ref, g_ref, sc_ref, dx_ref, dsh_ref, p_scr):
    xv = x_ref[...]
    r = lax.rsqrt(jnp.mean(xv * xv, axis=-1, keepdims=True) + RMS_EPS)
    xn = xv * r
    dsh_ref[...] += _colsum(dh)
    p_scr[...] += _colsum(dh * xn)
    dxn = dh * (g_ref[...] * (1.0 + sc_ref[...]))
    dx_ref[...] = r * (dxn - xn * jnp.mean(dxn * xn, axis=-1, keepdims=True)) + dres_ref[...]


def _layer_b_bwd(proj, v_act, dgv, dx2, x1, gate, g1, scale, lng, lnb, wt, wtt, bsf, wo, wi, name):
    s, e3 = proj.shape
    e = e3 // 3
    d = dx2.shape[1]
    gw = e // GROUPS
    t = min(s, 256)
    n_t = s // t

    def body(pu_ref, pz_ref, v_ref, dgv_ref, dx_ref, x1_ref, gate_ref, g_ref, sc_ref, lng_ref, lnb_ref, wt_ref, wtt_ref,
             bsf_ref, wo_ref, wi_ref,
             dp_ref, y_ref, dx1_ref, dws_ref, dbs_ref, dlg_ref, dlb_ref, dsh_ref, dsc_ref, dg_ref,
             v_scr, dbr_scr, dvn_scr, dbs_scr, p_scr, dy_scr, vn_scr, mixed_scr):
        @pl.when(pl.program_id(0) == 0)
        def _():
            dws_ref[...] = jnp.zeros_like(dws_ref)
            dlg_ref[...] = jnp.zeros_like(dlg_ref)
            dlb_ref[...] = jnp.zeros_like(dlb_ref)
            dsh_ref[...] = jnp.zeros_like(dsh_ref)
            dbs_scr[...] = jnp.zeros_like(dbs_scr)
            p_scr[...] = jnp.zeros_like(p_scr)
        dbr_scr[...] = (dx_ref[...] * gate_ref[...]).astype(BF16)
        mu, rs = _ln_stats(lambda g: v_ref[:, g * gw:(g + 1) * gw].astype(F32), v_scr, t, e)
        tril = (lax.broadcasted_iota(jnp.int32, (CHUNK, CHUNK), 0) >= lax.broadcasted_iota(jnp.int32, (CHUNK, CHUNK), 1))
        c1 = jnp.zeros((t, 1), F32)
        c2 = jnp.zeros((t, 1), F32)
        span = 2
        kw = span * gw
        dh = jnp.zeros((t, d), F32)

        def through_w_in(c0):
            return lax.dot_general(dp_ref[:, c0:c0 + kw], wi_ref[:, c0:c0 + kw], NT_DIMS, preferred_element_type=F32)

        dy_scr[...] = lax.dot_general(dbr_scr[...], wo_ref[...], NT_DIMS, preferred_element_type=F32)
        for g in range(GROUPS):
            gs = slice(g * gw, (g + 1) * gw)
            vhat = (v_scr[:, gs] - mu) * rs
            v_scr[:, gs] = vhat
            vn = (vhat * lng_ref[:, gs] + lnb_ref[:, gs]).astype(BF16)
            vn_scr[:, gs] = vn
            for ch in range(t // CHUNK):
                rows = slice(ch * CHUNK, (ch + 1) * CHUNK)
                mixed_scr[rows, gs] = jnp.dot(wt_ref[g], vn[rows], preferred_element_type=F32) + bsf_ref[:, gs]
        for g in range(GROUPS):
            gs = slice(g * gw, (g + 1) * gw)
            vhat = v_scr[:, gs]
            lg = lng_ref[:, gs]
            for ch in range(t // CHUNK):
                rows = slice(ch * CHUNK, (ch + 1) * CHUNK)
                mixed = mixed_scr[rows, gs]
                u, dgu = _gelu(pu_ref[rows, gs].astype(F32))
                sz, dsz = _silu(pz_ref[rows, gs].astype(F32))
                sgate = u * mixed
                y_ref[rows, gs] = (sz * sgate).astype(BF16)
                dy = dy_scr[rows, gs]
                dp_ref[rows, 2 * e + g * gw:2 * e + (g + 1) * gw] = (dy * sgate * dsz).astype(BF16)
                ds = dy * sz
                dp_ref[rows, gs] = (ds * mixed * dgu).astype(BF16)
                dm = ds * u
                dbs_scr[:, gs] += dm
                dmb = dm.astype(BF16)
                dws_ref[g] += jnp.where(tril, lax.dot_general(dmb, vn_scr[rows, gs], NT_DIMS, preferred_element_type=F32), 0.0)
                dvn_scr[rows, gs] = jnp.dot(wtt_ref[g], dmb, preferred_element_type=F32)
            dvn = dvn_scr[:, gs]
            dlb_ref[:, gs] += _colsum(dvn)
            dlg_ref[:, gs] += _colsum(dvn * vhat)
            dvh = dvn * lg
            c1 = c1 + _rowsum(dvh)
            c2 = c2 + _rowsum(dvh * vhat)
            if g % span == span - 1:
                dh = dh + through_w_in(g * gw + gw - kw) + through_w_in(2 * e + g * gw + gw - kw)
        c1 = c1 * (1.0 / e)
        c2 = c2 * (1.0 / e)
        for g in range(GROUPS):
            gs = slice(g * gw, (g + 1) * gw)
            dv = rs * (dvn_scr[:, gs] * lng_ref[:, gs] - c1 - v_scr[:, gs] * c2)
            dp_ref[:, e + g * gw:e + (g + 1) * gw] = (dv * dgv_ref[:, gs]).astype(BF16)
            if g % span == span - 1:
                dh = dh + through_w_in(e + g * gw + gw - kw)
        _norm_modulate_bwd(dh, x1_ref, dx_ref, g_ref, sc_ref, dx1_ref, dsh_ref, p_scr)

        @pl.when(pl.program_id(0) == n_t - 1)
        def _():
            lane = lax.broadcasted_iota(jnp.int32, (CHUNK, 128), 1)
            acc = jnp.zeros((CHUNK, 128), F32)
            for g in range(GROUPS):
                acc = acc + jnp.where(lane == g, _rowsum(dbs_scr[:, g * gw:(g + 1) * gw]), 0.0)
            dbs_ref[...] = acc
            dsc_ref[...] = p_scr[...] * g_ref[...]
            dg_ref[...] = p_scr[...] * (1.0 + sc_ref[...])

    tok = pl.BlockSpec((t, d), lambda i: (i, 0))
    vec, evec, ws = _full((1, d)), _full((1, e)), _full((GROUPS, CHUNK, CHUNK))
    vshape = jax.ShapeDtypeStruct((1, d), F32)
    return pl.pallas_call(
        body, name=name, grid=(n_t,),
        out_shape=(jax.ShapeDtypeStruct((s, e3), BF16), jax.ShapeDtypeStruct((s, e), BF16), jax.ShapeDtypeStruct((s, d), F32),
                   jax.ShapeDtypeStruct((GROUPS, CHUNK, CHUNK), F32), jax.ShapeDtypeStruct((CHUNK, 128), F32),
                   jax.ShapeDtypeStruct((1, e), F32), jax.ShapeDtypeStruct((1, e), F32), vshape, vshape, vshape),
        in_specs=[pl.BlockSpec((t, e), lambda i: (i, 0)), pl.BlockSpec((t, e), lambda i: (i, 2)),
                  pl.BlockSpec((t, e), lambda i: (i, 0)), pl.BlockSpec((t, e), lambda i: (i, 0)),
                  tok, tok, vec, vec, vec, evec, evec, ws, ws,
                  _resident((CHUNK, e)), _resident((e, d)), _resident((d, e3))],
        out_specs=(pl.BlockSpec((t, e3), lambda i: (i, 0)), pl.BlockSpec((t, e), lambda i: (i, 0)), tok,
                   ws, _full((CHUNK, 128)), evec, evec, vec, vec, vec),
        scratch_shapes=[pltpu.VMEM((t, e), F32), pltpu.VMEM((t, d), BF16),
                        pltpu.VMEM((t, e), F32), pltpu.VMEM((CHUNK, e), F32), pltpu.VMEM((1, d), F32),
                        pltpu.VMEM((t, e), F32), pltpu.VMEM((t, e), BF16), pltpu.VMEM((t, e), F32)],
        compiler_params=_params(("arbitrary",)),
    )(proj, proj, v_act, dgv, dx2, x1, gate, g1, scale, lng, lnb, wt, wtt, bsf, wo, wi)


def _conv_mixer_bwd(proj, dx1, br, conv_a, cw, gate, wo, name, ex=None):
    s, e4 = proj.shape
    e = e4 // 4
    d = dx1.shape[1]
    t = min(s, 256)
    n_t = s // t
    cwid = min(e, 512)

    def body(p_ref, dx_ref, br_ref, conv_ref, cw_ref, gate_ref, wo_ref,
             dp_ref, y_ref, dgate_ref, dcb_ref, dcw_ref, dy_scr, head_scr):
        i = pl.program_id(0)

        @pl.when(i == 0)
        def _():
            dgate_ref[...] = jnp.zeros_like(dgate_ref)
            dcb_ref[...] = jnp.zeros_like(dcb_ref)
            dcw_ref[...] = jnp.zeros_like(dcw_ref)
            head_scr[...] = jnp.zeros_like(head_scr)
        dx = dx_ref[...]
        dgate_ref[...] += _colsum(dx * br_ref[...].astype(F32))
        dy_scr[...] = lax.dot_general((dx * gate_ref[...]).astype(BF16), wo_ref[...], NT_DIMS,
                                      preferred_element_type=F32)
        row = lax.broadcasted_iota(jnp.int32, (t, cwid), 0)
        for c0 in range(0, e, cwid):
            sl = slice(c0, c0 + cwid)
            bg = p_ref[:, c0:c0 + cwid].astype(F32)
            cg = p_ref[:, e + c0:e + c0 + cwid].astype(F32)
            xin = p_ref[:, 2 * e + c0:2 * e + c0 + cwid].astype(F32)
            z = p_ref[:, 3 * e + c0:3 * e + c0 + cwid].astype(F32)
            cx = cg * xin
            w0, w1, w2 = cw_ref[0:1, sl], cw_ref[1:2, sl], cw_ref[2:3, sl]
            conv = conv_ref[:, sl].astype(F32)
            sz, dsz = _silu(z)
            dy = dy_scr[:, sl]
            y_ref[:, sl] = (sz * bg * conv).astype(BF16)
            dp_ref[:, 3 * e + c0:3 * e + c0 + cwid] = (dy * bg * conv * dsz).astype(BF16)
            dp_ref[:, c0:c0 + cwid] = (dy * sz * conv).astype(BF16)
            dconv = dy * sz * bg
            h0, h1 = head_scr[0:1, sl], head_scr[1:2, sl]
            n1 = jnp.where(row == t - 1, h0, pltpu.roll(dconv, t - 1, 0))
            n2 = jnp.where(row == t - 2, h0, jnp.where(row == t - 1, h1, pltpu.roll(dconv, t - 2, 0)))
            dcb_ref[:, sl] += _colsum(dconv)
            dcw_ref[2:3, sl] += _colsum(dconv * cx)
            dcw_ref[1:2, sl] += _colsum(n1 * cx)
            dcw_ref[0:1, sl] += _colsum(n2 * cx)
            dcx = w2 * dconv + w1 * n1 + w0 * n2
            dp_ref[:, e + c0:e + c0 + cwid] = (dcx * xin).astype(BF16)
            dp_ref[:, 2 * e + c0:2 * e + c0 + cwid] = (dcx * cg).astype(BF16)
            head_scr[:, sl] = dconv[0:8, :]

    body = _carry(ex, body, 7, 5, lambda: pl.program_id(0) == 0, None, lambda: pl.program_id(0) == n_t - 1)
    ex_args, ex_in, ex_shapes, ex_out, ex_sems = _carried(ex)
    rev = lambda i: (n_t - 1 - i, 0)
    out = pl.pallas_call(
        body, name=name, grid=(n_t,),
        out_shape=(jax.ShapeDtypeStruct((s, e4), BF16), jax.ShapeDtypeStruct((s, e), BF16),
                   jax.ShapeDtypeStruct((1, d), F32), jax.ShapeDtypeStruct((1, e), F32), jax.ShapeDtypeStruct((8, e), F32),
                   *ex_shapes),
        in_specs=[pl.BlockSpec((t, e4), rev), pl.BlockSpec((t, d), rev), pl.BlockSpec((t, d), rev),
                  pl.BlockSpec((t, e), rev), _full((3, e)), _full((1, d)), _full((e, d)), *ex_in],
        out_specs=(pl.BlockSpec((t, e4), rev), pl.BlockSpec((t, e), rev), _full((1, d)), _full((1, e)), _full((8, e)),
                   *ex_out),
        scratch_shapes=[pltpu.VMEM((t, e), F32), pltpu.VMEM((8, e), F32), *ex_sems],
        compiler_params=_params(("arbitrary",)),
    )(proj, dx1, br, conv_a, cw, gate, wo, *ex_args)
    return (*out[:5], out[5:])


def _matmul_nt_norm_bwd(dproj, w, xin, dres, g, scale, name, ex=None):
    s, d = xin.shape
    n = w.shape[1]
    tm = min(s, 512)
    n_i = s // tm

    def body(dp_ref, w_ref, x_ref, dres_ref, g_ref, sc_ref, dx_ref, dsh_ref, dsc_ref, dg_ref, p_scr):
        i = pl.program_id(0)

        @pl.when(i == 0)
        def _():
            dsh_ref[...] = jnp.zeros_like(dsh_ref)
            p_scr[...] = jnp.zeros_like(p_scr)
        dh = lax.dot_general(dp_ref[...], w_ref[...], NT_DIMS, preferred_element_type=F32)
        _norm_modulate_bwd(dh, x_ref, dres_ref, g_ref, sc_ref, dx_ref, dsh_ref, p_scr)

        @pl.when(i == n_i - 1)
        def _():
            dsc_ref[...] = p_scr[...] * g_ref[...]
            dg_ref[...] = p_scr[...] * (1.0 + sc_ref[...])

    body = _carry(ex, body, 6, 4, lambda: pl.program_id(0) == 0, None, lambda: pl.program_id(0) == n_i - 1)
    ex_args, ex_in, ex_shapes, ex_out, ex_sems = _carried(ex)
    tok = pl.BlockSpec((tm, d), lambda i: (i, 0))
    vec = pl.BlockSpec((1, d), lambda i: (0, 0))
    vshape = jax.ShapeDtypeStruct((1, d), F32)
    out = pl.pallas_call(
        body, name=name, grid=(n_i,),
        out_shape=(jax.ShapeDtypeStruct((s, d), F32), vshape, vshape, vshape, *ex_shapes),
        in_specs=[pl.BlockSpec((tm, n), lambda i: (i, 0)), _resident((d, n)), tok, tok, vec, vec, *ex_in],
        out_specs=(tok, vec, vec, vec, *ex_out),
        scratch_shapes=[pltpu.VMEM((1, d), F32), *ex_sems],
        compiler_params=_params(("arbitrary",)),
    )(dproj, w, xin, dres, g, scale, *ex_args)
    return (*out[:4], out[4:])


def _matmul_tn(a, b, colscale, rows_split, name, ex=None, a_cols=None):
    s, m = a.shape
    a_blk = 0
    if a_cols is not None:
        a_blk, m = a_cols
    n = b.shape[1]
    n_j, tn = (1, n) if rows_split else (NDEV, n // NDEV)
    fixed = (4 + 4 + 2 * 2) * m * tn
    tk = s
    while fixed + 2 * tk * (2 * m + b.dtype.itemsize * tn) > TN_VMEM_BUDGET:
        tk //= 2
    n_k = s // tk

    def body(a_ref, b_ref, cs_ref, o_ref, acc):
        k = pl.program_id(1)
        part = lax.dot_general(a_ref[...], b_ref[...].astype(BF16), TN_DIMS, preferred_element_type=F32)
        if n_k == 1:
            o_ref[...] = (part * cs_ref[...]).astype(BF16)
            return

        @pl.when(k == 0)
        def _():
            acc[...] = part

        @pl.when((k > 0) & (k < n_k - 1))
        def _():
            acc[...] += part

        @pl.when(k == n_k - 1)
        def _():
            o_ref[...] = ((acc[...] + part) * cs_ref[...]).astype(BF16)

    at = lambda j, k: (pl.program_id(0) == j) & (pl.program_id(1) == k)
    body = _carry(ex, body, 3, 1, lambda: at(0, 0), None, lambda: at(n_j - 1, n_k - 1))
    ex_args, ex_in, ex_shapes, ex_out, ex_sems = _carried(ex)
    out = pl.pallas_call(
        body, name=name, grid=(n_j, n_k),
        out_shape=(jax.ShapeDtypeStruct((n_j, m, tn), BF16), *ex_shapes),
        in_specs=[pl.BlockSpec((tk, m), lambda j, k: (k, a_blk)), pl.BlockSpec((tk, tn), lambda j, k: (k, j)),
                  pl.BlockSpec((1, tn), lambda j, k: (0, j)), *ex_in],
        out_specs=(pl.BlockSpec((None, m, tn), lambda j, k: (j, 0, 0)), *ex_out),
        scratch_shapes=[pltpu.VMEM((m, tn), F32), *ex_sems],
        compiler_params=_params(("arbitrary", "arbitrary")),
    )(a, b, colscale, *ex_args)
    return (out[0].reshape(NDEV, m // NDEV, n) if rows_split else out[0]), out[1:]


def _adam_update(w, g, m, v):
    m = ADAM_B1 * m + (1.0 - ADAM_B1) * g
    v = ADAM_B2 * v + (1.0 - ADAM_B2) * (g * g)
    m_hat = m / (1.0 - ADAM_B1 ** ADAM_STEP)
    v_hat = v / (1.0 - ADAM_B2 ** ADAM_STEP)
    return -ADAM_LR * (m_hat / (jnp.sqrt(v_hat) + ADAM_EPS) + ADAM_WD * w), m, v


def _adamw_reduce(parts, w, m, v, name, ex=None):
    n_l = len(parts)
    n_p, _, c = parts[0].shape
    rows = [p.shape[1] for p in parts]
    r = sum(rows)
    tr = min(min(rows), 128 if ex is not None else 256)
    n_i = r // tr
    tiles = [r_l // tr for r_l in rows]
    first_tile = [sum(tiles[:l]) for l in range(n_l)]

    def body(*refs):
        p_refs, (w_ref, m_ref, v_ref, g_out, d_out, m_out, v_out) = refs[:n_l], refs[n_l:]
        g = None
        for l, p_ref in enumerate(p_refs):
            g_l = p_ref[0].astype(F32)
            for j in range(1, n_p):
                g_l = g_l + p_ref[j].astype(F32)
            g = g_l if g is None else jnp.where(pl.program_id(0) >= first_tile[l], g_l, g)
        g_out[...] = g
        d_out[...], m_out[...], v_out[...] = _adam_update(w_ref[...], g, m_ref[...], v_ref[...])

    step = lambda k: (lambda: pl.program_id(0) == k)
    body = _carry(ex, body, n_l + 3, 4, step(0), [step(n_i // 3), step((2 * n_i) // 3)], step(n_i - 1))
    ex_args, ex_in, ex_shapes, ex_out, ex_sems = _carried(ex)
    blk = pl.BlockSpec((tr, c), lambda i: (i, 0))
    p_specs = [pl.BlockSpec((n_p, tr, c), lambda i, l=l: (0, jnp.clip(i - first_tile[l], 0, tiles[l] - 1), 0))
               for l in range(n_l)]
    shp = jax.ShapeDtypeStruct((r, c), F32)
    out = pl.pallas_call(
        body, name=name, grid=(n_i,), out_shape=(shp, shp, shp, shp, *ex_shapes),
        in_specs=[*p_specs, blk, blk, blk, *ex_in],
        out_specs=(blk, blk, blk, blk, *ex_out), scratch_shapes=ex_sems,
        compiler_params=_params(("arbitrary",)),
    )(*parts, w, m, v, *ex_args)
    return (*out[:4], out[4:])


def _adamw_small(gs, ws, ms, vs, name):
    n = len(gs)

    def body(*refs):
        ins, outs = refs[:4 * n], refs[4 * n:]
        for a in range(n):
            d, m, v = _adam_update(ins[n + a][...], ins[a][...], ins[2 * n + a][...], ins[3 * n + a][...])
            outs[a][...], outs[n + a][...], outs[2 * n + a][...] = d, m, v

    shapes = tuple(jax.ShapeDtypeStruct(w.shape, F32) for w in ws) * 3
    out = pl.pallas_call(
        body, name=name, out_shape=shapes,
        in_specs=[_vmem()] * (4 * n), out_specs=tuple([_vmem()] * (3 * n)),
        compiler_params=pltpu.CompilerParams(vmem_limit_bytes=VMEM_LIMIT),
    )(*gs, *ws, *ms, *vs)
    return out[:n], out[n:2 * n], out[2 * n:]


def _sum_devices(packed, name):
    _, r, wdt = packed.shape

    def body(p_ref, o_ref):
        acc = p_ref[0]
        for j in range(1, NDEV):
            acc = acc + p_ref[j]
        o_ref[...] = acc

    return pl.pallas_call(
        body, name=name, out_shape=jax.ShapeDtypeStruct((r, wdt), F32),
        in_specs=[_vmem()], out_specs=_vmem(),
        compiler_params=pltpu.CompilerParams(vmem_limit_bytes=VMEM_LIMIT),
    )(packed)


def _mod_w_grad(c_t, dmod, name):
    n_layers, _, w3 = dmod.shape
    d = c_t.shape[0]

    def body(c_ref, dm_ref, o_ref):
        for i in range(n_layers):
            acc = c_ref[:, 0:1] * dm_ref[i, 0:1, :]
            for b in range(1, NDEV):
                acc = acc + c_ref[:, b:b + 1] * dm_ref[i, b:b + 1, :]
            o_ref[i] = acc

    return pl.pallas_call(
        body, name=name, out_shape=jax.ShapeDtypeStruct((n_layers, d, w3), F32),
        in_specs=[_vmem(), _vmem()], out_specs=_vmem(),
        compiler_params=pltpu.CompilerParams(vmem_limit_bytes=VMEM_LIMIT),
    )(c_t, dmod)


def _mask_transpose_ws(w_s, name):
    def body(w_ref, wt_ref, wtt_ref):
        tril = (lax.broadcasted_iota(jnp.int32, (CHUNK, CHUNK), 0) >= lax.broadcasted_iota(jnp.int32, (CHUNK, CHUNK), 1))
        for g in range(GROUPS):
            wm = jnp.where(tril, w_ref[g], 0.0)
            wt_ref[g] = wm.astype(BF16)
            wtt_ref[g] = wm.T.astype(BF16)

    shp = jax.ShapeDtypeStruct(w_s.shape, BF16)
    return pl.pallas_call(
        body, name=name, out_shape=(shp, shp), in_specs=[_vmem()], out_specs=(_vmem(), _vmem()),
    )(w_s)


def _pack(pieces):
    flat = jnp.concatenate([p.reshape(-1) for p in pieces])
    rows = -(-flat.shape[0] // (8 * PACK_W)) * 8
    return jnp.pad(flat, (0, rows * PACK_W - flat.shape[0])).reshape(rows, PACK_W)


def _unpack(flat, shapes):
    out, off = [], 0
    for shp in shapes:
        size = 1
        for dim in shp:
            size *= dim
        out.append(flat[off:off + size].reshape(shp))
        off += size
    return out


def kernel(x, c, mod_w, mod_b, norm_g, a_w_in, a_conv_w, a_conv_b, a_w_out, b_w_in, b_ln_g, b_ln_b, b_w_s, b_b_s, b_w_out, final_g, loss_target, m_mod_w, m_mod_b, m_norm_g, m_a_w_in, m_a_conv_w, m_a_conv_b, m_a_w_out, m_b_w_in, m_b_ln_g, m_b_ln_b, m_b_w_s, m_b_b_s, m_b_w_out, m_final_g, v_mod_w, v_mod_b, v_norm_g, v_a_w_in, v_a_conv_w, v_a_conv_b, v_a_w_out, v_b_w_in, v_b_ln_g, v_b_ln_b, v_b_w_s, v_b_b_s, v_b_w_out, v_final_g):
    s, d = x.shape[1], x.shape[2]
    es = a_w_out.shape[1]
    e = NDEV * es
    w3 = mod_w.shape[2]
    me = _index(_pos())
    x0 = x.reshape(s, d)
    tgt = loss_target.reshape(s, d)

    small = jnp.concatenate([a_conv_w[0], b_ln_g, b_ln_b, jnp.zeros((3, es), F32)], axis=0)
    gather_a = _gather_exchange([a_w_in[0].astype(BF16), a_w_out[0].astype(BF16), small], [True, False, True])
    gather_b = _gather_exchange([b_w_in[0].astype(BF16), b_w_out[0].astype(BF16)], [True, False])
    mod, c_all, (wa, woa, small_all) = _mod_vectors(c, mod_w, mod_b, gather_a)
    conv_w, ln_g, ln_b = small_all[0:3], small_all[3:4], small_all[4:5]
    bsf = jnp.repeat(b_b_s[0].T, e // GROUPS, axis=1)
    wt, wtt = _mask_transpose_ws(b_w_s[0], "mask_w_s")
    shift0, scale0, gate0 = mod[0:1, 0:d], mod[0:1, d:2 * d], mod[0:1, 2 * d:]
    shift1, scale1, gate1 = mod[1:2, 0:d], mod[1:2, d:2 * d], mod[1:2, 2 * d:]
    g0, g1, fg = norm_g[0:1], norm_g[1:2], final_g.reshape(1, d)

    proj_a, h0, x1, br_a, conv_a, (wb, wob) = _layer_a_fwd(
        x0, g0, scale0, shift0, gate0, wa, conv_w, a_conv_b, woa, "a_fwd", gather_b)
    proj_b, h1, dx2, loss_acc, dfg, dgate1, v_b, dgv_b, _ = _layer_b_fwd_loss(
        x1, tgt, g1, scale1, shift1, gate1, fg, wb, ln_g, ln_b, wt, bsf, wob, "b_fwd_loss")

    dproj_b, y_b, dx1, dws, dbs, dlg, dlb, dshift1, dscale1, dg1 = _layer_b_bwd(
        proj_b, v_b, dgv_b, dx2, x1, gate1, g1, scale1, ln_g, ln_b, wt, wtt, bsf, wob, wb, "b_bwd")
    gs_b_out, _ = _matmul_tn(y_b, dx2, gate1, True, "b_w_out_grad")
    gs_b_in, (gr_b_out,) = _matmul_tn(h1, dproj_b, jnp.ones((1, dproj_b.shape[1]), F32), False, "b_w_in_grad",
                                      _scatter_exchange([gs_b_out]))
    dproj_a, y_a, dgate0, dcb, dcw, (gr_b_in,) = _conv_mixer_bwd(
        proj_a, dx1, br_a, conv_a, conv_w, gate0, woa, "a_mixer_bwd", _scatter_exchange([gs_b_in]))
    gs_a_out, _ = _matmul_tn(y_a, dx1, gate0, True, "a_w_out_grad")
    gs_a_in, (gr_a_out,) = _matmul_tn(h0, dproj_a, jnp.ones((1, dproj_a.shape[1]), F32), False, "a_w_in_grad",
                                      _scatter_exchange([gs_a_out]))
    send_sems, recv_sems, gs_thru, land_thru, token = _scatter_start(gs_a_in, "a_w_in_scatter_start")
    dx0, dshift0, dscale0, dg0, _ = _matmul_nt_norm_bwd(dproj_a, wa, x0, dx1, g0, scale0 + token[0:1, 0:1], "a_in_bwd")
    gs_a_in, landed = _scatter_wait(send_sems, recv_sems, gs_thru, land_thru, dshift0, "a_w_in_scatter_wait")
    gr_a_in = lax.dynamic_update_slice_in_dim(landed, lax.dynamic_slice_in_dim(gs_a_in, me, 1, axis=0), me, axis=0)

    def big(parts, w, m, v, name, ex=None):
        shp = w.shape
        r2 = lambda t_: t_.reshape(-1, shp[-1])
        g, dl, nm, nv, ex_out = _adamw_reduce(parts, r2(w), r2(m), r2(v), name, ex)
        return tuple(t_.reshape(shp) for t_ in (g, dl, nm, nv)), ex_out

    res = {}
    res["b_w_in"], _ = big([gr_b_in], b_w_in, m_b_w_in, v_b_w_in, "adamw_b_w_in")
    gr_a_in = [gr_a_in]

    pieces = [dshift0, dscale0, dgate0, dshift1, dscale1, dgate1, dg0, dg1, dcb, dcw[0:3], dlg, dlb, dfg,
              dbs[:, 0:GROUPS].T, dws, loss_acc[0:1, 0:1]]
    shapes = [p.shape for p in pieces]
    res["a_w_in"], (packed_all,) = big(gr_a_in, a_w_in, m_a_w_in, v_a_w_in, "adamw_a_w_in",
                                       _gather_exchange([_pack(pieces)], [False]))
    packed_all = packed_all.reshape(NDEV, -1, PACK_W)
    total = _sum_devices(packed_all, "sum_small_grads").reshape(-1)
    (t_sh0, t_sc0, t_ga0, t_sh1, t_sc1, t_ga1, t_g0, t_g1, t_cb, t_cw, t_lg, t_lb, t_fg, t_bs, t_ws, t_loss) = _unpack(
        total, shapes)
    loss = t_loss.reshape(())
    grad_mod_b = jnp.concatenate([jnp.concatenate([t_sh0, t_sc0, t_ga0], axis=1),
                                  jnp.concatenate([t_sh1, t_sc1, t_ga1], axis=1)], axis=0)
    grad_norm_g = jnp.concatenate([t_g0, t_g1], axis=0)
    dmod_all = packed_all.reshape(NDEV, -1)[:, 0:6 * d].reshape(NDEV, 2, 3 * d).transpose(1, 0, 2)
    dmod_mine = lax.dynamic_slice_in_dim(dmod_all, me * w3, w3, axis=2)
    grad_mod_w = _mod_w_grad(c_all.T, dmod_mine, "mod_w_grad")
    grad_a_conv_w = lax.dynamic_slice_in_dim(t_cw, me * es, es, axis=1)
    grad_b_ln_g = lax.dynamic_slice_in_dim(t_lg, me * es, es, axis=1)
    grad_b_ln_b = lax.dynamic_slice_in_dim(t_lb, me * es, es, axis=1)

    res["mod_w"], _ = big([grad_mod_w.reshape(1, -1, w3)], mod_w, m_mod_w, v_mod_w, "adamw_mod_w")
    res["a_w_out"], _ = big([gr_a_out], a_w_out, m_a_w_out, v_a_w_out, "adamw_a_w_out")
    res["b_w_out"], _ = big([gr_b_out], b_w_out, m_b_w_out, v_b_w_out, "adamw_b_w_out")

    small_names = ["mod_b", "norm_g", "a_conv_w", "a_conv_b", "b_ln_g", "b_ln_b", "b_w_s", "b_b_s", "final_g"]
    small_g = [grad_mod_b, grad_norm_g, grad_a_conv_w, t_cb, grad_b_ln_g, grad_b_ln_b, t_ws, t_bs, t_fg]
    small_w = [mod_b, norm_g, a_conv_w, a_conv_b, b_ln_g, b_ln_b, b_w_s, b_b_s, final_g]
    small_m = [m_mod_b, m_norm_g, m_a_conv_w, m_a_conv_b, m_b_ln_g, m_b_ln_b, m_b_w_s, m_b_b_s, m_final_g]
    small_v = [v_mod_b, v_norm_g, v_a_conv_w, v_a_conv_b, v_b_ln_g, v_b_ln_b, v_b_w_s, v_b_b_s, v_final_g]
    as2d = lambda t_: t_.reshape(-1, t_.shape[-1])
    dls, nms, nvs = _adamw_small([as2d(t_) for t_ in small_g], [as2d(t_) for t_ in small_w],
                                 [as2d(t_) for t_ in small_m], [as2d(t_) for t_ in small_v], "adamw_small")
    for a, nme in enumerate(small_names):
        shp = small_w[a].shape
        res[nme] = (small_g[a].reshape(shp), dls[a].reshape(shp), nms[a].reshape(shp), nvs[a].reshape(shp))

    order = ["mod_w", "mod_b", "norm_g", "a_w_in", "a_conv_w", "a_conv_b", "a_w_out", "b_w_in", "b_ln_g", "b_ln_b",
             "b_w_s", "b_b_s", "b_w_out", "final_g"]
    return (loss, dx0.reshape(x.shape), *[res[k][0] for k in order], *[res[k][1] for k in order],
            *[res[k][2] for k in order], *[res[k][3] for k in order])
```

```python
import functools

import jax
import jax.numpy as jnp
from jax import lax
from jax.experimental import pallas as pl
from jax.experimental.pallas import tpu as pltpu

NDEV = 8
CHUNK = 128
GROUPS = 8
RMS_EPS = 1e-6
LN_EPS = 1e-5
ADAM_LR, ADAM_B1, ADAM_B2, ADAM_EPS, ADAM_WD, ADAM_STEP = 0.001, 0.9, 0.999, 1e-08, 0.01, 10
V7X_VMEM_BYTES = 64 * 1024 * 1024
VMEM_LIMIT = V7X_VMEM_BYTES - 8 * 1024 * 1024
TN_VMEM_BUDGET = 46 * 1024 * 1024
PACK_W = 1024
F32, BF16 = jnp.float32, jnp.bfloat16
MESH = pl.DeviceIdType.MESH
RSQRT2 = 0.7071067811865476
INV_SQRT_2PI = 0.3989422804014327
NT_DIMS = (((1,), (1,)), ((), ()))
TN_DIMS = (((0,), (0,)), ((), ()))


def _params(sem=None):
    return pltpu.CompilerParams(dimension_semantics=sem, vmem_limit_bytes=VMEM_LIMIT)


def _vmem():
    return pl.BlockSpec(memory_space=pltpu.VMEM)


def _hbm():
    return pl.BlockSpec(memory_space=pltpu.HBM)


def _full(shape):
    return pl.BlockSpec(shape, lambda *_: (0,) * len(shape))


def _pos():
    return lax.axis_index("x"), lax.axis_index("y"), lax.axis_index("c")


def _index(p):
    return 4 * p[0] + 2 * p[1] + p[2]


def _peer(k):
    x, y, c = _pos()
    return ((1 - x) if (k >> 2) & 1 else x, (1 - y) if (k >> 1) & 1 else y, (1 - c) if k & 1 else c)


def _silu(z):
    sg = jax.nn.sigmoid(z)
    return z * sg, sg * (1.0 + z * (1.0 - sg))


def _gelu(v):
    phi = 0.5 * (1.0 + lax.erf(v * RSQRT2))
    return v * phi, phi + v * (jnp.exp(-0.5 * v * v) * INV_SQRT_2PI)


def _colsum(v):
    return jnp.sum(v, axis=0, keepdims=True)


def _rowsum(v):
    return jnp.sum(v, axis=-1, keepdims=True)


def _gather_all_vmem(slab_ref, send_sems, recv_sems, base):
    me = _index(_pos())
    sends = []
    for k in range(1, NDEV):
        cp = pltpu.make_async_remote_copy(
            src_ref=slab_ref.at[me], dst_ref=slab_ref.at[me],
            send_sem=send_sems.at[base + k - 1], recv_sem=recv_sems.at[base + k - 1],
            device_id=_peer(k), device_id_type=MESH)
        cp.start()
        sends.append(cp)
    for k in range(1, NDEV):
        src = _index(_peer(k))
        pltpu.make_async_remote_copy(
            src_ref=slab_ref.at[src], dst_ref=slab_ref.at[src],
            send_sem=send_sems.at[base + k - 1], recv_sem=recv_sems.at[base + k - 1],
            device_id=_peer(k), device_id_type=MESH).wait_recv()
    for cp in sends:
        cp.wait_send()


def _mod_vectors(c, mod_w, mod_b, ex):
    n_layers, d, w3 = mod_w.shape
    r_in, r_out = len(ex.arrays), len(ex.out_shapes)

    def body(*refs):
        c_ref, mw_ref, mb_ref = refs[:3]
        ex_ins = refs[3:3 + r_in]
        mod_ref, call_ref = refs[3 + r_in:5 + r_in]
        ex_outs = refs[5 + r_in:5 + r_in + r_out]
        cslab, pslab, send_sems, recv_sems = refs[5 + r_in + r_out:9 + r_in + r_out]
        ex_sems = refs[9 + r_in + r_out:]
        ex.start(ex_ins, ex_outs, ex_sems)
        me = _index(_pos())
        cv = c_ref[...]
        cslab[me] = jnp.broadcast_to(cv * jax.nn.sigmoid(cv), (8, d))
        _gather_all_vmem(cslab, send_sems, recv_sems, 0)
        c_all = jnp.concatenate([cslab[k, 0:1, :] for k in range(NDEV)], axis=0)
        call_ref[...] = c_all
        for i in range(n_layers):
            pslab[me, i * NDEV:(i + 1) * NDEV, :] = jnp.dot(
                c_all, mw_ref[i], preferred_element_type=F32, precision=lax.Precision.HIGHEST)
        _gather_all_vmem(pslab, send_sems, recv_sems, NDEV - 1)
        for i in range(n_layers):
            for k in range(NDEV):
                mod_ref[i:i + 1, k * w3:(k + 1) * w3] = (
                    pslab[k, pl.ds(i * NDEV + me, 1), :] + mb_ref[i:i + 1, k * w3:(k + 1) * w3])
        for passing_on in ex.middles:
            passing_on(ex_ins, ex_outs, ex_sems)
        ex.finish(ex_ins, ex_outs, ex_sems)

    out = pl.pallas_call(
        body, name="mod_vectors",
        out_shape=(jax.ShapeDtypeStruct((n_layers, 3 * d), F32), jax.ShapeDtypeStruct((NDEV, d), F32), *ex.out_shapes),
        in_specs=[_vmem(), _vmem(), _vmem()] + [_hbm()] * r_in, out_specs=(_vmem(), _vmem(), *([_hbm()] * r_out)),
        scratch_shapes=[pltpu.VMEM((NDEV, 8, d), F32), pltpu.VMEM((NDEV, n_layers * NDEV, w3), F32),
                        pltpu.SemaphoreType.DMA((2 * (NDEV - 1),)), pltpu.SemaphoreType.DMA((2 * (NDEV - 1),)), *ex.sems],
        compiler_params=pltpu.CompilerParams(vmem_limit_bytes=VMEM_LIMIT),
    )(c, mod_w, mod_b, *ex.arrays)
    return out[0], out[1], out[2:]


class _Exchange:
    def __init__(self, arrays, out_shapes, sems, start, middles, finish):
        self.arrays, self.out_shapes, self.sems = list(arrays), list(out_shapes), list(sems)
        self.start, self.middles, self.finish = start, list(middles), finish


def _gather_exchange(shards, by_cols):
    n = len(shards)
    shapes = [sh.shape for sh in shards]

    def tools(ins, outs, sems):
        send_sems, recv_sems, local_sems = sems
        x, y, c = _pos()
        chips = [(1 - x, y), (x, 1 - y), (1 - x, 1 - y)]
        south = c == 0
        relayed = (jnp.where(south, 1 - x, x), jnp.where(south, y, 1 - y), c)
        relay_to = (jnp.where(south, x, 1 - x), jnp.where(south, 1 - y, y), c)

        def place(a, block):
            r, cc = shapes[a]
            if by_cols[a]:
                return outs[a].at[:, pl.ds(_index(block) * cc, cc)]
            return outs[a].at[pl.ds(_index(block) * r, r), :]

        def copy(a, k, block, to, src=None):
            dst = place(a, block)
            return pltpu.make_async_remote_copy(
                src_ref=dst if src is None else src, dst_ref=dst,
                send_sem=send_sems.at[a * 7 + k], recv_sem=recv_sems.at[a * 7 + k],
                device_id=to, device_id_type=MESH)

        mine = [pltpu.make_async_copy(ins[a], place(a, (x, y, c)), local_sems.at[a]) for a in range(n)]
        first = []
        for a in range(n):
            first.append(copy(a, 0, (x, y, c), (x, y, 1 - c), src=ins[a]))
            first += [copy(a, 1 + j, (x, y, c), (*chip, c), src=ins[a]) for j, chip in enumerate(chips[:2])]
        relays = [copy(a, 3, relayed, relay_to) for a in range(n)]
        passed = [copy(a, 4 + j, (*chip, c), (x, y, 1 - c)) for j, chip in enumerate(chips) for a in range(n)]
        return (x, y, c), chips, copy, mine, first, relays, passed

    def start(ins, outs, sems):
        _, _, _, mine, first, _, _ = tools(ins, outs, sems)
        for cp in mine + first:
            cp.start()

    def pass_neighbours(ins, outs, sems):
        (x, y, c), chips, copy, _, _, relays, passed = tools(ins, outs, sems)
        for j, chip in enumerate(chips[:2]):
            for a in range(n):
                copy(a, 1 + j, (*chip, c), (x, y, c)).wait_recv()
                passed[j * n + a].start()
        for cp in relays:
            cp.start()

    def pass_diagonal(ins, outs, sems):
        (x, y, c), chips, copy, _, _, _, passed = tools(ins, outs, sems)
        for a in range(n):
            copy(a, 3, (*chips[2], c), (x, y, c)).wait_recv()
            passed[2 * n + a].start()

    def finish(ins, outs, sems):
        (x, y, c), chips, copy, mine, first, relays, passed = tools(ins, outs, sems)
        for a in range(n):
            copy(a, 0, (x, y, 1 - c), (x, y, c)).wait_recv()
        for j, chip in enumerate(chips):
            for a in range(n):
                copy(a, 4 + j, (*chip, 1 - c), (x, y, c)).wait_recv()
        for cp in first + relays + passed:
            cp.wait_send()
        for cp in mine:
            cp.wait()

    out_shapes = [jax.ShapeDtypeStruct((r, NDEV * cc) if bc else (NDEV * r, cc), sh.dtype)
                  for (r, cc), bc, sh in zip(shapes, by_cols, shards)]
    sems = [pltpu.SemaphoreType.DMA((7 * n,)), pltpu.SemaphoreType.DMA((7 * n,)), pltpu.SemaphoreType.DMA((n,))]
    return _Exchange(shards, out_shapes, sems, start, [pass_neighbours, pass_diagonal], finish)


def _scatter_exchange(parts):
    n = len(parts)

    def tools(ins, outs, sems):
        send_sems, recv_sems, local_sems = sems
        me = _index(_pos())
        mine = [pltpu.make_async_copy(ins[a].at[me], outs[a].at[me], local_sems.at[a]) for a in range(n)]
        sends, arrivals = [], []
        for k in range(1, NDEV):
            peer = _peer(k)
            for a in range(n):
                pair = dict(send_sem=send_sems.at[a * 7 + k - 1], recv_sem=recv_sems.at[a * 7 + k - 1],
                            device_id=peer, device_id_type=MESH)
                sends.append(pltpu.make_async_remote_copy(src_ref=ins[a].at[_index(peer)], dst_ref=outs[a].at[me], **pair))
                slot = outs[a].at[_index(peer)]
                arrivals.append(pltpu.make_async_remote_copy(src_ref=slot, dst_ref=slot, **pair))
        return mine, sends, arrivals

    def start(ins, outs, sems):
        mine, sends, _ = tools(ins, outs, sems)
        for cp in mine + sends:
            cp.start()

    def finish(ins, outs, sems):
        mine, sends, arrivals = tools(ins, outs, sems)
        for cp in arrivals:
            cp.wait_recv()
        for cp in sends:
            cp.wait_send()
        for cp in mine:
            cp.wait()

    out_shapes = [jax.ShapeDtypeStruct(p.shape, p.dtype) for p in parts]
    sems = [pltpu.SemaphoreType.DMA((7 * n,)), pltpu.SemaphoreType.DMA((7 * n,)), pltpu.SemaphoreType.DMA((n,))]
    return _Exchange(parts, out_shapes, sems, start, [], finish)


def _scatter_start(part, name):
    def body(part_ref, land_ref, send_sems, recv_sems, part_thru, land_thru, token):
        me = _index(_pos())
        for k in range(1, NDEV):
            peer = _peer(k)
            pltpu.make_async_remote_copy(
                src_ref=part_ref.at[_index(peer)], dst_ref=land_ref.at[me],
                send_sem=send_sems.at[k - 1], recv_sem=recv_sems.at[k - 1],
                device_id=peer, device_id_type=MESH).start()
        token[...] = jnp.zeros_like(token)

    sem = pl.BlockSpec(memory_space=pltpu.SEMAPHORE)
    return pl.pallas_call(
        body, name=name,
        out_shape=(pltpu.SemaphoreType.DMA((NDEV - 1,)), pltpu.SemaphoreType.DMA((NDEV - 1,)),
                   pltpu.HBM(part.shape, part.dtype), pltpu.HBM(part.shape, part.dtype),
                   jax.ShapeDtypeStruct((8, 128), F32)),
        in_specs=(_hbm(), _hbm()), out_specs=(sem, sem, _hbm(), _hbm(), _vmem()),
        input_output_aliases={0: 2, 1: 3},
        compiler_params=pltpu.CompilerParams(has_side_effects=pltpu.SideEffectType.DATAFLOW_SIDE_EFFECTING),
    )(pltpu.with_memory_space_constraint(part, pltpu.HBM),
      pltpu.with_memory_space_constraint(lax.empty(part.shape, part.dtype), pltpu.HBM))


def _scatter_wait(send_sems, recv_sems, part_thru, land_thru, after, name):
    def body(part_ref, land_ref, send_sems, recv_sems, after_ref, part_dead, got_ref):
        for k in range(1, NDEV):
            peer = _peer(k)
            slot = land_ref.at[_index(peer)]
            copy = pltpu.make_async_remote_copy(
                src_ref=part_ref.at[_index(peer)], dst_ref=slot,
                send_sem=send_sems.at[k - 1], recv_sem=recv_sems.at[k - 1],
                device_id=peer, device_id_type=MESH)
            copy.wait_send()
            copy.wait_recv()

    sem = pl.BlockSpec(memory_space=pltpu.SEMAPHORE)
    return pl.pallas_call(
        body, name=name,
        out_shape=(pltpu.HBM(part_thru.shape, part_thru.dtype), pltpu.HBM(land_thru.shape, land_thru.dtype)),
        in_specs=(_hbm(), _hbm(), sem, sem, pl.BlockSpec(memory_space=pl.ANY)), out_specs=(_hbm(), _hbm()),
        input_output_aliases={0: 0, 1: 1},
        compiler_params=pltpu.CompilerParams(has_side_effects=pltpu.SideEffectType.DATAFLOW_SIDE_EFFECTING),
    )(part_thru, land_thru, send_sems, recv_sems, after)


def _carry(ex, body, n_in, n_out, first, middle, last):
    if ex is None:
        return body
    r_in, r_out = len(ex.arrays), len(ex.out_shapes)

    def wrapped(*refs):
        ins, rins = refs[:n_in], refs[n_in:n_in + r_in]
        outs = refs[n_in + r_in:n_in + r_in + n_out]
        routs = refs[n_in + r_in + n_out:n_in + r_in + n_out + r_out]
        rest = refs[n_in + r_in + n_out + r_out:]
        scratch, sems = rest[:len(rest) - len(ex.sems)], rest[len(rest) - len(ex.sems):]

        @pl.when(first())
        def _():
            ex.start(rins, routs, sems)

        for passing_on, at_step in zip(ex.middles, middle or []):
            pl.when(at_step())(functools.partial(passing_on, rins, routs, sems))

        body(*ins, *outs, *scratch)

        @pl.when(last())
        def _():
            ex.finish(rins, routs, sems)

    return wrapped


def _carried(ex):
    if ex is None:
        return [], [], [], [], []
    return ex.arrays, [_hbm()] * len(ex.arrays), ex.out_shapes, [_hbm()] * len(ex.out_shapes), ex.sems


def _resident(shape):
    return pl.BlockSpec(shape, lambda *_: (0,) * len(shape), pipeline_mode=pl.Buffered(1))


def _norm_modulate(x_ref, g_ref, sc_ref, sh_ref):
    xv = x_ref[...]
    r = lax.rsqrt(jnp.mean(xv * xv, axis=-1, keepdims=True) + RMS_EPS)
    return ((xv * r) * g_ref[...] * (1.0 + sc_ref[...]) + sh_ref[...]).astype(BF16)


def _conv_taps(cx, t6, t7, row):
    p1 = jnp.where(row == 0, t7, pltpu.roll(cx, 1, 0))
    p2 = jnp.where(row == 0, t6, jnp.where(row == 1, t7, pltpu.roll(cx, 2, 0)))
    return p1, p2


def _layer_a_fwd(x, g, scale, shift, gate, wi, cw, cb, wo, name, ex=None):
    s, d = x.shape
    e = wo.shape[0]
    t = min(s, 256)
    n_t = s // t
    cwid = min(e, 512)

    def body(x_ref, g_ref, sc_ref, sh_ref, gate_ref, wi_ref, cw_ref, cb_ref, wo_ref,
             proj_ref, h_ref, x1_ref, br_ref, conv_ref, y_scr, tail_scr):
        @pl.when(pl.program_id(0) == 0)
        def _():
            tail_scr[...] = jnp.zeros_like(tail_scr)
        h_ref[...] = _norm_modulate(x_ref, g_ref, sc_ref, sh_ref)
        row = lax.broadcasted_iota(jnp.int32, (t, cwid), 0)

        def project(c0):
            v = jnp.dot(h_ref[...], wi_ref[:, c0:c0 + cwid], preferred_element_type=F32)
            proj_ref[:, c0:c0 + cwid] = v.astype(BF16)
            return v

        for c0 in range(0, e, cwid):
            sl = slice(c0, c0 + cwid)
            bg, z = project(c0), project(3 * e + c0)
            cx = project(e + c0) * project(2 * e + c0)
            p1, p2 = _conv_taps(cx, tail_scr[6:7, sl], tail_scr[7:8, sl], row)
            conv = cb_ref[:, sl] + cw_ref[2:3, sl] * cx + cw_ref[0:1, sl] * p2 + cw_ref[1:2, sl] * p1
            conv_ref[:, sl] = conv.astype(BF16)
            y_scr[:, sl] = (_silu(z)[0] * bg * conv).astype(BF16)
            tail_scr[:, sl] = cx[t - 8:t, :]
        br = jnp.dot(y_scr[...], wo_ref[...], preferred_element_type=F32)
        x1_ref[...] = x_ref[...] + gate_ref[...] * br
        br_ref[...] = br.astype(BF16)

    step = lambda k: (lambda: pl.program_id(0) == k)
    body = _carry(ex, body, 9, 5, step(0), [step(n_t // 3), step((2 * n_t) // 3)], step(n_t - 1))
    ex_args, ex_in, ex_shapes, ex_out, ex_sems = _carried(ex)
    tok = pl.BlockSpec((t, d), lambda i: (i, 0))
    out = pl.pallas_call(
        body, name=name, grid=(n_t,),
        out_shape=(jax.ShapeDtypeStruct((s, 4 * e), BF16), jax.ShapeDtypeStruct((s, d), BF16),
                   jax.ShapeDtypeStruct((s, d), F32), jax.ShapeDtypeStruct((s, d), BF16),
                   jax.ShapeDtypeStruct((s, e), BF16), *ex_shapes),
        in_specs=[tok, _full((1, d)), _full((1, d)), _full((1, d)), _full((1, d)), _resident((d, 4 * e)),
                  _full((3, e)), _full((1, e)), _resident((e, d)), *ex_in],
        out_specs=(pl.BlockSpec((t, 4 * e), lambda i: (i, 0)), tok, tok, tok,
                   pl.BlockSpec((t, e), lambda i: (i, 0)), *ex_out),
        scratch_shapes=[pltpu.VMEM((t, e), BF16), pltpu.VMEM((8, e), F32), *ex_sems],
        compiler_params=_params(("arbitrary",)),
    )(x, g, scale, shift, gate, wi, cw, cb, wo, *ex_args)
    return (*out[:5], out[5:])


def _ln_stats(v_of, v_scr, t, e):
    gw = e // GROUPS
    s1 = jnp.zeros((t, 1), F32)
    for g in range(GROUPS):
        v = v_of(g)
        v_scr[:, g * gw:(g + 1) * gw] = v
        s1 = s1 + _rowsum(v)
    mu = s1 * (1.0 / e)
    s2 = jnp.zeros((t, 1), F32)
    for g in range(GROUPS):
        dv = v_scr[:, g * gw:(g + 1) * gw] - mu
        s2 = s2 + _rowsum(dv * dv)
    return mu, lax.rsqrt(s2 * (1.0 / e) + LN_EPS)


def _layer_b_fwd_loss(x1, tgt, g1, scale, shift, gate, fg, wi, lng, lnb, wt, bsf, wo, name, ex=None):
    s, d = x1.shape
    e = wo.shape[0]
    gw = e // GROUPS
    t = min(s, 256)
    n_t = s // t

    def body(x1_ref, tgt_ref, g_ref, sc_ref, sh_ref, gate_ref, fg_ref, wi_ref, lng_ref, lnb_ref, wt_ref, bsf_ref, wo_ref,
             proj_ref, h_ref, dx2_ref, loss_ref, dfg_ref, dgate_ref, v_ref, dgv_ref, v_scr, y_scr):
        @pl.when(pl.program_id(0) == 0)
        def _():
            loss_ref[...] = jnp.zeros_like(loss_ref)
            dfg_ref[...] = jnp.zeros_like(dfg_ref)
            dgate_ref[...] = jnp.zeros_like(dgate_ref)
        h_ref[...] = _norm_modulate(x1_ref, g_ref, sc_ref, sh_ref)

        def project(c0):
            v = jnp.dot(h_ref[...], wi_ref[:, c0:c0 + gw], preferred_element_type=F32)
            proj_ref[:, c0:c0 + gw] = v.astype(BF16)
            return v

        def gelu_v(g):
            gs = slice(g * gw, (g + 1) * gw)
            v, dgv = _gelu(project(e + g * gw))
            v_ref[:, gs] = v.astype(BF16)
            dgv_ref[:, gs] = dgv.astype(BF16)
            return v

        mu, rs = _ln_stats(gelu_v, v_scr, t, e)
        for g in range(GROUPS):
            gs = slice(g * gw, (g + 1) * gw)
            vn = (((v_scr[:, gs] - mu) * rs) * lng_ref[:, gs] + lnb_ref[:, gs]).astype(BF16)
            u = _gelu(project(g * gw))[0]
            sz = _silu(project(2 * e + g * gw))[0]
            for ch in range(t // CHUNK):
                rows = slice(ch * CHUNK, (ch + 1) * CHUNK)
                mixed = jnp.dot(wt_ref[g], vn[rows], preferred_element_type=F32) + bsf_ref[:, gs]
                y_scr[rows, gs] = (sz[rows] * (u[rows] * mixed)).astype(BF16)
        br = jnp.dot(y_scr[...], wo_ref[...], preferred_element_type=F32)
        x2 = x1_ref[...] + gate_ref[...] * br
        r2 = lax.rsqrt(jnp.mean(x2 * x2, axis=-1, keepdims=True) + RMS_EPS)
        xn = x2 * r2
        diff = xn * fg_ref[...] - tgt_ref[...]
        loss_ref[...] += jnp.broadcast_to(0.5 * _colsum(jnp.mean(diff * diff, axis=-1, keepdims=True)), loss_ref.shape)
        dout = diff * (1.0 / d)
        dfg_ref[...] += _colsum(dout * xn)
        dxn = dout * fg_ref[...]
        dx2 = r2 * (dxn - xn * jnp.mean(dxn * xn, axis=-1, keepdims=True))
        dx2_ref[...] = dx2
        dgate_ref[...] += _colsum(dx2 * br)

    step = lambda k: (lambda: pl.program_id(0) == k)
    body = _carry(ex, body, 13, 8, step(0), [step(n_t // 3), step((2 * n_t) // 3)], step(n_t - 1))
    ex_args, ex_in, ex_shapes, ex_out, ex_sems = _carried(ex)
    tok = pl.BlockSpec((t, d), lambda i: (i, 0))
    vec = _full((1, d))
    out = pl.pallas_call(
        body, name=name, grid=(n_t,),
        out_shape=(jax.ShapeDtypeStruct((s, 3 * e), BF16), jax.ShapeDtypeStruct((s, d), BF16),
                   jax.ShapeDtypeStruct((s, d), F32), jax.ShapeDtypeStruct((8, 128), F32),
                   jax.ShapeDtypeStruct((1, d), F32), jax.ShapeDtypeStruct((1, d), F32),
                   jax.ShapeDtypeStruct((s, e), BF16), jax.ShapeDtypeStruct((s, e), BF16), *ex_shapes),
        in_specs=[tok, tok, vec, vec, vec, vec, vec, _resident((d, 3 * e)), _full((1, e)), _full((1, e)),
                  _full((GROUPS, CHUNK, CHUNK)), _resident((CHUNK, e)), _resident((e, d)), *ex_in],
        out_specs=(pl.BlockSpec((t, 3 * e), lambda i: (i, 0)), tok, tok, _full((8, 128)), vec, vec,
                   pl.BlockSpec((t, e), lambda i: (i, 0)), pl.BlockSpec((t, e), lambda i: (i, 0)), *ex_out),
        scratch_shapes=[pltpu.VMEM((t, e), F32), pltpu.VMEM((t, e), BF16), *ex_sems],
        compiler_params=_params(("arbitrary",)),
    )(x1, tgt, g1, scale, shift, gate, fg, wi, lng, lnb, wt, bsf, wo, *ex_args)
    return (*out[:8], out[8:])


def _norm_modulate_bwd(dh, x_ref, dres_ref, g_ref, sc_ref, dx_ref, dsh_ref, p_scr):
    xv = x_ref[...]
    r = lax.rsqrt(jnp.mean(xv * xv, axis=-1, keepdims=True) + RMS_EPS)
    xn = xv * r
    dsh_ref[...] += _colsum(dh)
    p_scr[...] += _colsum(dh * xn)
    dxn = dh * (g_ref[...] * (1.0 + sc_ref[...]))
    dx_ref[...] = r * (dxn - xn * jnp.mean(dxn * xn, axis=-1, keepdims=True)) + dres_ref[...]


def _layer_b_bwd(proj, v_act, dgv, dx2, x1, gate, g1, scale, lng, lnb, wt, wtt, bsf, wo, wi, name):
    s, e3 = proj.shape
    e = e3 // 3
    d = dx2.shape[1]
    gw = e // GROUPS
    t = min(s, 256)
    n_t = s // t

    def body(pu_ref, pz_ref, v_ref, dgv_ref, dx_ref, x1_ref, gate_ref, g_ref, sc_ref, lng_ref, lnb_ref, wt_ref, wtt_ref,
             bsf_ref, wo_ref, wi_ref,
             dp_ref, y_ref, dx1_ref, dws_ref, dbs_ref, dlg_ref, dlb_ref, dsh_ref, dsc_ref, dg_ref,
             v_scr, dbr_scr, dvn_scr, dbs_scr, p_scr, dy_scr, vn_scr, mixed_scr):
        @pl.when(pl.program_id(0) == 0)
        def _():
            dws_ref[...] = jnp.zeros_like(dws_ref)
            dlg_ref[...] = jnp.zeros_like(dlg_ref)
            dlb_ref[...] = jnp.zeros_like(dlb_ref)
            dsh_ref[...] = jnp.zeros_like(dsh_ref)
            dbs_scr[...] = jnp.zeros_like(dbs_scr)
            p_scr[...] = jnp.zeros_like(p_scr)
        dbr_scr[...] = (dx_ref[...] * gate_ref[...]).astype(BF16)
        mu, rs = _ln_stats(lambda g: v_ref[:, g * gw:(g + 1) * gw].astype(F32), v_scr, t, e)
        tril = (lax.broadcasted_iota(jnp.int32, (CHUNK, CHUNK), 0) >= lax.broadcasted_iota(jnp.int32, (CHUNK, CHUNK), 1))
        c1 = jnp.zeros((t, 1), F32)
        c2 = jnp.zeros((t, 1), F32)
        span = 2
        kw = span * gw
        dh = jnp.zeros((t, d), F32)

        def through_w_in(c0):
            return lax.dot_general(dp_ref[:, c0:c0 + kw], wi_ref[:, c0:c0 + kw], NT_DIMS, preferred_element_type=F32)

        dy_scr[...] = lax.dot_general(dbr_scr[...], wo_ref[...], NT_DIMS, preferred_element_type=F32)
        for g in range(GROUPS):
            gs = slice(g * gw, (g + 1) * gw)
            vhat = (v_scr[:, gs] - mu) * rs
            v_scr[:, gs] = vhat
            vn = (vhat * lng_ref[:, gs] + lnb_ref[:, gs]).astype(BF16)
            vn_scr[:, gs] = vn
            for ch in range(t // CHUNK):
                rows = slice(ch * CHUNK, (ch + 1) * CHUNK)
                mixed_scr[rows, gs] = jnp.dot(wt_ref[g], vn[rows], preferred_element_type=F32) + bsf_ref[:, gs]
        for g in range(GROUPS):
            gs = slice(g * gw, (g + 1) * gw)
            vhat = v_scr[:, gs]
            lg = lng_ref[:, gs]
            for ch in range(t // CHUNK):
                rows = slice(ch * CHUNK, (ch + 1) * CHUNK)
                mixed = mixed_scr[rows, gs]
                u, dgu = _gelu(pu_ref[rows, gs].astype(F32))
                sz, dsz = _silu(pz_ref[rows, gs].astype(F32))
                sgate = u * mixed
                y_ref[rows, gs] = (sz * sgate).astype(BF16)
                dy = dy_scr[rows, gs]
                dp_ref[rows, 2 * e + g * gw:2 * e + (g + 1) * gw] = (dy * sgate * dsz).astype(BF16)
                ds = dy * sz
                dp_ref[rows, gs] = (ds * mixed * dgu).astype(BF16)
                dm = ds * u
                dbs_scr[:, gs] += dm
                dmb = dm.astype(BF16)
                dws_ref[g] += jnp.where(tril, lax.dot_general(dmb, vn_scr[rows, gs], NT_DIMS, preferred_element_type=F32), 0.0)
                dvn_scr[rows, gs] = jnp.dot(wtt_ref[g], dmb, preferred_element_type=F32)
            dvn = dvn_scr[:, gs]
            dlb_ref[:, gs] += _colsum(dvn)
            dlg_ref[:, gs] += _colsum(dvn * vhat)
            dvh = dvn * lg
            c1 = c1 + _rowsum(dvh)
            c2 = c2 + _rowsum(dvh * vhat)
            if g % span == span - 1:
                dh = dh + through_w_in(g * gw + gw - kw) + through_w_in(2 * e + g * gw + gw - kw)
        c1 = c1 * (1.0 / e)
        c2 = c2 * (1.0 / e)
        for g in range(GROUPS):
            gs = slice(g * gw, (g + 1) * gw)
            dv = rs * (dvn_scr[:, gs] * lng_ref[:, gs] - c1 - v_scr[:, gs] * c2)
            dp_ref[:, e + g * gw:e + (g + 1) * gw] = (dv * dgv_ref[:, gs]).astype(BF16)
            if g % span == span - 1:
                dh = dh + through_w_in(e + g * gw + gw - kw)
        _norm_modulate_bwd(dh, x1_ref, dx_ref, g_ref, sc_ref, dx1_ref, dsh_ref, p_scr)

        @pl.when(pl.program_id(0) == n_t - 1)
        def _():
            lane = lax.broadcasted_iota(jnp.int32, (CHUNK, 128), 1)
            acc = jnp.zeros((CHUNK, 128), F32)
            for g in range(GROUPS):
                acc = acc + jnp.where(lane == g, _rowsum(dbs_scr[:, g * gw:(g + 1) * gw]), 0.0)
            dbs_ref[...] = acc
            dsc_ref[...] = p_scr[...] * g_ref[...]
            dg_ref[...] = p_scr[...] * (1.0 + sc_ref[...])

    tok = pl.BlockSpec((t, d), lambda i: (i, 0))
    vec, evec, ws = _full((1, d)), _full((1, e)), _full((GROUPS, CHUNK, CHUNK))
    vshape = jax.ShapeDtypeStruct((1, d), F32)
    return pl.pallas_call(
        body, name=name, grid=(n_t,),
        out_shape=(jax.ShapeDtypeStruct((s, e3), BF16), jax.ShapeDtypeStruct((s, e), BF16), jax.ShapeDtypeStruct((s, d), F32),
                   jax.ShapeDtypeStruct((GROUPS, CHUNK, CHUNK), F32), jax.ShapeDtypeStruct((CHUNK, 128), F32),
                   jax.ShapeDtypeStruct((1, e), F32), jax.ShapeDtypeStruct((1, e), F32), vshape, vshape, vshape),
        in_specs=[pl.BlockSpec((t, e), lambda i: (i, 0)), pl.BlockSpec((t, e), lambda i: (i, 2)),
                  pl.BlockSpec((t, e), lambda i: (i, 0)), pl.BlockSpec((t, e), lambda i: (i, 0)),
                  tok, tok, vec, vec, vec, evec, evec, ws, ws,
                  _resident((CHUNK, e)), _resident((e, d)), _resident((d, e3))],
        out_specs=(pl.BlockSpec((t, e3), lambda i: (i, 0)), pl.BlockSpec((t, e), lambda i: (i, 0)), tok,
                   ws, _full((CHUNK, 128)), evec, evec, vec, vec, vec),
        scratch_shapes=[pltpu.VMEM((t, e), F32), pltpu.VMEM((t, d), BF16),
                        pltpu.VMEM((t, e), F32), pltpu.VMEM((CHUNK, e), F32), pltpu.VMEM((1, d), F32),
                        pltpu.VMEM((t, e), F32), pltpu.VMEM((t, e), BF16), pltpu.VMEM((t, e), F32)],
        compiler_params=_params(("arbitrary",)),
    )(proj, proj, v_act, dgv, dx2, x1, gate, g1, scale, lng, lnb, wt, wtt, bsf, wo, wi)


def _conv_mixer_bwd(proj, dx1, br, conv_a, cw, gate, wo, name, ex=None):
    s, e4 = proj.shape
    e = e4 // 4
    d = dx1.shape[1]
    t = min(s, 256)
    n_t = s // t
    cwid = min(e, 512)

    def body(p_ref, dx_ref, br_ref, conv_ref, cw_ref, gate_ref, wo_ref,
             dp_ref, y_ref, dgate_ref, dcb_ref, dcw_ref, dy_scr, head_scr):
        i = pl.program_id(0)

        @pl.when(i == 0)
        def _():
            dgate_ref[...] = jnp.zeros_like(dgate_ref)
            dcb_ref[...] = jnp.zeros_like(dcb_ref)
            dcw_ref[...] = jnp.zeros_like(dcw_ref)
            head_scr[...] = jnp.zeros_like(head_scr)
        dx = dx_ref[...]
        dgate_ref[...] += _colsum(dx * br_ref[...].astype(F32))
        dy_scr[...] = lax.dot_general((dx * gate_ref[...]).astype(BF16), wo_ref[...], NT_DIMS,
                                      preferred_element_type=F32)
        row = lax.broadcasted_iota(jnp.int32, (t, cwid), 0)
        for c0 in range(0, e, cwid):
            sl = slice(c0, c0 + cwid)
            bg = p_ref[:, c0:c0 + cwid].astype(F32)
            cg = p_ref[:, e + c0:e + c0 + cwid].astype(F32)
            xin = p_ref[:, 2 * e + c0:2 * e + c0 + cwid].astype(F32)
            z = p_ref[:, 3 * e + c0:3 * e + c0 + cwid].astype(F32)
            cx = cg * xin
            w0, w1, w2 = cw_ref[0:1, sl], cw_ref[1:2, sl], cw_ref[2:3, sl]
            conv = conv_ref[:, sl].astype(F32)
            sz, dsz = _silu(z)
            dy = dy_scr[:, sl]
            y_ref[:, sl] = (sz * bg * conv).astype(BF16)
            dp_ref[:, 3 * e + c0:3 * e + c0 + cwid] = (dy * bg * conv * dsz).astype(BF16)
            dp_ref[:, c0:c0 + cwid] = (dy * sz * conv).astype(BF16)
            dconv = dy * sz * bg
            h0, h1 = head_scr[0:1, sl], head_scr[1:2, sl]
            n1 = jnp.where(row == t - 1, h0, pltpu.roll(dconv, t - 1, 0))
            n2 = jnp.where(row == t - 2, h0, jnp.where(row == t - 1, h1, pltpu.roll(dconv, t - 2, 0)))
            dcb_ref[:, sl] += _colsum(dconv)
            dcw_ref[2:3, sl] += _colsum(dconv * cx)
            dcw_ref[1:2, sl] += _colsum(n1 * cx)
            dcw_ref[0:1, sl] += _colsum(n2 * cx)
            dcx = w2 * dconv + w1 * n1 + w0 * n2
            dp_ref[:, e + c0:e + c0 + cwid] = (dcx * xin).astype(BF16)
            dp_ref[:, 2 * e + c0:2 * e + c0 + cwid] = (dcx * cg).astype(BF16)
            head_scr[:, sl] = dconv[0:8, :]

    body = _carry(ex, body, 7, 5, lambda: pl.program_id(0) == 0, None, lambda: pl.program_id(0) == n_t - 1)
    ex_args, ex_in, ex_shapes, ex_out, ex_sems = _carried(ex)
    rev = lambda i: (n_t - 1 - i, 0)
    out = pl.pallas_call(
        body, name=name, grid=(n_t,),
        out_shape=(jax.ShapeDtypeStruct((s, e4), BF16), jax.ShapeDtypeStruct((s, e), BF16),
                   jax.ShapeDtypeStruct((1, d), F32), jax.ShapeDtypeStruct((1, e), F32), jax.ShapeDtypeStruct((8, e), F32),
                   *ex_shapes),
        in_specs=[pl.BlockSpec((t, e4), rev), pl.BlockSpec((t, d), rev), pl.BlockSpec((t, d), rev),
                  pl.BlockSpec((t, e), rev), _full((3, e)), _full((1, d)), _full((e, d)), *ex_in],
        out_specs=(pl.BlockSpec((t, e4), rev), pl.BlockSpec((t, e), rev), _full((1, d)), _full((1, e)), _full((8, e)),
                   *ex_out),
        scratch_shapes=[pltpu.VMEM((t, e), F32), pltpu.VMEM((8, e), F32), *ex_sems],
        compiler_params=_params(("arbitrary",)),
    )(proj, dx1, br, conv_a, cw, gate, wo, *ex_args)
    return (*out[:5], out[5:])


def _matmul_nt_norm_bwd(dproj, w, xin, dres, g, scale, name, ex=None):
    s, d = xin.shape
    n = w.shape[1]
    tm = min(s, 512)
    n_i = s // tm

    def body(dp_ref, w_ref, x_ref, dres_ref, g_ref, sc_ref, dx_ref, dsh_ref, dsc_ref, dg_ref, p_scr):
        i = pl.program_id(0)

        @pl.when(i == 0)
        def _():
            dsh_ref[...] = jnp.zeros_like(dsh_ref)
            p_scr[...] = jnp.zeros_like(p_scr)
        dh = lax.dot_general(dp_ref[...], w_ref[...], NT_DIMS, preferred_element_type=F32)
        _norm_modulate_bwd(dh, x_ref, dres_ref, g_ref, sc_ref, dx_ref, dsh_ref, p_scr)

        @pl.when(i == n_i - 1)
        def _():
            dsc_ref[...] = p_scr[...] * g_ref[...]
            dg_ref[...] = p_scr[...] * (1.0 + sc_ref[...])

    body = _carry(ex, body, 6, 4, lambda: pl.program_id(0) == 0, None, lambda: pl.program_id(0) == n_i - 1)
    ex_args, ex_in, ex_shapes, ex_out, ex_sems = _carried(ex)
    tok = pl.BlockSpec((tm, d), lambda i: (i, 0))
    vec = pl.BlockSpec((1, d), lambda i: (0, 0))
    vshape = jax.ShapeDtypeStruct((1, d), F32)
    out = pl.pallas_call(
        body, name=name, grid=(n_i,),
        out_shape=(jax.ShapeDtypeStruct((s, d), F32), vshape, vshape, vshape, *ex_shapes),
        in_specs=[pl.BlockSpec((tm, n), lambda i: (i, 0)), _resident((d, n)), tok, tok, vec, vec, *ex_in],
        out_specs=(tok, vec, vec, vec, *ex_out),
        scratch_shapes=[pltpu.VMEM((1, d), F32), *ex_sems],
        compiler_params=_params(("arbitrary",)),
    )(dproj, w, xin, dres, g, scale, *ex_args)
    return (*out[:4], out[4:])


def _matmul_tn(a, b, colscale, rows_split, name, ex=None, a_cols=None):
    s, m = a.shape
    a_blk = 0
    if a_cols is not None:
        a_blk, m = a_cols
    n = b.shape[1]
    n_j, tn = (1, n) if rows_split else (NDEV, n // NDEV)
    fixed = (4 + 4 + 2 * 2) * m * tn
    tk = s
    while fixed + 2 * tk * (2 * m + b.dtype.itemsize * tn) > TN_VMEM_BUDGET:
        tk //= 2
    n_k = s // tk

    def body(a_ref, b_ref, cs_ref, o_ref, acc):
        k = pl.program_id(1)
        part = lax.dot_general(a_ref[...], b_ref[...].astype(BF16), TN_DIMS, preferred_element_type=F32)
        if n_k == 1:
            o_ref[...] = (part * cs_ref[...]).astype(BF16)
            return

        @pl.when(k == 0)
        def _():
            acc[...] = part

        @pl.when((k > 0) & (k < n_k - 1))
        def _():
            acc[...] += part

        @pl.when(k == n_k - 1)
        def _():
            o_ref[...] = ((acc[...] + part) * cs_ref[...]).astype(BF16)

    at = lambda j, k: (pl.program_id(0) == j) & (pl.program_id(1) == k)
    body = _carry(ex, body, 3, 1, lambda: at(0, 0), None, lambda: at(n_j - 1, n_k - 1))
    ex_args, ex_in, ex_shapes, ex_out, ex_sems = _carried(ex)
    out = pl.pallas_call(
        body, name=name, grid=(n_j, n_k),
        out_shape=(jax.ShapeDtypeStruct((n_j, m, tn), BF16), *ex_shapes),
        in_specs=[pl.BlockSpec((tk, m), lambda j, k: (k, a_blk)), pl.BlockSpec((tk, tn), lambda j, k: (k, j)),
                  pl.BlockSpec((1, tn), lambda j, k: (0, j)), *ex_in],
        out_specs=(pl.BlockSpec((None, m, tn), lambda j, k: (j, 0, 0)), *ex_out),
        scratch_shapes=[pltpu.VMEM((m, tn), F32), *ex_sems],
        compiler_params=_params(("arbitrary", "arbitrary")),
    )(a, b, colscale, *ex_args)
    return (out[0].reshape(NDEV, m // NDEV, n) if rows_split else out[0]), out[1:]


def _adam_update(w, g, m, v):
    m = ADAM_B1 * m + (1.0 - ADAM_B1) * g
    v = ADAM_B2 * v + (1.0 - ADAM_B2) * (g * g)
    m_hat = m / (1.0 - ADAM_B1 ** ADAM_STEP)
    v_hat = v / (1.0 - ADAM_B2 ** ADAM_STEP)
    return -ADAM_LR * (m_hat / (jnp.sqrt(v_hat) + ADAM_EPS) + ADAM_WD * w), m, v


def _adamw_reduce(parts, w, m, v, name, ex=None):
    n_l = len(parts)
    n_p, _, c = parts[0].shape
    rows = [p.shape[1] for p in parts]
    r = sum(rows)
    tr = min(min(rows), 128 if ex is not None else 256)
    n_i = r // tr
    tiles = [r_l // tr for r_l in rows]
    first_tile = [sum(tiles[:l]) for l in range(n_l)]

    def body(*refs):
        p_refs, (w_ref, m_ref, v_ref, g_out, d_out, m_out, v_out) = refs[:n_l], refs[n_l:]
        g = None
        for l, p_ref in enumerate(p_refs):
            g_l = p_ref[0].astype(F32)
            for j in range(1, n_p):
                g_l = g_l + p_ref[j].astype(F32)
            g = g_l if g is None else jnp.where(pl.program_id(0) >= first_tile[l], g_l, g)
        g_out[...] = g
        d_out[...], m_out[...], v_out[...] = _adam_update(w_ref[...], g, m_ref[...], v_ref[...])

    step = lambda k: (lambda: pl.program_id(0) == k)
    body = _carry(ex, body, n_l + 3, 4, step(0), [step(n_i // 3), step((2 * n_i) // 3)], step(n_i - 1))
    ex_args, ex_in, ex_shapes, ex_out, ex_sems = _carried(ex)
    blk = pl.BlockSpec((tr, c), lambda i: (i, 0))
    p_specs = [pl.BlockSpec((n_p, tr, c), lambda i, l=l: (0, jnp.clip(i - first_tile[l], 0, tiles[l] - 1), 0))
               for l in range(n_l)]
    shp = jax.ShapeDtypeStruct((r, c), F32)
    out = pl.pallas_call(
        body, name=name, grid=(n_i,), out_shape=(shp, shp, shp, shp, *ex_shapes),
        in_specs=[*p_specs, blk, blk, blk, *ex_in],
        out_specs=(blk, blk, blk, blk, *ex_out), scratch_shapes=ex_sems,
        compiler_params=_params(("arbitrary",)),
    )(*parts, w, m, v, *ex_args)
    return (*out[:4], out[4:])


def _adamw_small(gs, ws, ms, vs, name):
    n = len(gs)

    def body(*refs):
        ins, outs = refs[:4 * n], refs[4 * n:]
        for a in range(n):
            d, m, v = _adam_update(ins[n + a][...], ins[a][...], ins[2 * n + a][...], ins[3 * n + a][...])
            outs[a][...], outs[n + a][...], outs[2 * n + a][...] = d, m, v

    shapes = tuple(jax.ShapeDtypeStruct(w.shape, F32) for w in ws) * 3
    out = pl.pallas_call(
        body, name=name, out_shape=shapes,
        in_specs=[_vmem()] * (4 * n), out_specs=tuple([_vmem()] * (3 * n)),
        compiler_params=pltpu.CompilerParams(vmem_limit_bytes=VMEM_LIMIT),
    )(*gs, *ws, *ms, *vs)
    return out[:n], out[n:2 * n], out[2 * n:]


def _sum_devices(packed, name):
    _, r, wdt = packed.shape

    def body(p_ref, o_ref):
        acc = p_ref[0]
        for j in range(1, NDEV):
            acc = acc + p_ref[j]
        o_ref[...] = acc

    return pl.pallas_call(
        body, name=name, out_shape=jax.ShapeDtypeStruct((r, wdt), F32),
        in_specs=[_vmem()], out_specs=_vmem(),
        compiler_params=pltpu.CompilerParams(vmem_limit_bytes=VMEM_LIMIT),
    )(packed)


def _mod_w_grad(c_t, dmod, name):
    n_layers, _, w3 = dmod.shape
    d = c_t.shape[0]

    def body(c_ref, dm_ref, o_ref):
        for i in range(n_layers):
            acc = c_ref[:, 0:1] * dm_ref[i, 0:1, :]
            for b in range(1, NDEV):
                acc = acc + c_ref[:, b:b + 1] * dm_ref[i, b:b + 1, :]
            o_ref[i] = acc

    return pl.pallas_call(
        body, name=name, out_shape=jax.ShapeDtypeStruct((n_layers, d, w3), F32),
        in_specs=[_vmem(), _vmem()], out_specs=_vmem(),
        compiler_params=pltpu.CompilerParams(vmem_limit_bytes=VMEM_LIMIT),
    )(c_t, dmod)


def _mask_transpose_ws(w_s, name):
    def body(w_ref, wt_ref, wtt_ref):
        tril = (lax.broadcasted_iota(jnp.int32, (CHUNK, CHUNK), 0) >= lax.broadcasted_iota(jnp.int32, (CHUNK, CHUNK), 1))
        for g in range(GROUPS):
            wm = jnp.where(tril, w_ref[g], 0.0)
            wt_ref[g] = wm.astype(BF16)
            wtt_ref[g] = wm.T.astype(BF16)

    shp = jax.ShapeDtypeStruct(w_s.shape, BF16)
    return pl.pallas_call(
        body, name=name, out_shape=(shp, shp), in_specs=[_vmem()], out_specs=(_vmem(), _vmem()),
    )(w_s)


def _pack(pieces):
    flat = jnp.concatenate([p.reshape(-1) for p in pieces])
    rows = -(-flat.shape[0] // (8 * PACK_W)) * 8
    return jnp.pad(flat, (0, rows * PACK_W - flat.shape[0])).reshape(rows, PACK_W)


def _unpack(flat, shapes):
    out, off = [], 0
    for shp in shapes:
        size = 1
        for dim in shp:
            size *= dim
        out.append(flat[off:off + size].reshape(shp))
        off += size
    return out


def kernel(x, c, mod_w, mod_b, norm_g, a_w_in, a_conv_w, a_conv_b, a_w_out, b_w_in, b_ln_g, b_ln_b, b_w_s, b_b_s, b_w_out, final_g, loss_target, m_mod_w, m_mod_b, m_norm_g, m_a_w_in, m_a_conv_w, m_a_conv_b, m_a_w_out, m_b_w_in, m_b_ln_g, m_b_ln_b, m_b_w_s, m_b_b_s, m_b_w_out, m_final_g, v_mod_w, v_mod_b, v_norm_g, v_a_w_in, v_a_conv_w, v_a_conv_b, v_a_w_out, v_b_w_in, v_b_ln_g, v_b_ln_b, v_b_w_s, v_b_b_s, v_b_w_out, v_final_g):
    s, d = x.shape[1], x.shape[2]
    es = a_w_out.shape[1]
    e = NDEV * es
    w3 = mod_w.shape[2]
    me = _index(_pos())
    x0 = x.reshape(s, d)
    tgt = loss_target.reshape(s, d)

    small = jnp.concatenate([a_conv_w[0], b_ln_g, b_ln_b, jnp.zeros((3, es), F32)], axis=0)
    gather_a = _gather_exchange([a_w_in[0].astype(BF16), a_w_out[0].astype(BF16), small], [True, False, True])
    gather_b = _gather_exchange([b_w_in[0].astype(BF16), b_w_out[0].astype(BF16)], [True, False])
    mod, c_all, (wa, woa, small_all) = _mod_vectors(c, mod_w, mod_b, gather_a)
    conv_w, ln_g, ln_b = small_all[0:3], small_all[3:4], small_all[4:5]
    bsf = jnp.repeat(b_b_s[0].T, e // GROUPS, axis=1)
    wt, wtt = _mask_transpose_ws(b_w_s[0], "mask_w_s")
    shift0, scale0, gate0 = mod[0:1, 0:d], mod[0:1, d:2 * d], mod[0:1, 2 * d:]
    shift1, scale1, gate1 = mod[1:2, 0:d], mod[1:2, d:2 * d], mod[1:2, 2 * d:]
    g0, g1, fg = norm_g[0:1], norm_g[1:2], final_g.reshape(1, d)

    proj_a, h0, x1, br_a, conv_a, (wb, wob) = _layer_a_fwd(
        x0, g0, scale0, shift0, gate0, wa, conv_w, a_conv_b, woa, "a_fwd", gather_b)
    proj_b, h1, dx2, loss_acc, dfg, dgate1, v_b, dgv_b, _ = _layer_b_fwd_loss(
        x1, tgt, g1, scale1, shift1, gate1, fg, wb, ln_g, ln_b, wt, bsf, wob, "b_fwd_loss")

    dproj_b, y_b, dx1, dws, dbs, dlg, dlb, dshift1, dscale1, dg1 = _layer_b_bwd(
        proj_b, v_b, dgv_b, dx2, x1, gate1, g1, scale1, ln_g, ln_b, wt, wtt, bsf, wob, wb, "b_bwd")
    def after(vec, started):
        return vec + started[4][0:1, 0:1]

    gs_b_out, _ = _matmul_tn(y_b, dx2, gate1, True, "b_w_out_grad")
    b_out_sent = _scatter_start(gs_b_out, "b_w_out_scatter_start")
    gs_b_in, _ = _matmul_tn(h1, dproj_b, after(jnp.ones((1, dproj_b.shape[1]), F32), b_out_sent), False, "b_w_in_grad")
    b_in_sent = _scatter_start(gs_b_in, "b_w_in_scatter_start")
    dproj_a, y_a, dgate0, dcb, dcw, _ = _conv_mixer_bwd(
        proj_a, dx1, br_a, conv_a, conv_w, after(gate0, b_in_sent), woa, "a_mixer_bwd")
    gs_a_out, _ = _matmul_tn(y_a, dx1, gate0, True, "a_w_out_grad")
    a_out_sent = _scatter_start(gs_a_out, "a_w_out_scatter_start")
    gs_a_in, _ = _matmul_tn(h0, dproj_a, after(jnp.ones((1, dproj_a.shape[1]), F32), a_out_sent), False, "a_w_in_grad")
    a_in_sent = _scatter_start(gs_a_in, "a_w_in_scatter_start")
    dx0, dshift0, dscale0, dg0, _ = _matmul_nt_norm_bwd(dproj_a, wa, x0, dx1, g0, after(scale0, a_in_sent), "a_in_bwd")

    def arrived(started, name):
        part, landed = _scatter_wait(*started[:4], dshift0, name)
        return lax.dynamic_update_slice_in_dim(landed, lax.dynamic_slice_in_dim(part, me, 1, axis=0), me, axis=0)

    gr_b_out, gr_b_in = arrived(b_out_sent, "b_w_out_scatter_wait"), arrived(b_in_sent, "b_w_in_scatter_wait")
    gr_a_out, gr_a_in = arrived(a_out_sent, "a_w_out_scatter_wait"), arrived(a_in_sent, "a_w_in_scatter_wait")

    def big(parts, w, m, v, name, ex=None):
        shp = w.shape
        r2 = lambda t_: t_.reshape(-1, shp[-1])
        g, dl, nm, nv, ex_out = _adamw_reduce(parts, r2(w), r2(m), r2(v), name, ex)
        return tuple(t_.reshape(shp) for t_ in (g, dl, nm, nv)), ex_out

    res = {}
    res["b_w_in"], _ = big([gr_b_in], b_w_in, m_b_w_in, v_b_w_in, "adamw_b_w_in")
    gr_a_in = [gr_a_in]

    pieces = [dshift0, dscale0, dgate0, dshift1, dscale1, dgate1, dg0, dg1, dcb, dcw[0:3], dlg, dlb, dfg,
              dbs[:, 0:GROUPS].T, dws, loss_acc[0:1, 0:1]]
    shapes = [p.shape for p in pieces]
    res["a_w_in"], (packed_all,) = big(gr_a_in, a_w_in, m_a_w_in, v_a_w_in, "adamw_a_w_in",
                                       _gather_exchange([_pack(pieces)], [False]))
    packed_all = packed_all.reshape(NDEV, -1, PACK_W)
    total = _sum_devices(packed_all, "sum_small_grads").reshape(-1)
    (t_sh0, t_sc0, t_ga0, t_sh1, t_sc1, t_ga1, t_g0, t_g1, t_cb, t_cw, t_lg, t_lb, t_fg, t_bs, t_ws, t_loss) = _unpack(
        total, shapes)
    loss = t_loss.reshape(())
    grad_mod_b = jnp.concatenate([jnp.concatenate([t_sh0, t_sc0, t_ga0], axis=1),
                                  jnp.concatenate([t_sh1, t_sc1, t_ga1], axis=1)], axis=0)
    grad_norm_g = jnp.concatenate([t_g0, t_g1], axis=0)
    dmod_all = packed_all.reshape(NDEV, -1)[:, 0:6 * d].reshape(NDEV, 2, 3 * d).transpose(1, 0, 2)
    dmod_mine = lax.dynamic_slice_in_dim(dmod_all, me * w3, w3, axis=2)
    grad_mod_w = _mod_w_grad(c_all.T, dmod_mine, "mod_w_grad")
    grad_a_conv_w = lax.dynamic_slice_in_dim(t_cw, me * es, es, axis=1)
    grad_b_ln_g = lax.dynamic_slice_in_dim(t_lg, me * es, es, axis=1)
    grad_b_ln_b = lax.dynamic_slice_in_dim(t_lb, me * es, es, axis=1)

    res["mod_w"], _ = big([grad_mod_w.reshape(1, -1, w3)], mod_w, m_mod_w, v_mod_w, "adamw_mod_w")
    res["a_w_out"], _ = big([gr_a_out], a_w_out, m_a_w_out, v_a_w_out, "adamw_a_w_out")
    res["b_w_out"], _ = big([gr_b_out], b_w_out, m_b_w_out, v_b_w_out, "adamw_b_w_out")

    small_names = ["mod_b", "norm_g", "a_conv_w", "a_conv_b", "b_ln_g", "b_ln_b", "b_w_s", "b_b_s", "final_g"]
    small_g = [grad_mod_b, grad_norm_g, grad_a_conv_w, t_cb, grad_b_ln_g, grad_b_ln_b, t_ws, t_bs, t_fg]
    small_w = [mod_b, norm_g, a_conv_w, a_conv_b, b_ln_g, b_ln_b, b_w_s, b_b_s, final_g]
    small_m = [m_mod_b, m_norm_g, m_a_conv_w, m_a_conv_b, m_b_ln_g, m_b_ln_b, m_b_w_s, m_b_b_s, m_final_g]
    small_v = [v_mod_b, v_norm_g, v_a_conv_w, v_a_conv_b, v_b_ln_g, v_b_ln_b, v_b_w_s, v_b_b_s, v_final_g]
    as2d = lambda t_: t_.reshape(-1, t_.shape[-1])
    dls, nms, nvs = _adamw_small([as2d(t_) for t_ in small_g], [as2d(t_) for t_ in small_w],
                                 [as2d(t_) for t_ in small_m], [as2d(t_) for t_ in small_v], "adamw_small")
    for a, nme in enumerate(small_names):
        shp = small_w[a].shape
        res[nme] = (small_g[a].reshape(shp), dls[a].reshape(shp), nms[a].reshape(shp), nvs[a].reshape(shp))

    order = ["mod_w", "mod_b", "norm_g", "a_w_in", "a_conv_w", "a_conv_b", "a_w_out", "b_w_in", "b_ln_g", "b_ln_b",
             "b_w_s", "b_b_s", "b_w_out", "final_g"]
    return (loss, dx0.reshape(x.shape), *[res[k][0] for k in order], *[res[k][1] for k in order],
            *[res[k][2] for k in order], *[res[k][3] for k in order])
```

```python
import functools

import jax
import jax.numpy as jnp
from jax import lax
from jax.experimental import pallas as pl
from jax.experimental.pallas import tpu as pltpu

NDEV = 8
CHUNK = 128
GROUPS = 8
RMS_EPS = 1e-6
LN_EPS = 1e-5
ADAM_LR, ADAM_B1, ADAM_B2, ADAM_EPS, ADAM_WD, ADAM_STEP = 0.001, 0.9, 0.999, 1e-08, 0.01, 10
V7X_VMEM_BYTES = 64 * 1024 * 1024
VMEM_LIMIT = V7X_VMEM_BYTES - 8 * 1024 * 1024
TN_VMEM_BUDGET = 46 * 1024 * 1024
PACK_W = 1024
F32, BF16 = jnp.float32, jnp.bfloat16
MESH = pl.DeviceIdType.MESH
RSQRT2 = 0.7071067811865476
INV_SQRT_2PI = 0.3989422804014327
NT_DIMS = (((1,), (1,)), ((), ()))
TN_DIMS = (((0,), (0,)), ((), ()))


def _params(sem=None):
    return pltpu.CompilerParams(dimension_semantics=sem, vmem_limit_bytes=VMEM_LIMIT)


def _vmem():
    return pl.BlockSpec(memory_space=pltpu.VMEM)


def _hbm():
    return pl.BlockSpec(memory_space=pltpu.HBM)


def _full(shape):
    return pl.BlockSpec(shape, lambda *_: (0,) * len(shape))


def _pos():
    return lax.axis_index("x"), lax.axis_index("y"), lax.axis_index("c")


def _index(p):
    return 4 * p[0] + 2 * p[1] + p[2]


def _peer(k):
    x, y, c = _pos()
    return ((1 - x) if (k >> 2) & 1 else x, (1 - y) if (k >> 1) & 1 else y, (1 - c) if k & 1 else c)


def _silu(z):
    sg = jax.nn.sigmoid(z)
    return z * sg, sg * (1.0 + z * (1.0 - sg))


def _gelu(v):
    phi = 0.5 * (1.0 + lax.erf(v * RSQRT2))
    return v * phi, phi + v * (jnp.exp(-0.5 * v * v) * INV_SQRT_2PI)


def _colsum(v):
    return jnp.sum(v, axis=0, keepdims=True)


def _rowsum(v):
    return jnp.sum(v, axis=-1, keepdims=True)


def _gather_all_vmem(slab_ref, send_sems, recv_sems, base):
    me = _index(_pos())
    sends = []
    for k in range(1, NDEV):
        cp = pltpu.make_async_remote_copy(
            src_ref=slab_ref.at[me], dst_ref=slab_ref.at[me],
            send_sem=send_sems.at[base + k - 1], recv_sem=recv_sems.at[base + k - 1],
            device_id=_peer(k), device_id_type=MESH)
        cp.start()
        sends.append(cp)
    for k in range(1, NDEV):
        src = _index(_peer(k))
        pltpu.make_async_remote_copy(
            src_ref=slab_ref.at[src], dst_ref=slab_ref.at[src],
            send_sem=send_sems.at[base + k - 1], recv_sem=recv_sems.at[base + k - 1],
            device_id=_peer(k), device_id_type=MESH).wait_recv()
    for cp in sends:
        cp.wait_send()


def _mod_vectors(c, mod_w, mod_b, ex):
    n_layers, d, w3 = mod_w.shape
    r_in, r_out = len(ex.arrays), len(ex.out_shapes)

    def body(*refs):
        c_ref, mw_ref, mb_ref = refs[:3]
        ex_ins = refs[3:3 + r_in]
        mod_ref, call_ref = refs[3 + r_in:5 + r_in]
        ex_outs = refs[5 + r_in:5 + r_in + r_out]
        cslab, pslab, send_sems, recv_sems = refs[5 + r_in + r_out:9 + r_in + r_out]
        ex_sems = refs[9 + r_in + r_out:]
        ex.start(ex_ins, ex_outs, ex_sems)
        me = _index(_pos())
        cv = c_ref[...]
        cslab[me] = jnp.broadcast_to(cv * jax.nn.sigmoid(cv), (8, d))
        _gather_all_vmem(cslab, send_sems, recv_sems, 0)
        c_all = jnp.concatenate([cslab[k, 0:1, :] for k in range(NDEV)], axis=0)
        call_ref[...] = c_all
        for i in range(n_layers):
            pslab[me, i * NDEV:(i + 1) * NDEV, :] = jnp.dot(
                c_all, mw_ref[i], preferred_element_type=F32, precision=lax.Precision.HIGHEST)
        _gather_all_vmem(pslab, send_sems, recv_sems, NDEV - 1)
        for i in range(n_layers):
            for k in range(NDEV):
                mod_ref[i:i + 1, k * w3:(k + 1) * w3] = (
                    pslab[k, pl.ds(i * NDEV + me, 1), :] + mb_ref[i:i + 1, k * w3:(k + 1) * w3])
        for passing_on in ex.middles:
            passing_on(ex_ins, ex_outs, ex_sems)
        ex.finish(ex_ins, ex_outs, ex_sems)

    out = pl.pallas_call(
        body, name="mod_vectors",
        out_shape=(jax.ShapeDtypeStruct((n_layers, 3 * d), F32), jax.ShapeDtypeStruct((NDEV, d), F32), *ex.out_shapes),
        in_specs=[_vmem(), _vmem(), _vmem()] + [_hbm()] * r_in, out_specs=(_vmem(), _vmem(), *([_hbm()] * r_out)),
        scratch_shapes=[pltpu.VMEM((NDEV, 8, d), F32), pltpu.VMEM((NDEV, n_layers * NDEV, w3), F32),
                        pltpu.SemaphoreType.DMA((2 * (NDEV - 1),)), pltpu.SemaphoreType.DMA((2 * (NDEV - 1),)), *ex.sems],
        compiler_params=pltpu.CompilerParams(vmem_limit_bytes=VMEM_LIMIT),
    )(c, mod_w, mod_b, *ex.arrays)
    return out[0], out[1], out[2:]


class _Exchange:
    def __init__(self, arrays, out_shapes, sems, start, middles, finish):
        self.arrays, self.out_shapes, self.sems = list(arrays), list(out_shapes), list(sems)
        self.start, self.middles, self.finish = start, list(middles), finish


def _gather_exchange(shards, by_cols):
    n = len(shards)
    shapes = [sh.shape for sh in shards]

    def tools(ins, outs, sems):
        send_sems, recv_sems, local_sems = sems
        x, y, c = _pos()
        chips = [(1 - x, y), (x, 1 - y), (1 - x, 1 - y)]
        south = c == 0
        relayed = (jnp.where(south, 1 - x, x), jnp.where(south, y, 1 - y), c)
        relay_to = (jnp.where(south, x, 1 - x), jnp.where(south, 1 - y, y), c)

        def place(a, block):
            r, cc = shapes[a]
            if by_cols[a]:
                return outs[a].at[:, pl.ds(_index(block) * cc, cc)]
            return outs[a].at[pl.ds(_index(block) * r, r), :]

        def copy(a, k, block, to, src=None):
            dst = place(a, block)
            return pltpu.make_async_remote_copy(
                src_ref=dst if src is None else src, dst_ref=dst,
                send_sem=send_sems.at[a * 7 + k], recv_sem=recv_sems.at[a * 7 + k],
                device_id=to, device_id_type=MESH)

        mine = [pltpu.make_async_copy(ins[a], place(a, (x, y, c)), local_sems.at[a]) for a in range(n)]
        first = []
        for a in range(n):
            first.append(copy(a, 0, (x, y, c), (x, y, 1 - c), src=ins[a]))
            first += [copy(a, 1 + j, (x, y, c), (*chip, c), src=ins[a]) for j, chip in enumerate(chips[:2])]
        relays = [copy(a, 3, relayed, relay_to) for a in range(n)]
        passed = [copy(a, 4 + j, (*chip, c), (x, y, 1 - c)) for j, chip in enumerate(chips) for a in range(n)]
        return (x, y, c), chips, copy, mine, first, relays, passed

    def start(ins, outs, sems):
        _, _, _, mine, first, _, _ = tools(ins, outs, sems)
        for cp in mine + first:
            cp.start()

    def pass_neighbours(ins, outs, sems):
        (x, y, c), chips, copy, _, _, relays, passed = tools(ins, outs, sems)
        for j, chip in enumerate(chips[:2]):
            for a in range(n):
                copy(a, 1 + j, (*chip, c), (x, y, c)).wait_recv()
                passed[j * n + a].start()
        for cp in relays:
            cp.start()

    def pass_diagonal(ins, outs, sems):
        (x, y, c), chips, copy, _, _, _, passed = tools(ins, outs, sems)
        for a in range(n):
            copy(a, 3, (*chips[2], c), (x, y, c)).wait_recv()
            passed[2 * n + a].start()

    def finish(ins, outs, sems):
        (x, y, c), chips, copy, mine, first, relays, passed = tools(ins, outs, sems)
        for a in range(n):
            copy(a, 0, (x, y, 1 - c), (x, y, c)).wait_recv()
        for j, chip in enumerate(chips):
            for a in range(n):
                copy(a, 4 + j, (*chip, 1 - c), (x, y, c)).wait_recv()
        for cp in first + relays + passed:
            cp.wait_send()
        for cp in mine:
            cp.wait()

    out_shapes = [jax.ShapeDtypeStruct((r, NDEV * cc) if bc else (NDEV * r, cc), sh.dtype)
                  for (r, cc), bc, sh in zip(shapes, by_cols, shards)]
    sems = [pltpu.SemaphoreType.DMA((7 * n,)), pltpu.SemaphoreType.DMA((7 * n,)), pltpu.SemaphoreType.DMA((n,))]
    return _Exchange(shards, out_shapes, sems, start, [pass_neighbours, pass_diagonal], finish)


def _scatter_exchange(parts):
    n = len(parts)

    def tools(ins, outs, sems):
        send_sems, recv_sems, local_sems = sems
        me = _index(_pos())
        mine = [pltpu.make_async_copy(ins[a].at[me], outs[a].at[me], local_sems.at[a]) for a in range(n)]
        sends, arrivals = [], []
        for k in range(1, NDEV):
            peer = _peer(k)
            for a in range(n):
                pair = dict(send_sem=send_sems.at[a * 7 + k - 1], recv_sem=recv_sems.at[a * 7 + k - 1],
                            device_id=peer, device_id_type=MESH)
                sends.append(pltpu.make_async_remote_copy(src_ref=ins[a].at[_index(peer)], dst_ref=outs[a].at[me], **pair))
                slot = outs[a].at[_index(peer)]
                arrivals.append(pltpu.make_async_remote_copy(src_ref=slot, dst_ref=slot, **pair))
        return mine, sends, arrivals

    def start(ins, outs, sems):
        mine, sends, _ = tools(ins, outs, sems)
        for cp in mine + sends:
            cp.start()

    def finish(ins, outs, sems):
        mine, sends, arrivals = tools(ins, outs, sems)
        for cp in arrivals:
            cp.wait_recv()
        for cp in sends:
            cp.wait_send()
        for cp in mine:
            cp.wait()

    out_shapes = [jax.ShapeDtypeStruct(p.shape, p.dtype) for p in parts]
    sems = [pltpu.SemaphoreType.DMA((7 * n,)), pltpu.SemaphoreType.DMA((7 * n,)), pltpu.SemaphoreType.DMA((n,))]
    return _Exchange(parts, out_shapes, sems, start, [], finish)


def _scatter_start(part, name):
    def body(part_ref, land_ref, send_sems, recv_sems, part_thru, land_thru, token):
        me = _index(_pos())
        for k in range(1, NDEV):
            peer = _peer(k)
            pltpu.make_async_remote_copy(
                src_ref=part_ref.at[_index(peer)], dst_ref=land_ref.at[me],
                send_sem=send_sems.at[k - 1], recv_sem=recv_sems.at[k - 1],
                device_id=peer, device_id_type=MESH).start()
        token[...] = jnp.zeros_like(token)

    sem = pl.BlockSpec(memory_space=pltpu.SEMAPHORE)
    return pl.pallas_call(
        body, name=name,
        out_shape=(pltpu.SemaphoreType.DMA((NDEV - 1,)), pltpu.SemaphoreType.DMA((NDEV - 1,)),
                   pltpu.HBM(part.shape, part.dtype), pltpu.HBM(part.shape, part.dtype),
                   jax.ShapeDtypeStruct((8, 128), F32)),
        in_specs=(_hbm(), _hbm()), out_specs=(sem, sem, _hbm(), _hbm(), _vmem()),
        input_output_aliases={0: 2, 1: 3},
        compiler_params=pltpu.CompilerParams(has_side_effects=pltpu.SideEffectType.DATAFLOW_SIDE_EFFECTING),
    )(pltpu.with_memory_space_constraint(part, pltpu.HBM),
      pltpu.with_memory_space_constraint(lax.empty(part.shape, part.dtype), pltpu.HBM))


def _scatter_wait(send_sems, recv_sems, part_thru, land_thru, after, name):
    def body(part_ref, land_ref, send_sems, recv_sems, after_ref, part_dead, got_ref):
        for k in range(1, NDEV):
            peer = _peer(k)
            slot = land_ref.at[_index(peer)]
            copy = pltpu.make_async_remote_copy(
                src_ref=part_ref.at[_index(peer)], dst_ref=slot,
                send_sem=send_sems.at[k - 1], recv_sem=recv_sems.at[k - 1],
                device_id=peer, device_id_type=MESH)
            copy.wait_send()
            copy.wait_recv()

    sem = pl.BlockSpec(memory_space=pltpu.SEMAPHORE)
    return pl.pallas_call(
        body, name=name,
        out_shape=(pltpu.HBM(part_thru.shape, part_thru.dtype), pltpu.HBM(land_thru.shape, land_thru.dtype)),
        in_specs=(_hbm(), _hbm(), sem, sem, pl.BlockSpec(memory_space=pl.ANY)), out_specs=(_hbm(), _hbm()),
        input_output_aliases={0: 0, 1: 1},
        compiler_params=pltpu.CompilerParams(has_side_effects=pltpu.SideEffectType.DATAFLOW_SIDE_EFFECTING),
    )(part_thru, land_thru, send_sems, recv_sems, after)


def _carry(ex, body, n_in, n_out, first, middle, last):
    if ex is None:
        return body
    r_in, r_out = len(ex.arrays), len(ex.out_shapes)

    def wrapped(*refs):
        ins, rins = refs[:n_in], refs[n_in:n_in + r_in]
        outs = refs[n_in + r_in:n_in + r_in + n_out]
        routs = refs[n_in + r_in + n_out:n_in + r_in + n_out + r_out]
        rest = refs[n_in + r_in + n_out + r_out:]
        scratch, sems = rest[:len(rest) - len(ex.sems)], rest[len(rest) - len(ex.sems):]

        @pl.when(first())
        def _():
            ex.start(rins, routs, sems)

        for passing_on, at_step in zip(ex.middles, middle or []):
            pl.when(at_step())(functools.partial(passing_on, rins, routs, sems))

        body(*ins, *outs, *scratch)

        @pl.when(last())
        def _():
            ex.finish(rins, routs, sems)

    return wrapped


def _carried(ex):
    if ex is None:
        return [], [], [], [], []
    return ex.arrays, [_hbm()] * len(ex.arrays), ex.out_shapes, [_hbm()] * len(ex.out_shapes), ex.sems


def _resident(shape):
    return pl.BlockSpec(shape, lambda *_: (0,) * len(shape), pipeline_mode=pl.Buffered(1))


def _norm_modulate(x_ref, g_ref, sc_ref, sh_ref):
    xv = x_ref[...]
    r = lax.rsqrt(jnp.mean(xv * xv, axis=-1, keepdims=True) + RMS_EPS)
    return ((xv * r) * g_ref[...] * (1.0 + sc_ref[...]) + sh_ref[...]).astype(BF16)


def _conv_taps(cx, t6, t7, row):
    p1 = jnp.where(row == 0, t7, pltpu.roll(cx, 1, 0))
    p2 = jnp.where(row == 0, t6, jnp.where(row == 1, t7, pltpu.roll(cx, 2, 0)))
    return p1, p2


def _layer_a_fwd(x, g, scale, shift, gate, wi, cw, cb, wo, name, ex=None):
    s, d = x.shape
    e = wo.shape[0]
    t = min(s, 256)
    n_t = s // t
    cwid = min(e, 512)

    def body(x_ref, g_ref, sc_ref, sh_ref, gate_ref, wi_ref, cw_ref, cb_ref, wo_ref,
             proj_ref, h_ref, x1_ref, br_ref, conv_ref, y_scr, tail_scr):
        @pl.when(pl.program_id(0) == 0)
        def _():
            tail_scr[...] = jnp.zeros_like(tail_scr)
        h_ref[...] = _norm_modulate(x_ref, g_ref, sc_ref, sh_ref)
        row = lax.broadcasted_iota(jnp.int32, (t, cwid), 0)

        def project(c0):
            v = jnp.dot(h_ref[...], wi_ref[:, c0:c0 + cwid], preferred_element_type=F32)
            proj_ref[:, c0:c0 + cwid] = v.astype(BF16)
            return v

        for c0 in range(0, e, cwid):
            sl = slice(c0, c0 + cwid)
            bg, z = project(c0), project(3 * e + c0)
            cx = project(e + c0) * project(2 * e + c0)
            p1, p2 = _conv_taps(cx, tail_scr[6:7, sl], tail_scr[7:8, sl], row)
            conv = cb_ref[:, sl] + cw_ref[2:3, sl] * cx + cw_ref[0:1, sl] * p2 + cw_ref[1:2, sl] * p1
            conv_ref[:, sl] = conv.astype(BF16)
            y_scr[:, sl] = (_silu(z)[0] * bg * conv).astype(BF16)
            tail_scr[:, sl] = cx[t - 8:t, :]
        br = jnp.dot(y_scr[...], wo_ref[...], preferred_element_type=F32)
        x1_ref[...] = x_ref[...] + gate_ref[...] * br
        br_ref[...] = br.astype(BF16)

    step = lambda k: (lambda: pl.program_id(0) == k)
    body = _carry(ex, body, 9, 5, step(0), [step(n_t // 3), step((2 * n_t) // 3)], step(n_t - 1))
    ex_args, ex_in, ex_shapes, ex_out, ex_sems = _carried(ex)
    tok = pl.BlockSpec((t, d), lambda i: (i, 0))
    out = pl.pallas_call(
        body, name=name, grid=(n_t,),
        out_shape=(jax.ShapeDtypeStruct((s, 4 * e), BF16), jax.ShapeDtypeStruct((s, d), BF16),
                   jax.ShapeDtypeStruct((s, d), F32), jax.ShapeDtypeStruct((s, d), BF16),
                   jax.ShapeDtypeStruct((s, e), BF16), *ex_shapes),
        in_specs=[tok, _full((1, d)), _full((1, d)), _full((1, d)), _full((1, d)), _resident((d, 4 * e)),
                  _full((3, e)), _full((1, e)), _resident((e, d)), *ex_in],
        out_specs=(pl.BlockSpec((t, 4 * e), lambda i: (i, 0)), tok, tok, tok,
                   pl.BlockSpec((t, e), lambda i: (i, 0)), *ex_out),
        scratch_shapes=[pltpu.VMEM((t, e), BF16), pltpu.VMEM((8, e), F32), *ex_sems],
        compiler_params=_params(("arbitrary",)),
    )(x, g, scale, shift, gate, wi, cw, cb, wo, *ex_args)
    return (*out[:5], out[5:])


def _ln_stats(v_of, v_scr, t, e):
    gw = e // GROUPS
    s1 = jnp.zeros((t, 1), F32)
    for g in range(GROUPS):
        v = v_of(g)
        v_scr[:, g * gw:(g + 1) * gw] = v
        s1 = s1 + _rowsum(v)
    mu = s1 * (1.0 / e)
    s2 = jnp.zeros((t, 1), F32)
    for g in range(GROUPS):
        dv = v_scr[:, g * gw:(g + 1) * gw] - mu
        s2 = s2 + _rowsum(dv * dv)
    return mu, lax.rsqrt(s2 * (1.0 / e) + LN_EPS)


def _layer_b_fwd_loss(x1, tgt, g1, scale, shift, gate, fg, wi, lng, lnb, wt, bsf, wo, name, ex=None):
    s, d = x1.shape
    e = wo.shape[0]
    gw = e // GROUPS
    t = min(s, 256)
    n_t = s // t

    def body(x1_ref, tgt_ref, g_ref, sc_ref, sh_ref, gate_ref, fg_ref, wi_ref, lng_ref, lnb_ref, wt_ref, bsf_ref, wo_ref,
             proj_ref, h_ref, dx2_ref, loss_ref, dfg_ref, dgate_ref, v_ref, dgv_ref, v_scr, y_scr):
        @pl.when(pl.program_id(0) == 0)
        def _():
            loss_ref[...] = jnp.zeros_like(loss_ref)
            dfg_ref[...] = jnp.zeros_like(dfg_ref)
            dgate_ref[...] = jnp.zeros_like(dgate_ref)
        h_ref[...] = _norm_modulate(x1_ref, g_ref, sc_ref, sh_ref)

        def project(c0):
            v = jnp.dot(h_ref[...], wi_ref[:, c0:c0 + gw], preferred_element_type=F32)
            proj_ref[:, c0:c0 + gw] = v.astype(BF16)
            return v

        def gelu_v(g):
            gs = slice(g * gw, (g + 1) * gw)
            v, dgv = _gelu(project(e + g * gw))
            v_ref[:, gs] = v.astype(BF16)
            dgv_ref[:, gs] = dgv.astype(BF16)
            return v

        mu, rs = _ln_stats(gelu_v, v_scr, t, e)
        for g in range(GROUPS):
            gs = slice(g * gw, (g + 1) * gw)
            vn = (((v_scr[:, gs] - mu) * rs) * lng_ref[:, gs] + lnb_ref[:, gs]).astype(BF16)
            u = _gelu(project(g * gw))[0]
            sz = _silu(project(2 * e + g * gw))[0]
            for ch in range(t // CHUNK):
                rows = slice(ch * CHUNK, (ch + 1) * CHUNK)
                mixed = jnp.dot(wt_ref[g], vn[rows], preferred_element_type=F32) + bsf_ref[:, gs]
                y_scr[rows, gs] = (sz[rows] * (u[rows] * mixed)).astype(BF16)
        br = jnp.dot(y_scr[...], wo_ref[...], preferred_element_type=F32)
        x2 = x1_ref[...] + gate_ref[...] * br
        r2 = lax.rsqrt(jnp.mean(x2 * x2, axis=-1, keepdims=True) + RMS_EPS)
        xn = x2 * r2
        diff = xn * fg_ref[...] - tgt_ref[...]
        loss_ref[...] += jnp.broadcast_to(0.5 * _colsum(jnp.mean(diff * diff, axis=-1, keepdims=True)), loss_ref.shape)
        dout = diff * (1.0 / d)
        dfg_ref[...] += _colsum(dout * xn)
        dxn = dout * fg_ref[...]
        dx2 = r2 * (dxn - xn * jnp.mean(dxn * xn, axis=-1, keepdims=True))
        dx2_ref[...] = dx2
        dgate_ref[...] += _colsum(dx2 * br)

    step = lambda k: (lambda: pl.program_id(0) == k)
    body = _carry(ex, body, 13, 8, step(0), [step(n_t // 3), step((2 * n_t) // 3)], step(n_t - 1))
    ex_args, ex_in, ex_shapes, ex_out, ex_sems = _carried(ex)
    tok = pl.BlockSpec((t, d), lambda i: (i, 0))
    vec = _full((1, d))
    out = pl.pallas_call(
        body, name=name, grid=(n_t,),
        out_shape=(jax.ShapeDtypeStruct((s, 3 * e), BF16), jax.ShapeDtypeStruct((s, d), BF16),
                   jax.ShapeDtypeStruct((s, d), F32), jax.ShapeDtypeStruct((8, 128), F32),
                   jax.ShapeDtypeStruct((1, d), F32), jax.ShapeDtypeStruct((1, d), F32),
                   jax.ShapeDtypeStruct((s, e), BF16), jax.ShapeDtypeStruct((s, e), BF16), *ex_shapes),
        in_specs=[tok, tok, vec, vec, vec, vec, vec, _resident((d, 3 * e)), _full((1, e)), _full((1, e)),
                  _full((GROUPS, CHUNK, CHUNK)), _resident((CHUNK, e)), _resident((e, d)), *ex_in],
        out_specs=(pl.BlockSpec((t, 3 * e), lambda i: (i, 0)), tok, tok, _full((8, 128)), vec, vec,
                   pl.BlockSpec((t, e), lambda i: (i, 0)), pl.BlockSpec((t, e), lambda i: (i, 0)), *ex_out),
        scratch_shapes=[pltpu.VMEM((t, e), F32), pltpu.VMEM((t, e), BF16), *ex_sems],
        compiler_params=_params(("arbitrary",)),
    )(x1, tgt, g1, scale, shift, gate, fg, wi, lng, lnb, wt, bsf, wo, *ex_args)
    return (*out[:8], out[8:])


def _norm_modulate_bwd(dh, x_ref, dres_ref, g_ref, sc_ref, dx_ref, dsh_ref, p_scr):
    xv = x_ref[...]
    r = lax.rsqrt(jnp.mean(xv * xv, axis=-1, keepdims=True) + RMS_EPS)
    xn = xv * r
    dsh_ref[...] += _colsum(dh)
    p_scr[...] += _colsum(dh * xn)
    dxn = dh * (g_ref[...] * (1.0 + sc_ref[...]))
    dx_ref[...] = r * (dxn - xn * jnp.mean(dxn * xn, axis=-1, keepdims=True)) + dres_ref[...]


def _layer_b_bwd(proj, v_act, dgv, dx2, x1, gate, g1, scale, lng, lnb, wt, wtt, bsf, wo, wi, name):
    s, e3 = proj.shape
    e = e3 // 3
    d = dx2.shape[1]
    gw = e // GROUPS
    t = min(s, 256)
    n_t = s // t

    def body(pu_ref, pz_ref, v_ref, dgv_ref, dx_ref, x1_ref, gate_ref, g_ref, sc_ref, lng_ref, lnb_ref, wt_ref, wtt_ref,
             bsf_ref, wo_ref, wi_ref,
             dp_ref, y_ref, dx1_ref, dws_ref, dbs_ref, dlg_ref, dlb_ref, dsh_ref, dsc_ref, dg_ref,
             v_scr, dbr_scr, dvn_scr, dbs_scr, p_scr, dy_scr, vn_scr, mixed_scr):
        @pl.when(pl.program_id(0) == 0)
        def _():
            dws_ref[...] = jnp.zeros_like(dws_ref)
            dlg_ref[...] = jnp.zeros_like(dlg_ref)
            dlb_ref[...] = jnp.zeros_like(dlb_ref)
            dsh_ref[...] = jnp.zeros_like(dsh_ref)
            dbs_scr[...] = jnp.zeros_like(dbs_scr)
            p_scr[...] = jnp.zeros_like(p_scr)
        dbr_scr[...] = (dx_ref[...] * gate_ref[...]).astype(BF16)
        mu, rs = _ln_stats(lambda g: v_ref[:, g * gw:(g + 1) * gw].astype(F32), v_scr, t, e)
        tril = (lax.broadcasted_iota(jnp.int32, (CHUNK, CHUNK), 0) >= lax.broadcasted_iota(jnp.int32, (CHUNK, CHUNK), 1))
        c1 = jnp.zeros((t, 1), F32)
        c2 = jnp.zeros((t, 1), F32)
        span = 2
        kw = span * gw
        dh = jnp.zeros((t, d), F32)

        def through_w_in(c0):
            return lax.dot_general(dp_ref[:, c0:c0 + kw], wi_ref[:, c0:c0 + kw], NT_DIMS, preferred_element_type=F32)

        dy_scr[...] = lax.dot_general(dbr_scr[...], wo_ref[...], NT_DIMS, preferred_element_type=F32)
        for g in range(GROUPS):
            gs = slice(g * gw, (g + 1) * gw)
            vhat = (v_scr[:, gs] - mu) * rs
            v_scr[:, gs] = vhat
            vn = (vhat * lng_ref[:, gs] + lnb_ref[:, gs]).astype(BF16)
            vn_scr[:, gs] = vn
            for ch in range(t // CHUNK):
                rows = slice(ch * CHUNK, (ch + 1) * CHUNK)
                mixed_scr[rows, gs] = jnp.dot(wt_ref[g], vn[rows], preferred_element_type=F32) + bsf_ref[:, gs]
        for g in range(GROUPS):
            gs = slice(g * gw, (g + 1) * gw)
            vhat = v_scr[:, gs]
            lg = lng_ref[:, gs]
            for ch in range(t // CHUNK):
                rows = slice(ch * CHUNK, (ch + 1) * CHUNK)
                mixed = mixed_scr[rows, gs]
                u, dgu = _gelu(pu_ref[rows, gs].astype(F32))
                sz, dsz = _silu(pz_ref[rows, gs].astype(F32))
                sgate = u * mixed
                y_ref[rows, gs] = (sz * sgate).astype(BF16)
                dy = dy_scr[rows, gs]
                dp_ref[rows, 2 * e + g * gw:2 * e + (g + 1) * gw] = (dy * sgate * dsz).astype(BF16)
                ds = dy * sz
                dp_ref[rows, gs] = (ds * mixed * dgu).astype(BF16)
                dm = ds * u
                dbs_scr[:, gs] += dm
                dmb = dm.astype(BF16)
                dws_ref[g] += jnp.where(tril, lax.dot_general(dmb, vn_scr[rows, gs], NT_DIMS, preferred_element_type=F32), 0.0)
                dvn_scr[rows, gs] = jnp.dot(wtt_ref[g], dmb, preferred_element_type=F32)
            dvn = dvn_scr[:, gs]
            dlb_ref[:, gs] += _colsum(dvn)
            dlg_ref[:, gs] += _colsum(dvn * vhat)
            dvh = dvn * lg
            c1 = c1 + _rowsum(dvh)
            c2 = c2 + _rowsum(dvh * vhat)
            if g % span == span - 1:
                dh = dh + through_w_in(g * gw + gw - kw) + through_w_in(2 * e + g * gw + gw - kw)
        c1 = c1 * (1.0 / e)
        c2 = c2 * (1.0 / e)
        for g in range(GROUPS):
            gs = slice(g * gw, (g + 1) * gw)
            dv = rs * (dvn_scr[:, gs] * lng_ref[:, gs] - c1 - v_scr[:, gs] * c2)
            dp_ref[:, e + g * gw:e + (g + 1) * gw] = (dv * dgv_ref[:, gs]).astype(BF16)
            if g % span == span - 1:
                dh = dh + through_w_in(e + g * gw + gw - kw)
        _norm_modulate_bwd(dh, x1_ref, dx_ref, g_ref, sc_ref, dx1_ref, dsh_ref, p_scr)

        @pl.when(pl.program_id(0) == n_t - 1)
        def _():
            lane = lax.broadcasted_iota(jnp.int32, (CHUNK, 128), 1)
            acc = jnp.zeros((CHUNK, 128), F32)
            for g in range(GROUPS):
                acc = acc + jnp.where(lane == g, _rowsum(dbs_scr[:, g * gw:(g + 1) * gw]), 0.0)
            dbs_ref[...] = acc
            dsc_ref[...] = p_scr[...] * g_ref[...]
            dg_ref[...] = p_scr[...] * (1.0 + sc_ref[...])

    tok = pl.BlockSpec((t, d), lambda i: (i, 0))
    vec, evec, ws = _full((1, d)), _full((1, e)), _full((GROUPS, CHUNK, CHUNK))
    vshape = jax.ShapeDtypeStruct((1, d), F32)
    return pl.pallas_call(
        body, name=name, grid=(n_t,),
        out_shape=(jax.ShapeDtypeStruct((s, e3), BF16), jax.ShapeDtypeStruct((s, e), BF16), jax.ShapeDtypeStruct((s, d), F32),
                   jax.ShapeDtypeStruct((GROUPS, CHUNK, CHUNK), F32), jax.ShapeDtypeStruct((CHUNK, 128), F32),
                   jax.ShapeDtypeStruct((1, e), F32), jax.ShapeDtypeStruct((1, e), F32), vshape, vshape, vshape),
        in_specs=[pl.BlockSpec((t, e), lambda i: (i, 0)), pl.BlockSpec((t, e), lambda i: (i, 2)),
                  pl.BlockSpec((t, e), lambda i: (i, 0)), pl.BlockSpec((t, e), lambda i: (i, 0)),
                  tok, tok, vec, vec, vec, evec, evec, ws, ws,
                  _resident((CHUNK, e)), _resident((e, d)), _resident((d, e3))],
        out_specs=(pl.BlockSpec((t, e3), lambda i: (i, 0)), pl.BlockSpec((t, e), lambda i: (i, 0)), tok,
                   ws, _full((CHUNK, 128)), evec, evec, vec, vec, vec),
        scratch_shapes=[pltpu.VMEM((t, e), F32), pltpu.VMEM((t, d), BF16),
                        pltpu.VMEM((t, e), F32), pltpu.VMEM((CHUNK, e), F32), pltpu.VMEM((1, d), F32),
                        pltpu.VMEM((t, e), F32), pltpu.VMEM((t, e), BF16), pltpu.VMEM((t, e), F32)],
        compiler_params=_params(("arbitrary",)),
    )(proj, proj, v_act, dgv, dx2, x1, gate, g1, scale, lng, lnb, wt, wtt, bsf, wo, wi)


def _conv_mixer_bwd(proj, dx1, br, conv_a, cw, gate, wo, name, ex=None):
    s, e4 = proj.shape
    e = e4 // 4
    d = dx1.shape[1]
    t = min(s, 256)
    n_t = s // t
    cwid = min(e, 512)

    def body(p_ref, dx_ref, br_ref, conv_ref, cw_ref, gate_ref, wo_ref,
             dp_ref, y_ref, dgate_ref, dcb_ref, dcw_ref, dy_scr, head_scr):
        i = pl.program_id(0)

        @pl.when(i == 0)
        def _():
            dgate_ref[...] = jnp.zeros_like(dgate_ref)
            dcb_ref[...] = jnp.zeros_like(dcb_ref)
            dcw_ref[...] = jnp.zeros_like(dcw_ref)
            head_scr[...] = jnp.zeros_like(head_scr)
        dx = dx_ref[...]
        dgate_ref[...] += _colsum(dx * br_ref[...].astype(F32))
        dy_scr[...] = lax.dot_general((dx * gate_ref[...]).astype(BF16), wo_ref[...], NT_DIMS,
                                      preferred_element_type=F32)
        row = lax.broadcasted_iota(jnp.int32, (t, cwid), 0)
        for c0 in range(0, e, cwid):
            sl = slice(c0, c0 + cwid)
            bg = p_ref[:, c0:c0 + cwid].astype(F32)
            cg = p_ref[:, e + c0:e + c0 + cwid].astype(F32)
            xin = p_ref[:, 2 * e + c0:2 * e + c0 + cwid].astype(F32)
            z = p_ref[:, 3 * e + c0:3 * e + c0 + cwid].astype(F32)
            cx = cg * xin
            w0, w1, w2 = cw_ref[0:1, sl], cw_ref[1:2, sl], cw_ref[2:3, sl]
            conv = conv_ref[:, sl].astype(F32)
            sz, dsz = _silu(z)
            dy = dy_scr[:, sl]
            y_ref[:, sl] = (sz * bg * conv).astype(BF16)
            dp_ref[:, 3 * e + c0:3 * e + c0 + cwid] = (dy * bg * conv * dsz).astype(BF16)
            dp_ref[:, c0:c0 + cwid] = (dy * sz * conv).astype(BF16)
            dconv = dy * sz * bg
            h0, h1 = head_scr[0:1, sl], head_scr[1:2, sl]
            n1 = jnp.where(row == t - 1, h0, pltpu.roll(dconv, t - 1, 0))
            n2 = jnp.where(row == t - 2, h0, jnp.where(row == t - 1, h1, pltpu.roll(dconv, t - 2, 0)))
            dcb_ref[:, sl] += _colsum(dconv)
            dcw_ref[2:3, sl] += _colsum(dconv * cx)
            dcw_ref[1:2, sl] += _colsum(n1 * cx)
            dcw_ref[0:1, sl] += _colsum(n2 * cx)
            dcx = w2 * dconv + w1 * n1 + w0 * n2
            dp_ref[:, e + c0:e + c0 + cwid] = (dcx * xin).astype(BF16)
            dp_ref[:, 2 * e + c0:2 * e + c0 + cwid] = (dcx * cg).astype(BF16)
            head_scr[:, sl] = dconv[0:8, :]

    body = _carry(ex, body, 7, 5, lambda: pl.program_id(0) == 0, None, lambda: pl.program_id(0) == n_t - 1)
    ex_args, ex_in, ex_shapes, ex_out, ex_sems = _carried(ex)
    rev = lambda i: (n_t - 1 - i, 0)
    out = pl.pallas_call(
        body, name=name, grid=(n_t,),
        out_shape=(jax.ShapeDtypeStruct((s, e4), BF16), jax.ShapeDtypeStruct((s, e), BF16),
                   jax.ShapeDtypeStruct((1, d), F32), jax.ShapeDtypeStruct((1, e), F32), jax.ShapeDtypeStruct((8, e), F32),
                   *ex_shapes),
        in_specs=[pl.BlockSpec((t, e4), rev), pl.BlockSpec((t, d), rev), pl.BlockSpec((t, d), rev),
                  pl.BlockSpec((t, e), rev), _full((3, e)), _full((1, d)), _full((e, d)), *ex_in],
        out_specs=(pl.BlockSpec((t, e4), rev), pl.BlockSpec((t, e), rev), _full((1, d)), _full((1, e)), _full((8, e)),
                   *ex_out),
        scratch_shapes=[pltpu.VMEM((t, e), F32), pltpu.VMEM((8, e), F32), *ex_sems],
        compiler_params=_params(("arbitrary",)),
    )(proj, dx1, br, conv_a, cw, gate, wo, *ex_args)
    return (*out[:5], out[5:])


def _matmul_nt_norm_bwd(dproj, w, xin, dres, g, scale, name, ex=None):
    s, d = xin.shape
    n = w.shape[1]
    tm = min(s, 512)
    n_i = s // tm

    def body(dp_ref, w_ref, x_ref, dres_ref, g_ref, sc_ref, dx_ref, dsh_ref, dsc_ref, dg_ref, p_scr):
        i = pl.program_id(0)

        @pl.when(i == 0)
        def _():
            dsh_ref[...] = jnp.zeros_like(dsh_ref)
            p_scr[...] = jnp.zeros_like(p_scr)
        dh = lax.dot_general(dp_ref[...], w_ref[...], NT_DIMS, preferred_element_type=F32)
        _norm_modulate_bwd(dh, x_ref, dres_ref, g_ref, sc_ref, dx_ref, dsh_ref, p_scr)

        @pl.when(i == n_i - 1)
        def _():
            dsc_ref[...] = p_scr[...] * g_ref[...]
            dg_ref[...] = p_scr[...] * (1.0 + sc_ref[...])

    body = _carry(ex, body, 6, 4, lambda: pl.program_id(0) == 0, None, lambda: pl.program_id(0) == n_i - 1)
    ex_args, ex_in, ex_shapes, ex_out, ex_sems = _carried(ex)
    tok = pl.BlockSpec((tm, d), lambda i: (i, 0))
    vec = pl.BlockSpec((1, d), lambda i: (0, 0))
    vshape = jax.ShapeDtypeStruct((1, d), F32)
    out = pl.pallas_call(
        body, name=name, grid=(n_i,),
        out_shape=(jax.ShapeDtypeStruct((s, d), F32), vshape, vshape, vshape, *ex_shapes),
        in_specs=[pl.BlockSpec((tm, n), lambda i: (i, 0)), _resident((d, n)), tok, tok, vec, vec, *ex_in],
        out_specs=(tok, vec, vec, vec, *ex_out),
        scratch_shapes=[pltpu.VMEM((1, d), F32), *ex_sems],
        compiler_params=_params(("arbitrary",)),
    )(dproj, w, xin, dres, g, scale, *ex_args)
    return (*out[:4], out[4:])


def _matmul_tn(a, b, colscale, rows_split, name, ex=None, a_cols=None):
    s, m = a.shape
    a_blk = 0
    if a_cols is not None:
        a_blk, m = a_cols
    n = b.shape[1]
    n_j, tn = (1, n) if rows_split else (NDEV, n // NDEV)
    fixed = (4 + 4 + 2 * 2) * m * tn
    tk = s
    while fixed + 2 * tk * (2 * m + b.dtype.itemsize * tn) > TN_VMEM_BUDGET:
        tk //= 2
    n_k = s // tk

    def body(a_ref, b_ref, cs_ref, o_ref, acc):
        k = pl.program_id(1)
        part = lax.dot_general(a_ref[...], b_ref[...].astype(BF16), TN_DIMS, preferred_element_type=F32)
        if n_k == 1:
            o_ref[...] = (part * cs_ref[...]).astype(BF16)
            return

        @pl.when(k == 0)
        def _():
            acc[...] = part

        @pl.when((k > 0) & (k < n_k - 1))
        def _():
            acc[...] += part

        @pl.when(k == n_k - 1)
        def _():
            o_ref[...] = ((acc[...] + part) * cs_ref[...]).astype(BF16)

    at = lambda j, k: (pl.program_id(0) == j) & (pl.program_id(1) == k)
    body = _carry(ex, body, 3, 1, lambda: at(0, 0), None, lambda: at(n_j - 1, n_k - 1))
    ex_args, ex_in, ex_shapes, ex_out, ex_sems = _carried(ex)
    out = pl.pallas_call(
        body, name=name, grid=(n_j, n_k),
        out_shape=(jax.ShapeDtypeStruct((n_j, m, tn), BF16), *ex_shapes),
        in_specs=[pl.BlockSpec((tk, m), lambda j, k: (k, a_blk)), pl.BlockSpec((tk, tn), lambda j, k: (k, j)),
                  pl.BlockSpec((1, tn), lambda j, k: (0, j)), *ex_in],
        out_specs=(pl.BlockSpec((None, m, tn), lambda j, k: (j, 0, 0)), *ex_out),
        scratch_shapes=[pltpu.VMEM((m, tn), F32), *ex_sems],
        compiler_params=_params(("arbitrary", "arbitrary")),
    )(a, b, colscale, *ex_args)
    return (out[0].reshape(NDEV, m // NDEV, n) if rows_split else out[0]), out[1:]


def _adam_update(w, g, m, v):
    m = ADAM_B1 * m + (1.0 - ADAM_B1) * g
    v = ADAM_B2 * v + (1.0 - ADAM_B2) * (g * g)
    m_hat = m / (1.0 - ADAM_B1 ** ADAM_STEP)
    v_hat = v / (1.0 - ADAM_B2 ** ADAM_STEP)
    return -ADAM_LR * (m_hat / (jnp.sqrt(v_hat) + ADAM_EPS) + ADAM_WD * w), m, v


def _adamw_reduce(parts, w, m, v, name, ex=None):
    n_l = len(parts)
    n_p, _, c = parts[0].shape
    rows = [p.shape[1] for p in parts]
    r = sum(rows)
    tr = min(min(rows), 128 if ex is not None else 256)
    n_i = r // tr
    tiles = [r_l // tr for r_l in rows]
    first_tile = [sum(tiles[:l]) for l in range(n_l)]

    def body(*refs):
        p_refs, (w_ref, m_ref, v_ref, g_out, d_out, m_out, v_out) = refs[:n_l], refs[n_l:]
        g = None
        for l, p_ref in enumerate(p_refs):
            g_l = p_ref[0].astype(F32)
            for j in range(1, n_p):
                g_l = g_l + p_ref[j].astype(F32)
            g = g_l if g is None else jnp.where(pl.program_id(0) >= first_tile[l], g_l, g)
        g_out[...] = g
        d_out[...], m_out[...], v_out[...] = _adam_update(w_ref[...], g, m_ref[...], v_ref[...])

    step = lambda k: (lambda: pl.program_id(0) == k)
    body = _carry(ex, body, n_l + 3, 4, step(0), [step(n_i // 3), step((2 * n_i) // 3)], step(n_i - 1))
    ex_args, ex_in, ex_shapes, ex_out, ex_sems = _carried(ex)
    blk = pl.BlockSpec((tr, c), lambda i: (i, 0))
    p_specs = [pl.BlockSpec((n_p, tr, c), lambda i, l=l: (0, jnp.clip(i - first_tile[l], 0, tiles[l] - 1), 0))
               for l in range(n_l)]
    shp = jax.ShapeDtypeStruct((r, c), F32)
    out = pl.pallas_call(
        body, name=name, grid=(n_i,), out_shape=(shp, shp, shp, shp, *ex_shapes),
        in_specs=[*p_specs, blk, blk, blk, *ex_in],
        out_specs=(blk, blk, blk, blk, *ex_out), scratch_shapes=ex_sems,
        compiler_params=_params(("arbitrary",)),
    )(*parts, w, m, v, *ex_args)
    return (*out[:4], out[4:])


def _adamw_small(gs, ws, ms, vs, name):
    n = len(gs)

    def body(*refs):
        ins, outs = refs[:4 * n], refs[4 * n:]
        for a in range(n):
            d, m, v = _adam_update(ins[n + a][...], ins[a][...], ins[2 * n + a][...], ins[3 * n + a][...])
            outs[a][...], outs[n + a][...], outs[2 * n + a][...] = d, m, v

    shapes = tuple(jax.ShapeDtypeStruct(w.shape, F32) for w in ws) * 3
    out = pl.pallas_call(
        body, name=name, out_shape=shapes,
        in_specs=[_vmem()] * (4 * n), out_specs=tuple([_vmem()] * (3 * n)),
        compiler_params=pltpu.CompilerParams(vmem_limit_bytes=VMEM_LIMIT),
    )(*gs, *ws, *ms, *vs)
    return out[:n], out[n:2 * n], out[2 * n:]


def _sum_devices(packed, name):
    n = len(packed)

    def body(*refs):
        for p_ref, o_ref in zip(refs[:n], refs[n:]):
            acc = p_ref[0].astype(F32)
            for j in range(1, NDEV):
                acc = acc + p_ref[j].astype(F32)
            o_ref[...] = acc

    return pl.pallas_call(
        body, name=name, out_shape=tuple(jax.ShapeDtypeStruct(p.shape[1:], F32) for p in packed),
        in_specs=[_vmem()] * n, out_specs=tuple([_vmem()] * n),
        compiler_params=pltpu.CompilerParams(vmem_limit_bytes=VMEM_LIMIT),
    )(*packed)


def _mod_w_grad(c_t, dmod, name):
    n_layers, _, w3 = dmod.shape
    d = c_t.shape[0]

    def body(c_ref, dm_ref, o_ref):
        for i in range(n_layers):
            acc = c_ref[:, 0:1] * dm_ref[i, 0:1, :]
            for b in range(1, NDEV):
                acc = acc + c_ref[:, b:b + 1] * dm_ref[i, b:b + 1, :]
            o_ref[i] = acc

    return pl.pallas_call(
        body, name=name, out_shape=jax.ShapeDtypeStruct((n_layers, d, w3), F32),
        in_specs=[_vmem(), _vmem()], out_specs=_vmem(),
        compiler_params=pltpu.CompilerParams(vmem_limit_bytes=VMEM_LIMIT),
    )(c_t, dmod)


def _mask_transpose_ws(w_s, name):
    def body(w_ref, wt_ref, wtt_ref):
        tril = (lax.broadcasted_iota(jnp.int32, (CHUNK, CHUNK), 0) >= lax.broadcasted_iota(jnp.int32, (CHUNK, CHUNK), 1))
        for g in range(GROUPS):
            wm = jnp.where(tril, w_ref[g], 0.0)
            wt_ref[g] = wm.astype(BF16)
            wtt_ref[g] = wm.T.astype(BF16)

    shp = jax.ShapeDtypeStruct(w_s.shape, BF16)
    return pl.pallas_call(
        body, name=name, out_shape=(shp, shp), in_specs=[_vmem()], out_specs=(_vmem(), _vmem()),
    )(w_s)


def _pack(pieces):
    flat = jnp.concatenate([p.reshape(-1) for p in pieces])
    rows = -(-flat.shape[0] // (8 * PACK_W)) * 8
    return jnp.pad(flat, (0, rows * PACK_W - flat.shape[0])).reshape(rows, PACK_W)


def _unpack(flat, shapes):
    out, off = [], 0
    for shp in shapes:
        size = 1
        for dim in shp:
            size *= dim
        out.append(flat[off:off + size].reshape(shp))
        off += size
    return out


def kernel(x, c, mod_w, mod_b, norm_g, a_w_in, a_conv_w, a_conv_b, a_w_out, b_w_in, b_ln_g, b_ln_b, b_w_s, b_b_s, b_w_out, final_g, loss_target, m_mod_w, m_mod_b, m_norm_g, m_a_w_in, m_a_conv_w, m_a_conv_b, m_a_w_out, m_b_w_in, m_b_ln_g, m_b_ln_b, m_b_w_s, m_b_b_s, m_b_w_out, m_final_g, v_mod_w, v_mod_b, v_norm_g, v_a_w_in, v_a_conv_w, v_a_conv_b, v_a_w_out, v_b_w_in, v_b_ln_g, v_b_ln_b, v_b_w_s, v_b_b_s, v_b_w_out, v_final_g):
    s, d = x.shape[1], x.shape[2]
    es = a_w_out.shape[1]
    e = NDEV * es
    w3 = mod_w.shape[2]
    me = _index(_pos())
    x0 = x.reshape(s, d)
    tgt = loss_target.reshape(s, d)

    small = jnp.concatenate([a_conv_w[0], b_ln_g, b_ln_b, jnp.zeros((3, es), F32)], axis=0)
    gather_a = _gather_exchange([a_w_in[0].astype(BF16), a_w_out[0].astype(BF16), small], [True, False, True])
    gather_b = _gather_exchange([b_w_in[0].astype(BF16), b_w_out[0].astype(BF16)], [True, False])
    mod, c_all, (wa, woa, small_all) = _mod_vectors(c, mod_w, mod_b, gather_a)
    conv_w, ln_g, ln_b = small_all[0:3], small_all[3:4], small_all[4:5]
    bsf = jnp.repeat(b_b_s[0].T, e // GROUPS, axis=1)
    wt, wtt = _mask_transpose_ws(b_w_s[0], "mask_w_s")
    shift0, scale0, gate0 = mod[0:1, 0:d], mod[0:1, d:2 * d], mod[0:1, 2 * d:]
    shift1, scale1, gate1 = mod[1:2, 0:d], mod[1:2, d:2 * d], mod[1:2, 2 * d:]
    g0, g1, fg = norm_g[0:1], norm_g[1:2], final_g.reshape(1, d)

    proj_a, h0, x1, br_a, conv_a, (wb, wob) = _layer_a_fwd(
        x0, g0, scale0, shift0, gate0, wa, conv_w, a_conv_b, woa, "a_fwd", gather_b)
    proj_b, h1, dx2, loss_acc, dfg, dgate1, v_b, dgv_b, _ = _layer_b_fwd_loss(
        x1, tgt, g1, scale1, shift1, gate1, fg, wb, ln_g, ln_b, wt, bsf, wob, "b_fwd_loss")

    dproj_b, y_b, dx1, dws, dbs, dlg, dlb, dshift1, dscale1, dg1 = _layer_b_bwd(
        proj_b, v_b, dgv_b, dx2, x1, gate1, g1, scale1, ln_g, ln_b, wt, wtt, bsf, wob, wb, "b_bwd")
    gs_b_out, _ = _matmul_tn(y_b, dx2, gate1, True, "b_w_out_grad")
    gs_b_in, (gr_b_out,) = _matmul_tn(h1, dproj_b, jnp.ones((1, dproj_b.shape[1]), F32), False, "b_w_in_grad",
                                      _scatter_exchange([gs_b_out]))
    dproj_a, y_a, dgate0, dcb, dcw, (gr_b_in,) = _conv_mixer_bwd(
        proj_a, dx1, br_a, conv_a, conv_w, gate0, woa, "a_mixer_bwd", _scatter_exchange([gs_b_in]))
    gs_a_out, _ = _matmul_tn(y_a, dx1, gate0, True, "a_w_out_grad")
    gs_a_in, (gr_a_out,) = _matmul_tn(h0, dproj_a, jnp.ones((1, dproj_a.shape[1]), F32), False, "a_w_in_grad",
                                      _scatter_exchange([gs_a_out]))
    send_sems, recv_sems, gs_thru, land_thru, token = _scatter_start(gs_a_in, "a_w_in_scatter_start")
    dx0, dshift0, dscale0, dg0, _ = _matmul_nt_norm_bwd(dproj_a, wa, x0, dx1, g0, scale0 + token[0:1, 0:1], "a_in_bwd")
    gs_a_in, landed = _scatter_wait(send_sems, recv_sems, gs_thru, land_thru, dshift0, "a_w_in_scatter_wait")
    gr_a_in = lax.dynamic_update_slice_in_dim(landed, lax.dynamic_slice_in_dim(gs_a_in, me, 1, axis=0), me, axis=0)

    def big(parts, w, m, v, name, ex=None):
        shp = w.shape
        r2 = lambda t_: t_.reshape(-1, shp[-1])
        g, dl, nm, nv, ex_out = _adamw_reduce(parts, r2(w), r2(m), r2(v), name, ex)
        return tuple(t_.reshape(shp) for t_ in (g, dl, nm, nv)), ex_out

    res = {}
    res["b_w_in"], _ = big([gr_b_in], b_w_in, m_b_w_in, v_b_w_in, "adamw_b_w_in")
    gr_a_in = [gr_a_in]

    pieces = [dshift0, dscale0, dgate0, dshift1, dscale1, dgate1, dg0, dg1, dcb, dcw[0:3], dlg, dlb, dfg,
              dbs[:, 0:GROUPS].T, loss_acc[0:1, 0:1]]
    shapes = [p.shape for p in pieces]
    res["a_w_in"], (packed_all, dws_all) = big(
        gr_a_in, a_w_in, m_a_w_in, v_a_w_in, "adamw_a_w_in",
        _gather_exchange([_pack(pieces), dws.astype(BF16).reshape(-1, PACK_W)], [False, False]))
    packed_all = packed_all.reshape(NDEV, -1, PACK_W)
    total, t_ws = _sum_devices([packed_all, dws_all.reshape(NDEV, -1, PACK_W)], "sum_small_grads")
    total, t_ws = total.reshape(-1), t_ws.reshape(dws.shape)
    (t_sh0, t_sc0, t_ga0, t_sh1, t_sc1, t_ga1, t_g0, t_g1, t_cb, t_cw, t_lg, t_lb, t_fg, t_bs, t_loss) = _unpack(
        total, shapes)
    loss = t_loss.reshape(())
    grad_mod_b = jnp.concatenate([jnp.concatenate([t_sh0, t_sc0, t_ga0], axis=1),
                                  jnp.concatenate([t_sh1, t_sc1, t_ga1], axis=1)], axis=0)
    grad_norm_g = jnp.concatenate([t_g0, t_g1], axis=0)
    dmod_all = packed_all.reshape(NDEV, -1)[:, 0:6 * d].reshape(NDEV, 2, 3 * d).transpose(1, 0, 2)
    dmod_mine = lax.dynamic_slice_in_dim(dmod_all, me * w3, w3, axis=2)
    grad_mod_w = _mod_w_grad(c_all.T, dmod_mine, "mod_w_grad")
    grad_a_conv_w = lax.dynamic_slice_in_dim(t_cw, me * es, es, axis=1)
    grad_b_ln_g = lax.dynamic_slice_in_dim(t_lg, me * es, es, axis=1)
    grad_b_ln_b = lax.dynamic_slice_in_dim(t_lb, me * es, es, axis=1)

    res["mod_w"], _ = big([grad_mod_w.reshape(1, -1, w3)], mod_w, m_mod_w, v_mod_w, "adamw_mod_w")
    res["a_w_out"], _ = big([gr_a_out], a_w_out, m_a_w_out, v_a_w_out, "adamw_a_w_out")
    res["b_w_out"], _ = big([gr_b_out], b_w_out, m_b_w_out, v_b_w_out, "adamw_b_w_out")

    small_names = ["mod_b", "norm_g", "a_conv_w", "a_conv_b", "b_ln_g", "b_ln_b", "b_w_s", "b_b_s", "final_g"]
    small_g = [grad_mod_b, grad_norm_g, grad_a_conv_w, t_cb, grad_b_ln_g, grad_b_ln_b, t_ws, t_bs, t_fg]
    small_w = [mod_b, norm_g, a_conv_w, a_conv_b, b_ln_g, b_ln_b, b_w_s, b_b_s, final_g]
    small_m = [m_mod_b, m_norm_g, m_a_conv_w, m_a_conv_b, m_b_ln_g, m_b_ln_b, m_b_w_s, m_b_b_s, m_final_g]
    small_v = [v_mod_b, v_norm_g, v_a_conv_w, v_a_conv_b, v_b_ln_g, v_b_ln_b, v_b_w_s, v_b_b_s, v_final_g]
    as2d = lambda t_: t_.reshape(-1, t_.shape[-1])
    dls, nms, nvs = _adamw_small([as2d(t_) for t_ in small_g], [as2d(t_) for t_ in small_w],
                                 [as2d(t_) for t_ in small_m], [as2d(t_) for t_ in small_v], "adamw_small")
    for a, nme in enumerate(small_names):
        shp = small_w[a].shape
        res[nme] = (small_g[a].reshape(shp), dls[a].reshape(shp), nms[a].reshape(shp), nvs[a].reshape(shp))

    order = ["mod_w", "mod_b", "norm_g", "a_w_in", "a_conv_w", "a_conv_b", "a_w_out", "b_w_in", "b_ln_g", "b_ln_b",
             "b_w_s", "b_b_s", "b_w_out", "final_g"]
    return (loss, dx0.reshape(x.shape), *[res[k][0] for k in order], *[res[k][1] for k in order],
            *[res[k][2] for k in order], *[res[k][3] for k in order])
```

```python
import functools

import jax
import jax.numpy as jnp
from jax import lax
from jax.experimental import pallas as pl
from jax.experimental.pallas import tpu as pltpu

NDEV = 8
CHUNK = 128
GROUPS = 8
RMS_EPS = 1e-6
LN_EPS = 1e-5
ADAM_LR, ADAM_B1, ADAM_B2, ADAM_EPS, ADAM_WD, ADAM_STEP = 0.001, 0.9, 0.999, 1e-08, 0.01, 10
V7X_VMEM_BYTES = 64 * 1024 * 1024
VMEM_LIMIT = V7X_VMEM_BYTES - 8 * 1024 * 1024
TN_VMEM_BUDGET = 46 * 1024 * 1024
PACK_W = 1024
F32, BF16 = jnp.float32, jnp.bfloat16
MESH = pl.DeviceIdType.MESH
RSQRT2 = 0.7071067811865476
INV_SQRT_2PI = 0.3989422804014327
NT_DIMS = (((1,), (1,)), ((), ()))
TN_DIMS = (((0,), (0,)), ((), ()))


def _params(sem=None):
    return pltpu.CompilerParams(dimension_semantics=sem, vmem_limit_bytes=VMEM_LIMIT)


def _vmem():
    return pl.BlockSpec(memory_space=pltpu.VMEM)


def _hbm():
    return pl.BlockSpec(memory_space=pltpu.HBM)


def _full(shape):
    return pl.BlockSpec(shape, lambda *_: (0,) * len(shape))


def _pos():
    return lax.axis_index("x"), lax.axis_index("y"), lax.axis_index("c")


def _index(p):
    return 4 * p[0] + 2 * p[1] + p[2]


def _peer(k):
    x, y, c = _pos()
    return ((1 - x) if (k >> 2) & 1 else x, (1 - y) if (k >> 1) & 1 else y, (1 - c) if k & 1 else c)


def _silu(z):
    sg = jax.nn.sigmoid(z)
    return z * sg, sg * (1.0 + z * (1.0 - sg))


def _gelu(v):
    phi = 0.5 * (1.0 + lax.erf(v * RSQRT2))
    return v * phi, phi + v * (jnp.exp(-0.5 * v * v) * INV_SQRT_2PI)


def _colsum(v):
    return jnp.sum(v, axis=0, keepdims=True)


def _rowsum(v):
    return jnp.sum(v, axis=-1, keepdims=True)


def _gather_all_vmem(slab_ref, send_sems, recv_sems, base):
    me = _index(_pos())
    sends = []
    for k in range(1, NDEV):
        cp = pltpu.make_async_remote_copy(
            src_ref=slab_ref.at[me], dst_ref=slab_ref.at[me],
            send_sem=send_sems.at[base + k - 1], recv_sem=recv_sems.at[base + k - 1],
            device_id=_peer(k), device_id_type=MESH)
        cp.start()
        sends.append(cp)
    for k in range(1, NDEV):
        src = _index(_peer(k))
        pltpu.make_async_remote_copy(
            src_ref=slab_ref.at[src], dst_ref=slab_ref.at[src],
            send_sem=send_sems.at[base + k - 1], recv_sem=recv_sems.at[base + k - 1],
            device_id=_peer(k), device_id_type=MESH).wait_recv()
    for cp in sends:
        cp.wait_send()


def _mod_vectors(c, mod_w, mod_b, ex):
    n_layers, d, w3 = mod_w.shape
    r_in, r_out = len(ex.arrays), len(ex.out_shapes)

    def body(*refs):
        c_ref, mw_ref, mb_ref = refs[:3]
        ex_ins = refs[3:3 + r_in]
        mod_ref, call_ref = refs[3 + r_in:5 + r_in]
        ex_outs = refs[5 + r_in:5 + r_in + r_out]
        cslab, pslab, send_sems, recv_sems = refs[5 + r_in + r_out:9 + r_in + r_out]
        ex_sems = refs[9 + r_in + r_out:]
        ex.start(ex_ins, ex_outs, ex_sems)
        me = _index(_pos())
        cv = c_ref[...]
        cslab[me] = jnp.broadcast_to(cv * jax.nn.sigmoid(cv), (8, d))
        _gather_all_vmem(cslab, send_sems, recv_sems, 0)
        c_all = jnp.concatenate([cslab[k, 0:1, :] for k in range(NDEV)], axis=0)
        call_ref[...] = c_all
        for i in range(n_layers):
            pslab[me, i * NDEV:(i + 1) * NDEV, :] = jnp.dot(
                c_all, mw_ref[i], preferred_element_type=F32, precision=lax.Precision.HIGHEST)
        _gather_all_vmem(pslab, send_sems, recv_sems, NDEV - 1)
        for i in range(n_layers):
            for k in range(NDEV):
                mod_ref[i:i + 1, k * w3:(k + 1) * w3] = (
                    pslab[k, pl.ds(i * NDEV + me, 1), :] + mb_ref[i:i + 1, k * w3:(k + 1) * w3])
        for passing_on in ex.middles:
            passing_on(ex_ins, ex_outs, ex_sems)
        ex.finish(ex_ins, ex_outs, ex_sems)

    out = pl.pallas_call(
        body, name="mod_vectors",
        out_shape=(jax.ShapeDtypeStruct((n_layers, 3 * d), F32), jax.ShapeDtypeStruct((NDEV, d), F32), *ex.out_shapes),
        in_specs=[_vmem(), _vmem(), _vmem()] + [_hbm()] * r_in, out_specs=(_vmem(), _vmem(), *([_hbm()] * r_out)),
        scratch_shapes=[pltpu.VMEM((NDEV, 8, d), F32), pltpu.VMEM((NDEV, n_layers * NDEV, w3), F32),
                        pltpu.SemaphoreType.DMA((2 * (NDEV - 1),)), pltpu.SemaphoreType.DMA((2 * (NDEV - 1),)), *ex.sems],
        compiler_params=pltpu.CompilerParams(vmem_limit_bytes=VMEM_LIMIT),
    )(c, mod_w, mod_b, *ex.arrays)
    return out[0], out[1], out[2:]


class _Exchange:
    def __init__(self, arrays, out_shapes, sems, start, middles, finish):
        self.arrays, self.out_shapes, self.sems = list(arrays), list(out_shapes), list(sems)
        self.start, self.middles, self.finish = start, list(middles), finish


def _gather_exchange(shards, by_cols):
    n = len(shards)
    shapes = [sh.shape for sh in shards]

    def tools(ins, outs, sems):
        send_sems, recv_sems, local_sems = sems
        x, y, c = _pos()
        chips = [(1 - x, y), (x, 1 - y), (1 - x, 1 - y)]
        south = c == 0
        relayed = (jnp.where(south, 1 - x, x), jnp.where(south, y, 1 - y), c)
        relay_to = (jnp.where(south, x, 1 - x), jnp.where(south, 1 - y, y), c)

        def place(a, block):
            r, cc = shapes[a]
            if by_cols[a]:
                return outs[a].at[:, pl.ds(_index(block) * cc, cc)]
            return outs[a].at[pl.ds(_index(block) * r, r), :]

        def copy(a, k, block, to, src=None):
            dst = place(a, block)
            return pltpu.make_async_remote_copy(
                src_ref=dst if src is None else src, dst_ref=dst,
                send_sem=send_sems.at[a * 7 + k], recv_sem=recv_sems.at[a * 7 + k],
                device_id=to, device_id_type=MESH)

        mine = [pltpu.make_async_copy(ins[a], place(a, (x, y, c)), local_sems.at[a]) for a in range(n)]
        first = []
        for a in range(n):
            first.append(copy(a, 0, (x, y, c), (x, y, 1 - c), src=ins[a]))
            first += [copy(a, 1 + j, (x, y, c), (*chip, c), src=ins[a]) for j, chip in enumerate(chips[:2])]
        relays = [copy(a, 3, relayed, relay_to) for a in range(n)]
        passed = [copy(a, 4 + j, (*chip, c), (x, y, 1 - c)) for j, chip in enumerate(chips) for a in range(n)]
        return (x, y, c), chips, copy, mine, first, relays, passed

    def start(ins, outs, sems):
        _, _, _, mine, first, _, _ = tools(ins, outs, sems)
        for cp in mine + first:
            cp.start()

    def pass_neighbours(ins, outs, sems):
        (x, y, c), chips, copy, _, _, relays, passed = tools(ins, outs, sems)
        for j, chip in enumerate(chips[:2]):
            for a in range(n):
                copy(a, 1 + j, (*chip, c), (x, y, c)).wait_recv()
                passed[j * n + a].start()
        for cp in relays:
            cp.start()

    def pass_diagonal(ins, outs, sems):
        (x, y, c), chips, copy, _, _, _, passed = tools(ins, outs, sems)
        for a in range(n):
            copy(a, 3, (*chips[2], c), (x, y, c)).wait_recv()
            passed[2 * n + a].start()

    def finish(ins, outs, sems):
        (x, y, c), chips, copy, mine, first, relays, passed = tools(ins, outs, sems)
        for a in range(n):
            copy(a, 0, (x, y, 1 - c), (x, y, c)).wait_recv()
        for j, chip in enumerate(chips):
            for a in range(n):
                copy(a, 4 + j, (*chip, 1 - c), (x, y, c)).wait_recv()
        for cp in first + relays + passed:
            cp.wait_send()
        for cp in mine:
            cp.wait()

    out_shapes = [jax.ShapeDtypeStruct((r, NDEV * cc) if bc else (NDEV * r, cc), sh.dtype)
                  for (r, cc), bc, sh in zip(shapes, by_cols, shards)]
    sems = [pltpu.SemaphoreType.DMA((7 * n,)), pltpu.SemaphoreType.DMA((7 * n,)), pltpu.SemaphoreType.DMA((n,))]
    return _Exchange(shards, out_shapes, sems, start, [pass_neighbours, pass_diagonal], finish)


def _scatter_exchange(parts):
    n = len(parts)

    def tools(ins, outs, sems):
        send_sems, recv_sems, local_sems = sems
        me = _index(_pos())
        mine = [pltpu.make_async_copy(ins[a].at[me], outs[a].at[me], local_sems.at[a]) for a in range(n)]
        sends, arrivals = [], []
        for k in range(1, NDEV):
            peer = _peer(k)
            for a in range(n):
                pair = dict(send_sem=send_sems.at[a * 7 + k - 1], recv_sem=recv_sems.at[a * 7 + k - 1],
                            device_id=peer, device_id_type=MESH)
                sends.append(pltpu.make_async_remote_copy(src_ref=ins[a].at[_index(peer)], dst_ref=outs[a].at[me], **pair))
                slot = outs[a].at[_index(peer)]
                arrivals.append(pltpu.make_async_remote_copy(src_ref=slot, dst_ref=slot, **pair))
        return mine, sends, arrivals

    def start(ins, outs, sems):
        mine, sends, _ = tools(ins, outs, sems)
        for cp in mine + sends:
            cp.start()

    def finish(ins, outs, sems):
        mine, sends, arrivals = tools(ins, outs, sems)
        for cp in arrivals:
            cp.wait_recv()
        for cp in sends:
            cp.wait_send()
        for cp in mine:
            cp.wait()

    out_shapes = [jax.ShapeDtypeStruct(p.shape, p.dtype) for p in parts]
    sems = [pltpu.SemaphoreType.DMA((7 * n,)), pltpu.SemaphoreType.DMA((7 * n,)), pltpu.SemaphoreType.DMA((n,))]
    return _Exchange(parts, out_shapes, sems, start, [], finish)


def _scatter_start(part, name):
    def body(part_ref, land_ref, send_sems, recv_sems, part_thru, land_thru, token):
        me = _index(_pos())
        for k in range(1, NDEV):
            peer = _peer(k)
            pltpu.make_async_remote_copy(
                src_ref=part_ref.at[_index(peer)], dst_ref=land_ref.at[me],
                send_sem=send_sems.at[k - 1], recv_sem=recv_sems.at[k - 1],
                device_id=peer, device_id_type=MESH).start()
        token[...] = jnp.zeros_like(token)

    sem = pl.BlockSpec(memory_space=pltpu.SEMAPHORE)
    return pl.pallas_call(
        body, name=name,
        out_shape=(pltpu.SemaphoreType.DMA((NDEV - 1,)), pltpu.SemaphoreType.DMA((NDEV - 1,)),
                   pltpu.HBM(part.shape, part.dtype), pltpu.HBM(part.shape, part.dtype),
                   jax.ShapeDtypeStruct((8, 128), F32)),
        in_specs=(_hbm(), _hbm()), out_specs=(sem, sem, _hbm(), _hbm(), _vmem()),
        input_output_aliases={0: 2, 1: 3},
        compiler_params=pltpu.CompilerParams(has_side_effects=pltpu.SideEffectType.DATAFLOW_SIDE_EFFECTING),
    )(pltpu.with_memory_space_constraint(part, pltpu.HBM),
      pltpu.with_memory_space_constraint(lax.empty(part.shape, part.dtype), pltpu.HBM))


def _scatter_wait(send_sems, recv_sems, part_thru, land_thru, after, name):
    def body(part_ref, land_ref, send_sems, recv_sems, after_ref, part_dead, got_ref):
        for k in range(1, NDEV):
            peer = _peer(k)
            slot = land_ref.at[_index(peer)]
            copy = pltpu.make_async_remote_copy(
                src_ref=part_ref.at[_index(peer)], dst_ref=slot,
                send_sem=send_sems.at[k - 1], recv_sem=recv_sems.at[k - 1],
                device_id=peer, device_id_type=MESH)
            copy.wait_send()
            copy.wait_recv()

    sem = pl.BlockSpec(memory_space=pltpu.SEMAPHORE)
    return pl.pallas_call(
        body, name=name,
        out_shape=(pltpu.HBM(part_thru.shape, part_thru.dtype), pltpu.HBM(land_thru.shape, land_thru.dtype)),
        in_specs=(_hbm(), _hbm(), sem, sem, pl.BlockSpec(memory_space=pl.ANY)), out_specs=(_hbm(), _hbm()),
        input_output_aliases={0: 0, 1: 1},
        compiler_params=pltpu.CompilerParams(has_side_effects=pltpu.SideEffectType.DATAFLOW_SIDE_EFFECTING),
    )(part_thru, land_thru, send_sems, recv_sems, after)


def _gather_start(blocks, name):
    n = len(blocks)

    def body(*refs):
        b_refs, l_refs = refs[:n], refs[n:2 * n]
        send_sems, recv_sems = refs[2 * n:2 * n + 2]
        token, local_sems = refs[-2], refs[-1]
        me = _index(_pos())
        mine = [pltpu.make_async_copy(b_refs[a], l_refs[a].at[me], local_sems.at[a]) for a in range(n)]
        for cp in mine:
            cp.start()
        for k in range(1, NDEV):
            peer = _peer(k)
            for a in range(n):
                pltpu.make_async_remote_copy(
                    src_ref=b_refs[a], dst_ref=l_refs[a].at[me],
                    send_sem=send_sems.at[a * (NDEV - 1) + k - 1], recv_sem=recv_sems.at[a * (NDEV - 1) + k - 1],
                    device_id=peer, device_id_type=MESH).start()
        token[...] = jnp.zeros_like(token)
        for cp in mine:
            cp.wait()

    sem = pl.BlockSpec(memory_space=pltpu.SEMAPHORE)
    lands = [lax.empty((NDEV, *b.shape), b.dtype) for b in blocks]
    out = pl.pallas_call(
        body, name=name,
        out_shape=(pltpu.SemaphoreType.DMA((n * (NDEV - 1),)), pltpu.SemaphoreType.DMA((n * (NDEV - 1),)),
                   *[pltpu.HBM(b.shape, b.dtype) for b in blocks], *[pltpu.HBM(l.shape, l.dtype) for l in lands],
                   jax.ShapeDtypeStruct((8, 128), F32)),
        in_specs=tuple([_hbm()] * (2 * n)), out_specs=(sem, sem, *([_hbm()] * (2 * n)), _vmem()),
        input_output_aliases={a: 2 + a for a in range(2 * n)},
        scratch_shapes=[pltpu.SemaphoreType.DMA((n,))],
        compiler_params=pltpu.CompilerParams(has_side_effects=pltpu.SideEffectType.DATAFLOW_SIDE_EFFECTING),
    )(*[pltpu.with_memory_space_constraint(b, pltpu.HBM) for b in blocks],
      *[pltpu.with_memory_space_constraint(l, pltpu.HBM) for l in lands])
    return out[0], out[1], out[2:2 + n], out[2 + n:2 + 2 * n], out[-1]


def _gather_wait(send_sems, recv_sems, blocks_thru, lands_thru, after, name):
    n = len(blocks_thru)

    def body(*refs):
        b_refs, l_refs = refs[:n], refs[n:2 * n]
        send_sems, recv_sems = refs[2 * n:2 * n + 2]
        for k in range(1, NDEV):
            peer = _peer(k)
            for a in range(n):
                copy = pltpu.make_async_remote_copy(
                    src_ref=b_refs[a], dst_ref=l_refs[a].at[_index(peer)],
                    send_sem=send_sems.at[a * (NDEV - 1) + k - 1], recv_sem=recv_sems.at[a * (NDEV - 1) + k - 1],
                    device_id=peer, device_id_type=MESH)
                copy.wait_send()
                copy.wait_recv()

    sem = pl.BlockSpec(memory_space=pltpu.SEMAPHORE)
    out = pl.pallas_call(
        body, name=name,
        out_shape=tuple(pltpu.HBM(l.shape, l.dtype) for l in lands_thru),
        in_specs=(*([_hbm()] * (2 * n)), sem, sem, *([pl.BlockSpec(memory_space=pl.ANY)] * len(after))),
        out_specs=tuple([_hbm()] * n),
        input_output_aliases={n + a: a for a in range(n)},
        compiler_params=pltpu.CompilerParams(has_side_effects=pltpu.SideEffectType.DATAFLOW_SIDE_EFFECTING),
    )(*blocks_thru, *lands_thru, send_sems, recv_sems, *after)
    return list(out)


def _carry(ex, body, n_in, n_out, first, middle, last):
    if ex is None:
        return body
    r_in, r_out = len(ex.arrays), len(ex.out_shapes)

    def wrapped(*refs):
        ins, rins = refs[:n_in], refs[n_in:n_in + r_in]
        outs = refs[n_in + r_in:n_in + r_in + n_out]
        routs = refs[n_in + r_in + n_out:n_in + r_in + n_out + r_out]
        rest = refs[n_in + r_in + n_out + r_out:]
        scratch, sems = rest[:len(rest) - len(ex.sems)], rest[len(rest) - len(ex.sems):]

        @pl.when(first())
        def _():
            ex.start(rins, routs, sems)

        for passing_on, at_step in zip(ex.middles, middle or []):
            pl.when(at_step())(functools.partial(passing_on, rins, routs, sems))

        body(*ins, *outs, *scratch)

        @pl.when(last())
        def _():
            ex.finish(rins, routs, sems)

    return wrapped


def _carried(ex):
    if ex is None:
        return [], [], [], [], []
    return ex.arrays, [_hbm()] * len(ex.arrays), ex.out_shapes, [_hbm()] * len(ex.out_shapes), ex.sems


def _resident(shape):
    return pl.BlockSpec(shape, lambda *_: (0,) * len(shape), pipeline_mode=pl.Buffered(1))


def _norm_modulate(x_ref, g_ref, sc_ref, sh_ref):
    xv = x_ref[...]
    r = lax.rsqrt(jnp.mean(xv * xv, axis=-1, keepdims=True) + RMS_EPS)
    return ((xv * r) * g_ref[...] * (1.0 + sc_ref[...]) + sh_ref[...]).astype(BF16)


def _conv_taps(cx, t6, t7, row):
    p1 = jnp.where(row == 0, t7, pltpu.roll(cx, 1, 0))
    p2 = jnp.where(row == 0, t6, jnp.where(row == 1, t7, pltpu.roll(cx, 2, 0)))
    return p1, p2


def _layer_a_fwd(x, g, scale, shift, gate, wi, cw, cb, wo, name, ex=None):
    s, d = x.shape
    e = wo.shape[0]
    t = min(s, 256)
    n_t = s // t
    cwid = min(e, 512)

    def body(x_ref, g_ref, sc_ref, sh_ref, gate_ref, wi_ref, cw_ref, cb_ref, wo_ref,
             proj_ref, h_ref, x1_ref, br_ref, conv_ref, y_scr, tail_scr):
        @pl.when(pl.program_id(0) == 0)
        def _():
            tail_scr[...] = jnp.zeros_like(tail_scr)
        h_ref[...] = _norm_modulate(x_ref, g_ref, sc_ref, sh_ref)
        row = lax.broadcasted_iota(jnp.int32, (t, cwid), 0)

        def project(c0):
            v = jnp.dot(h_ref[...], wi_ref[:, c0:c0 + cwid], preferred_element_type=F32)
            proj_ref[:, c0:c0 + cwid] = v.astype(BF16)
            return v

        for c0 in range(0, e, cwid):
            sl = slice(c0, c0 + cwid)
            bg, z = project(c0), project(3 * e + c0)
            cx = project(e + c0) * project(2 * e + c0)
            p1, p2 = _conv_taps(cx, tail_scr[6:7, sl], tail_scr[7:8, sl], row)
            conv = cb_ref[:, sl] + cw_ref[2:3, sl] * cx + cw_ref[0:1, sl] * p2 + cw_ref[1:2, sl] * p1
            conv_ref[:, sl] = conv.astype(BF16)
            y_scr[:, sl] = (_silu(z)[0] * bg * conv).astype(BF16)
            tail_scr[:, sl] = cx[t - 8:t, :]
        br = jnp.dot(y_scr[...], wo_ref[...], preferred_element_type=F32)
        x1_ref[...] = x_ref[...] + gate_ref[...] * br
        br_ref[...] = br.astype(BF16)

    step = lambda k: (lambda: pl.program_id(0) == k)
    body = _carry(ex, body, 9, 5, step(0), [step(n_t // 3), step((2 * n_t) // 3)], step(n_t - 1))
    ex_args, ex_in, ex_shapes, ex_out, ex_sems = _carried(ex)
    tok = pl.BlockSpec((t, d), lambda i: (i, 0))
    out = pl.pallas_call(
        body, name=name, grid=(n_t,),
        out_shape=(jax.ShapeDtypeStruct((s, 4 * e), BF16), jax.ShapeDtypeStruct((s, d), BF16),
                   jax.ShapeDtypeStruct((s, d), F32), jax.ShapeDtypeStruct((s, d), BF16),
                   jax.ShapeDtypeStruct((s, e), BF16), *ex_shapes),
        in_specs=[tok, _full((1, d)), _full((1, d)), _full((1, d)), _full((1, d)), _resident((d, 4 * e)),
                  _full((3, e)), _full((1, e)), _resident((e, d)), *ex_in],
        out_specs=(pl.BlockSpec((t, 4 * e), lambda i: (i, 0)), tok, tok, tok,
                   pl.BlockSpec((t, e), lambda i: (i, 0)), *ex_out),
        scratch_shapes=[pltpu.VMEM((t, e), BF16), pltpu.VMEM((8, e), F32), *ex_sems],
        compiler_params=_params(("arbitrary",)),
    )(x, g, scale, shift, gate, wi, cw, cb, wo, *ex_args)
    return (*out[:5], out[5:])


def _ln_stats(v_of, v_scr, t, e):
    gw = e // GROUPS
    s1 = jnp.zeros((t, 1), F32)
    for g in range(GROUPS):
        v = v_of(g)
        v_scr[:, g * gw:(g + 1) * gw] = v
        s1 = s1 + _rowsum(v)
    mu = s1 * (1.0 / e)
    s2 = jnp.zeros((t, 1), F32)
    for g in range(GROUPS):
        dv = v_scr[:, g * gw:(g + 1) * gw] - mu
        s2 = s2 + _rowsum(dv * dv)
    return mu, lax.rsqrt(s2 * (1.0 / e) + LN_EPS)


def _layer_b_fwd_loss(x1, tgt, g1, scale, shift, gate, fg, wi, lng, lnb, wt, bsf, wo, name, ex=None):
    s, d = x1.shape
    e = wo.shape[0]
    gw = e // GROUPS
    t = min(s, 256)
    n_t = s // t

    def body(x1_ref, tgt_ref, g_ref, sc_ref, sh_ref, gate_ref, fg_ref, wi_ref, lng_ref, lnb_ref, wt_ref, bsf_ref, wo_ref,
             proj_ref, h_ref, dx2_ref, loss_ref, dfg_ref, dgate_ref, v_ref, dgv_ref, v_scr, y_scr):
        @pl.when(pl.program_id(0) == 0)
        def _():
            loss_ref[...] = jnp.zeros_like(loss_ref)
            dfg_ref[...] = jnp.zeros_like(dfg_ref)
            dgate_ref[...] = jnp.zeros_like(dgate_ref)
        h_ref[...] = _norm_modulate(x1_ref, g_ref, sc_ref, sh_ref)

        def project(c0):
            v = jnp.dot(h_ref[...], wi_ref[:, c0:c0 + gw], preferred_element_type=F32)
            proj_ref[:, c0:c0 + gw] = v.astype(BF16)
            return v

        def gelu_v(g):
            gs = slice(g * gw, (g + 1) * gw)
            v, dgv = _gelu(project(e + g * gw))
            v_ref[:, gs] = v.astype(BF16)
            dgv_ref[:, gs] = dgv.astype(BF16)
            return v

        mu, rs = _ln_stats(gelu_v, v_scr, t, e)
        for g in range(GROUPS):
            gs = slice(g * gw, (g + 1) * gw)
            vn = (((v_scr[:, gs] - mu) * rs) * lng_ref[:, gs] + lnb_ref[:, gs]).astype(BF16)
            u = _gelu(project(g * gw))[0]
            sz = _silu(project(2 * e + g * gw))[0]
            for ch in range(t // CHUNK):
                rows = slice(ch * CHUNK, (ch + 1) * CHUNK)
                mixed = jnp.dot(wt_ref[g], vn[rows], preferred_element_type=F32) + bsf_ref[:, gs]
                y_scr[rows, gs] = (sz[rows] * (u[rows] * mixed)).astype(BF16)
        br = jnp.dot(y_scr[...], wo_ref[...], preferred_element_type=F32)
        x2 = x1_ref[...] + gate_ref[...] * br
        r2 = lax.rsqrt(jnp.mean(x2 * x2, axis=-1, keepdims=True) + RMS_EPS)
        xn = x2 * r2
        diff = xn * fg_ref[...] - tgt_ref[...]
        loss_ref[...] += jnp.broadcast_to(0.5 * _colsum(jnp.mean(diff * diff, axis=-1, keepdims=True)), loss_ref.shape)
        dout = diff * (1.0 / d)
        dfg_ref[...] += _colsum(dout * xn)
        dxn = dout * fg_ref[...]
        dx2 = r2 * (dxn - xn * jnp.mean(dxn * xn, axis=-1, keepdims=True))
        dx2_ref[...] = dx2
        dgate_ref[...] += _colsum(dx2 * br)

    step = lambda k: (lambda: pl.program_id(0) == k)
    body = _carry(ex, body, 13, 8, step(0), [step(n_t // 3), step((2 * n_t) // 3)], step(n_t - 1))
    ex_args, ex_in, ex_shapes, ex_out, ex_sems = _carried(ex)
    tok = pl.BlockSpec((t, d), lambda i: (i, 0))
    vec = _full((1, d))
    out = pl.pallas_call(
        body, name=name, grid=(n_t,),
        out_shape=(jax.ShapeDtypeStruct((s, 3 * e), BF16), jax.ShapeDtypeStruct((s, d), BF16),
                   jax.ShapeDtypeStruct((s, d), F32), jax.ShapeDtypeStruct((8, 128), F32),
                   jax.ShapeDtypeStruct((1, d), F32), jax.ShapeDtypeStruct((1, d), F32),
                   jax.ShapeDtypeStruct((s, e), BF16), jax.ShapeDtypeStruct((s, e), BF16), *ex_shapes),
        in_specs=[tok, tok, vec, vec, vec, vec, vec, _resident((d, 3 * e)), _full((1, e)), _full((1, e)),
                  _full((GROUPS, CHUNK, CHUNK)), _resident((CHUNK, e)), _resident((e, d)), *ex_in],
        out_specs=(pl.BlockSpec((t, 3 * e), lambda i: (i, 0)), tok, tok, _full((8, 128)), vec, vec,
                   pl.BlockSpec((t, e), lambda i: (i, 0)), pl.BlockSpec((t, e), lambda i: (i, 0)), *ex_out),
        scratch_shapes=[pltpu.VMEM((t, e), F32), pltpu.VMEM((t, e), BF16), *ex_sems],
        compiler_params=_params(("arbitrary",)),
    )(x1, tgt, g1, scale, shift, gate, fg, wi, lng, lnb, wt, bsf, wo, *ex_args)
    return (*out[:8], out[8:])


def _norm_modulate_bwd(dh, x_ref, dres_ref, g_ref, sc_ref, dx_ref, dsh_ref, p_scr):
    xv = x_ref[...]
    r = lax.rsqrt(jnp.mean(xv * xv, axis=-1, keepdims=True) + RMS_EPS)
    xn = xv * r
    dsh_ref[...] += _colsum(dh)
    p_scr[...] += _colsum(dh * xn)
    dxn = dh * (g_ref[...] * (1.0 + sc_ref[...]))
    dx_ref[...] = r * (dxn - xn * jnp.mean(dxn * xn, axis=-1, keepdims=True)) + dres_ref[...]


def _layer_b_bwd(proj, v_act, dgv, dx2, x1, gate, g1, scale, lng, lnb, wt, wtt, bsf, wo, wi, name):
    s, e3 = proj.shape
    e = e3 // 3
    d = dx2.shape[1]
    gw = e // GROUPS
    t = min(s, 256)
    n_t = s // t

    def body(pu_ref, pz_ref, v_ref, dgv_ref, dx_ref, x1_ref, gate_ref, g_ref, sc_ref, lng_ref, lnb_ref, wt_ref, wtt_ref,
             bsf_ref, wo_ref, wi_ref,
             dp_ref, y_ref, dx1_ref, dws_ref, dbs_ref, dlg_ref, dlb_ref, dsh_ref, dsc_ref, dg_ref,
             v_scr, dbr_scr, dvn_scr, dbs_scr, p_scr, dy_scr, vn_scr, mixed_scr):
        @pl.when(pl.program_id(0) == 0)
        def _():
            dws_ref[...] = jnp.zeros_like(dws_ref)
            dlg_ref[...] = jnp.zeros_like(dlg_ref)
            dlb_ref[...] = jnp.zeros_like(dlb_ref)
            dsh_ref[...] = jnp.zeros_like(dsh_ref)
            dbs_scr[...] = jnp.zeros_like(dbs_scr)
            p_scr[...] = jnp.zeros_like(p_scr)
        dbr_scr[...] = (dx_ref[...] * gate_ref[...]).astype(BF16)
        mu, rs = _ln_stats(lambda g: v_ref[:, g * gw:(g + 1) * gw].astype(F32), v_scr, t, e)
        tril = (lax.broadcasted_iota(jnp.int32, (CHUNK, CHUNK), 0) >= lax.broadcasted_iota(jnp.int32, (CHUNK, CHUNK), 1))
        c1 = jnp.zeros((t, 1), F32)
        c2 = jnp.zeros((t, 1), F32)
        span = 2
        kw = span * gw
        dh = jnp.zeros((t, d), F32)

        def through_w_in(c0):
            return lax.dot_general(dp_ref[:, c0:c0 + kw], wi_ref[:, c0:c0 + kw], NT_DIMS, preferred_element_type=F32)

        dy_scr[...] = lax.dot_general(dbr_scr[...], wo_ref[...], NT_DIMS, preferred_element_type=F32)
        for g in range(GROUPS):
            gs = slice(g * gw, (g + 1) * gw)
            vhat = (v_scr[:, gs] - mu) * rs
            v_scr[:, gs] = vhat
            vn = (vhat * lng_ref[:, gs] + lnb_ref[:, gs]).astype(BF16)
            vn_scr[:, gs] = vn
            for ch in range(t // CHUNK):
                rows = slice(ch * CHUNK, (ch + 1) * CHUNK)
                mixed_scr[rows, gs] = jnp.dot(wt_ref[g], vn[rows], preferred_element_type=F32) + bsf_ref[:, gs]
        for g in range(GROUPS):
            gs = slice(g * gw, (g + 1) * gw)
            vhat = v_scr[:, gs]
            lg = lng_ref[:, gs]
            for ch in range(t // CHUNK):
                rows = slice(ch * CHUNK, (ch + 1) * CHUNK)
                mixed = mixed_scr[rows, gs]
                u, dgu = _gelu(pu_ref[rows, gs].astype(F32))
                sz, dsz = _silu(pz_ref[rows, gs].astype(F32))
                sgate = u * mixed
                y_ref[rows, gs] = (sz * sgate).astype(BF16)
                dy = dy_scr[rows, gs]
                dp_ref[rows, 2 * e + g * gw:2 * e + (g + 1) * gw] = (dy * sgate * dsz).astype(BF16)
                ds = dy * sz
                dp_ref[rows, gs] = (ds * mixed * dgu).astype(BF16)
                dm = ds * u
                dbs_scr[:, gs] += dm
                dmb = dm.astype(BF16)
                dws_ref[g] += jnp.where(tril, lax.dot_general(dmb, vn_scr[rows, gs], NT_DIMS, preferred_element_type=F32), 0.0)
                dvn_scr[rows, gs] = jnp.dot(wtt_ref[g], dmb, preferred_element_type=F32)
            dvn = dvn_scr[:, gs]
            dlb_ref[:, gs] += _colsum(dvn)
            dlg_ref[:, gs] += _colsum(dvn * vhat)
            dvh = dvn * lg
            c1 = c1 + _rowsum(dvh)
            c2 = c2 + _rowsum(dvh * vhat)
            if g % span == span - 1:
                dh = dh + through_w_in(g * gw + gw - kw) + through_w_in(2 * e + g * gw + gw - kw)
        c1 = c1 * (1.0 / e)
        c2 = c2 * (1.0 / e)
        for g in range(GROUPS):
            gs = slice(g * gw, (g + 1) * gw)
            dv = rs * (dvn_scr[:, gs] * lng_ref[:, gs] - c1 - v_scr[:, gs] * c2)
            dp_ref[:, e + g * gw:e + (g + 1) * gw] = (dv * dgv_ref[:, gs]).astype(BF16)
            if g % span == span - 1:
                dh = dh + through_w_in(e + g * gw + gw - kw)
        _norm_modulate_bwd(dh, x1_ref, dx_ref, g_ref, sc_ref, dx1_ref, dsh_ref, p_scr)

        @pl.when(pl.program_id(0) == n_t - 1)
        def _():
            lane = lax.broadcasted_iota(jnp.int32, (CHUNK, 128), 1)
            acc = jnp.zeros((CHUNK, 128), F32)
            for g in range(GROUPS):
                acc = acc + jnp.where(lane == g, _rowsum(dbs_scr[:, g * gw:(g + 1) * gw]), 0.0)
            dbs_ref[...] = acc
            dsc_ref[...] = p_scr[...] * g_ref[...]
            dg_ref[...] = p_scr[...] * (1.0 + sc_ref[...])

    tok = pl.BlockSpec((t, d), lambda i: (i, 0))
    vec, evec, ws = _full((1, d)), _full((1, e)), _full((GROUPS, CHUNK, CHUNK))
    vshape = jax.ShapeDtypeStruct((1, d), F32)
    return pl.pallas_call(
        body, name=name, grid=(n_t,),
        out_shape=(jax.ShapeDtypeStruct((s, e3), BF16), jax.ShapeDtypeStruct((s, e), BF16), jax.ShapeDtypeStruct((s, d), F32),
                   jax.ShapeDtypeStruct((GROUPS, CHUNK, CHUNK), F32), jax.ShapeDtypeStruct((CHUNK, 128), F32),
                   jax.ShapeDtypeStruct((1, e), F32), jax.ShapeDtypeStruct((1, e), F32), vshape, vshape, vshape),
        in_specs=[pl.BlockSpec((t, e), lambda i: (i, 0)), pl.BlockSpec((t, e), lambda i: (i, 2)),
                  pl.BlockSpec((t, e), lambda i: (i, 0)), pl.BlockSpec((t, e), lambda i: (i, 0)),
                  tok, tok, vec, vec, vec, evec, evec, ws, ws,
                  _resident((CHUNK, e)), _resident((e, d)), _resident((d, e3))],
        out_specs=(pl.BlockSpec((t, e3), lambda i: (i, 0)), pl.BlockSpec((t, e), lambda i: (i, 0)), tok,
                   ws, _full((CHUNK, 128)), evec, evec, vec, vec, vec),
        scratch_shapes=[pltpu.VMEM((t, e), F32), pltpu.VMEM((t, d), BF16),
                        pltpu.VMEM((t, e), F32), pltpu.VMEM((CHUNK, e), F32), pltpu.VMEM((1, d), F32),
                        pltpu.VMEM((t, e), F32), pltpu.VMEM((t, e), BF16), pltpu.VMEM((t, e), F32)],
        compiler_params=_params(("arbitrary",)),
    )(proj, proj, v_act, dgv, dx2, x1, gate, g1, scale, lng, lnb, wt, wtt, bsf, wo, wi)


def _conv_mixer_bwd(proj, dx1, br, conv_a, cw, gate, wo, name, ex=None):
    s, e4 = proj.shape
    e = e4 // 4
    d = dx1.shape[1]
    t = min(s, 256)
    n_t = s // t
    cwid = min(e, 512)

    def body(p_ref, dx_ref, br_ref, conv_ref, cw_ref, gate_ref, wo_ref,
             dp_ref, y_ref, dgate_ref, dcb_ref, dcw_ref, dy_scr, head_scr):
        i = pl.program_id(0)

        @pl.when(i == 0)
        def _():
            dgate_ref[...] = jnp.zeros_like(dgate_ref)
            dcb_ref[...] = jnp.zeros_like(dcb_ref)
            dcw_ref[...] = jnp.zeros_like(dcw_ref)
            head_scr[...] = jnp.zeros_like(head_scr)
        dx = dx_ref[...]
        dgate_ref[...] += _colsum(dx * br_ref[...].astype(F32))
        dy_scr[...] = lax.dot_general((dx * gate_ref[...]).astype(BF16), wo_ref[...], NT_DIMS,
                                      preferred_element_type=F32)
        row = lax.broadcasted_iota(jnp.int32, (t, cwid), 0)
        for c0 in range(0, e, cwid):
            sl = slice(c0, c0 + cwid)
            bg = p_ref[:, c0:c0 + cwid].astype(F32)
            cg = p_ref[:, e + c0:e + c0 + cwid].astype(F32)
            xin = p_ref[:, 2 * e + c0:2 * e + c0 + cwid].astype(F32)
            z = p_ref[:, 3 * e + c0:3 * e + c0 + cwid].astype(F32)
            cx = cg * xin
            w0, w1, w2 = cw_ref[0:1, sl], cw_ref[1:2, sl], cw_ref[2:3, sl]
            conv = conv_ref[:, sl].astype(F32)
            sz, dsz = _silu(z)
            dy = dy_scr[:, sl]
            y_ref[:, sl] = (sz * bg * conv).astype(BF16)
            dp_ref[:, 3 * e + c0:3 * e + c0 + cwid] = (dy * bg * conv * dsz).astype(BF16)
            dp_ref[:, c0:c0 + cwid] = (dy * sz * conv).astype(BF16)
            dconv = dy * sz * bg
            h0, h1 = head_scr[0:1, sl], head_scr[1:2, sl]
            n1 = jnp.where(row == t - 1, h0, pltpu.roll(dconv, t - 1, 0))
            n2 = jnp.where(row == t - 2, h0, jnp.where(row == t - 1, h1, pltpu.roll(dconv, t - 2, 0)))
            dcb_ref[:, sl] += _colsum(dconv)
            dcw_ref[2:3, sl] += _colsum(dconv * cx)
            dcw_ref[1:2, sl] += _colsum(n1 * cx)
            dcw_ref[0:1, sl] += _colsum(n2 * cx)
            dcx = w2 * dconv + w1 * n1 + w0 * n2
            dp_ref[:, e + c0:e + c0 + cwid] = (dcx * xin).astype(BF16)
            dp_ref[:, 2 * e + c0:2 * e + c0 + cwid] = (dcx * cg).astype(BF16)
            head_scr[:, sl] = dconv[0:8, :]

    body = _carry(ex, body, 7, 5, lambda: pl.program_id(0) == 0, None, lambda: pl.program_id(0) == n_t - 1)
    ex_args, ex_in, ex_shapes, ex_out, ex_sems = _carried(ex)
    rev = lambda i: (n_t - 1 - i, 0)
    out = pl.pallas_call(
        body, name=name, grid=(n_t,),
        out_shape=(jax.ShapeDtypeStruct((s, e4), BF16), jax.ShapeDtypeStruct((s, e), BF16),
                   jax.ShapeDtypeStruct((1, d), F32), jax.ShapeDtypeStruct((1, e), F32), jax.ShapeDtypeStruct((8, e), F32),
                   *ex_shapes),
        in_specs=[pl.BlockSpec((t, e4), rev), pl.BlockSpec((t, d), rev), pl.BlockSpec((t, d), rev),
                  pl.BlockSpec((t, e), rev), _full((3, e)), _full((1, d)), _full((e, d)), *ex_in],
        out_specs=(pl.BlockSpec((t, e4), rev), pl.BlockSpec((t, e), rev), _full((1, d)), _full((1, e)), _full((8, e)),
                   *ex_out),
        scratch_shapes=[pltpu.VMEM((t, e), F32), pltpu.VMEM((8, e), F32), *ex_sems],
        compiler_params=_params(("arbitrary",)),
    )(proj, dx1, br, conv_a, cw, gate, wo, *ex_args)
    return (*out[:5], out[5:])


def _matmul_nt_norm_bwd(dproj, w, xin, dres, g, scale, name, ex=None):
    s, d = xin.shape
    n = w.shape[1]
    tm = min(s, 512)
    n_i = s // tm

    def body(dp_ref, w_ref, x_ref, dres_ref, g_ref, sc_ref, dx_ref, dsh_ref, dsc_ref, dg_ref, p_scr):
        i = pl.program_id(0)

        @pl.when(i == 0)
        def _():
            dsh_ref[...] = jnp.zeros_like(dsh_ref)
            p_scr[...] = jnp.zeros_like(p_scr)
        dh = lax.dot_general(dp_ref[...], w_ref[...], NT_DIMS, preferred_element_type=F32)
        _norm_modulate_bwd(dh, x_ref, dres_ref, g_ref, sc_ref, dx_ref, dsh_ref, p_scr)

        @pl.when(i == n_i - 1)
        def _():
            dsc_ref[...] = p_scr[...] * g_ref[...]
            dg_ref[...] = p_scr[...] * (1.0 + sc_ref[...])

    body = _carry(ex, body, 6, 4, lambda: pl.program_id(0) == 0, None, lambda: pl.program_id(0) == n_i - 1)
    ex_args, ex_in, ex_shapes, ex_out, ex_sems = _carried(ex)
    tok = pl.BlockSpec((tm, d), lambda i: (i, 0))
    vec = pl.BlockSpec((1, d), lambda i: (0, 0))
    vshape = jax.ShapeDtypeStruct((1, d), F32)
    out = pl.pallas_call(
        body, name=name, grid=(n_i,),
        out_shape=(jax.ShapeDtypeStruct((s, d), F32), vshape, vshape, vshape, *ex_shapes),
        in_specs=[pl.BlockSpec((tm, n), lambda i: (i, 0)), _resident((d, n)), tok, tok, vec, vec, *ex_in],
        out_specs=(tok, vec, vec, vec, *ex_out),
        scratch_shapes=[pltpu.VMEM((1, d), F32), *ex_sems],
        compiler_params=_params(("arbitrary",)),
    )(dproj, w, xin, dres, g, scale, *ex_args)
    return (*out[:4], out[4:])


def _matmul_tn(a, b, colscale, rows_split, name, ex=None, a_cols=None):
    s, m = a.shape
    a_blk = 0
    if a_cols is not None:
        a_blk, m = a_cols
    n = b.shape[1]
    n_j, tn = (1, n) if rows_split else (NDEV, n // NDEV)
    fixed = (4 + 4 + 2 * 2) * m * tn
    tk = s
    while fixed + 2 * tk * (2 * m + b.dtype.itemsize * tn) > TN_VMEM_BUDGET:
        tk //= 2
    n_k = s // tk

    def body(a_ref, b_ref, cs_ref, o_ref, acc):
        k = pl.program_id(1)
        part = lax.dot_general(a_ref[...], b_ref[...].astype(BF16), TN_DIMS, preferred_element_type=F32)
        if n_k == 1:
            o_ref[...] = (part * cs_ref[...]).astype(BF16)
            return

        @pl.when(k == 0)
        def _():
            acc[...] = part

        @pl.when((k > 0) & (k < n_k - 1))
        def _():
            acc[...] += part

        @pl.when(k == n_k - 1)
        def _():
            o_ref[...] = ((acc[...] + part) * cs_ref[...]).astype(BF16)

    at = lambda j, k: (pl.program_id(0) == j) & (pl.program_id(1) == k)
    body = _carry(ex, body, 3, 1, lambda: at(0, 0), None, lambda: at(n_j - 1, n_k - 1))
    ex_args, ex_in, ex_shapes, ex_out, ex_sems = _carried(ex)
    out = pl.pallas_call(
        body, name=name, grid=(n_j, n_k),
        out_shape=(jax.ShapeDtypeStruct((n_j, m, tn), BF16), *ex_shapes),
        in_specs=[pl.BlockSpec((tk, m), lambda j, k: (k, a_blk)), pl.BlockSpec((tk, tn), lambda j, k: (k, j)),
                  pl.BlockSpec((1, tn), lambda j, k: (0, j)), *ex_in],
        out_specs=(pl.BlockSpec((None, m, tn), lambda j, k: (j, 0, 0)), *ex_out),
        scratch_shapes=[pltpu.VMEM((m, tn), F32), *ex_sems],
        compiler_params=_params(("arbitrary", "arbitrary")),
    )(a, b, colscale, *ex_args)
    return (out[0].reshape(NDEV, m // NDEV, n) if rows_split else out[0]), out[1:]


def _adam_update(w, g, m, v):
    m = ADAM_B1 * m + (1.0 - ADAM_B1) * g
    v = ADAM_B2 * v + (1.0 - ADAM_B2) * (g * g)
    m_hat = m / (1.0 - ADAM_B1 ** ADAM_STEP)
    v_hat = v / (1.0 - ADAM_B2 ** ADAM_STEP)
    return -ADAM_LR * (m_hat / (jnp.sqrt(v_hat) + ADAM_EPS) + ADAM_WD * w), m, v


def _adamw_reduce(parts, w, m, v, name, ex=None, after=()):
    n_l = len(parts)
    n_after = len(after)
    n_p, _, c = parts[0].shape
    rows = [p.shape[1] for p in parts]
    r = sum(rows)
    tr = min(min(rows), 128 if ex is not None else 256)
    n_i = r // tr
    tiles = [r_l // tr for r_l in rows]
    first_tile = [sum(tiles[:l]) for l in range(n_l)]

    def body(*refs):
        p_refs, (w_ref, m_ref, v_ref) = refs[:n_l], refs[n_l:n_l + 3]
        g_out, d_out, m_out, v_out = refs[n_l + 3 + n_after:]
        g = None
        for l, p_ref in enumerate(p_refs):
            g_l = p_ref[0].astype(F32)
            for j in range(1, n_p):
                g_l = g_l + p_ref[j].astype(F32)
            g = g_l if g is None else jnp.where(pl.program_id(0) >= first_tile[l], g_l, g)
        g_out[...] = g
        d_out[...], m_out[...], v_out[...] = _adam_update(w_ref[...], g, m_ref[...], v_ref[...])

    step = lambda k: (lambda: pl.program_id(0) == k)
    body = _carry(ex, body, n_l + 3 + n_after, 4, step(0), [step(n_i // 3), step((2 * n_i) // 3)], step(n_i - 1))
    ex_args, ex_in, ex_shapes, ex_out, ex_sems = _carried(ex)
    blk = pl.BlockSpec((tr, c), lambda i: (i, 0))
    anywhere = [pl.BlockSpec(memory_space=pl.ANY)] * n_after
    p_specs = [pl.BlockSpec((n_p, tr, c), lambda i, l=l: (0, jnp.clip(i - first_tile[l], 0, tiles[l] - 1), 0))
               for l in range(n_l)]
    shp = jax.ShapeDtypeStruct((r, c), F32)
    out = pl.pallas_call(
        body, name=name, grid=(n_i,), out_shape=(shp, shp, shp, shp, *ex_shapes),
        in_specs=[*p_specs, blk, blk, blk, *anywhere, *ex_in],
        out_specs=(blk, blk, blk, blk, *ex_out), scratch_shapes=ex_sems,
        compiler_params=_params(("arbitrary",)),
    )(*parts, w, m, v, *after, *ex_args)
    return (*out[:4], out[4:])


def _adamw_small(gs, ws, ms, vs, name):
    n = len(gs)

    def body(*refs):
        ins, outs = refs[:4 * n], refs[4 * n:]
        for a in range(n):
            d, m, v = _adam_update(ins[n + a][...], ins[a][...], ins[2 * n + a][...], ins[3 * n + a][...])
            outs[a][...], outs[n + a][...], outs[2 * n + a][...] = d, m, v

    shapes = tuple(jax.ShapeDtypeStruct(w.shape, F32) for w in ws) * 3
    out = pl.pallas_call(
        body, name=name, out_shape=shapes,
        in_specs=[_vmem()] * (4 * n), out_specs=tuple([_vmem()] * (3 * n)),
        compiler_params=pltpu.CompilerParams(vmem_limit_bytes=VMEM_LIMIT),
    )(*gs, *ws, *ms, *vs)
    return out[:n], out[n:2 * n], out[2 * n:]


def _sum_devices(packed, name):
    n = len(packed)

    def body(*refs):
        for p_ref, o_ref in zip(refs[:n], refs[n:]):
            acc = p_ref[0].astype(F32)
            for j in range(1, NDEV):
                acc = acc + p_ref[j].astype(F32)
            o_ref[...] = acc

    return pl.pallas_call(
        body, name=name, out_shape=tuple(jax.ShapeDtypeStruct(p.shape[1:], F32) for p in packed),
        in_specs=[_vmem()] * n, out_specs=tuple([_vmem()] * n),
        compiler_params=pltpu.CompilerParams(vmem_limit_bytes=VMEM_LIMIT),
    )(*packed)


def _mod_w_grad(c_t, dmod, name):
    n_layers, _, w3 = dmod.shape
    d = c_t.shape[0]

    def body(c_ref, dm_ref, o_ref):
        for i in range(n_layers):
            acc = c_ref[:, 0:1] * dm_ref[i, 0:1, :]
            for b in range(1, NDEV):
                acc = acc + c_ref[:, b:b + 1] * dm_ref[i, b:b + 1, :]
            o_ref[i] = acc

    return pl.pallas_call(
        body, name=name, out_shape=jax.ShapeDtypeStruct((n_layers, d, w3), F32),
        in_specs=[_vmem(), _vmem()], out_specs=_vmem(),
        compiler_params=pltpu.CompilerParams(vmem_limit_bytes=VMEM_LIMIT),
    )(c_t, dmod)


def _mask_transpose_ws(w_s, name):
    def body(w_ref, wt_ref, wtt_ref):
        tril = (lax.broadcasted_iota(jnp.int32, (CHUNK, CHUNK), 0) >= lax.broadcasted_iota(jnp.int32, (CHUNK, CHUNK), 1))
        for g in range(GROUPS):
            wm = jnp.where(tril, w_ref[g], 0.0)
            wt_ref[g] = wm.astype(BF16)
            wtt_ref[g] = wm.T.astype(BF16)

    shp = jax.ShapeDtypeStruct(w_s.shape, BF16)
    return pl.pallas_call(
        body, name=name, out_shape=(shp, shp), in_specs=[_vmem()], out_specs=(_vmem(), _vmem()),
    )(w_s)


def _pack(pieces):
    flat = jnp.concatenate([p.reshape(-1) for p in pieces])
    rows = -(-flat.shape[0] // (8 * PACK_W)) * 8
    return jnp.pad(flat, (0, rows * PACK_W - flat.shape[0])).reshape(rows, PACK_W)


def _unpack(flat, shapes):
    out, off = [], 0
    for shp in shapes:
        size = 1
        for dim in shp:
            size *= dim
        out.append(flat[off:off + size].reshape(shp))
        off += size
    return out


def kernel(x, c, mod_w, mod_b, norm_g, a_w_in, a_conv_w, a_conv_b, a_w_out, b_w_in, b_ln_g, b_ln_b, b_w_s, b_b_s, b_w_out, final_g, loss_target, m_mod_w, m_mod_b, m_norm_g, m_a_w_in, m_a_conv_w, m_a_conv_b, m_a_w_out, m_b_w_in, m_b_ln_g, m_b_ln_b, m_b_w_s, m_b_b_s, m_b_w_out, m_final_g, v_mod_w, v_mod_b, v_norm_g, v_a_w_in, v_a_conv_w, v_a_conv_b, v_a_w_out, v_b_w_in, v_b_ln_g, v_b_ln_b, v_b_w_s, v_b_b_s, v_b_w_out, v_final_g):
    s, d = x.shape[1], x.shape[2]
    es = a_w_out.shape[1]
    e = NDEV * es
    w3 = mod_w.shape[2]
    me = _index(_pos())
    x0 = x.reshape(s, d)
    tgt = loss_target.reshape(s, d)

    small = jnp.concatenate([a_conv_w[0], b_ln_g, b_ln_b, jnp.zeros((3, es), F32)], axis=0)
    gather_a = _gather_exchange([a_w_in[0].astype(BF16), a_w_out[0].astype(BF16), small], [True, False, True])
    gather_b = _gather_exchange([b_w_in[0].astype(BF16), b_w_out[0].astype(BF16)], [True, False])
    mod, c_all, (wa, woa, small_all) = _mod_vectors(c, mod_w, mod_b, gather_a)
    conv_w, ln_g, ln_b = small_all[0:3], small_all[3:4], small_all[4:5]
    bsf = jnp.repeat(b_b_s[0].T, e // GROUPS, axis=1)
    wt, wtt = _mask_transpose_ws(b_w_s[0], "mask_w_s")
    shift0, scale0, gate0 = mod[0:1, 0:d], mod[0:1, d:2 * d], mod[0:1, 2 * d:]
    shift1, scale1, gate1 = mod[1:2, 0:d], mod[1:2, d:2 * d], mod[1:2, 2 * d:]
    g0, g1, fg = norm_g[0:1], norm_g[1:2], final_g.reshape(1, d)

    proj_a, h0, x1, br_a, conv_a, (wb, wob) = _layer_a_fwd(
        x0, g0, scale0, shift0, gate0, wa, conv_w, a_conv_b, woa, "a_fwd", gather_b)
    proj_b, h1, dx2, loss_acc, dfg, dgate1, v_b, dgv_b, _ = _layer_b_fwd_loss(
        x1, tgt, g1, scale1, shift1, gate1, fg, wb, ln_g, ln_b, wt, bsf, wob, "b_fwd_loss")

    dproj_b, y_b, dx1, dws, dbs, dlg, dlb, dshift1, dscale1, dg1 = _layer_b_bwd(
        proj_b, v_b, dgv_b, dx2, x1, gate1, g1, scale1, ln_g, ln_b, wt, wtt, bsf, wob, wb, "b_bwd")
    gs_b_out, _ = _matmul_tn(y_b, dx2, gate1, True, "b_w_out_grad")
    gs_b_in, (gr_b_out,) = _matmul_tn(h1, dproj_b, jnp.ones((1, dproj_b.shape[1]), F32), False, "b_w_in_grad",
                                      _scatter_exchange([gs_b_out]))
    dproj_a, y_a, dgate0, dcb, dcw, (gr_b_in,) = _conv_mixer_bwd(
        proj_a, dx1, br_a, conv_a, conv_w, gate0, woa, "a_mixer_bwd", _scatter_exchange([gs_b_in]))
    gs_a_out, _ = _matmul_tn(y_a, dx1, gate0, True, "a_w_out_grad")
    gs_a_in, (gr_a_out,) = _matmul_tn(h0, dproj_a, jnp.ones((1, dproj_a.shape[1]), F32), False, "a_w_in_grad",
                                      _scatter_exchange([gs_a_out]))
    send_sems, recv_sems, gs_thru, land_thru, token = _scatter_start(gs_a_in, "a_w_in_scatter_start")
    dx0, dshift0, dscale0, dg0, _ = _matmul_nt_norm_bwd(dproj_a, wa, x0, dx1, g0, scale0 + token[0:1, 0:1], "a_in_bwd")
    gs_a_in, landed = _scatter_wait(send_sems, recv_sems, gs_thru, land_thru, dshift0, "a_w_in_scatter_wait")
    gr_a_in = lax.dynamic_update_slice_in_dim(landed, lax.dynamic_slice_in_dim(gs_a_in, me, 1, axis=0), me, axis=0)

    def big(parts, w, m, v, name, ex=None, after=()):
        shp = w.shape
        r2 = lambda t_: t_.reshape(-1, shp[-1])
        g, dl, nm, nv, ex_out = _adamw_reduce(parts, r2(w), r2(m), r2(v), name, ex, after)
        return tuple(t_.reshape(shp) for t_ in (g, dl, nm, nv)), ex_out

    pieces = [dshift0, dscale0, dgate0, dshift1, dscale1, dgate1, dg0, dg1, dcb, dcw[0:3], dlg, dlb, dfg,
              dbs[:, 0:GROUPS].T, loss_acc[0:1, 0:1]]
    shapes = [p.shape for p in pieces]
    g_send, g_recv, blocks_thru, lands_thru, g_token = _gather_start(
        [_pack(pieces), dws.astype(BF16).reshape(-1, PACK_W)], "small_gather_start")
    res = {}
    res["a_w_in"], _ = big([gr_a_in], a_w_in, m_a_w_in, v_a_w_in, "adamw_a_w_in", after=[g_token])
    res["b_w_in"], _ = big([gr_b_in], b_w_in, m_b_w_in, v_b_w_in, "adamw_b_w_in", after=[g_token])
    res["a_w_out"], _ = big([gr_a_out], a_w_out, m_a_w_out, v_a_w_out, "adamw_a_w_out", after=[g_token])
    res["b_w_out"], _ = big([gr_b_out], b_w_out, m_b_w_out, v_b_w_out, "adamw_b_w_out", after=[g_token])
    packed_all, dws_all = _gather_wait(
        g_send, g_recv, blocks_thru, lands_thru, [res[k][1] for k in ("a_w_in", "b_w_in", "a_w_out", "b_w_out")],
        "small_gather_wait")
    total, t_ws = _sum_devices([packed_all, dws_all], "sum_small_grads")
    total, t_ws = total.reshape(-1), t_ws.reshape(dws.shape)
    (t_sh0, t_sc0, t_ga0, t_sh1, t_sc1, t_ga1, t_g0, t_g1, t_cb, t_cw, t_lg, t_lb, t_fg, t_bs, t_loss) = _unpack(
        total, shapes)
    loss = t_loss.reshape(())
    grad_mod_b = jnp.concatenate([jnp.concatenate([t_sh0, t_sc0, t_ga0], axis=1),
                                  jnp.concatenate([t_sh1, t_sc1, t_ga1], axis=1)], axis=0)
    grad_norm_g = jnp.concatenate([t_g0, t_g1], axis=0)
    dmod_all = packed_all.reshape(NDEV, -1)[:, 0:6 * d].reshape(NDEV, 2, 3 * d).transpose(1, 0, 2)
    dmod_mine = lax.dynamic_slice_in_dim(dmod_all, me * w3, w3, axis=2)
    grad_mod_w = _mod_w_grad(c_all.T, dmod_mine, "mod_w_grad")
    grad_a_conv_w = lax.dynamic_slice_in_dim(t_cw, me * es, es, axis=1)
    grad_b_ln_g = lax.dynamic_slice_in_dim(t_lg, me * es, es, axis=1)
    grad_b_ln_b = lax.dynamic_slice_in_dim(t_lb, me * es, es, axis=1)

    res["mod_w"], _ = big([grad_mod_w.reshape(1, -1, w3)], mod_w, m_mod_w, v_mod_w, "adamw_mod_w")

    small_names = ["mod_b", "norm_g", "a_conv_w", "a_conv_b", "b_ln_g", "b_ln_b", "b_w_s", "b_b_s", "final_g"]
    small_g = [grad_mod_b, grad_norm_g, grad_a_conv_w, t_cb, grad_b_ln_g, grad_b_ln_b, t_ws, t_bs, t_fg]
    small_w = [mod_b, norm_g, a_conv_w, a_conv_b, b_ln_g, b_ln_b, b_w_s, b_b_s, final_g]
    small_m = [m_mod_b, m_norm_g, m_a_conv_w, m_a_conv_b, m_b_ln_g, m_b_ln_b, m_b_w_s, m_b_b_s, m_final_g]
    small_v = [v_mod_b, v_norm_g, v_a_conv_w, v_a_conv_b, v_b_ln_g, v_b_ln_b, v_b_w_s, v_b_b_s, v_final_g]
    as2d = lambda t_: t_.reshape(-1, t_.shape[-1])
    dls, nms, nvs = _adamw_small([as2d(t_) for t_ in small_g], [as2d(t_) for t_ in small_w],
                                 [as2d(t_) for t_ in small_m], [as2d(t_) for t_ in small_v], "adamw_small")
    for a, nme in enumerate(small_names):
        shp = small_w[a].shape
        res[nme] = (small_g[a].reshape(shp), dls[a].reshape(shp), nms[a].reshape(shp), nvs[a].reshape(shp))

    order = ["mod_w", "mod_b", "norm_g", "a_w_in", "a_conv_w", "a_conv_b", "a_w_out", "b_w_in", "b_ln_g", "b_ln_b",
             "b_w_s", "b_b_s", "b_w_out", "final_g"]
    return (loss, dx0.reshape(x.shape), *[res[k][0] for k in order], *[res[k][1] for k in order],
            *[res[k][2] for k in order], *[res[k][3] for k in order])
```

```python
import functools

import jax
import jax.numpy as jnp
from jax import lax
from jax.experimental import pallas as pl
from jax.experimental.pallas import tpu as pltpu

NDEV = 8
CHUNK = 128
GROUPS = 8
RMS_EPS = 1e-6
LN_EPS = 1e-5
ADAM_LR, ADAM_B1, ADAM_B2, ADAM_EPS, ADAM_WD, ADAM_STEP = 0.001, 0.9, 0.999, 1e-08, 0.01, 10
V7X_VMEM_BYTES = 64 * 1024 * 1024
VMEM_LIMIT = V7X_VMEM_BYTES - 8 * 1024 * 1024
TN_VMEM_BUDGET = 46 * 1024 * 1024
PACK_W = 1024
F32, BF16 = jnp.float32, jnp.bfloat16
MESH = pl.DeviceIdType.MESH
RSQRT2 = 0.7071067811865476
INV_SQRT_2PI = 0.3989422804014327
NT_DIMS = (((1,), (1,)), ((), ()))
TN_DIMS = (((0,), (0,)), ((), ()))


def _params(sem=None):
    return pltpu.CompilerParams(dimension_semantics=sem, vmem_limit_bytes=VMEM_LIMIT)


def _vmem():
    return pl.BlockSpec(memory_space=pltpu.VMEM)


def _hbm():
    return pl.BlockSpec(memory_space=pltpu.HBM)


def _full(shape):
    return pl.BlockSpec(shape, lambda *_: (0,) * len(shape))


def _pos():
    return lax.axis_index("x"), lax.axis_index("y"), lax.axis_index("c")


def _index(p):
    return 4 * p[0] + 2 * p[1] + p[2]


def _peer(k):
    x, y, c = _pos()
    return ((1 - x) if (k >> 2) & 1 else x, (1 - y) if (k >> 1) & 1 else y, (1 - c) if k & 1 else c)


def _silu(z):
    sg = jax.nn.sigmoid(z)
    return z * sg, sg * (1.0 + z * (1.0 - sg))


def _gelu(v):
    phi = 0.5 * (1.0 + lax.erf(v * RSQRT2))
    return v * phi, phi + v * (jnp.exp(-0.5 * v * v) * INV_SQRT_2PI)


def _colsum(v):
    return jnp.sum(v, axis=0, keepdims=True)


def _rowsum(v):
    return jnp.sum(v, axis=-1, keepdims=True)


def _gather_all_vmem(slab_ref, send_sems, recv_sems, base):
    me = _index(_pos())
    sends = []
    for k in range(1, NDEV):
        cp = pltpu.make_async_remote_copy(
            src_ref=slab_ref.at[me], dst_ref=slab_ref.at[me],
            send_sem=send_sems.at[base + k - 1], recv_sem=recv_sems.at[base + k - 1],
            device_id=_peer(k), device_id_type=MESH)
        cp.start()
        sends.append(cp)
    for k in range(1, NDEV):
        src = _index(_peer(k))
        pltpu.make_async_remote_copy(
            src_ref=slab_ref.at[src], dst_ref=slab_ref.at[src],
            send_sem=send_sems.at[base + k - 1], recv_sem=recv_sems.at[base + k - 1],
            device_id=_peer(k), device_id_type=MESH).wait_recv()
    for cp in sends:
        cp.wait_send()


def _mod_vectors(c, mod_w, mod_b, ex):
    n_layers, d, w3 = mod_w.shape
    r_in, r_out = len(ex.arrays), len(ex.out_shapes)

    def body(*refs):
        c_ref, mw_ref, mb_ref = refs[:3]
        ex_ins = refs[3:3 + r_in]
        mod_ref, call_ref = refs[3 + r_in:5 + r_in]
        ex_outs = refs[5 + r_in:5 + r_in + r_out]
        cslab, pslab, send_sems, recv_sems = refs[5 + r_in + r_out:9 + r_in + r_out]
        ex_sems = refs[9 + r_in + r_out:]
        ex.start(ex_ins, ex_outs, ex_sems)
        me = _index(_pos())
        cv = c_ref[...]
        cslab[me] = jnp.broadcast_to(cv * jax.nn.sigmoid(cv), (8, d))
        _gather_all_vmem(cslab, send_sems, recv_sems, 0)
        c_all = jnp.concatenate([cslab[k, 0:1, :] for k in range(NDEV)], axis=0)
        call_ref[...] = c_all
        for i in range(n_layers):
            pslab[me, i * NDEV:(i + 1) * NDEV, :] = jnp.dot(
                c_all, mw_ref[i], preferred_element_type=F32, precision=lax.Precision.HIGHEST)
        _gather_all_vmem(pslab, send_sems, recv_sems, NDEV - 1)
        for i in range(n_layers):
            for k in range(NDEV):
                mod_ref[i:i + 1, k * w3:(k + 1) * w3] = (
                    pslab[k, pl.ds(i * NDEV + me, 1), :] + mb_ref[i:i + 1, k * w3:(k + 1) * w3])
        for passing_on in ex.middles:
            passing_on(ex_ins, ex_outs, ex_sems)
        ex.finish(ex_ins, ex_outs, ex_sems)

    out = pl.pallas_call(
        body, name="mod_vectors",
        out_shape=(jax.ShapeDtypeStruct((n_layers, 3 * d), F32), jax.ShapeDtypeStruct((NDEV, d), F32), *ex.out_shapes),
        in_specs=[_vmem(), _vmem(), _vmem()] + [_hbm()] * r_in, out_specs=(_vmem(), _vmem(), *([_hbm()] * r_out)),
        scratch_shapes=[pltpu.VMEM((NDEV, 8, d), F32), pltpu.VMEM((NDEV, n_layers * NDEV, w3), F32),
                        pltpu.SemaphoreType.DMA((2 * (NDEV - 1),)), pltpu.SemaphoreType.DMA((2 * (NDEV - 1),)), *ex.sems],
        compiler_params=pltpu.CompilerParams(vmem_limit_bytes=VMEM_LIMIT),
    )(c, mod_w, mod_b, *ex.arrays)
    return out[0], out[1], out[2:]


class _Exchange:
    def __init__(self, arrays, out_shapes, sems, start, middles, finish):
        self.arrays, self.out_shapes, self.sems = list(arrays), list(out_shapes), list(sems)
        self.start, self.middles, self.finish = start, list(middles), finish


def _gather_exchange(shards, by_cols):
    n = len(shards)
    shapes = [sh.shape for sh in shards]

    def tools(ins, outs, sems):
        send_sems, recv_sems, local_sems = sems
        x, y, c = _pos()
        chips = [(1 - x, y), (x, 1 - y), (1 - x, 1 - y)]
        south = c == 0
        relayed = (jnp.where(south, 1 - x, x), jnp.where(south, y, 1 - y), c)
        relay_to = (jnp.where(south, x, 1 - x), jnp.where(south, 1 - y, y), c)

        def place(a, block):
            r, cc = shapes[a]
            if by_cols[a]:
                return outs[a].at[:, pl.ds(_index(block) * cc, cc)]
            return outs[a].at[pl.ds(_index(block) * r, r), :]

        def copy(a, k, block, to, src=None):
            dst = place(a, block)
            return pltpu.make_async_remote_copy(
                src_ref=dst if src is None else src, dst_ref=dst,
                send_sem=send_sems.at[a * 7 + k], recv_sem=recv_sems.at[a * 7 + k],
                device_id=to, device_id_type=MESH)

        mine = [pltpu.make_async_copy(ins[a], place(a, (x, y, c)), local_sems.at[a]) for a in range(n)]
        first = []
        for a in range(n):
            first.append(copy(a, 0, (x, y, c), (x, y, 1 - c), src=ins[a]))
            first += [copy(a, 1 + j, (x, y, c), (*chip, c), src=ins[a]) for j, chip in enumerate(chips[:2])]
        relays = [copy(a, 3, relayed, relay_to) for a in range(n)]
        passed = [copy(a, 4 + j, (*chip, c), (x, y, 1 - c)) for j, chip in enumerate(chips) for a in range(n)]
        return (x, y, c), chips, copy, mine, first, relays, passed

    def start(ins, outs, sems):
        _, _, _, mine, first, _, _ = tools(ins, outs, sems)
        for cp in mine + first:
            cp.start()

    def pass_neighbours(ins, outs, sems):
        (x, y, c), chips, copy, _, _, relays, passed = tools(ins, outs, sems)
        for j, chip in enumerate(chips[:2]):
            for a in range(n):
                copy(a, 1 + j, (*chip, c), (x, y, c)).wait_recv()
                passed[j * n + a].start()
        for cp in relays:
            cp.start()

    def pass_diagonal(ins, outs, sems):
        (x, y, c), chips, copy, _, _, _, passed = tools(ins, outs, sems)
        for a in range(n):
            copy(a, 3, (*chips[2], c), (x, y, c)).wait_recv()
            passed[2 * n + a].start()

    def finish(ins, outs, sems):
        (x, y, c), chips, copy, mine, first, relays, passed = tools(ins, outs, sems)
        for a in range(n):
            copy(a, 0, (x, y, 1 - c), (x, y, c)).wait_recv()
        for j, chip in enumerate(chips):
            for a in range(n):
                copy(a, 4 + j, (*chip, 1 - c), (x, y, c)).wait_recv()
        for cp in first + relays + passed:
            cp.wait_send()
        for cp in mine:
            cp.wait()

    out_shapes = [jax.ShapeDtypeStruct((r, NDEV * cc) if bc else (NDEV * r, cc), sh.dtype)
                  for (r, cc), bc, sh in zip(shapes, by_cols, shards)]
    sems = [pltpu.SemaphoreType.DMA((7 * n,)), pltpu.SemaphoreType.DMA((7 * n,)), pltpu.SemaphoreType.DMA((n,))]
    return _Exchange(shards, out_shapes, sems, start, [pass_neighbours, pass_diagonal], finish)


def _scatter_exchange(parts):
    n = len(parts)

    def tools(ins, outs, sems):
        send_sems, recv_sems, local_sems = sems
        me = _index(_pos())
        mine = [pltpu.make_async_copy(ins[a].at[me], outs[a].at[me], local_sems.at[a]) for a in range(n)]
        sends, arrivals = [], []
        for k in range(1, NDEV):
            peer = _peer(k)
            for a in range(n):
                pair = dict(send_sem=send_sems.at[a * 7 + k - 1], recv_sem=recv_sems.at[a * 7 + k - 1],
                            device_id=peer, device_id_type=MESH)
                sends.append(pltpu.make_async_remote_copy(src_ref=ins[a].at[_index(peer)], dst_ref=outs[a].at[me], **pair))
                slot = outs[a].at[_index(peer)]
                arrivals.append(pltpu.make_async_remote_copy(src_ref=slot, dst_ref=slot, **pair))
        return mine, sends, arrivals

    def start(ins, outs, sems):
        mine, sends, _ = tools(ins, outs, sems)
        for cp in mine + sends:
            cp.start()

    def finish(ins, outs, sems):
        mine, sends, arrivals = tools(ins, outs, sems)
        for cp in arrivals:
            cp.wait_recv()
        for cp in sends:
            cp.wait_send()
        for cp in mine:
            cp.wait()

    out_shapes = [jax.ShapeDtypeStruct(p.shape, p.dtype) for p in parts]
    sems = [pltpu.SemaphoreType.DMA((7 * n,)), pltpu.SemaphoreType.DMA((7 * n,)), pltpu.SemaphoreType.DMA((n,))]
    return _Exchange(parts, out_shapes, sems, start, [], finish)


def _scatter_start(part, name):
    def body(part_ref, land_ref, send_sems, recv_sems, part_thru, land_thru, token):
        me = _index(_pos())
        for k in range(1, NDEV):
            peer = _peer(k)
            pltpu.make_async_remote_copy(
                src_ref=part_ref.at[_index(peer)], dst_ref=land_ref.at[me],
                send_sem=send_sems.at[k - 1], recv_sem=recv_sems.at[k - 1],
                device_id=peer, device_id_type=MESH).start()
        token[...] = jnp.zeros_like(token)

    sem = pl.BlockSpec(memory_space=pltpu.SEMAPHORE)
    return pl.pallas_call(
        body, name=name,
        out_shape=(pltpu.SemaphoreType.DMA((NDEV - 1,)), pltpu.SemaphoreType.DMA((NDEV - 1,)),
                   pltpu.HBM(part.shape, part.dtype), pltpu.HBM(part.shape, part.dtype),
                   jax.ShapeDtypeStruct((8, 128), F32)),
        in_specs=(_hbm(), _hbm()), out_specs=(sem, sem, _hbm(), _hbm(), _vmem()),
        input_output_aliases={0: 2, 1: 3},
        compiler_params=pltpu.CompilerParams(has_side_effects=pltpu.SideEffectType.DATAFLOW_SIDE_EFFECTING),
    )(pltpu.with_memory_space_constraint(part, pltpu.HBM),
      pltpu.with_memory_space_constraint(lax.empty(part.shape, part.dtype), pltpu.HBM))


def _scatter_wait(send_sems, recv_sems, part_thru, land_thru, after, name):
    def body(part_ref, land_ref, send_sems, recv_sems, after_ref, part_dead, got_ref):
        for k in range(1, NDEV):
            peer = _peer(k)
            slot = land_ref.at[_index(peer)]
            copy = pltpu.make_async_remote_copy(
                src_ref=part_ref.at[_index(peer)], dst_ref=slot,
                send_sem=send_sems.at[k - 1], recv_sem=recv_sems.at[k - 1],
                device_id=peer, device_id_type=MESH)
            copy.wait_send()
            copy.wait_recv()

    sem = pl.BlockSpec(memory_space=pltpu.SEMAPHORE)
    return pl.pallas_call(
        body, name=name,
        out_shape=(pltpu.HBM(part_thru.shape, part_thru.dtype), pltpu.HBM(land_thru.shape, land_thru.dtype)),
        in_specs=(_hbm(), _hbm(), sem, sem, pl.BlockSpec(memory_space=pl.ANY)), out_specs=(_hbm(), _hbm()),
        input_output_aliases={0: 0, 1: 1},
        compiler_params=pltpu.CompilerParams(has_side_effects=pltpu.SideEffectType.DATAFLOW_SIDE_EFFECTING),
    )(part_thru, land_thru, send_sems, recv_sems, after)


def _gather_start(blocks, name):
    n = len(blocks)

    def body(*refs):
        b_refs, l_refs = refs[:n], refs[n:2 * n]
        send_sems, recv_sems = refs[2 * n:2 * n + 2]
        token = refs[-1]
        me = _index(_pos())
        for k in range(1, NDEV):
            peer = _peer(k)
            for a in range(n):
                pltpu.make_async_remote_copy(
                    src_ref=b_refs[a], dst_ref=l_refs[a].at[me],
                    send_sem=send_sems.at[a * (NDEV - 1) + k - 1], recv_sem=recv_sems.at[a * (NDEV - 1) + k - 1],
                    device_id=peer, device_id_type=MESH).start()
        token[...] = jnp.zeros_like(token)

    sem = pl.BlockSpec(memory_space=pltpu.SEMAPHORE)
    lands =[lax.empty((NDEV, *b.shape), b.dtype) for b in blocks]
    out = pl.pallas_call(
        body, name=name,
        out_shape=(pltpu.SemaphoreType.DMA((n * (NDEV - 1),)), pltpu.SemaphoreType.DMA((n * (NDEV - 1),)),
                   *[pltpu.HBM(b.shape, b.dtype) for b in blocks], *[pltpu.HBM(l.shape, l.dtype) for l in lands],
                   jax.ShapeDtypeStruct((8, 128), F32)),
        in_specs=tuple([_hbm()] * (2 * n)), out_specs=(sem, sem, *([_hbm()] * (2 * n)), _vmem()),
        input_output_aliases={a: 2 + a for a in range(2 * n)},
        compiler_params=pltpu.CompilerParams(has_side_effects=pltpu.SideEffectType.DATAFLOW_SIDE_EFFECTING),
    )(*[pltpu.with_memory_space_constraint(b, pltpu.HBM) for b in blocks],
      *[pltpu.with_memory_space_constraint(l, pltpu.HBM) for l in lands])
    return out[0], out[1], out[2:2 + n], out[2 + n:2 + 2 * n], out[-1]


def _gather_wait(send_sems, recv_sems, blocks_thru, lands_thru, after, name):
    n = len(blocks_thru)

    def body(*refs):
        b_refs, l_refs = refs[:n], refs[n:2 * n]
        send_sems, recv_sems = refs[2 * n:2 * n + 2]
        local_sems = refs[-1]
        mine = [pltpu.make_async_copy(b_refs[a], l_refs[a].at[_index(_pos())], local_sems.at[a]) for a in range(n)]
        for cp in mine:
            cp.start()
        for cp in mine:
            cp.wait()
        for k in range(1, NDEV):
            peer = _peer(k)
            for a in range(n):
                copy = pltpu.make_async_remote_copy(
                    src_ref=b_refs[a], dst_ref=l_refs[a].at[_index(peer)],
                    send_sem=send_sems.at[a * (NDEV - 1) + k - 1], recv_sem=recv_sems.at[a * (NDEV - 1) + k - 1],
                    device_id=peer, device_id_type=MESH)
                copy.wait_send()
                copy.wait_recv()

    sem = pl.BlockSpec(memory_space=pltpu.SEMAPHORE)
    out = pl.pallas_call(
        body, name=name,
        out_shape=tuple(pltpu.HBM(l.shape, l.dtype) for l in lands_thru),
        in_specs=(*([_hbm()] * (2 * n)), sem, sem, *([pl.BlockSpec(memory_space=pl.ANY)] * len(after))),
        out_specs=tuple([_hbm()] * n),
        input_output_aliases={n + a: a for a in range(n)},
        scratch_shapes=[pltpu.SemaphoreType.DMA((n,))],
        compiler_params=pltpu.CompilerParams(has_side_effects=pltpu.SideEffectType.DATAFLOW_SIDE_EFFECTING),
    )(*blocks_thru, *lands_thru, send_sems, recv_sems, *after)
    return list(out)


def _carry(ex, body, n_in, n_out, first, middle, last):
    if ex is None:
        return body
    r_in, r_out = len(ex.arrays), len(ex.out_shapes)

    def wrapped(*refs):
        ins, rins = refs[:n_in], refs[n_in:n_in + r_in]
        outs = refs[n_in + r_in:n_in + r_in + n_out]
        routs = refs[n_in + r_in + n_out:n_in + r_in + n_out + r_out]
        rest = refs[n_in + r_in + n_out + r_out:]
        scratch, sems = rest[:len(rest) - len(ex.sems)], rest[len(rest) - len(ex.sems):]

        @pl.when(first())
        def _():
            ex.start(rins, routs, sems)

        for passing_on, at_step in zip(ex.middles, middle or []):
            pl.when(at_step())(functools.partial(passing_on, rins, routs, sems))

        body(*ins, *outs, *scratch)

        @pl.when(last())
        def _():
            ex.finish(rins, routs, sems)

    return wrapped


def _carried(ex):
    if ex is None:
        return [], [], [], [], []
    return ex.arrays, [_hbm()] * len(ex.arrays), ex.out_shapes, [_hbm()] * len(ex.out_shapes), ex.sems


def _resident(shape):
    return pl.BlockSpec(shape, lambda *_: (0,) * len(shape), pipeline_mode=pl.Buffered(1))


def _norm_modulate(x_ref, g_ref, sc_ref, sh_ref):
    xv = x_ref[...]
    r = lax.rsqrt(jnp.mean(xv * xv, axis=-1, keepdims=True) + RMS_EPS)
    return ((xv * r) * g_ref[...] * (1.0 + sc_ref[...]) + sh_ref[...]).astype(BF16)


def _conv_taps(cx, t6, t7, row):
    p1 = jnp.where(row == 0, t7, pltpu.roll(cx, 1, 0))
    p2 = jnp.where(row == 0, t6, jnp.where(row == 1, t7, pltpu.roll(cx, 2, 0)))
    return p1, p2


def _layer_a_fwd(x, g, scale, shift, gate, wi, cw, cb, wo, name, ex=None):
    s, d = x.shape
    e = wo.shape[0]
    t = min(s, 256)
    n_t = s // t
    cwid = min(e, 512)

    def body(x_ref, g_ref, sc_ref, sh_ref, gate_ref, wi_ref, cw_ref, cb_ref, wo_ref,
             proj_ref, h_ref, x1_ref, br_ref, conv_ref, y_scr, tail_scr):
        @pl.when(pl.program_id(0) == 0)
        def _():
            tail_scr[...] = jnp.zeros_like(tail_scr)
        h_ref[...] = _norm_modulate(x_ref, g_ref, sc_ref, sh_ref)
        row = lax.broadcasted_iota(jnp.int32, (t, cwid), 0)

        def project(c0):
            v = jnp.dot(h_ref[...], wi_ref[:, c0:c0 + cwid], preferred_element_type=F32)
            proj_ref[:, c0:c0 + cwid] = v.astype(BF16)
            return v

        for c0 in range(0, e, cwid):
            sl = slice(c0, c0 + cwid)
            bg, z = project(c0), project(3 * e + c0)
            cx = project(e + c0) * project(2 * e + c0)
            p1, p2 = _conv_taps(cx, tail_scr[6:7, sl], tail_scr[7:8, sl], row)
            conv = cb_ref[:, sl] + cw_ref[2:3, sl] * cx + cw_ref[0:1, sl] * p2 + cw_ref[1:2, sl] * p1
            conv_ref[:, sl] = conv.astype(BF16)
            y_scr[:, sl] = (_silu(z)[0] * bg * conv).astype(BF16)
            tail_scr[:, sl] = cx[t - 8:t, :]
        br = jnp.dot(y_scr[...], wo_ref[...], preferred_element_type=F32)
        x1_ref[...] = x_ref[...] + gate_ref[...] * br
        br_ref[...] = br.astype(BF16)

    step = lambda k: (lambda: pl.program_id(0) == k)
    body = _carry(ex, body, 9, 5, step(0), [step(n_t // 3), step((2 * n_t) // 3)], step(n_t - 1))
    ex_args, ex_in, ex_shapes, ex_out, ex_sems = _carried(ex)
    tok = pl.BlockSpec((t, d), lambda i: (i, 0))
    out = pl.pallas_call(
        body, name=name, grid=(n_t,),
        out_shape=(jax.ShapeDtypeStruct((s, 4 * e), BF16), jax.ShapeDtypeStruct((s, d), BF16),
                   jax.ShapeDtypeStruct((s, d), F32), jax.ShapeDtypeStruct((s, d), BF16),
                   jax.ShapeDtypeStruct((s, e), BF16), *ex_shapes),
        in_specs=[tok, _full((1, d)), _full((1, d)), _full((1, d)), _full((1, d)), _resident((d, 4 * e)),
                  _full((3, e)), _full((1, e)), _resident((e, d)), *ex_in],
        out_specs=(pl.BlockSpec((t, 4 * e), lambda i: (i, 0)), tok, tok, tok,
                   pl.BlockSpec((t, e), lambda i: (i, 0)), *ex_out),
        scratch_shapes=[pltpu.VMEM((t, e), BF16), pltpu.VMEM((8, e), F32), *ex_sems],
        compiler_params=_params(("arbitrary",)),
    )(x, g, scale, shift, gate, wi, cw, cb, wo, *ex_args)
    return (*out[:5], out[5:])


def _ln_stats(v_of, v_scr, t, e):
    gw = e // GROUPS
    s1 = jnp.zeros((t, 1), F32)
    for g in range(GROUPS):
        v = v_of(g)
        v_scr[:, g * gw:(g + 1) * gw] = v
        s1 = s1 + _rowsum(v)
    mu = s1 * (1.0 / e)
    s2 = jnp.zeros((t, 1), F32)
    for g in range(GROUPS):
        dv = v_scr[:, g * gw:(g + 1) * gw] - mu
        s2 = s2 + _rowsum(dv * dv)
    return mu, lax.rsqrt(s2 * (1.0 / e) + LN_EPS)


def _layer_b_fwd_loss(x1, tgt, g1, scale, shift, gate, fg, wi, lng, lnb, wt, bsf, wo, name, ex=None):
    s, d = x1.shape
    e = wo.shape[0]
    gw = e // GROUPS
    t = min(s, 256)
    n_t = s // t

    def body(x1_ref, tgt_ref, g_ref, sc_ref, sh_ref, gate_ref, fg_ref, wi_ref, lng_ref, lnb_ref, wt_ref, bsf_ref, wo_ref,
             proj_ref, h_ref, dx2_ref, loss_ref, dfg_ref, dgate_ref, v_ref, dgv_ref, v_scr, y_scr):
        @pl.when(pl.program_id(0) == 0)
        def _():
            loss_ref[...] = jnp.zeros_like(loss_ref)
            dfg_ref[...] = jnp.zeros_like(dfg_ref)
            dgate_ref[...] = jnp.zeros_like(dgate_ref)
        h_ref[...] = _norm_modulate(x1_ref, g_ref, sc_ref, sh_ref)

        def project(c0):
            v = jnp.dot(h_ref[...], wi_ref[:, c0:c0 + gw], preferred_element_type=F32)
            proj_ref[:, c0:c0 + gw] = v.astype(BF16)
            return v

        def gelu_v(g):
            gs = slice(g * gw, (g + 1) * gw)
            v, dgv = _gelu(project(e + g * gw))
            v_ref[:, gs] = v.astype(BF16)
            dgv_ref[:, gs] = dgv.astype(BF16)
            return v

        mu, rs = _ln_stats(gelu_v, v_scr, t, e)
        for g in range(GROUPS):
            gs = slice(g * gw, (g + 1) * gw)
            vn = (((v_scr[:, gs] - mu) * rs) * lng_ref[:, gs] + lnb_ref[:, gs]).astype(BF16)
            u = _gelu(project(g * gw))[0]
            sz = _silu(project(2 * e + g * gw))[0]
            for ch in range(t // CHUNK):
                rows = slice(ch * CHUNK, (ch + 1) * CHUNK)
                mixed = jnp.dot(wt_ref[g], vn[rows], preferred_element_type=F32) + bsf_ref[:, gs]
                y_scr[rows, gs] = (sz[rows] * (u[rows] * mixed)).astype(BF16)
        br = jnp.dot(y_scr[...], wo_ref[...], preferred_element_type=F32)
        x2 = x1_ref[...] + gate_ref[...] * br
        r2 = lax.rsqrt(jnp.mean(x2 * x2, axis=-1, keepdims=True) + RMS_EPS)
        xn = x2 * r2
        diff = xn * fg_ref[...] - tgt_ref[...]
        loss_ref[...] += jnp.broadcast_to(0.5 * _colsum(jnp.mean(diff * diff, axis=-1, keepdims=True)), loss_ref.shape)
        dout = diff * (1.0 / d)
        dfg_ref[...] += _colsum(dout * xn)
        dxn = dout * fg_ref[...]
        dx2 = r2 * (dxn - xn * jnp.mean(dxn * xn, axis=-1, keepdims=True))
        dx2_ref[...] = dx2
        dgate_ref[...] += _colsum(dx2 * br)

    step = lambda k: (lambda: pl.program_id(0) == k)
    body = _carry(ex, body, 13, 8, step(0), [step(n_t // 3), step((2 * n_t) // 3)], step(n_t - 1))
    ex_args, ex_in, ex_shapes, ex_out, ex_sems = _carried(ex)
    tok = pl.BlockSpec((t, d), lambda i: (i, 0))
    vec = _full((1, d))
    out = pl.pallas_call(
        body, name=name, grid=(n_t,),
        out_shape=(jax.ShapeDtypeStruct((s, 3 * e), BF16), jax.ShapeDtypeStruct((s, d), BF16),
                   jax.ShapeDtypeStruct((s, d), F32), jax.ShapeDtypeStruct((8, 128), F32),
                   jax.ShapeDtypeStruct((1, d), F32), jax.ShapeDtypeStruct((1, d), F32),
                   jax.ShapeDtypeStruct((s, e), BF16), jax.ShapeDtypeStruct((s, e), BF16), *ex_shapes),
        in_specs=[tok, tok, vec, vec, vec, vec, vec, _resident((d, 3 * e)), _full((1, e)), _full((1, e)),
                  _full((GROUPS, CHUNK, CHUNK)), _resident((CHUNK, e)), _resident((e, d)), *ex_in],
        out_specs=(pl.BlockSpec((t, 3 * e), lambda i: (i, 0)), tok, tok, _full((8, 128)), vec, vec,
                   pl.BlockSpec((t, e), lambda i: (i, 0)), pl.BlockSpec((t, e), lambda i: (i, 0)), *ex_out),
        scratch_shapes=[pltpu.VMEM((t, e), F32), pltpu.VMEM((t, e), BF16), *ex_sems],
        compiler_params=_params(("arbitrary",)),
    )(x1, tgt, g1, scale, shift, gate, fg, wi, lng, lnb, wt, bsf, wo, *ex_args)
    return (*out[:8], out[8:])


def _norm_modulate_bwd(dh, x_ref, dres_ref, g_ref, sc_ref, dx_ref, dsh_ref, p_scr):
    xv = x_ref[...]
    r = lax.rsqrt(jnp.mean(xv * xv, axis=-1, keepdims=True) + RMS_EPS)
    xn = xv * r
    dsh_ref[...] += _colsum(dh)
    p_scr[...] += _colsum(dh * xn)
    dxn = dh * (g_ref[...] * (1.0 + sc_ref[...]))
    dx_ref[...] = r * (dxn - xn * jnp.mean(dxn * xn, axis=-1, keepdims=True)) + dres_ref[...]


def _layer_b_bwd(proj, v_act, dgv, dx2, x1, gate, g1, scale, lng, lnb, wt, wtt, bsf, wo, wi, name):
    s, e3 = proj.shape
    e = e3 // 3
    d = dx2.shape[1]
    gw = e // GROUPS
    t = min(s, 256)
    n_t = s // t

    def body(pu_ref, pz_ref, v_ref, dgv_ref, dx_ref, x1_ref, gate_ref, g_ref, sc_ref, lng_ref, lnb_ref, wt_ref, wtt_ref,
             bsf_ref, wo_ref, wi_ref,
             dp_ref, y_ref, dx1_ref, dws_ref, dbs_ref, dlg_ref, dlb_ref, dsh_ref, dsc_ref, dg_ref,
             v_scr, dbr_scr, dvn_scr, dbs_scr, p_scr, dy_scr, vn_scr, mixed_scr):
        @pl.when(pl.program_id(0) == 0)
        def _():
            dws_ref[...] = jnp.zeros_like(dws_ref)
            dlg_ref[...] = jnp.zeros_like(dlg_ref)
            dlb_ref[...] = jnp.zeros_like(dlb_ref)
            dsh_ref[...] = jnp.zeros_like(dsh_ref)
            dbs_scr[...] = jnp.zeros_like(dbs_scr)
            p_scr[...] = jnp.zeros_like(p_scr)
        dbr_scr[...] = (dx_ref[...] * gate_ref[...]).astype(BF16)
        mu, rs = _ln_stats(lambda g: v_ref[:, g * gw:(g + 1) * gw].astype(F32), v_scr, t, e)
        tril = (lax.broadcasted_iota(jnp.int32, (CHUNK, CHUNK), 0) >= lax.broadcasted_iota(jnp.int32, (CHUNK, CHUNK), 1))
        c1 = jnp.zeros((t, 1), F32)
        c2 = jnp.zeros((t, 1), F32)
        span = 2
        kw = span * gw
        dh = jnp.zeros((t, d), F32)

        def through_w_in(c0):
            return lax.dot_general(dp_ref[:, c0:c0 + kw], wi_ref[:, c0:c0 + kw], NT_DIMS, preferred_element_type=F32)

        dy_scr[...] = lax.dot_general(dbr_scr[...], wo_ref[...], NT_DIMS, preferred_element_type=F32)
        for g in range(GROUPS):
            gs = slice(g * gw, (g + 1) * gw)
            vhat = (v_scr[:, gs] - mu) * rs
            v_scr[:, gs] = vhat
            vn = (vhat * lng_ref[:, gs] + lnb_ref[:, gs]).astype(BF16)
            vn_scr[:, gs] = vn
            for ch in range(t // CHUNK):
                rows = slice(ch * CHUNK, (ch + 1) * CHUNK)
                mixed_scr[rows, gs] = jnp.dot(wt_ref[g], vn[rows], preferred_element_type=F32) + bsf_ref[:, gs]
        for g in range(GROUPS):
            gs = slice(g * gw, (g + 1) * gw)
            vhat = v_scr[:, gs]
            lg = lng_ref[:, gs]
            for ch in range(t // CHUNK):
                rows = slice(ch * CHUNK, (ch + 1) * CHUNK)
                mixed = mixed_scr[rows, gs]
                u, dgu = _gelu(pu_ref[rows, gs].astype(F32))
                sz, dsz = _silu(pz_ref[rows, gs].astype(F32))
                sgate = u * mixed
                y_ref[rows, gs] = (sz * sgate).astype(BF16)
                dy = dy_scr[rows, gs]
                dp_ref[rows, 2 * e + g * gw:2 * e + (g + 1) * gw] = (dy * sgate * dsz).astype(BF16)
                ds = dy * sz
                dp_ref[rows, gs] = (ds * mixed * dgu).astype(BF16)
                dm = ds * u
                dbs_scr[:, gs] += dm
                dmb = dm.astype(BF16)
                dws_ref[g] += jnp.where(tril, lax.dot_general(dmb, vn_scr[rows, gs], NT_DIMS, preferred_element_type=F32), 0.0)
                dvn_scr[rows, gs] = jnp.dot(wtt_ref[g], dmb, preferred_element_type=F32)
            dvn = dvn_scr[:, gs]
            dlb_ref[:, gs] += _colsum(dvn)
            dlg_ref[:, gs] += _colsum(dvn * vhat)
            dvh = dvn * lg
            c1 = c1 + _rowsum(dvh)
            c2 = c2 + _rowsum(dvh * vhat)
            if g % span == span - 1:
                dh = dh + through_w_in(g * gw + gw - kw) + through_w_in(2 * e + g * gw + gw - kw)
        c1 = c1 * (1.0 / e)
        c2 = c2 * (1.0 / e)
        for g in range(GROUPS):
            gs = slice(g * gw, (g + 1) * gw)
            dv = rs * (dvn_scr[:, gs] * lng_ref[:, gs] - c1 - v_scr[:, gs] * c2)
            dp_ref[:, e + g * gw:e + (g + 1) * gw] = (dv * dgv_ref[:, gs]).astype(BF16)
            if g % span == span - 1:
                dh = dh + through_w_in(e + g * gw + gw - kw)
        _norm_modulate_bwd(dh, x1_ref, dx_ref, g_ref, sc_ref, dx1_ref, dsh_ref, p_scr)

        @pl.when(pl.program_id(0) == n_t - 1)
        def _():
            lane = lax.broadcasted_iota(jnp.int32, (CHUNK, 128), 1)
            acc = jnp.zeros((CHUNK, 128), F32)
            for g in range(GROUPS):
                acc = acc + jnp.where(lane == g, _rowsum(dbs_scr[:, g * gw:(g + 1) * gw]), 0.0)
            dbs_ref[...] = acc
            dsc_ref[...] = p_scr[...] * g_ref[...]
            dg_ref[...] = p_scr[...] * (1.0 + sc_ref[...])

    tok = pl.BlockSpec((t, d), lambda i: (i, 0))
    vec, evec, ws = _full((1, d)), _full((1, e)), _full((GROUPS, CHUNK, CHUNK))
    vshape = jax.ShapeDtypeStruct((1, d), F32)
    return pl.pallas_call(
        body, name=name, grid=(n_t,),
        out_shape=(jax.ShapeDtypeStruct((s, e3), BF16), jax.ShapeDtypeStruct((s, e), BF16), jax.ShapeDtypeStruct((s, d), F32),
                   jax.ShapeDtypeStruct((GROUPS, CHUNK, CHUNK), F32), jax.ShapeDtypeStruct((CHUNK, 128), F32),
                   jax.ShapeDtypeStruct((1, e), F32), jax.ShapeDtypeStruct((1, e), F32), vshape, vshape, vshape),
        in_specs=[pl.BlockSpec((t, e), lambda i: (i, 0)), pl.BlockSpec((t, e), lambda i: (i, 2)),
                  pl.BlockSpec((t, e), lambda i: (i, 0)), pl.BlockSpec((t, e), lambda i: (i, 0)),
                  tok, tok, vec, vec, vec, evec, evec, ws, ws,
                  _resident((CHUNK, e)), _resident((e, d)), _resident((d, e3))],
        out_specs=(pl.BlockSpec((t, e3), lambda i: (i, 0)), pl.BlockSpec((t, e), lambda i: (i, 0)), tok,
                   ws, _full((CHUNK, 128)), evec, evec, vec, vec, vec),
        scratch_shapes=[pltpu.VMEM((t, e), F32), pltpu.VMEM((t, d), BF16),
                        pltpu.VMEM((t, e), F32), pltpu.VMEM((CHUNK, e), F32), pltpu.VMEM((1, d), F32),
                        pltpu.VMEM((t, e), F32), pltpu.VMEM((t, e), BF16), pltpu.VMEM((t, e), F32)],
        compiler_params=_params(("arbitrary",)),
    )(proj, proj, v_act, dgv, dx2, x1, gate, g1, scale, lng, lnb, wt, wtt, bsf, wo, wi)


def _conv_mixer_bwd(proj, dx1, br, conv_a, cw, gate, wo, name, ex=None):
    s, e4 = proj.shape
    e = e4 // 4
    d = dx1.shape[1]
    t = min(s, 256)
    n_t = s // t
    cwid = min(e, 512)

    def body(p_ref, dx_ref, br_ref, conv_ref, cw_ref, gate_ref, wo_ref,
             dp_ref, y_ref, dgate_ref, dcb_ref, dcw_ref, dy_scr, head_scr):
        i = pl.program_id(0)

        @pl.when(i == 0)
        def _():
            dgate_ref[...] = jnp.zeros_like(dgate_ref)
            dcb_ref[...] = jnp.zeros_like(dcb_ref)
            dcw_ref[...] = jnp.zeros_like(dcw_ref)
            head_scr[...] = jnp.zeros_like(head_scr)
        dx = dx_ref[...]
        dgate_ref[...] += _colsum(dx * br_ref[...].astype(F32))
        dy_scr[...] = lax.dot_general((dx * gate_ref[...]).astype(BF16), wo_ref[...], NT_DIMS,
                                      preferred_element_type=F32)
        row = lax.broadcasted_iota(jnp.int32, (t, cwid), 0)
        for c0 in range(0, e, cwid):
            sl = slice(c0, c0 + cwid)
            bg = p_ref[:, c0:c0 + cwid].astype(F32)
            cg = p_ref[:, e + c0:e + c0 + cwid].astype(F32)
            xin = p_ref[:, 2 * e + c0:2 * e + c0 + cwid].astype(F32)
            z = p_ref[:, 3 * e + c0:3 * e + c0 + cwid].astype(F32)
            cx = cg * xin
            w0, w1, w2 = cw_ref[0:1, sl], cw_ref[1:2, sl], cw_ref[2:3, sl]
            conv = conv_ref[:, sl].astype(F32)
            sz, dsz = _silu(z)
            dy = dy_scr[:, sl]
            y_ref[:, sl] = (sz * bg * conv).astype(BF16)
            dp_ref[:, 3 * e + c0:3 * e + c0 + cwid] = (dy * bg * conv * dsz).astype(BF16)
            dp_ref[:, c0:c0 + cwid] = (dy * sz * conv).astype(BF16)
            dconv = dy * sz * bg
            h0, h1 = head_scr[0:1, sl], head_scr[1:2, sl]
            n1 = jnp.where(row == t - 1, h0, pltpu.roll(dconv, t - 1, 0))
            n2 = jnp.where(row == t - 2, h0, jnp.where(row == t - 1, h1, pltpu.roll(dconv, t - 2, 0)))
            dcb_ref[:, sl] += _colsum(dconv)
            dcw_ref[2:3, sl] += _colsum(dconv * cx)
            dcw_ref[1:2, sl] += _colsum(n1 * cx)
            dcw_ref[0:1, sl] += _colsum(n2 * cx)
            dcx = w2 * dconv + w1 * n1 + w0 * n2
            dp_ref[:, e + c0:e + c0 + cwid] = (dcx * xin).astype(BF16)
            dp_ref[:, 2 * e + c0:2 * e + c0 + cwid] = (dcx * cg).astype(BF16)
            head_scr[:, sl] = dconv[0:8, :]

    body = _carry(ex, body, 7, 5, lambda: pl.program_id(0) == 0, None, lambda: pl.program_id(0) == n_t - 1)
    ex_args, ex_in, ex_shapes, ex_out, ex_sems = _carried(ex)
    rev = lambda i: (n_t - 1 - i, 0)
    out = pl.pallas_call(
        body, name=name, grid=(n_t,),
        out_shape=(jax.ShapeDtypeStruct((s, e4), BF16), jax.ShapeDtypeStruct((s, e), BF16),
                   jax.ShapeDtypeStruct((1, d), F32), jax.ShapeDtypeStruct((1, e), F32), jax.ShapeDtypeStruct((8, e), F32),
                   *ex_shapes),
        in_specs=[pl.BlockSpec((t, e4), rev), pl.BlockSpec((t, d), rev), pl.BlockSpec((t, d), rev),
                  pl.BlockSpec((t, e), rev), _full((3, e)), _full((1, d)), _full((e, d)), *ex_in],
        out_specs=(pl.BlockSpec((t, e4), rev), pl.BlockSpec((t, e), rev), _full((1, d)), _full((1, e)), _full((8, e)),
                   *ex_out),
        scratch_shapes=[pltpu.VMEM((t, e), F32), pltpu.VMEM((8, e), F32), *ex_sems],
        compiler_params=_params(("arbitrary",)),
    )(proj, dx1, br, conv_a, cw, gate, wo, *ex_args)
    return (*out[:5], out[5:])


def _matmul_nt_norm_bwd(dproj, w, xin, dres, g, scale, name, ex=None):
    s, d = xin.shape
    n = w.shape[1]
    tm = min(s, 512)
    n_i = s // tm

    def body(dp_ref, w_ref, x_ref, dres_ref, g_ref, sc_ref, dx_ref, dsh_ref, dsc_ref, dg_ref, p_scr):
        i = pl.program_id(0)

        @pl.when(i == 0)
        def _():
            dsh_ref[...] = jnp.zeros_like(dsh_ref)
            p_scr[...] = jnp.zeros_like(p_scr)
        dh = lax.dot_general(dp_ref[...], w_ref[...], NT_DIMS, preferred_element_type=F32)
        _norm_modulate_bwd(dh, x_ref, dres_ref, g_ref, sc_ref, dx_ref, dsh_ref, p_scr)

        @pl.when(i == n_i - 1)
        def _():
            dsc_ref[...] = p_scr[...] * g_ref[...]
            dg_ref[...] = p_scr[...] * (1.0 + sc_ref[...])

    body = _carry(ex, body, 6, 4, lambda: pl.program_id(0) == 0, None, lambda: pl.program_id(0) == n_i - 1)
    ex_args, ex_in, ex_shapes, ex_out, ex_sems = _carried(ex)
    tok = pl.BlockSpec((tm, d), lambda i: (i, 0))
    vec = pl.BlockSpec((1, d), lambda i: (0, 0))
    vshape = jax.ShapeDtypeStruct((1, d), F32)
    out = pl.pallas_call(
        body, name=name, grid=(n_i,),
        out_shape=(jax.ShapeDtypeStruct((s, d), F32), vshape, vshape, vshape, *ex_shapes),
        in_specs=[pl.BlockSpec((tm, n), lambda i: (i, 0)), _resident((d, n)), tok, tok, vec, vec, *ex_in],
        out_specs=(tok, vec, vec, vec, *ex_out),
        scratch_shapes=[pltpu.VMEM((1, d), F32), *ex_sems],
        compiler_params=_params(("arbitrary",)),
    )(dproj, w, xin, dres, g, scale, *ex_args)
    return (*out[:4], out[4:])


def _matmul_tn(a, b, colscale, rows_split, name, ex=None, a_cols=None):
    s, m = a.shape
    a_blk = 0
    if a_cols is not None:
        a_blk, m = a_cols
    n = b.shape[1]
    n_j, tn = (1, n) if rows_split else (NDEV, n // NDEV)
    fixed = (4 + 4 + 2 * 2) * m * tn
    tk = s
    while fixed + 2 * tk * (2 * m + b.dtype.itemsize * tn) > TN_VMEM_BUDGET:
        tk //= 2
    n_k = s // tk

    def body(a_ref, b_ref, cs_ref, o_ref, acc):
        k = pl.program_id(1)
        part = lax.dot_general(a_ref[...], b_ref[...].astype(BF16), TN_DIMS, preferred_element_type=F32)
        if n_k == 1:
            o_ref[...] = (part * cs_ref[...]).astype(BF16)
            return

        @pl.when(k == 0)
        def _():
            acc[...] = part

        @pl.when((k > 0) & (k < n_k - 1))
        def _():
            acc[...] += part

        @pl.when(k == n_k - 1)
        def _():
            o_ref[...] = ((acc[...] + part) * cs_ref[...]).astype(BF16)

    at = lambda j, k: (pl.program_id(0) == j) & (pl.program_id(1) == k)
    body = _carry(ex, body, 3, 1, lambda: at(0, 0), None, lambda: at(n_j - 1, n_k - 1))
    ex_args, ex_in, ex_shapes, ex_out, ex_sems = _carried(ex)
    out = pl.pallas_call(
        body, name=name, grid=(n_j, n_k),
        out_shape=(jax.ShapeDtypeStruct((n_j, m, tn), BF16), *ex_shapes),
        in_specs=[pl.BlockSpec((tk, m), lambda j, k: (k, a_blk)), pl.BlockSpec((tk, tn), lambda j, k: (k, j)),
                  pl.BlockSpec((1, tn), lambda j, k: (0, j)), *ex_in],
        out_specs=(pl.BlockSpec((None, m, tn), lambda j, k: (j, 0, 0)), *ex_out),
        scratch_shapes=[pltpu.VMEM((m, tn), F32), *ex_sems],
        compiler_params=_params(("arbitrary", "arbitrary")),
    )(a, b, colscale, *ex_args)
    return (out[0].reshape(NDEV, m // NDEV, n) if rows_split else out[0]), out[1:]


def _adam_update(w, g, m, v):
    m = ADAM_B1 * m + (1.0 - ADAM_B1) * g
    v = ADAM_B2 * v + (1.0 - ADAM_B2) * (g * g)
    m_hat = m / (1.0 - ADAM_B1 ** ADAM_STEP)
    v_hat = v / (1.0 - ADAM_B2 ** ADAM_STEP)
    return -ADAM_LR * (m_hat / (jnp.sqrt(v_hat) + ADAM_EPS) + ADAM_WD * w), m, v


def _adamw_reduce(parts, w, m, v, name, ex=None, after=()):
    n_l = len(parts)
    n_after = len(after)
    n_p, _, c = parts[0].shape
    rows = [p.shape[1] for p in parts]
    r = sum(rows)
    tr = min(min(rows), 128 if ex is not None else 256)
    n_i = r // tr
    tiles = [r_l // tr for r_l in rows]
    first_tile = [sum(tiles[:l]) for l in range(n_l)]

    def body(*refs):
        p_refs, (w_ref, m_ref, v_ref) = refs[:n_l], refs[n_l:n_l + 3]
        g_out, d_out, m_out, v_out = refs[n_l + 3 + n_after:]
        g = None
        for l, p_ref in enumerate(p_refs):
            g_l = p_ref[0].astype(F32)
            for j in range(1, n_p):
                g_l = g_l + p_ref[j].astype(F32)
            g = g_l if g is None else jnp.where(pl.program_id(0) >= first_tile[l], g_l, g)
        g_out[...] = g
        d_out[...], m_out[...], v_out[...] = _adam_update(w_ref[...], g, m_ref[...], v_ref[...])

    step = lambda k: (lambda: pl.program_id(0) == k)
    body = _carry(ex, body, n_l + 3 + n_after, 4, step(0), [step(n_i // 3), step((2 * n_i) // 3)], step(n_i - 1))
    ex_args, ex_in, ex_shapes, ex_out, ex_sems = _carried(ex)
    blk = pl.BlockSpec((tr, c), lambda i: (i, 0))
    anywhere = [pl.BlockSpec(memory_space=pl.ANY)] * n_after
    p_specs = [pl.BlockSpec((n_p, tr, c), lambda i, l=l: (0, jnp.clip(i - first_tile[l], 0, tiles[l] - 1), 0))
               for l in range(n_l)]
    shp = jax.ShapeDtypeStruct((r, c), F32)
    out = pl.pallas_call(
        body, name=name, grid=(n_i,), out_shape=(shp, shp, shp, shp, *ex_shapes),
        in_specs=[*p_specs, blk, blk, blk, *anywhere, *ex_in],
        out_specs=(blk, blk, blk, blk, *ex_out), scratch_shapes=ex_sems,
        compiler_params=_params(("arbitrary",)),
    )(*parts, w, m, v, *after, *ex_args)
    return (*out[:4], out[4:])


def _adamw_small(gs, ws, ms, vs, name):
    n = len(gs)

    def body(*refs):
        ins, outs = refs[:4 * n], refs[4 * n:]
        for a in range(n):
            d, m, v = _adam_update(ins[n + a][...], ins[a][...], ins[2 * n + a][...], ins[3 * n + a][...])
            outs[a][...], outs[n + a][...], outs[2 * n + a][...] = d, m, v

    shapes = tuple(jax.ShapeDtypeStruct(w.shape, F32) for w in ws) * 3
    out = pl.pallas_call(
        body, name=name, out_shape=shapes,
        in_specs=[_vmem()] * (4 * n), out_specs=tuple([_vmem()] * (3 * n)),
        compiler_params=pltpu.CompilerParams(vmem_limit_bytes=VMEM_LIMIT),
    )(*gs, *ws, *ms, *vs)
    return out[:n], out[n:2 * n], out[2 * n:]


def _sum_devices(packed, name):
    n = len(packed)

    def body(*refs):
        for p_ref, o_ref in zip(refs[:n], refs[n:]):
            acc = p_ref[0].astype(F32)
            for j in range(1, NDEV):
                acc = acc + p_ref[j].astype(F32)
            o_ref[...] = acc

    return pl.pallas_call(
        body, name=name, out_shape=tuple(jax.ShapeDtypeStruct(p.shape[1:], F32) for p in packed),
        in_specs=[_vmem()] * n, out_specs=tuple([_vmem()] * n),
        compiler_params=pltpu.CompilerParams(vmem_limit_bytes=VMEM_LIMIT),
    )(*packed)


def _mod_w_grad(c_t, dmod, name):
    n_layers, _, w3 = dmod.shape
    d = c_t.shape[0]

    def body(c_ref, dm_ref, o_ref):
        for i in range(n_layers):
            acc = c_ref[:, 0:1] * dm_ref[i, 0:1, :]
            for b in range(1, NDEV):
                acc = acc + c_ref[:, b:b + 1] * dm_ref[i, b:b + 1, :]
            o_ref[i] = acc

    return pl.pallas_call(
        body, name=name, out_shape=jax.ShapeDtypeStruct((n_layers, d, w3), F32),
        in_specs=[_vmem(), _vmem()], out_specs=_vmem(),
        compiler_params=pltpu.CompilerParams(vmem_limit_bytes=VMEM_LIMIT),
    )(c_t, dmod)


def _mask_transpose_ws(w_s, name):
    def body(w_ref, wt_ref, wtt_ref):
        tril = (lax.broadcasted_iota(jnp.int32, (CHUNK, CHUNK), 0) >= lax.broadcasted_iota(jnp.int32, (CHUNK, CHUNK), 1))
        for g in range(GROUPS):
            wm = jnp.where(tril, w_ref[g], 0.0)
            wt_ref[g] = wm.astype(BF16)
            wtt_ref[g] = wm.T.astype(BF16)

    shp = jax.ShapeDtypeStruct(w_s.shape, BF16)
    return pl.pallas_call(
        body, name=name, out_shape=(shp, shp), in_specs=[_vmem()], out_specs=(_vmem(), _vmem()),
    )(w_s)


def _pack(pieces):
    flat = jnp.concatenate([p.reshape(-1) for p in pieces])
    rows = -(-flat.shape[0] // (8 * PACK_W)) * 8
    return jnp.pad(flat, (0, rows * PACK_W - flat.shape[0])).reshape(rows, PACK_W)


def _unpack(flat, shapes):
    out, off = [], 0
    for shp in shapes:
        size = 1
        for dim in shp:
            size *= dim
        out.append(flat[off:off + size].reshape(shp))
        off += size
    return out


def kernel(x, c, mod_w, mod_b, norm_g, a_w_in, a_conv_w, a_conv_b, a_w_out, b_w_in, b_ln_g, b_ln_b, b_w_s, b_b_s, b_w_out, final_g, loss_target, m_mod_w, m_mod_b, m_norm_g, m_a_w_in, m_a_conv_w, m_a_conv_b, m_a_w_out, m_b_w_in, m_b_ln_g, m_b_ln_b, m_b_w_s, m_b_b_s, m_b_w_out, m_final_g, v_mod_w, v_mod_b, v_norm_g, v_a_w_in, v_a_conv_w, v_a_conv_b, v_a_w_out, v_b_w_in, v_b_ln_g, v_b_ln_b, v_b_w_s, v_b_b_s, v_b_w_out, v_final_g):
    s, d = x.shape[1], x.shape[2]
    es = a_w_out.shape[1]
    e = NDEV * es
    w3 = mod_w.shape[2]
    me = _index(_pos())
    x0 = x.reshape(s, d)
    tgt = loss_target.reshape(s, d)

    small = jnp.concatenate([a_conv_w[0], b_ln_g, b_ln_b, jnp.zeros((3, es), F32)], axis=0)
    gather_a = _gather_exchange([a_w_in[0].astype(BF16), a_w_out[0].astype(BF16), small], [True, False, True])
    gather_b = _gather_exchange([b_w_in[0].astype(BF16), b_w_out[0].astype(BF16)], [True, False])
    mod, c_all, (wa, woa, small_all) = _mod_vectors(c, mod_w, mod_b, gather_a)
    conv_w, ln_g, ln_b = small_all[0:3], small_all[3:4], small_all[4:5]
    bsf = jnp.repeat(b_b_s[0].T, e // GROUPS, axis=1)
    wt, wtt = _mask_transpose_ws(b_w_s[0], "mask_w_s")
    shift0, scale0, gate0 = mod[0:1, 0:d], mod[0:1, d:2 * d], mod[0:1, 2 * d:]
    shift1, scale1, gate1 = mod[1:2, 0:d], mod[1:2, d:2 * d], mod[1:2, 2 * d:]
    g0, g1, fg = norm_g[0:1], norm_g[1:2], final_g.reshape(1, d)

    proj_a, h0, x1, br_a, conv_a, (wb, wob) = _layer_a_fwd(
        x0, g0, scale0, shift0, gate0, wa, conv_w, a_conv_b, woa, "a_fwd", gather_b)
    proj_b, h1, dx2, loss_acc, dfg, dgate1, v_b, dgv_b, _ = _layer_b_fwd_loss(
        x1, tgt, g1, scale1, shift1, gate1, fg, wb, ln_g, ln_b, wt, bsf, wob, "b_fwd_loss")

    dproj_b, y_b, dx1, dws, dbs, dlg, dlb, dshift1, dscale1, dg1 = _layer_b_bwd(
        proj_b, v_b, dgv_b, dx2, x1, gate1, g1, scale1, ln_g, ln_b, wt, wtt, bsf, wob, wb, "b_bwd")
    gs_b_out, _ = _matmul_tn(y_b, dx2, gate1, True, "b_w_out_grad")
    gs_b_in, (gr_b_out,) = _matmul_tn(h1, dproj_b, jnp.ones((1, dproj_b.shape[1]), F32), False, "b_w_in_grad",
                                      _scatter_exchange([gs_b_out]))
    dproj_a, y_a, dgate0, dcb, dcw, (gr_b_in,) = _conv_mixer_bwd(
        proj_a, dx1, br_a, conv_a, conv_w, gate0, woa, "a_mixer_bwd", _scatter_exchange([gs_b_in]))
    gs_a_out, _ = _matmul_tn(y_a, dx1, gate0, True, "a_w_out_grad")
    gs_a_in, (gr_a_out,) = _matmul_tn(h0, dproj_a, jnp.ones((1, dproj_a.shape[1]), F32), False, "a_w_in_grad",
                                      _scatter_exchange([gs_a_out]))
    send_sems, recv_sems, gs_thru, land_thru, token = _scatter_start(gs_a_in, "a_w_in_scatter_start")
    dx0, dshift0, dscale0, dg0, _ = _matmul_nt_norm_bwd(dproj_a, wa, x0, dx1, g0, scale0 + token[0:1, 0:1], "a_in_bwd")
    gs_a_in, landed = _scatter_wait(send_sems, recv_sems, gs_thru, land_thru, dshift0, "a_w_in_scatter_wait")
    gr_a_in = lax.dynamic_update_slice_in_dim(landed, lax.dynamic_slice_in_dim(gs_a_in, me, 1, axis=0), me, axis=0)

    def big(parts, w, m, v, name, ex=None, after=()):
        shp = w.shape
        r2 = lambda t_: t_.reshape(-1, shp[-1])
        g, dl, nm, nv, ex_out = _adamw_reduce(parts, r2(w), r2(m), r2(v), name, ex, after)
        return tuple(t_.reshape(shp) for t_ in (g, dl, nm, nv)), ex_out

    pieces = [dshift0, dscale0, dgate0, dshift1, dscale1, dgate1, dg0, dg1, dcb, dcw[0:3], dlg, dlb, dfg,
              dbs[:, 0:GROUPS].T, loss_acc[0:1, 0:1]]
    shapes = [p.shape for p in pieces]
    g_send, g_recv, blocks_thru, lands_thru, g_token = _gather_start(
        [_pack(pieces), dws.astype(BF16).reshape(-1, PACK_W)], "small_gather_start")
    res = {}
    res["a_w_in"], _ = big([gr_a_in], a_w_in, m_a_w_in, v_a_w_in, "adamw_a_w_in", after=[g_token])
    res["b_w_in"], _ = big([gr_b_in], b_w_in, m_b_w_in, v_b_w_in, "adamw_b_w_in", after=[g_token])
    res["a_w_out"], _ = big([gr_a_out], a_w_out, m_a_w_out, v_a_w_out, "adamw_a_w_out", after=[g_token])
    res["b_w_out"], _ = big([gr_b_out], b_w_out, m_b_w_out, v_b_w_out, "adamw_b_w_out", after=[g_token])
    packed_all, dws_all = _gather_wait(
        g_send, g_recv, blocks_thru, lands_thru, [res[k][1] for k in ("a_w_in", "b_w_in", "a_w_out", "b_w_out")],
        "small_gather_wait")
    total, t_ws = _sum_devices([packed_all, dws_all], "sum_small_grads")
    total, t_ws = total.reshape(-1), t_ws.reshape(dws.shape)
    (t_sh0, t_sc0, t_ga0, t_sh1, t_sc1, t_ga1, t_g0, t_g1, t_cb, t_cw, t_lg, t_lb, t_fg, t_bs, t_loss) = _unpack(
        total, shapes)
    loss = t_loss.reshape(())
    grad_mod_b = jnp.concatenate([jnp.concatenate([t_sh0, t_sc0, t_ga0], axis=1),
                                  jnp.concatenate([t_sh1, t_sc1, t_ga1], axis=1)], axis=0)
    grad_norm_g = jnp.concatenate([t_g0, t_g1], axis=0)
    dmod_all = packed_all.reshape(NDEV, -1)[:, 0:6 * d].reshape(NDEV, 2, 3 * d).transpose(1, 0, 2)
    dmod_mine = lax.dynamic_slice_in_dim(dmod_all, me * w3, w3, axis=2)
    grad_mod_w = _mod_w_grad(c_all.T, dmod_mine, "mod_w_grad")
    grad_a_conv_w = lax.dynamic_slice_in_dim(t_cw, me * es, es, axis=1)
    grad_b_ln_g = lax.dynamic_slice_in_dim(t_lg, me * es, es, axis=1)
    grad_b_ln_b = lax.dynamic_slice_in_dim(t_lb, me * es, es, axis=1)

    res["mod_w"], _ = big([grad_mod_w.reshape(1, -1, w3)], mod_w, m_mod_w, v_mod_w, "adamw_mod_w")

    small_names = ["mod_b", "norm_g", "a_conv_w", "a_conv_b", "b_ln_g", "b_ln_b", "b_w_s", "b_b_s", "final_g"]
    small_g = [grad_mod_b, grad_norm_g, grad_a_conv_w, t_cb, grad_b_ln_g, grad_b_ln_b, t_ws, t_bs, t_fg]
    small_w = [mod_b, norm_g, a_conv_w, a_conv_b, b_ln_g, b_ln_b, b_w_s, b_b_s, final_g]
    small_m = [m_mod_b, m_norm_g, m_a_conv_w, m_a_conv_b, m_b_ln_g, m_b_ln_b, m_b_w_s, m_b_b_s, m_final_g]
    small_v = [v_mod_b, v_norm_g, v_a_conv_w, v_a_conv_b, v_b_ln_g, v_b_ln_b, v_b_w_s, v_b_b_s, v_final_g]
    as2d = lambda t_: t_.reshape(-1, t_.shape[-1])
    dls, nms, nvs = _adamw_small([as2d(t_) for t_ in small_g], [as2d(t_) for t_ in small_w],
                                 [as2d(t_) for t_ in small_m], [as2d(t_) for t_ in small_v], "adamw_small")
    for a, nme in enumerate(small_names):
        shp = small_w[a].shape
        res[nme] = (small_g[a].reshape(shp), dls[a].reshape(shp), nms[a].reshape(shp), nvs[a].reshape(shp))

    order = ["mod_w", "mod_b", "norm_g", "a_w_in", "a_conv_w", "a_conv_b", "a_w_out", "b_w_in", "b_ln_g", "b_ln_b",
             "b_w_s", "b_b_s", "b_w_out", "final_g"]
    return (loss, dx0.reshape(x.shape), *[res[k][0] for k in order], *[res[k][1] for k in order],
            *[res[k][2] for k in order], *[res[k][3] for k in order])
```

```python
import functools

import jax
import jax.numpy as jnp
from jax import lax
from jax.experimental import pallas as pl
from jax.experimental.pallas import tpu as pltpu

NDEV = 8
CHUNK = 128
GROUPS = 8
RMS_EPS = 1e-6
LN_EPS = 1e-5
ADAM_LR, ADAM_B1, ADAM_B2, ADAM_EPS, ADAM_WD, ADAM_STEP = 0.001, 0.9, 0.999, 1e-08, 0.01, 10
V7X_VMEM_BYTES = 64 * 1024 * 1024
VMEM_LIMIT = V7X_VMEM_BYTES - 8 * 1024 * 1024
TN_VMEM_BUDGET = 46 * 1024 * 1024
PACK_W = 1024
F32, BF16 = jnp.float32, jnp.bfloat16
MESH = pl.DeviceIdType.MESH
RSQRT2 = 0.7071067811865476
INV_SQRT_2PI = 0.3989422804014327
NT_DIMS = (((1,), (1,)), ((), ()))
TN_DIMS = (((0,), (0,)), ((), ()))


def _params(sem=None):
    return pltpu.CompilerParams(dimension_semantics=sem, vmem_limit_bytes=VMEM_LIMIT)


def _vmem():
    return pl.BlockSpec(memory_space=pltpu.VMEM)


def _hbm():
    return pl.BlockSpec(memory_space=pltpu.HBM)


def _full(shape):
    return pl.BlockSpec(shape, lambda *_: (0,) * len(shape))


def _pos():
    return lax.axis_index("x"), lax.axis_index("y"), lax.axis_index("c")


def _index(p):
    return 4 * p[0] + 2 * p[1] + p[2]


def _peer(k):
    x, y, c = _pos()
    return ((1 - x) if (k >> 2) & 1 else x, (1 - y) if (k >> 1) & 1 else y, (1 - c) if k & 1 else c)


def _silu(z):
    sg = jax.nn.sigmoid(z)
    return z * sg, sg * (1.0 + z * (1.0 - sg))


def _gelu(v):
    phi = 0.5 * (1.0 + lax.erf(v * RSQRT2))
    return v * phi, phi + v * (jnp.exp(-0.5 * v * v) * INV_SQRT_2PI)


def _colsum(v):
    return jnp.sum(v, axis=0, keepdims=True)


def _rowsum(v):
    return jnp.sum(v, axis=-1, keepdims=True)


def _gather_all_vmem(slab_ref, send_sems, recv_sems, base):
    me = _index(_pos())
    sends = []
    for k in range(1, NDEV):
        cp = pltpu.make_async_remote_copy(
            src_ref=slab_ref.at[me], dst_ref=slab_ref.at[me],
            send_sem=send_sems.at[base + k - 1], recv_sem=recv_sems.at[base + k - 1],
            device_id=_peer(k), device_id_type=MESH)
        cp.start()
        sends.append(cp)
    for k in range(1, NDEV):
        src = _index(_peer(k))
        pltpu.make_async_remote_copy(
            src_ref=slab_ref.at[src], dst_ref=slab_ref.at[src],
            send_sem=send_sems.at[base + k - 1], recv_sem=recv_sems.at[base + k - 1],
            device_id=_peer(k), device_id_type=MESH).wait_recv()
    for cp in sends:
        cp.wait_send()


def _mod_vectors(c, mod_w, mod_b, ex):
    n_layers, d, w3 = mod_w.shape
    r_in, r_out = len(ex.arrays), len(ex.out_shapes)

    def body(*refs):
        c_ref, mw_ref, mb_ref = refs[:3]
        ex_ins = refs[3:3 + r_in]
        mod_ref, call_ref = refs[3 + r_in:5 + r_in]
        ex_outs = refs[5 + r_in:5 + r_in + r_out]
        cslab, pslab, send_sems, recv_sems = refs[5 + r_in + r_out:9 + r_in + r_out]
        ex_sems = refs[9 + r_in + r_out:]
        ex.start(ex_ins, ex_outs, ex_sems)
        me = _index(_pos())
        cv = c_ref[...]
        cslab[me] = jnp.broadcast_to(cv * jax.nn.sigmoid(cv), (8, d))
        _gather_all_vmem(cslab, send_sems, recv_sems, 0)
        c_all = jnp.concatenate([cslab[k, 0:1, :] for k in range(NDEV)], axis=0)
        call_ref[...] = c_all
        for i in range(n_layers):
            pslab[me, i * NDEV:(i + 1) * NDEV, :] = jnp.dot(
                c_all, mw_ref[i], preferred_element_type=F32, precision=lax.Precision.HIGHEST)
        _gather_all_vmem(pslab, send_sems, recv_sems, NDEV - 1)
        for i in range(n_layers):
            for k in range(NDEV):
                mod_ref[i:i + 1, k * w3:(k + 1) * w3] = (
                    pslab[k, pl.ds(i * NDEV + me, 1), :] + mb_ref[i:i + 1, k * w3:(k + 1) * w3])
        for passing_on in ex.middles:
            passing_on(ex_ins, ex_outs, ex_sems)
        ex.finish(ex_ins, ex_outs, ex_sems)

    out = pl.pallas_call(
        body, name="mod_vectors",
        out_shape=(jax.ShapeDtypeStruct((n_layers, 3 * d), F32), jax.ShapeDtypeStruct((NDEV, d), F32), *ex.out_shapes),
        in_specs=[_vmem(), _vmem(), _vmem()] + [_hbm()] * r_in, out_specs=(_vmem(), _vmem(), *([_hbm()] * r_out)),
        scratch_shapes=[pltpu.VMEM((NDEV, 8, d), F32), pltpu.VMEM((NDEV, n_layers * NDEV, w3), F32),
                        pltpu.SemaphoreType.DMA((2 * (NDEV - 1),)), pltpu.SemaphoreType.DMA((2 * (NDEV - 1),)), *ex.sems],
        compiler_params=pltpu.CompilerParams(vmem_limit_bytes=VMEM_LIMIT),
    )(c, mod_w, mod_b, *ex.arrays)
    return out[0], out[1], out[2:]


class _Exchange:
    def __init__(self, arrays, out_shapes, sems, start, middles, finish):
        self.arrays, self.out_shapes, self.sems = list(arrays), list(out_shapes), list(sems)
        self.start, self.middles, self.finish = start, list(middles), finish


def _gather_exchange(shards, by_cols):
    n = len(shards)
    shapes = [sh.shape for sh in shards]

    def tools(ins, outs, sems):
        send_sems, recv_sems, local_sems = sems
        x, y, c = _pos()
        chips = [(1 - x, y), (x, 1 - y), (1 - x, 1 - y)]
        south = c == 0
        relayed = (jnp.where(south, 1 - x, x), jnp.where(south, y, 1 - y), c)
        relay_to = (jnp.where(south, x, 1 - x), jnp.where(south, 1 - y, y), c)

        def place(a, block):
            r, cc = shapes[a]
            if by_cols[a]:
                return outs[a].at[:, pl.ds(_index(block) * cc, cc)]
            return outs[a].at[pl.ds(_index(block) * r, r), :]

        def copy(a, k, block, to, src=None):
            dst = place(a, block)
            return pltpu.make_async_remote_copy(
                src_ref=dst if src is None else src, dst_ref=dst,
                send_sem=send_sems.at[a * 7 + k], recv_sem=recv_sems.at[a * 7 + k],
                device_id=to, device_id_type=MESH)

        mine = [pltpu.make_async_copy(ins[a], place(a, (x, y, c)), local_sems.at[a]) for a in range(n)]
        first = []
        for a in range(n):
            first.append(copy(a, 0, (x, y, c), (x, y, 1 - c), src=ins[a]))
            first += [copy(a, 1 + j, (x, y, c), (*chip, c), src=ins[a]) for j, chip in enumerate(chips[:2])]
        relays = [copy(a, 3, relayed, relay_to) for a in range(n)]
        passed = [copy(a, 4 + j, (*chip, c), (x, y, 1 - c)) for j, chip in enumerate(chips) for a in range(n)]
        return (x, y, c), chips, copy, mine, first, relays, passed

    def start(ins, outs, sems):
        _, _, _, mine, first, _, _ = tools(ins, outs, sems)
        for cp in mine + first:
            cp.start()

    def pass_neighbours(ins, outs, sems):
        (x, y, c), chips, copy, _, _, relays, passed = tools(ins, outs, sems)
        for j, chip in enumerate(chips[:2]):
            for a in range(n):
                copy(a, 1 + j, (*chip, c), (x, y, c)).wait_recv()
                passed[j * n + a].start()
        for cp in relays:
            cp.start()

    def pass_diagonal(ins, outs, sems):
        (x, y, c), chips, copy, _, _, _, passed = tools(ins, outs, sems)
        for a in range(n):
            copy(a, 3, (*chips[2], c), (x, y, c)).wait_recv()
            passed[2 * n + a].start()

    def finish(ins, outs, sems):
        (x, y, c), chips, copy, mine, first, relays, passed = tools(ins, outs, sems)
        for a in range(n):
            copy(a, 0, (x, y, 1 - c), (x, y, c)).wait_recv()
        for j, chip in enumerate(chips):
            for a in range(n):
                copy(a, 4 + j, (*chip, 1 - c), (x, y, c)).wait_recv()
        for cp in first + relays + passed:
            cp.wait_send()
        for cp in mine:
            cp.wait()

    out_shapes = [jax.ShapeDtypeStruct((r, NDEV * cc) if bc else (NDEV * r, cc), sh.dtype)
                  for (r, cc), bc, sh in zip(shapes, by_cols, shards)]
    sems = [pltpu.SemaphoreType.DMA((7 * n,)), pltpu.SemaphoreType.DMA((7 * n,)), pltpu.SemaphoreType.DMA((n,))]
    return _Exchange(shards, out_shapes, sems, start, [pass_neighbours, pass_diagonal], finish)


def _scatter_exchange(parts):
    n = len(parts)

    def tools(ins, outs, sems):
        send_sems, recv_sems, local_sems = sems
        me = _index(_pos())
        mine = [pltpu.make_async_copy(ins[a].at[me], outs[a].at[me], local_sems.at[a]) for a in range(n)]
        sends, arrivals = [], []
        for k in range(1, NDEV):
            peer = _peer(k)
            for a in range(n):
                pair = dict(send_sem=send_sems.at[a * 7 + k - 1], recv_sem=recv_sems.at[a * 7 + k - 1],
                            device_id=peer, device_id_type=MESH)
                sends.append(pltpu.make_async_remote_copy(src_ref=ins[a].at[_index(peer)], dst_ref=outs[a].at[me], **pair))
                slot = outs[a].at[_index(peer)]
                arrivals.append(pltpu.make_async_remote_copy(src_ref=slot, dst_ref=slot, **pair))
        return mine, sends, arrivals

    def start(ins, outs, sems):
        mine, sends, _ = tools(ins, outs, sems)
        for cp in mine + sends:
            cp.start()

    def finish(ins, outs, sems):
        mine, sends, arrivals = tools(ins, outs, sems)
        for cp in arrivals:
            cp.wait_recv()
        for cp in sends:
            cp.wait_send()
        for cp in mine:
            cp.wait()

    out_shapes = [jax.ShapeDtypeStruct(p.shape, p.dtype) for p in parts]
    sems = [pltpu.SemaphoreType.DMA((7 * n,)), pltpu.SemaphoreType.DMA((7 * n,)), pltpu.SemaphoreType.DMA((n,))]
    return _Exchange(parts, out_shapes, sems, start, [], finish)


def _scatter_start(part, name):
    def body(part_ref, land_ref, send_sems, recv_sems, part_thru, land_thru, token):
        me = _index(_pos())
        for k in range(1, NDEV):
            peer = _peer(k)
            pltpu.make_async_remote_copy(
                src_ref=part_ref.at[_index(peer)], dst_ref=land_ref.at[me],
                send_sem=send_sems.at[k - 1], recv_sem=recv_sems.at[k - 1],
                device_id=peer, device_id_type=MESH).start()
        token[...] = jnp.zeros_like(token)

    sem = pl.BlockSpec(memory_space=pltpu.SEMAPHORE)
    return pl.pallas_call(
        body, name=name,
        out_shape=(pltpu.SemaphoreType.DMA((NDEV - 1,)), pltpu.SemaphoreType.DMA((NDEV - 1,)),
                   pltpu.HBM(part.shape, part.dtype), pltpu.HBM(part.shape, part.dtype),
                   jax.ShapeDtypeStruct((8, 128), F32)),
        in_specs=(_hbm(), _hbm()), out_specs=(sem, sem, _hbm(), _hbm(), _vmem()),
        input_output_aliases={0: 2, 1: 3},
        compiler_params=pltpu.CompilerParams(has_side_effects=pltpu.SideEffectType.DATAFLOW_SIDE_EFFECTING),
    )(pltpu.with_memory_space_constraint(part, pltpu.HBM),
      pltpu.with_memory_space_constraint(lax.empty(part.shape, part.dtype), pltpu.HBM))


def _scatter_wait(send_sems, recv_sems, part_thru, land_thru, after, name):
    def body(part_ref, land_ref, send_sems, recv_sems, after_ref, part_dead, got_ref):
        for k in range(1, NDEV):
            peer = _peer(k)
            slot = land_ref.at[_index(peer)]
            copy = pltpu.make_async_remote_copy(
                src_ref=part_ref.at[_index(peer)], dst_ref=slot,
                send_sem=send_sems.at[k - 1], recv_sem=recv_sems.at[k - 1],
                device_id=peer, device_id_type=MESH)
            copy.wait_send()
            copy.wait_recv()

    sem = pl.BlockSpec(memory_space=pltpu.SEMAPHORE)
    return pl.pallas_call(
        body, name=name,
        out_shape=(pltpu.HBM(part_thru.shape, part_thru.dtype), pltpu.HBM(land_thru.shape, land_thru.dtype)),
        in_specs=(_hbm(), _hbm(), sem, sem, pl.BlockSpec(memory_space=pl.ANY)), out_specs=(_hbm(), _hbm()),
        input_output_aliases={0: 0, 1: 1},
        compiler_params=pltpu.CompilerParams(has_side_effects=pltpu.SideEffectType.DATAFLOW_SIDE_EFFECTING),
    )(part_thru, land_thru, send_sems, recv_sems, after)


def _gather_start(blocks, name):
    n = len(blocks)

    def body(*refs):
        b_refs, l_refs = refs[:n], refs[n:2 * n]
        send_sems, recv_sems = refs[2 * n:2 * n + 2]
        token = refs[-1]
        me = _index(_pos())
        for k in range(1, NDEV):
            peer = _peer(k)
            for a in range(n):
                pltpu.make_async_remote_copy(
                    src_ref=b_refs[a], dst_ref=l_refs[a].at[me],
                    send_sem=send_sems.at[a * (NDEV - 1) + k - 1], recv_sem=recv_sems.at[a * (NDEV - 1) + k - 1],
                    device_id=peer, device_id_type=MESH).start()
        token[...] = jnp.zeros_like(token)

    sem = pl.BlockSpec(memory_space=pltpu.SEMAPHORE)
    lands =[lax.empty((NDEV, *b.shape), b.dtype) for b in blocks]
    out = pl.pallas_call(
        body, name=name,
        out_shape=(pltpu.SemaphoreType.DMA((n * (NDEV - 1),)), pltpu.SemaphoreType.DMA((n * (NDEV - 1),)),
                   *[pltpu.HBM(b.shape, b.dtype) for b in blocks], *[pltpu.HBM(l.shape, l.dtype) for l in lands],
                   jax.ShapeDtypeStruct((8, 128), F32)),
        in_specs=tuple([_hbm()] * (2 * n)), out_specs=(sem, sem, *([_hbm()] * (2 * n)), _vmem()),
        input_output_aliases={a: 2 + a for a in range(2 * n)},
        compiler_params=pltpu.CompilerParams(has_side_effects=pltpu.SideEffectType.DATAFLOW_SIDE_EFFECTING),
    )(*[pltpu.with_memory_space_constraint(b, pltpu.HBM) for b in blocks],
      *[pltpu.with_memory_space_constraint(l, pltpu.HBM) for l in lands])
    return out[0], out[1], out[2:2 + n], out[2 + n:2 + 2 * n], out[-1]


def _gather_wait(send_sems, recv_sems, blocks_thru, lands_thru, after, name):
    n = len(blocks_thru)

    def body(*refs):
        b_refs, l_refs = refs[:n], refs[n:2 * n]
        send_sems, recv_sems = refs[2 * n:2 * n + 2]
        for k in range(1, NDEV):
            peer = _peer(k)
            for a in range(n):
                copy = pltpu.make_async_remote_copy(
                    src_ref=b_refs[a], dst_ref=l_refs[a].at[_index(peer)],
                    send_sem=send_sems.at[a * (NDEV - 1) + k - 1], recv_sem=recv_sems.at[a * (NDEV - 1) + k - 1],
                    device_id=peer, device_id_type=MESH)
                copy.wait_send()
                copy.wait_recv()

    sem = pl.BlockSpec(memory_space=pltpu.SEMAPHORE)
    out = pl.pallas_call(
        body, name=name,
        out_shape=tuple(pltpu.HBM(l.shape, l.dtype) for l in lands_thru),
        in_specs=(*([_hbm()] * (2 * n)), sem, sem, *([pl.BlockSpec(memory_space=pl.ANY)] * len(after))),
        out_specs=tuple([_hbm()] * n),
        input_output_aliases={n + a: a for a in range(n)},
        compiler_params=pltpu.CompilerParams(has_side_effects=pltpu.SideEffectType.DATAFLOW_SIDE_EFFECTING),
    )(*blocks_thru, *lands_thru, send_sems, recv_sems, *after)
    return list(out)


def _carry(ex, body, n_in, n_out, first, middle, last):
    if ex is None:
        return body
    r_in, r_out = len(ex.arrays), len(ex.out_shapes)

    def wrapped(*refs):
        ins, rins = refs[:n_in], refs[n_in:n_in + r_in]
        outs = refs[n_in + r_in:n_in + r_in + n_out]
        routs = refs[n_in + r_in + n_out:n_in + r_in + n_out + r_out]
        rest = refs[n_in + r_in + n_out + r_out:]
        scratch, sems = rest[:len(rest) - len(ex.sems)], rest[len(rest) - len(ex.sems):]

        @pl.when(first())
        def _():
            ex.start(rins, routs, sems)

        for passing_on, at_step in zip(ex.middles, middle or []):
            pl.when(at_step())(functools.partial(passing_on, rins, routs, sems))

        body(*ins, *outs, *scratch)

        @pl.when(last())
        def _():
            ex.finish(rins, routs, sems)

    return wrapped


def _carried(ex):
    if ex is None:
        return [], [], [], [], []
    return ex.arrays, [_hbm()] * len(ex.arrays), ex.out_shapes, [_hbm()] * len(ex.out_shapes), ex.sems


def _resident(shape):
    return pl.BlockSpec(shape, lambda *_: (0,) * len(shape), pipeline_mode=pl.Buffered(1))


def _norm_modulate(x_ref, g_ref, sc_ref, sh_ref):
    xv = x_ref[...]
    r = lax.rsqrt(jnp.mean(xv * xv, axis=-1, keepdims=True) + RMS_EPS)
    return ((xv * r) * g_ref[...] * (1.0 + sc_ref[...]) + sh_ref[...]).astype(BF16)


def _conv_taps(cx, t6, t7, row):
    p1 = jnp.where(row == 0, t7, pltpu.roll(cx, 1, 0))
    p2 = jnp.where(row == 0, t6, jnp.where(row == 1, t7, pltpu.roll(cx, 2, 0)))
    return p1, p2


def _layer_a_fwd(x, g, scale, shift, gate, wi, cw, cb, wo, name, ex=None):
    s, d = x.shape
    e = wo.shape[0]
    t = min(s, 256)
    n_t = s // t
    cwid = min(e, 512)

    def body(x_ref, g_ref, sc_ref, sh_ref, gate_ref, wi_ref, cw_ref, cb_ref, wo_ref,
             proj_ref, h_ref, x1_ref, br_ref, conv_ref, y_scr, tail_scr):
        @pl.when(pl.program_id(0) == 0)
        def _():
            tail_scr[...] = jnp.zeros_like(tail_scr)
        h_ref[...] = _norm_modulate(x_ref, g_ref, sc_ref, sh_ref)
        row = lax.broadcasted_iota(jnp.int32, (t, cwid), 0)

        def project(c0):
            v = jnp.dot(h_ref[...], wi_ref[:, c0:c0 + cwid], preferred_element_type=F32)
            proj_ref[:, c0:c0 + cwid] = v.astype(BF16)
            return v

        for c0 in range(0, e, cwid):
            sl = slice(c0, c0 + cwid)
            bg, z = project(c0), project(3 * e + c0)
            cx = project(e + c0) * project(2 * e + c0)
            p1, p2 = _conv_taps(cx, tail_scr[6:7, sl], tail_scr[7:8, sl], row)
            conv = cb_ref[:, sl] + cw_ref[2:3, sl] * cx + cw_ref[0:1, sl] * p2 + cw_ref[1:2, sl] * p1
            conv_ref[:, sl] = conv.astype(BF16)
            y_scr[:, sl] = (_silu(z)[0] * bg * conv).astype(BF16)
            tail_scr[:, sl] = cx[t - 8:t, :]
        br = jnp.dot(y_scr[...], wo_ref[...], preferred_element_type=F32)
        x1_ref[...] = x_ref[...] + gate_ref[...] * br
        br_ref[...] = br.astype(BF16)

    step = lambda k: (lambda: pl.program_id(0) == k)
    body = _carry(ex, body, 9, 5, step(0), [step(n_t // 3), step((2 * n_t) // 3)], step(n_t - 1))
    ex_args, ex_in, ex_shapes, ex_out, ex_sems = _carried(ex)
    tok = pl.BlockSpec((t, d), lambda i: (i, 0))
    out = pl.pallas_call(
        body, name=name, grid=(n_t,),
        out_shape=(jax.ShapeDtypeStruct((s, 4 * e), BF16), jax.ShapeDtypeStruct((s, d), BF16),
                   jax.ShapeDtypeStruct((s, d), F32), jax.ShapeDtypeStruct((s, d), BF16),
                   jax.ShapeDtypeStruct((s, e), BF16), *ex_shapes),
        in_specs=[tok, _full((1, d)), _full((1, d)), _full((1, d)), _full((1, d)), _resident((d, 4 * e)),
                  _full((3, e)), _full((1, e)), _resident((e, d)), *ex_in],
        out_specs=(pl.BlockSpec((t, 4 * e), lambda i: (i, 0)), tok, tok, tok,
                   pl.BlockSpec((t, e), lambda i: (i, 0)), *ex_out),
        scratch_shapes=[pltpu.VMEM((t, e), BF16), pltpu.VMEM((8, e), F32), *ex_sems],
        compiler_params=_params(("arbitrary",)),
    )(x, g, scale, shift, gate, wi, cw, cb, wo, *ex_args)
    return (*out[:5], out[5:])


def _ln_stats(v_of, v_scr, t, e):
    gw = e // GROUPS
    s1 = jnp.zeros((t, 1), F32)
    for g in range(GROUPS):
        v = v_of(g)
        v_scr[:, g * gw:(g + 1) * gw] = v
        s1 = s1 + _rowsum(v)
    mu = s1 * (1.0 / e)
    s2 = jnp.zeros((t, 1), F32)
    for g in range(GROUPS):
        dv = v_scr[:, g * gw:(g + 1) * gw] - mu
        s2 = s2 + _rowsum(dv * dv)
    return mu, lax.rsqrt(s2 * (1.0 / e) + LN_EPS)


def _layer_b_fwd_loss(x1, tgt, g1, scale, shift, gate, fg, wi, lng, lnb, wt, bsf, wo, name, ex=None):
    s, d = x1.shape
    e = wo.shape[0]
    gw = e // GROUPS
    t = min(s, 256)
    n_t = s // t

    def body(x1_ref, tgt_ref, g_ref, sc_ref, sh_ref, gate_ref, fg_ref, wi_ref, lng_ref, lnb_ref, wt_ref, bsf_ref, wo_ref,
             proj_ref, h_ref, dx2_ref, loss_ref, dfg_ref, dgate_ref, v_ref, dgv_ref, v_scr, y_scr):
        @pl.when(pl.program_id(0) == 0)
        def _():
            loss_ref[...] = jnp.zeros_like(loss_ref)
            dfg_ref[...] = jnp.zeros_like(dfg_ref)
            dgate_ref[...] = jnp.zeros_like(dgate_ref)
        h_ref[...] = _norm_modulate(x1_ref, g_ref, sc_ref, sh_ref)

        def project(c0):
            v = jnp.dot(h_ref[...], wi_ref[:, c0:c0 + gw], preferred_element_type=F32)
            proj_ref[:, c0:c0 + gw] = v.astype(BF16)
            return v

        def gelu_v(g):
            gs = slice(g * gw, (g + 1) * gw)
            v, dgv = _gelu(project(e + g * gw))
            v_ref[:, gs] = v.astype(BF16)
            dgv_ref[:, gs] = dgv.astype(BF16)
            return v

        mu, rs = _ln_stats(gelu_v, v_scr, t, e)
        for g in range(GROUPS):
            gs = slice(g * gw, (g + 1) * gw)
            vn = (((v_scr[:, gs] - mu) * rs) * lng_ref[:, gs] + lnb_ref[:, gs]).astype(BF16)
            u = _gelu(project(g * gw))[0]
            sz = _silu(project(2 * e + g * gw))[0]
            for ch in range(t // CHUNK):
                rows = slice(ch * CHUNK, (ch + 1) * CHUNK)
                mixed = jnp.dot(wt_ref[g], vn[rows], preferred_element_type=F32) + bsf_ref[:, gs]
                y_scr[rows, gs] = (sz[rows] * (u[rows] * mixed)).astype(BF16)
        br = jnp.dot(y_scr[...], wo_ref[...], preferred_element_type=F32)
        x2 = x1_ref[...] + gate_ref[...] * br
        r2 = lax.rsqrt(jnp.mean(x2 * x2, axis=-1, keepdims=True) + RMS_EPS)
        xn = x2 * r2
        diff = xn * fg_ref[...] - tgt_ref[...]
        loss_ref[...] += jnp.broadcast_to(0.5 * _colsum(jnp.mean(diff * diff, axis=-1, keepdims=True)), loss_ref.shape)
        dout = diff * (1.0 / d)
        dfg_ref[...] += _colsum(dout * xn)
        dxn = dout * fg_ref[...]
        dx2 = r2 * (dxn - xn * jnp.mean(dxn * xn, axis=-1, keepdims=True))
        dx2_ref[...] = dx2
        dgate_ref[...] += _colsum(dx2 * br)

    step = lambda k: (lambda: pl.program_id(0) == k)
    body = _carry(ex, body, 13, 8, step(0), [step(n_t // 3), step((2 * n_t) // 3)], step(n_t - 1))
    ex_args, ex_in, ex_shapes, ex_out, ex_sems = _carried(ex)
    tok = pl.BlockSpec((t, d), lambda i: (i, 0))
    vec = _full((1, d))
    out = pl.pallas_call(
        body, name=name, grid=(n_t,),
        out_shape=(jax.ShapeDtypeStruct((s, 3 * e), BF16), jax.ShapeDtypeStruct((s, d), BF16),
                   jax.ShapeDtypeStruct((s, d), F32), jax.ShapeDtypeStruct((8, 128), F32),
                   jax.ShapeDtypeStruct((1, d), F32), jax.ShapeDtypeStruct((1, d), F32),
                   jax.ShapeDtypeStruct((s, e), BF16), jax.ShapeDtypeStruct((s, e), BF16), *ex_shapes),
        in_specs=[tok, tok, vec, vec, vec, vec, vec, _resident((d, 3 * e)), _full((1, e)), _full((1, e)),
                  _full((GROUPS, CHUNK, CHUNK)), _resident((CHUNK, e)), _resident((e, d)), *ex_in],
        out_specs=(pl.BlockSpec((t, 3 * e), lambda i: (i, 0)), tok, tok, _full((8, 128)), vec, vec,
                   pl.BlockSpec((t, e), lambda i: (i, 0)), pl.BlockSpec((t, e), lambda i: (i, 0)), *ex_out),
        scratch_shapes=[pltpu.VMEM((t, e), F32), pltpu.VMEM((t, e), BF16), *ex_sems],
        compiler_params=_params(("arbitrary",)),
    )(x1, tgt, g1, scale, shift, gate, fg, wi, lng, lnb, wt, bsf, wo, *ex_args)
    return (*out[:8], out[8:])


def _norm_modulate_bwd(dh, x_ref, dres_ref, g_ref, sc_ref, dx_ref, dsh_ref, p_scr):
    xv = x_ref[...]
    r = lax.rsqrt(jnp.mean(xv * xv, axis=-1, keepdims=True) + RMS_EPS)
    xn = xv * r
    dsh_ref[...] += _colsum(dh)
    p_scr[...] += _colsum(dh * xn)
    dxn = dh * (g_ref[...] * (1.0 + sc_ref[...]))
    dx_ref[...] = r * (dxn - xn * jnp.mean(dxn * xn, axis=-1, keepdims=True)) + dres_ref[...]


def _layer_b_bwd(proj, v_act, dgv, dx2, x1, gate, g1, scale, lng, lnb, wt, wtt, bsf, wo, wi, name):
    s, e3 = proj.shape
    e = e3 // 3
    d = dx2.shape[1]
    gw = e // GROUPS
    t = min(s, 256)
    n_t = s // t

    def body(pu_ref, pz_ref, v_ref, dgv_ref, dx_ref, x1_ref, gate_ref, g_ref, sc_ref, lng_ref, lnb_ref, wt_ref, wtt_ref,
             bsf_ref, wo_ref, wi_ref,
             dp_ref, y_ref, dx1_ref, dws_ref, dbs_ref, dlg_ref, dlb_ref, dsh_ref, dsc_ref, dg_ref,
             v_scr, dbr_scr, dvn_scr, dbs_scr, p_scr, dy_scr, vn_scr, mixed_scr):
        @pl.when(pl.program_id(0) == 0)
        def _():
            dws_ref[...] = jnp.zeros_like(dws_ref)
            dlg_ref[...] = jnp.zeros_like(dlg_ref)
            dlb_ref[...] = jnp.zeros_like(dlb_ref)
            dsh_ref[...] = jnp.zeros_like(dsh_ref)
            dbs_scr[...] = jnp.zeros_like(dbs_scr)
            p_scr[...] = jnp.zeros_like(p_scr)
        dbr_scr[...] = (dx_ref[...] * gate_ref[...]).astype(BF16)
        mu, rs = _ln_stats(lambda g: v_ref[:, g * gw:(g + 1) * gw].astype(F32), v_scr, t, e)
        tril = (lax.broadcasted_iota(jnp.int32, (CHUNK, CHUNK), 0) >= lax.broadcasted_iota(jnp.int32, (CHUNK, CHUNK), 1))
        c1 = jnp.zeros((t, 1), F32)
        c2 = jnp.zeros((t, 1), F32)
        span = 2
        kw = span * gw
        dh = jnp.zeros((t, d), F32)

        def through_w_in(c0):
            return lax.dot_general(dp_ref[:, c0:c0 + kw], wi_ref[:, c0:c0 + kw], NT_DIMS, preferred_element_type=F32)

        dy_scr[...] = lax.dot_general(dbr_scr[...], wo_ref[...], NT_DIMS, preferred_element_type=F32)
        for g in range(GROUPS):
            gs = slice(g * gw, (g + 1) * gw)
            vhat = (v_scr[:, gs] - mu) * rs
            v_scr[:, gs] = vhat
            vn = (vhat * lng_ref[:, gs] + lnb_ref[:, gs]).astype(BF16)
            vn_scr[:, gs] = vn
            for ch in range(t // CHUNK):
                rows = slice(ch * CHUNK, (ch + 1) * CHUNK)
                mixed_scr[rows, gs] = jnp.dot(wt_ref[g], vn[rows], preferred_element_type=F32) + bsf_ref[:, gs]
        for g in range(GROUPS):
            gs = slice(g * gw, (g + 1) * gw)
            vhat = v_scr[:, gs]
            lg = lng_ref[:, gs]
            for ch in range(t // CHUNK):
                rows = slice(ch * CHUNK, (ch + 1) * CHUNK)
                mixed = mixed_scr[rows, gs]
                u, dgu = _gelu(pu_ref[rows, gs].astype(F32))
                sz, dsz = _silu(pz_ref[rows, gs].astype(F32))
                sgate = u * mixed
                y_ref[rows, gs] = (sz * sgate).astype(BF16)
                dy = dy_scr[rows, gs]
                dp_ref[rows, 2 * e + g * gw:2 * e + (g + 1) * gw] = (dy * sgate * dsz).astype(BF16)
                ds = dy * sz
                dp_ref[rows, gs] = (ds * mixed * dgu).astype(BF16)
                dm = ds * u
                dbs_scr[:, gs] += dm
                dmb = dm.astype(BF16)
                dws_ref[g] += jnp.where(tril, lax.dot_general(dmb, vn_scr[rows, gs], NT_DIMS, preferred_element_type=F32), 0.0)
                dvn_scr[rows, gs] = jnp.dot(wtt_ref[g], dmb, preferred_element_type=F32)
            dvn = dvn_scr[:, gs]
            dlb_ref[:, gs] += _colsum(dvn)
            dlg_ref[:, gs] += _colsum(dvn * vhat)
            dvh = dvn * lg
            c1 = c1 + _rowsum(dvh)
            c2 = c2 + _rowsum(dvh * vhat)
            if g % span == span - 1:
                dh = dh + through_w_in(g * gw + gw - kw) + through_w_in(2 * e + g * gw + gw - kw)
        c1 = c1 * (1.0 / e)
        c2 = c2 * (1.0 / e)
        for g in range(GROUPS):
            gs = slice(g * gw, (g + 1) * gw)
            dv = rs * (dvn_scr[:, gs] * lng_ref[:, gs] - c1 - v_scr[:, gs] * c2)
            dp_ref[:, e + g * gw:e + (g + 1) * gw] = (dv * dgv_ref[:, gs]).astype(BF16)
            if g % span == span - 1:
                dh = dh + through_w_in(e + g * gw + gw - kw)
        _norm_modulate_bwd(dh, x1_ref, dx_ref, g_ref, sc_ref, dx1_ref, dsh_ref, p_scr)

        @pl.when(pl.program_id(0) == n_t - 1)
        def _():
            lane = lax.broadcasted_iota(jnp.int32, (CHUNK, 128), 1)
            acc = jnp.zeros((CHUNK, 128), F32)
            for g in range(GROUPS):
                acc = acc + jnp.where(lane == g, _rowsum(dbs_scr[:, g * gw:(g + 1) * gw]), 0.0)
            dbs_ref[...] = acc
            dsc_ref[...] = p_scr[...] * g_ref[...]
            dg_ref[...] = p_scr[...] * (1.0 + sc_ref[...])

    tok = pl.BlockSpec((t, d), lambda i: (i, 0))
    vec, evec, ws = _full((1, d)), _full((1, e)), _full((GROUPS, CHUNK, CHUNK))
    vshape = jax.ShapeDtypeStruct((1, d), F32)
    return pl.pallas_call(
        body, name=name, grid=(n_t,),
        out_shape=(jax.ShapeDtypeStruct((s, e3), BF16), jax.ShapeDtypeStruct((s, e), BF16), jax.ShapeDtypeStruct((s, d), F32),
                   jax.ShapeDtypeStruct((GROUPS, CHUNK, CHUNK), F32), jax.ShapeDtypeStruct((CHUNK, 128), F32),
                   jax.ShapeDtypeStruct((1, e), F32), jax.ShapeDtypeStruct((1, e), F32), vshape, vshape, vshape),
        in_specs=[pl.BlockSpec((t, e), lambda i: (i, 0)), pl.BlockSpec((t, e), lambda i: (i, 2)),
                  pl.BlockSpec((t, e), lambda i: (i, 0)), pl.BlockSpec((t, e), lambda i: (i, 0)),
                  tok, tok, vec, vec, vec, evec, evec, ws, ws,
                  _resident((CHUNK, e)), _resident((e, d)), _resident((d, e3))],
        out_specs=(pl.BlockSpec((t, e3), lambda i: (i, 0)), pl.BlockSpec((t, e), lambda i: (i, 0)), tok,
                   ws, _full((CHUNK, 128)), evec, evec, vec, vec, vec),
        scratch_shapes=[pltpu.VMEM((t, e), F32), pltpu.VMEM((t, d), BF16),
                        pltpu.VMEM((t, e), F32), pltpu.VMEM((CHUNK, e), F32), pltpu.VMEM((1, d), F32),
                        pltpu.VMEM((t, e), F32), pltpu.VMEM((t, e), BF16), pltpu.VMEM((t, e), F32)],
        compiler_params=_params(("arbitrary",)),
    )(proj, proj, v_act, dgv, dx2, x1, gate, g1, scale, lng, lnb, wt, wtt, bsf, wo, wi)


def _conv_mixer_bwd(proj, dx1, br, conv_a, cw, gate, wo, name, ex=None):
    s, e4 = proj.shape
    e = e4 // 4
    d = dx1.shape[1]
    t = min(s, 256)
    n_t = s // t
    cwid = min(e, 512)

    def body(p_ref, dx_ref, br_ref, conv_ref, cw_ref, gate_ref, wo_ref,
             dp_ref, y_ref, dgate_ref, dcb_ref, dcw_ref, dy_scr, head_scr):
        i = pl.program_id(0)

        @pl.when(i == 0)
        def _():
            dgate_ref[...] = jnp.zeros_like(dgate_ref)
            dcb_ref[...] = jnp.zeros_like(dcb_ref)
            dcw_ref[...] = jnp.zeros_like(dcw_ref)
            head_scr[...] = jnp.zeros_like(head_scr)
        dx = dx_ref[...]
        dgate_ref[...] += _colsum(dx * br_ref[...].astype(F32))
        dy_scr[...] = lax.dot_general((dx * gate_ref[...]).astype(BF16), wo_ref[...], NT_DIMS,
                                      preferred_element_type=F32)
        row = lax.broadcasted_iota(jnp.int32, (t, cwid), 0)
        for c0 in range(0, e, cwid):
            sl = slice(c0, c0 + cwid)
            bg = p_ref[:, c0:c0 + cwid].astype(F32)
            cg = p_ref[:, e + c0:e + c0 + cwid].astype(F32)
            xin = p_ref[:, 2 * e + c0:2 * e + c0 + cwid].astype(F32)
            z = p_ref[:, 3 * e + c0:3 * e + c0 + cwid].astype(F32)
            cx = cg * xin
            w0, w1, w2 = cw_ref[0:1, sl], cw_ref[1:2, sl], cw_ref[2:3, sl]
            conv = conv_ref[:, sl].astype(F32)
            sz, dsz = _silu(z)
            dy = dy_scr[:, sl]
            y_ref[:, sl] = (sz * bg * conv).astype(BF16)
            dp_ref[:, 3 * e + c0:3 * e + c0 + cwid] = (dy * bg * conv * dsz).astype(BF16)
            dp_ref[:, c0:c0 + cwid] = (dy * sz * conv).astype(BF16)
            dconv = dy * sz * bg
            h0, h1 = head_scr[0:1, sl], head_scr[1:2, sl]
            n1 = jnp.where(row == t - 1, h0, pltpu.roll(dconv, t - 1, 0))
            n2 = jnp.where(row == t - 2, h0, jnp.where(row == t - 1, h1, pltpu.roll(dconv, t - 2, 0)))
            dcb_ref[:, sl] += _colsum(dconv)
            dcw_ref[2:3, sl] += _colsum(dconv * cx)
            dcw_ref[1:2, sl] += _colsum(n1 * cx)
            dcw_ref[0:1, sl] += _colsum(n2 * cx)
            dcx = w2 * dconv + w1 * n1 + w0 * n2
            dp_ref[:, e + c0:e + c0 + cwid] = (dcx * xin).astype(BF16)
            dp_ref[:, 2 * e + c0:2 * e + c0 + cwid] = (dcx * cg).astype(BF16)
            head_scr[:, sl] = dconv[0:8, :]

    body = _carry(ex, body, 7, 5, lambda: pl.program_id(0) == 0, None, lambda: pl.program_id(0) == n_t - 1)
    ex_args, ex_in, ex_shapes, ex_out, ex_sems = _carried(ex)
    rev = lambda i: (n_t - 1 - i, 0)
    out = pl.pallas_call(
        body, name=name, grid=(n_t,),
        out_shape=(jax.ShapeDtypeStruct((s, e4), BF16), jax.ShapeDtypeStruct((s, e), BF16),
                   jax.ShapeDtypeStruct((1, d), F32), jax.ShapeDtypeStruct((1, e), F32), jax.ShapeDtypeStruct((8, e), F32),
                   *ex_shapes),
        in_specs=[pl.BlockSpec((t, e4), rev), pl.BlockSpec((t, d), rev), pl.BlockSpec((t, d), rev),
                  pl.BlockSpec((t, e), rev), _full((3, e)), _full((1, d)), _full((e, d)), *ex_in],
        out_specs=(pl.BlockSpec((t, e4), rev), pl.BlockSpec((t, e), rev), _full((1, d)), _full((1, e)), _full((8, e)),
                   *ex_out),
        scratch_shapes=[pltpu.VMEM((t, e), F32), pltpu.VMEM((8, e), F32), *ex_sems],
        compiler_params=_params(("arbitrary",)),
    )(proj, dx1, br, conv_a, cw, gate, wo, *ex_args)
    return (*out[:5], out[5:])


def _matmul_nt_norm_bwd(dproj, w, xin, dres, g, scale, name, ex=None):
    s, d = xin.shape
    n = w.shape[1]
    tm = min(s, 512)
    n_i = s // tm

    def body(dp_ref, w_ref, x_ref, dres_ref, g_ref, sc_ref, dx_ref, dsh_ref, dsc_ref, dg_ref, p_scr):
        i = pl.program_id(0)

        @pl.when(i == 0)
        def _():
            dsh_ref[...] = jnp.zeros_like(dsh_ref)
            p_scr[...] = jnp.zeros_like(p_scr)
        dh = lax.dot_general(dp_ref[...], w_ref[...], NT_DIMS, preferred_element_type=F32)
        _norm_modulate_bwd(dh, x_ref, dres_ref, g_ref, sc_ref, dx_ref, dsh_ref, p_scr)

        @pl.when(i == n_i - 1)
        def _():
            dsc_ref[...] = p_scr[...] * g_ref[...]
            dg_ref[...] = p_scr[...] * (1.0 + sc_ref[...])

    body = _carry(ex, body, 6, 4, lambda: pl.program_id(0) == 0, None, lambda: pl.program_id(0) == n_i - 1)
    ex_args, ex_in, ex_shapes, ex_out, ex_sems = _carried(ex)
    tok = pl.BlockSpec((tm, d), lambda i: (i, 0))
    vec = pl.BlockSpec((1, d), lambda i: (0, 0))
    vshape = jax.ShapeDtypeStruct((1, d), F32)
    out = pl.pallas_call(
        body, name=name, grid=(n_i,),
        out_shape=(jax.ShapeDtypeStruct((s, d), F32), vshape, vshape, vshape, *ex_shapes),
        in_specs=[pl.BlockSpec((tm, n), lambda i: (i, 0)), _resident((d, n)), tok, tok, vec, vec, *ex_in],
        out_specs=(tok, vec, vec, vec, *ex_out),
        scratch_shapes=[pltpu.VMEM((1, d), F32), *ex_sems],
        compiler_params=_params(("arbitrary",)),
    )(dproj, w, xin, dres, g, scale, *ex_args)
    return (*out[:4], out[4:])


def _matmul_tn(a, b, colscale, rows_split, name, ex=None, a_cols=None):
    s, m = a.shape
    a_blk = 0
    if a_cols is not None:
        a_blk, m = a_cols
    n = b.shape[1]
    n_j, tn = (1, n) if rows_split else (NDEV, n // NDEV)
    fixed = (4 + 4 + 2 * 2) * m * tn
    tk = s
    while fixed + 2 * tk * (2 * m + b.dtype.itemsize * tn) > TN_VMEM_BUDGET:
        tk //= 2
    n_k = s // tk

    def body(a_ref, b_ref, cs_ref, o_ref, acc):
        k = pl.program_id(1)
        part = lax.dot_general(a_ref[...], b_ref[...].astype(BF16), TN_DIMS, preferred_element_type=F32)
        if n_k == 1:
            o_ref[...] = (part * cs_ref[...]).astype(BF16)
            return

        @pl.when(k == 0)
        def _():
            acc[...] = part

        @pl.when((k > 0) & (k < n_k - 1))
        def _():
            acc[...] += part

        @pl.when(k == n_k - 1)
        def _():
            o_ref[...] = ((acc[...] + part) * cs_ref[...]).astype(BF16)

    at = lambda j, k: (pl.program_id(0) == j) & (pl.program_id(1) == k)
    body = _carry(ex, body, 3, 1, lambda: at(0, 0), None, lambda: at(n_j - 1, n_k - 1))
    ex_args, ex_in, ex_shapes, ex_out, ex_sems = _carried(ex)
    out = pl.pallas_call(
        body, name=name, grid=(n_j, n_k),
        out_shape=(jax.ShapeDtypeStruct((n_j, m, tn), BF16), *ex_shapes),
        in_specs=[pl.BlockSpec((tk, m), lambda j, k: (k, a_blk)), pl.BlockSpec((tk, tn), lambda j, k: (k, j)),
                  pl.BlockSpec((1, tn), lambda j, k: (0, j)), *ex_in],
        out_specs=(pl.BlockSpec((None, m, tn), lambda j, k: (j, 0, 0)), *ex_out),
        scratch_shapes=[pltpu.VMEM((m, tn), F32), *ex_sems],
        compiler_params=_params(("arbitrary", "arbitrary")),
    )(a, b, colscale, *ex_args)
    return (out[0].reshape(NDEV, m // NDEV, n) if rows_split else out[0]), out[1:]


def _adam_update(w, g, m, v):
    m = ADAM_B1 * m + (1.0 - ADAM_B1) * g
    v = ADAM_B2 * v + (1.0 - ADAM_B2) * (g * g)
    m_hat = m / (1.0 - ADAM_B1 ** ADAM_STEP)
    v_hat = v / (1.0 - ADAM_B2 ** ADAM_STEP)
    return -ADAM_LR * (m_hat / (jnp.sqrt(v_hat) + ADAM_EPS) + ADAM_WD * w), m, v


def _adamw_reduce(parts, w, m, v, name, ex=None, after=()):
    n_l = len(parts)
    n_after = len(after)
    n_p, _, c = parts[0].shape
    rows = [p.shape[1] for p in parts]
    r = sum(rows)
    tr = min(min(rows), 128 if ex is not None else 256)
    n_i = r // tr
    tiles = [r_l // tr for r_l in rows]
    first_tile = [sum(tiles[:l]) for l in range(n_l)]

    def body(*refs):
        p_refs, (w_ref, m_ref, v_ref) = refs[:n_l], refs[n_l:n_l + 3]
        g_out, d_out, m_out, v_out = refs[n_l + 3 + n_after:]
        g = None
        for l, p_ref in enumerate(p_refs):
            g_l = p_ref[0].astype(F32)
            for j in range(1, n_p):
                g_l = g_l + p_ref[j].astype(F32)
            g = g_l if g is None else jnp.where(pl.program_id(0) >= first_tile[l], g_l, g)
        g_out[...] = g
        d_out[...], m_out[...], v_out[...] = _adam_update(w_ref[...], g, m_ref[...], v_ref[...])

    step = lambda k: (lambda: pl.program_id(0) == k)
    body = _carry(ex, body, n_l + 3 + n_after, 4, step(0), [step(n_i // 3), step((2 * n_i) // 3)], step(n_i - 1))
    ex_args, ex_in, ex_shapes, ex_out, ex_sems = _carried(ex)
    blk = pl.BlockSpec((tr, c), lambda i: (i, 0))
    anywhere = [pl.BlockSpec(memory_space=pl.ANY)] * n_after
    p_specs = [pl.BlockSpec((n_p, tr, c), lambda i, l=l: (0, jnp.clip(i - first_tile[l], 0, tiles[l] - 1), 0))
               for l in range(n_l)]
    shp = jax.ShapeDtypeStruct((r, c), F32)
    out = pl.pallas_call(
        body, name=name, grid=(n_i,), out_shape=(shp, shp, shp, shp, *ex_shapes),
        in_specs=[*p_specs, blk, blk, blk, *anywhere, *ex_in],
        out_specs=(blk, blk, blk, blk, *ex_out), scratch_shapes=ex_sems,
        compiler_params=_params(("arbitrary",)),
    )(*parts, w, m, v, *after, *ex_args)
    return (*out[:4], out[4:])


def _adamw_small(gs, ws, ms, vs, name):
    n = len(gs)

    def body(*refs):
        ins, outs = refs[:4 * n], refs[4 * n:]
        for a in range(n):
            d, m, v = _adam_update(ins[n + a][...], ins[a][...], ins[2 * n + a][...], ins[3 * n + a][...])
            outs[a][...], outs[n + a][...], outs[2 * n + a][...] = d, m, v

    shapes = tuple(jax.ShapeDtypeStruct(w.shape, F32) for w in ws) * 3
    out = pl.pallas_call(
        body, name=name, out_shape=shapes,
        in_specs=[_vmem()] * (4 * n), out_specs=tuple([_vmem()] * (3 * n)),
        compiler_params=pltpu.CompilerParams(vmem_limit_bytes=VMEM_LIMIT),
    )(*gs, *ws, *ms, *vs)
    return out[:n], out[n:2 * n], out[2 * n:]


def _sum_devices(packed, name):
    n = len(packed)

    def body(*refs):
        for p_ref, o_ref in zip(refs[:n], refs[n:]):
            acc = p_ref[0].astype(F32)
            for j in range(1, NDEV):
                acc = acc + p_ref[j].astype(F32)
            o_ref[...] = acc

    return pl.pallas_call(
        body, name=name, out_shape=tuple(jax.ShapeDtypeStruct(p.shape[1:], F32) for p in packed),
        in_specs=[_vmem()] * n, out_specs=tuple([_vmem()] * n),
        compiler_params=pltpu.CompilerParams(vmem_limit_bytes=VMEM_LIMIT),
    )(*packed)


def _mod_w_grad(c_t, dmod, name):
    n_layers, _, w3 = dmod.shape
    d = c_t.shape[0]

    def body(c_ref, dm_ref, o_ref):
        for i in range(n_layers):
            acc = c_ref[:, 0:1] * dm_ref[i, 0:1, :]
            for b in range(1, NDEV):
                acc = acc + c_ref[:, b:b + 1] * dm_ref[i, b:b + 1, :]
            o_ref[i] = acc

    return pl.pallas_call(
        body, name=name, out_shape=jax.ShapeDtypeStruct((n_layers, d, w3), F32),
        in_specs=[_vmem(), _vmem()], out_specs=_vmem(),
        compiler_params=pltpu.CompilerParams(vmem_limit_bytes=VMEM_LIMIT),
    )(c_t, dmod)


def _mask_transpose_ws(w_s, name):
    def body(w_ref, wt_ref, wtt_ref):
        tril = (lax.broadcasted_iota(jnp.int32, (CHUNK, CHUNK), 0) >= lax.broadcasted_iota(jnp.int32, (CHUNK, CHUNK), 1))
        for g in range(GROUPS):
            wm = jnp.where(tril, w_ref[g], 0.0)
            wt_ref[g] = wm.astype(BF16)
            wtt_ref[g] = wm.T.astype(BF16)

    shp = jax.ShapeDtypeStruct(w_s.shape, BF16)
    return pl.pallas_call(
        body, name=name, out_shape=(shp, shp), in_specs=[_vmem()], out_specs=(_vmem(), _vmem()),
    )(w_s)


def _pack(pieces):
    flat = jnp.concatenate([p.reshape(-1) for p in pieces])
    rows = -(-flat.shape[0] // (8 * PACK_W)) * 8
    return jnp.pad(flat, (0, rows * PACK_W - flat.shape[0])).reshape(rows, PACK_W)


def _unpack(flat, shapes):
    out, off = [], 0
    for shp in shapes:
        size = 1
        for dim in shp:
            size *= dim
        out.append(flat[off:off + size].reshape(shp))
        off += size
    return out


def kernel(x, c, mod_w, mod_b, norm_g, a_w_in, a_conv_w, a_conv_b, a_w_out, b_w_in, b_ln_g, b_ln_b, b_w_s, b_b_s, b_w_out, final_g, loss_target, m_mod_w, m_mod_b, m_norm_g, m_a_w_in, m_a_conv_w, m_a_conv_b, m_a_w_out, m_b_w_in, m_b_ln_g, m_b_ln_b, m_b_w_s, m_b_b_s, m_b_w_out, m_final_g, v_mod_w, v_mod_b, v_norm_g, v_a_w_in, v_a_conv_w, v_a_conv_b, v_a_w_out, v_b_w_in, v_b_ln_g, v_b_ln_b, v_b_w_s, v_b_b_s, v_b_w_out, v_final_g):
    s, d = x.shape[1], x.shape[2]
    es = a_w_out.shape[1]
    e = NDEV * es
    w3 = mod_w.shape[2]
    me = _index(_pos())
    x0 = x.reshape(s, d)
    tgt = loss_target.reshape(s, d)

    small = jnp.concatenate([a_conv_w[0], b_ln_g, b_ln_b, jnp.zeros((3, es), F32)], axis=0)
    gather_a = _gather_exchange([a_w_in[0].astype(BF16), a_w_out[0].astype(BF16), small], [True, False, True])
    gather_b = _gather_exchange([b_w_in[0].astype(BF16), b_w_out[0].astype(BF16)], [True, False])
    mod, c_all, (wa, woa, small_all) = _mod_vectors(c, mod_w, mod_b, gather_a)
    conv_w, ln_g, ln_b = small_all[0:3], small_all[3:4], small_all[4:5]
    bsf = jnp.repeat(b_b_s[0].T, e // GROUPS, axis=1)
    wt, wtt = _mask_transpose_ws(b_w_s[0], "mask_w_s")
    shift0, scale0, gate0 = mod[0:1, 0:d], mod[0:1, d:2 * d], mod[0:1, 2 * d:]
    shift1, scale1, gate1 = mod[1:2, 0:d], mod[1:2, d:2 * d], mod[1:2, 2 * d:]
    g0, g1, fg = norm_g[0:1], norm_g[1:2], final_g.reshape(1, d)

    proj_a, h0, x1, br_a, conv_a, (wb, wob) = _layer_a_fwd(
        x0, g0, scale0, shift0, gate0, wa, conv_w, a_conv_b, woa, "a_fwd", gather_b)
    proj_b, h1, dx2, loss_acc, dfg, dgate1, v_b, dgv_b, _ = _layer_b_fwd_loss(
        x1, tgt, g1, scale1, shift1, gate1, fg, wb, ln_g, ln_b, wt, bsf, wob, "b_fwd_loss")

    dproj_b, y_b, dx1, dws, dbs, dlg, dlb, dshift1, dscale1, dg1 = _layer_b_bwd(
        proj_b, v_b, dgv_b, dx2, x1, gate1, g1, scale1, ln_g, ln_b, wt, wtt, bsf, wob, wb, "b_bwd")
    gs_b_out, _ = _matmul_tn(y_b, dx2, gate1, True, "b_w_out_grad")
    gs_b_in, (gr_b_out,) = _matmul_tn(h1, dproj_b, jnp.ones((1, dproj_b.shape[1]), F32), False, "b_w_in_grad",
                                      _scatter_exchange([gs_b_out]))
    dproj_a, y_a, dgate0, dcb, dcw, (gr_b_in,) = _conv_mixer_bwd(
        proj_a, dx1, br_a, conv_a, conv_w, gate0, woa, "a_mixer_bwd", _scatter_exchange([gs_b_in]))
    gs_a_out, _ = _matmul_tn(y_a, dx1, gate0, True, "a_w_out_grad")
    gs_a_in, (gr_a_out,) = _matmul_tn(h0, dproj_a, jnp.ones((1, dproj_a.shape[1]), F32), False, "a_w_in_grad",
                                      _scatter_exchange([gs_a_out]))
    send_sems, recv_sems, gs_thru, land_thru, token = _scatter_start(gs_a_in, "a_w_in_scatter_start")
    dx0, dshift0, dscale0, dg0, _ = _matmul_nt_norm_bwd(dproj_a, wa, x0, dx1, g0, scale0 + token[0:1, 0:1], "a_in_bwd")
    gs_a_in, landed = _scatter_wait(send_sems, recv_sems, gs_thru, land_thru, dshift0, "a_w_in_scatter_wait")
    gr_a_in = lax.dynamic_update_slice_in_dim(landed, lax.dynamic_slice_in_dim(gs_a_in, me, 1, axis=0), me, axis=0)

    def big(parts, w, m, v, name, ex=None, after=()):
        shp = w.shape
        r2 = lambda t_: t_.reshape(-1, shp[-1])
        g, dl, nm, nv, ex_out = _adamw_reduce(parts, r2(w), r2(m), r2(v), name, ex, after)
        return tuple(t_.reshape(shp) for t_ in (g, dl, nm, nv)), ex_out

    pieces = [dshift0, dscale0, dgate0, dshift1, dscale1, dgate1, dg0, dg1, dcb, dcw[0:3], dlg, dlb, dfg,
              dbs[:, 0:GROUPS].T, loss_acc[0:1, 0:1]]
    shapes = [p.shape for p in pieces]
    g_send, g_recv, blocks_thru, lands_thru, g_token = _gather_start(
        [_pack(pieces), dws.astype(BF16).reshape(-1, PACK_W)], "small_gather_start")
    res = {}
    res["a_w_in"], _ = big([gr_a_in], a_w_in, m_a_w_in, v_a_w_in, "adamw_a_w_in", after=[g_token])
    res["b_w_in"], _ = big([gr_b_in], b_w_in, m_b_w_in, v_b_w_in, "adamw_b_w_in", after=[g_token])
    res["a_w_out"], _ = big([gr_a_out], a_w_out, m_a_w_out, v_a_w_out, "adamw_a_w_out", after=[g_token])
    res["b_w_out"], _ = big([gr_b_out], b_w_out, m_b_w_out, v_b_w_out, "adamw_b_w_out", after=[g_token])
    landed = _gather_wait(
        g_send, g_recv, blocks_thru, lands_thru, [res[k][1] for k in ("a_w_in", "b_w_in", "a_w_out", "b_w_out")],
        "small_gather_wait")
    packed_all, dws_all = [lax.dynamic_update_slice_in_dim(l, b[None], me, axis=0) for l, b in zip(landed, blocks_thru)]
    total, t_ws = _sum_devices([packed_all, dws_all], "sum_small_grads")
    total, t_ws = total.reshape(-1), t_ws.reshape(dws.shape)
    (t_sh0, t_sc0, t_ga0, t_sh1, t_sc1, t_ga1, t_g0, t_g1, t_cb, t_cw, t_lg, t_lb, t_fg, t_bs, t_loss) = _unpack(
        total, shapes)
    loss = t_loss.reshape(())
    grad_mod_b = jnp.concatenate([jnp.concatenate([t_sh0, t_sc0, t_ga0], axis=1),
                                  jnp.concatenate([t_sh1, t_sc1, t_ga1], axis=1)], axis=0)
    grad_norm_g = jnp.concatenate([t_g0, t_g1], axis=0)
    dmod_all = packed_all.reshape(NDEV, -1)[:, 0:6 * d].reshape(NDEV, 2, 3 * d).transpose(1, 0, 2)
    dmod_mine = lax.dynamic_slice_in_dim(dmod_all, me * w3, w3, axis=2)
    grad_mod_w = _mod_w_grad(c_all.T, dmod_mine, "mod_w_grad")
    grad_a_conv_w = lax.dynamic_slice_in_dim(t_cw, me * es, es, axis=1)
    grad_b_ln_g = lax.dynamic_slice_in_dim(t_lg, me * es, es, axis=1)
    grad_b_ln_b = lax.dynamic_slice_in_dim(t_lb, me * es, es, axis=1)

    res["mod_w"], _ = big([grad_mod_w.reshape(1, -1, w3)], mod_w, m_mod_w, v_mod_w, "adamw_mod_w")

    small_names = ["mod_b", "norm_g", "a_conv_w", "a_conv_b", "b_ln_g", "b_ln_b", "b_w_s", "b_b_s", "final_g"]
    small_g = [grad_mod_b, grad_norm_g, grad_a_conv_w, t_cb, grad_b_ln_g, grad_b_ln_b, t_ws, t_bs, t_fg]
    small_w = [mod_b, norm_g, a_conv_w, a_conv_b, b_ln_g, b_ln_b, b_w_s, b_b_s, final_g]
    small_m = [m_mod_b, m_norm_g, m_a_conv_w, m_a_conv_b, m_b_ln_g, m_b_ln_b, m_b_w_s, m_b_b_s, m_final_g]
    small_v = [v_mod_b, v_norm_g, v_a_conv_w, v_a_conv_b, v_b_ln_g, v_b_ln_b, v_b_w_s, v_b_b_s, v_final_g]
    as2d = lambda t_: t_.reshape(-1, t_.shape[-1])
    dls, nms, nvs = _adamw_small([as2d(t_) for t_ in small_g], [as2d(t_) for t_ in small_w],
                                 [as2d(t_) for t_ in small_m], [as2d(t_) for t_ in small_v], "adamw_small")
    for a, nme in enumerate(small_names):
        shp = small_w[a].shape
        res[nme] = (small_g[a].reshape(shp), dls[a].reshape(shp), nms[a].reshape(shp), nvs[a].reshape(shp))

    order = ["mod_w", "mod_b", "norm_g", "a_w_in", "a_conv_w", "a_conv_b", "a_w_out", "b_w_in", "b_ln_g", "b_ln_b",
             "b_w_s", "b_b_s", "b_w_out", "final_g"]
    return (loss, dx0.reshape(x.shape), *[res[k][0] for k in order], *[res[k][1] for k in order],
            *[res[k][2] for k in order], *[res[k][3] for k in order])
```

```python
import functools

import jax
import jax.numpy as jnp
from jax import lax
from jax.experimental import pallas as pl
from jax.experimental.pallas import tpu as pltpu

NDEV = 8
CHUNK = 128
GROUPS = 8
RMS_EPS = 1e-6
LN_EPS = 1e-5
ADAM_LR, ADAM_B1, ADAM_B2, ADAM_EPS, ADAM_WD, ADAM_STEP = 0.001, 0.9, 0.999, 1e-08, 0.01, 10
V7X_VMEM_BYTES = 64 * 1024 * 1024
VMEM_LIMIT = V7X_VMEM_BYTES - 8 * 1024 * 1024
TN_VMEM_BUDGET = 46 * 1024 * 1024
PACK_W = 1024
F32, BF16 = jnp.float32, jnp.bfloat16
MESH = pl.DeviceIdType.MESH
RSQRT2 = 0.7071067811865476
INV_SQRT_2PI = 0.3989422804014327
NT_DIMS = (((1,), (1,)), ((), ()))
TN_DIMS = (((0,), (0,)), ((), ()))


def _params(sem=None):
    return pltpu.CompilerParams(dimension_semantics=sem, vmem_limit_bytes=VMEM_LIMIT)


def _vmem():
    return pl.BlockSpec(memory_space=pltpu.VMEM)


def _hbm():
    return pl.BlockSpec(memory_space=pltpu.HBM)


def _full(shape):
    return pl.BlockSpec(shape, lambda *_: (0,) * len(shape))


def _pos():
    return lax.axis_index("x"), lax.axis_index("y"), lax.axis_index("c")


def _index(p):
    return 4 * p[0] + 2 * p[1] + p[2]


def _peer(k):
    x, y, c = _pos()
    return ((1 - x) if (k >> 2) & 1 else x, (1 - y) if (k >> 1) & 1 else y, (1 - c) if k & 1 else c)


def _silu(z):
    sg = jax.nn.sigmoid(z)
    return z * sg, sg * (1.0 + z * (1.0 - sg))


def _gelu(v):
    phi = 0.5 * (1.0 + lax.erf(v * RSQRT2))
    return v * phi, phi + v * (jnp.exp(-0.5 * v * v) * INV_SQRT_2PI)


def _colsum(v):
    return jnp.sum(v, axis=0, keepdims=True)


def _rowsum(v):
    return jnp.sum(v, axis=-1, keepdims=True)


def _gather_all_vmem(slab_ref, send_sems, recv_sems, base):
    me = _index(_pos())
    sends = []
    for k in range(1, NDEV):
        cp = pltpu.make_async_remote_copy(
            src_ref=slab_ref.at[me], dst_ref=slab_ref.at[me],
            send_sem=send_sems.at[base + k - 1], recv_sem=recv_sems.at[base + k - 1],
            device_id=_peer(k), device_id_type=MESH)
        cp.start()
        sends.append(cp)
    for k in range(1, NDEV):
        src = _index(_peer(k))
        pltpu.make_async_remote_copy(
            src_ref=slab_ref.at[src], dst_ref=slab_ref.at[src],
            send_sem=send_sems.at[base + k - 1], recv_sem=recv_sems.at[base + k - 1],
            device_id=_peer(k), device_id_type=MESH).wait_recv()
    for cp in sends:
        cp.wait_send()


def _mod_vectors(c, mod_w, mod_b, ex):
    n_layers, d, w3 = mod_w.shape
    r_in, r_out = len(ex.arrays), len(ex.out_shapes)

    def body(*refs):
        c_ref, mw_ref, mb_ref = refs[:3]
        ex_ins = refs[3:3 + r_in]
        mod_ref, call_ref = refs[3 + r_in:5 + r_in]
        ex_outs = refs[5 + r_in:5 + r_in + r_out]
        cslab, pslab, send_sems, recv_sems = refs[5 + r_in + r_out:9 + r_in + r_out]
        ex_sems = refs[9 + r_in + r_out:]
        ex.start(ex_ins, ex_outs, ex_sems)
        me = _index(_pos())
        cv = c_ref[...]
        cslab[me] = jnp.broadcast_to(cv * jax.nn.sigmoid(cv), (8, d))
        _gather_all_vmem(cslab, send_sems, recv_sems, 0)
        c_all = jnp.concatenate([cslab[k, 0:1, :] for k in range(NDEV)], axis=0)
        call_ref[...] = c_all
        for i in range(n_layers):
            pslab[me, i * NDEV:(i + 1) * NDEV, :] = jnp.dot(
                c_all, mw_ref[i], preferred_element_type=F32, precision=lax.Precision.HIGHEST)
        _gather_all_vmem(pslab, send_sems, recv_sems, NDEV - 1)
        for i in range(n_layers):
            for k in range(NDEV):
                mod_ref[i:i + 1, k * w3:(k + 1) * w3] = (
                    pslab[k, pl.ds(i * NDEV + me, 1), :] + mb_ref[i:i + 1, k * w3:(k + 1) * w3])
        for passing_on in ex.middles:
            passing_on(ex_ins, ex_outs, ex_sems)
        ex.finish(ex_ins, ex_outs, ex_sems)

    out = pl.pallas_call(
        body, name="mod_vectors",
        out_shape=(jax.ShapeDtypeStruct((n_layers, 3 * d), F32), jax.ShapeDtypeStruct((NDEV, d), F32), *ex.out_shapes),
        in_specs=[_vmem(), _vmem(), _vmem()] + [_hbm()] * r_in, out_specs=(_vmem(), _vmem(), *([_hbm()] * r_out)),
        scratch_shapes=[pltpu.VMEM((NDEV, 8, d), F32), pltpu.VMEM((NDEV, n_layers * NDEV, w3), F32),
                        pltpu.SemaphoreType.DMA((2 * (NDEV - 1),)), pltpu.SemaphoreType.DMA((2 * (NDEV - 1),)), *ex.sems],
        compiler_params=pltpu.CompilerParams(vmem_limit_bytes=VMEM_LIMIT),
    )(c, mod_w, mod_b, *ex.arrays)
    return out[0], out[1], out[2:]


class _Exchange:
    def __init__(self, arrays, out_shapes, sems, start, middles, finish):
        self.arrays, self.out_shapes, self.sems = list(arrays), list(out_shapes), list(sems)
        self.start, self.middles, self.finish = start, list(middles), finish


def _gather_exchange(shards, by_cols):
    n = len(shards)
    shapes = [sh.shape for sh in shards]

    def tools(ins, outs, sems):
        send_sems, recv_sems, local_sems = sems
        x, y, c = _pos()
        chips = [(1 - x, y), (x, 1 - y), (1 - x, 1 - y)]
        south = c == 0
        relayed = (jnp.where(south, 1 - x, x), jnp.where(south, y, 1 - y), c)
        relay_to = (jnp.where(south, x, 1 - x), jnp.where(south, 1 - y, y), c)

        def place(a, block):
            r, cc = shapes[a]
            if by_cols[a]:
                return outs[a].at[:, pl.ds(_index(block) * cc, cc)]
            return outs[a].at[pl.ds(_index(block) * r, r), :]

        def copy(a, k, block, to, src=None):
            dst = place(a, block)
            return pltpu.make_async_remote_copy(
                src_ref=dst if src is None else src, dst_ref=dst,
                send_sem=send_sems.at[a * 7 + k], recv_sem=recv_sems.at[a * 7 + k],
                device_id=to, device_id_type=MESH)

        mine = [pltpu.make_async_copy(ins[a], place(a, (x, y, c)), local_sems.at[a]) for a in range(n)]
        first = []
        for a in range(n):
            first.append(copy(a, 0, (x, y, c), (x, y, 1 - c), src=ins[a]))
            first += [copy(a, 1 + j, (x, y, c), (*chip, c), src=ins[a]) for j, chip in enumerate(chips[:2])]
        relays = [copy(a, 3, relayed, relay_to) for a in range(n)]
        passed = [copy(a, 4 + j, (*chip, c), (x, y, 1 - c)) for j, chip in enumerate(chips) for a in range(n)]
        return (x, y, c), chips, copy, mine, first, relays, passed

    def start(ins, outs, sems):
        _, _, _, mine, first, _, _ = tools(ins, outs, sems)
        for cp in mine + first:
            cp.start()

    def pass_neighbours(ins, outs, sems):
        (x, y, c), chips, copy, _, _, relays, passed = tools(ins, outs, sems)
        for j, chip in enumerate(chips[:2]):
            for a in range(n):
                copy(a, 1 + j, (*chip, c), (x, y, c)).wait_recv()
                passed[j * n + a].start()
        for cp in relays:
            cp.start()

    def pass_diagonal(ins, outs, sems):
        (x, y, c), chips, copy, _, _, _, passed = tools(ins, outs, sems)
        for a in range(n):
            copy(a, 3, (*chips[2], c), (x, y, c)).wait_recv()
            passed[2 * n + a].start()

    def finish(ins, outs, sems):
        (x, y, c), chips, copy, mine, first, relays, passed = tools(ins, outs, sems)
        for a in range(n):
            copy(a, 0, (x, y, 1 - c), (x, y, c)).wait_recv()
        for j, chip in enumerate(chips):
            for a in range(n):
                copy(a, 4 + j, (*chip, 1 - c), (x, y, c)).wait_recv()
        for cp in first + relays + passed:
            cp.wait_send()
        for cp in mine:
            cp.wait()

    out_shapes = [jax.ShapeDtypeStruct((r, NDEV * cc) if bc else (NDEV * r, cc), sh.dtype)
                  for (r, cc), bc, sh in zip(shapes, by_cols, shards)]
    sems = [pltpu.SemaphoreType.DMA((7 * n,)), pltpu.SemaphoreType.DMA((7 * n,)), pltpu.SemaphoreType.DMA((n,))]
    return _Exchange(shards, out_shapes, sems, start, [pass_neighbours, pass_diagonal], finish)


def _scatter_exchange(parts):
    n = len(parts)

    def tools(ins, outs, sems):
        send_sems, recv_sems, local_sems = sems
        me = _index(_pos())
        mine = [pltpu.make_async_copy(ins[a].at[me], outs[a].at[me], local_sems.at[a]) for a in range(n)]
        sends, arrivals = [], []
        for k in range(1, NDEV):
            peer = _peer(k)
            for a in range(n):
                pair = dict(send_sem=send_sems.at[a * 7 + k - 1], recv_sem=recv_sems.at[a * 7 + k - 1],
                            device_id=peer, device_id_type=MESH)
                sends.append(pltpu.make_async_remote_copy(src_ref=ins[a].at[_index(peer)], dst_ref=outs[a].at[me], **pair))
                slot = outs[a].at[_index(peer)]
                arrivals.append(pltpu.make_async_remote_copy(src_ref=slot, dst_ref=slot, **pair))
        return mine, sends, arrivals

    def start(ins, outs, sems):
        mine, sends, _ = tools(ins, outs, sems)
        for cp in mine + sends:
            cp.start()

    def finish(ins, outs, sems):
        mine, sends, arrivals = tools(ins, outs, sems)
        for cp in arrivals:
            cp.wait_recv()
        for cp in sends:
            cp.wait_send()
        for cp in mine:
            cp.wait()

    out_shapes = [jax.ShapeDtypeStruct(p.shape, p.dtype) for p in parts]
    sems = [pltpu.SemaphoreType.DMA((7 * n,)), pltpu.SemaphoreType.DMA((7 * n,)), pltpu.SemaphoreType.DMA((n,))]
    return _Exchange(parts, out_shapes, sems, start, [], finish)


def _scatter_start(part, name):
    def body(part_ref, land_ref, send_sems, recv_sems, part_thru, land_thru, token):
        me = _index(_pos())
        for k in range(1, NDEV):
            peer = _peer(k)
            pltpu.make_async_remote_copy(
                src_ref=part_ref.at[_index(peer)], dst_ref=land_ref.at[me],
                send_sem=send_sems.at[k - 1], recv_sem=recv_sems.at[k - 1],
                device_id=peer, device_id_type=MESH).start()
        token[...] = jnp.zeros_like(token)

    sem = pl.BlockSpec(memory_space=pltpu.SEMAPHORE)
    return pl.pallas_call(
        body, name=name,
        out_shape=(pltpu.SemaphoreType.DMA((NDEV - 1,)), pltpu.SemaphoreType.DMA((NDEV - 1,)),
                   pltpu.HBM(part.shape, part.dtype), pltpu.HBM(part.shape, part.dtype),
                   jax.ShapeDtypeStruct((8, 128), F32)),
        in_specs=(_hbm(), _hbm()), out_specs=(sem, sem, _hbm(), _hbm(), _vmem()),
        input_output_aliases={0: 2, 1: 3},
        compiler_params=pltpu.CompilerParams(has_side_effects=pltpu.SideEffectType.DATAFLOW_SIDE_EFFECTING),
    )(pltpu.with_memory_space_constraint(part, pltpu.HBM),
      pltpu.with_memory_space_constraint(lax.empty(part.shape, part.dtype), pltpu.HBM))


def _scatter_wait(send_sems, recv_sems, part_thru, land_thru, after, name):
    def body(part_ref, land_ref, send_sems, recv_sems, after_ref, part_dead, got_ref):
        for k in range(1, NDEV):
            peer = _peer(k)
            slot = land_ref.at[_index(peer)]
            copy = pltpu.make_async_remote_copy(
                src_ref=part_ref.at[_index(peer)], dst_ref=slot,
                send_sem=send_sems.at[k - 1], recv_sem=recv_sems.at[k - 1],
                device_id=peer, device_id_type=MESH)
            copy.wait_send()
            copy.wait_recv()

    sem = pl.BlockSpec(memory_space=pltpu.SEMAPHORE)
    return pl.pallas_call(
        body, name=name,
        out_shape=(pltpu.HBM(part_thru.shape, part_thru.dtype), pltpu.HBM(land_thru.shape, land_thru.dtype)),
        in_specs=(_hbm(), _hbm(), sem, sem, pl.BlockSpec(memory_space=pl.ANY)), out_specs=(_hbm(), _hbm()),
        input_output_aliases={0: 0, 1: 1},
        compiler_params=pltpu.CompilerParams(has_side_effects=pltpu.SideEffectType.DATAFLOW_SIDE_EFFECTING),
    )(part_thru, land_thru, send_sems, recv_sems, after)


def _gather_start(blocks, name):
    n = len(blocks)

    def body(*refs):
        b_refs, l_refs = refs[:n], refs[n:2 * n]
        send_sems, recv_sems = refs[2 * n:2 * n + 2]
        token = refs[-1]
        me = _index(_pos())
        for k in range(1, NDEV):
            peer = _peer(k)
            for a in range(n):
                pltpu.make_async_remote_copy(
                    src_ref=b_refs[a], dst_ref=l_refs[a].at[me],
                    send_sem=send_sems.at[a * (NDEV - 1) + k - 1], recv_sem=recv_sems.at[a * (NDEV - 1) + k - 1],
                    device_id=peer, device_id_type=MESH).start()
        token[...] = jnp.zeros_like(token)

    sem = pl.BlockSpec(memory_space=pltpu.SEMAPHORE)
    lands =[lax.empty((NDEV, *b.shape), b.dtype) for b in blocks]
    out = pl.pallas_call(
        body, name=name,
        out_shape=(pltpu.SemaphoreType.DMA((n * (NDEV - 1),)), pltpu.SemaphoreType.DMA((n * (NDEV - 1),)),
                   *[pltpu.HBM(b.shape, b.dtype) for b in blocks], *[pltpu.HBM(l.shape, l.dtype) for l in lands],
                   jax.ShapeDtypeStruct((8, 128), F32)),
        in_specs=tuple([_hbm()] * (2 * n)), out_specs=(sem, sem, *([_hbm()] * (2 * n)), _vmem()),
        input_output_aliases={a: 2 + a for a in range(2 * n)},
        compiler_params=pltpu.CompilerParams(has_side_effects=pltpu.SideEffectType.DATAFLOW_SIDE_EFFECTING),
    )(*[pltpu.with_memory_space_constraint(b, pltpu.HBM) for b in blocks],
      *[pltpu.with_memory_space_constraint(l, pltpu.HBM) for l in lands])
    return out[0], out[1], out[2:2 + n], out[2 + n:2 + 2 * n], out[-1]


def _gather_wait(send_sems, recv_sems, blocks_thru, lands_thru, after, name):
    n = len(blocks_thru)

    def body(*refs):
        b_refs, l_refs = refs[:n], refs[n:2 * n]
        send_sems, recv_sems = refs[2 * n:2 * n + 2]
        for k in range(1, NDEV):
            peer = _peer(k)
            for a in range(n):
                copy = pltpu.make_async_remote_copy(
                    src_ref=b_refs[a], dst_ref=l_refs[a].at[_index(peer)],
                    send_sem=send_sems.at[a * (NDEV - 1) + k - 1], recv_sem=recv_sems.at[a * (NDEV - 1) + k - 1],
                    device_id=peer, device_id_type=MESH)
                copy.wait_send()
                copy.wait_recv()

    sem = pl.BlockSpec(memory_space=pltpu.SEMAPHORE)
    out = pl.pallas_call(
        body, name=name,
        out_shape=tuple(pltpu.HBM(l.shape, l.dtype) for l in lands_thru),
        in_specs=(*([_hbm()] * (2 * n)), sem, sem, *([pl.BlockSpec(memory_space=pl.ANY)] * len(after))),
        out_specs=tuple([_hbm()] * n),
        input_output_aliases={n + a: a for a in range(n)},
        compiler_params=pltpu.CompilerParams(has_side_effects=pltpu.SideEffectType.DATAFLOW_SIDE_EFFECTING),
    )(*blocks_thru, *lands_thru, send_sems, recv_sems, *after)
    return list(out)


def _carry(ex, body, n_in, n_out, first, middle, last):
    if ex is None:
        return body
    r_in, r_out = len(ex.arrays), len(ex.out_shapes)

    def wrapped(*refs):
        ins, rins = refs[:n_in], refs[n_in:n_in + r_in]
        outs = refs[n_in + r_in:n_in + r_in + n_out]
        routs = refs[n_in + r_in + n_out:n_in + r_in + n_out + r_out]
        rest = refs[n_in + r_in + n_out + r_out:]
        scratch, sems = rest[:len(rest) - len(ex.sems)], rest[len(rest) - len(ex.sems):]

        @pl.when(first())
        def _():
            ex.start(rins, routs, sems)

        for passing_on, at_step in zip(ex.middles, middle or []):
            pl.when(at_step())(functools.partial(passing_on, rins, routs, sems))

        body(*ins, *outs, *scratch)

        @pl.when(last())
        def _():
            ex.finish(rins, routs, sems)

    return wrapped


def _carried(ex):
    if ex is None:
        return [], [], [], [], []
    return ex.arrays, [_hbm()] * len(ex.arrays), ex.out_shapes, [_hbm()] * len(ex.out_shapes), ex.sems


def _resident(shape):
    return pl.BlockSpec(shape, lambda *_: (0,) * len(shape), pipeline_mode=pl.Buffered(1))


def _norm_modulate(x_ref, g_ref, sc_ref, sh_ref):
    xv = x_ref[...]
    r = lax.rsqrt(jnp.mean(xv * xv, axis=-1, keepdims=True) + RMS_EPS)
    return ((xv * r) * g_ref[...] * (1.0 + sc_ref[...]) + sh_ref[...]).astype(BF16)


def _conv_taps(cx, t6, t7, row):
    p1 = jnp.where(row == 0, t7, pltpu.roll(cx, 1, 0))
    p2 = jnp.where(row == 0, t6, jnp.where(row == 1, t7, pltpu.roll(cx, 2, 0)))
    return p1, p2


def _layer_a_fwd(x, g, scale, shift, gate, wi, cw, cb, wo, name, ex=None):
    s, d = x.shape
    e = wo.shape[0]
    t = min(s, 256)
    n_t = s // t
    cwid = min(e, 512)

    def body(x_ref, g_ref, sc_ref, sh_ref, gate_ref, wi_ref, cw_ref, cb_ref, wo_ref,
             proj_ref, h_ref, x1_ref, br_ref, conv_ref, y_scr, tail_scr):
        @pl.when(pl.program_id(0) == 0)
        def _():
            tail_scr[...] = jnp.zeros_like(tail_scr)
        h_ref[...] = _norm_modulate(x_ref, g_ref, sc_ref, sh_ref)
        row = lax.broadcasted_iota(jnp.int32, (t, cwid), 0)

        def project(c0):
            v = jnp.dot(h_ref[...], wi_ref[:, c0:c0 + cwid], preferred_element_type=F32)
            proj_ref[:, c0:c0 + cwid] = v.astype(BF16)
            return v

        for c0 in range(0, e, cwid):
            sl = slice(c0, c0 + cwid)
            bg, z = project(c0), project(3 * e + c0)
            cx = project(e + c0) * project(2 * e + c0)
            p1, p2 = _conv_taps(cx, tail_scr[6:7, sl], tail_scr[7:8, sl], row)
            conv = cb_ref[:, sl] + cw_ref[2:3, sl] * cx + cw_ref[0:1, sl] * p2 + cw_ref[1:2, sl] * p1
            conv_ref[:, sl] = conv.astype(BF16)
            y_scr[:, sl] = (_silu(z)[0] * bg * conv).astype(BF16)
            tail_scr[:, sl] = cx[t - 8:t, :]
        br = jnp.dot(y_scr[...], wo_ref[...], preferred_element_type=F32)
        x1_ref[...] = x_ref[...] + gate_ref[...] * br
        br_ref[...] = br.astype(BF16)

    step = lambda k: (lambda: pl.program_id(0) == k)
    body = _carry(ex, body, 9, 5, step(0), [step(n_t // 3), step((2 * n_t) // 3)], step(n_t - 1))
    ex_args, ex_in, ex_shapes, ex_out, ex_sems = _carried(ex)
    tok = pl.BlockSpec((t, d), lambda i: (i, 0))
    out = pl.pallas_call(
        body, name=name, grid=(n_t,),
        out_shape=(jax.ShapeDtypeStruct((s, 4 * e), BF16), jax.ShapeDtypeStruct((s, d), BF16),
                   jax.ShapeDtypeStruct((s, d), F32), jax.ShapeDtypeStruct((s, d), BF16),
                   jax.ShapeDtypeStruct((s, e), BF16), *ex_shapes),
        in_specs=[tok, _full((1, d)), _full((1, d)), _full((1, d)), _full((1, d)), _resident((d, 4 * e)),
                  _full((3, e)), _full((1, e)), _resident((e, d)), *ex_in],
        out_specs=(pl.BlockSpec((t, 4 * e), lambda i: (i, 0)), tok, tok, tok,
                   pl.BlockSpec((t, e), lambda i: (i, 0)), *ex_out),
        scratch_shapes=[pltpu.VMEM((t, e), BF16), pltpu.VMEM((8, e), F32), *ex_sems],
        compiler_params=_params(("arbitrary",)),
    )(x, g, scale, shift, gate, wi, cw, cb, wo, *ex_args)
    return (*out[:5], out[5:])


def _ln_stats(v_of, v_scr, t, e):
    gw = e // GROUPS
    s1 = jnp.zeros((t, 1), F32)
    for g in range(GROUPS):
        v = v_of(g)
        v_scr[:, g * gw:(g + 1) * gw] = v
        s1 = s1 + _rowsum(v)
    mu = s1 * (1.0 / e)
    s2 = jnp.zeros((t, 1), F32)
    for g in range(GROUPS):
        dv = v_scr[:, g * gw:(g + 1) * gw] - mu
        s2 = s2 + _rowsum(dv * dv)
    return mu, lax.rsqrt(s2 * (1.0 / e) + LN_EPS)


def _layer_b_fwd_loss(x1, tgt, g1, scale, shift, gate, fg, wi, lng, lnb, wt, bsf, wo, name, ex=None):
    s, d = x1.shape
    e = wo.shape[0]
    gw = e // GROUPS
    t = min(s, 256)
    n_t = s // t

    def body(x1_ref, tgt_ref, g_ref, sc_ref, sh_ref, gate_ref, fg_ref, wi_ref, lng_ref, lnb_ref, wt_ref, bsf_ref, wo_ref,
             proj_ref, h_ref, dx2_ref, loss_ref, dfg_ref, dgate_ref, v_ref, dgv_ref, v_scr, y_scr):
        @pl.when(pl.program_id(0) == 0)
        def _():
            loss_ref[...] = jnp.zeros_like(loss_ref)
            dfg_ref[...] = jnp.zeros_like(dfg_ref)
            dgate_ref[...] = jnp.zeros_like(dgate_ref)
        h_ref[...] = _norm_modulate(x1_ref, g_ref, sc_ref, sh_ref)

        def project(c0):
            v = jnp.dot(h_ref[...], wi_ref[:, c0:c0 + gw], preferred_element_type=F32)
            proj_ref[:, c0:c0 + gw] = v.astype(BF16)
            return v

        def gelu_v(g):
            gs = slice(g * gw, (g + 1) * gw)
            v, dgv = _gelu(project(e + g * gw))
            v_ref[:, gs] = v.astype(BF16)
            dgv_ref[:, gs] = dgv.astype(BF16)
            return v

        mu, rs = _ln_stats(gelu_v, v_scr, t, e)
        for g in range(GROUPS):
            gs = slice(g * gw, (g + 1) * gw)
            vn = (((v_scr[:, gs] - mu) * rs) * lng_ref[:, gs] + lnb_ref[:, gs]).astype(BF16)
            u = _gelu(project(g * gw))[0]
            sz = _silu(project(2 * e + g * gw))[0]
            for ch in range(t // CHUNK):
                rows = slice(ch * CHUNK, (ch + 1) * CHUNK)
                mixed = jnp.dot(wt_ref[g], vn[rows], preferred_element_type=F32) + bsf_ref[:, gs]
                y_scr[rows, gs] = (sz[rows] * (u[rows] * mixed)).astype(BF16)
        br = jnp.dot(y_scr[...], wo_ref[...], preferred_element_type=F32)
        x2 = x1_ref[...] + gate_ref[...] * br
        r2 = lax.rsqrt(jnp.mean(x2 * x2, axis=-1, keepdims=True) + RMS_EPS)
        xn = x2 * r2
        diff = xn * fg_ref[...] - tgt_ref[...]
        loss_ref[...] += jnp.broadcast_to(0.5 * _colsum(jnp.mean(diff * diff, axis=-1, keepdims=True)), loss_ref.shape)
        dout = diff * (1.0 / d)
        dfg_ref[...] += _colsum(dout * xn)
        dxn = dout * fg_ref[...]
        dx2 = r2 * (dxn - xn * jnp.mean(dxn * xn, axis=-1, keepdims=True))
        dx2_ref[...] = dx2
        dgate_ref[...] += _colsum(dx2 * br)

    step = lambda k: (lambda: pl.program_id(0) == k)
    body = _carry(ex, body, 13, 8, step(0), [step(n_t // 3), step((2 * n_t) // 3)], step(n_t - 1))
    ex_args, ex_in, ex_shapes, ex_out, ex_sems = _carried(ex)
    tok = pl.BlockSpec((t, d), lambda i: (i, 0))
    vec = _full((1, d))
    out = pl.pallas_call(
        body, name=name, grid=(n_t,),
        out_shape=(jax.ShapeDtypeStruct((s, 3 * e), BF16), jax.ShapeDtypeStruct((s, d), BF16),
                   jax.ShapeDtypeStruct((s, d), F32), jax.ShapeDtypeStruct((8, 128), F32),
                   jax.ShapeDtypeStruct((1, d), F32), jax.ShapeDtypeStruct((1, d), F32),
                   jax.ShapeDtypeStruct((s, e), BF16), jax.ShapeDtypeStruct((s, e), BF16), *ex_shapes),
        in_specs=[tok, tok, vec, vec, vec, vec, vec, _resident((d, 3 * e)), _full((1, e)), _full((1, e)),
                  _full((GROUPS, CHUNK, CHUNK)), _resident((CHUNK, e)), _resident((e, d)), *ex_in],
        out_specs=(pl.BlockSpec((t, 3 * e), lambda i: (i, 0)), tok, tok, _full((8, 128)), vec, vec,
                   pl.BlockSpec((t, e), lambda i: (i, 0)), pl.BlockSpec((t, e), lambda i: (i, 0)), *ex_out),
        scratch_shapes=[pltpu.VMEM((t, e), F32), pltpu.VMEM((t, e), BF16), *ex_sems],
        compiler_params=_params(("arbitrary",)),
    )(x1, tgt, g1, scale, shift, gate, fg, wi, lng, lnb, wt, bsf, wo, *ex_args)
    return (*out[:8], out[8:])


def _norm_modulate_bwd(dh, x_ref, dres_ref, g_ref, sc_ref, dx_ref, dsh_ref, p_scr):
    xv = x_ref[...]
    r = lax.rsqrt(jnp.mean(xv * xv, axis=-1, keepdims=True) + RMS_EPS)
    xn = xv * r
    dsh_ref[...] += _colsum(dh)
    p_scr[...] += _colsum(dh * xn)
    dxn = dh * (g_ref[...] * (1.0 + sc_ref[...]))
    dx_ref[...] = r * (dxn - xn * jnp.mean(dxn * xn, axis=-1, keepdims=True)) + dres_ref[...]


def _layer_b_bwd(proj, v_act, dgv, dx2, x1, gate, g1, scale, lng, lnb, wt, wtt, bsf, wo, wi, name):
    s, e3 = proj.shape
    e = e3 // 3
    d = dx2.shape[1]
    gw = e // GROUPS
    t = min(s, 256)
    n_t = s // t

    def body(pu_ref, pz_ref, v_ref, dgv_ref, dx_ref, x1_ref, gate_ref, g_ref, sc_ref, lng_ref, lnb_ref, wt_ref, wtt_ref,
             bsf_ref, wo_ref, wi_ref,
             dp_ref, y_ref, dx1_ref, dws_ref, dbs_ref, dlg_ref, dlb_ref, dsh_ref, dsc_ref, dg_ref,
             v_scr, dbr_scr, dvn_scr, dbs_scr, p_scr, dy_scr, vn_scr, mixed_scr):
        @pl.when(pl.program_id(0) == 0)
        def _():
            dws_ref[...] = jnp.zeros_like(dws_ref)
            dlg_ref[...] = jnp.zeros_like(dlg_ref)
            dlb_ref[...] = jnp.zeros_like(dlb_ref)
            dsh_ref[...] = jnp.zeros_like(dsh_ref)
            dbs_scr[...] = jnp.zeros_like(dbs_scr)
            p_scr[...] = jnp.zeros_like(p_scr)
        dbr_scr[...] = (dx_ref[...] * gate_ref[...]).astype(BF16)
        mu, rs = _ln_stats(lambda g: v_ref[:, g * gw:(g + 1) * gw].astype(F32), v_scr, t, e)
        tril = (lax.broadcasted_iota(jnp.int32, (CHUNK, CHUNK), 0) >= lax.broadcasted_iota(jnp.int32, (CHUNK, CHUNK), 1))
        c1 = jnp.zeros((t, 1), F32)
        c2 = jnp.zeros((t, 1), F32)
        span = 2
        kw = span * gw
        dh = jnp.zeros((t, d), F32)

        def through_w_in(c0):
            return lax.dot_general(dp_ref[:, c0:c0 + kw], wi_ref[:, c0:c0 + kw], NT_DIMS, preferred_element_type=F32)

        dy_scr[...] = lax.dot_general(dbr_scr[...], wo_ref[...], NT_DIMS, preferred_element_type=F32)
        for g in range(GROUPS):
            gs = slice(g * gw, (g + 1) * gw)
            vhat = (v_scr[:, gs] - mu) * rs
            v_scr[:, gs] = vhat
            vn = (vhat * lng_ref[:, gs] + lnb_ref[:, gs]).astype(BF16)
            vn_scr[:, gs] = vn
            for ch in range(t // CHUNK):
                rows = slice(ch * CHUNK, (ch + 1) * CHUNK)
                mixed_scr[rows, gs] = jnp.dot(wt_ref[g], vn[rows], preferred_element_type=F32) + bsf_ref[:, gs]
        for g in range(GROUPS):
            gs = slice(g * gw, (g + 1) * gw)
            vhat = v_scr[:, gs]
            lg = lng_ref[:, gs]
            for ch in range(t // CHUNK):
                rows = slice(ch * CHUNK, (ch + 1) * CHUNK)
                mixed = mixed_scr[rows, gs]
                u, dgu = _gelu(pu_ref[rows, gs].astype(F32))
                sz, dsz = _silu(pz_ref[rows, gs].astype(F32))
                sgate = u * mixed
                y_ref[rows, gs] = (sz * sgate).astype(BF16)
                dy = dy_scr[rows, gs]
                dp_ref[rows, 2 * e + g * gw:2 * e + (g + 1) * gw] = (dy * sgate * dsz).astype(BF16)
                ds = dy * sz
                dp_ref[rows, gs] = (ds * mixed * dgu).astype(BF16)
                dm = ds * u
                dbs_scr[:, gs] += dm
                dmb = dm.astype(BF16)
                dws_ref[g] += jnp.where(tril, lax.dot_general(dmb, vn_scr[rows, gs], NT_DIMS, preferred_element_type=F32), 0.0)
                dvn_scr[rows, gs] = jnp.dot(wtt_ref[g], dmb, preferred_element_type=F32)
            dvn = dvn_scr[:, gs]
            dlb_ref[:, gs] += _colsum(dvn)
            dlg_ref[:, gs] += _colsum(dvn * vhat)
            dvh = dvn * lg
            c1 = c1 + _rowsum(dvh)
            c2 = c2 + _rowsum(dvh * vhat)
            if g % span == span - 1:
                dh = dh + through_w_in(g * gw + gw - kw) + through_w_in(2 * e + g * gw + gw - kw)
        c1 = c1 * (1.0 / e)
        c2 = c2 * (1.0 / e)
        for g in range(GROUPS):
            gs = slice(g * gw, (g + 1) * gw)
            dv = rs * (dvn_scr[:, gs] * lng_ref[:, gs] - c1 - v_scr[:, gs] * c2)
            dp_ref[:, e + g * gw:e + (g + 1) * gw] = (dv * dgv_ref[:, gs]).astype(BF16)
            if g % span == span - 1:
                dh = dh + through_w_in(e + g * gw + gw - kw)
        _norm_modulate_bwd(dh, x1_ref, dx_ref, g_ref, sc_ref, dx1_ref, dsh_ref, p_scr)

        @pl.when(pl.program_id(0) == n_t - 1)
        def _():
            lane = lax.broadcasted_iota(jnp.int32, (CHUNK, 128), 1)
            acc = jnp.zeros((CHUNK, 128), F32)
            for g in range(GROUPS):
                acc = acc + jnp.where(lane == g, _rowsum(dbs_scr[:, g * gw:(g + 1) * gw]), 0.0)
            dbs_ref[...] = acc
            dsc_ref[...] = p_scr[...] * g_ref[...]
            dg_ref[...] = p_scr[...] * (1.0 + sc_ref[...])

    tok = pl.BlockSpec((t, d), lambda i: (i, 0))
    vec, evec, ws = _full((1, d)), _full((1, e)), _full((GROUPS, CHUNK, CHUNK))
    vshape = jax.ShapeDtypeStruct((1, d), F32)
    return pl.pallas_call(
        body, name=name, grid=(n_t,),
        out_shape=(jax.ShapeDtypeStruct((s, e3), BF16), jax.ShapeDtypeStruct((s, e), BF16), jax.ShapeDtypeStruct((s, d), F32),
                   jax.ShapeDtypeStruct((GROUPS, CHUNK, CHUNK), F32), jax.ShapeDtypeStruct((CHUNK, 128), F32),
                   jax.ShapeDtypeStruct((1, e), F32), jax.ShapeDtypeStruct((1, e), F32), vshape, vshape, vshape),
        in_specs=[pl.BlockSpec((t, e), lambda i: (i, 0)), pl.BlockSpec((t, e), lambda i: (i, 2)),
                  pl.BlockSpec((t, e), lambda i: (i, 0)), pl.BlockSpec((t, e), lambda i: (i, 0)),
                  tok, tok, vec, vec, vec, evec, evec, ws, ws,
                  _resident((CHUNK, e)), _resident((e, d)), _resident((d, e3))],
        out_specs=(pl.BlockSpec((t, e3), lambda i: (i, 0)), pl.BlockSpec((t, e), lambda i: (i, 0)), tok,
                   ws, _full((CHUNK, 128)), evec, evec, vec, vec, vec),
        scratch_shapes=[pltpu.VMEM((t, e), F32), pltpu.VMEM((t, d), BF16),
                        pltpu.VMEM((t, e), F32), pltpu.VMEM((CHUNK, e), F32), pltpu.VMEM((1, d), F32),
                        pltpu.VMEM((t, e), F32), pltpu.VMEM((t, e), BF16), pltpu.VMEM((t, e), F32)],
        compiler_params=_params(("arbitrary",)),
    )(proj, proj, v_act, dgv, dx2, x1, gate, g1, scale, lng, lnb, wt, wtt, bsf, wo, wi)


def _conv_mixer_bwd(proj, dx1, br, conv_a, cw, gate, wo, name, ex=None):
    s, e4 = proj.shape
    e = e4 // 4
    d = dx1.shape[1]
    t = min(s, 256)
    n_t = s // t
    cwid = min(e, 512)

    def body(p_ref, dx_ref, br_ref, conv_ref, cw_ref, gate_ref, wo_ref,
             dp_ref, y_ref, dgate_ref, dcb_ref, dcw_ref, dy_scr, head_scr):
        i = pl.program_id(0)

        @pl.when(i == 0)
        def _():
            dgate_ref[...] = jnp.zeros_like(dgate_ref)
            dcb_ref[...] = jnp.zeros_like(dcb_ref)
            dcw_ref[...] = jnp.zeros_like(dcw_ref)
            head_scr[...] = jnp.zeros_like(head_scr)
        dx = dx_ref[...]
        dgate_ref[...] += _colsum(dx * br_ref[...].astype(F32))
        dy_scr[...] = lax.dot_general((dx * gate_ref[...]).astype(BF16), wo_ref[...], NT_DIMS,
                                      preferred_element_type=F32)
        row = lax.broadcasted_iota(jnp.int32, (t, cwid), 0)
        for c0 in range(0, e, cwid):
            sl = slice(c0, c0 + cwid)
            bg = p_ref[:, c0:c0 + cwid].astype(F32)
            cg = p_ref[:, e + c0:e + c0 + cwid].astype(F32)
            xin = p_ref[:, 2 * e + c0:2 * e + c0 + cwid].astype(F32)
            z = p_ref[:, 3 * e + c0:3 * e + c0 + cwid].astype(F32)
            cx = cg * xin
            w0, w1, w2 = cw_ref[0:1, sl], cw_ref[1:2, sl], cw_ref[2:3, sl]
            conv = conv_ref[:, sl].astype(F32)
            sz, dsz = _silu(z)
            dy = dy_scr[:, sl]
            y_ref[:, sl] = (sz * bg * conv).astype(BF16)
            dp_ref[:, 3 * e + c0:3 * e + c0 + cwid] = (dy * bg * conv * dsz).astype(BF16)
            dp_ref[:, c0:c0 + cwid] = (dy * sz * conv).astype(BF16)
            dconv = dy * sz * bg
            h0, h1 = head_scr[0:1, sl], head_scr[1:2, sl]
            n1 = jnp.where(row == t - 1, h0, pltpu.roll(dconv, t - 1, 0))
            n2 = jnp.where(row == t - 2, h0, jnp.where(row == t - 1, h1, pltpu.roll(dconv, t - 2, 0)))
            dcb_ref[:, sl] += _colsum(dconv)
            dcw_ref[2:3, sl] += _colsum(dconv * cx)
            dcw_ref[1:2, sl] += _colsum(n1 * cx)
            dcw_ref[0:1, sl] += _colsum(n2 * cx)
            dcx = w2 * dconv + w1 * n1 + w0 * n2
            dp_ref[:, e + c0:e + c0 + cwid] = (dcx * xin).astype(BF16)
            dp_ref[:, 2 * e + c0:2 * e + c0 + cwid] = (dcx * cg).astype(BF16)
            head_scr[:, sl] = dconv[0:8, :]

    body = _carry(ex, body, 7, 5, lambda: pl.program_id(0) == 0, None, lambda: pl.program_id(0) == n_t - 1)
    ex_args, ex_in, ex_shapes, ex_out, ex_sems = _carried(ex)
    rev = lambda i: (n_t - 1 - i, 0)
    out = pl.pallas_call(
        body, name=name, grid=(n_t,),
        out_shape=(jax.ShapeDtypeStruct((s, e4), BF16), jax.ShapeDtypeStruct((s, e), BF16),
                   jax.ShapeDtypeStruct((1, d), F32), jax.ShapeDtypeStruct((1, e), F32), jax.ShapeDtypeStruct((8, e), F32),
                   *ex_shapes),
        in_specs=[pl.BlockSpec((t, e4), rev), pl.BlockSpec((t, d), rev), pl.BlockSpec((t, d), rev),
                  pl.BlockSpec((t, e), rev), _full((3, e)), _full((1, d)), _full((e, d)), *ex_in],
        out_specs=(pl.BlockSpec((t, e4), rev), pl.BlockSpec((t, e), rev), _full((1, d)), _full((1, e)), _full((8, e)),
                   *ex_out),
        scratch_shapes=[pltpu.VMEM((t, e), F32), pltpu.VMEM((8, e), F32), *ex_sems],
        compiler_params=_params(("arbitrary",)),
    )(proj, dx1, br, conv_a, cw, gate, wo, *ex_args)
    return (*out[:5], out[5:])


def _matmul_nt_norm_bwd(dproj, w, xin, dres, g, scale, name, ex=None, after=()):
    s, d = xin.shape
    n = w.shape[1]
    tm = min(s, 512)
    n_i = s // tm

    def body(dp_ref, w_ref, x_ref, dres_ref, g_ref, sc_ref, *rest):
        dx_ref, dsh_ref, dsc_ref, dg_ref, p_scr = rest[len(after):]
        i = pl.program_id(0)

        @pl.when(i == 0)
        def _():
            dsh_ref[...] = jnp.zeros_like(dsh_ref)
            p_scr[...] = jnp.zeros_like(p_scr)
        dh = lax.dot_general(dp_ref[...], w_ref[...], NT_DIMS, preferred_element_type=F32)
        _norm_modulate_bwd(dh, x_ref, dres_ref, g_ref, sc_ref, dx_ref, dsh_ref, p_scr)

        @pl.when(i == n_i - 1)
        def _():
            dsc_ref[...] = p_scr[...] * g_ref[...]
            dg_ref[...] = p_scr[...] * (1.0 + sc_ref[...])

    body = _carry(ex, body, 6 + len(after), 4, lambda: pl.program_id(0) == 0, None, lambda: pl.program_id(0) == n_i - 1)
    anywhere = [pl.BlockSpec(memory_space=pl.ANY)] * len(after)
    ex_args, ex_in, ex_shapes, ex_out, ex_sems = _carried(ex)
    tok = pl.BlockSpec((tm, d), lambda i: (i, 0))
    vec = pl.BlockSpec((1, d), lambda i: (0, 0))
    vshape = jax.ShapeDtypeStruct((1, d), F32)
    out = pl.pallas_call(
        body, name=name, grid=(n_i,),
        out_shape=(jax.ShapeDtypeStruct((s, d), F32), vshape, vshape, vshape, *ex_shapes),
        in_specs=[pl.BlockSpec((tm, n), lambda i: (i, 0)), _resident((d, n)), tok, tok, vec, vec, *anywhere, *ex_in],
        out_specs=(tok, vec, vec, vec, *ex_out),
        scratch_shapes=[pltpu.VMEM((1, d), F32), *ex_sems],
        compiler_params=_params(("arbitrary",)),
    )(dproj, w, xin, dres, g, scale, *after, *ex_args)
    return (*out[:4], out[4:])


def _matmul_tn(a, b, colscale, rows_split, name, ex=None, a_cols=None):
    s, m = a.shape
    a_blk = 0
    if a_cols is not None:
        a_blk, m = a_cols
    n = b.shape[1]
    n_j, tn = (1, n) if rows_split else (NDEV, n // NDEV)
    fixed = (4 + 4 + 2 * 2) * m * tn
    tk = s
    while fixed + 2 * tk * (2 * m + b.dtype.itemsize * tn) > TN_VMEM_BUDGET:
        tk //= 2
    n_k = s // tk

    def body(a_ref, b_ref, cs_ref, o_ref, acc):
        k = pl.program_id(1)
        part = lax.dot_general(a_ref[...], b_ref[...].astype(BF16), TN_DIMS, preferred_element_type=F32)
        if n_k == 1:
            o_ref[...] = (part * cs_ref[...]).astype(BF16)
            return

        @pl.when(k == 0)
        def _():
            acc[...] = part

        @pl.when((k > 0) & (k < n_k - 1))
        def _():
            acc[...] += part

        @pl.when(k == n_k - 1)
        def _():
            o_ref[...] = ((acc[...] + part) * cs_ref[...]).astype(BF16)

    at = lambda j, k: (pl.program_id(0) == j) & (pl.program_id(1) == k)
    body = _carry(ex, body, 3, 1, lambda: at(0, 0), None, lambda: at(n_j - 1, n_k - 1))
    ex_args, ex_in, ex_shapes, ex_out, ex_sems = _carried(ex)
    out = pl.pallas_call(
        body, name=name, grid=(n_j, n_k),
        out_shape=(jax.ShapeDtypeStruct((n_j, m, tn), BF16), *ex_shapes),
        in_specs=[pl.BlockSpec((tk, m), lambda j, k: (k, a_blk)), pl.BlockSpec((tk, tn), lambda j, k: (k, j)),
                  pl.BlockSpec((1, tn), lambda j, k: (0, j)), *ex_in],
        out_specs=(pl.BlockSpec((None, m, tn), lambda j, k: (j, 0, 0)), *ex_out),
        scratch_shapes=[pltpu.VMEM((m, tn), F32), *ex_sems],
        compiler_params=_params(("arbitrary", "arbitrary")),
    )(a, b, colscale, *ex_args)
    return (out[0].reshape(NDEV, m // NDEV, n) if rows_split else out[0]), out[1:]


def _adam_update(w, g, m, v):
    m = ADAM_B1 * m + (1.0 - ADAM_B1) * g
    v = ADAM_B2 * v + (1.0 - ADAM_B2) * (g * g)
    m_hat = m / (1.0 - ADAM_B1 ** ADAM_STEP)
    v_hat = v / (1.0 - ADAM_B2 ** ADAM_STEP)
    return -ADAM_LR * (m_hat / (jnp.sqrt(v_hat) + ADAM_EPS) + ADAM_WD * w), m, v


def _adamw_reduce(parts, w, m, v, name, ex=None, after=()):
    n_l = len(parts)
    n_after = len(after)
    n_p, _, c = parts[0].shape
    rows = [p.shape[1] for p in parts]
    r = sum(rows)
    tr = min(min(rows), 128 if ex is not None else 256)
    n_i = r // tr
    tiles = [r_l // tr for r_l in rows]
    first_tile = [sum(tiles[:l]) for l in range(n_l)]

    def body(*refs):
        p_refs, (w_ref, m_ref, v_ref) = refs[:n_l], refs[n_l:n_l + 3]
        g_out, d_out, m_out, v_out = refs[n_l + 3 + n_after:]
        g = None
        for l, p_ref in enumerate(p_refs):
            g_l = p_ref[0].astype(F32)
            for j in range(1, n_p):
                g_l = g_l + p_ref[j].astype(F32)
            g = g_l if g is None else jnp.where(pl.program_id(0) >= first_tile[l], g_l, g)
        g_out[...] = g
        d_out[...], m_out[...], v_out[...] = _adam_update(w_ref[...], g, m_ref[...], v_ref[...])

    step = lambda k: (lambda: pl.program_id(0) == k)
    body = _carry(ex, body, n_l + 3 + n_after, 4, step(0), [step(n_i // 3), step((2 * n_i) // 3)], step(n_i - 1))
    ex_args, ex_in, ex_shapes, ex_out, ex_sems = _carried(ex)
    blk = pl.BlockSpec((tr, c), lambda i: (i, 0))
    anywhere = [pl.BlockSpec(memory_space=pl.ANY)] * n_after
    p_specs = [pl.BlockSpec((n_p, tr, c), lambda i, l=l: (0, jnp.clip(i - first_tile[l], 0, tiles[l] - 1), 0))
               for l in range(n_l)]
    shp = jax.ShapeDtypeStruct((r, c), F32)
    out = pl.pallas_call(
        body, name=name, grid=(n_i,), out_shape=(shp, shp, shp, shp, *ex_shapes),
        in_specs=[*p_specs, blk, blk, blk, *anywhere, *ex_in],
        out_specs=(blk, blk, blk, blk, *ex_out), scratch_shapes=ex_sems,
        compiler_params=_params(("arbitrary",)),
    )(*parts, w, m, v, *after, *ex_args)
    return (*out[:4], out[4:])


def _adamw_small(gs, ws, ms, vs, name):
    n = len(gs)

    def body(*refs):
        ins, outs = refs[:4 * n], refs[4 * n:]
        for a in range(n):
            d, m, v = _adam_update(ins[n + a][...], ins[a][...], ins[2 * n + a][...], ins[3 * n + a][...])
            outs[a][...], outs[n + a][...], outs[2 * n + a][...] = d, m, v

    shapes = tuple(jax.ShapeDtypeStruct(w.shape, F32) for w in ws) * 3
    out = pl.pallas_call(
        body, name=name, out_shape=shapes,
        in_specs=[_vmem()] * (4 * n), out_specs=tuple([_vmem()] * (3 * n)),
        compiler_params=pltpu.CompilerParams(vmem_limit_bytes=VMEM_LIMIT),
    )(*gs, *ws, *ms, *vs)
    return out[:n], out[n:2 * n], out[2 * n:]


def _sum_devices(packed, name):
    n = len(packed)

    def body(*refs):
        for p_ref, o_ref in zip(refs[:n], refs[n:]):
            acc = p_ref[0].astype(F32)
            for j in range(1, NDEV):
                acc = acc + p_ref[j].astype(F32)
            o_ref[...] = acc

    return pl.pallas_call(
        body, name=name, out_shape=tuple(jax.ShapeDtypeStruct(p.shape[1:], F32) for p in packed),
        in_specs=[_vmem()] * n, out_specs=tuple([_vmem()] * n),
        compiler_params=pltpu.CompilerParams(vmem_limit_bytes=VMEM_LIMIT),
    )(*packed)


def _mod_w_grad(c_t, dmod, name):
    n_layers, _, w3 = dmod.shape
    d = c_t.shape[0]

    def body(c_ref, dm_ref, o_ref):
        for i in range(n_layers):
            acc = c_ref[:, 0:1] * dm_ref[i, 0:1, :]
            for b in range(1, NDEV):
                acc = acc + c_ref[:, b:b + 1] * dm_ref[i, b:b + 1, :]
            o_ref[i] = acc

    return pl.pallas_call(
        body, name=name, out_shape=jax.ShapeDtypeStruct((n_layers, d, w3), F32),
        in_specs=[_vmem(), _vmem()], out_specs=_vmem(),
        compiler_params=pltpu.CompilerParams(vmem_limit_bytes=VMEM_LIMIT),
    )(c_t, dmod)


def _mask_transpose_ws(w_s, name):
    def body(w_ref, wt_ref, wtt_ref):
        tril = (lax.broadcasted_iota(jnp.int32, (CHUNK, CHUNK), 0) >= lax.broadcasted_iota(jnp.int32, (CHUNK, CHUNK), 1))
        for g in range(GROUPS):
            wm = jnp.where(tril, w_ref[g], 0.0)
            wt_ref[g] = wm.astype(BF16)
            wtt_ref[g] = wm.T.astype(BF16)

    shp = jax.ShapeDtypeStruct(w_s.shape, BF16)
    return pl.pallas_call(
        body, name=name, out_shape=(shp, shp), in_specs=[_vmem()], out_specs=(_vmem(), _vmem()),
    )(w_s)


def _pack(pieces):
    flat = jnp.concatenate([p.reshape(-1) for p in pieces])
    rows = -(-flat.shape[0] // (8 * PACK_W)) * 8
    return jnp.pad(flat, (0, rows * PACK_W - flat.shape[0])).reshape(rows, PACK_W)


def _unpack(flat, shapes):
    out, off = [], 0
    for shp in shapes:
        size = 1
        for dim in shp:
            size *= dim
        out.append(flat[off:off + size].reshape(shp))
        off += size
    return out


def kernel(x, c, mod_w, mod_b, norm_g, a_w_in, a_conv_w, a_conv_b, a_w_out, b_w_in, b_ln_g, b_ln_b, b_w_s, b_b_s, b_w_out, final_g, loss_target, m_mod_w, m_mod_b, m_norm_g, m_a_w_in, m_a_conv_w, m_a_conv_b, m_a_w_out, m_b_w_in, m_b_ln_g, m_b_ln_b, m_b_w_s, m_b_b_s, m_b_w_out, m_final_g, v_mod_w, v_mod_b, v_norm_g, v_a_w_in, v_a_conv_w, v_a_conv_b, v_a_w_out, v_b_w_in, v_b_ln_g, v_b_ln_b, v_b_w_s, v_b_b_s, v_b_w_out, v_final_g):
    s, d = x.shape[1], x.shape[2]
    es = a_w_out.shape[1]
    e = NDEV * es
    w3 = mod_w.shape[2]
    me = _index(_pos())
    x0 = x.reshape(s, d)
    tgt = loss_target.reshape(s, d)

    small = jnp.concatenate([a_conv_w[0], b_ln_g, b_ln_b, jnp.zeros((3, es), F32)], axis=0)
    gather_a = _gather_exchange([a_w_in[0].astype(BF16), a_w_out[0].astype(BF16), small], [True, False, True])
    gather_b = _gather_exchange([b_w_in[0].astype(BF16), b_w_out[0].astype(BF16)], [True, False])
    mod, c_all, (wa, woa, small_all) = _mod_vectors(c, mod_w, mod_b, gather_a)
    conv_w, ln_g, ln_b = small_all[0:3], small_all[3:4], small_all[4:5]
    bsf = jnp.repeat(b_b_s[0].T, e // GROUPS, axis=1)
    wt, wtt = _mask_transpose_ws(b_w_s[0], "mask_w_s")
    shift0, scale0, gate0 = mod[0:1, 0:d], mod[0:1, d:2 * d], mod[0:1, 2 * d:]
    shift1, scale1, gate1 = mod[1:2, 0:d], mod[1:2, d:2 * d], mod[1:2, 2 * d:]
    g0, g1, fg = norm_g[0:1], norm_g[1:2], final_g.reshape(1, d)

    proj_a, h0, x1, br_a, conv_a, (wb, wob) = _layer_a_fwd(
        x0, g0, scale0, shift0, gate0, wa, conv_w, a_conv_b, woa, "a_fwd", gather_b)
    proj_b, h1, dx2, loss_acc, dfg, dgate1, v_b, dgv_b, _ = _layer_b_fwd_loss(
        x1, tgt, g1, scale1, shift1, gate1, fg, wb, ln_g, ln_b, wt, bsf, wob, "b_fwd_loss")

    dproj_b, y_b, dx1, dws, dbs, dlg, dlb, dshift1, dscale1, dg1 = _layer_b_bwd(
        proj_b, v_b, dgv_b, dx2, x1, gate1, g1, scale1, ln_g, ln_b, wt, wtt, bsf, wob, wb, "b_bwd")
    gs_b_out, _ = _matmul_tn(y_b, dx2, gate1, True, "b_w_out_grad")
    gs_b_in, (gr_b_out,) = _matmul_tn(h1, dproj_b, jnp.ones((1, dproj_b.shape[1]), F32), False, "b_w_in_grad",
                                      _scatter_exchange([gs_b_out]))
    dproj_a, y_a, dgate0, dcb, dcw, (gr_b_in,) = _conv_mixer_bwd(
        proj_a, dx1, br_a, conv_a, conv_w, gate0, woa, "a_mixer_bwd", _scatter_exchange([gs_b_in]))
    gs_a_out, _ = _matmul_tn(y_a, dx1, gate0, True, "a_w_out_grad")
    gs_a_in, (gr_a_out,) = _matmul_tn(h0, dproj_a, jnp.ones((1, dproj_a.shape[1]), F32), False, "a_w_in_grad",
                                      _scatter_exchange([gs_a_out]))
    send_sems, recv_sems, gs_thru, land_thru, token = _scatter_start(gs_a_in, "a_w_in_scatter_start")
    dx0, dshift0, dscale0, dg0, _ = _matmul_nt_norm_bwd(dproj_a, wa, x0, dx1, g0, scale0, "a_in_bwd", after=[token])
    gs_a_in, landed = _scatter_wait(send_sems, recv_sems, gs_thru, land_thru, dshift0, "a_w_in_scatter_wait")
    gr_a_in = lax.dynamic_update_slice_in_dim(landed, lax.dynamic_slice_in_dim(gs_a_in, me, 1, axis=0), me, axis=0)

    def big(parts, w, m, v, name, ex=None, after=()):
        shp = w.shape
        r2 = lambda t_: t_.reshape(-1, shp[-1])
        g, dl, nm, nv, ex_out = _adamw_reduce(parts, r2(w), r2(m), r2(v), name, ex, after)
        return tuple(t_.reshape(shp) for t_ in (g, dl, nm, nv)), ex_out

    pieces = [dshift0, dscale0, dgate0, dshift1, dscale1, dgate1, dg0, dg1, dcb, dcw[0:3], dlg, dlb, dfg,
              dbs[:, 0:GROUPS].T, loss_acc[0:1, 0:1]]
    shapes = [p.shape for p in pieces]
    g_send, g_recv, blocks_thru, lands_thru, g_token = _gather_start(
        [_pack(pieces), dws.astype(BF16).reshape(-1, PACK_W)], "small_gather_start")
    res = {}
    res["a_w_in"], _ = big([gr_a_in], a_w_in, m_a_w_in, v_a_w_in, "adamw_a_w_in", after=[g_token])
    res["b_w_in"], _ = big([gr_b_in], b_w_in, m_b_w_in, v_b_w_in, "adamw_b_w_in", after=[g_token])
    res["a_w_out"], _ = big([gr_a_out], a_w_out, m_a_w_out, v_a_w_out, "adamw_a_w_out", after=[g_token])
    res["b_w_out"], _ = big([gr_b_out], b_w_out, m_b_w_out, v_b_w_out, "adamw_b_w_out", after=[g_token])
    landed = _gather_wait(
        g_send, g_recv, blocks_thru, lands_thru, [res[k][1] for k in ("a_w_in", "b_w_in", "a_w_out", "b_w_out")],
        "small_gather_wait")
    packed_all, dws_all = [lax.dynamic_update_slice_in_dim(l, b[None], me, axis=0) for l, b in zip(landed, blocks_thru)]
    total, t_ws = _sum_devices([packed_all, dws_all], "sum_small_grads")
    total, t_ws = total.reshape(-1), t_ws.reshape(dws.shape)
    (t_sh0, t_sc0, t_ga0, t_sh1, t_sc1, t_ga1, t_g0, t_g1, t_cb, t_cw, t_lg, t_lb, t_fg, t_bs, t_loss) = _unpack(
        total, shapes)
    loss = t_loss.reshape(())
    grad_mod_b = jnp.concatenate([jnp.concatenate([t_sh0, t_sc0, t_ga0], axis=1),
                                  jnp.concatenate([t_sh1, t_sc1, t_ga1], axis=1)], axis=0)
    grad_norm_g = jnp.concatenate([t_g0, t_g1], axis=0)
    dmod_all = packed_all.reshape(NDEV, -1)[:, 0:6 * d].reshape(NDEV, 2, 3 * d).transpose(1, 0, 2)
    dmod_mine = lax.dynamic_slice_in_dim(dmod_all, me * w3, w3, axis=2)
    grad_mod_w = _mod_w_grad(c_all.T, dmod_mine, "mod_w_grad")
    grad_a_conv_w = lax.dynamic_slice_in_dim(t_cw, me * es, es, axis=1)
    grad_b_ln_g = lax.dynamic_slice_in_dim(t_lg, me * es, es, axis=1)
    grad_b_ln_b = lax.dynamic_slice_in_dim(t_lb, me * es, es, axis=1)

    res["mod_w"], _ = big([grad_mod_w.reshape(1, -1, w3)], mod_w, m_mod_w, v_mod_w, "adamw_mod_w")

    small_names = ["mod_b", "norm_g", "a_conv_w", "a_conv_b", "b_ln_g", "b_ln_b", "b_w_s", "b_b_s", "final_g"]
    small_g = [grad_mod_b, grad_norm_g, grad_a_conv_w, t_cb, grad_b_ln_g, grad_b_ln_b, t_ws, t_bs, t_fg]
    small_w = [mod_b, norm_g, a_conv_w, a_conv_b, b_ln_g, b_ln_b, b_w_s, b_b_s, final_g]
    small_m = [m_mod_b, m_norm_g, m_a_conv_w, m_a_conv_b, m_b_ln_g, m_b_ln_b, m_b_w_s, m_b_b_s, m_final_g]
    small_v = [v_mod_b, v_norm_g, v_a_conv_w, v_a_conv_b, v_b_ln_g, v_b_ln_b, v_b_w_s, v_b_b_s, v_final_g]
    as2d = lambda t_: t_.reshape(-1, t_.shape[-1])
    dls, nms, nvs = _adamw_small([as2d(t_) for t_ in small_g], [as2d(t_) for t_ in small_w],
                                 [as2d(t_) for t_ in small_m], [as2d(t_) for t_ in small_v], "adamw_small")
    for a, nme in enumerate(small_names):
        shp = small_w[a].shape
        res[nme] = (small_g[a].reshape(shp), dls[a].reshape(shp), nms[a].reshape(shp), nvs[a].reshape(shp))

    order = ["mod_w", "mod_b", "norm_g", "a_w_in", "a_conv_w", "a_conv_b", "a_w_out", "b_w_in", "b_ln_g", "b_ln_b",
             "b_w_s", "b_b_s", "b_w_out", "final_g"]
    return (loss, dx0.reshape(x.shape), *[res[k][0] for k in order], *[res[k][1] for k in order],
            *[res[k][2] for k in order], *[res[k][3] for k in order])
```
